```python
import jax, jax.numpy as jnp
from jax import lax
import numpy as np

D_MODEL = 2048
BATCH = 8
SEQ = 4096
DEPTH = 1

N_Q_HEADS = 16
N_KV_HEADS = 4
HEAD_DIM = 64
Q_GROUP = N_Q_HEADS // N_KV_HEADS
ATTN_WIDTH = N_Q_HEADS * HEAD_DIM
KV_WIDTH = N_KV_HEADS * HEAD_DIM
WINDOW = 128
BLOCK = 128
ROPE_THETA = 500000.0
ROT_DIM = HEAD_DIM // 4
POOL_WINDOWS = (2, 4, 8, 16)
N_POOL_GROUPS = len(POOL_WINDOWS)
POOL_WIDTH = D_MODEL // 2
POOL_GROUP = POOL_WIDTH // N_POOL_GROUPS
N_BRANCHES = 2
IN_SPLITS = (POOL_WIDTH, ATTN_WIDTH, KV_WIDTH, KV_WIDTH, D_MODEL, D_MODEL)
IN_WIDTH = sum(IN_SPLITS)
D_FF = 5504
N_SUBLAYERS = 3
LN_EPS = 1e-5
DN_ALPHA = (2 * DEPTH) ** 0.25
DN_BETA = (8 * DEPTH) ** -0.25

kernel_name = "hybrid_pool_swa_macaron_deepnorm_adaln"


def layer_norm(x, g, b):
    xf = x.astype(jnp.float32)
    mu = jnp.mean(xf, axis=-1, keepdims=True)
    var = jnp.mean(jnp.square(xf - mu), axis=-1, keepdims=True)
    y = (xf - mu) * lax.rsqrt(var + LN_EPS)
    return (y * g.astype(jnp.float32) + b.astype(jnp.float32)).astype(x.dtype)


def modulate(x, shift, scale):
    return x * (1.0 + scale[:, None, :]) + shift[:, None, :]


def swiglu(u, w_gu, w_down):
    a, b = jnp.split(u @ w_gu, 2, axis=-1)
    return (jax.nn.silu(a) * b) @ w_down


def rope_partial(t, cos, sin):
    half = ROT_DIM // 2
    t1 = t[..., :half]
    t2 = t[..., half:ROT_DIM]
    c = cos[None, :, None, :].astype(t.dtype)
    s = sin[None, :, None, :].astype(t.dtype)
    return jnp.concatenate([t1 * c - t2 * s, t2 * c + t1 * s, t[..., ROT_DIM:]], axis=-1)


def pool_mixer(xp, w_pool, pool_scale):
    B, S, _ = xp.shape
    groups = xp.reshape(B, S, N_POOL_GROUPS, POOL_GROUP)
    t1 = jnp.arange(S) + 1
    outs = []
    for gi, w in enumerate(POOL_WINDOWS):
        xg = groups[:, :, gi, :].astype(jnp.float32)
        cs = jnp.cumsum(xg, axis=1)
        lag = jnp.pad(cs, ((0, 0), (w, 0), (0, 0)))[:, :S]
        count = jnp.minimum(t1, w).astype(jnp.float32)[None, :, None]
        outs.append((cs - lag) / count - xg)
    pooled = jnp.stack(outs, axis=2).astype(xp.dtype)
    mixed = jnp.einsum('bsgc,gcd->bsgd', pooled, w_pool)
    return mixed.reshape(B, S, POOL_WIDTH) * pool_scale


def sliding_window_attention(q, k, v, sinks):
    B, S = q.shape[0], q.shape[1]
    nb = S // BLOCK
    qb = q.reshape(B, nb, BLOCK, N_KV_HEADS, Q_GROUP, HEAD_DIM)

    def with_prev(t):
        tb = t.reshape(B, nb, BLOCK, N_KV_HEADS, HEAD_DIM)
        prev = jnp.pad(tb[:, :-1], ((0, 0), (1, 0), (0, 0), (0, 0), (0, 0)))
        return jnp.concatenate([prev, tb], axis=2)

    kw = with_prev(k)
    vw = with_prev(v)
    s = jnp.einsum('bnqhgd,bnkhd->bnhgqk', qb, kw,
                   preferred_element_type=jnp.float32) * (HEAD_DIM ** -0.5)
    qi = jnp.arange(BLOCK)[:, None]
    kj = jnp.arange(2 * BLOCK)[None, :]
    diff = qi - kj + BLOCK
    kpos = jnp.arange(nb)[:, None, None] * BLOCK - BLOCK + kj[None]
    valid = (diff >= 0)[None] & (diff < WINDOW)[None] & (kpos >= 0)
    s = jnp.where(valid[None, :, None, None], s, -1e30)
    sink = sinks.astype(jnp.float32).reshape(1, 1, N_KV_HEADS, Q_GROUP, 1, 1)
    m = jnp.maximum(jnp.max(s, axis=-1, keepdims=True), sink)
    p = jnp.exp(s - m)
    probs = p / (jnp.sum(p, axis=-1, keepdims=True) + jnp.exp(sink - m))
    o = jnp.einsum('bnhgqk,bnkhd->bnqhgd', probs.astype(v.dtype), vw)
    return o.reshape(B, S, ATTN_WIDTH)


def _fwd_setup_inputs(seed: int = 0) -> dict:
    key = jax.random.key(seed)
    ks = jax.random.split(key, 24)
    f32 = jnp.float32
    L, D = DEPTH, D_MODEL

    def nrm(k, shape, std):
        return jax.random.normal(k, shape, f32) * std

    x = jax.random.normal(ks[0], (BATCH, SEQ, D), f32)
    c = jax.random.normal(ks[1], (BATCH, D), f32)
    w_ada = nrm(ks[2], (L, D, N_SUBLAYERS * 3 * D), 0.2 * D ** -0.5)
    b_ada = nrm(ks[3], (L, N_SUBLAYERS * 3 * D), 0.01)
    ln_g = 1.0 + nrm(ks[4], (L, N_SUBLAYERS, D), 0.05)
    ln_b = nrm(ks[5], (L, N_SUBLAYERS, D), 0.01)
    w_ffn1_in = nrm(ks[6], (L, D, 2 * D_FF), DN_BETA * D ** -0.5)
    w_ffn1_out = nrm(ks[7], (L, D_FF, D), DN_BETA * D_FF ** -0.5)
    w_in = jnp.concatenate([
        nrm(ks[8], (L, D, POOL_WIDTH), D ** -0.5),
        nrm(ks[9], (L, D, ATTN_WIDTH), D ** -0.5),
        nrm(ks[10], (L, D, KV_WIDTH), D ** -0.5),
        nrm(ks[11], (L, D, KV_WIDTH), DN_BETA * D ** -0.5),
        nrm(ks[12], (L, D, N_BRANCHES * D), D ** -0.5),
    ], axis=-1)
    b_in = nrm(ks[13], (L, IN_WIDTH), 0.01)
    w_pool = nrm(ks[14], (L, N_POOL_GROUPS, POOL_GROUP, POOL_GROUP), POOL_GROUP ** -0.5)
    pool_scale = 1.0 + nrm(ks[15], (L, POOL_WIDTH), 0.1)
    sinks = nrm(ks[16], (L, N_Q_HEADS), 0.5)
    w_branch_a = nrm(ks[17], (L, POOL_WIDTH, D), DN_BETA * POOL_WIDTH ** -0.5)
    w_branch_b = nrm(ks[18], (L, ATTN_WIDTH, D), DN_BETA * ATTN_WIDTH ** -0.5)
    w_out = nrm(ks[19], (L, D, D), DN_BETA * D ** -0.5)
    w_ffn2_in = nrm(ks[20], (L, D, 2 * D_FF), DN_BETA * D ** -0.5)
    w_ffn2_out = nrm(ks[21], (L, D_FF, D), DN_BETA * D_FF ** -0.5)
    return {"x": x, "c": c, "w_ada": w_ada, "b_ada": b_ada, "ln_g": ln_g, "ln_b": ln_b,
            "w_ffn1_in": w_ffn1_in, "w_ffn1_out": w_ffn1_out, "w_in": w_in, "b_in": b_in,
            "w_pool": w_pool, "pool_scale": pool_scale, "sinks": sinks,
            "w_branch_a": w_branch_a, "w_branch_b": w_branch_b, "w_out": w_out,
            "w_ffn2_in": w_ffn2_in, "w_ffn2_out": w_ffn2_out}


def _fwd_reference(x, c, w_ada, b_ada, ln_g, ln_b, w_ffn1_in, w_ffn1_out, w_in, b_in, w_pool,
              pool_scale, sinks, w_branch_a, w_branch_b, w_out, w_ffn2_in, w_ffn2_out):
    B, S, D = x.shape
    pos = jnp.arange(S, dtype=jnp.float32)
    inv_freq = ROPE_THETA ** (-jnp.arange(0, ROT_DIM, 2, dtype=jnp.float32) / ROT_DIM)
    ang = pos[:, None] * inv_freq[None, :]
    cos, sin = jnp.cos(ang), jnp.sin(ang)
    split_at = list(np.cumsum(IN_SPLITS)[:-1])
    c_act = jax.nn.silu(c)

    for l in range(DEPTH):
        mod = (c_act @ w_ada[l] + b_ada[l]).reshape(B, N_SUBLAYERS, 3, D)

        u = modulate(x, mod[:, 0, 0], mod[:, 0, 1])
        y = swiglu(u, w_ffn1_in[l], w_ffn1_out[l])
        x = layer_norm(DN_ALPHA * x + 0.5 * (1.0 + mod[:, 0, 2])[:, None, :] * y,
                       ln_g[l, 0], ln_b[l, 0])

        u = modulate(x, mod[:, 1, 0], mod[:, 1, 1])
        h = u @ w_in[l] + b_in[l]
        xp, q, k, v, gl_a, gl_b = jnp.split(h, split_at, axis=-1)
        q = rope_partial(q.reshape(B, S, N_Q_HEADS, HEAD_DIM), cos, sin)
        k = rope_partial(k.reshape(B, S, N_KV_HEADS, HEAD_DIM), cos, sin)
        v = v.reshape(B, S, N_KV_HEADS, HEAD_DIM)
        y_a = pool_mixer(xp, w_pool[l], pool_scale[l]) @ w_branch_a[l]
        y_b = sliding_window_attention(q, k, v, sinks[l]) @ w_branch_b[l]
        merged = jax.nn.sigmoid(gl_a) * y_a + jax.nn.sigmoid(gl_b) * y_b
        y = merged @ w_out[l]
        x = layer_norm(DN_ALPHA * x + (1.0 + mod[:, 1, 2])[:, None, :] * y,
                       ln_g[l, 1], ln_b[l, 1])

        u = modulate(x, mod[:, 2, 0], mod[:, 2, 1])
        y = swiglu(u, w_ffn2_in[l], w_ffn2_out[l])
        x = layer_norm(DN_ALPHA * x + 0.5 * (1.0 + mod[:, 2, 2])[:, None, :] * y,
                       ln_g[l, 2], ln_b[l, 2])
    return x


import jax as _jax
import jax.numpy as _jnp

TWIN_FORMAT = 'train_step'
FWD_PARAMS = ['x', 'c', 'w_ada', 'b_ada', 'ln_g', 'ln_b', 'w_ffn1_in', 'w_ffn1_out', 'w_in', 'b_in', 'w_pool', 'pool_scale', 'sinks', 'w_branch_a', 'w_branch_b', 'w_out', 'w_ffn2_in', 'w_ffn2_out']
TWIN_WEIGHTS = ['w_ada', 'b_ada', 'ln_g', 'ln_b', 'w_ffn1_in', 'w_ffn1_out', 'w_in', 'b_in', 'w_pool', 'pool_scale', 'sinks', 'w_branch_a', 'w_branch_b', 'w_out', 'w_ffn2_in', 'w_ffn2_out']
TWIN_DIFF_INPUT = 'x'
TWIN_INPUTS = ['x', 'c', 'w_ada', 'b_ada', 'ln_g', 'ln_b', 'w_ffn1_in', 'w_ffn1_out', 'w_in', 'b_in', 'w_pool', 'pool_scale', 'sinks', 'w_branch_a', 'w_branch_b', 'w_out', 'w_ffn2_in', 'w_ffn2_out', 'loss_target', 'm_w_ada', 'm_b_ada', 'm_ln_g', 'm_ln_b', 'm_w_ffn1_in', 'm_w_ffn1_out', 'm_w_in', 'm_b_in', 'm_w_pool', 'm_pool_scale', 'm_sinks', 'm_w_branch_a', 'm_w_branch_b', 'm_w_out', 'm_w_ffn2_in', 'm_w_ffn2_out', 'v_w_ada', 'v_b_ada', 'v_ln_g', 'v_ln_b', 'v_w_ffn1_in', 'v_w_ffn1_out', 'v_w_in', 'v_b_in', 'v_w_pool', 'v_pool_scale', 'v_sinks', 'v_w_branch_a', 'v_w_branch_b', 'v_w_out', 'v_w_ffn2_in', 'v_w_ffn2_out']
TWIN_OUTPUTS = ['loss', 'grad_x', 'grad_w_ada', 'grad_b_ada', 'grad_ln_g', 'grad_ln_b', 'grad_w_ffn1_in', 'grad_w_ffn1_out', 'grad_w_in', 'grad_b_in', 'grad_w_pool', 'grad_pool_scale', 'grad_sinks', 'grad_w_branch_a', 'grad_w_branch_b', 'grad_w_out', 'grad_w_ffn2_in', 'grad_w_ffn2_out', 'delta_w_ada', 'delta_b_ada', 'delta_ln_g', 'delta_ln_b', 'delta_w_ffn1_in', 'delta_w_ffn1_out', 'delta_w_in', 'delta_b_in', 'delta_w_pool', 'delta_pool_scale', 'delta_sinks', 'delta_w_branch_a', 'delta_w_branch_b', 'delta_w_out', 'delta_w_ffn2_in', 'delta_w_ffn2_out', 'new_m_w_ada', 'new_m_b_ada', 'new_m_ln_g', 'new_m_ln_b', 'new_m_w_ffn1_in', 'new_m_w_ffn1_out', 'new_m_w_in', 'new_m_b_in', 'new_m_w_pool', 'new_m_pool_scale', 'new_m_sinks', 'new_m_w_branch_a', 'new_m_w_branch_b', 'new_m_w_out', 'new_m_w_ffn2_in', 'new_m_w_ffn2_out', 'new_v_w_ada', 'new_v_b_ada', 'new_v_ln_g', 'new_v_ln_b', 'new_v_w_ffn1_in', 'new_v_w_ffn1_out', 'new_v_w_in', 'new_v_b_in', 'new_v_w_pool', 'new_v_pool_scale', 'new_v_sinks', 'new_v_w_branch_a', 'new_v_w_branch_b', 'new_v_w_out', 'new_v_w_ffn2_in', 'new_v_w_ffn2_out']
TWIN_LEAF_KINDS = {'loss': 'loss', 'grad_x': 'grad_x', 'grad_w_ada': 'grad_w', 'grad_b_ada': 'grad_w', 'grad_ln_g': 'grad_w', 'grad_ln_b': 'grad_w', 'grad_w_ffn1_in': 'grad_w', 'grad_w_ffn1_out': 'grad_w', 'grad_w_in': 'grad_w', 'grad_b_in': 'grad_w', 'grad_w_pool': 'grad_w', 'grad_pool_scale': 'grad_w', 'grad_sinks': 'grad_w', 'grad_w_branch_a': 'grad_w', 'grad_w_branch_b': 'grad_w', 'grad_w_out': 'grad_w', 'grad_w_ffn2_in': 'grad_w', 'grad_w_ffn2_out': 'grad_w', 'delta_w_ada': 'delta_w', 'delta_b_ada': 'delta_w', 'delta_ln_g': 'delta_w', 'delta_ln_b': 'delta_w', 'delta_w_ffn1_in': 'delta_w', 'delta_w_ffn1_out': 'delta_w', 'delta_w_in': 'delta_w', 'delta_b_in': 'delta_w', 'delta_w_pool': 'delta_w', 'delta_pool_scale': 'delta_w', 'delta_sinks': 'delta_w', 'delta_w_branch_a': 'delta_w', 'delta_w_branch_b': 'delta_w', 'delta_w_out': 'delta_w', 'delta_w_ffn2_in': 'delta_w', 'delta_w_ffn2_out': 'delta_w', 'new_m_w_ada': 'new_m', 'new_m_b_ada': 'new_m', 'new_m_ln_g': 'new_m', 'new_m_ln_b': 'new_m', 'new_m_w_ffn1_in': 'new_m', 'new_m_w_ffn1_out': 'new_m', 'new_m_w_in': 'new_m', 'new_m_b_in': 'new_m', 'new_m_w_pool': 'new_m', 'new_m_pool_scale': 'new_m', 'new_m_sinks': 'new_m', 'new_m_w_branch_a': 'new_m', 'new_m_w_branch_b': 'new_m', 'new_m_w_out': 'new_m', 'new_m_w_ffn2_in': 'new_m', 'new_m_w_ffn2_out': 'new_m', 'new_v_w_ada': 'new_v', 'new_v_b_ada': 'new_v', 'new_v_ln_g': 'new_v', 'new_v_ln_b': 'new_v', 'new_v_w_ffn1_in': 'new_v', 'new_v_w_ffn1_out': 'new_v', 'new_v_w_in': 'new_v', 'new_v_b_in': 'new_v', 'new_v_w_pool': 'new_v', 'new_v_pool_scale': 'new_v', 'new_v_sinks': 'new_v', 'new_v_w_branch_a': 'new_v', 'new_v_w_branch_b': 'new_v', 'new_v_w_out': 'new_v', 'new_v_w_ffn2_in': 'new_v', 'new_v_w_ffn2_out': 'new_v'}


def _forward(args):
    return _fwd_reference(*[args[k] for k in FWD_PARAMS])


def _output_shape():
    def fwd():
        inp = _fwd_setup_inputs(0)
        return _fwd_reference(*[inp[k] for k in FWD_PARAMS])
    out = _jax.eval_shape(fwd)
    return out.shape, out.dtype

N_MICROBATCH = 1
ADAM_LR = 0.001
ADAM_B1 = 0.9
ADAM_B2 = 0.999
ADAM_EPS = 1e-08
ADAM_WD = 0.01
ADAM_STEP = 10
PER_EXAMPLE_BATCH_AXIS = {'x': 0, 'c': 0, 'loss_target': 0}
SHARED_INPUTS = []
_WEIGHT_DTYPES = {'w_ada': _jnp.float32, 'b_ada': _jnp.float32, 'ln_g': _jnp.float32, 'ln_b': _jnp.float32, 'w_ffn1_in': _jnp.float32, 'w_ffn1_out': _jnp.float32, 'w_in': _jnp.float32, 'b_in': _jnp.float32, 'w_pool': _jnp.float32, 'pool_scale': _jnp.float32, 'sinks': _jnp.float32, 'w_branch_a': _jnp.float32, 'w_branch_b': _jnp.float32, 'w_out': _jnp.float32, 'w_ffn2_in': _jnp.float32, 'w_ffn2_out': _jnp.float32}
MOMENT_SCALE = {'w_ada': 5.569330e-03, 'b_ada': 1.232948e-02, 'ln_g': 9.415568e+00, 'ln_b': 1.892833e-01, 'w_ffn1_in': 4.688149e-03, 'w_ffn1_out': 7.597932e-03, 'w_in': 7.926494e-03, 'b_in': 1.344064e-02, 'w_pool': 1.838956e-02, 'pool_scale': 1.849670e-02, 'sinks': 1.851651e-03, 'w_branch_a': 2.185264e-02, 'w_branch_b': 3.514685e-03, 'w_out': 2.203824e-02, 'w_ffn2_in': 4.643074e-03, 'w_ffn2_out': 7.534248e-03}


def _to_microbatches(a, axis):
    t = _jnp.moveaxis(a, axis, 0)
    t = t.reshape((N_MICROBATCH, t.shape[0] // N_MICROBATCH) + t.shape[1:])
    return _jnp.moveaxis(t, 1, axis + 1)


def setup_inputs(seed: int = 0) -> dict:
    inp = _fwd_setup_inputs(seed)
    key = _jax.random.fold_in(_jax.random.key(seed), 7919)
    shape, _ = _output_shape()
    out = dict(inp)
    out["loss_target"] = _jax.random.normal(_jax.random.fold_in(key, 0), shape, _jnp.float32)
    for i, name in enumerate(TWIN_WEIGHTS):
        w = inp[name].astype(_jnp.float32)
        if MOMENT_SCALE is None:
            s = _jnp.sqrt(_jnp.mean(_jnp.square(w)) + 1e-30)
        else:
            s = MOMENT_SCALE[name]
        km, kv = _jax.random.split(_jax.random.fold_in(key, i + 1))
        out[name] = w
        out["m_" + name] = s * _jax.random.normal(km, w.shape, _jnp.float32)
        out["v_" + name] = (s * s) * _jax.random.uniform(kv, w.shape, _jnp.float32, 0.5, 1.5)
    if N_MICROBATCH > 1:
        for name, axis in PER_EXAMPLE_BATCH_AXIS.items():
            out[name] = _to_microbatches(out[name], axis)
    return {'x': out['x'], 'c': out['c'], 'w_ada': out['w_ada'], 'b_ada': out['b_ada'], 'ln_g': out['ln_g'], 'ln_b': out['ln_b'], 'w_ffn1_in': out['w_ffn1_in'], 'w_ffn1_out': out['w_ffn1_out'], 'w_in': out['w_in'], 'b_in': out['b_in'], 'w_pool': out['w_pool'], 'pool_scale': out['pool_scale'], 'sinks': out['sinks'], 'w_branch_a': out['w_branch_a'], 'w_branch_b': out['w_branch_b'], 'w_out': out['w_out'], 'w_ffn2_in': out['w_ffn2_in'], 'w_ffn2_out': out['w_ffn2_out'], 'loss_target': out['loss_target'], 'm_w_ada': out['m_w_ada'], 'm_b_ada': out['m_b_ada'], 'm_ln_g': out['m_ln_g'], 'm_ln_b': out['m_ln_b'], 'm_w_ffn1_in': out['m_w_ffn1_in'], 'm_w_ffn1_out': out['m_w_ffn1_out'], 'm_w_in': out['m_w_in'], 'm_b_in': out['m_b_in'], 'm_w_pool': out['m_w_pool'], 'm_pool_scale': out['m_pool_scale'], 'm_sinks': out['m_sinks'], 'm_w_branch_a': out['m_w_branch_a'], 'm_w_branch_b': out['m_w_branch_b'], 'm_w_out': out['m_w_out'], 'm_w_ffn2_in': out['m_w_ffn2_in'], 'm_w_ffn2_out': out['m_w_ffn2_out'], 'v_w_ada': out['v_w_ada'], 'v_b_ada': out['v_b_ada'], 'v_ln_g': out['v_ln_g'], 'v_ln_b': out['v_ln_b'], 'v_w_ffn1_in': out['v_w_ffn1_in'], 'v_w_ffn1_out': out['v_w_ffn1_out'], 'v_w_in': out['v_w_in'], 'v_b_in': out['v_b_in'], 'v_w_pool': out['v_w_pool'], 'v_pool_scale': out['v_pool_scale'], 'v_sinks': out['v_sinks'], 'v_w_branch_a': out['v_w_branch_a'], 'v_w_branch_b': out['v_w_branch_b'], 'v_w_out': out['v_w_out'], 'v_w_ffn2_in': out['v_w_ffn2_in'], 'v_w_ffn2_out': out['v_w_ffn2_out']}


def _loss(weights, diff, rest, loss_target):
    with _jax.named_scope("forward"):
        args = {**rest, TWIN_DIFF_INPUT: diff, **{k: w.astype(_WEIGHT_DTYPES[k]) for k, w in weights.items()}}
        y = _forward(args)
    with _jax.named_scope("loss_head"):
        err = _jnp.square(y.astype(_jnp.float32) - loss_target)
        return 0.5 * _jnp.sum(_jnp.mean(err, axis=-1)) if err.ndim else 0.5 * err


def _adamw(w, g, m, v):
    m = ADAM_B1 * m + (1.0 - ADAM_B1) * g
    v = ADAM_B2 * v + (1.0 - ADAM_B2) * _jnp.square(g)
    m_hat = m / (1.0 - ADAM_B1 ** ADAM_STEP)
    v_hat = v / (1.0 - ADAM_B2 ** ADAM_STEP)
    delta = -ADAM_LR * (m_hat / (_jnp.sqrt(v_hat) + ADAM_EPS) + ADAM_WD * w)
    return delta, m, v


def reference(x, c, w_ada, b_ada, ln_g, ln_b, w_ffn1_in, w_ffn1_out, w_in, b_in, w_pool, pool_scale, sinks, w_branch_a, w_branch_b, w_out, w_ffn2_in, w_ffn2_out, loss_target, m_w_ada, m_b_ada, m_ln_g, m_ln_b, m_w_ffn1_in, m_w_ffn1_out, m_w_in, m_b_in, m_w_pool, m_pool_scale, m_sinks, m_w_branch_a, m_w_branch_b, m_w_out, m_w_ffn2_in, m_w_ffn2_out, v_w_ada, v_b_ada, v_ln_g, v_ln_b, v_w_ffn1_in, v_w_ffn1_out, v_w_in, v_b_in, v_w_pool, v_pool_scale, v_sinks, v_w_branch_a, v_w_branch_b, v_w_out, v_w_ffn2_in, v_w_ffn2_out):
    given = dict(x=x, c=c, w_ada=w_ada, b_ada=b_ada, ln_g=ln_g, ln_b=ln_b, w_ffn1_in=w_ffn1_in, w_ffn1_out=w_ffn1_out, w_in=w_in, b_in=b_in, w_pool=w_pool, pool_scale=pool_scale, sinks=sinks, w_branch_a=w_branch_a, w_branch_b=w_branch_b, w_out=w_out, w_ffn2_in=w_ffn2_in, w_ffn2_out=w_ffn2_out, loss_target=loss_target, m_w_ada=m_w_ada, m_b_ada=m_b_ada, m_ln_g=m_ln_g, m_ln_b=m_ln_b, m_w_ffn1_in=m_w_ffn1_in, m_w_ffn1_out=m_w_ffn1_out, m_w_in=m_w_in, m_b_in=m_b_in, m_w_pool=m_w_pool, m_pool_scale=m_pool_scale, m_sinks=m_sinks, m_w_branch_a=m_w_branch_a, m_w_branch_b=m_w_branch_b, m_w_out=m_w_out, m_w_ffn2_in=m_w_ffn2_in, m_w_ffn2_out=m_w_ffn2_out, v_w_ada=v_w_ada, v_b_ada=v_b_ada, v_ln_g=v_ln_g, v_ln_b=v_ln_b, v_w_ffn1_in=v_w_ffn1_in, v_w_ffn1_out=v_w_ffn1_out, v_w_in=v_w_in, v_b_in=v_b_in, v_w_pool=v_w_pool, v_pool_scale=v_pool_scale, v_sinks=v_sinks, v_w_branch_a=v_w_branch_a, v_w_branch_b=v_w_branch_b, v_w_out=v_w_out, v_w_ffn2_in=v_w_ffn2_in, v_w_ffn2_out=v_w_ffn2_out)
    weights = {n: given[n] for n in TWIN_WEIGHTS}
    shared = {n: given[n] for n in SHARED_INPUTS}
    per_example = {n: given[n] for n in ['x', 'c']}
    grad_fn = _jax.value_and_grad(_loss, argnums=(0, 1))

    def one_microbatch(ex, loss_target):
        ex = dict(ex)
        diff = ex.pop(TWIN_DIFF_INPUT)
        return grad_fn(weights, diff, {**shared, **ex}, loss_target)

    if N_MICROBATCH == 1:
        loss, (grad_w, grad_x) = one_microbatch(per_example, given["loss_target"])
    else:
        def body(carry, xs):
            loss_sum, grad_sum = carry
            l_k, (gw_k, gx_k) = one_microbatch(xs[0], xs[1])
            with _jax.named_scope("update"):
                return (loss_sum + l_k, _jax.tree.map(_jnp.add, grad_sum, gw_k)), gx_k

        init = (_jnp.zeros((), _jnp.float32), _jax.tree.map(_jnp.zeros_like, weights))
        (loss, grad_w), grad_x = _jax.lax.scan(body, init, (per_example, given["loss_target"]))
    with _jax.named_scope("update"):
        delta_w, new_m, new_v = {}, {}, {}
        for n in TWIN_WEIGHTS:
            delta_w[n], new_m[n], new_v[n] = _adamw(weights[n], grad_w[n], given["m_" + n], given["v_" + n])
    return (loss, grad_x, *[grad_w[n] for n in TWIN_WEIGHTS], *[delta_w[n] for n in TWIN_WEIGHTS],
            *[new_m[n] for n in TWIN_WEIGHTS], *[new_v[n] for n in TWIN_WEIGHTS])
```

```python
import jax
import jax.numpy as jnp
from jax import lax
from jax.experimental import pallas as pl
from jax.experimental.pallas import tpu as pltpu

F32 = jnp.float32
BF16 = jnp.bfloat16
MESH = pl.DeviceIdType.MESH
ANY = pl.BlockSpec(memory_space=pl.ANY)

D = 2048
N_Q, N_KV, HD = 16, 4, 64
QW, KVW = N_Q * HD, N_KV * HD
BLK = 128
POOL_WINDOWS = (2, 4, 8, 16)
PW, PG = 1024, 256
HALO = 16
ROPE_THETA = 500000.0
ROT = HD // 4
LN_EPS = 1e-5
ALPHA = 2.0 ** 0.25
FH = 2752
FHP = 2816
FO = 1376
IN_W = 6656
IN_SH = IN_W // 4
ADA_SH = 18432 // 4
B1, B2, LR, EPS, WD, STEP = 0.9, 0.999, 0.001, 1e-08, 0.01, 10
VMEM_LIMIT = 56 * 1024 * 1024
FLIPS = ((1, 0), (0, 1), (1, 1))
NN = (((1,), (0,)), ((), ()))
NT = (((1,), (1,)), ((), ()))
TN = (((0,), (0,)), ((), ()))


def _params(sem):
    return pltpu.CompilerParams(dimension_semantics=sem, vmem_limit_bytes=VMEM_LIMIT)


def _aligned(v, m):
    return v if isinstance(v, int) else pl.multiple_of(v, m)


def _sigmoid(v):
    return 1.0 / (1.0 + jnp.exp(-v))


def T_(arr, width=None, off=0):
    return ("t", arr, width, off)


def B_(arr, width=None, off=0):
    return ("b", arr, width, off)


def rowmap(name, fn, ins, outs, accs=(), *, rows, tm, ncol=1, with_ids=False):
    tm = min(tm, rows)
    nrow = rows // tm
    in_specs, arrs = [], []
    for kind, arr, width, off in ins:
        w = arr.shape[1] if width is None else width
        r = tm if kind == "t" else arr.shape[0]
        if kind == "t":
            in_specs.append(pl.BlockSpec((r, w), lambda j, i, off=off: (i, off + j)))
        else:
            in_specs.append(pl.BlockSpec((r, w), lambda j, i, off=off: (0, off + j)))
        arrs.append(arr)
    out_shape, out_specs = [], []
    for width, dt in outs:
        out_shape.append(jax.ShapeDtypeStruct((rows, width), dt))
        out_specs.append(pl.BlockSpec((tm, width // ncol), lambda j, i: (i, j)))
    for r, width in accs:
        out_shape.append(jax.ShapeDtypeStruct((r, width), F32))
        out_specs.append(pl.BlockSpec((r, width // ncol), lambda j, i: (0, j)))
    ni, no = len(ins), len(outs)

    def body(*refs):
        i = pl.program_id(1)
        vals = [r[...] for r in refs[:ni]]
        res = fn(pl.program_id(0), i, *vals) if with_ids else fn(*vals)
        if not isinstance(res, (tuple, list)):
            res = (res,)
        for r, v in zip(refs[ni:ni + no], res[:no]):
            r[...] = v.astype(r.dtype)
        for r, v in zip(refs[ni + no:], res[no:]):
            @pl.when(i == 0)
            def _(r=r, v=v):
                r[...] = v

            @pl.when(i > 0)
            def _(r=r, v=v):
                r[...] += v

    res = pl.pallas_call(
        body, name=name, grid=(ncol, nrow), in_specs=in_specs, out_specs=out_specs, out_shape=out_shape,
        compiler_params=_params(("arbitrary", "arbitrary")),
    )(*arrs)
    return res[0] if len(res) == 1 else res


def colsum(v):
    return jnp.sum(v, axis=0, keepdims=True)


def mm(name, a_ops, b_ops, ops, *, dims, grid, a_specs, b_specs, outs, out_specs, acc_shapes,
       epilogue=None, extras=(), extra_specs=(), carry=None):
    gk = grid[2]
    na, nb, ne, nacc = len(a_ops), len(b_ops), len(extras), len(acc_shapes)
    nc = 0 if carry is None else 1
    no = len(outs)

    def body(*refs):
        a_refs = refs[:na]
        b_refs = refs[na:na + nb]
        e_refs = refs[na + nb:na + nb + ne]
        o_refs = refs[na + nb + ne + nc:na + nb + ne + nc + no]
        acc_refs = refs[na + nb + ne + nc + no:]
        k = pl.program_id(2)

        def partials():
            res = [None] * nacc
            for ai, bi, ci in ops:
                p = lax.dot_general(a_refs[ai][...], b_refs[bi][...], dims, preferred_element_type=F32)
                res[ci] = p if res[ci] is None else res[ci] + p
            return res

        def finish(accs):
            outv = epilogue(accs, [e[...] for e in e_refs]) if epilogue else (accs[0],)
            for o, v in zip(o_refs, outv):
                o[...] = v.astype(o.dtype)

        if gk == 1:
            finish(partials())
        else:
            ps = partials()

            @pl.when(k == 0)
            def _():
                for acc, p in zip(acc_refs, ps):
                    acc[...] = p

            @pl.when(k > 0)
            def _():
                for acc, p in zip(acc_refs, ps):
                    acc[...] += p

            @pl.when(k == gk - 1)
            def _():
                finish([acc[...] for acc in acc_refs])

    res = pl.pallas_call(
        body, name=name, grid=grid,
        in_specs=list(a_specs) + list(b_specs) + list(extra_specs) + ([ANY] if nc else []),
        out_specs=list(out_specs), out_shape=list(outs),
        scratch_shapes=[pltpu.VMEM(s, F32) for s in acc_shapes] if gk > 1 else [],
        input_output_aliases={na + nb + ne: 0} if nc else {},
        compiler_params=_params(("arbitrary", "arbitrary", "arbitrary")),
    )(*a_ops, *b_ops, *extras, *([carry] if nc else []))
    return res[0] if len(res) == 1 else res


def sds(shape, dt):
    return jax.ShapeDtypeStruct(shape, dt)


def _place():
    x, y, c = lax.axis_index("x"), lax.axis_index("y"), lax.axis_index("c")
    chips = [((1 - x) if fx else x, (1 - y) if fy else y) for fx, fy in FLIPS]
    return x, y, c, chips


def allgather_small(name, v):
    r = v.shape[0]

    def body(x_ref, out_ref, send_sems, recv_sems, local_sem):
        x, y, c, chips = _place()
        me, sibling = (x, y, c), (x, y, 1 - c)

        def rows(px, py, pc):
            return out_ref.at[4 * px + 2 * py + pc]

        def copy(k, block, to, src=None):
            return pltpu.make_async_remote_copy(
                src_ref=rows(*block) if src is None else src, dst_ref=rows(*block),
                send_sem=send_sems.at[k], recv_sem=recv_sems.at[k], device_id=to, device_id_type=MESH)

        mine = pltpu.make_async_copy(x_ref, rows(*me), local_sem)
        mine.start()
        first = [copy(0, me, sibling, src=x_ref)]
        first += [copy(1 + j, me, (*chip, c), src=x_ref) for j, chip in enumerate(chips)]
        for cp in first:
            cp.start()
        passed = [copy(4 + j, (*chip, c), sibling) for j, chip in enumerate(chips)]
        for j, chip in enumerate(chips):
            copy(1 + j, (*chip, c), me).wait_recv()
            passed[j].start()
        copy(0, sibling, me).wait_recv()
        for j, chip in enumerate(chips):
            copy(4 + j, (*chip, 1 - c), me).wait_recv()
        for cp in first + passed:
            cp.wait_send()
        mine.wait()

    return pl.pallas_call(
        body, name=name, out_shape=sds((8, r, 128), v.dtype),
        in_specs=[pl.BlockSpec(memory_space=pltpu.VMEM)], out_specs=pl.BlockSpec(memory_space=pltpu.VMEM),
        scratch_shapes=[pltpu.SemaphoreType.DMA((7,)), pltpu.SemaphoreType.DMA((7,)), pltpu.SemaphoreType.DMA],
    )(v)


def _half(ref, rows, hf):
    hr = rows // 2
    return ref.at[pl.ds(_aligned(hf * hr, 16), hr)]


def view_lead(ref, p):
    return ref.at[p]


def view_ffn_out(ref, p):
    return ref.at[p // 2, pl.ds(_aligned((p % 2) * FO, 16), FO)]


def gather_weights(items):
    nw = len(items)
    pads = [w for w, it in enumerate(items) if it[2] is view_ffn_out]

    def body(*refs):
        ins, outs = refs[:nw], refs[nw:2 * nw]
        zero_ref, send_sems, recv_sems, loc_sems, pad_sems = refs[2 * nw:]
        x, y, c, chips = _place()
        me = 2 * x + y
        sibling = (x, y, 1 - c)
        zero_ref[...] = jnp.zeros_like(zero_ref)
        padcp = []
        for n, w in enumerate(pads):
            for h in range(2):
                cp = pltpu.make_async_copy(zero_ref, outs[w].at[h, pl.ds(2 * FO, FHP - 2 * FO)], pad_sems.at[2 * n + h])
                cp.start()
                padcp.append(cp)
        locs, sends, fwds = [], [], []
        for w, (shard, _, view) in enumerate(items):
            cp = pltpu.make_async_copy(ins[w], view(outs[w], me), loc_sems.at[w])
            cp.start()
            locs.append(cp)
        for w, (shard, _, view) in enumerate(items):
            rws = shard.shape[0]
            for f, (px, py) in enumerate(chips):
                cp = pltpu.make_async_remote_copy(
                    src_ref=_half(ins[w], rws, c), dst_ref=_half(view(outs[w], me), rws, c),
                    send_sem=send_sems.at[w, f], recv_sem=recv_sems.at[w, f], device_id=(px, py, c), device_id_type=MESH)
                cp.start()
                sends.append(cp)
        for w, (shard, _, view) in enumerate(items):
            rws = shard.shape[0]
            for f, (px, py) in enumerate(chips):
                land = _half(view(outs[w], 2 * px + py), rws, c)
                pltpu.make_async_remote_copy(
                    src_ref=land, dst_ref=land, send_sem=send_sems.at[w, f], recv_sem=recv_sems.at[w, f],
                    device_id=(px, py, c), device_id_type=MESH).wait_recv()
                fw = pltpu.make_async_remote_copy(
                    src_ref=land, dst_ref=land, send_sem=send_sems.at[w, 3 + f], recv_sem=recv_sems.at[w, 3 + f],
                    device_id=sibling, device_id_type=MESH)
                fw.start()
                fwds.append(fw)
        for w, (shard, _, view) in enumerate(items):
            rws = shard.shape[0]
            for f, (px, py) in enumerate(chips):
                land = _half(view(outs[w], 2 * px + py), rws, 1 - c)
                pltpu.make_async_remote_copy(
                    src_ref=land, dst_ref=land, send_sem=send_sems.at[w, 3 + f], recv_sem=recv_sems.at[w, 3 + f],
                    device_id=sibling, device_id_type=MESH).wait_recv()
        for cp in sends + fwds:
            cp.wait_send()
        for cp in locs + padcp:
            cp.wait()

    return pl.pallas_call(
        body, name="gather_weights", out_shape=[sds(it[1], BF16) for it in items],
        in_specs=[ANY] * nw, out_specs=[ANY] * nw,
        scratch_shapes=[pltpu.VMEM((FHP - 2 * FO, D), BF16), pltpu.SemaphoreType.DMA((nw, 6)),
                        pltpu.SemaphoreType.DMA((nw, 6)), pltpu.SemaphoreType.DMA((nw,)),
                        pltpu.SemaphoreType.DMA((max(2 * len(pads), 1),))],
    )(*[it[0] for it in items])


def reduce_sibling(items):
    nw = len(items)

    def body(*refs):
        ins, own, got = refs[:nw], refs[nw:2 * nw], refs[2 * nw:3 * nw]
        send_sems, recv_sems, loc_sems = refs[3 * nw:]
        x, y, c, _ = _place()
        sibling = (x, y, 1 - c)
        cps, locs = [], []
        for w, (_, view, rws, _) in enumerate(items):
            for p in range(4):
                cp = pltpu.make_async_remote_copy(
                    src_ref=_half(view(ins[w], p), rws, 1 - c), dst_ref=got[w].at[p],
                    send_sem=send_sems.at[w, p], recv_sem=recv_sems.at[w, p], device_id=sibling, device_id_type=MESH)
                cp.start()
                cps.append(cp)
                lc = pltpu.make_async_copy(_half(view(ins[w], p), rws, c), own[w].at[p], loc_sems.at[w, p])
                lc.start()
                locs.append(lc)
        for cp in cps:
            cp.wait()
        for lc in locs:
            lc.wait()

    shapes = [sds((4, it[2] // 2, it[3]), BF16) for it in items]
    res = pl.pallas_call(
        body, name="reduce_sibling", out_shape=shapes + shapes, in_specs=[ANY] * nw, out_specs=[ANY] * (2 * nw),
        scratch_shapes=[pltpu.SemaphoreType.DMA((nw, 4)), pltpu.SemaphoreType.DMA((nw, 4)), pltpu.SemaphoreType.DMA((nw, 4))],
    )(*[it[0] for it in items])
    return res[:nw], res[nw:]


def reduce_chips(qs):
    nw = len(qs)

    def body(*refs):
        ins, own, got = refs[:nw], refs[nw:2 * nw], refs[2 * nw:3 * nw]
        send_sems, recv_sems, loc_sems = refs[3 * nw:]
        x, y, c, chips = _place()
        me = 2 * x + y
        cps, locs = [], []
        for w in range(nw):
            for f, (px, py) in enumerate(chips):
                cp = pltpu.make_async_remote_copy(
                    src_ref=ins[w].at[2 * px + py], dst_ref=got[w].at[f],
                    send_sem=send_sems.at[w, f], recv_sem=recv_sems.at[w, f], device_id=(px, py, c), device_id_type=MESH)
                cp.start()
                cps.append(cp)
            lc = pltpu.make_async_copy(ins[w].at[me], own[w], loc_sems.at[w])
            lc.start()
            locs.append(lc)
        for cp in cps:
            cp.wait()
        for lc in locs:
            lc.wait()

    res = pl.pallas_call(
        body, name="reduce_chips",
        out_shape=[sds(q.shape[1:], BF16) for q in qs] + [sds((3,) + q.shape[1:], BF16) for q in qs],
        in_specs=[ANY] * nw, out_specs=[ANY] * (2 * nw),
        scratch_shapes=[pltpu.SemaphoreType.DMA((nw, 3)), pltpu.SemaphoreType.DMA((nw, 3)), pltpu.SemaphoreType.DMA((nw,))],
    )(*qs)
    return res[:nw], res[nw:]


def share_halves(fs):
    nw = len(fs)

    def body(*refs):
        ins, outs = refs[:nw], refs[nw:2 * nw]
        send_sems, recv_sems, loc_sems = refs[2 * nw:]
        x, y, c, _ = _place()
        sibling = (x, y, 1 - c)
        cps, locs = [], []
        for w in range(nw):
            rws = 2 * fs[w].shape[0]
            cp = pltpu.make_async_remote_copy(
                src_ref=ins[w], dst_ref=_half(outs[w], rws, c), send_sem=send_sems.at[w], recv_sem=recv_sems.at[w],
                device_id=sibling, device_id_type=MESH)
            cp.start()
            cps.append(cp)
            lc = pltpu.make_async_copy(ins[w], _half(outs[w], rws, c), loc_sems.at[w])
            lc.start()
            locs.append(lc)
        for w, cp in enumerate(cps):
            cp.wait_send()
            rws = 2 * fs[w].shape[0]
            pltpu.make_async_remote_copy(
                src_ref=ins[w], dst_ref=_half(outs[w], rws, 1 - c), send_sem=send_sems.at[w], recv_sem=recv_sems.at[w],
                device_id=sibling, device_id_type=MESH).wait_recv()
        for lc in locs:
            lc.wait()

    return pl.pallas_call(
        body, name="share_halves", out_shape=[sds((2 * f.shape[0], f.shape[1]), F32) for f in fs],
        in_specs=[ANY] * nw, out_specs=[ANY] * nw,
        scratch_shapes=[pltpu.SemaphoreType.DMA((nw,)), pltpu.SemaphoreType.DMA((nw,)), pltpu.SemaphoreType.DMA((nw,))],
    )(*fs)


def rope_tables(t):
    pos = jnp.arange(t, dtype=F32)
    inv_freq = ROPE_THETA ** (-jnp.arange(0, ROT, 2, dtype=F32) / ROT)
    ang = pos[:, None] * inv_freq[None, :]
    cos, sin = jnp.cos(ang), jnp.sin(ang)
    d = jnp.arange(128) % HD
    half = ROT // 2
    cs = jnp.take(cos, d % half, axis=1)
    sn = jnp.take(sin, d % half, axis=1)
    cc = jnp.where(d[None] < ROT, cs, 1.0)
    sa = jnp.where(d[None] < half, -sn, 0.0)
    sb = jnp.where((d[None] >= half) & (d[None] < ROT), sn, 0.0)
    return cc, sa, sb


def _rope(v, cc, sa, sb):
    w = v.shape[1]
    reps = w // 128
    half = ROT // 2
    return (v * jnp.tile(cc, (1, reps)) + pltpu.roll(v, w - half, 1) * jnp.tile(sa, (1, reps))
            + pltpu.roll(v, half, 1) * jnp.tile(sb, (1, reps)))


def _rope_t(dv, cc, sa, sb):
    w = dv.shape[1]
    reps = w // 128
    half = ROT // 2
    return (dv * jnp.tile(cc, (1, reps)) + pltpu.roll(dv * jnp.tile(sa, (1, reps)), half, 1)
            + pltpu.roll(dv * jnp.tile(sb, (1, reps)), w - half, 1))


def pool_fwd(h, b_in, t, tm):
    tm = min(tm, t)
    per = tm // HALO

    def body(prev_ref, cur_ref, b_ref, o_ref, xx):
        i = pl.program_id(0)
        b = b_ref[...]
        xx[pl.ds(0, HALO), :] = jnp.where(i > 0, prev_ref[...] + b, 0.0)
        xx[pl.ds(HALO, tm), :] = cur_ref[...] + b
        tpos = i * tm + lax.broadcasted_iota(jnp.int32, (tm, PG), 0) + 1
        for gi, w in enumerate(POOL_WINDOWS):
            cols = pl.ds(gi * PG, PG)
            acc = xx[pl.ds(HALO, tm), cols]
            for s in range(1, w):
                acc = acc + xx[pl.ds(HALO - s, tm), cols]
            cnt = jnp.minimum(tpos, w).astype(F32)
            o_ref[:, cols] = (acc / cnt - xx[pl.ds(HALO, tm), cols]).astype(o_ref.dtype)

    return pl.pallas_call(
        body, name="pool_fwd", grid=(t // tm,),
        in_specs=[pl.BlockSpec((HALO, PW), lambda i: (jnp.maximum(i * per - 1, 0), 0)),
                  pl.BlockSpec((tm, PW), lambda i: (i, 0)), pl.BlockSpec((1, PW), lambda i: (0, 0))],
        out_specs=pl.BlockSpec((tm, PW), lambda i: (i, 0)), out_shape=sds((t, PW), BF16),
        scratch_shapes=[pltpu.VMEM((tm + HALO, PW), F32)], compiler_params=_params(("arbitrary",)),
    )(h, h, b_in)


def pool_bwd(dpooled, t, tm):
    tm = min(tm, t)
    per = tm // HALO
    nt = t // tm

    def body(cur_ref, nxt_ref, o_ref, db_ref, ee):
        i = pl.program_id(0)
        tpos = i * tm + lax.broadcasted_iota(jnp.int32, (tm, PG), 0) + 1
        for gi, w in enumerate(POOL_WINDOWS):
            cols = pl.ds(gi * PG, PG)
            ee[pl.ds(0, tm), cols] = cur_ref[:, cols] / jnp.minimum(tpos, w).astype(F32)
            ee[pl.ds(tm, HALO), cols] = jnp.where(i < nt - 1, nxt_ref[:, cols] / float(w), 0.0)
        for gi, w in enumerate(POOL_WINDOWS):
            cols = pl.ds(gi * PG, PG)
            acc = ee[pl.ds(0, tm), cols]
            for s in range(1, w):
                acc = acc + ee[pl.ds(s, tm), cols]
            dxp = acc - cur_ref[:, cols]
            o_ref[:, cols] = dxp.astype(o_ref.dtype)
            part = colsum(dxp)

            @pl.when(i == 0)
            def _(cols=cols, part=part):
                db_ref[:, cols] = part

            @pl.when(i > 0)
            def _(cols=cols, part=part):
                db_ref[:, cols] += part

    return pl.pallas_call(
        body, name="pool_bwd", grid=(nt,),
        in_specs=[pl.BlockSpec((tm, PW), lambda i: (i, 0)),
                  pl.BlockSpec((HALO, PW), lambda i: (jnp.minimum((i + 1) * per, t // HALO - 1), 0))],
        out_specs=[pl.BlockSpec((tm, PW), lambda i: (i, 0)), pl.BlockSpec((1, PW), lambda i: (0, 0))],
        out_shape=[sds((t, PW), BF16), sds((1, PW), F32)],
        scratch_shapes=[pltpu.VMEM((tm + HALO, PW), F32)], compiler_params=_params(("arbitrary",)),
    )(dpooled, dpooled)


def _scores(qh, kp, kc, mask_p, mask_c, sink):
    sp = jnp.where(mask_p, lax.dot_general(qh, kp, NT, preferred_element_type=F32), -1e30)
    sc = jnp.where(mask_c, lax.dot_general(qh, kc, NT, preferred_element_type=F32), -1e30)
    m = jnp.maximum(jnp.maximum(jnp.max(sp, axis=-1, keepdims=True), jnp.max(sc, axis=-1, keepdims=True)), sink)
    pp, pc = jnp.exp(sp - m), jnp.exp(sc - m)
    es = jnp.exp(sink - m)
    den = jnp.sum(pp, axis=-1, keepdims=True) + jnp.sum(pc, axis=-1, keepdims=True) + es
    return pp / den, pc / den, es / den


def _masks(n):
    qi = lax.broadcasted_iota(jnp.int32, (BLK, BLK), 0)
    kj = lax.broadcasted_iota(jnp.int32, (BLK, BLK), 1)
    return (kj > qi) & (n > 0), kj <= qi


def attn_fwd(q, k, v, sinks, t):
    def body(s_ref, q_ref, kp_ref, kc_ref, vp_ref, vc_ref, o_ref):
        n = pl.program_id(0)
        mask_p, mask_c = _masks(n)
        for h in range(N_Q):
            kv = pl.ds(HD * (h // (N_Q // N_KV)), HD)
            hq = pl.ds(HD * h, HD)
            pp, pc, _ = _scores(q_ref[:, hq], kp_ref[:, kv], kc_ref[:, kv], mask_p, mask_c, s_ref[0, h])
            o = (lax.dot_general(pp.astype(BF16), vp_ref[:, kv], NN, preferred_element_type=F32)
                 + lax.dot_general(pc.astype(BF16), vc_ref[:, kv], NN, preferred_element_type=F32))
            o_ref[:, hq] = o.astype(o_ref.dtype)

    prev = lambda n: (jnp.maximum(n - 1, 0), 0)
    cur = lambda n: (n, 0)
    return pl.pallas_call(
        body, name="attn_fwd", grid=(t // BLK,),
        in_specs=[pl.BlockSpec(memory_space=pltpu.SMEM), pl.BlockSpec((BLK, QW), cur),
                  pl.BlockSpec((BLK, KVW), prev), pl.BlockSpec((BLK, KVW), cur),
                  pl.BlockSpec((BLK, KVW), prev), pl.BlockSpec((BLK, KVW), cur)],
        out_specs=pl.BlockSpec((BLK, QW), cur), out_shape=sds((t, QW), BF16),
        compiler_params=_params(("arbitrary",)),
    )(sinks, q, k, k, v, v)


def attn_bwd(q, k, v, do, sinks, t):
    nb = t // BLK
    grp = N_Q // N_KV

    def body(s_ref, q_ref, do_ref, kp_ref, kc_ref, vp_ref, vc_ref, dq_ref, dk_ref, dv_ref, ds_ref, dkc, dvc):
        n = pl.program_id(0)

        @pl.when(n == 0)
        def _():
            dkc[...] = jnp.zeros_like(dkc)
            dvc[...] = jnp.zeros_like(dvc)
            ds_ref[...] = jnp.zeros_like(ds_ref)

        @pl.when(n < nb)
        def _():
            mask_p, mask_c = _masks(n)
            lane = lax.broadcasted_iota(jnp.int32, (1, 128), 1)
            dsink = jnp.zeros((1, 128), F32)
            for hk in range(N_KV):
                kv = pl.ds(HD * hk, HD)
                kp, kc, vp, vc = kp_ref[:, kv], kc_ref[:, kv], vp_ref[:, kv], vc_ref[:, kv]
                dkp = jnp.zeros((BLK, HD), F32)
                dkn = jnp.zeros((BLK, HD), F32)
                dvp = jnp.zeros((BLK, HD), F32)
                dvn = jnp.zeros((BLK, HD), F32)
                for g in range(grp):
                    h = grp * hk + g
                    hq = pl.ds(HD * h, HD)
                    qh, doh = q_ref[:, hq], do_ref[:, hq]
                    pp, pc, ps = _scores(qh, kp, kc, mask_p, mask_c, s_ref[0, h])
                    dpp = lax.dot_general(doh, vp, NT, preferred_element_type=F32)
                    dpc = lax.dot_general(doh, vc, NT, preferred_element_type=F32)
                    delta = jnp.sum(pp * dpp, axis=-1, keepdims=True) + jnp.sum(pc * dpc, axis=-1, keepdims=True)
                    dsp = (pp * (dpp - delta)).astype(BF16)
                    dsc = (pc * (dpc - delta)).astype(BF16)
                    dsink = dsink + jnp.where(lane == h, -jnp.sum(ps * delta), 0.0)
                    dq_ref[:, hq] = (lax.dot_general(dsp, kp, NN, preferred_element_type=F32)
                                     + lax.dot_general(dsc, kc, NN, preferred_element_type=F32))
                    dkp = dkp + lax.dot_general(dsp, qh, TN, preferred_element_type=F32)
                    dkn = dkn + lax.dot_general(dsc, qh, TN, preferred_element_type=F32)
                    dvp = dvp + lax.dot_general(pp.astype(BF16), doh, TN, preferred_element_type=F32)
                    dvn = dvn + lax.dot_general(pc.astype(BF16), doh, TN, preferred_element_type=F32)
                dk_ref[:, kv] = dkc[:, kv] + dkp
                dv_ref[:, kv] = dvc[:, kv] + dvp
                dkc[:, kv] = dkn
                dvc[:, kv] = dvn
            ds_ref[...] += dsink

        @pl.when(n == nb)
        def _():
            dk_ref[...] = dkc[...]
            dv_ref[...] = dvc[...]

    cur = lambda n: (jnp.minimum(n, nb - 1), 0)
    prev = lambda n: (jnp.clip(n - 1, 0, nb - 1), 0)
    return pl.pallas_call(
        body, name="attn_bwd", grid=(nb + 1,),
        in_specs=[pl.BlockSpec(memory_space=pltpu.SMEM), pl.BlockSpec((BLK, QW), cur), pl.BlockSpec((BLK, QW), cur),
                  pl.BlockSpec((BLK, KVW), prev), pl.BlockSpec((BLK, KVW), cur),
                  pl.BlockSpec((BLK, KVW), prev), pl.BlockSpec((BLK, KVW), cur)],
        out_specs=[pl.BlockSpec((BLK, QW), cur), pl.BlockSpec((BLK, KVW), prev), pl.BlockSpec((BLK, KVW), prev),
                   pl.BlockSpec((1, 128), lambda n: (0, 0))],
        out_shape=[sds((t, QW), F32), sds((t, KVW), F32), sds((t, KVW), F32), sds((1, 128), F32)],
        scratch_shapes=[pltpu.VMEM((BLK, KVW), F32), pltpu.VMEM((BLK, KVW), F32)],
        compiler_params=_params(("arbitrary",)),
    )(sinks, q, do, k, k, v, v)


def _adamw(w, g, m, v):
    m2 = B1 * m + (1.0 - B1) * g
    v2 = B2 * v + (1.0 - B2) * jnp.square(g)
    m_hat = m2 / (1.0 - B1 ** STEP)
    v_hat = v2 / (1.0 - B2 ** STEP)
    return -LR * (m_hat / (jnp.sqrt(v_hat) + EPS) + WD * w), m2, v2


def ada_fwd(c16, w_ada, b_sh):
    tn = 512

    def body(c_ref, w_ref, b_ref, o_ref):
        cv = c_ref[...]
        sc = (cv * _sigmoid(cv)).astype(BF16)
        o_ref[...] = lax.dot_general(sc, w_ref[...].astype(BF16), NN, preferred_element_type=F32) + b_ref[...]

    return pl.pallas_call(
        body, name="ada_fwd", grid=(ADA_SH // tn,),
        in_specs=[pl.BlockSpec((16, D), lambda j: (0, 0)), pl.BlockSpec((D, tn), lambda j: (0, j)),
                  pl.BlockSpec((1, tn), lambda j: (0, j))],
        out_specs=pl.BlockSpec((16, tn), lambda j: (0, j)), out_shape=sds((16, ADA_SH), F32),
        compiler_params=_params(("arbitrary",)),
    )(c16, w_ada, b_sh)


def ada_bwd_adam(c16, gm16, w, m, v):
    tm, tn = 256, 512

    def body(c_ref, g_ref, w_ref, m_ref, v_ref, go_ref, d_ref, mo_ref, vo_ref):
        cv = c_ref[...]
        sc = (cv * _sigmoid(cv)).astype(BF16)
        g = lax.dot_general(sc, g_ref[...].astype(BF16), TN, preferred_element_type=F32)
        dl, m2, v2 = _adamw(w_ref[...], g, m_ref[...], v_ref[...])
        go_ref[...] = g
        d_ref[...] = dl
        mo_ref[...] = m2
        vo_ref[...] = v2

    blk = pl.BlockSpec((tm, tn), lambda i, j: (i, j))
    return pl.pallas_call(
        body, name="ada_bwd_adam", grid=(D // tm, ADA_SH // tn),
        in_specs=[pl.BlockSpec((16, tm), lambda i, j: (0, i)), pl.BlockSpec((16, tn), lambda i, j: (0, j)), blk, blk, blk],
        out_specs=[blk] * 4, out_shape=[sds((D, ADA_SH), F32)] * 4,
        compiler_params=_params(("arbitrary", "arbitrary")),
    )(c16, gm16, w, m, v)


def adam_rows(name, w, g, m, v, tm):
    rows, cols = w.shape

    def fn(wv, gv, mv, vv):
        gv = gv[:, :cols]
        dl, m2, v2 = _adamw(wv, gv, mv, vv)
        return gv, dl, m2, v2

    return rowmap(name, fn, [T_(w), T_(g), T_(m), T_(v)], [(cols, F32)] * 4, rows=rows, tm=tm)


def adam_small(name, w, g, m, v):
    def body(w_ref, g_ref, m_ref, v_ref, d_ref, mo_ref, vo_ref):
        dl, m2, v2 = _adamw(w_ref[...], g_ref[...], m_ref[...], v_ref[...])
        d_ref[...] = dl
        mo_ref[...] = m2
        vo_ref[...] = v2

    return pl.pallas_call(body, name=name, out_shape=[sds(w.shape, F32)] * 3)(w, g, m, v)


def sum_devices(allv):
    def body(a_ref, o_ref):
        acc = a_ref[0]
        for d in range(1, 8):
            acc = acc + a_ref[d]
        o_ref[...] = acc

    return pl.pallas_call(body, name="sum_devices", out_shape=sds(allv.shape[1:], F32))(allv)


def _ln_fwd(z, g, b):
    mu = jnp.mean(z, axis=-1, keepdims=True)
    zc = z - mu
    var = jnp.mean(jnp.square(zc), axis=-1, keepdims=True)
    return zc * lax.rsqrt(var + LN_EPS) * g + b


def _ln_bwd(z, g, dout):
    mu = jnp.mean(z, axis=-1, keepdims=True)
    zc = z - mu
    var = jnp.mean(jnp.square(zc), axis=-1, keepdims=True)
    rstd = lax.rsqrt(var + LN_EPS)
    xh = zc * rstd
    dxh = dout * g
    dz = rstd * (dxh - jnp.mean(dxh, axis=-1, keepdims=True) - xh * jnp.mean(dxh * xh, axis=-1, keepdims=True))
    return dz, colsum(dout * xh), colsum(dout)


def modulate(name, xin, shift, scale, t):
    return rowmap(name, lambda xv, sh, sc: xv * (1.0 + sc) + sh, [T_(xin), B_(shift), B_(scale)], [(D, BF16)],
                  rows=t, tm=512)


def residual_ln(name, xin, y, gate, lg, lb, wgt, t):
    def fn(xv, yv, gt, g, b):
        z = ALPHA * xv + (wgt * (1.0 + gt)) * yv
        return _ln_fwd(z, g, b), z

    return rowmap(name, fn, [T_(xin), T_(y), B_(gate), B_(lg), B_(lb)], [(D, F32), (D, F32)], rows=t, tm=256)


def residual_ln_bwd(name, z, dout, y, gate, lg, wgt, t):
    def fn(zv, dv, yv, gt, g):
        dz, dg, db = _ln_bwd(zv, g, dv)
        return dz, (wgt * (1.0 + gt)) * dz, dg, db, colsum(wgt * dz * yv)

    return rowmap(name, fn, [T_(z), T_(dout), T_(y), B_(gate), B_(lg)], [(D, F32), (D, BF16)],
                  [(1, D), (1, D), (1, D)], rows=t, tm=256)


def modulate_bwd(name, dz, du, xin, scale, t):
    def fn(dzv, duv, xv, sc):
        return ALPHA * dzv + duv * (1.0 + sc), colsum(duv), colsum(duv * xv)

    return rowmap(name, fn, [T_(dz), T_(du), T_(xin), B_(scale)], [(D, F32)], [(1, D), (1, D)], rows=t, tm=256)


def ffn_fwd(tag, xin, mod, lg, lb, wi, wo, t):
    tm = min(1024, t)
    tn = 256
    per = FHP // tn
    u = modulate(tag + "_mod", xin, mod[0], mod[1], t)

    def act(accs, _):
        a, b = accs
        return a, b, a * _sigmoid(a) * b

    hblk = pl.BlockSpec((tm, tn), lambda i, j, k: (i, j))
    ha, hb, g = mm(
        tag + "_up", [u], [wi, wi], [(0, 0, 0), (0, 1, 1)], dims=NN, grid=(t // tm, 2 * per, 1),
        a_specs=[pl.BlockSpec((tm, D), lambda i, j, k: (i, 0))],
        b_specs=[pl.BlockSpec((None, D, tn), lambda i, j, k: (j // per, 0, j % per)),
                 pl.BlockSpec((None, D, tn), lambda i, j, k: (2 + j // per, 0, j % per))],
        outs=[sds((t, 2 * FHP), BF16)] * 3, out_specs=[hblk] * 3, acc_shapes=[(tm, tn)] * 2, epilogue=act)
    tk = FHP // 2
    y = mm(
        tag + "_down", [g], [wo], [(0, 0, 0)], dims=NN, grid=(t // tm, 2, 4),
        a_specs=[pl.BlockSpec((tm, tk), lambda i, j, k: (i, k))],
        b_specs=[pl.BlockSpec((tk, D // 2), lambda i, j, k: (k, j))],
        outs=[sds((t, D), F32)], out_specs=[pl.BlockSpec((tm, D // 2), lambda i, j, k: (i, j))],
        acc_shapes=[(tm, D // 2)])
    xo, z = residual_ln(tag + "_ln", xin, y, mod[2], lg, lb, 0.5, t)
    return xo, (u, ha, hb, g, y, z)


def ffn_bwd(tag, xin, saved, dout, mod, lg, wi, wo, t):
    u, ha, hb, g, y, z = saved
    tm = min(1024, t)
    dz, dy, dlg, dlb, dgate = residual_ln_bwd(tag + "_ln_bwd", z, dout, y, mod[2], lg, 0.5, t)

    def dact(accs, ex):
        dg = accs[0]
        a, b = ex[0].astype(F32), ex[1].astype(F32)
        s = _sigmoid(a)
        return dg * b * (s * (1.0 + a * (1.0 - s))), dg * (a * s)

    tn = 256
    hblk = pl.BlockSpec((tm, tn), lambda i, j, k: (i, j))
    dha, dhb = mm(
        tag + "_dact", [dy], [wo], [(0, 0, 0)], dims=NT, grid=(t // tm, 2 * FHP // tn, 1),
        a_specs=[pl.BlockSpec((tm, D), lambda i, j, k: (i, 0))],
        b_specs=[pl.BlockSpec((tn, D), lambda i, j, k: (j, 0))],
        outs=[sds((t, 2 * FHP), BF16)] * 2, out_specs=[hblk] * 2, acc_shapes=[(tm, tn)],
        epilogue=dact, extras=[ha, hb], extra_specs=[hblk] * 2)
    tk = min(1024, t)
    th = FHP // 2
    dwo = mm(
        tag + "_dwo", [g], [dy], [(0, 0, 0)], dims=TN, grid=(4, 2, t // tk),
        a_specs=[pl.BlockSpec((tk, th), lambda i, j, k: (k, i))],
        b_specs=[pl.BlockSpec((tk, D // 2), lambda i, j, k: (k, j))],
        outs=[sds((2 * FHP, D), BF16)], out_specs=[pl.BlockSpec((th, D // 2), lambda i, j, k: (i, j))],
        acc_shapes=[(th, D // 2)])
    dwi = None
    for part, dh in enumerate((dha, dhb)):
        dwi = mm(
            f"{tag}_dwi{part}", [u], [dh], [(0, 0, 0)], dims=TN, grid=(2, 4, t // tk),
            a_specs=[pl.BlockSpec((tk, D // 2), lambda i, j, k: (k, i))],
            b_specs=[pl.BlockSpec((tk, th), lambda i, j, k: (k, j))],
            outs=[sds((4, D, FHP), BF16)],
            out_specs=[pl.BlockSpec((None, D // 2, th), lambda i, j, k, part=part: (2 * part + j // 2, i, j % 2))],
            acc_shapes=[(D // 2, th)], carry=dwi)
    du = mm(
        tag + "_du", [dha, dhb], [wi, wi], [(0, 0, 0), (1, 1, 0)], dims=NT, grid=(t // tm, 2, 4),
        a_specs=[pl.BlockSpec((tm, th), lambda i, j, k: (i, k))] * 2,
        b_specs=[pl.BlockSpec((None, D // 2, th), lambda i, j, k: (k // 2, j, k % 2)),
                 pl.BlockSpec((None, D // 2, th), lambda i, j, k: (2 + k // 2, j, k % 2))],
        outs=[sds((t, D), F32)], out_specs=[pl.BlockSpec((tm, D // 2), lambda i, j, k: (i, j))],
        acc_shapes=[(tm, D // 2)])
    dx, dshift, dscale = modulate_bwd(tag + "_mod_bwd", dz, du, xin, mod[1], t)
    return dx, dwi, dwo.reshape(2, FHP, D), (dshift, dscale, dgate), dlg, dlb


def mix_fwd(xin, mod, lg, lb, wts, b_in, pool_scale, sinks, tabs, t):
    w_in, wp, wba, wbb, wo = wts
    tm = min(1024, t)
    u = modulate("mix_mod", xin, mod[0], mod[1], t)
    tmh = min(512, t)
    h = mm("mix_in", [u], [w_in], [(0, 0, 0)], dims=NN, grid=(t // tmh, 4, 1),
           a_specs=[pl.BlockSpec((tmh, D), lambda i, j, k: (i, 0))],
           b_specs=[pl.BlockSpec((None, D, IN_SH), lambda i, j, k: (j, 0, 0))],
           outs=[sds((t, IN_W), F32)], out_specs=[pl.BlockSpec((tmh, IN_SH), lambda i, j, k: (i, j))],
           acc_shapes=[(tmh, IN_SH)])
    pooled = pool_fwd(h, b_in, t, 512)
    gblk = pl.BlockSpec((tm, PG), lambda i, j, k: (i, j))
    mixed = mm("mix_pool", [pooled], [wp], [(0, 0, 0)], dims=NN, grid=(t // tm, 4, 1), a_specs=[gblk],
               b_specs=[pl.BlockSpec((None, PG, PG), lambda i, j, k: (j, 0, 0))],
               outs=[sds((t, PW), F32)], out_specs=[gblk], acc_shapes=[(tm, PG)])
    pm = rowmap("mix_pscale", lambda mv, ps: mv * ps, [T_(mixed), B_(pool_scale)], [(PW, BF16)], rows=t, tm=512)

    def branch(name, a, w):
        return mm(name, [a], [w], [(0, 0, 0)], dims=NN, grid=(t // tm, 4, 1),
                  a_specs=[pl.BlockSpec((tm, PW), lambda i, j, k: (i, 0))],
                  b_specs=[pl.BlockSpec((None, PW, D // 4), lambda i, j, k: (j, 0, 0))],
                  outs=[sds((t, D), F32)], out_specs=[pl.BlockSpec((tm, D // 4), lambda i, j, k: (i, j))],
                  acc_shapes=[(tm, D // 4)])

    ya = branch("mix_branch_a", pm, wba)

    def qkv(hq, hk, hv, bq, bk, bv, cc, sa, sb):
        return (_rope(hq + bq, cc, sa, sb) * (HD ** -0.5), _rope(hk + bk, cc, sa, sb), hv + bv)

    qr, kr, vv = rowmap(
        "mix_rope", qkv,
        [T_(h, QW, 1), T_(h, KVW, 8), T_(h, KVW, 9), B_(b_in, QW, 1), B_(b_in, KVW, 8), B_(b_in, KVW, 9),
         T_(tabs[0]), T_(tabs[1]), T_(tabs[2])],
        [(QW, BF16), (KVW, BF16), (KVW, BF16)], rows=t, tm=512)
    attn = attn_fwd(qr, kr, vv, sinks, t)
    yb = branch("mix_branch_b", attn, wbb)
    cw = 512

    def merge(ga, gb, ba, bb, yav, ybv):
        return _sigmoid(ga + ba) * yav + _sigmoid(gb + bb) * ybv

    merged = rowmap(
        "mix_merge", merge,
        [T_(h, cw, 5), T_(h, cw, 9), B_(b_in, cw, 5), B_(b_in, cw, 9), T_(ya, cw), T_(yb, cw)],
        [(D, BF16)], rows=t, tm=512, ncol=D // cw)
    y = mm("mix_out", [merged], [wo], [(0, 0, 0)], dims=NN, grid=(t // tm, 2, 1),
           a_specs=[pl.BlockSpec((tm, D), lambda i, j, k: (i, 0))],
           b_specs=[pl.BlockSpec((D, D // 2), lambda i, j, k: (0, j))],
           outs=[sds((t, D), F32)], out_specs=[pl.BlockSpec((tm, D // 2), lambda i, j, k: (i, j))],
           acc_shapes=[(tm, D // 2)])
    xo, z = residual_ln("mix_ln", xin, y, mod[2], lg, lb, 1.0, t)
    return xo, (u, h, pooled, mixed, pm, ya, qr, kr, vv, attn, yb, merged, y, z)


def mix_bwd(xin, saved, dout, mod, lg, wts, b_in, pool_scale, sinks, tabs, t):
    u, h, pooled, mixed, pm, ya, qr, kr, vv, attn, yb, merged, y, z = saved
    w_in, wp, wba, wbb, wo = wts
    tm = min(1024, t)
    tk = min(1024, t)
    dz, dy, dlg, dlb, dgate = residual_ln_bwd("mix_ln_bwd", z, dout, y, mod[2], lg, 1.0, t)
    dmerged = mm("mix_dmerged", [dy], [wo], [(0, 0, 0)], dims=NT, grid=(t // tm, 2, 1),
                 a_specs=[pl.BlockSpec((tm, D), lambda i, j, k: (i, 0))],
                 b_specs=[pl.BlockSpec((D // 2, D), lambda i, j, k: (j, 0))],
                 outs=[sds((t, D), F32)], out_specs=[pl.BlockSpec((tm, D // 2), lambda i, j, k: (i, j))],
                 acc_shapes=[(tm, D // 2)])
    half = pl.BlockSpec((tk, D // 2), lambda i, j, k: (k, i))
    dwo = mm("mix_dwo", [merged], [dy], [(0, 0, 0)], dims=TN, grid=(2, 2, t // tk), a_specs=[half],
             b_specs=[pl.BlockSpec((tk, D // 2), lambda i, j, k: (k, j))],
             outs=[sds((D, D), BF16)], out_specs=[pl.BlockSpec((D // 2, D // 2), lambda i, j, k: (i, j))],
             acc_shapes=[(D // 2, D // 2)])
    cw = 512

    def dmerge(dm, ga, gb, ba, bb, yav, ybv):
        sa_, sb_ = _sigmoid(ga + ba), _sigmoid(gb + bb)
        dga = dm * yav * sa_ * (1.0 - sa_)
        dgb = dm * ybv * sb_ * (1.0 - sb_)
        return dm * sa_, dm * sb_, dga, dgb, colsum(dga), colsum(dgb)

    dya, dyb, dgla, dglb, dbga, dbgb = rowmap(
        "mix_dmerge", dmerge,
        [T_(dmerged, cw), T_(h, cw, 5), T_(h, cw, 9), B_(b_in, cw, 5), B_(b_in, cw, 9), T_(ya, cw), T_(yb, cw)],
        [(D, BF16)] * 4, [(1, D), (1, D)], rows=t, tm=512, ncol=D // cw)

    def dbranch(name, dyv, act, w):
        dwb = mm(name + "_dw", [act], [dyv], [(0, 0, 0)], dims=TN, grid=(1, 4, t // tk),
                 a_specs=[pl.BlockSpec((tk, PW), lambda i, j, k: (k, 0))],
                 b_specs=[pl.BlockSpec((tk, D // 4), lambda i, j, k: (k, j))],
                 outs=[sds((4, PW, D // 4), BF16)], out_specs=[pl.BlockSpec((None, PW, D // 4), lambda i, j, k: (j, 0, 0))],
                 acc_shapes=[(PW, D // 4)])
        return dwb, lambda dt: mm(
            name + "_dx", [dyv], [w], [(0, 0, 0)], dims=NT, grid=(t // tm, 1, 4),
            a_specs=[pl.BlockSpec((tm, D // 4), lambda i, j, k: (i, k))],
            b_specs=[pl.BlockSpec((None, PW, D // 4), lambda i, j, k: (k, 0, 0))],
            outs=[sds((t, PW), dt)], out_specs=[pl.BlockSpec((tm, PW), lambda i, j, k: (i, 0))], acc_shapes=[(tm, PW)])

    dwba, dpm_fn = dbranch("mix_dbranch_a", dya, pm, wba)
    dwbb, dattn_fn = dbranch("mix_dbranch_b", dyb, attn, wbb)
    dpm, dattn = dpm_fn(F32), dattn_fn(BF16)
    dmixed, dps = rowmap("mix_dpscale", lambda dp, mv, ps: (dp * ps, colsum(dp * mv)),
                         [T_(dpm), T_(mixed), B_(pool_scale)], [(PW, BF16)], [(1, PW)], rows=t, tm=512)
    gblk = pl.BlockSpec((tm, PG), lambda i, j, k: (i, j))
    dpooled = mm("mix_dpool", [dmixed], [wp], [(0, 0, 0)], dims=NT, grid=(t // tm, 4, 1), a_specs=[gblk],
                 b_specs=[pl.BlockSpec((None, PG, PG), lambda i, j, k: (j, 0, 0))],
                 outs=[sds((t, PW), F32)], out_specs=[gblk], acc_shapes=[(tm, PG)])
    kblk = pl.BlockSpec((tk, PG), lambda i, j, k: (k, i))
    dwp = mm("mix_dwpool", [pooled], [dmixed], [(0, 0, 0)], dims=TN, grid=(4, 1, t // tk), a_specs=[kblk], b_specs=[kblk],
             outs=[sds((4, PG, PG), BF16)], out_specs=[pl.BlockSpec((None, PG, PG), lambda i, j, k: (i, 0, 0))],
             acc_shapes=[(PG, PG)])
    dxp, dbxp = pool_bwd(dpooled, t, 512)
    dqr, dkr, dvv, dsinks = attn_bwd(qr, kr, vv, dattn, sinks, t)

    def dqkv(dq, dk, dv, cc, sa, sb):
        dq = _rope_t(dq, cc, sa, sb) * (HD ** -0.5)
        dk = _rope_t(dk, cc, sa, sb)
        return dq, dk, dv, colsum(dq), colsum(dk), colsum(dv)

    dq, dk, dvb, dbq, dbk, dbv = rowmap(
        "mix_rope_bwd", dqkv, [T_(dqr), T_(dkr), T_(dvv), T_(tabs[0]), T_(tabs[1]), T_(tabs[2])],
        [(QW, BF16), (KVW, BF16), (KVW, BF16)], [(1, QW), (1, KVW), (1, KVW)], rows=t, tm=512)
    dh = jnp.concatenate([dxp, dq, dk, dvb, dgla, dglb], axis=1)
    db_in = jnp.concatenate([dbxp, dbq, dbk, dbv, dbga, dbgb], axis=1)
    dwin = mm("mix_dwin", [u], [dh], [(0, 0, 0)], dims=TN, grid=(2, 4, t // tk), a_specs=[half],
              b_specs=[pl.BlockSpec((tk, IN_SH), lambda i, j, k: (k, j))],
              outs=[sds((4, D, IN_SH), BF16)], out_specs=[pl.BlockSpec((None, D // 2, IN_SH), lambda i, j, k: (j, i, 0))],
              acc_shapes=[(D // 2, IN_SH)])
    du = mm("mix_du", [dh], [w_in], [(0, 0, 0)], dims=NT, grid=(t // tm, 2, 4),
            a_specs=[pl.BlockSpec((tm, IN_SH), lambda i, j, k: (i, k))],
            b_specs=[pl.BlockSpec((None, D // 2, IN_SH), lambda i, j, k: (k, j, 0))],
            outs=[sds((t, D), F32)], out_specs=[pl.BlockSpec((tm, D // 2), lambda i, j, k: (i, j))],
            acc_shapes=[(tm, D // 2)])
    dx, dshift, dscale = modulate_bwd("mix_mod_bwd", dz, du, xin, mod[1], t)
    return dx, (dwin, dwp, dwba, dwbb, dwo), (dshift, dscale, dgate), dlg, dlb, db_in, dps, dsinks


def cast_shard(name, w, pad=0):
    rows, cols = w.shape

    def fn(wv):
        wb = wv.astype(BF16)
        return jnp.concatenate([wb, jnp.zeros((wb.shape[0], pad), BF16)], axis=1) if pad else wb

    tm = rows // 4 if rows % 64 == 0 else rows // 2
    return rowmap(name, fn, [T_(w)], [(cols + pad, BF16)], rows=rows, tm=tm)


def kernel(x, c, w_ada, b_ada, ln_g, ln_b, w_ffn1_in, w_ffn1_out, w_in, b_in, w_pool, pool_scale, sinks, w_branch_a, w_branch_b, w_out, w_ffn2_in, w_ffn2_out, loss_target, m_w_ada, m_b_ada, m_ln_g, m_ln_b, m_w_ffn1_in, m_w_ffn1_out, m_w_in, m_b_in, m_w_pool, m_pool_scale, m_sinks, m_w_branch_a, m_w_branch_b, m_w_out, m_w_ffn2_in, m_w_ffn2_out, v_w_ada, v_b_ada, v_ln_g, v_ln_b, v_w_ffn1_in, v_w_ffn1_out, v_w_in, v_b_in, v_w_pool, v_pool_scale, v_sinks, v_w_branch_a, v_w_branch_b, v_w_out, v_w_ffn2_in, v_w_ffn2_out):
    t = x.shape[1]
    xs, tgt = x[0], loss_target[0]
    xi, yi, ci = lax.axis_index("x"), lax.axis_index("y"), lax.axis_index("c")
    chip = 2 * xi + yi
    dev = 2 * chip + ci
    b_in2, ps2, sinks2 = b_in, pool_scale, sinks

    first = jnp.concatenate([c.reshape(-1), ln_g.reshape(-1), ln_b.reshape(-1)]).reshape(-1, 128)
    first_all = allgather_small("gather_cond", first).reshape(8, -1)
    c_all = first_all[:, :D]
    ln_parts = first_all[0::2, D:].reshape(4, 2, 3, D // 4)
    ln_full = jnp.transpose(ln_parts, (1, 2, 0, 3)).reshape(2, 3, D)
    lgs = [ln_full[0, s:s + 1] for s in range(3)]
    lbs = [ln_full[1, s:s + 1] for s in range(3)]
    c16 = jnp.pad(c_all, ((0, 8), (0, 0)))
    b_ada_sh = lax.dynamic_slice(b_ada, (0, chip * ADA_SH), (1, ADA_SH))
    mod_part = ada_fwd(c16, w_ada[0], b_ada_sh)[:8]
    mod_all = allgather_small("gather_mod", mod_part.reshape(-1, 128)).reshape(8, 8, ADA_SH)
    mod_mine = lax.dynamic_index_in_dim(mod_all[0::2], dev, axis=1, keepdims=False).reshape(9, D)
    mods = [[mod_mine[3 * s + k:3 * s + k + 1] for k in range(3)] for s in range(3)]

    plain = [("f1o", w_ffn1_out[0]), ("win", w_in[0]), ("wp", w_pool[0].reshape(4 * 64, PG)), ("wba", w_branch_a[0]),
             ("wbb", w_branch_b[0]), ("wo", w_out[0]), ("f2o", w_ffn2_out[0])]
    sh = {n: cast_shard("cast_" + n, w) for n, w in plain}
    sh["f1i"] = cast_shard("cast_f1i", w_ffn1_in[0], FHP - FH)
    sh["f2i"] = cast_shard("cast_f2i", w_ffn2_in[0], FHP - FH)
    order = ["f1i", "f1o", "win", "wp", "wba", "wbb", "wo", "f2i", "f2o"]
    views = {n: (view_ffn_out if n in ("f1o", "f2o") else view_lead) for n in order}
    gshape = {n: ((2, FHP, D) if n in ("f1o", "f2o") else (4,) + sh[n].shape) for n in order}
    gath = dict(zip(order, gather_weights([(sh[n], gshape[n], views[n]) for n in order])))
    wp_full = jnp.transpose(gath["wp"].reshape(4, 4, 64, PG), (1, 0, 2, 3)).reshape(4, PG, PG)
    wts = (gath["win"], wp_full, gath["wba"], gath["wbb"], gath["wo"].reshape(D, D))
    f1o, f2o = gath["f1o"].reshape(2 * FHP, D), gath["f2o"].reshape(2 * FHP, D)
    tabs = rope_tables(t)

    x1, sv1 = ffn_fwd("ffn1", xs, mods[0], lgs[0], lbs[0], gath["f1i"], f1o, t)
    x2, sv2 = mix_fwd(x1, mods[1], lgs[1], lbs[1], wts, b_in2, ps2, sinks2, tabs, t)
    x3, sv3 = ffn_fwd("ffn2", x2, mods[2], lgs[2], lbs[2], gath["f2i"], f2o, t)

    def lossfn(xv, tv):
        d = xv - tv
        return d * (1.0 / D), jnp.sum(d * d).reshape(1, 1)

    dx3, lsum = rowmap("loss", lossfn, [T_(x3), T_(tgt)], [(D, F32)], [(1, 1)], rows=t, tm=512)
    loss = lax.psum(0.5 * lsum[0, 0] / D, ("x", "y", "c"))

    dx2, dw_f2i, dw_f2o, gm2, dlg2, dlb2 = ffn_bwd("ffn2", x2, sv3, dx3, mods[2], lgs[2], gath["f2i"], f2o, t)
    dx1, dmix, gm1, dlg1, dlb1, db_in, dps, dsinks = mix_bwd(x1, sv2, dx2, mods[1], lgs[1], wts, b_in2, ps2, sinks2, tabs, t)
    dx0, dw_f1i, dw_f1o, gm0, dlg0, dlb0 = ffn_bwd("ffn1", xs, sv1, dx1, mods[0], lgs[0], gath["f1i"], f1o, t)
    dwin, dwp, dwba, dwbb, dwo = dmix

    small = jnp.concatenate([*gm0, *gm1, *gm2, dlg0, dlg1, dlg2, dlb0, dlb1, dlb2, db_in, dps, dsinks], axis=1)
    n_small = small.shape[1]
    rows_small = -(-n_small // 1024) * 8
    small = jnp.pad(small, ((0, 0), (0, rows_small * 128 - n_small))).reshape(rows_small, 128)
    small_all = allgather_small("gather_small", small)
    tot = sum_devices(small_all).reshape(1, -1)
    gmod_all = small_all.reshape(8, -1)[:, :9 * D]
    o = 9 * D
    g_b_ada = tot[:, :o]
    g_ln_g = lax.dynamic_slice(tot[:, o:o + 3 * D].reshape(3, D), (0, chip * (D // 4)), (3, D // 4))
    g_ln_b = lax.dynamic_slice(tot[:, o + 3 * D:o + 6 * D].reshape(3, D), (0, chip * (D // 4)), (3, D // 4))
    o += 6 * D
    g_b_in, g_ps, g_sinks = tot[:, o:o + IN_W], tot[:, o + IN_W:o + IN_W + PW], tot[:, o + IN_W + PW:o + IN_W + PW + N_Q]

    gm16 = jnp.pad(lax.dynamic_slice(gmod_all, (0, chip * ADA_SH), (8, ADA_SH)), ((0, 8), (0, 0)))
    g_w_ada, d_w_ada, nm_w_ada, nv_w_ada = ada_bwd_adam(c16, gm16, w_ada[0], m_w_ada[0], v_w_ada[0])

    dwp_sh = jnp.transpose(dwp.reshape(4, 4, 64, PG), (1, 0, 2, 3)).reshape(4, 4 * 64, PG)
    parts = {"f1i": dw_f1i, "f1o": dw_f1o, "win": dwin, "wp": dwp_sh, "wba": dwba, "wbb": dwbb,
             "wo": dwo.reshape(4, D // 4, D), "f2i": dw_f2i, "f2o": dw_f2o}
    own_a, got_a = reduce_sibling([(parts[n], views[n], sh[n].shape[0], sh[n].shape[1]) for n in order])
    tiles = {"f1i": 512, "f1o": FO // 2, "win": 512, "wp": 128, "wba": 512, "wbb": 512, "wo": 256, "f2i": 512, "f2o": FO // 2}
    qs = []
    for n, a, b in zip(order, own_a, got_a):
        r4, cols = a.shape[0] * a.shape[1], a.shape[2]
        q = rowmap("chipsum_" + n, lambda av, bv: av.astype(F32) + bv.astype(F32),
                   [T_(a.reshape(r4, cols)), T_(b.reshape(r4, cols))], [(cols, BF16)], rows=r4, tm=tiles[n])
        qs.append(q.reshape(a.shape))
    own_c, got_c = reduce_chips(qs)
    fs = []
    for n, a, b in zip(order, own_c, got_c):
        f = rowmap("total_" + n,
                   lambda av, b0, b1, b2: ((av.astype(F32) + b0.astype(F32)) + b1.astype(F32)) + b2.astype(F32),
                   [T_(a), T_(b[0]), T_(b[1]), T_(b[2])], [(a.shape[1], F32)], rows=a.shape[0], tm=tiles[n])
        fs.append(f)
    gw = dict(zip(order, share_halves(fs)))

    def big(n, w, m, v, tm):
        shape = w.shape
        w2, m2, v2 = (a.reshape(shape[-2] if a.ndim == 3 else -1, shape[-1]) for a in (w, m, v))
        return [r.reshape(shape) for r in adam_rows("adam_" + n, w2, gw[n], m2, v2, tm)]

    def tiny(n, w, g, m, v):
        return [g.reshape(w.shape)] + list(adam_small("adam_" + n, w, g.reshape(w.shape), m, v))

    res = {
        "w_ada": [a[None] for a in (g_w_ada, d_w_ada, nm_w_ada, nv_w_ada)],
        "b_ada": tiny("b_ada", b_ada, g_b_ada, m_b_ada, v_b_ada),
        "ln_g": tiny("ln_g", ln_g, g_ln_g, m_ln_g, v_ln_g),
        "ln_b": tiny("ln_b", ln_b, g_ln_b, m_ln_b, v_ln_b),
        "w_ffn1_in": big("f1i", w_ffn1_in, m_w_ffn1_in, v_w_ffn1_in, 128),
        "w_ffn1_out": big("f1o", w_ffn1_out, m_w_ffn1_out, v_w_ffn1_out, 32),
        "w_in": big("win", w_in, m_w_in, v_w_in, 256),
        "b_in": tiny("b_in", b_in, g_b_in, m_b_in, v_b_in),
        "w_pool": big("wp", w_pool, m_w_pool, v_w_pool, 256),
        "pool_scale": tiny("pool_scale", pool_scale, g_ps, m_pool_scale, v_pool_scale),
        "sinks": tiny("sinks", sinks, g_sinks, m_sinks, v_sinks),
        "w_branch_a": big("wba", w_branch_a, m_w_branch_a, v_w_branch_a, 512),
        "w_branch_b": big("wbb", w_branch_b, m_w_branch_b, v_w_branch_b, 512),
        "w_out": big("wo", w_out, m_w_out, v_w_out, 128),
        "w_ffn2_in": big("f2i", w_ffn2_in, m_w_ffn2_in, v_w_ffn2_in, 128),
        "w_ffn2_out": big("f2o", w_ffn2_out, m_w_ffn2_out, v_w_ffn2_out, 32),
    }
    names = ["w_ada", "b_ada", "ln_g", "ln_b", "w_ffn1_in", "w_ffn1_out", "w_in", "b_in", "w_pool", "pool_scale", "sinks",
             "w_branch_a", "w_branch_b", "w_out", "w_ffn2_in", "w_ffn2_out"]
    return (loss, dx0[None], *[res[n][0] for n in names], *[res[n][1] for n in names],
            *[res[n][2] for n in names], *[res[n][3] for n in names])
```

```python
import jax
import jax.numpy as jnp
from jax import lax
from jax.experimental import pallas as pl
from jax.experimental.pallas import tpu as pltpu

F32 = jnp.float32
BF16 = jnp.bfloat16
MESH = pl.DeviceIdType.MESH
ANY = pl.BlockSpec(memory_space=pl.ANY)

D = 2048
N_Q, N_KV, HD = 16, 4, 64
QW, KVW = N_Q * HD, N_KV * HD
BLK = 128
POOL_WINDOWS = (2, 4, 8, 16)
PW, PG = 1024, 256
HALO = 16
ROPE_THETA = 500000.0
ROT = HD // 4
LN_EPS = 1e-5
ALPHA = 2.0 ** 0.25
FH = 2752
FHP = 2816
FO = 1376
IN_W = 6656
IN_SH = IN_W // 4
ADA_SH = 18432 // 4
B1, B2, LR, EPS, WD, STEP = 0.9, 0.999, 0.001, 1e-08, 0.01, 10
VMEM_LIMIT = 56 * 1024 * 1024
FLIPS = ((1, 0), (0, 1), (1, 1))
NN = (((1,), (0,)), ((), ()))
NT = (((1,), (1,)), ((), ()))
TN = (((0,), (0,)), ((), ()))


def _params(sem):
    return pltpu.CompilerParams(dimension_semantics=sem, vmem_limit_bytes=VMEM_LIMIT)


def _aligned(v, m):
    return v if isinstance(v, int) else pl.multiple_of(v, m)


def _sigmoid(v):
    return 1.0 / (1.0 + jnp.exp(-v))


def T_(arr, width=None, off=0):
    return ("t", arr, width, off)


def B_(arr, width=None, off=0):
    return ("b", arr, width, off)


def X_(arr, spec):
    return ("x", arr, spec, 0)


def rowmap(name, fn, ins, outs, accs=(), *, rows, tm, ncol=1, with_ids=False, sp=None, alias=None):
    tm = min(tm, rows)
    nrow = rows // tm
    in_specs, arrs = [], []
    for kind, arr, width, off in ins:
        if kind == "x":
            in_specs.append(width)
        elif kind == "t":
            w = arr.shape[1] if width is None else width
            in_specs.append(pl.BlockSpec((tm, w), lambda j, i, *_, off=off: (i, off + j)))
        else:
            w = arr.shape[1] if width is None else width
            in_specs.append(pl.BlockSpec((arr.shape[0], w), lambda j, i, *_, off=off: (0, off + j)))
        arrs.append(arr)
    out_shape, out_specs = [], []
    for o in outs:
        if len(o) == 3:
            out_shape.append(jax.ShapeDtypeStruct(o[0], o[1]))
            out_specs.append(o[2])
        else:
            out_shape.append(jax.ShapeDtypeStruct((rows, o[0]), o[1]))
            out_specs.append(pl.BlockSpec((tm, o[0] // ncol), lambda j, i, *_: (i, j)))
    for r, width in accs:
        out_shape.append(jax.ShapeDtypeStruct((r, width), F32))
        out_specs.append(pl.BlockSpec((r, width // ncol), lambda j, i, *_: (0, j)))
    ni, no = len(ins), len(outs)
    nsp = 0 if sp is None else 1

    def body(*refs):
        refs = refs[nsp:]
        i = pl.program_id(1)
        vals = [r[...] for r in refs[:ni]]
        res = fn(pl.program_id(0), i, *vals) if with_ids else fn(*vals)
        if not isinstance(res, (tuple, list)):
            res = (res,)
        for r, v in zip(refs[ni:ni + no], res[:no]):
            r[...] = v.astype(r.dtype)
        for r, v in zip(refs[ni + no:], res[no:]):
            @pl.when(i == 0)
            def _(r=r, v=v):
                r[...] = v

            @pl.when(i > 0)
            def _(r=r, v=v):
                r[...] += v

    grid_spec = pltpu.PrefetchScalarGridSpec(num_scalar_prefetch=nsp, grid=(ncol, nrow), in_specs=in_specs,
                                             out_specs=out_specs)
    res = pl.pallas_call(
        body, name=name, grid_spec=grid_spec, out_shape=out_shape,
        input_output_aliases={nsp + k: v for k, v in (alias or {}).items()},
        compiler_params=_params(("arbitrary", "arbitrary")),
    )(*([sp] if nsp else []), *arrs)
    return res[0] if len(res) == 1 else res


def colsum(v):
    return jnp.sum(v, axis=0, keepdims=True)


def mm(name, a_ops, b_ops, ops, *, dims, grid, a_specs, b_specs, outs, out_specs, acc_shapes,
       epilogue=None, extras=(), extra_specs=(), carry=None):
    gk = grid[2]
    na, nb, ne, nacc = len(a_ops), len(b_ops), len(extras), len(acc_shapes)
    nc = 0 if carry is None else 1
    no = len(outs)

    def body(*refs):
        a_refs = refs[:na]
        b_refs = refs[na:na + nb]
        e_refs = refs[na + nb:na + nb + ne]
        o_refs = refs[na + nb + ne + nc:na + nb + ne + nc + no]
        acc_refs = refs[na + nb + ne + nc + no:]
        k = pl.program_id(2)

        def partials():
            res = [None] * nacc
            for ai, bi, ci in ops:
                p = lax.dot_general(a_refs[ai][...], b_refs[bi][...], dims, preferred_element_type=F32)
                res[ci] = p if res[ci] is None else res[ci] + p
            return res

        def finish(accs):
            outv = epilogue(accs, [e[...] for e in e_refs]) if epilogue else (accs[0],)
            for o, v in zip(o_refs, outv):
                o[...] = v.astype(o.dtype)

        if gk == 1:
            finish(partials())
        else:
            ps = partials()

            @pl.when(k == 0)
            def _():
                for acc, p in zip(acc_refs, ps):
                    acc[...] = p

            @pl.when(k > 0)
            def _():
                for acc, p in zip(acc_refs, ps):
                    acc[...] += p

            @pl.when(k == gk - 1)
            def _():
                finish([acc[...] for acc in acc_refs])

    res = pl.pallas_call(
        body, name=name, grid=grid,
        in_specs=list(a_specs) + list(b_specs) + list(extra_specs) + ([ANY] if nc else []),
        out_specs=list(out_specs), out_shape=list(outs),
        scratch_shapes=[pltpu.VMEM(s, F32) for s in acc_shapes] if gk > 1 else [],
        input_output_aliases={na + nb + ne: 0} if nc else {},
        compiler_params=_params(("arbitrary", "arbitrary", "arbitrary")),
    )(*a_ops, *b_ops, *extras, *([carry] if nc else []))
    return res[0] if len(res) == 1 else res


def sds(shape, dt):
    return jax.ShapeDtypeStruct(shape, dt)


def _place():
    x, y, c = lax.axis_index("x"), lax.axis_index("y"), lax.axis_index("c")
    chips = [((1 - x) if fx else x, (1 - y) if fy else y) for fx, fy in FLIPS]
    return x, y, c, chips


def allgather_small(name, v):
    r = v.shape[0]

    def body(x_ref, out_ref, send_sems, recv_sems, local_sem):
        x, y, c, chips = _place()
        me, sibling = (x, y, c), (x, y, 1 - c)

        def rows(px, py, pc):
            return out_ref.at[4 * px + 2 * py + pc]

        def copy(k, block, to, src=None):
            return pltpu.make_async_remote_copy(
                src_ref=rows(*block) if src is None else src, dst_ref=rows(*block),
                send_sem=send_sems.at[k], recv_sem=recv_sems.at[k], device_id=to, device_id_type=MESH)

        mine = pltpu.make_async_copy(x_ref, rows(*me), local_sem)
        mine.start()
        first = [copy(0, me, sibling, src=x_ref)]
        first += [copy(1 + j, me, (*chip, c), src=x_ref) for j, chip in enumerate(chips)]
        for cp in first:
            cp.start()
        passed = [copy(4 + j, (*chip, c), sibling) for j, chip in enumerate(chips)]
        for j, chip in enumerate(chips):
            copy(1 + j, (*chip, c), me).wait_recv()
            passed[j].start()
        copy(0, sibling, me).wait_recv()
        for j, chip in enumerate(chips):
            copy(4 + j, (*chip, 1 - c), me).wait_recv()
        for cp in first + passed:
            cp.wait_send()
        mine.wait()

    return pl.pallas_call(
        body, name=name, out_shape=sds((8, r, 128), v.dtype),
        in_specs=[pl.BlockSpec(memory_space=pltpu.VMEM)], out_specs=pl.BlockSpec(memory_space=pltpu.VMEM),
        scratch_shapes=[pltpu.SemaphoreType.DMA((7,)), pltpu.SemaphoreType.DMA((7,)), pltpu.SemaphoreType.DMA],
    )(v)


def _half(ref, rows, hf):
    hr = rows // 2
    return ref.at[pl.ds(_aligned(hf * hr, 16), hr)]


def view_lead(ref, p):
    return ref.at[p]


def view_ffn_out(ref, p):
    return ref.at[p // 2, pl.ds(_aligned((p % 2) * FO, 16), FO)]


def gather_weights(items):
    nw = len(items)
    pads = [w for w, it in enumerate(items) if it[1] is view_ffn_out]

    def body(*refs):
        outs = refs[nw:2 * nw]
        zero_ref, send_sems, recv_sems, pad_sems = refs[2 * nw:]
        x, y, c, chips = _place()
        me = 2 * x + y
        sibling = (x, y, 1 - c)
        zero_ref[...] = jnp.zeros_like(zero_ref)
        padcp = []
        for n, w in enumerate(pads):
            for h in range(2):
                cp = pltpu.make_async_copy(zero_ref, outs[w].at[h, pl.ds(2 * FO, FHP - 2 * FO)], pad_sems.at[2 * n + h])
                cp.start()
                padcp.append(cp)
        sends, fwds = [], []
        for w, (_, view, rws) in enumerate(items):
            mine = _half(view(outs[w], me), rws, c)
            for f, (px, py) in enumerate(chips):
                cp = pltpu.make_async_remote_copy(
                    src_ref=mine, dst_ref=mine,
                    send_sem=send_sems.at[w, f], recv_sem=recv_sems.at[w, f], device_id=(px, py, c), device_id_type=MESH)
                cp.start()
                sends.append(cp)
        for w, (_, view, rws) in enumerate(items):
            for f, (px, py) in enumerate(chips):
                land = _half(view(outs[w], 2 * px + py), rws, c)
                pltpu.make_async_remote_copy(
                    src_ref=land, dst_ref=land, send_sem=send_sems.at[w, f], recv_sem=recv_sems.at[w, f],
                    device_id=(px, py, c), device_id_type=MESH).wait_recv()
                fw = pltpu.make_async_remote_copy(
                    src_ref=land, dst_ref=land, send_sem=send_sems.at[w, 3 + f], recv_sem=recv_sems.at[w, 3 + f],
                    device_id=sibling, device_id_type=MESH)
                fw.start()
                fwds.append(fw)
        for w, (_, view, rws) in enumerate(items):
            for f, (px, py) in enumerate(chips):
                land = _half(view(outs[w], 2 * px + py), rws, 1 - c)
                pltpu.make_async_remote_copy(
                    src_ref=land, dst_ref=land, send_sem=send_sems.at[w, 3 + f], recv_sem=recv_sems.at[w, 3 + f],
                    device_id=sibling, device_id_type=MESH).wait_recv()
        for cp in sends + fwds:
            cp.wait_send()
        for cp in padcp:
            cp.wait()

    return pl.pallas_call(
        body, name="gather_weights", out_shape=[sds(it[0].shape, BF16) for it in items],
        in_specs=[ANY] * nw, out_specs=[ANY] * nw, input_output_aliases={w: w for w in range(nw)},
        scratch_shapes=[pltpu.VMEM((FHP - 2 * FO, D), BF16), pltpu.SemaphoreType.DMA((nw, 6)),
                        pltpu.SemaphoreType.DMA((nw, 6)), pltpu.SemaphoreType.DMA((max(2 * len(pads), 1),))],
    )(*[it[0] for it in items])


def reduce_sibling(items):
    nw = len(items)

    def body(*refs):
        ins, got = refs[:nw], refs[nw:2 * nw]
        send_sems, recv_sems = refs[2 * nw:]
        x, y, c, _ = _place()
        sibling = (x, y, 1 - c)
        cps = []
        for w, (_, view, rws, _) in enumerate(items):
            for p in range(4):
                cp = pltpu.make_async_remote_copy(
                    src_ref=_half(view(ins[w], p), rws, 1 - c), dst_ref=got[w].at[p],
                    send_sem=send_sems.at[w, p], recv_sem=recv_sems.at[w, p], device_id=sibling, device_id_type=MESH)
                cp.start()
                cps.append(cp)
        for cp in cps:
            cp.wait()

    return pl.pallas_call(
        body, name="reduce_sibling", out_shape=[sds((4, it[2] // 2, it[3]), BF16) for it in items],
        in_specs=[ANY] * nw, out_specs=[ANY] * nw,
        scratch_shapes=[pltpu.SemaphoreType.DMA((nw, 4)), pltpu.SemaphoreType.DMA((nw, 4))],
    )(*[it[0] for it in items])


def reduce_chips(qs):
    nw = len(qs)

    def body(*refs):
        ins, got = refs[:nw], refs[nw:2 * nw]
        send_sems, recv_sems = refs[2 * nw:]
        x, y, c, chips = _place()
        cps = []
        for w in range(nw):
            for f, (px, py) in enumerate(chips):
                cp = pltpu.make_async_remote_copy(
                    src_ref=ins[w].at[2 * px + py], dst_ref=got[w].at[f],
                    send_sem=send_sems.at[w, f], recv_sem=recv_sems.at[w, f], device_id=(px, py, c), device_id_type=MESH)
                cp.start()
                cps.append(cp)
        for cp in cps:
            cp.wait()

    return pl.pallas_call(
        body, name="reduce_chips", out_shape=[sds((3,) + q.shape[1:], BF16) for q in qs],
        in_specs=[ANY] * nw, out_specs=[ANY] * nw,
        scratch_shapes=[pltpu.SemaphoreType.DMA((nw, 3)), pltpu.SemaphoreType.DMA((nw, 3))],
    )(*qs)


def share_halves(gs):
    nw = len(gs)

    def body(*refs):
        outs = refs[nw:2 * nw]
        send_sems, recv_sems = refs[2 * nw:]
        x, y, c, _ = _place()
        sibling = (x, y, 1 - c)
        cps = []
        for w in range(nw):
            mine = _half(outs[w], gs[w].shape[0], c)
            cp = pltpu.make_async_remote_copy(
                src_ref=mine, dst_ref=mine, send_sem=send_sems.at[w], recv_sem=recv_sems.at[w],
                device_id=sibling, device_id_type=MESH)
            cp.start()
            cps.append(cp)
        for w, cp in enumerate(cps):
            cp.wait_send()
            theirs = _half(outs[w], gs[w].shape[0], 1 - c)
            pltpu.make_async_remote_copy(
                src_ref=theirs, dst_ref=theirs, send_sem=send_sems.at[w], recv_sem=recv_sems.at[w],
                device_id=sibling, device_id_type=MESH).wait_recv()

    return pl.pallas_call(
        body, name="share_halves", out_shape=[sds(g.shape, F32) for g in gs],
        in_specs=[ANY] * nw, out_specs=[ANY] * nw, input_output_aliases={w: w for w in range(nw)},
        scratch_shapes=[pltpu.SemaphoreType.DMA((nw,)), pltpu.SemaphoreType.DMA((nw,))],
    )(*gs)


def rope_tables(t):
    pos = jnp.arange(t, dtype=F32)
    inv_freq = ROPE_THETA ** (-jnp.arange(0, ROT, 2, dtype=F32) / ROT)
    ang = pos[:, None] * inv_freq[None, :]
    cos, sin = jnp.cos(ang), jnp.sin(ang)
    d = jnp.arange(128) % HD
    half = ROT // 2
    cs = jnp.take(cos, d % half, axis=1)
    sn = jnp.take(sin, d % half, axis=1)
    cc = jnp.where(d[None] < ROT, cs, 1.0)
    sa = jnp.where(d[None] < half, -sn, 0.0)
    sb = jnp.where((d[None] >= half) & (d[None] < ROT), sn, 0.0)
    return cc, sa, sb


def _rope(v, cc, sa, sb):
    w = v.shape[1]
    reps = w // 128
    half = ROT // 2
    return (v * jnp.tile(cc, (1, reps)) + pltpu.roll(v, w - half, 1) * jnp.tile(sa, (1, reps))
            + pltpu.roll(v, half, 1) * jnp.tile(sb, (1, reps)))


def _rope_t(dv, cc, sa, sb):
    w = dv.shape[1]
    reps = w // 128
    half = ROT // 2
    return (dv * jnp.tile(cc, (1, reps)) + pltpu.roll(dv * jnp.tile(sa, (1, reps)), half, 1)
            + pltpu.roll(dv * jnp.tile(sb, (1, reps)), w - half, 1))


def pool_fwd(h, b_in, t, tm):
    tm = min(tm, t)
    per = tm // HALO

    def body(prev_ref, cur_ref, b_ref, o_ref, xx):
        i = pl.program_id(0)
        b = b_ref[...]
        xx[pl.ds(0, HALO), :] = jnp.where(i > 0, prev_ref[...] + b, 0.0)
        xx[pl.ds(HALO, tm), :] = cur_ref[...] + b
        tpos = i * tm + lax.broadcasted_iota(jnp.int32, (tm, PG), 0) + 1
        for gi, w in enumerate(POOL_WINDOWS):
            cols = pl.ds(gi * PG, PG)
            acc = xx[pl.ds(HALO, tm), cols]
            for s in range(1, w):
                acc = acc + xx[pl.ds(HALO - s, tm), cols]
            cnt = jnp.minimum(tpos, w).astype(F32)
            o_ref[:, cols] = (acc / cnt - xx[pl.ds(HALO, tm), cols]).astype(o_ref.dtype)

    return pl.pallas_call(
        body, name="pool_fwd", grid=(t // tm,),
        in_specs=[pl.BlockSpec((HALO, PW), lambda i: (jnp.maximum(i * per - 1, 0), 0)),
                  pl.BlockSpec((tm, PW), lambda i: (i, 0)), pl.BlockSpec((1, PW), lambda i: (0, 0))],
        out_specs=pl.BlockSpec((tm, PW), lambda i: (i, 0)), out_shape=sds((t, PW), BF16),
        scratch_shapes=[pltpu.VMEM((tm + HALO, PW), F32)], compiler_params=_params(("arbitrary",)),
    )(h, h, b_in)


def pool_bwd(dpooled, t, tm):
    tm = min(tm, t)
    per = tm // HALO
    nt = t // tm

    def body(cur_ref, nxt_ref, o_ref, db_ref, ee):
        i = pl.program_id(0)
        tpos = i * tm + lax.broadcasted_iota(jnp.int32, (tm, PG), 0) + 1
        for gi, w in enumerate(POOL_WINDOWS):
            cols = pl.ds(gi * PG, PG)
            ee[pl.ds(0, tm), cols] = cur_ref[:, cols] / jnp.minimum(tpos, w).astype(F32)
            ee[pl.ds(tm, HALO), cols] = jnp.where(i < nt - 1, nxt_ref[:, cols] / float(w), 0.0)
        for gi, w in enumerate(POOL_WINDOWS):
            cols = pl.ds(gi * PG, PG)
            acc = ee[pl.ds(0, tm), cols]
            for s in range(1, w):
                acc = acc + ee[pl.ds(s, tm), cols]
            dxp = acc - cur_ref[:, cols]
            o_ref[:, cols] = dxp.astype(o_ref.dtype)
            part = colsum(dxp)

            @pl.when(i == 0)
            def _(cols=cols, part=part):
                db_ref[:, cols] = part

            @pl.when(i > 0)
            def _(cols=cols, part=part):
                db_ref[:, cols] += part

    return pl.pallas_call(
        body, name="pool_bwd", grid=(nt,),
        in_specs=[pl.BlockSpec((tm, PW), lambda i: (i, 0)),
                  pl.BlockSpec((HALO, PW), lambda i: (jnp.minimum((i + 1) * per, t // HALO - 1), 0))],
        out_specs=[pl.BlockSpec((tm, PW), lambda i: (i, 0)), pl.BlockSpec((1, PW), lambda i: (0, 0))],
        out_shape=[sds((t, PW), BF16), sds((1, PW), F32)],
        scratch_shapes=[pltpu.VMEM((tm + HALO, PW), F32)], compiler_params=_params(("arbitrary",)),
    )(dpooled, dpooled)


def _scores(qh, kp, kc, mask_p, mask_c, sink):
    sp = jnp.where(mask_p, lax.dot_general(qh, kp, NT, preferred_element_type=F32), -1e30)
    sc = jnp.where(mask_c, lax.dot_general(qh, kc, NT, preferred_element_type=F32), -1e30)
    m = jnp.maximum(jnp.maximum(jnp.max(sp, axis=-1, keepdims=True), jnp.max(sc, axis=-1, keepdims=True)), sink)
    pp, pc = jnp.exp(sp - m), jnp.exp(sc - m)
    es = jnp.exp(sink - m)
    den = jnp.sum(pp, axis=-1, keepdims=True) + jnp.sum(pc, axis=-1, keepdims=True) + es
    return pp / den, pc / den, es / den


def _masks(n):
    qi = lax.broadcasted_iota(jnp.int32, (BLK, BLK), 0)
    kj = lax.broadcasted_iota(jnp.int32, (BLK, BLK), 1)
    return (kj > qi) & (n > 0), kj <= qi


def attn_fwd(q, k, v, sinks, t):
    def body(s_ref, q_ref, kp_ref, kc_ref, vp_ref, vc_ref, o_ref):
        n = pl.program_id(0)
        mask_p, mask_c = _masks(n)
        for h in range(N_Q):
            kv = pl.ds(HD * (h // (N_Q // N_KV)), HD)
            hq = pl.ds(HD * h, HD)
            pp, pc, _ = _scores(q_ref[:, hq], kp_ref[:, kv], kc_ref[:, kv], mask_p, mask_c, s_ref[0, h])
            o = (lax.dot_general(pp.astype(BF16), vp_ref[:, kv], NN, preferred_element_type=F32)
                 + lax.dot_general(pc.astype(BF16), vc_ref[:, kv], NN, preferred_element_type=F32))
            o_ref[:, hq] = o.astype(o_ref.dtype)

    prev = lambda n: (jnp.maximum(n - 1, 0), 0)
    cur = lambda n: (n, 0)
    return pl.pallas_call(
        body, name="attn_fwd", grid=(t // BLK,),
        in_specs=[pl.BlockSpec(memory_space=pltpu.SMEM), pl.BlockSpec((BLK, QW), cur),
                  pl.BlockSpec((BLK, KVW), prev), pl.BlockSpec((BLK, KVW), cur),
                  pl.BlockSpec((BLK, KVW), prev), pl.BlockSpec((BLK, KVW), cur)],
        out_specs=pl.BlockSpec((BLK, QW), cur), out_shape=sds((t, QW), BF16),
        compiler_params=_params(("arbitrary",)),
    )(sinks, q, k, k, v, v)


def attn_bwd(q, k, v, do, sinks, t):
    nb = t // BLK
    grp = N_Q // N_KV

    def body(s_ref, q_ref, do_ref, kp_ref, kc_ref, vp_ref, vc_ref, dq_ref, dk_ref, dv_ref, ds_ref, dkc, dvc):
        n = pl.program_id(0)

        @pl.when(n == 0)
        def _():
            dkc[...] = jnp.zeros_like(dkc)
            dvc[...] = jnp.zeros_like(dvc)
            ds_ref[...] = jnp.zeros_like(ds_ref)

        @pl.when(n < nb)
        def _():
            mask_p, mask_c = _masks(n)
            lane = lax.broadcasted_iota(jnp.int32, (1, 128), 1)
            dsink = jnp.zeros((1, 128), F32)
            for hk in range(N_KV):
                kv = pl.ds(HD * hk, HD)
                kp, kc, vp, vc = kp_ref[:, kv], kc_ref[:, kv], vp_ref[:, kv], vc_ref[:, kv]
                dkp = jnp.zeros((BLK, HD), F32)
                dkn = jnp.zeros((BLK, HD), F32)
                dvp = jnp.zeros((BLK, HD), F32)
                dvn = jnp.zeros((BLK, HD), F32)
                for g in range(grp):
                    h = grp * hk + g
                    hq = pl.ds(HD * h, HD)
                    qh, doh = q_ref[:, hq], do_ref[:, hq]
                    pp, pc, ps = _scores(qh, kp, kc, mask_p, mask_c, s_ref[0, h])
                    dpp = lax.dot_general(doh, vp, NT, preferred_element_type=F32)
                    dpc = lax.dot_general(doh, vc, NT, preferred_element_type=F32)
                    delta = jnp.sum(pp * dpp, axis=-1, keepdims=True) + jnp.sum(pc * dpc, axis=-1, keepdims=True)
                    dsp = (pp * (dpp - delta)).astype(BF16)
                    dsc = (pc * (dpc - delta)).astype(BF16)
                    dsink = dsink + jnp.where(lane == h, -jnp.sum(ps * delta), 0.0)
                    dq_ref[:, hq] = (lax.dot_general(dsp, kp, NN, preferred_element_type=F32)
                                     + lax.dot_general(dsc, kc, NN, preferred_element_type=F32))
                    dkp = dkp + lax.dot_general(dsp, qh, TN, preferred_element_type=F32)
                    dkn = dkn + lax.dot_general(dsc, qh, TN, preferred_element_type=F32)
                    dvp = dvp + lax.dot_general(pp.astype(BF16), doh, TN, preferred_element_type=F32)
                    dvn = dvn + lax.dot_general(pc.astype(BF16), doh, TN, preferred_element_type=F32)
                dk_ref[:, kv] = dkc[:, kv] + dkp
                dv_ref[:, kv] = dvc[:, kv] + dvp
                dkc[:, kv] = dkn
                dvc[:, kv] = dvn
            ds_ref[...] += dsink

        @pl.when(n == nb)
        def _():
            dk_ref[...] = dkc[...]
            dv_ref[...] = dvc[...]

    cur = lambda n: (jnp.minimum(n, nb - 1), 0)
    prev = lambda n: (jnp.clip(n - 1, 0, nb - 1), 0)
    return pl.pallas_call(
        body, name="attn_bwd", grid=(nb + 1,),
        in_specs=[pl.BlockSpec(memory_space=pltpu.SMEM), pl.BlockSpec((BLK, QW), cur), pl.BlockSpec((BLK, QW), cur),
                  pl.BlockSpec((BLK, KVW), prev), pl.BlockSpec((BLK, KVW), cur),
                  pl.BlockSpec((BLK, KVW), prev), pl.BlockSpec((BLK, KVW), cur)],
        out_specs=[pl.BlockSpec((BLK, QW), cur), pl.BlockSpec((BLK, KVW), prev), pl.BlockSpec((BLK, KVW), prev),
                   pl.BlockSpec((1, 128), lambda n: (0, 0))],
        out_shape=[sds((t, QW), F32), sds((t, KVW), F32), sds((t, KVW), F32), sds((1, 128), F32)],
        scratch_shapes=[pltpu.VMEM((BLK, KVW), F32), pltpu.VMEM((BLK, KVW), F32)],
        compiler_params=_params(("arbitrary",)),
    )(sinks, q, do, k, k, v, v)


def _adamw(w, g, m, v):
    m2 = B1 * m + (1.0 - B1) * g
    v2 = B2 * v + (1.0 - B2) * jnp.square(g)
    m_hat = m2 / (1.0 - B1 ** STEP)
    v_hat = v2 / (1.0 - B2 ** STEP)
    return -LR * (m_hat / (jnp.sqrt(v_hat) + EPS) + WD * w), m2, v2


def ada_fwd(c16, w_ada, b_sh):
    tn = 512

    def body(c_ref, w_ref, b_ref, o_ref):
        cv = c_ref[...]
        sc = (cv * _sigmoid(cv)).astype(BF16)
        o_ref[...] = lax.dot_general(sc, w_ref[...].astype(BF16), NN, preferred_element_type=F32) + b_ref[...]

    return pl.pallas_call(
        body, name="ada_fwd", grid=(ADA_SH // tn,),
        in_specs=[pl.BlockSpec((16, D), lambda j: (0, 0)), pl.BlockSpec((D, tn), lambda j: (0, j)),
                  pl.BlockSpec((1, tn), lambda j: (0, j))],
        out_specs=pl.BlockSpec((16, tn), lambda j: (0, j)), out_shape=sds((16, ADA_SH), F32),
        compiler_params=_params(("arbitrary",)),
    )(c16, w_ada, b_sh)


def ada_bwd_adam(c16, gm16, w, m, v):
    tm, tn = 256, 512

    def body(c_ref, g_ref, w_ref, m_ref, v_ref, go_ref, d_ref, mo_ref, vo_ref):
        cv = c_ref[...]
        sc = (cv * _sigmoid(cv)).astype(BF16)
        g = lax.dot_general(sc, g_ref[...].astype(BF16), TN, preferred_element_type=F32)
        dl, m2, v2 = _adamw(w_ref[...], g, m_ref[...], v_ref[...])
        go_ref[...] = g
        d_ref[...] = dl
        mo_ref[...] = m2
        vo_ref[...] = v2

    blk = pl.BlockSpec((tm, tn), lambda i, j: (i, j))
    return pl.pallas_call(
        body, name="ada_bwd_adam", grid=(D // tm, ADA_SH // tn),
        in_specs=[pl.BlockSpec((16, tm), lambda i, j: (0, i)), pl.BlockSpec((16, tn), lambda i, j: (0, j)), blk, blk, blk],
        out_specs=[blk] * 4, out_shape=[sds((D, ADA_SH), F32)] * 4,
        compiler_params=_params(("arbitrary", "arbitrary")),
    )(c16, gm16, w, m, v)


def adam_rows(name, w, g, m, v, tm):
    rows, cols = w.shape

    def fn(wv, gv, mv, vv):
        gv = gv[:, :cols]
        dl, m2, v2 = _adamw(wv, gv, mv, vv)
        return gv, dl, m2, v2

    return rowmap(name, fn, [T_(w), T_(g), T_(m), T_(v)], [(cols, F32)] * 4, rows=rows, tm=tm)


def adam_small(name, w, g, m, v):
    def body(w_ref, g_ref, m_ref, v_ref, d_ref, mo_ref, vo_ref):
        dl, m2, v2 = _adamw(w_ref[...], g_ref[...], m_ref[...], v_ref[...])
        d_ref[...] = dl
        mo_ref[...] = m2
        vo_ref[...] = v2

    return pl.pallas_call(body, name=name, out_shape=[sds(w.shape, F32)] * 3)(w, g, m, v)


def sum_devices(allv):
    def body(a_ref, o_ref):
        acc = a_ref[0]
        for d in range(1, 8):
            acc = acc + a_ref[d]
        o_ref[...] = acc

    return pl.pallas_call(body, name="sum_devices", out_shape=sds(allv.shape[1:], F32))(allv)


def _ln_fwd(z, g, b):
    mu = jnp.mean(z, axis=-1, keepdims=True)
    zc = z - mu
    var = jnp.mean(jnp.square(zc), axis=-1, keepdims=True)
    return zc * lax.rsqrt(var + LN_EPS) * g + b


def _ln_bwd(z, g, dout):
    mu = jnp.mean(z, axis=-1, keepdims=True)
    zc = z - mu
    var = jnp.mean(jnp.square(zc), axis=-1, keepdims=True)
    rstd = lax.rsqrt(var + LN_EPS)
    xh = zc * rstd
    dxh = dout * g
    dz = rstd * (dxh - jnp.mean(dxh, axis=-1, keepdims=True) - xh * jnp.mean(dxh * xh, axis=-1, keepdims=True))
    return dz, colsum(dout * xh), colsum(dout)


def modulate(name, xin, shift, scale, t):
    return rowmap(name, lambda xv, sh, sc: xv * (1.0 + sc) + sh, [T_(xin), B_(shift), B_(scale)], [(D, BF16)],
                  rows=t, tm=512)


def residual_ln(name, xin, y, gate, lg, lb, wgt, t):
    def fn(xv, yv, gt, g, b):
        z = ALPHA * xv + (wgt * (1.0 + gt)) * yv
        return _ln_fwd(z, g, b), z

    return rowmap(name, fn, [T_(xin), T_(y), B_(gate), B_(lg), B_(lb)], [(D, F32), (D, F32)], rows=t, tm=256)


def residual_ln_bwd(name, z, dout, y, gate, lg, wgt, t):
    def fn(zv, dv, yv, gt, g):
        dz, dg, db = _ln_bwd(zv, g, dv)
        return dz, (wgt * (1.0 + gt)) * dz, dg, db, colsum(wgt * dz * yv)

    return rowmap(name, fn, [T_(z), T_(dout), T_(y), B_(gate), B_(lg)], [(D, F32), (D, BF16)],
                  [(1, D), (1, D), (1, D)], rows=t, tm=256)


def modulate_bwd(name, dz, du, xin, scale, t):
    def fn(dzv, duv, xv, sc):
        return ALPHA * dzv + duv * (1.0 + sc), colsum(duv), colsum(duv * xv)

    return rowmap(name, fn, [T_(dz), T_(du), T_(xin), B_(scale)], [(D, F32)], [(1, D), (1, D)], rows=t, tm=256)


def ffn_fwd(tag, xin, mod, lg, lb, wi, wo, t):
    tm = min(1024, t)
    tn = 256
    per = FHP // tn
    u = modulate(tag + "_mod", xin, mod[0], mod[1], t)

    def act(accs, _):
        a, b = accs
        return a, b, a * _sigmoid(a) * b

    hblk = pl.BlockSpec((tm, tn), lambda i, j, k: (i, j))
    ha, hb, g = mm(
        tag + "_up", [u], [wi, wi], [(0, 0, 0), (0, 1, 1)], dims=NN, grid=(t // tm, 2 * per, 1),
        a_specs=[pl.BlockSpec((tm, D), lambda i, j, k: (i, 0))],
        b_specs=[pl.BlockSpec((None, D, tn), lambda i, j, k: (j // per, 0, j % per)),
                 pl.BlockSpec((None, D, tn), lambda i, j, k: (2 + j // per, 0, j % per))],
        outs=[sds((t, 2 * FHP), BF16)] * 3, out_specs=[hblk] * 3, acc_shapes=[(tm, tn)] * 2, epilogue=act)
    tk = FHP // 2
    y = mm(
        tag + "_down", [g], [wo], [(0, 0, 0)], dims=NN, grid=(t // tm, 2, 4),
        a_specs=[pl.BlockSpec((tm, tk), lambda i, j, k: (i, k))],
        b_specs=[pl.BlockSpec((tk, D // 2), lambda i, j, k: (k, j))],
        outs=[sds((t, D), F32)], out_specs=[pl.BlockSpec((tm, D // 2), lambda i, j, k: (i, j))],
        acc_shapes=[(tm, D // 2)])
    xo, z = residual_ln(tag + "_ln", xin, y, mod[2], lg, lb, 0.5, t)
    return xo, (u, ha, hb, g, y, z)


def ffn_bwd(tag, xin, saved, dout, mod, lg, wi, wo, t):
    u, ha, hb, g, y, z = saved
    tm = min(1024, t)
    dz, dy, dlg, dlb, dgate = residual_ln_bwd(tag + "_ln_bwd", z, dout, y, mod[2], lg, 0.5, t)

    def dact(accs, ex):
        dg = accs[0]
        a, b = ex[0].astype(F32), ex[1].astype(F32)
        s = _sigmoid(a)
        return dg * b * (s * (1.0 + a * (1.0 - s))), dg * (a * s)

    tn = 256
    hblk = pl.BlockSpec((tm, tn), lambda i, j, k: (i, j))
    dha, dhb = mm(
        tag + "_dact", [dy], [wo], [(0, 0, 0)], dims=NT, grid=(t // tm, 2 * FHP // tn, 1),
        a_specs=[pl.BlockSpec((tm, D), lambda i, j, k: (i, 0))],
        b_specs=[pl.BlockSpec((tn, D), lambda i, j, k: (j, 0))],
        outs=[sds((t, 2 * FHP), BF16)] * 2, out_specs=[hblk] * 2, acc_shapes=[(tm, tn)],
        epilogue=dact, extras=[ha, hb], extra_specs=[hblk] * 2)
    tk = min(1024, t)
    th = FHP // 2
    dwo = mm(
        tag + "_dwo", [g], [dy], [(0, 0, 0)], dims=TN, grid=(4, 2, t // tk),
        a_specs=[pl.BlockSpec((tk, th), lambda i, j, k: (k, i))],
        b_specs=[pl.BlockSpec((tk, D // 2), lambda i, j, k: (k, j))],
        outs=[sds((2 * FHP, D), BF16)], out_specs=[pl.BlockSpec((th, D // 2), lambda i, j, k: (i, j))],
        acc_shapes=[(th, D // 2)])
    dwi = None
    for part, dh in enumerate((dha, dhb)):
        dwi = mm(
            f"{tag}_dwi{part}", [u], [dh], [(0, 0, 0)], dims=TN, grid=(2, 4, t // tk),
            a_specs=[pl.BlockSpec((tk, D // 2), lambda i, j, k: (k, i))],
            b_specs=[pl.BlockSpec((tk, th), lambda i, j, k: (k, j))],
            outs=[sds((4, D, FHP), BF16)],
            out_specs=[pl.BlockSpec((None, D // 2, th), lambda i, j, k, part=part: (2 * part + j // 2, i, j % 2))],
            acc_shapes=[(D // 2, th)], carry=dwi)
    du = mm(
        tag + "_du", [dha, dhb], [wi, wi], [(0, 0, 0), (1, 1, 0)], dims=NT, grid=(t // tm, 2, 4),
        a_specs=[pl.BlockSpec((tm, th), lambda i, j, k: (i, k))] * 2,
        b_specs=[pl.BlockSpec((None, D // 2, th), lambda i, j, k: (k // 2, j, k % 2)),
                 pl.BlockSpec((None, D // 2, th), lambda i, j, k: (2 + k // 2, j, k % 2))],
        outs=[sds((t, D), F32)], out_specs=[pl.BlockSpec((tm, D // 2), lambda i, j, k: (i, j))],
        acc_shapes=[(tm, D // 2)])
    dx, dshift, dscale = modulate_bwd(tag + "_mod_bwd", dz, du, xin, mod[1], t)
    return dx, dwi, dwo.reshape(2, FHP, D), (dshift, dscale, dgate), dlg, dlb


def mix_fwd(xin, mod, lg, lb, wts, b_in, pool_scale, sinks, tabs, t):
    w_in, wp, wba, wbb, wo = wts
    tm = min(1024, t)
    u = modulate("mix_mod", xin, mod[0], mod[1], t)
    tmh = min(512, t)
    h = mm("mix_in", [u], [w_in], [(0, 0, 0)], dims=NN, grid=(t // tmh, 4, 1),
           a_specs=[pl.BlockSpec((tmh, D), lambda i, j, k: (i, 0))],
           b_specs=[pl.BlockSpec((None, D, IN_SH), lambda i, j, k: (j, 0, 0))],
           outs=[sds((t, IN_W), F32)], out_specs=[pl.BlockSpec((tmh, IN_SH), lambda i, j, k: (i, j))],
           acc_shapes=[(tmh, IN_SH)])
    pooled = pool_fwd(h, b_in, t, 512)
    gblk = pl.BlockSpec((tm, PG), lambda i, j, k: (i, j))
    mixed = mm("mix_pool", [pooled], [wp], [(0, 0, 0)], dims=NN, grid=(t // tm, 4, 1), a_specs=[gblk],
               b_specs=[pl.BlockSpec((None, PG, PG), lambda i, j, k: (j, 0, 0))],
               outs=[sds((t, PW), F32)], out_specs=[gblk], acc_shapes=[(tm, PG)])
    pm = rowmap("mix_pscale", lambda mv, ps: mv * ps, [T_(mixed), B_(pool_scale)], [(PW, BF16)], rows=t, tm=512)

    def branch(name, a, w):
        return mm(name, [a], [w], [(0, 0, 0)], dims=NN, grid=(t // tm, 4, 1),
                  a_specs=[pl.BlockSpec((tm, PW), lambda i, j, k: (i, 0))],
                  b_specs=[pl.BlockSpec((None, PW, D // 4), lambda i, j, k: (j, 0, 0))],
                  outs=[sds((t, D), F32)], out_specs=[pl.BlockSpec((tm, D // 4), lambda i, j, k: (i, j))],
                  acc_shapes=[(tm, D // 4)])

    ya = branch("mix_branch_a", pm, wba)

    def qkv(hq, hk, hv, bq, bk, bv, cc, sa, sb):
        return (_rope(hq + bq, cc, sa, sb) * (HD ** -0.5), _rope(hk + bk, cc, sa, sb), hv + bv)

    qr, kr, vv = rowmap(
        "mix_rope", qkv,
        [T_(h, QW, 1), T_(h, KVW, 8), T_(h, KVW, 9), B_(b_in, QW, 1), B_(b_in, KVW, 8), B_(b_in, KVW, 9),
         T_(tabs[0]), T_(tabs[1]), T_(tabs[2])],
        [(QW, BF16), (KVW, BF16), (KVW, BF16)], rows=t, tm=512)
    attn = attn_fwd(qr, kr, vv, sinks, t)
    yb = branch("mix_branch_b", attn, wbb)
    cw = 512

    def merge(ga, gb, ba, bb, yav, ybv):
        return _sigmoid(ga + ba) * yav + _sigmoid(gb + bb) * ybv

    merged = rowmap(
        "mix_merge", merge,
        [T_(h, cw, 5), T_(h, cw, 9), B_(b_in, cw, 5), B_(b_in, cw, 9), T_(ya, cw), T_(yb, cw)],
        [(D, BF16)], rows=t, tm=512, ncol=D // cw)
    y = mm("mix_out", [merged], [wo], [(0, 0, 0)], dims=NN, grid=(t // tm, 2, 1),
           a_specs=[pl.BlockSpec((tm, D), lambda i, j, k: (i, 0))],
           b_specs=[pl.BlockSpec((D, D // 2), lambda i, j, k: (0, j))],
           outs=[sds((t, D), F32)], out_specs=[pl.BlockSpec((tm, D // 2), lambda i, j, k: (i, j))],
           acc_shapes=[(tm, D // 2)])
    xo, z = residual_ln("mix_ln", xin, y, mod[2], lg, lb, 1.0, t)
    return xo, (u, h, pooled, mixed, pm, ya, qr, kr, vv, attn, yb, merged, y, z)


def mix_bwd(xin, saved, dout, mod, lg, wts, b_in, pool_scale, sinks, tabs, t):
    u, h, pooled, mixed, pm, ya, qr, kr, vv, attn, yb, merged, y, z = saved
    w_in, wp, wba, wbb, wo = wts
    tm = min(1024, t)
    tk = min(1024, t)
    dz, dy, dlg, dlb, dgate = residual_ln_bwd("mix_ln_bwd", z, dout, y, mod[2], lg, 1.0, t)
    dmerged = mm("mix_dmerged", [dy], [wo], [(0, 0, 0)], dims=NT, grid=(t // tm, 2, 1),
                 a_specs=[pl.BlockSpec((tm, D), lambda i, j, k: (i, 0))],
                 b_specs=[pl.BlockSpec((D // 2, D), lambda i, j, k: (j, 0))],
                 outs=[sds((t, D), F32)], out_specs=[pl.BlockSpec((tm, D // 2), lambda i, j, k: (i, j))],
                 acc_shapes=[(tm, D // 2)])
    half = pl.BlockSpec((tk, D // 2), lambda i, j, k: (k, i))
    dwo = mm("mix_dwo", [merged], [dy], [(0, 0, 0)], dims=TN, grid=(2, 2, t // tk), a_specs=[half],
             b_specs=[pl.BlockSpec((tk, D // 2), lambda i, j, k: (k, j))],
             outs=[sds((D, D), BF16)], out_specs=[pl.BlockSpec((D // 2, D // 2), lambda i, j, k: (i, j))],
             acc_shapes=[(D // 2, D // 2)])
    cw = 512

    def dmerge(dm, ga, gb, ba, bb, yav, ybv):
        sa_, sb_ = _sigmoid(ga + ba), _sigmoid(gb + bb)
        dga = dm * yav * sa_ * (1.0 - sa_)
        dgb = dm * ybv * sb_ * (1.0 - sb_)
        return dm * sa_, dm * sb_, dga, dgb, colsum(dga), colsum(dgb)

    dya, dyb, dgla, dglb, dbga, dbgb = rowmap(
        "mix_dmerge", dmerge,
        [T_(dmerged, cw), T_(h, cw, 5), T_(h, cw, 9), B_(b_in, cw, 5), B_(b_in, cw, 9), T_(ya, cw), T_(yb, cw)],
        [(D, BF16)] * 4, [(1, D), (1, D)], rows=t, tm=512, ncol=D // cw)

    def dbranch(name, dyv, act, w):
        dwb = mm(name + "_dw", [act], [dyv], [(0, 0, 0)], dims=TN, grid=(1, 4, t // tk),
                 a_specs=[pl.BlockSpec((tk, PW), lambda i, j, k: (k, 0))],
                 b_specs=[pl.BlockSpec((tk, D // 4), lambda i, j, k: (k, j))],
                 outs=[sds((4, PW, D // 4), BF16)], out_specs=[pl.BlockSpec((None, PW, D // 4), lambda i, j, k: (j, 0, 0))],
                 acc_shapes=[(PW, D // 4)])
        return dwb, lambda dt: mm(
            name + "_dx", [dyv], [w], [(0, 0, 0)], dims=NT, grid=(t // tm, 1, 4),
            a_specs=[pl.BlockSpec((tm, D // 4), lambda i, j, k: (i, k))],
            b_specs=[pl.BlockSpec((None, PW, D // 4), lambda i, j, k: (k, 0, 0))],
            outs=[sds((t, PW), dt)], out_specs=[pl.BlockSpec((tm, PW), lambda i, j, k: (i, 0))], acc_shapes=[(tm, PW)])

    dwba, dpm_fn = dbranch("mix_dbranch_a", dya, pm, wba)
    dwbb, dattn_fn = dbranch("mix_dbranch_b", dyb, attn, wbb)
    dpm, dattn = dpm_fn(F32), dattn_fn(BF16)
    dmixed, dps = rowmap("mix_dpscale", lambda dp, mv, ps: (dp * ps, colsum(dp * mv)),
                         [T_(dpm), T_(mixed), B_(pool_scale)], [(PW, BF16)], [(1, PW)], rows=t, tm=512)
    gblk = pl.BlockSpec((tm, PG), lambda i, j, k: (i, j))
    dpooled = mm("mix_dpool", [dmixed], [wp], [(0, 0, 0)], dims=NT, grid=(t // tm, 4, 1), a_specs=[gblk],
                 b_specs=[pl.BlockSpec((None, PG, PG), lambda i, j, k: (j, 0, 0))],
                 outs=[sds((t, PW), F32)], out_specs=[gblk], acc_shapes=[(tm, PG)])
    kblk = pl.BlockSpec((tk, PG), lambda i, j, k: (k, i))
    dwp = mm("mix_dwpool", [pooled], [dmixed], [(0, 0, 0)], dims=TN, grid=(4, 1, t // tk), a_specs=[kblk], b_specs=[kblk],
             outs=[sds((4, PG, PG), BF16)], out_specs=[pl.BlockSpec((None, PG, PG), lambda i, j, k: (i, 0, 0))],
             acc_shapes=[(PG, PG)])
    dxp, dbxp = pool_bwd(dpooled, t, 512)
    dqr, dkr, dvv, dsinks = attn_bwd(qr, kr, vv, dattn, sinks, t)

    def dqkv(dq, dk, dv, cc, sa, sb):
        dq = _rope_t(dq, cc, sa, sb) * (HD ** -0.5)
        dk = _rope_t(dk, cc, sa, sb)
        return dq, dk, dv, colsum(dq), colsum(dk), colsum(dv)

    dq, dk, dvb, dbq, dbk, dbv = rowmap(
        "mix_rope_bwd", dqkv, [T_(dqr), T_(dkr), T_(dvv), T_(tabs[0]), T_(tabs[1]), T_(tabs[2])],
        [(QW, BF16), (KVW, BF16), (KVW, BF16)], [(1, QW), (1, KVW), (1, KVW)], rows=t, tm=512)
    dh = jnp.concatenate([dxp, dq, dk, dvb, dgla, dglb], axis=1)
    db_in = jnp.concatenate([dbxp, dbq, dbk, dbv, dbga, dbgb], axis=1)
    dwin = mm("mix_dwin", [u], [dh], [(0, 0, 0)], dims=TN, grid=(2, 4, t // tk), a_specs=[half],
              b_specs=[pl.BlockSpec((tk, IN_SH), lambda i, j, k: (k, j))],
              outs=[sds((4, D, IN_SH), BF16)], out_specs=[pl.BlockSpec((None, D // 2, IN_SH), lambda i, j, k: (j, i, 0))],
              acc_shapes=[(D // 2, IN_SH)])
    du = mm("mix_du", [dh], [w_in], [(0, 0, 0)], dims=NT, grid=(t // tm, 2, 4),
            a_specs=[pl.BlockSpec((tm, IN_SH), lambda i, j, k: (i, k))],
            b_specs=[pl.BlockSpec((None, D // 2, IN_SH), lambda i, j, k: (k, j, 0))],
            outs=[sds((t, D), F32)], out_specs=[pl.BlockSpec((tm, D // 2), lambda i, j, k: (i, j))],
            acc_shapes=[(tm, D // 2)])
    dx, dshift, dscale = modulate_bwd("mix_mod_bwd", dz, du, xin, mod[1], t)
    return dx, (dwin, dwp, dwba, dwbb, dwo), (dshift, dscale, dgate), dlg, dlb, db_in, dps, dsinks


def cast_shard(name, w, sp, pad=0, ffn_out=False):
    rows, cols = w.shape

    def fn(wv):
        wb = wv.astype(BF16)
        return jnp.concatenate([wb, jnp.zeros((wb.shape[0], pad), BF16)], axis=1) if pad else wb

    if ffn_out:
        tm = rows // 2
        shape = (2, FHP, D)
        spec = pl.BlockSpec((None, tm, cols), lambda j, i, s: (s[0] // 2, (s[0] % 2) * 2 + i, 0))
    else:
        tm = rows // 4
        shape = (4, rows, cols + pad)
        spec = pl.BlockSpec((None, tm, cols + pad), lambda j, i, s: (s[0], i, 0))
    return rowmap(name, fn, [T_(w)], [(shape, BF16, spec)], rows=rows, tm=tm, sp=sp)


def chip_sum(name, dw, got, sp, rows, tm, ffn_out=False):
    hr, cols = rows // 2, got.shape[2]
    per = hr // tm
    pos = pl.BlockSpec((None, tm, cols), lambda j, i, s: (i // per, i % per, 0))
    if ffn_out:
        mine = pl.BlockSpec((None, tm, cols), lambda j, i, s: (i // 2, (i % 2) * 2 + s[1], 0))
    else:
        mine = pl.BlockSpec((None, tm, cols), lambda j, i, s: (i // per, s[1] * per + i % per, 0))
    return rowmap(name, lambda av, bv: av.astype(F32) + bv.astype(F32), [X_(dw, mine), X_(got, pos)],
                  [(got.shape, BF16, pos)], rows=4 * hr, tm=tm, sp=sp)


def chip_total(name, q, got, sp, rows, tm):
    hr, cols = rows // 2, q.shape[2]
    per = hr // tm

    def part(f):
        return X_(got, pl.BlockSpec((None, tm, cols), lambda j, i, s, f=f: (f, i, 0)))

    return rowmap(
        name, lambda av, b0, b1, b2: ((av.astype(F32) + b0.astype(F32)) + b1.astype(F32)) + b2.astype(F32),
        [X_(q, pl.BlockSpec((None, tm, cols), lambda j, i, s: (s[0], i, 0))), part(0), part(1), part(2)],
        [((rows, cols), F32, pl.BlockSpec((tm, cols), lambda j, i, s: (s[1] * per + i, 0)))], rows=hr, tm=tm, sp=sp)


def kernel(x, c, w_ada, b_ada, ln_g, ln_b, w_ffn1_in, w_ffn1_out, w_in, b_in, w_pool, pool_scale, sinks, w_branch_a, w_branch_b, w_out, w_ffn2_in, w_ffn2_out, loss_target, m_w_ada, m_b_ada, m_ln_g, m_ln_b, m_w_ffn1_in, m_w_ffn1_out, m_w_in, m_b_in, m_w_pool, m_pool_scale, m_sinks, m_w_branch_a, m_w_branch_b, m_w_out, m_w_ffn2_in, m_w_ffn2_out, v_w_ada, v_b_ada, v_ln_g, v_ln_b, v_w_ffn1_in, v_w_ffn1_out, v_w_in, v_b_in, v_w_pool, v_pool_scale, v_sinks, v_w_branch_a, v_w_branch_b, v_w_out, v_w_ffn2_in, v_w_ffn2_out):
    t = x.shape[1]
    xs, tgt = x[0], loss_target[0]
    xi, yi, ci = lax.axis_index("x"), lax.axis_index("y"), lax.axis_index("c")
    chip = 2 * xi + yi
    dev = 2 * chip + ci
    b_in2, ps2, sinks2 = b_in, pool_scale, sinks

    first = jnp.concatenate([c.reshape(-1), ln_g.reshape(-1), ln_b.reshape(-1)]).reshape(-1, 128)
    first_all = allgather_small("gather_cond", first).reshape(8, -1)
    c_all = first_all[:, :D]
    ln_parts = first_all[0::2, D:].reshape(4, 2, 3, D // 4)
    ln_full = jnp.transpose(ln_parts, (1, 2, 0, 3)).reshape(2, 3, D)
    lgs = [ln_full[0, s:s + 1] for s in range(3)]
    lbs = [ln_full[1, s:s + 1] for s in range(3)]
    c16 = jnp.pad(c_all, ((0, 8), (0, 0)))
    b_ada_sh = lax.dynamic_slice(b_ada, (0, chip * ADA_SH), (1, ADA_SH))
    mod_part = ada_fwd(c16, w_ada[0], b_ada_sh)[:8]
    mod_all = allgather_small("gather_mod", mod_part.reshape(-1, 128)).reshape(8, 8, ADA_SH)
    mod_mine = lax.dynamic_index_in_dim(mod_all[0::2], dev, axis=1, keepdims=False).reshape(9, D)
    mods = [[mod_mine[3 * s + k:3 * s + k + 1] for k in range(3)] for s in range(3)]

    plain = [("f1o", w_ffn1_out[0]), ("win", w_in[0]), ("wp", w_pool[0].reshape(4 * 64, PG)), ("wba", w_branch_a[0]),
             ("wbb", w_branch_b[0]), ("wo", w_out[0]), ("f2o", w_ffn2_out[0])]
    sp = jnp.stack([chip, ci]).astype(jnp.int32)
    sh = {n: cast_shard("cast_" + n, w, sp, ffn_out=n in ("f1o", "f2o")) for n, w in plain}
    sh["f1i"] = cast_shard("cast_f1i", w_ffn1_in[0], sp, FHP - FH)
    sh["f2i"] = cast_shard("cast_f2i", w_ffn2_in[0], sp, FHP - FH)
    order = ["f1i", "f1o", "win", "wp", "wba", "wbb", "wo", "f2i", "f2o"]
    views = {n: (view_ffn_out if n in ("f1o", "f2o") else view_lead) for n in order}
    shard_rows = {n: (FO if n in ("f1o", "f2o") else sh[n].shape[1]) for n in order}
    shard_cols = {n: sh[n].shape[2] for n in order}
    gath = dict(zip(order, gather_weights([(sh[n], views[n], shard_rows[n]) for n in order])))
    wp_full = jnp.transpose(gath["wp"].reshape(4, 4, 64, PG), (1, 0, 2, 3)).reshape(4, PG, PG)
    wts = (gath["win"], wp_full, gath["wba"], gath["wbb"], gath["wo"].reshape(D, D))
    f1o, f2o = gath["f1o"].reshape(2 * FHP, D), gath["f2o"].reshape(2 * FHP, D)
    tabs = rope_tables(t)

    x1, sv1 = ffn_fwd("ffn1", xs, mods[0], lgs[0], lbs[0], gath["f1i"], f1o, t)
    x2, sv2 = mix_fwd(x1, mods[1], lgs[1], lbs[1], wts, b_in2, ps2, sinks2, tabs, t)
    x3, sv3 = ffn_fwd("ffn2", x2, mods[2], lgs[2], lbs[2], gath["f2i"], f2o, t)

    def lossfn(xv, tv):
        d = xv - tv
        return d * (1.0 / D), jnp.sum(d * d).reshape(1, 1)

    dx3, lsum = rowmap("loss", lossfn, [T_(x3), T_(tgt)], [(D, F32)], [(1, 1)], rows=t, tm=512)
    loss = lax.psum(0.5 * lsum[0, 0] / D, ("x", "y", "c"))

    dx2, dw_f2i, dw_f2o, gm2, dlg2, dlb2 = ffn_bwd("ffn2", x2, sv3, dx3, mods[2], lgs[2], gath["f2i"], f2o, t)
    dx1, dmix, gm1, dlg1, dlb1, db_in, dps, dsinks = mix_bwd(x1, sv2, dx2, mods[1], lgs[1], wts, b_in2, ps2, sinks2, tabs, t)
    dx0, dw_f1i, dw_f1o, gm0, dlg0, dlb0 = ffn_bwd("ffn1", xs, sv1, dx1, mods[0], lgs[0], gath["f1i"], f1o, t)
    dwin, dwp, dwba, dwbb, dwo = dmix

    small = jnp.concatenate([*gm0, *gm1, *gm2, dlg0, dlg1, dlg2, dlb0, dlb1, dlb2, db_in, dps, dsinks], axis=1)
    n_small = small.shape[1]
    rows_small = -(-n_small // 1024) * 8
    small = jnp.pad(small, ((0, 0), (0, rows_small * 128 - n_small))).reshape(rows_small, 128)
    small_all = allgather_small("gather_small", small)
    tot = sum_devices(small_all).reshape(1, -1)
    gmod_all = small_all.reshape(8, -1)[:, :9 * D]
    o = 9 * D
    g_b_ada = tot[:, :o]
    g_ln_g = lax.dynamic_slice(tot[:, o:o + 3 * D].reshape(3, D), (0, chip * (D // 4)), (3, D // 4))
    g_ln_b = lax.dynamic_slice(tot[:, o + 3 * D:o + 6 * D].reshape(3, D), (0, chip * (D // 4)), (3, D // 4))
    o += 6 * D
    g_b_in, g_ps, g_sinks = tot[:, o:o + IN_W], tot[:, o + IN_W:o + IN_W + PW], tot[:, o + IN_W + PW:o + IN_W + PW + N_Q]

    gm16 = jnp.pad(lax.dynamic_slice(gmod_all, (0, chip * ADA_SH), (8, ADA_SH)), ((0, 8), (0, 0)))
    g_w_ada, d_w_ada, nm_w_ada, nv_w_ada = ada_bwd_adam(c16, gm16, w_ada[0], m_w_ada[0], v_w_ada[0])

    dwp_sh = jnp.transpose(dwp.reshape(4, 4, 64, PG), (1, 0, 2, 3)).reshape(4, 4 * 64, PG)
    parts = {"f1i": dw_f1i, "f1o": dw_f1o, "win": dwin, "wp": dwp_sh, "wba": dwba, "wbb": dwbb,
             "wo": dwo.reshape(4, D // 4, D), "f2i": dw_f2i, "f2o": dw_f2o}
    got_a = reduce_sibling([(parts[n], views[n], shard_rows[n], shard_cols[n]) for n in order])
    tiles = {"f1i": 512, "f1o": FO // 2, "win": 512, "wp": 128, "wba": 512, "wbb": 512, "wo": 256, "f2i": 512, "f2o": FO // 2}
    qs = [chip_sum("chipsum_" + n, parts[n], g, sp, shard_rows[n], tiles[n], ffn_out=n in ("f1o", "f2o"))
          for n, g in zip(order, got_a)]
    got_c = reduce_chips(qs)
    halves = [chip_total("total_" + n, q, g, sp, shard_rows[n], tiles[n]) for n, q, g in zip(order, qs, got_c)]
    gw = dict(zip(order, share_halves(halves)))

    def big(n, w, m, v, tm):
        shape = w.shape
        w2, m2, v2 = (a.reshape(shape[-2] if a.ndim == 3 else -1, shape[-1]) for a in (w, m, v))
        return [r.reshape(shape) for r in adam_rows("adam_" + n, w2, gw[n], m2, v2, tm)]

    def tiny(n, w, g, m, v):
        return [g.reshape(w.shape)] + list(adam_small("adam_" + n, w, g.reshape(w.shape), m, v))

    res = {
        "w_ada": [a[None] for a in (g_w_ada, d_w_ada, nm_w_ada, nv_w_ada)],
        "b_ada": tiny("b_ada", b_ada, g_b_ada, m_b_ada, v_b_ada),
        "ln_g": tiny("ln_g", ln_g, g_ln_g, m_ln_g, v_ln_g),
        "ln_b": tiny("ln_b", ln_b, g_ln_b, m_ln_b, v_ln_b),
        "w_ffn1_in": big("f1i", w_ffn1_in, m_w_ffn1_in, v_w_ffn1_in, 128),
        "w_ffn1_out": big("f1o", w_ffn1_out, m_w_ffn1_out, v_w_ffn1_out, 32),
        "w_in": big("win", w_in, m_w_in, v_w_in, 256),
        "b_in": tiny("b_in", b_in, g_b_in, m_b_in, v_b_in),
        "w_pool": big("wp", w_pool, m_w_pool, v_w_pool, 256),
        "pool_scale": tiny("pool_scale", pool_scale, g_ps, m_pool_scale, v_pool_scale),
        "sinks": tiny("sinks", sinks, g_sinks, m_sinks, v_sinks),
        "w_branch_a": big("wba", w_branch_a, m_w_branch_a, v_w_branch_a, 512),
        "w_branch_b": big("wbb", w_branch_b, m_w_branch_b, v_w_branch_b, 512),
        "w_out": big("wo", w_out, m_w_out, v_w_out, 128),
        "w_ffn2_in": big("f2i", w_ffn2_in, m_w_ffn2_in, v_w_ffn2_in, 128),
        "w_ffn2_out": big("f2o", w_ffn2_out, m_w_ffn2_out, v_w_ffn2_out, 32),
    }
    names = ["w_ada", "b_ada", "ln_g", "ln_b", "w_ffn1_in", "w_ffn1_out", "w_in", "b_in", "w_pool", "pool_scale", "sinks",
             "w_branch_a", "w_branch_b", "w_out", "w_ffn2_in", "w_ffn2_out"]
    return (loss, dx0[None], *[res[n][0] for n in names], *[res[n][1] for n in names],
            *[res[n][2] for n in names], *[res[n][3] for n in names])
```

```python
import jax
import jax.numpy as jnp
from jax import lax
from jax.experimental import pallas as pl
from jax.experimental.pallas import tpu as pltpu

F32 = jnp.float32
BF16 = jnp.bfloat16
MESH = pl.DeviceIdType.MESH
ANY = pl.BlockSpec(memory_space=pl.ANY)

D = 2048
N_Q, N_KV, HD = 16, 4, 64
QW, KVW = N_Q * HD, N_KV * HD
BLK = 128
POOL_WINDOWS = (2, 4, 8, 16)
PW, PG = 1024, 256
HALO = 16
ROPE_THETA = 500000.0
ROT = HD // 4
LN_EPS = 1e-5
ALPHA = 2.0 ** 0.25
FH = 2752
FHP = 2816
FO = 1376
IN_W = 6656
IN_SH = IN_W // 4
ADA_SH = 18432 // 4
B1, B2, LR, EPS, WD, STEP = 0.9, 0.999, 0.001, 1e-08, 0.01, 10
VMEM_LIMIT = 56 * 1024 * 1024
FLIPS = ((1, 0), (0, 1), (1, 1))
NN = (((1,), (0,)), ((), ()))
NT = (((1,), (1,)), ((), ()))
TN = (((0,), (0,)), ((), ()))


def _params(sem):
    return pltpu.CompilerParams(dimension_semantics=sem, vmem_limit_bytes=VMEM_LIMIT)


def _aligned(v, m):
    return v if isinstance(v, int) else pl.multiple_of(v, m)


def _sigmoid(v):
    return 1.0 / (1.0 + jnp.exp(-v))


def T_(arr, width=None, off=0):
    return ("t", arr, width, off)


def B_(arr, width=None, off=0):
    return ("b", arr, width, off)


def X_(arr, spec):
    return ("x", arr, spec, 0)


def rowmap(name, fn, ins, outs, accs=(), *, rows, tm, ncol=1, with_ids=False, sp=None, alias=None):
    tm = min(tm, rows)
    nrow = rows // tm
    in_specs, arrs = [], []
    for kind, arr, width, off in ins:
        if kind == "x":
            in_specs.append(width)
        elif kind == "t":
            w = arr.shape[1] if width is None else width
            in_specs.append(pl.BlockSpec((tm, w), lambda j, i, *_, off=off: (i, off + j)))
        else:
            w = arr.shape[1] if width is None else width
            in_specs.append(pl.BlockSpec((arr.shape[0], w), lambda j, i, *_, off=off: (0, off + j)))
        arrs.append(arr)
    out_shape, out_specs = [], []
    for o in outs:
        if len(o) == 3:
            out_shape.append(jax.ShapeDtypeStruct(o[0], o[1]))
            out_specs.append(o[2])
        else:
            out_shape.append(jax.ShapeDtypeStruct((rows, o[0]), o[1]))
            out_specs.append(pl.BlockSpec((tm, o[0] // ncol), lambda j, i, *_: (i, j)))
    for r, width in accs:
        out_shape.append(jax.ShapeDtypeStruct((r, width), F32))
        out_specs.append(pl.BlockSpec((r, width // ncol), lambda j, i, *_: (0, j)))
    ni, no = len(ins), len(outs)
    nsp = 0 if sp is None else 1

    def body(*refs):
        refs = refs[nsp:]
        i = pl.program_id(1)
        vals = [r[...] for r in refs[:ni]]
        res = fn(pl.program_id(0), i, *vals) if with_ids else fn(*vals)
        if not isinstance(res, (tuple, list)):
            res = (res,)
        for r, v in zip(refs[ni:ni + no], res[:no]):
            r[...] = v.astype(r.dtype)
        for r, v in zip(refs[ni + no:], res[no:]):
            @pl.when(i == 0)
            def _(r=r, v=v):
                r[...] = v

            @pl.when(i > 0)
            def _(r=r, v=v):
                r[...] += v

    grid_spec = pltpu.PrefetchScalarGridSpec(num_scalar_prefetch=nsp, grid=(ncol, nrow), in_specs=in_specs,
                                             out_specs=out_specs)
    res = pl.pallas_call(
        body, name=name, grid_spec=grid_spec, out_shape=out_shape,
        input_output_aliases={nsp + k: v for k, v in (alias or {}).items()},
        compiler_params=_params(("arbitrary", "arbitrary")),
    )(*([sp] if nsp else []), *arrs)
    return res[0] if len(res) == 1 else res


def colsum(v):
    return jnp.sum(v, axis=0, keepdims=True)


def mm(name, a_ops, b_ops, ops, *, dims, grid, a_specs, b_specs, outs, out_specs, acc_shapes,
       epilogue=None, extras=(), extra_specs=(), carry=None, job=None):
    gk = grid[2]
    na, nb, ne, nacc = len(a_ops), len(b_ops), len(extras), len(acc_shapes)
    nc = 0 if carry is None else 1
    no = len(outs)

    def body(*refs):
        a_refs = refs[:na]
        b_refs = refs[na:na + nb]
        e_refs = refs[na + nb:na + nb + ne]
        o_refs = refs[na + nb + ne + nc:na + nb + ne + nc + no]
        acc_refs = refs[na + nb + ne + nc + no:]
        k = pl.program_id(2)

        def partials():
            res = [None] * nacc
            for ai, bi, ci in ops:
                p = lax.dot_general(a_refs[ai][...], b_refs[bi][...], dims, preferred_element_type=F32)
                res[ci] = p if res[ci] is None else res[ci] + p
            return res

        def finish(accs):
            outv = epilogue(accs, [e[...] for e in e_refs]) if epilogue else (accs[0],)
            for o, v in zip(o_refs, outv):
                o[...] = v.astype(o.dtype)

        if gk == 1:
            finish(partials())
        else:
            ps = partials()

            @pl.when(k == 0)
            def _():
                for acc, p in zip(acc_refs, ps):
                    acc[...] = p

            @pl.when(k > 0)
            def _():
                for acc, p in zip(acc_refs, ps):
                    acc[...] += p

            @pl.when(k == gk - 1)
            def _():
                finish([acc[...] for acc in acc_refs])

    res, moved = carried_call(
        body, name, grid,
        list(a_specs) + list(b_specs) + list(extra_specs) + ([ANY] if nc else []), list(out_specs), list(outs),
        [pltpu.VMEM(s, F32) for s in acc_shapes] if gk > 1 else [],
        [*a_ops, *b_ops, *extras, *([carry] if nc else [])], {na + nb + ne: 0} if nc else {}, job)
    res = res[0] if len(res) == 1 else res
    return res if job is None else (res, moved)


def sds(shape, dt):
    return jax.ShapeDtypeStruct(shape, dt)


class Job:
    def __init__(self, ins, outs, aliases, scratch, start, mid, finish):
        self.ins, self.outs, self.aliases, self.scratch = list(ins), list(outs), dict(aliases), list(scratch)
        self.start, self.mid, self.finish = start, mid, finish


def carried_call(body, name, grid, in_specs, out_specs, out_shape, scratch, args, aliases, job, mid_at=0.75):
    sem = ("arbitrary",) * len(grid)
    if job is None:
        res = pl.pallas_call(body, name=name, grid=grid, in_specs=in_specs, out_specs=out_specs, out_shape=out_shape,
                             scratch_shapes=scratch, input_output_aliases=aliases, compiler_params=_params(sem))(*args)
        return list(res), []
    ni, no, ns = len(in_specs), len(out_specs), len(scratch)
    ci, co = len(job.ins), len(job.outs)
    total = 1
    for g in grid:
        total *= g
    mid_step = min(max(int(total * mid_at), 1), total - 1)

    def full(*refs):
        ins, cins = refs[:ni], refs[ni:ni + ci]
        outs, couts = refs[ni + ci:ni + ci + no], refs[ni + ci + no:ni + ci + no + co]
        scr, cscr = refs[ni + ci + no + co:ni + ci + no + co + ns], refs[ni + ci + no + co + ns:]
        step = 0
        for d, g in enumerate(grid):
            step = step * g + pl.program_id(d)

        @pl.when(step == 0)
        def _():
            job.start(cins, couts, cscr)

        body(*ins, *outs, *scr)

        @pl.when(step == mid_step)
        def _():
            job.mid(cins, couts, cscr)

        @pl.when(step == total - 1)
        def _():
            job.finish(cins, couts, cscr)

    al = dict(aliases)
    al.update({ni + k: no + v for k, v in job.aliases.items()})
    res = pl.pallas_call(
        full, name=name, grid=grid, in_specs=in_specs + [ANY] * ci, out_specs=out_specs + [ANY] * co,
        out_shape=out_shape + job.outs, scratch_shapes=scratch + job.scratch, input_output_aliases=al,
        compiler_params=_params(sem))(*args, *job.ins)
    return list(res[:no]), list(res[no:])


def _with_moved(res, job):
    return res if job is not None else (res, [])


def run_job(name, job):
    ci = len(job.ins)

    def body(*refs):
        cins, couts, cscr = refs[:ci], refs[ci:ci + len(job.outs)], refs[ci + len(job.outs):]
        job.start(cins, couts, cscr)
        job.mid(cins, couts, cscr)
        job.finish(cins, couts, cscr)

    return list(pl.pallas_call(
        body, name=name, in_specs=[ANY] * ci, out_specs=[ANY] * len(job.outs), out_shape=job.outs,
        scratch_shapes=job.scratch, input_output_aliases=job.aliases)(*job.ins))


def _place():
    x, y, c = lax.axis_index("x"), lax.axis_index("y"), lax.axis_index("c")
    chips = [((1 - x) if fx else x, (1 - y) if fy else y) for fx, fy in FLIPS]
    return x, y, c, chips


def allgather_small(name, v):
    r = v.shape[0]

    def body(x_ref, out_ref, send_sems, recv_sems, local_sem):
        x, y, c, chips = _place()
        me, sibling = (x, y, c), (x, y, 1 - c)

        def rows(px, py, pc):
            return out_ref.at[4 * px + 2 * py + pc]

        def copy(k, block, to, src=None):
            return pltpu.make_async_remote_copy(
                src_ref=rows(*block) if src is None else src, dst_ref=rows(*block),
                send_sem=send_sems.at[k], recv_sem=recv_sems.at[k], device_id=to, device_id_type=MESH)

        mine = pltpu.make_async_copy(x_ref, rows(*me), local_sem)
        mine.start()
        first = [copy(0, me, sibling, src=x_ref)]
        first += [copy(1 + j, me, (*chip, c), src=x_ref) for j, chip in enumerate(chips)]
        for cp in first:
            cp.start()
        passed = [copy(4 + j, (*chip, c), sibling) for j, chip in enumerate(chips)]
        for j, chip in enumerate(chips):
            copy(1 + j, (*chip, c), me).wait_recv()
            passed[j].start()
        copy(0, sibling, me).wait_recv()
        for j, chip in enumerate(chips):
            copy(4 + j, (*chip, 1 - c), me).wait_recv()
        for cp in first + passed:
            cp.wait_send()
        mine.wait()

    return pl.pallas_call(
        body, name=name, out_shape=sds((8, r, 128), v.dtype),
        in_specs=[pl.BlockSpec(memory_space=pltpu.VMEM)], out_specs=pl.BlockSpec(memory_space=pltpu.VMEM),
        scratch_shapes=[pltpu.SemaphoreType.DMA((7,)), pltpu.SemaphoreType.DMA((7,)), pltpu.SemaphoreType.DMA],
    )(v)


def _half(ref, rows, hf):
    hr = rows // 2
    return ref.at[pl.ds(_aligned(hf * hr, 16), hr)]


def view_lead(ref, p):
    return ref.at[p]


def view_ffn_out(ref, p):
    return ref.at[p // 2, pl.ds(_aligned((p % 2) * FO, 16), FO)]


def _remote(ref, dst, send_sems, recv_sems, idx, to):
    return pltpu.make_async_remote_copy(src_ref=ref, dst_ref=dst, send_sem=send_sems.at[idx], recv_sem=recv_sems.at[idx],
                                        device_id=to, device_id_type=MESH)


def gather_job(items):
    nw = len(items)
    pads = [w for w, it in enumerate(items) if it[1] is view_ffn_out]

    def piece(ref, w, p, hf):
        _, view, rws, part, parts = items[w]
        pr = rws // 2 // parts
        return view(ref, p).at[pl.ds(_aligned(hf * (rws // 2) + part * pr, 16), pr)]

    def pad_copies(outs, scr):
        return [pltpu.make_async_copy(scr[2], outs[w].at[h, pl.ds(2 * FO, FHP - 2 * FO)], scr[3].at[2 * n + h])
                for n, w in enumerate(pads) for h in range(2)]

    def start(_, outs, scr):
        x, y, c, chips = _place()
        if pads:
            scr[2][...] = jnp.zeros_like(scr[2])
            for cp in pad_copies(outs, scr):
                cp.start()
        for w in range(nw):
            mine = piece(outs[w], w, 2 * x + y, c)
            for f, (px, py) in enumerate(chips):
                _remote(mine, mine, scr[0], scr[1], (w, f), (px, py, c)).start()

    def mid(_, outs, scr):
        x, y, c, chips = _place()
        for w in range(nw):
            for f, (px, py) in enumerate(chips):
                land = piece(outs[w], w, 2 * px + py, c)
                _remote(land, land, scr[0], scr[1], (w, f), (px, py, c)).wait_recv()
                _remote(land, land, scr[0], scr[1], (w, 3 + f), (x, y, 1 - c)).start()

    def finish(_, outs, scr):
        x, y, c, chips = _place()
        for w in range(nw):
            for f, (px, py) in enumerate(chips):
                land = piece(outs[w], w, 2 * px + py, 1 - c)
                _remote(land, land, scr[0], scr[1], (w, 3 + f), (x, y, 1 - c)).wait_recv()
        for w in range(nw):
            mine = piece(outs[w], w, 2 * x + y, c)
            for f in range(6):
                _remote(mine, mine, scr[0], scr[1], (w, f), (x, y, 1 - c)).wait_send()
        for cp in pad_copies(outs, scr):
            cp.wait()

    scratch = [pltpu.SemaphoreType.DMA((nw, 6)), pltpu.SemaphoreType.DMA((nw, 6))]
    if pads:
        scratch += [pltpu.VMEM((FHP - 2 * FO, D), BF16), pltpu.SemaphoreType.DMA((2 * len(pads),))]
    bufs = [it[0] for it in items]
    return Job(bufs, [sds(b.shape, BF16) for b in bufs], {w: w for w in range(nw)}, scratch, start, mid, finish)


def reduce_sibling_job(items):
    nw = len(items)

    def copies(ins, got, scr):
        x, y, c, _ = _place()
        return [_remote(_half(view(ins[w], p), rws, 1 - c), got[w].at[p], scr[0], scr[1], (w, p), (x, y, 1 - c))
                for w, (_, view, rws, _) in enumerate(items) for p in range(4)]

    def start(ins, got, scr):
        for cp in copies(ins, got, scr):
            cp.start()

    def finish(ins, got, scr):
        for cp in copies(ins, got, scr):
            cp.wait()

    return Job([it[0] for it in items], [sds((4, it[2] // 2, it[3]), BF16) for it in items], {},
               [pltpu.SemaphoreType.DMA((nw, 4)), pltpu.SemaphoreType.DMA((nw, 4))], start, lambda *_: None, finish)


def reduce_chips_job(qs):
    nw = len(qs)

    def copies(ins, got, scr):
        x, y, c, chips = _place()
        return [_remote(ins[w].at[2 * px + py], got[w].at[f], scr[0], scr[1], (w, f), (px, py, c))
                for w in range(nw) for f, (px, py) in enumerate(chips)]

    def start(ins, got, scr):
        for cp in copies(ins, got, scr):
            cp.start()

    def finish(ins, got, scr):
        for cp in copies(ins, got, scr):
            cp.wait()

    return Job(qs, [sds((3,) + q.shape[1:], BF16) for q in qs], {},
               [pltpu.SemaphoreType.DMA((nw, 3)), pltpu.SemaphoreType.DMA((nw, 3))], start, lambda *_: None, finish)


def share_halves_job(gs):
    nw = len(gs)

    def start(_, outs, scr):
        x, y, c, _ = _place()
        for w in range(nw):
            mine = _half(outs[w], gs[w].shape[0], c)
            _remote(mine, mine, scr[0], scr[1], w, (x, y, 1 - c)).start()

    def finish(_, outs, scr):
        x, y, c, _ = _place()
        for w in range(nw):
            mine = _half(outs[w], gs[w].shape[0], c)
            theirs = _half(outs[w], gs[w].shape[0], 1 - c)
            _remote(mine, mine, scr[0], scr[1], w, (x, y, 1 - c)).wait_send()
            _remote(theirs, theirs, scr[0], scr[1], w, (x, y, 1 - c)).wait_recv()

    return Job(gs, [sds(g.shape, F32) for g in gs], {w: w for w in range(nw)},
               [pltpu.SemaphoreType.DMA((nw,)), pltpu.SemaphoreType.DMA((nw,))], start, lambda *_: None, finish)


def rope_tables(t):
    pos = jnp.arange(t, dtype=F32)
    inv_freq = ROPE_THETA ** (-jnp.arange(0, ROT, 2, dtype=F32) / ROT)
    ang = pos[:, None] * inv_freq[None, :]
    cos, sin = jnp.cos(ang), jnp.sin(ang)
    d = jnp.arange(128) % HD
    half = ROT // 2
    cs = jnp.take(cos, d % half, axis=1)
    sn = jnp.take(sin, d % half, axis=1)
    cc = jnp.where(d[None] < ROT, cs, 1.0)
    sa = jnp.where(d[None] < half, -sn, 0.0)
    sb = jnp.where((d[None] >= half) & (d[None] < ROT), sn, 0.0)
    return cc, sa, sb


def _rope(v, cc, sa, sb):
    w = v.shape[1]
    reps = w // 128
    half = ROT // 2
    return (v * jnp.tile(cc, (1, reps)) + pltpu.roll(v, w - half, 1) * jnp.tile(sa, (1, reps))
            + pltpu.roll(v, half, 1) * jnp.tile(sb, (1, reps)))


def _rope_t(dv, cc, sa, sb):
    w = dv.shape[1]
    reps = w // 128
    half = ROT // 2
    return (dv * jnp.tile(cc, (1, reps)) + pltpu.roll(dv * jnp.tile(sa, (1, reps)), half, 1)
            + pltpu.roll(dv * jnp.tile(sb, (1, reps)), w - half, 1))


def pool_fwd(h, b_in, t, tm):
    tm = min(tm, t)
    per = tm // HALO

    def body(prev_ref, cur_ref, b_ref, o_ref, xx):
        i = pl.program_id(0)
        b = b_ref[...]
        xx[pl.ds(0, HALO), :] = jnp.where(i > 0, prev_ref[...] + b, 0.0)
        xx[pl.ds(HALO, tm), :] = cur_ref[...] + b
        tpos = i * tm + lax.broadcasted_iota(jnp.int32, (tm, PG), 0) + 1
        for gi, w in enumerate(POOL_WINDOWS):
            cols = pl.ds(gi * PG, PG)
            acc = xx[pl.ds(HALO, tm), cols]
            for s in range(1, w):
                acc = acc + xx[pl.ds(HALO - s, tm), cols]
            cnt = jnp.minimum(tpos, w).astype(F32)
            o_ref[:, cols] = (acc / cnt - xx[pl.ds(HALO, tm), cols]).astype(o_ref.dtype)

    return pl.pallas_call(
        body, name="pool_fwd", grid=(t // tm,),
        in_specs=[pl.BlockSpec((HALO, PW), lambda i: (jnp.maximum(i * per - 1, 0), 0)),
                  pl.BlockSpec((tm, PW), lambda i: (i, 0)), pl.BlockSpec((1, PW), lambda i: (0, 0))],
        out_specs=pl.BlockSpec((tm, PW), lambda i: (i, 0)), out_shape=sds((t, PW), BF16),
        scratch_shapes=[pltpu.VMEM((tm + HALO, PW), F32)], compiler_params=_params(("arbitrary",)),
    )(h, h, b_in)


def pool_bwd(dpooled, t, tm):
    tm = min(tm, t)
    per = tm // HALO
    nt = t // tm

    def body(cur_ref, nxt_ref, o_ref, db_ref, ee):
        i = pl.program_id(0)
        tpos = i * tm + lax.broadcasted_iota(jnp.int32, (tm, PG), 0) + 1
        for gi, w in enumerate(POOL_WINDOWS):
            cols = pl.ds(gi * PG, PG)
            ee[pl.ds(0, tm), cols] = cur_ref[:, cols] / jnp.minimum(tpos, w).astype(F32)
            ee[pl.ds(tm, HALO), cols] = jnp.where(i < nt - 1, nxt_ref[:, cols] / float(w), 0.0)
        for gi, w in enumerate(POOL_WINDOWS):
            cols = pl.ds(gi * PG, PG)
            acc = ee[pl.ds(0, tm), cols]
            for s in range(1, w):
                acc = acc + ee[pl.ds(s, tm), cols]
            dxp = acc - cur_ref[:, cols]
            o_ref[:, cols] = dxp.astype(o_ref.dtype)
            part = colsum(dxp)

            @pl.when(i == 0)
            def _(cols=cols, part=part):
                db_ref[:, cols] = part

            @pl.when(i > 0)
            def _(cols=cols, part=part):
                db_ref[:, cols] += part

    return pl.pallas_call(
        body, name="pool_bwd", grid=(nt,),
        in_specs=[pl.BlockSpec((tm, PW), lambda i: (i, 0)),
                  pl.BlockSpec((HALO, PW), lambda i: (jnp.minimum((i + 1) * per, t // HALO - 1), 0))],
        out_specs=[pl.BlockSpec((tm, PW), lambda i: (i, 0)), pl.BlockSpec((1, PW), lambda i: (0, 0))],
        out_shape=[sds((t, PW), BF16), sds((1, PW), F32)],
        scratch_shapes=[pltpu.VMEM((tm + HALO, PW), F32)], compiler_params=_params(("arbitrary",)),
    )(dpooled, dpooled)


def _scores(qh, kp, kc, mask_p, mask_c, sink):
    sp = jnp.where(mask_p, lax.dot_general(qh, kp, NT, preferred_element_type=F32), -1e30)
    sc = jnp.where(mask_c, lax.dot_general(qh, kc, NT, preferred_element_type=F32), -1e30)
    m = jnp.maximum(jnp.maximum(jnp.max(sp, axis=-1, keepdims=True), jnp.max(sc, axis=-1, keepdims=True)), sink)
    pp, pc = jnp.exp(sp - m), jnp.exp(sc - m)
    es = jnp.exp(sink - m)
    den = jnp.sum(pp, axis=-1, keepdims=True) + jnp.sum(pc, axis=-1, keepdims=True) + es
    return pp / den, pc / den, es / den


def _masks(n):
    qi = lax.broadcasted_iota(jnp.int32, (BLK, BLK), 0)
    kj = lax.broadcasted_iota(jnp.int32, (BLK, BLK), 1)
    return (kj > qi) & (n > 0), kj <= qi


def attn_fwd(q, k, v, sinks, t, job=None):
    def body(s_ref, q_ref, kp_ref, kc_ref, vp_ref, vc_ref, o_ref):
        n = pl.program_id(0)
        mask_p, mask_c = _masks(n)
        for h in range(N_Q):
            kv = pl.ds(HD * (h // (N_Q // N_KV)), HD)
            hq = pl.ds(HD * h, HD)
            pp, pc, _ = _scores(q_ref[:, hq], kp_ref[:, kv], kc_ref[:, kv], mask_p, mask_c, s_ref[0, h])
            o = (lax.dot_general(pp.astype(BF16), vp_ref[:, kv], NN, preferred_element_type=F32)
                 + lax.dot_general(pc.astype(BF16), vc_ref[:, kv], NN, preferred_element_type=F32))
            o_ref[:, hq] = o.astype(o_ref.dtype)

    prev = lambda n: (jnp.maximum(n - 1, 0), 0)
    cur = lambda n: (n, 0)
    res, moved = carried_call(
        body, "attn_fwd", (t // BLK,),
        [pl.BlockSpec(memory_space=pltpu.SMEM), pl.BlockSpec((BLK, QW), cur),
         pl.BlockSpec((BLK, KVW), prev), pl.BlockSpec((BLK, KVW), cur),
         pl.BlockSpec((BLK, KVW), prev), pl.BlockSpec((BLK, KVW), cur)],
        [pl.BlockSpec((BLK, QW), cur)], [sds((t, QW), BF16)], [], [sinks, q, k, k, v, v], {}, job)
    return res[0], moved


def attn_bwd(q, k, v, do, sinks, t):
    nb = t // BLK
    grp = N_Q // N_KV

    def body(s_ref, q_ref, do_ref, kp_ref, kc_ref, vp_ref, vc_ref, dq_ref, dk_ref, dv_ref, ds_ref, dkc, dvc):
        n = pl.program_id(0)

        @pl.when(n == 0)
        def _():
            dkc[...] = jnp.zeros_like(dkc)
            dvc[...] = jnp.zeros_like(dvc)
            ds_ref[...] = jnp.zeros_like(ds_ref)

        @pl.when(n < nb)
        def _():
            mask_p, mask_c = _masks(n)
            lane = lax.broadcasted_iota(jnp.int32, (1, 128), 1)
            dsink = jnp.zeros((1, 128), F32)
            for hk in range(N_KV):
                kv = pl.ds(HD * hk, HD)
                kp, kc, vp, vc = kp_ref[:, kv], kc_ref[:, kv], vp_ref[:, kv], vc_ref[:, kv]
                dkp = jnp.zeros((BLK, HD), F32)
                dkn = jnp.zeros((BLK, HD), F32)
                dvp = jnp.zeros((BLK, HD), F32)
                dvn = jnp.zeros((BLK, HD), F32)
                for g in range(grp):
                    h = grp * hk + g
                    hq = pl.ds(HD * h, HD)
                    qh, doh = q_ref[:, hq], do_ref[:, hq]
                    pp, pc, ps = _scores(qh, kp, kc, mask_p, mask_c, s_ref[0, h])
                    dpp = lax.dot_general(doh, vp, NT, preferred_element_type=F32)
                    dpc = lax.dot_general(doh, vc, NT, preferred_element_type=F32)
                    delta = jnp.sum(pp * dpp, axis=-1, keepdims=True) + jnp.sum(pc * dpc, axis=-1, keepdims=True)
                    dsp = (pp * (dpp - delta)).astype(BF16)
                    dsc = (pc * (dpc - delta)).astype(BF16)
                    dsink = dsink + jnp.where(lane == h, -jnp.sum(ps * delta), 0.0)
                    dq_ref[:, hq] = (lax.dot_general(dsp, kp, NN, preferred_element_type=F32)
                                     + lax.dot_general(dsc, kc, NN, preferred_element_type=F32))
                    dkp = dkp + lax.dot_general(dsp, qh, TN, preferred_element_type=F32)
                    dkn = dkn + lax.dot_general(dsc, qh, TN, preferred_element_type=F32)
                    dvp = dvp + lax.dot_general(pp.astype(BF16), doh, TN, preferred_element_type=F32)
                    dvn = dvn + lax.dot_general(pc.astype(BF16), doh, TN, preferred_element_type=F32)
                dk_ref[:, kv] = dkc[:, kv] + dkp
                dv_ref[:, kv] = dvc[:, kv] + dvp
                dkc[:, kv] = dkn
                dvc[:, kv] = dvn
            ds_ref[...] += dsink

        @pl.when(n == nb)
        def _():
            dk_ref[...] = dkc[...]
            dv_ref[...] = dvc[...]

    cur = lambda n: (jnp.minimum(n, nb - 1), 0)
    prev = lambda n: (jnp.clip(n - 1, 0, nb - 1), 0)
    return pl.pallas_call(
        body, name="attn_bwd", grid=(nb + 1,),
        in_specs=[pl.BlockSpec(memory_space=pltpu.SMEM), pl.BlockSpec((BLK, QW), cur), pl.BlockSpec((BLK, QW), cur),
                  pl.BlockSpec((BLK, KVW), prev), pl.BlockSpec((BLK, KVW), cur),
                  pl.BlockSpec((BLK, KVW), prev), pl.BlockSpec((BLK, KVW), cur)],
        out_specs=[pl.BlockSpec((BLK, QW), cur), pl.BlockSpec((BLK, KVW), prev), pl.BlockSpec((BLK, KVW), prev),
                   pl.BlockSpec((1, 128), lambda n: (0, 0))],
        out_shape=[sds((t, QW), F32), sds((t, KVW), F32), sds((t, KVW), F32), sds((1, 128), F32)],
        scratch_shapes=[pltpu.VMEM((BLK, KVW), F32), pltpu.VMEM((BLK, KVW), F32)],
        compiler_params=_params(("arbitrary",)),
    )(sinks, q, do, k, k, v, v)


def _adamw(w, g, m, v):
    m2 = B1 * m + (1.0 - B1) * g
    v2 = B2 * v + (1.0 - B2) * jnp.square(g)
    m_hat = m2 / (1.0 - B1 ** STEP)
    v_hat = v2 / (1.0 - B2 ** STEP)
    return -LR * (m_hat / (jnp.sqrt(v_hat) + EPS) + WD * w), m2, v2


def ada_fwd(c16, w_ada, b_sh):
    tn = 512

    def body(c_ref, w_ref, b_ref, o_ref):
        cv = c_ref[...]
        sc = (cv * _sigmoid(cv)).astype(BF16)
        o_ref[...] = lax.dot_general(sc, w_ref[...].astype(BF16), NN, preferred_element_type=F32) + b_ref[...]

    return pl.pallas_call(
        body, name="ada_fwd", grid=(ADA_SH // tn,),
        in_specs=[pl.BlockSpec((16, D), lambda j: (0, 0)), pl.BlockSpec((D, tn), lambda j: (0, j)),
                  pl.BlockSpec((1, tn), lambda j: (0, j))],
        out_specs=pl.BlockSpec((16, tn), lambda j: (0, j)), out_shape=sds((16, ADA_SH), F32),
        compiler_params=_params(("arbitrary",)),
    )(c16, w_ada, b_sh)


def ada_bwd_adam(c16, gm16, w, m, v):
    tm, tn = 256, 512

    def body(c_ref, g_ref, w_ref, m_ref, v_ref, go_ref, d_ref, mo_ref, vo_ref):
        cv = c_ref[...]
        sc = (cv * _sigmoid(cv)).astype(BF16)
        g = lax.dot_general(sc, g_ref[...].astype(BF16), TN, preferred_element_type=F32)
        dl, m2, v2 = _adamw(w_ref[...], g, m_ref[...], v_ref[...])
        go_ref[...] = g
        d_ref[...] = dl
        mo_ref[...] = m2
        vo_ref[...] = v2

    blk = pl.BlockSpec((tm, tn), lambda i, j: (i, j))
    return pl.pallas_call(
        body, name="ada_bwd_adam", grid=(D // tm, ADA_SH // tn),
        in_specs=[pl.BlockSpec((16, tm), lambda i, j: (0, i)), pl.BlockSpec((16, tn), lambda i, j: (0, j)), blk, blk, blk],
        out_specs=[blk] * 4, out_shape=[sds((D, ADA_SH), F32)] * 4,
        compiler_params=_params(("arbitrary", "arbitrary")),
    )(c16, gm16, w, m, v)


def adam_rows(name, w, g, m, v, tm):
    rows, cols = w.shape

    def fn(wv, gv, mv, vv):
        gv = gv[:, :cols]
        dl, m2, v2 = _adamw(wv, gv, mv, vv)
        return gv, dl, m2, v2

    return rowmap(name, fn, [T_(w), T_(g), T_(m), T_(v)], [(cols, F32)] * 4, rows=rows, tm=tm)


def adam_small(name, w, g, m, v):
    def body(w_ref, g_ref, m_ref, v_ref, d_ref, mo_ref, vo_ref):
        dl, m2, v2 = _adamw(w_ref[...], g_ref[...], m_ref[...], v_ref[...])
        d_ref[...] = dl
        mo_ref[...] = m2
        vo_ref[...] = v2

    return pl.pallas_call(body, name=name, out_shape=[sds(w.shape, F32)] * 3)(w, g, m, v)


def sum_devices(allv):
    def body(a_ref, o_ref):
        acc = a_ref[0]
        for d in range(1, 8):
            acc = acc + a_ref[d]
        o_ref[...] = acc

    return pl.pallas_call(body, name="sum_devices", out_shape=sds(allv.shape[1:], F32))(allv)


def _ln_fwd(z, g, b):
    mu = jnp.mean(z, axis=-1, keepdims=True)
    zc = z - mu
    var = jnp.mean(jnp.square(zc), axis=-1, keepdims=True)
    return zc * lax.rsqrt(var + LN_EPS) * g + b


def _ln_bwd(z, g, dout):
    mu = jnp.mean(z, axis=-1, keepdims=True)
    zc = z - mu
    var = jnp.mean(jnp.square(zc), axis=-1, keepdims=True)
    rstd = lax.rsqrt(var + LN_EPS)
    xh = zc * rstd
    dxh = dout * g
    dz = rstd * (dxh - jnp.mean(dxh, axis=-1, keepdims=True) - xh * jnp.mean(dxh * xh, axis=-1, keepdims=True))
    return dz, colsum(dout * xh), colsum(dout)


def modulate(name, xin, shift, scale, t):
    return rowmap(name, lambda xv, sh, sc: xv * (1.0 + sc) + sh, [T_(xin), B_(shift), B_(scale)], [(D, BF16)],
                  rows=t, tm=512)


def residual_ln(name, xin, y, gate, lg, lb, wgt, t):
    def fn(xv, yv, gt, g, b):
        z = ALPHA * xv + (wgt * (1.0 + gt)) * yv
        return _ln_fwd(z, g, b), z

    return rowmap(name, fn, [T_(xin), T_(y), B_(gate), B_(lg), B_(lb)], [(D, F32), (D, F32)], rows=t, tm=256)


def residual_ln_bwd(name, z, dout, y, gate, lg, wgt, t):
    def fn(zv, dv, yv, gt, g):
        dz, dg, db = _ln_bwd(zv, g, dv)
        return dz, (wgt * (1.0 + gt)) * dz, dg, db, colsum(wgt * dz * yv)

    return rowmap(name, fn, [T_(z), T_(dout), T_(y), B_(gate), B_(lg)], [(D, F32), (D, BF16)],
                  [(1, D), (1, D), (1, D)], rows=t, tm=256)


def modulate_bwd(name, dz, du, xin, scale, t):
    def fn(dzv, duv, xv, sc):
        return ALPHA * dzv + duv * (1.0 + sc), colsum(duv), colsum(duv * xv)

    return rowmap(name, fn, [T_(dz), T_(du), T_(xin), B_(scale)], [(D, F32)], [(1, D), (1, D)], rows=t, tm=256)


def ffn_fwd(tag, xin, mod, lg, lb, wi, t, up_job, down_job=None):
    tm = min(1024, t)
    tn = 256
    per = FHP // tn
    u = modulate(tag + "_mod", xin, mod[0], mod[1], t)

    def act(accs, _):
        a, b = accs
        return a, b, a * _sigmoid(a) * b

    hblk = pl.BlockSpec((tm, tn), lambda i, j, k: (i, j))
    (ha, hb, g), up_moved = mm(
        tag + "_up", [u], [wi, wi], [(0, 0, 0), (0, 1, 1)], dims=NN, grid=(t // tm, 2 * per, 1),
        a_specs=[pl.BlockSpec((tm, D), lambda i, j, k: (i, 0))],
        b_specs=[pl.BlockSpec((None, D, tn), lambda i, j, k: (j // per, 0, j % per)),
                 pl.BlockSpec((None, D, tn), lambda i, j, k: (2 + j // per, 0, j % per))],
        outs=[sds((t, 2 * FHP), BF16)] * 3, out_specs=[hblk] * 3, acc_shapes=[(tm, tn)] * 2, epilogue=act, job=up_job)
    wo = up_moved[0].reshape(2 * FHP, D)
    tk = FHP // 2
    y, down_moved = _with_moved(mm(
        tag + "_down", [g], [wo], [(0, 0, 0)], dims=NN, grid=(t // tm, 2, 4),
        a_specs=[pl.BlockSpec((tm, tk), lambda i, j, k: (i, k))],
        b_specs=[pl.BlockSpec((tk, D // 2), lambda i, j, k: (k, j))],
        outs=[sds((t, D), F32)], out_specs=[pl.BlockSpec((tm, D // 2), lambda i, j, k: (i, j))],
        acc_shapes=[(tm, D // 2)], job=down_job), down_job)
    xo, z = residual_ln(tag + "_ln", xin, y, mod[2], lg, lb, 0.5, t)
    return xo, (u, ha, hb, g, y, z), wo, up_moved, down_moved


def ffn_bwd(tag, xin, saved, dout, mod, lg, wi, wo, t, sp, dact_job=None, dwo_job=None):
    u, ha, hb, g, y, z = saved
    tm = min(1024, t)
    dz, dy, dlg, dlb, dgate = residual_ln_bwd(tag + "_ln_bwd", z, dout, y, mod[2], lg, 0.5, t)

    def dact(accs, ex):
        dg = accs[0]
        a, b = ex[0].astype(F32), ex[1].astype(F32)
        s = _sigmoid(a)
        return dg * b * (s * (1.0 + a * (1.0 - s))), dg * (a * s)

    tn = 256
    hblk = pl.BlockSpec((tm, tn), lambda i, j, k: (i, j))
    (dha, dhb), dact_moved = _with_moved(mm(
        tag + "_dact", [dy], [wo], [(0, 0, 0)], dims=NT, grid=(t // tm, 2 * FHP // tn, 1),
        a_specs=[pl.BlockSpec((tm, D), lambda i, j, k: (i, 0))],
        b_specs=[pl.BlockSpec((tn, D), lambda i, j, k: (j, 0))],
        outs=[sds((t, 2 * FHP), BF16)] * 2, out_specs=[hblk] * 2, acc_shapes=[(tm, tn)],
        epilogue=dact, extras=[ha, hb], extra_specs=[hblk] * 2, job=dact_job), dact_job)
    tk = min(1024, t)
    th = FHP // 2
    dwo, dwo_moved = _with_moved(mm(
        tag + "_dwo", [g], [dy], [(0, 0, 0)], dims=TN, grid=(4, 2, t // tk),
        a_specs=[pl.BlockSpec((tk, th), lambda i, j, k: (k, i))],
        b_specs=[pl.BlockSpec((tk, D // 2), lambda i, j, k: (k, j))],
        outs=[sds((2 * FHP, D), BF16)], out_specs=[pl.BlockSpec((th, D // 2), lambda i, j, k: (i, j))],
        acc_shapes=[(th, D // 2)], job=dwo_job), dwo_job)
    dwo = dwo.reshape(2, FHP, D)

    def dwi_part(part, dh, carry, job):
        return mm(
            f"{tag}_dwi{part}", [u], [dh], [(0, 0, 0)], dims=TN, grid=(2, 4, t // tk),
            a_specs=[pl.BlockSpec((tk, D // 2), lambda i, j, k: (k, i))],
            b_specs=[pl.BlockSpec((tk, th), lambda i, j, k: (k, j))],
            outs=[sds((4, D, FHP), BF16)],
            out_specs=[pl.BlockSpec((None, D // 2, th), lambda i, j, k: (2 * part + j // 2, i, j % 2))],
            acc_shapes=[(D // 2, th)], carry=carry, job=job)

    dwi, (sib_fo,) = dwi_part(0, dha, None, reduce_sibling_job([(dwo, view_ffn_out, FO, D)]))
    q_fo = chip_sum(tag + "_chipsum_fo", dwo, sib_fo, sp, FO, FO // 2, ffn_out=True)
    dwi, (far_fo,) = dwi_part(1, dhb, dwi, reduce_chips_job([q_fo]))
    (sib_fi,) = run_job(tag + "_sibling_fi", reduce_sibling_job([(dwi, view_lead, D, FHP)]))
    q_fi = chip_sum(tag + "_chipsum_fi", dwi, sib_fi, sp, D, 512)
    du, (far_fi,) = mm(
        tag + "_du", [dha, dhb], [wi, wi], [(0, 0, 0), (1, 1, 0)], dims=NT, grid=(t // tm, 2, 4),
        a_specs=[pl.BlockSpec((tm, th), lambda i, j, k: (i, k))] * 2,
        b_specs=[pl.BlockSpec((None, D // 2, th), lambda i, j, k: (k // 2, j, k % 2)),
                 pl.BlockSpec((None, D // 2, th), lambda i, j, k: (2 + k // 2, j, k % 2))],
        outs=[sds((t, D), F32)], out_specs=[pl.BlockSpec((tm, D // 2), lambda i, j, k: (i, j))],
        acc_shapes=[(tm, D // 2)], job=reduce_chips_job([q_fi]))
    dx, dshift, dscale = modulate_bwd(tag + "_mod_bwd", dz, du, xin, mod[1], t)
    return dx, (q_fi, far_fi), (q_fo, far_fo), (dshift, dscale, dgate), dlg, dlb, dact_moved, dwo_moved


def mix_fwd(xin, mod, lg, lb, wts, b_in, pool_scale, sinks, tabs, t, in_job, attn_job):
    w_in, wp, wba, wbb, wo = wts
    tm = min(1024, t)
    u = modulate("mix_mod", xin, mod[0], mod[1], t)
    tmh = min(512, t)
    h, in_moved = mm("mix_in", [u], [w_in], [(0, 0, 0)], dims=NN, grid=(t // tmh, 4, 1),
                     a_specs=[pl.BlockSpec((tmh, D), lambda i, j, k: (i, 0))],
                     b_specs=[pl.BlockSpec((None, D, IN_SH), lambda i, j, k: (j, 0, 0))],
                     outs=[sds((t, IN_W), F32)], out_specs=[pl.BlockSpec((tmh, IN_SH), lambda i, j, k: (i, j))],
                     acc_shapes=[(tmh, IN_SH)], job=in_job)
    attn_job = attn_job(in_moved)
    pooled = pool_fwd(h, b_in, t, 512)
    gblk = pl.BlockSpec((tm, PG), lambda i, j, k: (i, j))
    mixed = mm("mix_pool", [pooled], [wp], [(0, 0, 0)], dims=NN, grid=(t // tm, 4, 1), a_specs=[gblk],
               b_specs=[pl.BlockSpec((None, PG, PG), lambda i, j, k: (j, 0, 0))],
               outs=[sds((t, PW), F32)], out_specs=[gblk], acc_shapes=[(tm, PG)])
    pm = rowmap("mix_pscale", lambda mv, ps: mv * ps, [T_(mixed), B_(pool_scale)], [(PW, BF16)], rows=t, tm=512)

    def branch(name, a, w):
        return mm(name, [a], [w], [(0, 0, 0)], dims=NN, grid=(t // tm, 4, 1),
                  a_specs=[pl.BlockSpec((tm, PW), lambda i, j, k: (i, 0))],
                  b_specs=[pl.BlockSpec((None, PW, D // 4), lambda i, j, k: (j, 0, 0))],
                  outs=[sds((t, D), F32)], out_specs=[pl.BlockSpec((tm, D // 4), lambda i, j, k: (i, j))],
                  acc_shapes=[(tm, D // 4)])

    ya = branch("mix_branch_a", pm, wba)

    def qkv(hq, hk, hv, bq, bk, bv, cc, sa, sb):
        return (_rope(hq + bq, cc, sa, sb) * (HD ** -0.5), _rope(hk + bk, cc, sa, sb), hv + bv)

    qr, kr, vv = rowmap(
        "mix_rope", qkv,
        [T_(h, QW, 1), T_(h, KVW, 8), T_(h, KVW, 9), B_(b_in, QW, 1), B_(b_in, KVW, 8), B_(b_in, KVW, 9),
         T_(tabs[0]), T_(tabs[1]), T_(tabs[2])],
        [(QW, BF16), (KVW, BF16), (KVW, BF16)], rows=t, tm=512)
    attn, attn_moved = attn_fwd(qr, kr, vv, sinks, t, attn_job)
    yb = branch("mix_branch_b", attn, wbb)
    cw = 512

    def merge(ga, gb, ba, bb, yav, ybv):
        return _sigmoid(ga + ba) * yav + _sigmoid(gb + bb) * ybv

    merged = rowmap(
        "mix_merge", merge,
        [T_(h, cw, 5), T_(h, cw, 9), B_(b_in, cw, 5), B_(b_in, cw, 9), T_(ya, cw), T_(yb, cw)],
        [(D, BF16)], rows=t, tm=512, ncol=D // cw)
    y = mm("mix_out", [merged], [wo], [(0, 0, 0)], dims=NN, grid=(t // tm, 2, 1),
           a_specs=[pl.BlockSpec((tm, D), lambda i, j, k: (i, 0))],
           b_specs=[pl.BlockSpec((D, D // 2), lambda i, j, k: (0, j))],
           outs=[sds((t, D), F32)], out_specs=[pl.BlockSpec((tm, D // 2), lambda i, j, k: (i, j))],
           acc_shapes=[(tm, D // 2)])
    xo, z = residual_ln("mix_ln", xin, y, mod[2], lg, lb, 1.0, t)
    return xo, (u, h, pooled, mixed, pm, ya, qr, kr, vv, attn, yb, merged, y, z), attn_moved


def mix_bwd(xin, saved, dout, mod, lg, wts, b_in, pool_scale, sinks, tabs, t):
    u, h, pooled, mixed, pm, ya, qr, kr, vv, attn, yb, merged, y, z = saved
    w_in, wp, wba, wbb, wo = wts
    tm = min(1024, t)
    tk = min(1024, t)
    dz, dy, dlg, dlb, dgate = residual_ln_bwd("mix_ln_bwd", z, dout, y, mod[2], lg, 1.0, t)
    dmerged = mm("mix_dmerged", [dy], [wo], [(0, 0, 0)], dims=NT, grid=(t // tm, 2, 1),
                 a_specs=[pl.BlockSpec((tm, D), lambda i, j, k: (i, 0))],
                 b_specs=[pl.BlockSpec((D // 2, D), lambda i, j, k: (j, 0))],
                 outs=[sds((t, D), F32)], out_specs=[pl.BlockSpec((tm, D // 2), lambda i, j, k: (i, j))],
                 acc_shapes=[(tm, D // 2)])
    half = pl.BlockSpec((tk, D // 2), lambda i, j, k: (k, i))
    dwo = mm("mix_dwo", [merged], [dy], [(0, 0, 0)], dims=TN, grid=(2, 2, t // tk), a_specs=[half],
             b_specs=[pl.BlockSpec((tk, D // 2), lambda i, j, k: (k, j))],
             outs=[sds((D, D), BF16)], out_specs=[pl.BlockSpec((D // 2, D // 2), lambda i, j, k: (i, j))],
             acc_shapes=[(D // 2, D // 2)])
    cw = 512

    def dmerge(dm, ga, gb, ba, bb, yav, ybv):
        sa_, sb_ = _sigmoid(ga + ba), _sigmoid(gb + bb)
        dga = dm * yav * sa_ * (1.0 - sa_)
        dgb = dm * ybv * sb_ * (1.0 - sb_)
        return dm * sa_, dm * sb_, dga, dgb, colsum(dga), colsum(dgb)

    dya, dyb, dgla, dglb, dbga, dbgb = rowmap(
        "mix_dmerge", dmerge,
        [T_(dmerged, cw), T_(h, cw, 5), T_(h, cw, 9), B_(b_in, cw, 5), B_(b_in, cw, 9), T_(ya, cw), T_(yb, cw)],
        [(D, BF16)] * 4, [(1, D), (1, D)], rows=t, tm=512, ncol=D // cw)

    def dbranch(name, dyv, act, w):
        dwb = mm(name + "_dw", [act], [dyv], [(0, 0, 0)], dims=TN, grid=(1, 4, t // tk),
                 a_specs=[pl.BlockSpec((tk, PW), lambda i, j, k: (k, 0))],
                 b_specs=[pl.BlockSpec((tk, D // 4), lambda i, j, k: (k, j))],
                 outs=[sds((4, PW, D // 4), BF16)], out_specs=[pl.BlockSpec((None, PW, D // 4), lambda i, j, k: (j, 0, 0))],
                 acc_shapes=[(PW, D // 4)])
        return dwb, lambda dt: mm(
            name + "_dx", [dyv], [w], [(0, 0, 0)], dims=NT, grid=(t // tm, 1, 4),
            a_specs=[pl.BlockSpec((tm, D // 4), lambda i, j, k: (i, k))],
            b_specs=[pl.BlockSpec((None, PW, D // 4), lambda i, j, k: (k, 0, 0))],
            outs=[sds((t, PW), dt)], out_specs=[pl.BlockSpec((tm, PW), lambda i, j, k: (i, 0))], acc_shapes=[(tm, PW)])

    dwba, dpm_fn = dbranch("mix_dbranch_a", dya, pm, wba)
    dwbb, dattn_fn = dbranch("mix_dbranch_b", dyb, attn, wbb)
    dpm, dattn = dpm_fn(F32), dattn_fn(BF16)
    dmixed, dps = rowmap("mix_dpscale", lambda dp, mv, ps: (dp * ps, colsum(dp * mv)),
                         [T_(dpm), T_(mixed), B_(pool_scale)], [(PW, BF16)], [(1, PW)], rows=t, tm=512)
    gblk = pl.BlockSpec((tm, PG), lambda i, j, k: (i, j))
    dpooled = mm("mix_dpool", [dmixed], [wp], [(0, 0, 0)], dims=NT, grid=(t // tm, 4, 1), a_specs=[gblk],
                 b_specs=[pl.BlockSpec((None, PG, PG), lambda i, j, k: (j, 0, 0))],
                 outs=[sds((t, PW), F32)], out_specs=[gblk], acc_shapes=[(tm, PG)])
    kblk = pl.BlockSpec((tk, PG), lambda i, j, k: (k, i))
    dwp = mm("mix_dwpool", [pooled], [dmixed], [(0, 0, 0)], dims=TN, grid=(4, 1, t // tk), a_specs=[kblk], b_specs=[kblk],
             outs=[sds((4, PG, PG), BF16)], out_specs=[pl.BlockSpec((None, PG, PG), lambda i, j, k: (i, 0, 0))],
             acc_shapes=[(PG, PG)])
    dxp, dbxp = pool_bwd(dpooled, t, 512)
    dqr, dkr, dvv, dsinks = attn_bwd(qr, kr, vv, dattn, sinks, t)

    def dqkv(dq, dk, dv, cc, sa, sb):
        dq = _rope_t(dq, cc, sa, sb) * (HD ** -0.5)
        dk = _rope_t(dk, cc, sa, sb)
        return dq, dk, dv, colsum(dq), colsum(dk), colsum(dv)

    dq, dk, dvb, dbq, dbk, dbv = rowmap(
        "mix_rope_bwd", dqkv, [T_(dqr), T_(dkr), T_(dvv), T_(tabs[0]), T_(tabs[1]), T_(tabs[2])],
        [(QW, BF16), (KVW, BF16), (KVW, BF16)], [(1, QW), (1, KVW), (1, KVW)], rows=t, tm=512)
    dh = jnp.concatenate([dxp, dq, dk, dvb, dgla, dglb], axis=1)
    db_in = jnp.concatenate([dbxp, dbq, dbk, dbv, dbga, dbgb], axis=1)
    dwin = mm("mix_dwin", [u], [dh], [(0, 0, 0)], dims=TN, grid=(2, 4, t // tk), a_specs=[half],
              b_specs=[pl.BlockSpec((tk, IN_SH), lambda i, j, k: (k, j))],
              outs=[sds((4, D, IN_SH), BF16)], out_specs=[pl.BlockSpec((None, D // 2, IN_SH), lambda i, j, k: (j, i, 0))],
              acc_shapes=[(D // 2, IN_SH)])
    du = mm("mix_du", [dh], [w_in], [(0, 0, 0)], dims=NT, grid=(t // tm, 2, 4),
            a_specs=[pl.BlockSpec((tm, IN_SH), lambda i, j, k: (i, k))],
            b_specs=[pl.BlockSpec((None, D // 2, IN_SH), lambda i, j, k: (k, j, 0))],
            outs=[sds((t, D), F32)], out_specs=[pl.BlockSpec((tm, D // 2), lambda i, j, k: (i, j))],
            acc_shapes=[(tm, D // 2)])
    dx, dshift, dscale = modulate_bwd("mix_mod_bwd", dz, du, xin, mod[1], t)
    return dx, (dwin, dwp, dwba, dwbb, dwo), (dshift, dscale, dgate), dlg, dlb, db_in, dps, dsinks


def cast_shard(name, w, sp, pad=0, ffn_out=False):
    rows, cols = w.shape

    def fn(wv):
        wb = wv.astype(BF16)
        return jnp.concatenate([wb, jnp.zeros((wb.shape[0], pad), BF16)], axis=1) if pad else wb

    if ffn_out:
        tm = rows // 2
        shape = (2, FHP, D)
        spec = pl.BlockSpec((None, tm, cols), lambda j, i, s: (s[0] // 2, (s[0] % 2) * 2 + i, 0))
    else:
        tm = rows // 4
        shape = (4, rows, cols + pad)
        spec = pl.BlockSpec((None, tm, cols + pad), lambda j, i, s: (s[0], i, 0))
    return rowmap(name, fn, [T_(w)], [(shape, BF16, spec)], rows=rows, tm=tm, sp=sp)


def chip_sum(name, dw, got, sp, rows, tm, ffn_out=False):
    hr, cols = rows // 2, got.shape[2]
    per = hr // tm
    pos = pl.BlockSpec((None, tm, cols), lambda j, i, s: (i // per, i % per, 0))
    if ffn_out:
        mine = pl.BlockSpec((None, tm, cols), lambda j, i, s: (i // 2, (i % 2) * 2 + s[1], 0))
    else:
        mine = pl.BlockSpec((None, tm, cols), lambda j, i, s: (i // per, s[1] * per + i % per, 0))
    return rowmap(name, lambda av, bv: av.astype(F32) + bv.astype(F32), [X_(dw, mine), X_(got, pos)],
                  [(got.shape, BF16, pos)], rows=4 * hr, tm=tm, sp=sp)


def chip_total(name, q, got, sp, rows, tm):
    hr, cols = rows // 2, q.shape[2]
    per = hr // tm

    def part(f):
        return X_(got, pl.BlockSpec((None, tm, cols), lambda j, i, s, f=f: (f, i, 0)))

    return rowmap(
        name, lambda av, b0, b1, b2: ((av.astype(F32) + b0.astype(F32)) + b1.astype(F32)) + b2.astype(F32),
        [X_(q, pl.BlockSpec((None, tm, cols), lambda j, i, s: (s[0], i, 0))), part(0), part(1), part(2)],
        [((rows, cols), F32, pl.BlockSpec((tm, cols), lambda j, i, s: (s[1] * per + i, 0)))], rows=hr, tm=tm, sp=sp)


def kernel(x, c, w_ada, b_ada, ln_g, ln_b, w_ffn1_in, w_ffn1_out, w_in, b_in, w_pool, pool_scale, sinks, w_branch_a, w_branch_b, w_out, w_ffn2_in, w_ffn2_out, loss_target, m_w_ada, m_b_ada, m_ln_g, m_ln_b, m_w_ffn1_in, m_w_ffn1_out, m_w_in, m_b_in, m_w_pool, m_pool_scale, m_sinks, m_w_branch_a, m_w_branch_b, m_w_out, m_w_ffn2_in, m_w_ffn2_out, v_w_ada, v_b_ada, v_ln_g, v_ln_b, v_w_ffn1_in, v_w_ffn1_out, v_w_in, v_b_in, v_w_pool, v_pool_scale, v_sinks, v_w_branch_a, v_w_branch_b, v_w_out, v_w_ffn2_in, v_w_ffn2_out):
    t = x.shape[1]
    xs, tgt = x[0], loss_target[0]
    xi, yi, ci = lax.axis_index("x"), lax.axis_index("y"), lax.axis_index("c")
    chip = 2 * xi + yi
    dev = 2 * chip + ci
    b_in2, ps2, sinks2 = b_in, pool_scale, sinks

    first = jnp.concatenate([c.reshape(-1), ln_g.reshape(-1), ln_b.reshape(-1)]).reshape(-1, 128)
    first_all = allgather_small("gather_cond", first).reshape(8, -1)
    c_all = first_all[:, :D]
    ln_parts = first_all[0::2, D:].reshape(4, 2, 3, D // 4)
    ln_full = jnp.transpose(ln_parts, (1, 2, 0, 3)).reshape(2, 3, D)
    lgs = [ln_full[0, s:s + 1] for s in range(3)]
    lbs = [ln_full[1, s:s + 1] for s in range(3)]
    c16 = jnp.pad(c_all, ((0, 8), (0, 0)))
    b_ada_sh = lax.dynamic_slice(b_ada, (0, chip * ADA_SH), (1, ADA_SH))
    mod_part = ada_fwd(c16, w_ada[0], b_ada_sh)[:8]
    mod_all = allgather_small("gather_mod", mod_part.reshape(-1, 128)).reshape(8, 8, ADA_SH)
    mod_mine = lax.dynamic_index_in_dim(mod_all[0::2], dev, axis=1, keepdims=False).reshape(9, D)
    mods = [[mod_mine[3 * s + k:3 * s + k + 1] for k in range(3)] for s in range(3)]

    plain = [("f1o", w_ffn1_out[0]), ("win", w_in[0]), ("wp", w_pool[0].reshape(4 * 64, PG)), ("wba", w_branch_a[0]),
             ("wbb", w_branch_b[0]), ("wo", w_out[0]), ("f2o", w_ffn2_out[0])]
    sp = jnp.stack([chip, ci]).astype(jnp.int32)
    sh = {n: cast_shard("cast_" + n, w, sp, ffn_out=n in ("f1o", "f2o")) for n, w in plain}
    sh["f1i"] = cast_shard("cast_f1i", w_ffn1_in[0], sp, FHP - FH)
    sh["f2i"] = cast_shard("cast_f2i", w_ffn2_in[0], sp, FHP - FH)
    order = ["f1i", "f1o", "win", "wp", "wba", "wbb", "wo", "f2i", "f2o"]
    views = {n: (view_ffn_out if n in ("f1o", "f2o") else view_lead) for n in order}
    shard_rows = {n: (FO if n in ("f1o", "f2o") else sh[n].shape[1]) for n in order}
    shard_cols = {n: sh[n].shape[2] for n in order}
    tiles = {"f1i": 512, "f1o": FO // 2, "win": 512, "wp": 128, "wba": 512, "wbb": 512, "wo": 256, "f2i": 512, "f2o": FO // 2}

    def item(n, part=0, parts=1):
        return (sh[n], views[n], shard_rows[n], part, parts)

    tabs = rope_tables(t)

    (g_f1i,) = run_job("gather_f1i", gather_job([item("f1i")]))
    x1, sv1, f1o, up1, (g_win,) = ffn_fwd(
        "ffn1", xs, mods[0], lgs[0], lbs[0], g_f1i, t,
        gather_job([item(n) for n in ("f1o", "wp", "wba", "wbb", "wo")]), gather_job([item("win")]))
    _, g_wp, g_wba, g_wbb, g_wo = up1
    wp_full = jnp.transpose(g_wp.reshape(4, 4, 64, PG), (1, 0, 2, 3)).reshape(4, PG, PG)
    wts = (g_win, wp_full, g_wba, g_wbb, g_wo.reshape(D, D))
    x2, sv2, (g_f2i,) = mix_fwd(
        x1, mods[1], lgs[1], lbs[1], wts, b_in2, ps2, sinks2, tabs, t, gather_job([item("f2i", 0, 2)]),
        lambda moved: gather_job([(moved[0], view_lead, D, 1, 2)]))
    x3, sv3, f2o, _, _ = ffn_fwd("ffn2", x2, mods[2], lgs[2], lbs[2], g_f2i, t, gather_job([item("f2o")]))

    def lossfn(xv, tv):
        d = xv - tv
        return d * (1.0 / D), jnp.sum(d * d).reshape(1, 1)

    dx3, lsum = rowmap("loss", lossfn, [T_(x3), T_(tgt)], [(D, F32)], [(1, 1)], rows=t, tm=512)
    loss = lax.psum(0.5 * lsum[0, 0] / D, ("x", "y", "c"))

    dx2, red_f2i, red_f2o, gm2, dlg2, dlb2, _, _ = ffn_bwd("ffn2", x2, sv3, dx3, mods[2], lgs[2], g_f2i, f2o, t, sp)
    dx1, dmix, gm1, dlg1, dlb1, db_in, dps, dsinks = mix_bwd(x1, sv2, dx2, mods[1], lgs[1], wts, b_in2, ps2, sinks2, tabs, t)
    dwin, dwp, dwba, dwbb, dwo = dmix
    dwp_sh = jnp.transpose(dwp.reshape(4, 4, 64, PG), (1, 0, 2, 3)).reshape(4, 4 * 64, PG)
    mix_parts = {"win": dwin, "wp": dwp_sh, "wba": dwba, "wbb": dwbb, "wo": dwo.reshape(4, D // 4, D)}
    mix_names = list(mix_parts)
    sib = run_job("sibling_mix", reduce_sibling_job(
        [(mix_parts[n], view_lead, shard_rows[n], shard_cols[n]) for n in mix_names]))
    q = {n: chip_sum("chipsum_" + n, mix_parts[n], g, sp, shard_rows[n], tiles[n]) for n, g in zip(mix_names, sib)}
    dx0, red_f1i, red_f1o, gm0, dlg0, dlb0, far_a, far_b = ffn_bwd(
        "ffn1", xs, sv1, dx1, mods[0], lgs[0], g_f1i, f1o, t, sp,
        reduce_chips_job([q["win"], q["wp"]]), reduce_chips_job([q["wo"], q["wba"], q["wbb"]]))
    reduced = {"f1i": red_f1i, "f1o": red_f1o, "f2i": red_f2i, "f2o": red_f2o, "win": (q["win"], far_a[0]),
               "wp": (q["wp"], far_a[1]), "wo": (q["wo"], far_b[0]), "wba": (q["wba"], far_b[1]), "wbb": (q["wbb"], far_b[2])}
    halves = [chip_total("total_" + n, *reduced[n], sp, shard_rows[n], tiles[n]) for n in order]
    gw = dict(zip(order, run_job("share_halves", share_halves_job(halves))))

    small = jnp.concatenate([*gm0, *gm1, *gm2, dlg0, dlg1, dlg2, dlb0, dlb1, dlb2, db_in, dps, dsinks], axis=1)
    n_small = small.shape[1]
    rows_small = -(-n_small // 1024) * 8
    small = jnp.pad(small, ((0, 0), (0, rows_small * 128 - n_small))).reshape(rows_small, 128)
    small_all = allgather_small("gather_small", small)
    tot = sum_devices(small_all).reshape(1, -1)
    gmod_all = small_all.reshape(8, -1)[:, :9 * D]
    o = 9 * D
    g_b_ada = tot[:, :o]
    g_ln_g = lax.dynamic_slice(tot[:, o:o + 3 * D].reshape(3, D), (0, chip * (D // 4)), (3, D // 4))
    g_ln_b = lax.dynamic_slice(tot[:, o + 3 * D:o + 6 * D].reshape(3, D), (0, chip * (D // 4)), (3, D // 4))
    o += 6 * D
    g_b_in, g_ps, g_sinks = tot[:, o:o + IN_W], tot[:, o + IN_W:o + IN_W + PW], tot[:, o + IN_W + PW:o + IN_W + PW + N_Q]

    gm16 = jnp.pad(lax.dynamic_slice(gmod_all, (0, chip * ADA_SH), (8, ADA_SH)), ((0, 8), (0, 0)))
    g_w_ada, d_w_ada, nm_w_ada, nv_w_ada = ada_bwd_adam(c16, gm16, w_ada[0], m_w_ada[0], v_w_ada[0])

    def big(n, w, m, v, tm):
        shape = w.shape
        w2, m2, v2 = (a.reshape(shape[-2] if a.ndim == 3 else -1, shape[-1]) for a in (w, m, v))
        return [r.reshape(shape) for r in adam_rows("adam_" + n, w2, gw[n], m2, v2, tm)]

    def tiny(n, w, g, m, v):
        return [g.reshape(w.shape)] + list(adam_small("adam_" + n, w, g.reshape(w.shape), m, v))

    res = {
        "w_ada": [a[None] for a in (g_w_ada, d_w_ada, nm_w_ada, nv_w_ada)],
        "b_ada": tiny("b_ada", b_ada, g_b_ada, m_b_ada, v_b_ada),
        "ln_g": tiny("ln_g", ln_g, g_ln_g, m_ln_g, v_ln_g),
        "ln_b": tiny("ln_b", ln_b, g_ln_b, m_ln_b, v_ln_b),
        "w_ffn1_in": big("f1i", w_ffn1_in, m_w_ffn1_in, v_w_ffn1_in, 128),
        "w_ffn1_out": big("f1o", w_ffn1_out, m_w_ffn1_out, v_w_ffn1_out, 32),
        "w_in": big("win", w_in, m_w_in, v_w_in, 256),
        "b_in": tiny("b_in", b_in, g_b_in, m_b_in, v_b_in),
        "w_pool": big("wp", w_pool, m_w_pool, v_w_pool, 256),
        "pool_scale": tiny("pool_scale", pool_scale, g_ps, m_pool_scale, v_pool_scale),
        "sinks": tiny("sinks", sinks, g_sinks, m_sinks, v_sinks),
        "w_branch_a": big("wba", w_branch_a, m_w_branch_a, v_w_branch_a, 512),
        "w_branch_b": big("wbb", w_branch_b, m_w_branch_b, v_w_branch_b, 512),
        "w_out": big("wo", w_out, m_w_out, v_w_out, 128),
        "w_ffn2_in": big("f2i", w_ffn2_in, m_w_ffn2_in, v_w_ffn2_in, 128),
        "w_ffn2_out": big("f2o", w_ffn2_out, m_w_ffn2_out, v_w_ffn2_out, 32),
    }
    names = ["w_ada", "b_ada", "ln_g", "ln_b", "w_ffn1_in", "w_ffn1_out", "w_in", "b_in", "w_pool", "pool_scale", "sinks",
             "w_branch_a", "w_branch_b", "w_out", "w_ffn2_in", "w_ffn2_out"]
    return (loss, dx0[None], *[res[n][0] for n in names], *[res[n][1] for n in names],
            *[res[n][2] for n in names], *[res[n][3] for n in names])
```

```python
import jax
import jax.numpy as jnp
from jax import lax
from jax.experimental import pallas as pl
from jax.experimental.pallas import tpu as pltpu

F32 = jnp.float32
BF16 = jnp.bfloat16
MESH = pl.DeviceIdType.MESH
ANY = pl.BlockSpec(memory_space=pl.ANY)

D = 2048
N_Q, N_KV, HD = 16, 4, 64
QW, KVW = N_Q * HD, N_KV * HD
BLK = 128
POOL_WINDOWS = (2, 4, 8, 16)
PW, PG = 1024, 256
HALO = 16
ROPE_THETA = 500000.0
ROT = HD // 4
LN_EPS = 1e-5
ALPHA = 2.0 ** 0.25
FH = 2752
FHP = 2816
FO = 1376
IN_W = 6656
IN_SH = IN_W // 4
ADA_SH = 18432 // 4
B1, B2, LR, EPS, WD, STEP = 0.9, 0.999, 0.001, 1e-08, 0.01, 10
VMEM_LIMIT = 56 * 1024 * 1024
FLIPS = ((1, 0), (0, 1), (1, 1))
NN = (((1,), (0,)), ((), ()))
NT = (((1,), (1,)), ((), ()))
TN = (((0,), (0,)), ((), ()))


def _params(sem):
    return pltpu.CompilerParams(dimension_semantics=sem, vmem_limit_bytes=VMEM_LIMIT)


def _aligned(v, m):
    return v if isinstance(v, int) else pl.multiple_of(v, m)


def _sigmoid(v):
    return 1.0 / (1.0 + jnp.exp(-v))


def T_(arr, width=None, off=0):
    return ("t", arr, width, off)


def B_(arr, width=None, off=0):
    return ("b", arr, width, off)


def X_(arr, spec):
    return ("x", arr, spec, 0)


def rowmap(name, fn, ins, outs, accs=(), *, rows, tm, ncol=1, with_ids=False, sp=None, alias=None):
    tm = min(tm, rows)
    nrow = rows // tm
    in_specs, arrs = [], []
    for kind, arr, width, off in ins:
        if kind == "x":
            in_specs.append(width)
        elif kind == "t":
            w = arr.shape[1] if width is None else width
            in_specs.append(pl.BlockSpec((tm, w), lambda j, i, *_, off=off: (i, off + j)))
        else:
            w = arr.shape[1] if width is None else width
            in_specs.append(pl.BlockSpec((arr.shape[0], w), lambda j, i, *_, off=off: (0, off + j)))
        arrs.append(arr)
    out_shape, out_specs = [], []
    for o in outs:
        if len(o) == 3:
            out_shape.append(jax.ShapeDtypeStruct(o[0], o[1]))
            out_specs.append(o[2])
        else:
            out_shape.append(jax.ShapeDtypeStruct((rows, o[0]), o[1]))
            out_specs.append(pl.BlockSpec((tm, o[0] // ncol), lambda j, i, *_: (i, j)))
    for r, width in accs:
        out_shape.append(jax.ShapeDtypeStruct((r, width), F32))
        out_specs.append(pl.BlockSpec((r, width // ncol), lambda j, i, *_: (0, j)))
    ni, no = len(ins), len(outs)
    nsp = 0 if sp is None else 1

    def body(*refs):
        refs = refs[nsp:]
        i = pl.program_id(1)
        vals = [r[...] for r in refs[:ni]]
        res = fn(pl.program_id(0), i, *vals) if with_ids else fn(*vals)
        if not isinstance(res, (tuple, list)):
            res = (res,)
        for r, v in zip(refs[ni:ni + no], res[:no]):
            r[...] = v.astype(r.dtype)
        for r, v in zip(refs[ni + no:], res[no:]):
            @pl.when(i == 0)
            def _(r=r, v=v):
                r[...] = v

            @pl.when(i > 0)
            def _(r=r, v=v):
                r[...] += v

    grid_spec = pltpu.PrefetchScalarGridSpec(num_scalar_prefetch=nsp, grid=(ncol, nrow), in_specs=in_specs,
                                             out_specs=out_specs)
    res = pl.pallas_call(
        body, name=name, grid_spec=grid_spec, out_shape=out_shape,
        input_output_aliases={nsp + k: v for k, v in (alias or {}).items()},
        compiler_params=_params(("arbitrary", "arbitrary")),
    )(*([sp] if nsp else []), *arrs)
    return res[0] if len(res) == 1 else res


def colsum(v):
    return jnp.sum(v, axis=0, keepdims=True)


def mm(name, a_ops, b_ops, ops, *, dims, grid, a_specs, b_specs, outs, out_specs, acc_shapes,
       epilogue=None, extras=(), extra_specs=(), carry=None, job=None):
    gk = grid[2]
    na, nb, ne, nacc = len(a_ops), len(b_ops), len(extras), len(acc_shapes)
    nc = 0 if carry is None else 1
    no = len(outs)

    def body(*refs):
        a_refs = refs[:na]
        b_refs = refs[na:na + nb]
        e_refs = refs[na + nb:na + nb + ne]
        o_refs = refs[na + nb + ne + nc:na + nb + ne + nc + no]
        acc_refs = refs[na + nb + ne + nc + no:]
        k = pl.program_id(2)

        def partials():
            res = [None] * nacc
            for ai, bi, ci in ops:
                p = lax.dot_general(a_refs[ai][...], b_refs[bi][...], dims, preferred_element_type=F32)
                res[ci] = p if res[ci] is None else res[ci] + p
            return res

        def finish(accs):
            outv = epilogue(accs, [e[...] for e in e_refs]) if epilogue else (accs[0],)
            for o, v in zip(o_refs, outv):
                o[...] = v.astype(o.dtype)

        if gk == 1:
            finish(partials())
        else:
            ps = partials()

            @pl.when(k == 0)
            def _():
                for acc, p in zip(acc_refs, ps):
                    acc[...] = p

            @pl.when(k > 0)
            def _():
                for acc, p in zip(acc_refs, ps):
                    acc[...] += p

            @pl.when(k == gk - 1)
            def _():
                finish([acc[...] for acc in acc_refs])

    res, moved = carried_call(
        body, name, grid,
        list(a_specs) + list(b_specs) + list(extra_specs) + ([ANY] if nc else []), list(out_specs), list(outs),
        [pltpu.VMEM(s, F32) for s in acc_shapes] if gk > 1 else [],
        [*a_ops, *b_ops, *extras, *([carry] if nc else [])], {na + nb + ne: 0} if nc else {}, job)
    res = res[0] if len(res) == 1 else res
    return res if job is None else (res, moved)


def sds(shape, dt):
    return jax.ShapeDtypeStruct(shape, dt)


class Job:
    def __init__(self, ins, outs, aliases, scratch, start, mid, finish):
        self.ins, self.outs, self.aliases, self.scratch = list(ins), list(outs), dict(aliases), list(scratch)
        self.start, self.mid, self.finish = start, mid, finish


def carried_call(body, name, grid, in_specs, out_specs, out_shape, scratch, args, aliases, job, mid_at=0.75):
    sem = ("arbitrary",) * len(grid)
    if job is None:
        res = pl.pallas_call(body, name=name, grid=grid, in_specs=in_specs, out_specs=out_specs, out_shape=out_shape,
                             scratch_shapes=scratch, input_output_aliases=aliases, compiler_params=_params(sem))(*args)
        return list(res), []
    ni, no, ns = len(in_specs), len(out_specs), len(scratch)
    ci, co = len(job.ins), len(job.outs)
    total = 1
    for g in grid:
        total *= g
    mid_step = min(max(int(total * mid_at), 1), total - 1)

    def full(*refs):
        ins, cins = refs[:ni], refs[ni:ni + ci]
        outs, couts = refs[ni + ci:ni + ci + no], refs[ni + ci + no:ni + ci + no + co]
        scr, cscr = refs[ni + ci + no + co:ni + ci + no + co + ns], refs[ni + ci + no + co + ns:]
        step = 0
        for d, g in enumerate(grid):
            step = step * g + pl.program_id(d)

        @pl.when(step == 0)
        def _():
            job.start(cins, couts, cscr)

        body(*ins, *outs, *scr)

        @pl.when(step == mid_step)
        def _():
            job.mid(cins, couts, cscr)

        @pl.when(step == total - 1)
        def _():
            job.finish(cins, couts, cscr)

    al = dict(aliases)
    al.update({ni + k: no + v for k, v in job.aliases.items()})
    res = pl.pallas_call(
        full, name=name, grid=grid, in_specs=in_specs + [ANY] * ci, out_specs=out_specs + [ANY] * co,
        out_shape=out_shape + job.outs, scratch_shapes=scratch + job.scratch, input_output_aliases=al,
        compiler_params=_params(sem))(*args, *job.ins)
    return list(res[:no]), list(res[no:])


def _with_moved(res, job):
    return res if job is not None else (res, [])


def run_job(name, job):
    ci = len(job.ins)

    def body(*refs):
        cins, couts, cscr = refs[:ci], refs[ci:ci + len(job.outs)], refs[ci + len(job.outs):]
        job.start(cins, couts, cscr)
        job.mid(cins, couts, cscr)
        job.finish(cins, couts, cscr)

    return list(pl.pallas_call(
        body, name=name, in_specs=[ANY] * ci, out_specs=[ANY] * len(job.outs), out_shape=job.outs,
        scratch_shapes=job.scratch, input_output_aliases=job.aliases)(*job.ins))


def _place():
    x, y, c = lax.axis_index("x"), lax.axis_index("y"), lax.axis_index("c")
    chips = [((1 - x) if fx else x, (1 - y) if fy else y) for fx, fy in FLIPS]
    return x, y, c, chips


def allgather_small(name, v):
    r = v.shape[0]

    def body(x_ref, out_ref, send_sems, recv_sems, local_sem):
        x, y, c, chips = _place()
        me, sibling = (x, y, c), (x, y, 1 - c)

        def rows(px, py, pc):
            return out_ref.at[4 * px + 2 * py + pc]

        def copy(k, block, to, src=None):
            return pltpu.make_async_remote_copy(
                src_ref=rows(*block) if src is None else src, dst_ref=rows(*block),
                send_sem=send_sems.at[k], recv_sem=recv_sems.at[k], device_id=to, device_id_type=MESH)

        mine = pltpu.make_async_copy(x_ref, rows(*me), local_sem)
        mine.start()
        first = [copy(0, me, sibling, src=x_ref)]
        first += [copy(1 + j, me, (*chip, c), src=x_ref) for j, chip in enumerate(chips)]
        for cp in first:
            cp.start()
        passed = [copy(4 + j, (*chip, c), sibling) for j, chip in enumerate(chips)]
        for j, chip in enumerate(chips):
            copy(1 + j, (*chip, c), me).wait_recv()
            passed[j].start()
        copy(0, sibling, me).wait_recv()
        for j, chip in enumerate(chips):
            copy(4 + j, (*chip, 1 - c), me).wait_recv()
        for cp in first + passed:
            cp.wait_send()
        mine.wait()

    return pl.pallas_call(
        body, name=name, out_shape=sds((8, r, 128), v.dtype),
        in_specs=[pl.BlockSpec(memory_space=pltpu.VMEM)], out_specs=pl.BlockSpec(memory_space=pltpu.VMEM),
        scratch_shapes=[pltpu.SemaphoreType.DMA((7,)), pltpu.SemaphoreType.DMA((7,)), pltpu.SemaphoreType.DMA],
    )(v)


def _half(ref, rows, hf):
    hr = rows // 2
    return ref.at[pl.ds(_aligned(hf * hr, 16), hr)]


def view_lead(ref, p):
    return ref.at[p]


def view_ffn_out(ref, p):
    return ref.at[p // 2, pl.ds(_aligned((p % 2) * FO, 16), FO)]


def _remote(ref, dst, send_sems, recv_sems, idx, to):
    return pltpu.make_async_remote_copy(src_ref=ref, dst_ref=dst, send_sem=send_sems.at[idx], recv_sem=recv_sems.at[idx],
                                        device_id=to, device_id_type=MESH)


def gather_job(items):
    nw = len(items)
    pads = [w for w, it in enumerate(items) if it[1] is view_ffn_out]

    def piece(ref, w, p, hf):
        _, view, rws, part, parts = items[w]
        pr = rws // 2 // parts
        return view(ref, p).at[pl.ds(_aligned(hf * (rws // 2) + part * pr, 16), pr)]

    def pad_copies(outs, scr):
        return [pltpu.make_async_copy(scr[2], outs[w].at[h, pl.ds(2 * FO, FHP - 2 * FO)], scr[3].at[2 * n + h])
                for n, w in enumerate(pads) for h in range(2)]

    def start(_, outs, scr):
        x, y, c, chips = _place()
        if pads:
            scr[2][...] = jnp.zeros_like(scr[2])
            for cp in pad_copies(outs, scr):
                cp.start()
        for w in range(nw):
            mine = piece(outs[w], w, 2 * x + y, c)
            for f, (px, py) in enumerate(chips):
                _remote(mine, mine, scr[0], scr[1], (w, f), (px, py, c)).start()

    def mid(_, outs, scr):
        x, y, c, chips = _place()
        for w in range(nw):
            for f, (px, py) in enumerate(chips):
                land = piece(outs[w], w, 2 * px + py, c)
                _remote(land, land, scr[0], scr[1], (w, f), (px, py, c)).wait_recv()
                _remote(land, land, scr[0], scr[1], (w, 3 + f), (x, y, 1 - c)).start()

    def finish(_, outs, scr):
        x, y, c, chips = _place()
        for w in range(nw):
            for f, (px, py) in enumerate(chips):
                land = piece(outs[w], w, 2 * px + py, 1 - c)
                _remote(land, land, scr[0], scr[1], (w, 3 + f), (x, y, 1 - c)).wait_recv()
        for w in range(nw):
            mine = piece(outs[w], w, 2 * x + y, c)
            for f in range(6):
                _remote(mine, mine, scr[0], scr[1], (w, f), (x, y, 1 - c)).wait_send()
        for cp in pad_copies(outs, scr):
            cp.wait()

    scratch = [pltpu.SemaphoreType.DMA((nw, 6)), pltpu.SemaphoreType.DMA((nw, 6))]
    if pads:
        scratch += [pltpu.VMEM((FHP - 2 * FO, D), BF16), pltpu.SemaphoreType.DMA((2 * len(pads),))]
    bufs = [it[0] for it in items]
    return Job(bufs, [sds(b.shape, BF16) for b in bufs], {w: w for w in range(nw)}, scratch, start, mid, finish)


def reduce_sibling_job(items):
    nw = len(items)

    def copies(ins, got, scr):
        x, y, c, _ = _place()
        return [_remote(_half(view(ins[w], p), rws, 1 - c), got[w].at[p], scr[0], scr[1], (w, p), (x, y, 1 - c))
                for w, (_, view, rws, _) in enumerate(items) for p in range(4)]

    def start(ins, got, scr):
        for cp in copies(ins, got, scr):
            cp.start()

    def finish(ins, got, scr):
        for cp in copies(ins, got, scr):
            cp.wait()

    return Job([it[0] for it in items], [sds((4, it[2] // 2, it[3]), BF16) for it in items], {},
               [pltpu.SemaphoreType.DMA((nw, 4)), pltpu.SemaphoreType.DMA((nw, 4))], start, lambda *_: None, finish)


def reduce_chips_job(qs):
    nw = len(qs)

    def copies(ins, got, scr):
        x, y, c, chips = _place()
        return [_remote(ins[w].at[2 * px + py], got[w].at[f], scr[0], scr[1], (w, f), (px, py, c))
                for w in range(nw) for f, (px, py) in enumerate(chips)]

    def start(ins, got, scr):
        for cp in copies(ins, got, scr):
            cp.start()

    def finish(ins, got, scr):
        for cp in copies(ins, got, scr):
            cp.wait()

    return Job(qs, [sds((3,) + q.shape[1:], BF16) for q in qs], {},
               [pltpu.SemaphoreType.DMA((nw, 3)), pltpu.SemaphoreType.DMA((nw, 3))], start, lambda *_: None, finish)


def share_halves_job(gs):
    nw = len(gs)

    def start(_, outs, scr):
        x, y, c, _ = _place()
        for w in range(nw):
            mine = _half(outs[w], gs[w].shape[0], c)
            _remote(mine, mine, scr[0], scr[1], w, (x, y, 1 - c)).start()

    def finish(_, outs, scr):
        x, y, c, _ = _place()
        for w in range(nw):
            mine = _half(outs[w], gs[w].shape[0], c)
            theirs = _half(outs[w], gs[w].shape[0], 1 - c)
            _remote(mine, mine, scr[0], scr[1], w, (x, y, 1 - c)).wait_send()
            _remote(theirs, theirs, scr[0], scr[1], w, (x, y, 1 - c)).wait_recv()

    return Job(gs, [sds(g.shape, F32) for g in gs], {w: w for w in range(nw)},
               [pltpu.SemaphoreType.DMA((nw,)), pltpu.SemaphoreType.DMA((nw,))], start, lambda *_: None, finish)


def rope_tables(t):
    pos = jnp.arange(t, dtype=F32)
    inv_freq = ROPE_THETA ** (-jnp.arange(0, ROT, 2, dtype=F32) / ROT)
    ang = pos[:, None] * inv_freq[None, :]
    cos, sin = jnp.cos(ang), jnp.sin(ang)
    d = jnp.arange(128) % HD
    half = ROT // 2
    cs = jnp.take(cos, d % half, axis=1)
    sn = jnp.take(sin, d % half, axis=1)
    cc = jnp.where(d[None] < ROT, cs, 1.0)
    sa = jnp.where(d[None] < half, -sn, 0.0)
    sb = jnp.where((d[None] >= half) & (d[None] < ROT), sn, 0.0)
    return cc, sa, sb


def _rope(v, cc, sa, sb):
    w = v.shape[1]
    reps = w // 128
    half = ROT // 2
    return (v * jnp.tile(cc, (1, reps)) + pltpu.roll(v, w - half, 1) * jnp.tile(sa, (1, reps))
            + pltpu.roll(v, half, 1) * jnp.tile(sb, (1, reps)))


def _rope_t(dv, cc, sa, sb):
    w = dv.shape[1]
    reps = w // 128
    half = ROT // 2
    return (dv * jnp.tile(cc, (1, reps)) + pltpu.roll(dv * jnp.tile(sa, (1, reps)), half, 1)
            + pltpu.roll(dv * jnp.tile(sb, (1, reps)), w - half, 1))


def pool_fwd(h, b_in, t, tm):
    tm = min(tm, t)
    per = tm // HALO

    def body(prev_ref, cur_ref, b_ref, o_ref, xx):
        i = pl.program_id(0)
        b = b_ref[...]
        xx[pl.ds(0, HALO), :] = jnp.where(i > 0, prev_ref[...] + b, 0.0)
        xx[pl.ds(HALO, tm), :] = cur_ref[...] + b
        tpos = i * tm + lax.broadcasted_iota(jnp.int32, (tm, PG), 0) + 1
        for gi, w in enumerate(POOL_WINDOWS):
            cols = pl.ds(gi * PG, PG)
            acc = xx[pl.ds(HALO, tm), cols]
            for s in range(1, w):
                acc = acc + xx[pl.ds(HALO - s, tm), cols]
            cnt = jnp.minimum(tpos, w).astype(F32)
            o_ref[:, cols] = (acc / cnt - xx[pl.ds(HALO, tm), cols]).astype(o_ref.dtype)

    return pl.pallas_call(
        body, name="pool_fwd", grid=(t // tm,),
        in_specs=[pl.BlockSpec((HALO, PW), lambda i: (jnp.maximum(i * per - 1, 0), 0)),
                  pl.BlockSpec((tm, PW), lambda i: (i, 0)), pl.BlockSpec((1, PW), lambda i: (0, 0))],
        out_specs=pl.BlockSpec((tm, PW), lambda i: (i, 0)), out_shape=sds((t, PW), BF16),
        scratch_shapes=[pltpu.VMEM((tm + HALO, PW), F32)], compiler_params=_params(("arbitrary",)),
    )(h, h, b_in)


def pool_bwd(dpooled, t, tm):
    tm = min(tm, t)
    per = tm // HALO
    nt = t // tm

    def body(cur_ref, nxt_ref, o_ref, db_ref, ee):
        i = pl.program_id(0)
        tpos = i * tm + lax.broadcasted_iota(jnp.int32, (tm, PG), 0) + 1
        for gi, w in enumerate(POOL_WINDOWS):
            cols = pl.ds(gi * PG, PG)
            ee[pl.ds(0, tm), cols] = cur_ref[:, cols] / jnp.minimum(tpos, w).astype(F32)
            ee[pl.ds(tm, HALO), cols] = jnp.where(i < nt - 1, nxt_ref[:, cols] / float(w), 0.0)
        for gi, w in enumerate(POOL_WINDOWS):
            cols = pl.ds(gi * PG, PG)
            acc = ee[pl.ds(0, tm), cols]
            for s in range(1, w):
                acc = acc + ee[pl.ds(s, tm), cols]
            dxp = acc - cur_ref[:, cols]
            o_ref[:, cols] = dxp.astype(o_ref.dtype)
            part = colsum(dxp)

            @pl.when(i == 0)
            def _(cols=cols, part=part):
                db_ref[:, cols] = part

            @pl.when(i > 0)
            def _(cols=cols, part=part):
                db_ref[:, cols] += part

    return pl.pallas_call(
        body, name="pool_bwd", grid=(nt,),
        in_specs=[pl.BlockSpec((tm, PW), lambda i: (i, 0)),
                  pl.BlockSpec((HALO, PW), lambda i: (jnp.minimum((i + 1) * per, t // HALO - 1), 0))],
        out_specs=[pl.BlockSpec((tm, PW), lambda i: (i, 0)), pl.BlockSpec((1, PW), lambda i: (0, 0))],
        out_shape=[sds((t, PW), BF16), sds((1, PW), F32)],
        scratch_shapes=[pltpu.VMEM((tm + HALO, PW), F32)], compiler_params=_params(("arbitrary",)),
    )(dpooled, dpooled)


def _scores(qh, kp, kc, mask_p, mask_c, sink):
    sp = jnp.where(mask_p, lax.dot_general(qh, kp, NT, preferred_element_type=F32), -1e30)
    sc = jnp.where(mask_c, lax.dot_general(qh, kc, NT, preferred_element_type=F32), -1e30)
    m = jnp.maximum(jnp.maximum(jnp.max(sp, axis=-1, keepdims=True), jnp.max(sc, axis=-1, keepdims=True)), sink)
    pp, pc = jnp.exp(sp - m), jnp.exp(sc - m)
    es = jnp.exp(sink - m)
    den = jnp.sum(pp, axis=-1, keepdims=True) + jnp.sum(pc, axis=-1, keepdims=True) + es
    return pp / den, pc / den, es / den


def _masks(n):
    qi = lax.broadcasted_iota(jnp.int32, (BLK, BLK), 0)
    kj = lax.broadcasted_iota(jnp.int32, (BLK, BLK), 1)
    return (kj > qi) & (n > 0), kj <= qi


def attn_fwd(q, k, v, sinks, t, job=None):
    def body(s_ref, q_ref, kp_ref, kc_ref, vp_ref, vc_ref, o_ref):
        n = pl.program_id(0)
        mask_p, mask_c = _masks(n)
        for h in range(N_Q):
            kv = pl.ds(HD * (h // (N_Q // N_KV)), HD)
            hq = pl.ds(HD * h, HD)
            pp, pc, _ = _scores(q_ref[:, hq], kp_ref[:, kv], kc_ref[:, kv], mask_p, mask_c, s_ref[0, h])
            o = (lax.dot_general(pp.astype(BF16), vp_ref[:, kv], NN, preferred_element_type=F32)
                 + lax.dot_general(pc.astype(BF16), vc_ref[:, kv], NN, preferred_element_type=F32))
            o_ref[:, hq] = o.astype(o_ref.dtype)

    prev = lambda n: (jnp.maximum(n - 1, 0), 0)
    cur = lambda n: (n, 0)
    res, moved = carried_call(
        body, "attn_fwd", (t // BLK,),
        [pl.BlockSpec(memory_space=pltpu.SMEM), pl.BlockSpec((BLK, QW), cur),
         pl.BlockSpec((BLK, KVW), prev), pl.BlockSpec((BLK, KVW), cur),
         pl.BlockSpec((BLK, KVW), prev), pl.BlockSpec((BLK, KVW), cur)],
        [pl.BlockSpec((BLK, QW), cur)], [sds((t, QW), BF16)], [], [sinks, q, k, k, v, v], {}, job)
    return res[0], moved


def attn_bwd(q, k, v, do, sinks, t):
    nb = t // BLK
    grp = N_Q // N_KV

    def body(s_ref, q_ref, do_ref, kp_ref, kc_ref, vp_ref, vc_ref, dq_ref, dk_ref, dv_ref, ds_ref, dkc, dvc):
        n = pl.program_id(0)

        @pl.when(n == 0)
        def _():
            dkc[...] = jnp.zeros_like(dkc)
            dvc[...] = jnp.zeros_like(dvc)
            ds_ref[...] = jnp.zeros_like(ds_ref)

        @pl.when(n < nb)
        def _():
            mask_p, mask_c = _masks(n)
            lane = lax.broadcasted_iota(jnp.int32, (1, 128), 1)
            dsink = jnp.zeros((1, 128), F32)
            for hk in range(N_KV):
                kv = pl.ds(HD * hk, HD)
                kp, kc, vp, vc = kp_ref[:, kv], kc_ref[:, kv], vp_ref[:, kv], vc_ref[:, kv]
                dkp = jnp.zeros((BLK, HD), F32)
                dkn = jnp.zeros((BLK, HD), F32)
                dvp = jnp.zeros((BLK, HD), F32)
                dvn = jnp.zeros((BLK, HD), F32)
                for g in range(grp):
                    h = grp * hk + g
                    hq = pl.ds(HD * h, HD)
                    qh, doh = q_ref[:, hq], do_ref[:, hq]
                    pp, pc, ps = _scores(qh, kp, kc, mask_p, mask_c, s_ref[0, h])
                    dpp = lax.dot_general(doh, vp, NT, preferred_element_type=F32)
                    dpc = lax.dot_general(doh, vc, NT, preferred_element_type=F32)
                    delta = jnp.sum(pp * dpp, axis=-1, keepdims=True) + jnp.sum(pc * dpc, axis=-1, keepdims=True)
                    dsp = (pp * (dpp - delta)).astype(BF16)
                    dsc = (pc * (dpc - delta)).astype(BF16)
                    dsink = dsink + jnp.where(lane == h, -jnp.sum(ps * delta), 0.0)
                    dq_ref[:, hq] = (lax.dot_general(dsp, kp, NN, preferred_element_type=F32)
                                     + lax.dot_general(dsc, kc, NN, preferred_element_type=F32))
                    dkp = dkp + lax.dot_general(dsp, qh, TN, preferred_element_type=F32)
                    dkn = dkn + lax.dot_general(dsc, qh, TN, preferred_element_type=F32)
                    dvp = dvp + lax.dot_general(pp.astype(BF16), doh, TN, preferred_element_type=F32)
                    dvn = dvn + lax.dot_general(pc.astype(BF16), doh, TN, preferred_element_type=F32)
                dk_ref[:, kv] = dkc[:, kv] + dkp
                dv_ref[:, kv] = dvc[:, kv] + dvp
                dkc[:, kv] = dkn
                dvc[:, kv] = dvn
            ds_ref[...] += dsink

        @pl.when(n == nb)
        def _():
            dk_ref[...] = dkc[...]
            dv_ref[...] = dvc[...]

    cur = lambda n: (jnp.minimum(n, nb - 1), 0)
    prev = lambda n: (jnp.clip(n - 1, 0, nb - 1), 0)
    return pl.pallas_call(
        body, name="attn_bwd", grid=(nb + 1,),
        in_specs=[pl.BlockSpec(memory_space=pltpu.SMEM), pl.BlockSpec((BLK, QW), cur), pl.BlockSpec((BLK, QW), cur),
                  pl.BlockSpec((BLK, KVW), prev), pl.BlockSpec((BLK, KVW), cur),
                  pl.BlockSpec((BLK, KVW), prev), pl.BlockSpec((BLK, KVW), cur)],
        out_specs=[pl.BlockSpec((BLK, QW), cur), pl.BlockSpec((BLK, KVW), prev), pl.BlockSpec((BLK, KVW), prev),
                   pl.BlockSpec((1, 128), lambda n: (0, 0))],
        out_shape=[sds((t, QW), F32), sds((t, KVW), F32), sds((t, KVW), F32), sds((1, 128), F32)],
        scratch_shapes=[pltpu.VMEM((BLK, KVW), F32), pltpu.VMEM((BLK, KVW), F32)],
        compiler_params=_params(("arbitrary",)),
    )(sinks, q, do, k, k, v, v)


def _adamw(w, g, m, v):
    m2 = B1 * m + (1.0 - B1) * g
    v2 = B2 * v + (1.0 - B2) * jnp.square(g)
    m_hat = m2 / (1.0 - B1 ** STEP)
    v_hat = v2 / (1.0 - B2 ** STEP)
    return -LR * (m_hat / (jnp.sqrt(v_hat) + EPS) + WD * w), m2, v2


def ada_fwd(c16, w_ada, b_sh):
    tn = 512

    def body(c_ref, w_ref, b_ref, o_ref):
        cv = c_ref[...]
        sc = (cv * _sigmoid(cv)).astype(BF16)
        o_ref[...] = lax.dot_general(sc, w_ref[...].astype(BF16), NN, preferred_element_type=F32) + b_ref[...]

    return pl.pallas_call(
        body, name="ada_fwd", grid=(ADA_SH // tn,),
        in_specs=[pl.BlockSpec((16, D), lambda j: (0, 0)), pl.BlockSpec((D, tn), lambda j: (0, j)),
                  pl.BlockSpec((1, tn), lambda j: (0, j))],
        out_specs=pl.BlockSpec((16, tn), lambda j: (0, j)), out_shape=sds((16, ADA_SH), F32),
        compiler_params=_params(("arbitrary",)),
    )(c16, w_ada, b_sh)


def ada_bwd_adam(c16, gm16, w, m, v):
    tm, tn = 256, 512

    def body(c_ref, g_ref, w_ref, m_ref, v_ref, go_ref, d_ref, mo_ref, vo_ref):
        cv = c_ref[...]
        sc = (cv * _sigmoid(cv)).astype(BF16)
        g = lax.dot_general(sc, g_ref[...].astype(BF16), TN, preferred_element_type=F32)
        dl, m2, v2 = _adamw(w_ref[...], g, m_ref[...], v_ref[...])
        go_ref[...] = g
        d_ref[...] = dl
        mo_ref[...] = m2
        vo_ref[...] = v2

    blk = pl.BlockSpec((tm, tn), lambda i, j: (i, j))
    return pl.pallas_call(
        body, name="ada_bwd_adam", grid=(D // tm, ADA_SH // tn),
        in_specs=[pl.BlockSpec((16, tm), lambda i, j: (0, i)), pl.BlockSpec((16, tn), lambda i, j: (0, j)), blk, blk, blk],
        out_specs=[blk] * 4, out_shape=[sds((D, ADA_SH), F32)] * 4,
        compiler_params=_params(("arbitrary", "arbitrary")),
    )(c16, gm16, w, m, v)


def adam_rows(name, w, g, m, v, tm):
    rows, cols = w.shape

    def fn(wv, gv, mv, vv):
        gv = gv[:, :cols]
        dl, m2, v2 = _adamw(wv, gv, mv, vv)
        return gv, dl, m2, v2

    return rowmap(name, fn, [T_(w), T_(g), T_(m), T_(v)], [(cols, F32)] * 4, rows=rows, tm=tm)


def adam_small(name, w, g, m, v):
    def body(w_ref, g_ref, m_ref, v_ref, d_ref, mo_ref, vo_ref):
        dl, m2, v2 = _adamw(w_ref[...], g_ref[...], m_ref[...], v_ref[...])
        d_ref[...] = dl
        mo_ref[...] = m2
        vo_ref[...] = v2

    return pl.pallas_call(body, name=name, out_shape=[sds(w.shape, F32)] * 3)(w, g, m, v)


def sum_devices(allv):
    def body(a_ref, o_ref):
        acc = a_ref[0]
        for d in range(1, 8):
            acc = acc + a_ref[d]
        o_ref[...] = acc

    return pl.pallas_call(body, name="sum_devices", out_shape=sds(allv.shape[1:], F32))(allv)


def _ln_fwd(z, g, b):
    mu = jnp.mean(z, axis=-1, keepdims=True)
    zc = z - mu
    var = jnp.mean(jnp.square(zc), axis=-1, keepdims=True)
    return zc * lax.rsqrt(var + LN_EPS) * g + b


def _ln_bwd(z, g, dout):
    mu = jnp.mean(z, axis=-1, keepdims=True)
    zc = z - mu
    var = jnp.mean(jnp.square(zc), axis=-1, keepdims=True)
    rstd = lax.rsqrt(var + LN_EPS)
    xh = zc * rstd
    dxh = dout * g
    dz = rstd * (dxh - jnp.mean(dxh, axis=-1, keepdims=True) - xh * jnp.mean(dxh * xh, axis=-1, keepdims=True))
    return dz, colsum(dout * xh), colsum(dout)


def modulate(name, xin, shift, scale, t):
    return rowmap(name, lambda xv, sh, sc: xv * (1.0 + sc) + sh, [T_(xin), B_(shift), B_(scale)], [(D, BF16)],
                  rows=t, tm=512)


def residual_ln_mod(name, xin, y, gate, lg, lb, wgt, shift_n, scale_n, t):
    def fn(xv, yv, gt, g, b, sh, sc):
        z = ALPHA * xv + (wgt * (1.0 + gt)) * yv
        xo = _ln_fwd(z, g, b)
        return xo, z, xo * (1.0 + sc) + sh

    return rowmap(name, fn, [T_(xin), T_(y), B_(gate), B_(lg), B_(lb), B_(shift_n), B_(scale_n)],
                  [(D, F32), (D, F32), (D, BF16)], rows=t, tm=256)


def residual_ln_bwd(name, z, dnext, y, gate, lg, wgt, t):
    dzn, dun, xn, scn = dnext

    def fn(zv, yv, gt, g, dzv, duv, xv, sc):
        dv = ALPHA * dzv + duv * (1.0 + sc)
        dz, dg, db = _ln_bwd(zv, g, dv)
        return dz, (wgt * (1.0 + gt)) * dz, dg, db, colsum(wgt * dz * yv), colsum(duv), colsum(duv * xv)

    return rowmap(name, fn, [T_(z), T_(y), B_(gate), B_(lg), T_(dzn), T_(dun), T_(xn), B_(scn)],
                  [(D, F32), (D, BF16)], [(1, D)] * 5, rows=t, tm=256)


def residual_ln_loss_bwd(name, xin, y, tgt, gate, lg, lb, wgt, t):
    def fn(xv, yv, tv, gt, g, b):
        z = ALPHA * xv + (wgt * (1.0 + gt)) * yv
        d = _ln_fwd(z, g, b) - tv
        dz, dg, db = _ln_bwd(z, g, d * (1.0 / D))
        return dz, (wgt * (1.0 + gt)) * dz, dg, db, colsum(wgt * dz * yv), jnp.sum(d * d).reshape(1, 1)

    dz, dy, dlg, dlb, dgate, sq = rowmap(
        name, fn, [T_(xin), T_(y), T_(tgt), B_(gate), B_(lg), B_(lb)], [(D, F32), (D, BF16)],
        [(1, D), (1, D), (1, D), (1, 1)], rows=t, tm=256)
    return dz, dy, dlg, dlb, dgate, sq


def modulate_bwd(name, dz, du, xin, scale, t):
    def fn(dzv, duv, xv, sc):
        return ALPHA * dzv + duv * (1.0 + sc), colsum(duv), colsum(duv * xv)

    return rowmap(name, fn, [T_(dz), T_(du), T_(xin), B_(scale)], [(D, F32)], [(1, D), (1, D)], rows=t, tm=256)


def ffn_fwd(tag, u, wi, t, up_job, down_job=None):
    tm = min(1024, t)
    tn = 256
    per = FHP // tn

    def act(accs, _):
        a, b = accs
        return a, b, a * _sigmoid(a) * b

    hblk = pl.BlockSpec((tm, tn), lambda i, j, k: (i, j))
    (ha, hb, g), up_moved = mm(
        tag + "_up", [u], [wi, wi], [(0, 0, 0), (0, 1, 1)], dims=NT, grid=(t // tm, 2 * per, 1),
        a_specs=[pl.BlockSpec((tm, D), lambda i, j, k: (i, 0))],
        b_specs=[pl.BlockSpec((None, tn, D), lambda i, j, k: (j // per, j % per, 0)),
                 pl.BlockSpec((None, tn, D), lambda i, j, k: (2 + j // per, j % per, 0))],
        outs=[sds((t, 2 * FHP), BF16)] * 3, out_specs=[hblk] * 3, acc_shapes=[(tm, tn)] * 2, epilogue=act, job=up_job)
    wo = up_moved[0].reshape(2 * FHP, D)
    tk = FHP // 2
    y, down_moved = _with_moved(mm(
        tag + "_down", [g], [wo], [(0, 0, 0)], dims=NN, grid=(t // tm, 2, 4),
        a_specs=[pl.BlockSpec((tm, tk), lambda i, j, k: (i, k))],
        b_specs=[pl.BlockSpec((tk, D // 2), lambda i, j, k: (k, j))],
        outs=[sds((t, D), F32)], out_specs=[pl.BlockSpec((tm, D // 2), lambda i, j, k: (i, j))],
        acc_shapes=[(tm, D // 2)], job=down_job), down_job)
    return ha, hb, g, y, wo, up_moved, down_moved


def ffn_bwd(tag, u, ha, hb, g, dy, wi, wo, t, sp, dact_job=None, dwo_job=None):
    tm = min(1024, t)

    def dact(accs, ex):
        dg = accs[0]
        a, b = ex[0].astype(F32), ex[1].astype(F32)
        s = _sigmoid(a)
        return dg * b * (s * (1.0 + a * (1.0 - s))), dg * (a * s)

    tn = 256
    hblk = pl.BlockSpec((tm, tn), lambda i, j, k: (i, j))
    (dha, dhb), dact_moved = _with_moved(mm(
        tag + "_dact", [dy], [wo], [(0, 0, 0)], dims=NT, grid=(t // tm, 2 * FHP // tn, 1),
        a_specs=[pl.BlockSpec((tm, D), lambda i, j, k: (i, 0))],
        b_specs=[pl.BlockSpec((tn, D), lambda i, j, k: (j, 0))],
        outs=[sds((t, 2 * FHP), BF16)] * 2, out_specs=[hblk] * 2, acc_shapes=[(tm, tn)],
        epilogue=dact, extras=[ha, hb], extra_specs=[hblk] * 2, job=dact_job), dact_job)
    tk = min(1024, t)
    th = FHP // 2
    dwo, dwo_moved = _with_moved(mm(
        tag + "_dwo", [g], [dy], [(0, 0, 0)], dims=TN, grid=(4, 2, t // tk),
        a_specs=[pl.BlockSpec((tk, th), lambda i, j, k: (k, i))],
        b_specs=[pl.BlockSpec((tk, D // 2), lambda i, j, k: (k, j))],
        outs=[sds((2 * FHP, D), BF16)], out_specs=[pl.BlockSpec((th, D // 2), lambda i, j, k: (i, j))],
        acc_shapes=[(th, D // 2)], job=dwo_job), dwo_job)
    dwo = dwo.reshape(2, FHP, D)

    def dwi_part(part, dh, carry, job):
        return mm(
            f"{tag}_dwi{part}", [dh], [u], [(0, 0, 0)], dims=TN, grid=(4, 2, t // tk),
            a_specs=[pl.BlockSpec((tk, th), lambda i, j, k: (k, i))],
            b_specs=[pl.BlockSpec((tk, D // 2), lambda i, j, k: (k, j))],
            outs=[sds((4, FHP, D), BF16)],
            out_specs=[pl.BlockSpec((None, th, D // 2), lambda i, j, k: (2 * part + i // 2, i % 2, j))],
            acc_shapes=[(th, D // 2)], carry=carry, job=job)

    dwi, (sib_fo,) = dwi_part(0, dha, None, reduce_sibling_job([(dwo, view_ffn_out, FO, D)]))
    q_fo = chip_sum(tag + "_chipsum_fo", dwo, sib_fo, sp, FO, FO // 2, ffn_out=True)
    dwi, (far_fo,) = dwi_part(1, dhb, dwi, reduce_chips_job([q_fo]))
    (sib_fi,) = run_job(tag + "_sibling_fi", reduce_sibling_job([(dwi, view_lead, FHP, D)]))
    q_fi = chip_sum(tag + "_chipsum_fi", dwi, sib_fi, sp, FHP, FHP // 8)
    du, (far_fi,) = mm(
        tag + "_du", [dha, dhb], [wi, wi], [(0, 0, 0), (1, 1, 0)], dims=NN, grid=(t // tm, 2, 4),
        a_specs=[pl.BlockSpec((tm, th), lambda i, j, k: (i, k))] * 2,
        b_specs=[pl.BlockSpec((None, th, D // 2), lambda i, j, k: (k // 2, k % 2, j)),
                 pl.BlockSpec((None, th, D // 2), lambda i, j, k: (2 + k // 2, k % 2, j))],
        outs=[sds((t, D), F32)], out_specs=[pl.BlockSpec((tm, D // 2), lambda i, j, k: (i, j))],
        acc_shapes=[(tm, D // 2)], job=reduce_chips_job([q_fi]))
    return du, (q_fi, far_fi), (q_fo, far_fo), dact_moved, dwo_moved


def mix_fwd(u, wts, b_in, pool_scale, sinks, tabs, t, in_job, attn_job):
    w_in, wp, wba, wbb, wo = wts
    tm = min(1024, t)
    tmh = min(512, t)
    h, in_moved = mm("mix_in", [u], [w_in], [(0, 0, 0)], dims=NN, grid=(t // tmh, 4, 1),
                     a_specs=[pl.BlockSpec((tmh, D), lambda i, j, k: (i, 0))],
                     b_specs=[pl.BlockSpec((None, D, IN_SH), lambda i, j, k: (j, 0, 0))],
                     outs=[sds((t, IN_W), F32)], out_specs=[pl.BlockSpec((tmh, IN_SH), lambda i, j, k: (i, j))],
                     acc_shapes=[(tmh, IN_SH)], job=in_job)
    attn_job = attn_job(in_moved)
    pooled = pool_fwd(h, b_in, t, 512)
    gblk = pl.BlockSpec((tm, PG), lambda i, j, k: (i, j))
    mixed = mm("mix_pool", [pooled], [wp], [(0, 0, 0)], dims=NN, grid=(t // tm, 4, 1), a_specs=[gblk],
               b_specs=[pl.BlockSpec((None, PG, PG), lambda i, j, k: (j, 0, 0))],
               outs=[sds((t, PW), F32)], out_specs=[gblk], acc_shapes=[(tm, PG)])
    pm = rowmap("mix_pscale", lambda mv, ps: mv * ps, [T_(mixed), B_(pool_scale)], [(PW, BF16)], rows=t, tm=512)

    def branch(name, a, w):
        return mm(name, [a], [w], [(0, 0, 0)], dims=NN, grid=(t // tm, 4, 1),
                  a_specs=[pl.BlockSpec((tm, PW), lambda i, j, k: (i, 0))],
                  b_specs=[pl.BlockSpec((None, PW, D // 4), lambda i, j, k: (j, 0, 0))],
                  outs=[sds((t, D), F32)], out_specs=[pl.BlockSpec((tm, D // 4), lambda i, j, k: (i, j))],
                  acc_shapes=[(tm, D // 4)])

    ya = branch("mix_branch_a", pm, wba)

    def qkv(hq, hk, hv, bq, bk, bv, cc, sa, sb):
        return (_rope(hq + bq, cc, sa, sb) * (HD ** -0.5), _rope(hk + bk, cc, sa, sb), hv + bv)

    qr, kr, vv = rowmap(
        "mix_rope", qkv,
        [T_(h, QW, 1), T_(h, KVW, 8), T_(h, KVW, 9), B_(b_in, QW, 1), B_(b_in, KVW, 8), B_(b_in, KVW, 9),
         T_(tabs[0]), T_(tabs[1]), T_(tabs[2])],
        [(QW, BF16), (KVW, BF16), (KVW, BF16)], rows=t, tm=512)
    attn, attn_moved = attn_fwd(qr, kr, vv, sinks, t, attn_job)
    yb = branch("mix_branch_b", attn, wbb)
    cw = 512

    def merge(ga, gb, ba, bb, yav, ybv):
        return _sigmoid(ga + ba) * yav + _sigmoid(gb + bb) * ybv

    merged = rowmap(
        "mix_merge", merge,
        [T_(h, cw, 5), T_(h, cw, 9), B_(b_in, cw, 5), B_(b_in, cw, 9), T_(ya, cw), T_(yb, cw)],
        [(D, BF16)], rows=t, tm=512, ncol=D // cw)
    y = mm("mix_out", [merged], [wo], [(0, 0, 0)], dims=NN, grid=(t // tm, 2, 1),
           a_specs=[pl.BlockSpec((tm, D), lambda i, j, k: (i, 0))],
           b_specs=[pl.BlockSpec((D, D // 2), lambda i, j, k: (0, j))],
           outs=[sds((t, D), F32)], out_specs=[pl.BlockSpec((tm, D // 2), lambda i, j, k: (i, j))],
           acc_shapes=[(tm, D // 2)])
    return y, (h, pooled, mixed, pm, ya, qr, kr, vv, attn, yb, merged), attn_moved


def mix_bwd(u, saved, dy, wts, b_in, pool_scale, sinks, tabs, t):
    h, pooled, mixed, pm, ya, qr, kr, vv, attn, yb, merged = saved
    w_in, wp, wba, wbb, wo = wts
    tm = min(1024, t)
    tk = min(1024, t)
    dmerged = mm("mix_dmerged", [dy], [wo], [(0, 0, 0)], dims=NT, grid=(t // tm, 2, 1),
                 a_specs=[pl.BlockSpec((tm, D), lambda i, j, k: (i, 0))],
                 b_specs=[pl.BlockSpec((D // 2, D), lambda i, j, k: (j, 0))],
                 outs=[sds((t, D), F32)], out_specs=[pl.BlockSpec((tm, D // 2), lambda i, j, k: (i, j))],
                 acc_shapes=[(tm, D // 2)])
    half = pl.BlockSpec((tk, D // 2), lambda i, j, k: (k, i))
    dwo = mm("mix_dwo", [merged], [dy], [(0, 0, 0)], dims=TN, grid=(2, 2, t // tk), a_specs=[half],
             b_specs=[pl.BlockSpec((tk, D // 2), lambda i, j, k: (k, j))],
             outs=[sds((D, D), BF16)], out_specs=[pl.BlockSpec((D // 2, D // 2), lambda i, j, k: (i, j))],
             acc_shapes=[(D // 2, D // 2)])
    cw = 512

    def dmerge(dm, ga, gb, ba, bb, yav, ybv):
        sa_, sb_ = _sigmoid(ga + ba), _sigmoid(gb + bb)
        dga = dm * yav * sa_ * (1.0 - sa_)
        dgb = dm * ybv * sb_ * (1.0 - sb_)
        return dm * sa_, dm * sb_, dga, dgb, colsum(dga), colsum(dgb)

    dya, dyb, dgla, dglb, dbga, dbgb = rowmap(
        "mix_dmerge", dmerge,
        [T_(dmerged, cw), T_(h, cw, 5), T_(h, cw, 9), B_(b_in, cw, 5), B_(b_in, cw, 9), T_(ya, cw), T_(yb, cw)],
        [(D, BF16)] * 4, [(1, D), (1, D)], rows=t, tm=512, ncol=D // cw)

    def dbranch(name, dyv, act, w):
        dwb = mm(name + "_dw", [act], [dyv], [(0, 0, 0)], dims=TN, grid=(1, 4, t // tk),
                 a_specs=[pl.BlockSpec((tk, PW), lambda i, j, k: (k, 0))],
                 b_specs=[pl.BlockSpec((tk, D // 4), lambda i, j, k: (k, j))],
                 outs=[sds((4, PW, D // 4), BF16)], out_specs=[pl.BlockSpec((None, PW, D // 4), lambda i, j, k: (j, 0, 0))],
                 acc_shapes=[(PW, D // 4)])
        return dwb, lambda dt: mm(
            name + "_dx", [dyv], [w], [(0, 0, 0)], dims=NT, grid=(t // tm, 1, 4),
            a_specs=[pl.BlockSpec((tm, D // 4), lambda i, j, k: (i, k))],
            b_specs=[pl.BlockSpec((None, PW, D // 4), lambda i, j, k: (k, 0, 0))],
            outs=[sds((t, PW), dt)], out_specs=[pl.BlockSpec((tm, PW), lambda i, j, k: (i, 0))], acc_shapes=[(tm, PW)])

    dwba, dpm_fn = dbranch("mix_dbranch_a", dya, pm, wba)
    dwbb, dattn_fn = dbranch("mix_dbranch_b", dyb, attn, wbb)
    dpm, dattn = dpm_fn(F32), dattn_fn(BF16)
    dmixed, dps = rowmap("mix_dpscale", lambda dp, mv, ps: (dp * ps, colsum(dp * mv)),
                         [T_(dpm), T_(mixed), B_(pool_scale)], [(PW, BF16)], [(1, PW)], rows=t, tm=512)
    gblk = pl.BlockSpec((tm, PG), lambda i, j, k: (i, j))
    dpooled = mm("mix_dpool", [dmixed], [wp], [(0, 0, 0)], dims=NT, grid=(t // tm, 4, 1), a_specs=[gblk],
                 b_specs=[pl.BlockSpec((None, PG, PG), lambda i, j, k: (j, 0, 0))],
                 outs=[sds((t, PW), F32)], out_specs=[gblk], acc_shapes=[(tm, PG)])
    kblk = pl.BlockSpec((tk, PG), lambda i, j, k: (k, i))
    dwp = mm("mix_dwpool", [pooled], [dmixed], [(0, 0, 0)], dims=TN, grid=(4, 1, t // tk), a_specs=[kblk], b_specs=[kblk],
             outs=[sds((4, PG, PG), BF16)], out_specs=[pl.BlockSpec((None, PG, PG), lambda i, j, k: (i, 0, 0))],
             acc_shapes=[(PG, PG)])
    dxp, dbxp = pool_bwd(dpooled, t, 512)
    dqr, dkr, dvv, dsinks = attn_bwd(qr, kr, vv, dattn, sinks, t)

    def dqkv(dq, dk, dv, cc, sa, sb):
        dq = _rope_t(dq, cc, sa, sb) * (HD ** -0.5)
        dk = _rope_t(dk, cc, sa, sb)
        return dq, dk, dv, colsum(dq), colsum(dk), colsum(dv)

    dq, dk, dvb, dbq, dbk, dbv = rowmap(
        "mix_rope_bwd", dqkv, [T_(dqr), T_(dkr), T_(dvv), T_(tabs[0]), T_(tabs[1]), T_(tabs[2])],
        [(QW, BF16), (KVW, BF16), (KVW, BF16)], [(1, QW), (1, KVW), (1, KVW)], rows=t, tm=512)
    dh = jnp.concatenate([dxp, dq, dk, dvb, dgla, dglb], axis=1)
    db_in = jnp.concatenate([dbxp, dbq, dbk, dbv, dbga, dbgb], axis=1)
    dwin = mm("mix_dwin", [u], [dh], [(0, 0, 0)], dims=TN, grid=(2, 4, t // tk), a_specs=[half],
              b_specs=[pl.BlockSpec((tk, IN_SH), lambda i, j, k: (k, j))],
              outs=[sds((4, D, IN_SH), BF16)], out_specs=[pl.BlockSpec((None, D // 2, IN_SH), lambda i, j, k: (j, i, 0))],
              acc_shapes=[(D // 2, IN_SH)])
    du = mm("mix_du", [dh], [w_in], [(0, 0, 0)], dims=NT, grid=(t // tm, 2, 4),
            a_specs=[pl.BlockSpec((tm, IN_SH), lambda i, j, k: (i, k))],
            b_specs=[pl.BlockSpec((None, D // 2, IN_SH), lambda i, j, k: (k, j, 0))],
            outs=[sds((t, D), F32)], out_specs=[pl.BlockSpec((tm, D // 2), lambda i, j, k: (i, j))],
            acc_shapes=[(tm, D // 2)])
    return du, (dwin, dwp, dwba, dwbb, dwo), db_in, dps, dsinks


def cast_shard(name, w, sp, ffn_out=False):
    rows, cols = w.shape
    if ffn_out:
        tm = rows // 2
        shape = (2, FHP, D)
        spec = pl.BlockSpec((None, tm, cols), lambda j, i, s: (s[0] // 2, (s[0] % 2) * 2 + i, 0))
    else:
        tm = rows // 4
        shape = (4, rows, cols)
        spec = pl.BlockSpec((None, tm, cols), lambda j, i, s: (s[0], i, 0))
    return rowmap(name, lambda wv: wv, [T_(w)], [(shape, BF16, spec)], rows=rows, tm=tm, sp=sp)


def cast_ffn_in(name, wt, sp):
    tm = 64
    full = FH // tm

    def fn(_, i, wv):
        return jnp.where(i < full, wv, 0.0)

    return rowmap(name, fn, [X_(wt, pl.BlockSpec((tm, D), lambda j, i, s: (jnp.minimum(i, full - 1), 0)))],
                  [((4, FHP, D), BF16, pl.BlockSpec((None, tm, D), lambda j, i, s: (s[0], i, 0)))],
                  rows=FHP, tm=tm, sp=sp, with_ids=True)


def chip_sum(name, dw, got, sp, rows, tm, ffn_out=False):
    hr, cols = rows // 2, got.shape[2]
    per = hr // tm
    pos = pl.BlockSpec((None, tm, cols), lambda j, i, s: (i // per, i % per, 0))
    if ffn_out:
        mine = pl.BlockSpec((None, tm, cols), lambda j, i, s: (i // 2, (i % 2) * 2 + s[1], 0))
    else:
        mine = pl.BlockSpec((None, tm, cols), lambda j, i, s: (i // per, s[1] * per + i % per, 0))
    return rowmap(name, lambda av, bv: av.astype(F32) + bv.astype(F32), [X_(dw, mine), X_(got, pos)],
                  [(got.shape, BF16, pos)], rows=4 * hr, tm=tm, sp=sp)


def chip_total(name, q, got, sp, rows, tm):
    hr, cols = rows // 2, q.shape[2]
    per = hr // tm

    def part(f):
        return X_(got, pl.BlockSpec((None, tm, cols), lambda j, i, s, f=f: (f, i, 0)))

    return rowmap(
        name, lambda av, b0, b1, b2: ((av.astype(F32) + b0.astype(F32)) + b1.astype(F32)) + b2.astype(F32),
        [X_(q, pl.BlockSpec((None, tm, cols), lambda j, i, s: (s[0], i, 0))), part(0), part(1), part(2)],
        [((rows, cols), F32, pl.BlockSpec((tm, cols), lambda j, i, s: (s[1] * per + i, 0)))], rows=hr, tm=tm, sp=sp)


def kernel(x, c, w_ada, b_ada, ln_g, ln_b, w_ffn1_in, w_ffn1_out, w_in, b_in, w_pool, pool_scale, sinks, w_branch_a, w_branch_b, w_out, w_ffn2_in, w_ffn2_out, loss_target, m_w_ada, m_b_ada, m_ln_g, m_ln_b, m_w_ffn1_in, m_w_ffn1_out, m_w_in, m_b_in, m_w_pool, m_pool_scale, m_sinks, m_w_branch_a, m_w_branch_b, m_w_out, m_w_ffn2_in, m_w_ffn2_out, v_w_ada, v_b_ada, v_ln_g, v_ln_b, v_w_ffn1_in, v_w_ffn1_out, v_w_in, v_b_in, v_w_pool, v_pool_scale, v_sinks, v_w_branch_a, v_w_branch_b, v_w_out, v_w_ffn2_in, v_w_ffn2_out):
    t = x.shape[1]
    xs, tgt = x[0], loss_target[0]
    xi, yi, ci = lax.axis_index("x"), lax.axis_index("y"), lax.axis_index("c")
    chip = 2 * xi + yi
    dev = 2 * chip + ci
    b_in2, ps2, sinks2 = b_in, pool_scale, sinks

    first = jnp.concatenate([c.reshape(-1), ln_g.reshape(-1), ln_b.reshape(-1)]).reshape(-1, 128)
    first_all = allgather_small("gather_cond", first).reshape(8, -1)
    c_all = first_all[:, :D]
    ln_parts = first_all[0::2, D:].reshape(4, 2, 3, D // 4)
    ln_full = jnp.transpose(ln_parts, (1, 2, 0, 3)).reshape(2, 3, D)
    lgs = [ln_full[0, s:s + 1] for s in range(3)]
    lbs = [ln_full[1, s:s + 1] for s in range(3)]
    c16 = jnp.pad(c_all, ((0, 8), (0, 0)))
    b_ada_sh = lax.dynamic_slice(b_ada, (0, chip * ADA_SH), (1, ADA_SH))
    mod_part = ada_fwd(c16, w_ada[0], b_ada_sh)[:8]
    mod_all = allgather_small("gather_mod", mod_part.reshape(-1, 128)).reshape(8, 8, ADA_SH)
    mod_mine = lax.dynamic_index_in_dim(mod_all[0::2], dev, axis=1, keepdims=False).reshape(9, D)
    mods = [[mod_mine[3 * s + k:3 * s + k + 1] for k in range(3)] for s in range(3)]

    plain = [("f1o", w_ffn1_out[0]), ("win", w_in[0]), ("wp", w_pool[0].reshape(4 * 64, PG)), ("wba", w_branch_a[0]),
             ("wbb", w_branch_b[0]), ("wo", w_out[0]), ("f2o", w_ffn2_out[0])]
    sp = jnp.stack([chip, ci]).astype(jnp.int32)
    sh = {n: cast_shard("cast_" + n, w, sp, ffn_out=n in ("f1o", "f2o")) for n, w in plain}
    tr = lambda a: jnp.swapaxes(a[0], 0, 1)
    sh["f1i"] = cast_ffn_in("cast_f1i", tr(w_ffn1_in), sp)
    sh["f2i"] = cast_ffn_in("cast_f2i", tr(w_ffn2_in), sp)
    order = ["f1i", "f1o", "win", "wp", "wba", "wbb", "wo", "f2i", "f2o"]
    views = {n: (view_ffn_out if n in ("f1o", "f2o") else view_lead) for n in order}
    shard_rows = {n: (FO if n in ("f1o", "f2o") else sh[n].shape[1]) for n in order}
    shard_cols = {n: sh[n].shape[2] for n in order}
    tiles = {"f1i": FHP // 8, "f1o": FO // 2, "win": 512, "wp": 128, "wba": 512, "wbb": 512, "wo": 256,
             "f2i": FHP // 8, "f2o": FO // 2}

    def item(n, part=0, parts=1):
        return (sh[n], views[n], shard_rows[n], part, parts)

    tabs = rope_tables(t)
    (sh0, sc0, gt0), (sh1, sc1, gt1), (sh2, sc2, gt2) = mods

    (g_f1i,) = run_job("gather_f1i", gather_job([item("f1i")]))
    u0 = modulate("ffn1_mod", xs, sh0, sc0, t)
    ha1, hb1, g1, y1, f1o, up1, (g_win,) = ffn_fwd(
        "ffn1", u0, g_f1i, t, gather_job([item(n) for n in ("f1o", "wp", "wba", "wbb", "wo")]),
        gather_job([item("win")]))
    x1, z1, u1 = residual_ln_mod("ffn1_ln", xs, y1, gt0, lgs[0], lbs[0], 0.5, sh1, sc1, t)
    _, g_wp, g_wba, g_wbb, g_wo = up1
    wp_full = jnp.transpose(g_wp.reshape(4, 4, 64, PG), (1, 0, 2, 3)).reshape(4, PG, PG)
    wts = (g_win, wp_full, g_wba, g_wbb, g_wo.reshape(D, D))
    y2, sv2, (g_f2i,) = mix_fwd(
        u1, wts, b_in2, ps2, sinks2, tabs, t, gather_job([item("f2i", 0, 2)]),
        lambda moved: gather_job([(moved[0], view_lead, FHP, 1, 2)]))
    x2, z2, u2 = residual_ln_mod("mix_ln", x1, y2, gt1, lgs[1], lbs[1], 1.0, sh2, sc2, t)
    ha3, hb3, g3, y3, f2o, _, _ = ffn_fwd("ffn2", u2, g_f2i, t, gather_job([item("f2o")]))

    dz3, dy3, dlg2, dlb2, dgt2, sq = residual_ln_loss_bwd("ffn2_ln_loss", x2, y3, tgt, gt2, lgs[2], lbs[2], 0.5, t)
    loss = lax.psum(0.5 * sq[0, 0] / D, ("x", "y", "c"))
    du3, red_f2i, red_f2o, _, _ = ffn_bwd("ffn2", u2, ha3, hb3, g3, dy3, g_f2i, f2o, t, sp)
    dz2, dy2, dlg1, dlb1, dgt1, dsh2, dsc2 = residual_ln_bwd("mix_ln_bwd", z2, (dz3, du3, x2, sc2), y2, gt1, lgs[1], 1.0, t)
    du2, dmix, db_in, dps, dsinks = mix_bwd(u1, sv2, dy2, wts, b_in2, ps2, sinks2, tabs, t)
    dwin, dwp, dwba, dwbb, dwo = dmix
    dwp_sh = jnp.transpose(dwp.reshape(4, 4, 64, PG), (1, 0, 2, 3)).reshape(4, 4 * 64, PG)
    mix_parts = {"win": dwin, "wp": dwp_sh, "wba": dwba, "wbb": dwbb, "wo": dwo.reshape(4, D // 4, D)}
    mix_names = list(mix_parts)
    sib = run_job("sibling_mix", reduce_sibling_job(
        [(mix_parts[n], view_lead, shard_rows[n], shard_cols[n]) for n in mix_names]))
    q = {n: chip_sum("chipsum_" + n, mix_parts[n], g, sp, shard_rows[n], tiles[n]) for n, g in zip(mix_names, sib)}
    dz1, dy1, dlg0, dlb0, dgt0, dsh1, dsc1 = residual_ln_bwd("ffn1_ln_bwd", z1, (dz2, du2, x1, sc1), y1, gt0, lgs[0], 0.5, t)
    du1, red_f1i, red_f1o, far_a, far_b = ffn_bwd(
        "ffn1", u0, ha1, hb1, g1, dy1, g_f1i, f1o, t, sp,
        reduce_chips_job([q["win"], q["wp"]]), reduce_chips_job([q["wo"], q["wba"], q["wbb"]]))
    dx0, dsh0, dsc0 = modulate_bwd("ffn1_mod_bwd", dz1, du1, xs, sc0, t)
    gm0, gm1, gm2 = (dsh0, dsc0, dgt0), (dsh1, dsc1, dgt1), (dsh2, dsc2, dgt2)
    reduced = {"f1i": red_f1i, "f1o": red_f1o, "f2i": red_f2i, "f2o": red_f2o, "win": (q["win"], far_a[0]),
               "wp": (q["wp"], far_a[1]), "wo": (q["wo"], far_b[0]), "wba": (q["wba"], far_b[1]), "wbb": (q["wbb"], far_b[2])}
    halves = [chip_total("total_" + n, *reduced[n], sp, shard_rows[n], tiles[n]) for n in order]
    gw = dict(zip(order, run_job("share_halves", share_halves_job(halves))))

    small = jnp.concatenate([*gm0, *gm1, *gm2, dlg0, dlg1, dlg2, dlb0, dlb1, dlb2, db_in, dps, dsinks], axis=1)
    n_small = small.shape[1]
    rows_small = -(-n_small // 1024) * 8
    small = jnp.pad(small, ((0, 0), (0, rows_small * 128 - n_small))).reshape(rows_small, 128)
    small_all = allgather_small("gather_small", small)
    tot = sum_devices(small_all).reshape(1, -1)
    gmod_all = small_all.reshape(8, -1)[:, :9 * D]
    o = 9 * D
    g_b_ada = tot[:, :o]
    g_ln_g = lax.dynamic_slice(tot[:, o:o + 3 * D].reshape(3, D), (0, chip * (D // 4)), (3, D // 4))
    g_ln_b = lax.dynamic_slice(tot[:, o + 3 * D:o + 6 * D].reshape(3, D), (0, chip * (D // 4)), (3, D // 4))
    o += 6 * D
    g_b_in, g_ps, g_sinks = tot[:, o:o + IN_W], tot[:, o + IN_W:o + IN_W + PW], tot[:, o + IN_W + PW:o + IN_W + PW + N_Q]

    gm16 = jnp.pad(lax.dynamic_slice(gmod_all, (0, chip * ADA_SH), (8, ADA_SH)), ((0, 8), (0, 0)))
    g_w_ada, d_w_ada, nm_w_ada, nv_w_ada = ada_bwd_adam(c16, gm16, w_ada[0], m_w_ada[0], v_w_ada[0])

    def big(n, w, m, v, tm):
        shape = w.shape
        w2, m2, v2 = (a.reshape(shape[-2] if a.ndim == 3 else -1, shape[-1]) for a in (w, m, v))
        return [r.reshape(shape) for r in adam_rows("adam_" + n, w2, gw[n], m2, v2, tm)]

    def big_t(n, w, m, v):
        return [jnp.swapaxes(r, 0, 1)[None] for r in adam_rows("adam_" + n, tr(w), gw[n], tr(m), tr(v), 64)]

    def tiny(n, w, g, m, v):
        return [g.reshape(w.shape)] + list(adam_small("adam_" + n, w, g.reshape(w.shape), m, v))

    res = {
        "w_ada": [a[None] for a in (g_w_ada, d_w_ada, nm_w_ada, nv_w_ada)],
        "b_ada": tiny("b_ada", b_ada, g_b_ada, m_b_ada, v_b_ada),
        "ln_g": tiny("ln_g", ln_g, g_ln_g, m_ln_g, v_ln_g),
        "ln_b": tiny("ln_b", ln_b, g_ln_b, m_ln_b, v_ln_b),
        "w_ffn1_in": big_t("f1i", w_ffn1_in, m_w_ffn1_in, v_w_ffn1_in),
        "w_ffn1_out": big("f1o", w_ffn1_out, m_w_ffn1_out, v_w_ffn1_out, 32),
        "w_in": big("win", w_in, m_w_in, v_w_in, 256),
        "b_in": tiny("b_in", b_in, g_b_in, m_b_in, v_b_in),
        "w_pool": big("wp", w_pool, m_w_pool, v_w_pool, 256),
        "pool_scale": tiny("pool_scale", pool_scale, g_ps, m_pool_scale, v_pool_scale),
        "sinks": tiny("sinks", sinks, g_sinks, m_sinks, v_sinks),
        "w_branch_a": big("wba", w_branch_a, m_w_branch_a, v_w_branch_a, 512),
        "w_branch_b": big("wbb", w_branch_b, m_w_branch_b, v_w_branch_b, 512),
        "w_out": big("wo", w_out, m_w_out, v_w_out, 128),
        "w_ffn2_in": big_t("f2i", w_ffn2_in, m_w_ffn2_in, v_w_ffn2_in),
        "w_ffn2_out": big("f2o", w_ffn2_out, m_w_ffn2_out, v_w_ffn2_out, 32),
    }
    names = ["w_ada", "b_ada", "ln_g", "ln_b", "w_ffn1_in", "w_ffn1_out", "w_in", "b_in", "w_pool", "pool_scale", "sinks",
             "w_branch_a", "w_branch_b", "w_out", "w_ffn2_in", "w_ffn2_out"]
    return (loss, dx0[None], *[res[n][0] for n in names], *[res[n][1] for n in names],
            *[res[n][2] for n in names], *[res[n][3] for n in names])
```

```python
import jax
import jax.numpy as jnp
from jax import lax
from jax.experimental import pallas as pl
from jax.experimental.pallas import tpu as pltpu

F32 = jnp.float32
BF16 = jnp.bfloat16
MESH = pl.DeviceIdType.MESH
ANY = pl.BlockSpec(memory_space=pl.ANY)

D = 2048
N_Q, N_KV, HD = 16, 4, 64
QW, KVW = N_Q * HD, N_KV * HD
BLK = 128
POOL_WINDOWS = (2, 4, 8, 16)
PW, PG = 1024, 256
HALO = 16
ROPE_THETA = 500000.0
ROT = HD // 4
LN_EPS = 1e-5
ALPHA = 2.0 ** 0.25
FH = 2752
FHP = 2816
FO = 1376
IN_W = 6656
IN_SH = IN_W // 4
ADA_SH = 18432 // 4
B1, B2, LR, EPS, WD, STEP = 0.9, 0.999, 0.001, 1e-08, 0.01, 10
VMEM_LIMIT = 56 * 1024 * 1024
FLIPS = ((1, 0), (0, 1), (1, 1))
NN = (((1,), (0,)), ((), ()))
NT = (((1,), (1,)), ((), ()))
TN = (((0,), (0,)), ((), ()))


def _params(sem):
    return pltpu.CompilerParams(dimension_semantics=sem, vmem_limit_bytes=VMEM_LIMIT)


def _aligned(v, m):
    return v if isinstance(v, int) else pl.multiple_of(v, m)


def _sigmoid(v):
    return 1.0 / (1.0 + jnp.exp(-v))


def T_(arr, width=None, off=0):
    return ("t", arr, width, off)


def B_(arr, width=None, off=0):
    return ("b", arr, width, off)


def X_(arr, spec):
    return ("x", arr, spec, 0)


def rowmap(name, fn, ins, outs, accs=(), *, rows, tm, ncol=1, with_ids=False, sp=None, alias=None):
    tm = min(tm, rows)
    nrow = rows // tm
    in_specs, arrs = [], []
    for kind, arr, width, off in ins:
        if kind == "x":
            in_specs.append(width)
        elif kind == "t":
            w = arr.shape[1] if width is None else width
            in_specs.append(pl.BlockSpec((tm, w), lambda j, i, *_, off=off: (i, off + j)))
        else:
            w = arr.shape[1] if width is None else width
            in_specs.append(pl.BlockSpec((arr.shape[0], w), lambda j, i, *_, off=off: (0, off + j)))
        arrs.append(arr)
    out_shape, out_specs = [], []
    for o in outs:
        if len(o) == 3:
            out_shape.append(jax.ShapeDtypeStruct(o[0], o[1]))
            out_specs.append(o[2])
        else:
            out_shape.append(jax.ShapeDtypeStruct((rows, o[0]), o[1]))
            out_specs.append(pl.BlockSpec((tm, o[0] // ncol), lambda j, i, *_: (i, j)))
    for r, width in accs:
        out_shape.append(jax.ShapeDtypeStruct((r, width), F32))
        out_specs.append(pl.BlockSpec((r, width // ncol), lambda j, i, *_: (0, j)))
    ni, no = len(ins), len(outs)
    nsp = 0 if sp is None else 1

    def body(*refs):
        refs = refs[nsp:]
        i = pl.program_id(1)
        vals = [r[...] for r in refs[:ni]]
        res = fn(pl.program_id(0), i, *vals) if with_ids else fn(*vals)
        if not isinstance(res, (tuple, list)):
            res = (res,)
        for r, v in zip(refs[ni:ni + no], res[:no]):
            r[...] = v.astype(r.dtype)
        for r, v in zip(refs[ni + no:], res[no:]):
            @pl.when(i == 0)
            def _(r=r, v=v):
                r[...] = v

            @pl.when(i > 0)
            def _(r=r, v=v):
                r[...] += v

    grid_spec = pltpu.PrefetchScalarGridSpec(num_scalar_prefetch=nsp, grid=(ncol, nrow), in_specs=in_specs,
                                             out_specs=out_specs)
    res = pl.pallas_call(
        body, name=name, grid_spec=grid_spec, out_shape=out_shape,
        input_output_aliases={nsp + k: v for k, v in (alias or {}).items()},
        compiler_params=_params(("arbitrary", "arbitrary")),
    )(*([sp] if nsp else []), *arrs)
    return res[0] if len(res) == 1 else res


def colsum(v):
    return jnp.sum(v, axis=0, keepdims=True)


def mm(name, a_ops, b_ops, ops, *, dims, grid, a_specs, b_specs, outs, out_specs, acc_shapes,
       epilogue=None, extras=(), extra_specs=(), carry=None, job=None, sub_rows=None):
    gk = grid[2]
    na, nb, ne, nacc = len(a_ops), len(b_ops), len(extras), len(acc_shapes)
    nc = 0 if carry is None else 1
    no = len(outs)

    def body(*refs):
        a_refs = refs[:na]
        b_refs = refs[na:na + nb]
        e_refs = refs[na + nb:na + nb + ne]
        o_refs = refs[na + nb + ne + nc:na + nb + ne + nc + no]
        acc_refs = refs[na + nb + ne + nc + no:]
        k = pl.program_id(2)

        def partials(rows=slice(None)):
            res = [None] * nacc
            for ai, bi, ci in ops:
                p = lax.dot_general(a_refs[ai][rows], b_refs[bi][...], dims, preferred_element_type=F32)
                res[ci] = p if res[ci] is None else res[ci] + p
            return res

        def finish(accs, rows=slice(None)):
            outv = epilogue(accs, [e[rows] for e in e_refs]) if epilogue else (accs[0],)
            for o, v in zip(o_refs, outv):
                o[rows] = v.astype(o.dtype)

        if gk == 1 and sub_rows:
            for s in range(out_specs[0].block_shape[-2] // sub_rows):
                rows = pl.ds(s * sub_rows, sub_rows)
                finish(partials(rows), rows)
        elif gk == 1:
            finish(partials())
        else:
            ps = partials()

            @pl.when(k == 0)
            def _():
                for acc, p in zip(acc_refs, ps):
                    acc[...] = p

            @pl.when((k > 0) & (k < gk - 1))
            def _():
                for acc, p in zip(acc_refs, ps):
                    acc[...] += p

            @pl.when(k == gk - 1)
            def _():
                finish([acc[...] + p for acc, p in zip(acc_refs, ps)])

    res, moved = carried_call(
        body, name, grid,
        list(a_specs) + list(b_specs) + list(extra_specs) + ([ANY] if nc else []), list(out_specs), list(outs),
        [pltpu.VMEM(s, F32) for s in acc_shapes] if gk > 1 else [],
        [*a_ops, *b_ops, *extras, *([carry] if nc else [])], {na + nb + ne: 0} if nc else {}, job)
    res = res[0] if len(res) == 1 else res
    return res if job is None else (res, moved)


def sds(shape, dt):
    return jax.ShapeDtypeStruct(shape, dt)


class Job:
    def __init__(self, ins, outs, aliases, scratch, start, mid, finish):
        self.ins, self.outs, self.aliases, self.scratch = list(ins), list(outs), dict(aliases), list(scratch)
        self.start, self.mid, self.finish = start, mid, finish


def carried_call(body, name, grid, in_specs, out_specs, out_shape, scratch, args, aliases, job, mid_at=0.9):
    sem = ("arbitrary",) * len(grid)
    if job is None:
        res = pl.pallas_call(body, name=name, grid=grid, in_specs=in_specs, out_specs=out_specs, out_shape=out_shape,
                             scratch_shapes=scratch, input_output_aliases=aliases, compiler_params=_params(sem))(*args)
        return list(res), []
    ni, no, ns = len(in_specs), len(out_specs), len(scratch)
    ci, co = len(job.ins), len(job.outs)
    total = 1
    for g in grid:
        total *= g
    mid_step = min(max(int(total * mid_at), 1), total - 1)

    def full(*refs):
        ins, cins = refs[:ni], refs[ni:ni + ci]
        outs, couts = refs[ni + ci:ni + ci + no], refs[ni + ci + no:ni + ci + no + co]
        scr, cscr = refs[ni + ci + no + co:ni + ci + no + co + ns], refs[ni + ci + no + co + ns:]
        step = 0
        for d, g in enumerate(grid):
            step = step * g + pl.program_id(d)

        @pl.when(step == 0)
        def _():
            job.start(cins, couts, cscr)

        body(*ins, *outs, *scr)

        @pl.when(step == mid_step)
        def _():
            job.mid(cins, couts, cscr)

        @pl.when(step == total - 1)
        def _():
            job.finish(cins, couts, cscr)

    al = dict(aliases)
    al.update({ni + k: no + v for k, v in job.aliases.items()})
    res = pl.pallas_call(
        full, name=name, grid=grid, in_specs=in_specs + [ANY] * ci, out_specs=out_specs + [ANY] * co,
        out_shape=out_shape + job.outs, scratch_shapes=scratch + job.scratch, input_output_aliases=al,
        compiler_params=_params(sem))(*args, *job.ins)
    return list(res[:no]), list(res[no:])


def _with_moved(res, job):
    return res if job is not None else (res, [])


def run_job(name, job):
    ci = len(job.ins)

    def body(*refs):
        cins, couts, cscr = refs[:ci], refs[ci:ci + len(job.outs)], refs[ci + len(job.outs):]
        job.start(cins, couts, cscr)
        job.mid(cins, couts, cscr)
        job.finish(cins, couts, cscr)

    return list(pl.pallas_call(
        body, name=name, in_specs=[ANY] * ci, out_specs=[ANY] * len(job.outs), out_shape=job.outs,
        scratch_shapes=job.scratch, input_output_aliases=job.aliases)(*job.ins))


def _place():
    x, y, c = lax.axis_index("x"), lax.axis_index("y"), lax.axis_index("c")
    chips = [((1 - x) if fx else x, (1 - y) if fy else y) for fx, fy in FLIPS]
    return x, y, c, chips


def allgather_small(name, v):
    r = v.shape[0]

    def body(x_ref, out_ref, send_sems, recv_sems, local_sem):
        x, y, c, chips = _place()
        me, sibling = (x, y, c), (x, y, 1 - c)

        def rows(px, py, pc):
            return out_ref.at[4 * px + 2 * py + pc]

        def copy(k, block, to, src=None):
            return pltpu.make_async_remote_copy(
                src_ref=rows(*block) if src is None else src, dst_ref=rows(*block),
                send_sem=send_sems.at[k], recv_sem=recv_sems.at[k], device_id=to, device_id_type=MESH)

        mine = pltpu.make_async_copy(x_ref, rows(*me), local_sem)
        mine.start()
        first = [copy(0, me, sibling, src=x_ref)]
        first += [copy(1 + j, me, (*chip, c), src=x_ref) for j, chip in enumerate(chips)]
        for cp in first:
            cp.start()
        passed = [copy(4 + j, (*chip, c), sibling) for j, chip in enumerate(chips)]
        for j, chip in enumerate(chips):
            copy(1 + j, (*chip, c), me).wait_recv()
            passed[j].start()
        copy(0, sibling, me).wait_recv()
        for j, chip in enumerate(chips):
            copy(4 + j, (*chip, 1 - c), me).wait_recv()
        for cp in first + passed:
            cp.wait_send()
        mine.wait()

    return pl.pallas_call(
        body, name=name, out_shape=sds((8, r, 128), v.dtype),
        in_specs=[pl.BlockSpec(memory_space=pltpu.VMEM)], out_specs=pl.BlockSpec(memory_space=pltpu.VMEM),
        scratch_shapes=[pltpu.SemaphoreType.DMA((7,)), pltpu.SemaphoreType.DMA((7,)), pltpu.SemaphoreType.DMA],
    )(v)


def _half(ref, rows, hf):
    hr = rows // 2
    return ref.at[pl.ds(_aligned(hf * hr, 16), hr)]


def view_lead(ref, p):
    return ref.at[p]


def view_ffn_out(ref, p):
    return ref.at[p // 2, pl.ds(_aligned((p % 2) * FO, 16), FO)]


def _remote(ref, dst, send_sems, recv_sems, idx, to):
    return pltpu.make_async_remote_copy(src_ref=ref, dst_ref=dst, send_sem=send_sems.at[idx], recv_sem=recv_sems.at[idx],
                                        device_id=to, device_id_type=MESH)


def gather_job(items):
    nw = len(items)
    pads = [w for w, it in enumerate(items) if it[1] is view_ffn_out]

    def piece(ref, w, p, hf):
        _, view, rws, part, parts = items[w]
        pr = rws // 2 // parts
        return view(ref, p).at[pl.ds(_aligned(hf * (rws // 2) + part * pr, 16), pr)]

    def pad_copies(outs, scr):
        return [pltpu.make_async_copy(scr[2], outs[w].at[h, pl.ds(2 * FO, FHP - 2 * FO)], scr[3].at[2 * n + h])
                for n, w in enumerate(pads) for h in range(2)]

    def start(_, outs, scr):
        x, y, c, chips = _place()
        if pads:
            scr[2][...] = jnp.zeros_like(scr[2])
            for cp in pad_copies(outs, scr):
                cp.start()
        for w in range(nw):
            mine = piece(outs[w], w, 2 * x + y, c)
            for f, (px, py) in enumerate(chips):
                _remote(mine, mine, scr[0], scr[1], (w, f), (px, py, c)).start()

    def mid(_, outs, scr):
        x, y, c, chips = _place()
        for w in range(nw):
            for f, (px, py) in enumerate(chips):
                land = piece(outs[w], w, 2 * px + py, c)
                _remote(land, land, scr[0], scr[1], (w, f), (px, py, c)).wait_recv()
                _remote(land, land, scr[0], scr[1], (w, 3 + f), (x, y, 1 - c)).start()

    def finish(_, outs, scr):
        x, y, c, chips = _place()
        for w in range(nw):
            for f, (px, py) in enumerate(chips):
                land = piece(outs[w], w, 2 * px + py, 1 - c)
                _remote(land, land, scr[0], scr[1], (w, 3 + f), (x, y, 1 - c)).wait_recv()
        for w in range(nw):
            mine = piece(outs[w], w, 2 * x + y, c)
            for f in range(6):
                _remote(mine, mine, scr[0], scr[1], (w, f), (x, y, 1 - c)).wait_send()
        for cp in pad_copies(outs, scr):
            cp.wait()

    scratch = [pltpu.SemaphoreType.DMA((nw, 6)), pltpu.SemaphoreType.DMA((nw, 6))]
    if pads:
        scratch += [pltpu.VMEM((FHP - 2 * FO, D), BF16), pltpu.SemaphoreType.DMA((2 * len(pads),))]
    bufs = [it[0] for it in items]
    return Job(bufs, [sds(b.shape, BF16) for b in bufs], {w: w for w in range(nw)}, scratch, start, mid, finish)


def reduce_sibling_job(items):
    nw = len(items)

    def copies(ins, got, scr):
        x, y, c, _ = _place()
        return [_remote(_half(view(ins[w], p), rws, 1 - c), got[w].at[p], scr[0], scr[1], (w, p), (x, y, 1 - c))
                for w, (_, view, rws, _) in enumerate(items) for p in range(4)]

    def start(ins, got, scr):
        for cp in copies(ins, got, scr):
            cp.start()

    def finish(ins, got, scr):
        for cp in copies(ins, got, scr):
            cp.wait()

    return Job([it[0] for it in items], [sds((4, it[2] // 2, it[3]), BF16) for it in items], {},
               [pltpu.SemaphoreType.DMA((nw, 4)), pltpu.SemaphoreType.DMA((nw, 4))], start, lambda *_: None, finish)


def reduce_chips_job(qs):
    nw = len(qs)

    def copies(ins, got, scr):
        x, y, c, chips = _place()
        return [_remote(ins[w].at[2 * px + py], got[w].at[f], scr[0], scr[1], (w, f), (px, py, c))
                for w in range(nw) for f, (px, py) in enumerate(chips)]

    def start(ins, got, scr):
        for cp in copies(ins, got, scr):
            cp.start()

    def finish(ins, got, scr):
        for cp in copies(ins, got, scr):
            cp.wait()

    return Job(qs, [sds((3,) + q.shape[1:], BF16) for q in qs], {},
               [pltpu.SemaphoreType.DMA((nw, 3)), pltpu.SemaphoreType.DMA((nw, 3))], start, lambda *_: None, finish)


def share_halves_job(gs):
    nw = len(gs)

    def start(_, outs, scr):
        x, y, c, _ = _place()
        for w in range(nw):
            mine = _half(outs[w], gs[w].shape[0], c)
            _remote(mine, mine, scr[0], scr[1], w, (x, y, 1 - c)).start()

    def finish(_, outs, scr):
        x, y, c, _ = _place()
        for w in range(nw):
            mine = _half(outs[w], gs[w].shape[0], c)
            theirs = _half(outs[w], gs[w].shape[0], 1 - c)
            _remote(mine, mine, scr[0], scr[1], w, (x, y, 1 - c)).wait_send()
            _remote(theirs, theirs, scr[0], scr[1], w, (x, y, 1 - c)).wait_recv()

    return Job(gs, [sds(g.shape, F32) for g in gs], {w: w for w in range(nw)},
               [pltpu.SemaphoreType.DMA((nw,)), pltpu.SemaphoreType.DMA((nw,))], start, lambda *_: None, finish)


def rope_tables(t):
    pos = jnp.arange(t, dtype=F32)
    inv_freq = ROPE_THETA ** (-jnp.arange(0, ROT, 2, dtype=F32) / ROT)
    ang = pos[:, None] * inv_freq[None, :]
    cos, sin = jnp.cos(ang), jnp.sin(ang)
    d = jnp.arange(128) % HD
    half = ROT // 2
    cs = jnp.take(cos, d % half, axis=1)
    sn = jnp.take(sin, d % half, axis=1)
    cc = jnp.where(d[None] < ROT, cs, 1.0)
    sa = jnp.where(d[None] < half, -sn, 0.0)
    sb = jnp.where((d[None] >= half) & (d[None] < ROT), sn, 0.0)
    return cc, sa, sb


def _rope(v, cc, sa, sb):
    w = v.shape[1]
    reps = w // 128
    half = ROT // 2
    return (v * jnp.tile(cc, (1, reps)) + pltpu.roll(v, w - half, 1) * jnp.tile(sa, (1, reps))
            + pltpu.roll(v, half, 1) * jnp.tile(sb, (1, reps)))


def _rope_t(dv, cc, sa, sb):
    w = dv.shape[1]
    reps = w // 128
    half = ROT // 2
    return (dv * jnp.tile(cc, (1, reps)) + pltpu.roll(dv * jnp.tile(sa, (1, reps)), half, 1)
            + pltpu.roll(dv * jnp.tile(sb, (1, reps)), w - half, 1))


def pool_fwd(h, b_in, t, tm):
    tm = min(tm, t)
    per = tm // HALO

    def body(prev_ref, cur_ref, b_ref, o_ref, xx):
        i = pl.program_id(0)
        b = b_ref[...]
        xx[pl.ds(0, HALO), :] = jnp.where(i > 0, prev_ref[...] + b, 0.0)
        xx[pl.ds(HALO, tm), :] = cur_ref[...] + b
        tpos = i * tm + lax.broadcasted_iota(jnp.int32, (tm, PG), 0) + 1
        for gi, w in enumerate(POOL_WINDOWS):
            cols = pl.ds(gi * PG, PG)
            acc = xx[pl.ds(HALO, tm), cols]
            for s in range(1, w):
                acc = acc + xx[pl.ds(HALO - s, tm), cols]
            cnt = jnp.minimum(tpos, w).astype(F32)
            o_ref[:, cols] = (acc / cnt - xx[pl.ds(HALO, tm), cols]).astype(o_ref.dtype)

    return pl.pallas_call(
        body, name="pool_fwd", grid=(t // tm,),
        in_specs=[pl.BlockSpec((HALO, PW), lambda i: (jnp.maximum(i * per - 1, 0), 0)),
                  pl.BlockSpec((tm, PW), lambda i: (i, 0)), pl.BlockSpec((1, PW), lambda i: (0, 0))],
        out_specs=pl.BlockSpec((tm, PW), lambda i: (i, 0)), out_shape=sds((t, PW), BF16),
        scratch_shapes=[pltpu.VMEM((tm + HALO, PW), F32)], compiler_params=_params(("arbitrary",)),
    )(h, h, b_in)


def pool_bwd(dpooled, t, tm):
    tm = min(tm, t)
    per = tm // HALO
    nt = t // tm

    def body(cur_ref, nxt_ref, o_ref, db_ref, ee):
        i = pl.program_id(0)
        tpos = i * tm + lax.broadcasted_iota(jnp.int32, (tm, PG), 0) + 1
        for gi, w in enumerate(POOL_WINDOWS):
            cols = pl.ds(gi * PG, PG)
            ee[pl.ds(0, tm), cols] = cur_ref[:, cols] / jnp.minimum(tpos, w).astype(F32)
            ee[pl.ds(tm, HALO), cols] = jnp.where(i < nt - 1, nxt_ref[:, cols] / float(w), 0.0)
        for gi, w in enumerate(POOL_WINDOWS):
            cols = pl.ds(gi * PG, PG)
            acc = ee[pl.ds(0, tm), cols]
            for s in range(1, w):
                acc = acc + ee[pl.ds(s, tm), cols]
            dxp = acc - cur_ref[:, cols]
            o_ref[:, cols] = dxp.astype(o_ref.dtype)
            part = colsum(dxp)

            @pl.when(i == 0)
            def _(cols=cols, part=part):
                db_ref[:, cols] = part

            @pl.when(i > 0)
            def _(cols=cols, part=part):
                db_ref[:, cols] += part

    return pl.pallas_call(
        body, name="pool_bwd", grid=(nt,),
        in_specs=[pl.BlockSpec((tm, PW), lambda i: (i, 0)),
                  pl.BlockSpec((HALO, PW), lambda i: (jnp.minimum((i + 1) * per, t // HALO - 1), 0))],
        out_specs=[pl.BlockSpec((tm, PW), lambda i: (i, 0)), pl.BlockSpec((1, PW), lambda i: (0, 0))],
        out_shape=[sds((t, PW), BF16), sds((1, PW), F32)],
        scratch_shapes=[pltpu.VMEM((tm + HALO, PW), F32)], compiler_params=_params(("arbitrary",)),
    )(dpooled, dpooled)


def _scores(qh, kp, kc, mask_p, mask_c, sink):
    sp = jnp.where(mask_p, lax.dot_general(qh, kp, NT, preferred_element_type=F32), -1e30)
    sc = jnp.where(mask_c, lax.dot_general(qh, kc, NT, preferred_element_type=F32), -1e30)
    m = jnp.maximum(jnp.maximum(jnp.max(sp, axis=-1, keepdims=True), jnp.max(sc, axis=-1, keepdims=True)), sink)
    pp, pc = jnp.exp(sp - m), jnp.exp(sc - m)
    es = jnp.exp(sink - m)
    inv = 1.0 / (jnp.sum(pp, axis=-1, keepdims=True) + jnp.sum(pc, axis=-1, keepdims=True) + es)
    return pp * inv, pc * inv, es * inv


GRP = N_Q // N_KV


def _masks(n):
    qi = lax.broadcasted_iota(jnp.int32, (GRP * BLK, BLK), 0) % BLK
    kj = lax.broadcasted_iota(jnp.int32, (GRP * BLK, BLK), 1)
    return (kj > qi) & (n > 0), kj <= qi


def _head(hk, g):
    return pl.ds(HD * (GRP * hk + g), HD)


def _stack_heads(ref, hk):
    return jnp.concatenate([ref[:, _head(hk, g)] for g in range(GRP)], axis=0)


def _stack_sinks(s_ref, hk):
    return jnp.concatenate([jnp.full((BLK, 1), s_ref[0, GRP * hk + g], F32) for g in range(GRP)], axis=0)


def attn_fwd(q, k, v, sinks, t, job=None):
    def body(s_ref, q_ref, kp_ref, kc_ref, vp_ref, vc_ref, o_ref):
        n = pl.program_id(0)
        mask_p, mask_c = _masks(n)
        for hk in range(N_KV):
            kv = pl.ds(HD * hk, HD)
            pp, pc, _ = _scores(_stack_heads(q_ref, hk), kp_ref[:, kv], kc_ref[:, kv], mask_p, mask_c,
                                _stack_sinks(s_ref, hk))
            o = (lax.dot_general(pp.astype(BF16), vp_ref[:, kv], NN, preferred_element_type=F32)
                 + lax.dot_general(pc.astype(BF16), vc_ref[:, kv], NN, preferred_element_type=F32))
            for g in range(GRP):
                o_ref[:, _head(hk, g)] = o[g * BLK:(g + 1) * BLK].astype(o_ref.dtype)

    prev = lambda n: (jnp.maximum(n - 1, 0), 0)
    cur = lambda n: (n, 0)
    res, moved = carried_call(
        body, "attn_fwd", (t // BLK,),
        [pl.BlockSpec(memory_space=pltpu.SMEM), pl.BlockSpec((BLK, QW), cur),
         pl.BlockSpec((BLK, KVW), prev), pl.BlockSpec((BLK, KVW), cur),
         pl.BlockSpec((BLK, KVW), prev), pl.BlockSpec((BLK, KVW), cur)],
        [pl.BlockSpec((BLK, QW), cur)], [sds((t, QW), BF16)], [], [sinks, q, k, k, v, v], {}, job)
    return res[0], moved


def attn_bwd(q, k, v, do, sinks, t):
    nb = t // BLK

    def body(s_ref, q_ref, do_ref, kp_ref, kc_ref, vp_ref, vc_ref, dq_ref, dk_ref, dv_ref, ds_ref, dkc, dvc):
        n = pl.program_id(0)

        @pl.when(n == 0)
        def _():
            dkc[...] = jnp.zeros_like(dkc)
            dvc[...] = jnp.zeros_like(dvc)
            ds_ref[...] = jnp.zeros_like(ds_ref)

        @pl.when(n < nb)
        def _():
            mask_p, mask_c = _masks(n)
            lane = lax.broadcasted_iota(jnp.int32, (1, 128), 1)
            dsink = jnp.zeros((1, 128), F32)
            for hk in range(N_KV):
                kv = pl.ds(HD * hk, HD)
                kp, kc, vp, vc = kp_ref[:, kv], kc_ref[:, kv], vp_ref[:, kv], vc_ref[:, kv]
                qs, dos = _stack_heads(q_ref, hk), _stack_heads(do_ref, hk)
                pp, pc, ps = _scores(qs, kp, kc, mask_p, mask_c, _stack_sinks(s_ref, hk))
                dpp = lax.dot_general(dos, vp, NT, preferred_element_type=F32)
                dpc = lax.dot_general(dos, vc, NT, preferred_element_type=F32)
                delta = jnp.sum(pp * dpp, axis=-1, keepdims=True) + jnp.sum(pc * dpc, axis=-1, keepdims=True)
                dsp = (pp * (dpp - delta)).astype(BF16)
                dsc = (pc * (dpc - delta)).astype(BF16)
                sd = ps * delta
                dq = (lax.dot_general(dsp, kp, NN, preferred_element_type=F32)
                      + lax.dot_general(dsc, kc, NN, preferred_element_type=F32))
                for g in range(GRP):
                    rows = slice(g * BLK, (g + 1) * BLK)
                    dsink = dsink + jnp.where(lane == GRP * hk + g, -jnp.sum(sd[rows]), 0.0)
                    dq_ref[:, _head(hk, g)] = dq[rows]
                dk_ref[:, kv] = dkc[:, kv] + lax.dot_general(dsp, qs, TN, preferred_element_type=F32)
                dv_ref[:, kv] = dvc[:, kv] + lax.dot_general(pp.astype(BF16), dos, TN, preferred_element_type=F32)
                dkc[:, kv] = lax.dot_general(dsc, qs, TN, preferred_element_type=F32)
                dvc[:, kv] = lax.dot_general(pc.astype(BF16), dos, TN, preferred_element_type=F32)
            ds_ref[...] += dsink

        @pl.when(n == nb)
        def _():
            dk_ref[...] = dkc[...]
            dv_ref[...] = dvc[...]

    cur = lambda n: (jnp.minimum(n, nb - 1), 0)
    prev = lambda n: (jnp.clip(n - 1, 0, nb - 1), 0)
    return pl.pallas_call(
        body, name="attn_bwd", grid=(nb + 1,),
        in_specs=[pl.BlockSpec(memory_space=pltpu.SMEM), pl.BlockSpec((BLK, QW), cur), pl.BlockSpec((BLK, QW), cur),
                  pl.BlockSpec((BLK, KVW), prev), pl.BlockSpec((BLK, KVW), cur),
                  pl.BlockSpec((BLK, KVW), prev), pl.BlockSpec((BLK, KVW), cur)],
        out_specs=[pl.BlockSpec((BLK, QW), cur), pl.BlockSpec((BLK, KVW), prev), pl.BlockSpec((BLK, KVW), prev),
                   pl.BlockSpec((1, 128), lambda n: (0, 0))],
        out_shape=[sds((t, QW), F32), sds((t, KVW), F32), sds((t, KVW), F32), sds((1, 128), F32)],
        scratch_shapes=[pltpu.VMEM((BLK, KVW), F32), pltpu.VMEM((BLK, KVW), F32)],
        compiler_params=_params(("arbitrary",)),
    )(sinks, q, do, k, k, v, v)


def _adamw(w, g, m, v):
    m2 = B1 * m + (1.0 - B1) * g
    v2 = B2 * v + (1.0 - B2) * jnp.square(g)
    m_hat = m2 / (1.0 - B1 ** STEP)
    v_hat = v2 / (1.0 - B2 ** STEP)
    return -LR * (m_hat / (jnp.sqrt(v_hat) + EPS) + WD * w), m2, v2


def ada_fwd(c16, w_ada, b_sh):
    tn = 512

    def body(c_ref, w_ref, b_ref, o_ref):
        cv = c_ref[...]
        sc = (cv * _sigmoid(cv)).astype(BF16)
        o_ref[...] = lax.dot_general(sc, w_ref[...].astype(BF16), NN, preferred_element_type=F32) + b_ref[...]

    return pl.pallas_call(
        body, name="ada_fwd", grid=(ADA_SH // tn,),
        in_specs=[pl.BlockSpec((16, D), lambda j: (0, 0)), pl.BlockSpec((D, tn), lambda j: (0, j)),
                  pl.BlockSpec((1, tn), lambda j: (0, j))],
        out_specs=pl.BlockSpec((16, tn), lambda j: (0, j)), out_shape=sds((16, ADA_SH), F32),
        compiler_params=_params(("arbitrary",)),
    )(c16, w_ada, b_sh)


def ada_bwd_adam(c16, gm16, w, m, v):
    tm, tn = 256, 512

    def body(c_ref, g_ref, w_ref, m_ref, v_ref, go_ref, d_ref, mo_ref, vo_ref):
        cv = c_ref[...]
        sc = (cv * _sigmoid(cv)).astype(BF16)
        g = lax.dot_general(sc, g_ref[...].astype(BF16), TN, preferred_element_type=F32)
        dl, m2, v2 = _adamw(w_ref[...], g, m_ref[...], v_ref[...])
        go_ref[...] = g
        d_ref[...] = dl
        mo_ref[...] = m2
        vo_ref[...] = v2

    blk = pl.BlockSpec((tm, tn), lambda i, j: (i, j))
    return pl.pallas_call(
        body, name="ada_bwd_adam", grid=(D // tm, ADA_SH // tn),
        in_specs=[pl.BlockSpec((16, tm), lambda i, j: (0, i)), pl.BlockSpec((16, tn), lambda i, j: (0, j)), blk, blk, blk],
        out_specs=[blk] * 4, out_shape=[sds((D, ADA_SH), F32)] * 4,
        compiler_params=_params(("arbitrary", "arbitrary")),
    )(c16, gm16, w, m, v)


def adam_rows(name, w, g, m, v, tm):
    rows, cols = w.shape

    def fn(wv, gv, mv, vv):
        gv = gv[:, :cols]
        dl, m2, v2 = _adamw(wv, gv, mv, vv)
        return gv, dl, m2, v2

    return rowmap(name, fn, [T_(w), T_(g), T_(m), T_(v)], [(cols, F32)] * 4, rows=rows, tm=tm)


def adam_small(name, w, g, m, v):
    def body(w_ref, g_ref, m_ref, v_ref, d_ref, mo_ref, vo_ref):
        dl, m2, v2 = _adamw(w_ref[...], g_ref[...], m_ref[...], v_ref[...])
        d_ref[...] = dl
        mo_ref[...] = m2
        vo_ref[...] = v2

    return pl.pallas_call(body, name=name, out_shape=[sds(w.shape, F32)] * 3)(w, g, m, v)


def sum_devices(allv):
    def body(a_ref, o_ref):
        acc = a_ref[0]
        for d in range(1, 8):
            acc = acc + a_ref[d]
        o_ref[...] = acc

    return pl.pallas_call(body, name="sum_devices", out_shape=sds(allv.shape[1:], F32))(allv)


def _ln_fwd(z, g, b):
    mu = jnp.mean(z, axis=-1, keepdims=True)
    zc = z - mu
    var = jnp.mean(jnp.square(zc), axis=-1, keepdims=True)
    return zc * lax.rsqrt(var + LN_EPS) * g + b


def _ln_bwd(z, g, dout):
    mu = jnp.mean(z, axis=-1, keepdims=True)
    zc = z - mu
    var = jnp.mean(jnp.square(zc), axis=-1, keepdims=True)
    rstd = lax.rsqrt(var + LN_EPS)
    xh = zc * rstd
    dxh = dout * g
    dz = rstd * (dxh - jnp.mean(dxh, axis=-1, keepdims=True) - xh * jnp.mean(dxh * xh, axis=-1, keepdims=True))
    return dz, colsum(dout * xh), colsum(dout)


def modulate(name, xin, shift, scale, t):
    return rowmap(name, lambda xv, sh, sc: xv * (1.0 + sc) + sh, [T_(xin), B_(shift), B_(scale)], [(D, BF16)],
                  rows=t, tm=512)


def residual_ln_mod(name, xin, y, gate, lg, lb, wgt, shift_n, scale_n, t):
    def fn(xv, yv, gt, g, b, sh, sc):
        z = ALPHA * xv + (wgt * (1.0 + gt)) * yv
        xo = _ln_fwd(z, g, b)
        return xo, z, xo * (1.0 + sc) + sh

    return rowmap(name, fn, [T_(xin), T_(y), B_(gate), B_(lg), B_(lb), B_(shift_n), B_(scale_n)],
                  [(D, F32), (D, F32), (D, BF16)], rows=t, tm=256)


def residual_ln_bwd(name, z, dnext, y, gate, lg, wgt, t):
    dzn, dun, xn, scn = dnext

    def fn(zv, yv, gt, g, dzv, duv, xv, sc):
        dv = ALPHA * dzv + duv * (1.0 + sc)
        dz, dg, db = _ln_bwd(zv, g, dv)
        return dz, (wgt * (1.0 + gt)) * dz, dg, db, colsum(wgt * dz * yv), colsum(duv), colsum(duv * xv)

    return rowmap(name, fn, [T_(z), T_(y), B_(gate), B_(lg), T_(dzn), T_(dun), T_(xn), B_(scn)],
                  [(D, F32), (D, BF16)], [(1, D)] * 5, rows=t, tm=256)


def residual_ln_loss_bwd(name, xin, y, tgt, gate, lg, lb, wgt, t):
    def fn(xv, yv, tv, gt, g, b):
        z = ALPHA * xv + (wgt * (1.0 + gt)) * yv
        d = _ln_fwd(z, g, b) - tv
        dz, dg, db = _ln_bwd(z, g, d * (1.0 / D))
        return dz, (wgt * (1.0 + gt)) * dz, dg, db, colsum(wgt * dz * yv), jnp.sum(d * d).reshape(1, 1)

    dz, dy, dlg, dlb, dgate, sq = rowmap(
        name, fn, [T_(xin), T_(y), T_(tgt), B_(gate), B_(lg), B_(lb)], [(D, F32), (D, BF16)],
        [(1, D), (1, D), (1, D), (1, 1)], rows=t, tm=256)
    return dz, dy, dlg, dlb, dgate, sq


def modulate_bwd(name, dz, du, xin, scale, t):
    def fn(dzv, duv, xv, sc):
        return ALPHA * dzv + duv * (1.0 + sc), colsum(duv), colsum(duv * xv)

    return rowmap(name, fn, [T_(dz), T_(du), T_(xin), B_(scale)], [(D, F32)], [(1, D), (1, D)], rows=t, tm=256)


def ffn_fwd(tag, u, wi, t, up_job, down_job=None):
    tm = min(1024, t)
    tn = 256
    per = FHP // tn

    def act(accs, _):
        a, b = accs
        return a, b, a * _sigmoid(a) * b

    hblk = pl.BlockSpec((tm, tn), lambda i, j, k: (i, j))
    (ha, hb, g), up_moved = mm(
        tag + "_up", [u], [wi, wi], [(0, 0, 0), (0, 1, 1)], dims=NT, grid=(t // tm, 2 * per, 1),
        a_specs=[pl.BlockSpec((tm, D), lambda i, j, k: (i, 0))],
        b_specs=[pl.BlockSpec((None, tn, D), lambda i, j, k: (j // per, j % per, 0)),
                 pl.BlockSpec((None, tn, D), lambda i, j, k: (2 + j // per, j % per, 0))],
        outs=[sds((t, 2 * FHP), BF16)] * 3, out_specs=[hblk] * 3, acc_shapes=[(tm, tn)] * 2, epilogue=act, job=up_job,
        sub_rows=tm // 2)
    wo = up_moved[0].reshape(2 * FHP, D)
    tk = FHP
    y, down_moved = _with_moved(mm(
        tag + "_down", [g], [wo], [(0, 0, 0)], dims=NN, grid=(t // tm, 2, 2),
        a_specs=[pl.BlockSpec((tm, tk), lambda i, j, k: (i, k))],
        b_specs=[pl.BlockSpec((tk, D // 2), lambda i, j, k: (k, j))],
        outs=[sds((t, D), F32)], out_specs=[pl.BlockSpec((tm, D // 2), lambda i, j, k: (i, j))],
        acc_shapes=[(tm, D // 2)], job=down_job), down_job)
    return ha, hb, g, y, wo, up_moved, down_moved


def ffn_bwd(tag, u, ha, hb, g, dy, wi, wo, t, sp, dact_job=None, dwo_job=None):
    tm = min(1024, t)

    def dact(accs, ex):
        dg = accs[0]
        a, b = ex[0].astype(F32), ex[1].astype(F32)
        s = _sigmoid(a)
        return dg * b * (s * (1.0 + a * (1.0 - s))), dg * (a * s)

    tn = 256
    hblk = pl.BlockSpec((tm, tn), lambda i, j, k: (i, j))
    (dha, dhb), dact_moved = _with_moved(mm(
        tag + "_dact", [dy], [wo], [(0, 0, 0)], dims=NT, grid=(t // tm, 2 * FHP // tn, 1),
        a_specs=[pl.BlockSpec((tm, D), lambda i, j, k: (i, 0))],
        b_specs=[pl.BlockSpec((tn, D), lambda i, j, k: (j, 0))],
        outs=[sds((t, 2 * FHP), BF16)] * 2, out_specs=[hblk] * 2, acc_shapes=[(tm, tn)],
        epilogue=dact, extras=[ha, hb], extra_specs=[hblk] * 2, job=dact_job, sub_rows=tm // 2), dact_job)
    tk = min(2048, t)
    th = FHP // 2
    dwo, dwo_moved = _with_moved(mm(
        tag + "_dwo", [g], [dy], [(0, 0, 0)], dims=TN, grid=(4, 2, t // tk),
        a_specs=[pl.BlockSpec((tk, th), lambda i, j, k: (k, i))],
        b_specs=[pl.BlockSpec((tk, D // 2), lambda i, j, k: (k, j))],
        outs=[sds((2 * FHP, D), BF16)], out_specs=[pl.BlockSpec((th, D // 2), lambda i, j, k: (i, j))],
        acc_shapes=[(th, D // 2)], job=dwo_job), dwo_job)
    dwo = dwo.reshape(2, FHP, D)

    def dwi_part(part, dh, carry, job):
        return mm(
            f"{tag}_dwi{part}", [dh], [u], [(0, 0, 0)], dims=TN, grid=(4, 2, t // tk),
            a_specs=[pl.BlockSpec((tk, th), lambda i, j, k: (k, i))],
            b_specs=[pl.BlockSpec((tk, D // 2), lambda i, j, k: (k, j))],
            outs=[sds((4, FHP, D), BF16)],
            out_specs=[pl.BlockSpec((None, th, D // 2), lambda i, j, k: (2 * part + i // 2, i % 2, j))],
            acc_shapes=[(th, D // 2)], carry=carry, job=job)

    dwi, (sib_fo,) = dwi_part(0, dha, None, reduce_sibling_job([(dwo, view_ffn_out, FO, D)]))
    q_fo = chip_sum(tag + "_chipsum_fo", dwo, sib_fo, sp, FO, FO // 2, ffn_out=True)
    dwi, (far_fo,) = dwi_part(1, dhb, dwi, reduce_chips_job([q_fo]))
    (sib_fi,) = run_job(tag + "_sibling_fi", reduce_sibling_job([(dwi, view_lead, FHP, D)]))
    q_fi = chip_sum(tag + "_chipsum_fi", dwi, sib_fi, sp, FHP, FHP // 8)
    du, (far_fi,) = mm(
        tag + "_du", [dha, dhb], [wi, wi], [(0, 0, 0), (1, 1, 0)], dims=NN, grid=(t // tm, 2, 4),
        a_specs=[pl.BlockSpec((tm, th), lambda i, j, k: (i, k))] * 2,
        b_specs=[pl.BlockSpec((None, th, D // 2), lambda i, j, k: (k // 2, k % 2, j)),
                 pl.BlockSpec((None, th, D // 2), lambda i, j, k: (2 + k // 2, k % 2, j))],
        outs=[sds((t, D), F32)], out_specs=[pl.BlockSpec((tm, D // 2), lambda i, j, k: (i, j))],
        acc_shapes=[(tm, D // 2)], job=reduce_chips_job([q_fi]))
    return du, (q_fi, far_fi), (q_fo, far_fo), dact_moved, dwo_moved


def mix_fwd(u, wts, b_in, pool_scale, sinks, tabs, t, in_job, attn_job):
    w_in, wp, wba, wbb, wo = wts
    tm = min(1024, t)
    tmh = min(512, t)
    h, in_moved = mm("mix_in", [u], [w_in], [(0, 0, 0)], dims=NN, grid=(t // tmh, 4, 1),
                     a_specs=[pl.BlockSpec((tmh, D), lambda i, j, k: (i, 0))],
                     b_specs=[pl.BlockSpec((None, D, IN_SH), lambda i, j, k: (j, 0, 0))],
                     outs=[sds((t, IN_W), F32)], out_specs=[pl.BlockSpec((tmh, IN_SH), lambda i, j, k: (i, j))],
                     acc_shapes=[(tmh, IN_SH)], job=in_job)
    attn_job = attn_job(in_moved)
    pooled = pool_fwd(h, b_in, t, 512)
    gblk = pl.BlockSpec((tm, PG), lambda i, j, k: (i, j))
    mixed = mm("mix_pool", [pooled], [wp], [(0, 0, 0)], dims=NN, grid=(t // tm, 4, 1), a_specs=[gblk],
               b_specs=[pl.BlockSpec((None, PG, PG), lambda i, j, k: (j, 0, 0))],
               outs=[sds((t, PW), F32)], out_specs=[gblk], acc_shapes=[(tm, PG)])
    pm = rowmap("mix_pscale", lambda mv, ps: mv * ps, [T_(mixed), B_(pool_scale)], [(PW, BF16)], rows=t, tm=512)

    def branch(name, a, w):
        return mm(name, [a], [w], [(0, 0, 0)], dims=NN, grid=(t // tm, 4, 1),
                  a_specs=[pl.BlockSpec((tm, PW), lambda i, j, k: (i, 0))],
                  b_specs=[pl.BlockSpec((None, PW, D // 4), lambda i, j, k: (j, 0, 0))],
                  outs=[sds((t, D), F32)], out_specs=[pl.BlockSpec((tm, D // 4), lambda i, j, k: (i, j))],
                  acc_shapes=[(tm, D // 4)])

    ya = branch("mix_branch_a", pm, wba)

    def qkv(hq, hk, hv, bq, bk, bv, cc, sa, sb):
        return (_rope(hq + bq, cc, sa, sb) * (HD ** -0.5), _rope(hk + bk, cc, sa, sb), hv + bv)

    qr, kr, vv = rowmap(
        "mix_rope", qkv,
        [T_(h, QW, 1), T_(h, KVW, 8), T_(h, KVW, 9), B_(b_in, QW, 1), B_(b_in, KVW, 8), B_(b_in, KVW, 9),
         T_(tabs[0]), T_(tabs[1]), T_(tabs[2])],
        [(QW, BF16), (KVW, BF16), (KVW, BF16)], rows=t, tm=512)
    attn, attn_moved = attn_fwd(qr, kr, vv, sinks, t, attn_job)
    yb = branch("mix_branch_b", attn, wbb)
    cw = 512

    def merge(ga, gb, ba, bb, yav, ybv):
        return _sigmoid(ga + ba) * yav + _sigmoid(gb + bb) * ybv

    merged = rowmap(
        "mix_merge", merge,
        [T_(h, cw, 5), T_(h, cw, 9), B_(b_in, cw, 5), B_(b_in, cw, 9), T_(ya, cw), T_(yb, cw)],
        [(D, BF16)], rows=t, tm=512, ncol=D // cw)
    y = mm("mix_out", [merged], [wo], [(0, 0, 0)], dims=NN, grid=(t // tm, 2, 1),
           a_specs=[pl.BlockSpec((tm, D), lambda i, j, k: (i, 0))],
           b_specs=[pl.BlockSpec((D, D // 2), lambda i, j, k: (0, j))],
           outs=[sds((t, D), F32)], out_specs=[pl.BlockSpec((tm, D // 2), lambda i, j, k: (i, j))],
           acc_shapes=[(tm, D // 2)])
    return y, (h, pooled, mixed, pm, ya, qr, kr, vv, attn, yb, merged), attn_moved


def mix_bwd(u, saved, dy, wts, b_in, pool_scale, sinks, tabs, t):
    h, pooled, mixed, pm, ya, qr, kr, vv, attn, yb, merged = saved
    w_in, wp, wba, wbb, wo = wts
    tm = min(1024, t)
    tk = min(2048, t)
    dmerged = mm("mix_dmerged", [dy], [wo], [(0, 0, 0)], dims=NT, grid=(t // tm, 2, 1),
                 a_specs=[pl.BlockSpec((tm, D), lambda i, j, k: (i, 0))],
                 b_specs=[pl.BlockSpec((D // 2, D), lambda i, j, k: (j, 0))],
                 outs=[sds((t, D), F32)], out_specs=[pl.BlockSpec((tm, D // 2), lambda i, j, k: (i, j))],
                 acc_shapes=[(tm, D // 2)])
    half = pl.BlockSpec((tk, D // 2), lambda i, j, k: (k, i))
    dwo = mm("mix_dwo", [merged], [dy], [(0, 0, 0)], dims=TN, grid=(2, 2, t // tk), a_specs=[half],
             b_specs=[pl.BlockSpec((tk, D // 2), lambda i, j, k: (k, j))],
             outs=[sds((D, D), BF16)], out_specs=[pl.BlockSpec((D // 2, D // 2), lambda i, j, k: (i, j))],
             acc_shapes=[(D // 2, D // 2)])
    cw = 512

    def dmerge(dm, ga, gb, ba, bb, yav, ybv):
        sa_, sb_ = _sigmoid(ga + ba), _sigmoid(gb + bb)
        dga = dm * yav * sa_ * (1.0 - sa_)
        dgb = dm * ybv * sb_ * (1.0 - sb_)
        return dm * sa_, dm * sb_, dga, dgb, colsum(dga), colsum(dgb)

    dya, dyb, dgla, dglb, dbga, dbgb = rowmap(
        "mix_dmerge", dmerge,
        [T_(dmerged, cw), T_(h, cw, 5), T_(h, cw, 9), B_(b_in, cw, 5), B_(b_in, cw, 9), T_(ya, cw), T_(yb, cw)],
        [(D, BF16)] * 4, [(1, D), (1, D)], rows=t, tm=512, ncol=D // cw)

    def dbranch(name, dyv, act, w):
        dwb = mm(name + "_dw", [act], [dyv], [(0, 0, 0)], dims=TN, grid=(1, 4, t // tk),
                 a_specs=[pl.BlockSpec((tk, PW), lambda i, j, k: (k, 0))],
                 b_specs=[pl.BlockSpec((tk, D // 4), lambda i, j, k: (k, j))],
                 outs=[sds((4, PW, D // 4), BF16)], out_specs=[pl.BlockSpec((None, PW, D // 4), lambda i, j, k: (j, 0, 0))],
                 acc_shapes=[(PW, D // 4)])
        return dwb, lambda dt: mm(
            name + "_dx", [dyv], [w], [(0, 0, 0)], dims=NT, grid=(t // tm, 1, 4),
            a_specs=[pl.BlockSpec((tm, D // 4), lambda i, j, k: (i, k))],
            b_specs=[pl.BlockSpec((None, PW, D // 4), lambda i, j, k: (k, 0, 0))],
            outs=[sds((t, PW), dt)], out_specs=[pl.BlockSpec((tm, PW), lambda i, j, k: (i, 0))], acc_shapes=[(tm, PW)])

    dwba, dpm_fn = dbranch("mix_dbranch_a", dya, pm, wba)
    dwbb, dattn_fn = dbranch("mix_dbranch_b", dyb, attn, wbb)
    dpm, dattn = dpm_fn(F32), dattn_fn(BF16)
    dmixed, dps = rowmap("mix_dpscale", lambda dp, mv, ps: (dp * ps, colsum(dp * mv)),
                         [T_(dpm), T_(mixed), B_(pool_scale)], [(PW, BF16)], [(1, PW)], rows=t, tm=512)
    gblk = pl.BlockSpec((tm, PG), lambda i, j, k: (i, j))
    dpooled = mm("mix_dpool", [dmixed], [wp], [(0, 0, 0)], dims=NT, grid=(t // tm, 4, 1), a_specs=[gblk],
                 b_specs=[pl.BlockSpec((None, PG, PG), lambda i, j, k: (j, 0, 0))],
                 outs=[sds((t, PW), F32)], out_specs=[gblk], acc_shapes=[(tm, PG)])
    kblk = pl.BlockSpec((tk, PG), lambda i, j, k: (k, i))
    dwp = mm("mix_dwpool", [pooled], [dmixed], [(0, 0, 0)], dims=TN, grid=(4, 1, t // tk), a_specs=[kblk], b_specs=[kblk],
             outs=[sds((4, PG, PG), BF16)], out_specs=[pl.BlockSpec((None, PG, PG), lambda i, j, k: (i, 0, 0))],
             acc_shapes=[(PG, PG)])
    dxp, dbxp = pool_bwd(dpooled, t, 512)
    dqr, dkr, dvv, dsinks = attn_bwd(qr, kr, vv, dattn, sinks, t)

    def dqkv(dq, dk, dv, cc, sa, sb):
        dq = _rope_t(dq, cc, sa, sb) * (HD ** -0.5)
        dk = _rope_t(dk, cc, sa, sb)
        return dq, dk, dv, colsum(dq), colsum(dk), colsum(dv)

    dq, dk, dvb, dbq, dbk, dbv = rowmap(
        "mix_rope_bwd", dqkv, [T_(dqr), T_(dkr), T_(dvv), T_(tabs[0]), T_(tabs[1]), T_(tabs[2])],
        [(QW, BF16), (KVW, BF16), (KVW, BF16)], [(1, QW), (1, KVW), (1, KVW)], rows=t, tm=512)
    dh = jnp.concatenate([dxp, dq, dk, dvb, dgla, dglb], axis=1)
    db_in = jnp.concatenate([dbxp, dbq, dbk, dbv, dbga, dbgb], axis=1)
    dwin = mm("mix_dwin", [u], [dh], [(0, 0, 0)], dims=TN, grid=(2, 4, t // tk), a_specs=[half],
              b_specs=[pl.BlockSpec((tk, IN_SH), lambda i, j, k: (k, j))],
              outs=[sds((4, D, IN_SH), BF16)], out_specs=[pl.BlockSpec((None, D // 2, IN_SH), lambda i, j, k: (j, i, 0))],
              acc_shapes=[(D // 2, IN_SH)])
    du = mm("mix_du", [dh], [w_in], [(0, 0, 0)], dims=NT, grid=(t // tm, 2, 4),
            a_specs=[pl.BlockSpec((tm, IN_SH), lambda i, j, k: (i, k))],
            b_specs=[pl.BlockSpec((None, D // 2, IN_SH), lambda i, j, k: (k, j, 0))],
            outs=[sds((t, D), F32)], out_specs=[pl.BlockSpec((tm, D // 2), lambda i, j, k: (i, j))],
            acc_shapes=[(tm, D // 2)])
    return du, (dwin, dwp, dwba, dwbb, dwo), db_in, dps, dsinks


def cast_shard(name, w, sp, ffn_out=False):
    rows, cols = w.shape
    if ffn_out:
        tm = rows // 2
        shape = (2, FHP, D)
        spec = pl.BlockSpec((None, tm, cols), lambda j, i, s: (s[0] // 2, (s[0] % 2) * 2 + i, 0))
    else:
        tm = rows // 4
        shape = (4, rows, cols)
        spec = pl.BlockSpec((None, tm, cols), lambda j, i, s: (s[0], i, 0))
    return rowmap(name, lambda wv: wv, [T_(w)], [(shape, BF16, spec)], rows=rows, tm=tm, sp=sp)


def cast_ffn_in(name, wt, sp):
    tm = 64
    full = FH // tm

    def fn(_, i, wv):
        return jnp.where(i < full, wv, 0.0)

    return rowmap(name, fn, [X_(wt, pl.BlockSpec((tm, D), lambda j, i, s: (jnp.minimum(i, full - 1), 0)))],
                  [((4, FHP, D), BF16, pl.BlockSpec((None, tm, D), lambda j, i, s: (s[0], i, 0)))],
                  rows=FHP, tm=tm, sp=sp, with_ids=True)


def chip_sum(name, dw, got, sp, rows, tm, ffn_out=False):
    hr, cols = rows // 2, got.shape[2]
    per = hr // tm
    pos = pl.BlockSpec((None, tm, cols), lambda j, i, s: (i // per, i % per, 0))
    if ffn_out:
        mine = pl.BlockSpec((None, tm, cols), lambda j, i, s: (i // 2, (i % 2) * 2 + s[1], 0))
    else:
        mine = pl.BlockSpec((None, tm, cols), lambda j, i, s: (i // per, s[1] * per + i % per, 0))
    return rowmap(name, lambda av, bv: av.astype(F32) + bv.astype(F32), [X_(dw, mine), X_(got, pos)],
                  [(got.shape, BF16, pos)], rows=4 * hr, tm=tm, sp=sp)


def chip_total(name, q, got, sp, rows, tm):
    hr, cols = rows // 2, q.shape[2]
    per = hr // tm

    def part(f):
        return X_(got, pl.BlockSpec((None, tm, cols), lambda j, i, s, f=f: (f, i, 0)))

    return rowmap(
        name, lambda av, b0, b1, b2: ((av.astype(F32) + b0.astype(F32)) + b1.astype(F32)) + b2.astype(F32),
        [X_(q, pl.BlockSpec((None, tm, cols), lambda j, i, s: (s[0], i, 0))), part(0), part(1), part(2)],
        [((rows, cols), F32, pl.BlockSpec((tm, cols), lambda j, i, s: (s[1] * per + i, 0)))], rows=hr, tm=tm, sp=sp)


def kernel(x, c, w_ada, b_ada, ln_g, ln_b, w_ffn1_in, w_ffn1_out, w_in, b_in, w_pool, pool_scale, sinks, w_branch_a, w_branch_b, w_out, w_ffn2_in, w_ffn2_out, loss_target, m_w_ada, m_b_ada, m_ln_g, m_ln_b, m_w_ffn1_in, m_w_ffn1_out, m_w_in, m_b_in, m_w_pool, m_pool_scale, m_sinks, m_w_branch_a, m_w_branch_b, m_w_out, m_w_ffn2_in, m_w_ffn2_out, v_w_ada, v_b_ada, v_ln_g, v_ln_b, v_w_ffn1_in, v_w_ffn1_out, v_w_in, v_b_in, v_w_pool, v_pool_scale, v_sinks, v_w_branch_a, v_w_branch_b, v_w_out, v_w_ffn2_in, v_w_ffn2_out):
    t = x.shape[1]
    xs, tgt = x[0], loss_target[0]
    xi, yi, ci = lax.axis_index("x"), lax.axis_index("y"), lax.axis_index("c")
    chip = 2 * xi + yi
    dev = 2 * chip + ci
    b_in2, ps2, sinks2 = b_in, pool_scale, sinks

    first = jnp.concatenate([c.reshape(-1), ln_g.reshape(-1), ln_b.reshape(-1)]).reshape(-1, 128)
    first_all = allgather_small("gather_cond", first).reshape(8, -1)
    c_all = first_all[:, :D]
    ln_parts = first_all[0::2, D:].reshape(4, 2, 3, D // 4)
    ln_full = jnp.transpose(ln_parts, (1, 2, 0, 3)).reshape(2, 3, D)
    lgs = [ln_full[0, s:s + 1] for s in range(3)]
    lbs = [ln_full[1, s:s + 1] for s in range(3)]
    c16 = jnp.pad(c_all, ((0, 8), (0, 0)))
    b_ada_sh = lax.dynamic_slice(b_ada, (0, chip * ADA_SH), (1, ADA_SH))
    mod_part = ada_fwd(c16, w_ada[0], b_ada_sh)[:8]
    mod_all = allgather_small("gather_mod", mod_part.reshape(-1, 128)).reshape(8, 8, ADA_SH)
    mod_mine = lax.dynamic_index_in_dim(mod_all[0::2], dev, axis=1, keepdims=False).reshape(9, D)
    mods = [[mod_mine[3 * s + k:3 * s + k + 1] for k in range(3)] for s in range(3)]

    plain = [("f1o", w_ffn1_out[0]), ("win", w_in[0]), ("wp", w_pool[0].reshape(4 * 64, PG)), ("wba", w_branch_a[0]),
             ("wbb", w_branch_b[0]), ("wo", w_out[0]), ("f2o", w_ffn2_out[0])]
    sp = jnp.stack([chip, ci]).astype(jnp.int32)
    sh = {n: cast_shard("cast_" + n, w, sp, ffn_out=n in ("f1o", "f2o")) for n, w in plain}
    tr = lambda a: jnp.swapaxes(a[0], 0, 1)
    sh["f1i"] = cast_ffn_in("cast_f1i", tr(w_ffn1_in), sp)
    sh["f2i"] = cast_ffn_in("cast_f2i", tr(w_ffn2_in), sp)
    order = ["f1i", "f1o", "win", "wp", "wba", "wbb", "wo", "f2i", "f2o"]
    views = {n: (view_ffn_out if n in ("f1o", "f2o") else view_lead) for n in order}
    shard_rows = {n: (FO if n in ("f1o", "f2o") else sh[n].shape[1]) for n in order}
    shard_cols = {n: sh[n].shape[2] for n in order}
    tiles = {"f1i": FHP // 8, "f1o": FO // 2, "win": 512, "wp": 128, "wba": 512, "wbb": 512, "wo": 256,
             "f2i": FHP // 8, "f2o": FO // 2}

    def item(n, part=0, parts=1):
        return (sh[n], views[n], shard_rows[n], part, parts)

    tabs = rope_tables(t)
    (sh0, sc0, gt0), (sh1, sc1, gt1), (sh2, sc2, gt2) = mods

    (g_f1i,) = run_job("gather_f1i", gather_job([item("f1i")]))
    u0 = modulate("ffn1_mod", xs, sh0, sc0, t)
    ha1, hb1, g1, y1, f1o, up1, (g_win,) = ffn_fwd(
        "ffn1", u0, g_f1i, t, gather_job([item(n) for n in ("f1o", "wp", "wba", "wbb", "wo")]),
        gather_job([item("win")]))
    x1, z1, u1 = residual_ln_mod("ffn1_ln", xs, y1, gt0, lgs[0], lbs[0], 0.5, sh1, sc1, t)
    _, g_wp, g_wba, g_wbb, g_wo = up1
    wp_full = jnp.transpose(g_wp.reshape(4, 4, 64, PG), (1, 0, 2, 3)).reshape(4, PG, PG)
    wts = (g_win, wp_full, g_wba, g_wbb, g_wo.reshape(D, D))
    y2, sv2, (g_f2i,) = mix_fwd(
        u1, wts, b_in2, ps2, sinks2, tabs, t, gather_job([item("f2i", 0, 2)]),
        lambda moved: gather_job([(moved[0], view_lead, FHP, 1, 2)]))
    x2, z2, u2 = residual_ln_mod("mix_ln", x1, y2, gt1, lgs[1], lbs[1], 1.0, sh2, sc2, t)
    ha3, hb3, g3, y3, f2o, _, _ = ffn_fwd("ffn2", u2, g_f2i, t, gather_job([item("f2o")]))

    dz3, dy3, dlg2, dlb2, dgt2, sq = residual_ln_loss_bwd("ffn2_ln_loss", x2, y3, tgt, gt2, lgs[2], lbs[2], 0.5, t)
    loss = lax.psum(0.5 * sq[0, 0] / D, ("x", "y", "c"))
    du3, red_f2i, red_f2o, _, _ = ffn_bwd("ffn2", u2, ha3, hb3, g3, dy3, g_f2i, f2o, t, sp)
    dz2, dy2, dlg1, dlb1, dgt1, dsh2, dsc2 = residual_ln_bwd("mix_ln_bwd", z2, (dz3, du3, x2, sc2), y2, gt1, lgs[1], 1.0, t)
    du2, dmix, db_in, dps, dsinks = mix_bwd(u1, sv2, dy2, wts, b_in2, ps2, sinks2, tabs, t)
    dwin, dwp, dwba, dwbb, dwo = dmix
    dwp_sh = jnp.transpose(dwp.reshape(4, 4, 64, PG), (1, 0, 2, 3)).reshape(4, 4 * 64, PG)
    mix_parts = {"win": dwin, "wp": dwp_sh, "wba": dwba, "wbb": dwbb, "wo": dwo.reshape(4, D // 4, D)}
    mix_names = list(mix_parts)
    sib = run_job("sibling_mix", reduce_sibling_job(
        [(mix_parts[n], view_lead, shard_rows[n], shard_cols[n]) for n in mix_names]))
    q = {n: chip_sum("chipsum_" + n, mix_parts[n], g, sp, shard_rows[n], tiles[n]) for n, g in zip(mix_names, sib)}
    dz1, dy1, dlg0, dlb0, dgt0, dsh1, dsc1 = residual_ln_bwd("ffn1_ln_bwd", z1, (dz2, du2, x1, sc1), y1, gt0, lgs[0], 0.5, t)
    du1, red_f1i, red_f1o, far_a, far_b = ffn_bwd(
        "ffn1", u0, ha1, hb1, g1, dy1, g_f1i, f1o, t, sp,
        reduce_chips_job([q["win"], q["wp"]]), reduce_chips_job([q["wo"], q["wba"], q["wbb"]]))
    dx0, dsh0, dsc0 = modulate_bwd("ffn1_mod_bwd", dz1, du1, xs, sc0, t)
    gm0, gm1, gm2 = (dsh0, dsc0, dgt0), (dsh1, dsc1, dgt1), (dsh2, dsc2, dgt2)
    reduced = {"f1i": red_f1i, "f1o": red_f1o, "f2i": red_f2i, "f2o": red_f2o, "win": (q["win"], far_a[0]),
               "wp": (q["wp"], far_a[1]), "wo": (q["wo"], far_b[0]), "wba": (q["wba"], far_b[1]), "wbb": (q["wbb"], far_b[2])}
    halves = [chip_total("total_" + n, *reduced[n], sp, shard_rows[n], tiles[n]) for n in order]
    gw = dict(zip(order, run_job("share_halves", share_halves_job(halves))))

    small = jnp.concatenate([*gm0, *gm1, *gm2, dlg0, dlg1, dlg2, dlb0, dlb1, dlb2, db_in, dps, dsinks], axis=1)
    n_small = small.shape[1]
    rows_small = -(-n_small // 1024) * 8
    small = jnp.pad(small, ((0, 0), (0, rows_small * 128 - n_small))).reshape(rows_small, 128)
    small_all = allgather_small("gather_small", small)
    tot = sum_devices(small_all).reshape(1, -1)
    gmod_all = small_all.reshape(8, -1)[:, :9 * D]
    o = 9 * D
    g_b_ada = tot[:, :o]
    g_ln_g = lax.dynamic_slice(tot[:, o:o + 3 * D].reshape(3, D), (0, chip * (D // 4)), (3, D // 4))
    g_ln_b = lax.dynamic_slice(tot[:, o + 3 * D:o + 6 * D].reshape(3, D), (0, chip * (D // 4)), (3, D // 4))
    o += 6 * D
    g_b_in, g_ps, g_sinks = tot[:, o:o + IN_W], tot[:, o + IN_W:o + IN_W + PW], tot[:, o + IN_W + PW:o + IN_W + PW + N_Q]

    gm16 = jnp.pad(lax.dynamic_slice(gmod_all, (0, chip * ADA_SH), (8, ADA_SH)), ((0, 8), (0, 0)))
    g_w_ada, d_w_ada, nm_w_ada, nv_w_ada = ada_bwd_adam(c16, gm16, w_ada[0], m_w_ada[0], v_w_ada[0])

    def big(n, w, m, v, tm):
        shape = w.shape
        w2, m2, v2 = (a.reshape(shape[-2] if a.ndim == 3 else -1, shape[-1]) for a in (w, m, v))
        return [r.reshape(shape) for r in adam_rows("adam_" + n, w2, gw[n], m2, v2, tm)]

    def big_t(n, w, m, v):
        return [jnp.swapaxes(r, 0, 1)[None] for r in adam_rows("adam_" + n, tr(w), gw[n], tr(m), tr(v), 64)]

    def tiny(n, w, g, m, v):
        return [g.reshape(w.shape)] + list(adam_small("adam_" + n, w, g.reshape(w.shape), m, v))

    res = {
        "w_ada": [a[None] for a in (g_w_ada, d_w_ada, nm_w_ada, nv_w_ada)],
        "b_ada": tiny("b_ada", b_ada, g_b_ada, m_b_ada, v_b_ada),
        "ln_g": tiny("ln_g", ln_g, g_ln_g, m_ln_g, v_ln_g),
        "ln_b": tiny("ln_b", ln_b, g_ln_b, m_ln_b, v_ln_b),
        "w_ffn1_in": big_t("f1i", w_ffn1_in, m_w_ffn1_in, v_w_ffn1_in),
        "w_ffn1_out": big("f1o", w_ffn1_out, m_w_ffn1_out, v_w_ffn1_out, 32),
        "w_in": big("win", w_in, m_w_in, v_w_in, 256),
        "b_in": tiny("b_in", b_in, g_b_in, m_b_in, v_b_in),
        "w_pool": big("wp", w_pool, m_w_pool, v_w_pool, 256),
        "pool_scale": tiny("pool_scale", pool_scale, g_ps, m_pool_scale, v_pool_scale),
        "sinks": tiny("sinks", sinks, g_sinks, m_sinks, v_sinks),
        "w_branch_a": big("wba", w_branch_a, m_w_branch_a, v_w_branch_a, 512),
        "w_branch_b": big("wbb", w_branch_b, m_w_branch_b, v_w_branch_b, 512),
        "w_out": big("wo", w_out, m_w_out, v_w_out, 128),
        "w_ffn2_in": big_t("f2i", w_ffn2_in, m_w_ffn2_in, v_w_ffn2_in),
        "w_ffn2_out": big("f2o", w_ffn2_out, m_w_ffn2_out, v_w_ffn2_out, 32),
    }
    names = ["w_ada", "b_ada", "ln_g", "ln_b", "w_ffn1_in", "w_ffn1_out", "w_in", "b_in", "w_pool", "pool_scale", "sinks",
             "w_branch_a", "w_branch_b", "w_out", "w_ffn2_in", "w_ffn2_out"]
    return (loss, dx0[None], *[res[n][0] for n in names], *[res[n][1] for n in names],
            *[res[n][2] for n in names], *[res[n][3] for n in names])
```

```python
import jax
import jax.numpy as jnp
from jax import lax
from jax.experimental import pallas as pl
from jax.experimental.pallas import tpu as pltpu

F32 = jnp.float32
BF16 = jnp.bfloat16
MESH = pl.DeviceIdType.MESH
ANY = pl.BlockSpec(memory_space=pl.ANY)

D = 2048
N_Q, N_KV, HD = 16, 4, 64
QW, KVW = N_Q * HD, N_KV * HD
BLK = 128
POOL_WINDOWS = (2, 4, 8, 16)
PW, PG = 1024, 256
HALO = 16
ROPE_THETA = 500000.0
ROT = HD // 4
LN_EPS = 1e-5
ALPHA = 2.0 ** 0.25
FH = 2752
FHP = 2816
FO = 1376
IN_W = 6656
IN_SH = IN_W // 4
ADA_SH = 18432 // 4
B1, B2, LR, EPS, WD, STEP = 0.9, 0.999, 0.001, 1e-08, 0.01, 10
VMEM_LIMIT = 56 * 1024 * 1024
FLIPS = ((1, 0), (0, 1), (1, 1))
NN = (((1,), (0,)), ((), ()))
NT = (((1,), (1,)), ((), ()))
TN = (((0,), (0,)), ((), ()))


def _params(sem):
    return pltpu.CompilerParams(dimension_semantics=sem, vmem_limit_bytes=VMEM_LIMIT)


def _aligned(v, m):
    return v if isinstance(v, int) else pl.multiple_of(v, m)


def _sigmoid(v):
    return 1.0 / (1.0 + jnp.exp(-v))


def T_(arr, width=None, off=0):
    return ("t", arr, width, off)


def B_(arr, width=None, off=0):
    return ("b", arr, width, off)


def X_(arr, spec):
    return ("x", arr, spec, 0)


def rowmap(name, fn, ins, outs, accs=(), *, rows, tm, ncol=1, with_ids=False, sp=None, alias=None):
    tm = min(tm, rows)
    nrow = rows // tm
    in_specs, arrs = [], []
    for kind, arr, width, off in ins:
        if kind == "x":
            in_specs.append(width)
        elif kind == "t":
            w = arr.shape[1] if width is None else width
            in_specs.append(pl.BlockSpec((tm, w), lambda j, i, *_, off=off: (i, off + j)))
        else:
            w = arr.shape[1] if width is None else width
            in_specs.append(pl.BlockSpec((arr.shape[0], w), lambda j, i, *_, off=off: (0, off + j)))
        arrs.append(arr)
    out_shape, out_specs = [], []
    for o in outs:
        if len(o) == 3:
            out_shape.append(jax.ShapeDtypeStruct(o[0], o[1]))
            out_specs.append(o[2])
        else:
            out_shape.append(jax.ShapeDtypeStruct((rows, o[0]), o[1]))
            out_specs.append(pl.BlockSpec((tm, o[0] // ncol), lambda j, i, *_: (i, j)))
    for r, width in accs:
        out_shape.append(jax.ShapeDtypeStruct((r, width), F32))
        out_specs.append(pl.BlockSpec((r, width // ncol), lambda j, i, *_: (0, j)))
    ni, no = len(ins), len(outs)
    nsp = 0 if sp is None else 1

    def body(*refs):
        refs = refs[nsp:]
        i = pl.program_id(1)
        vals = [r[...] for r in refs[:ni]]
        res = fn(pl.program_id(0), i, *vals) if with_ids else fn(*vals)
        if not isinstance(res, (tuple, list)):
            res = (res,)
        for r, v in zip(refs[ni:ni + no], res[:no]):
            r[...] = v.astype(r.dtype)
        for r, v in zip(refs[ni + no:], res[no:]):
            @pl.when(i == 0)
            def _(r=r, v=v):
                r[...] = v

            @pl.when(i > 0)
            def _(r=r, v=v):
                r[...] += v

    grid_spec = pltpu.PrefetchScalarGridSpec(num_scalar_prefetch=nsp, grid=(ncol, nrow), in_specs=in_specs,
                                             out_specs=out_specs)
    res = pl.pallas_call(
        body, name=name, grid_spec=grid_spec, out_shape=out_shape,
        input_output_aliases={nsp + k: v for k, v in (alias or {}).items()},
        compiler_params=_params(("arbitrary", "arbitrary")),
    )(*([sp] if nsp else []), *arrs)
    return res[0] if len(res) == 1 else res


def colsum(v):
    return jnp.sum(v, axis=0, keepdims=True)


def mm(name, a_ops, b_ops, ops, *, dims, grid, a_specs, b_specs, outs, out_specs, acc_shapes,
       epilogue=None, extras=(), extra_specs=(), carry=None, job=None, sub_rows=None):
    gk = grid[2]
    na, nb, ne, nacc = len(a_ops), len(b_ops), len(extras), len(acc_shapes)
    nc = 0 if carry is None else 1
    no = len(outs)

    def body(*refs):
        a_refs = refs[:na]
        b_refs = refs[na:na + nb]
        e_refs = refs[na + nb:na + nb + ne]
        o_refs = refs[na + nb + ne + nc:na + nb + ne + nc + no]
        acc_refs = refs[na + nb + ne + nc + no:]
        k = pl.program_id(2)

        def partials(rows=slice(None)):
            res = [None] * nacc
            for ai, bi, ci in ops:
                p = lax.dot_general(a_refs[ai][rows], b_refs[bi][...], dims, preferred_element_type=F32)
                res[ci] = p if res[ci] is None else res[ci] + p
            return res

        def finish(accs, rows=slice(None)):
            outv = epilogue(accs, [e[rows] for e in e_refs]) if epilogue else (accs[0],)
            for o, v in zip(o_refs, outv):
                o[rows] = v.astype(o.dtype)

        if gk == 1 and sub_rows:
            for s in range(out_specs[0].block_shape[-2] // sub_rows):
                rows = pl.ds(s * sub_rows, sub_rows)
                finish(partials(rows), rows)
        elif gk == 1:
            finish(partials())
        else:
            ps = partials()

            @pl.when(k == 0)
            def _():
                for acc, p in zip(acc_refs, ps):
                    acc[...] = p

            @pl.when((k > 0) & (k < gk - 1))
            def _():
                for acc, p in zip(acc_refs, ps):
                    acc[...] += p

            @pl.when(k == gk - 1)
            def _():
                finish([acc[...] + p for acc, p in zip(acc_refs, ps)])

    res, moved = carried_call(
        body, name, grid,
        list(a_specs) + list(b_specs) + list(extra_specs) + ([ANY] if nc else []), list(out_specs), list(outs),
        [pltpu.VMEM(s, F32) for s in acc_shapes] if gk > 1 else [],
        [*a_ops, *b_ops, *extras, *([carry] if nc else [])], {na + nb + ne: 0} if nc else {}, job)
    res = res[0] if len(res) == 1 else res
    return res if job is None else (res, moved)


def sds(shape, dt):
    return jax.ShapeDtypeStruct(shape, dt)


class Job:
    def __init__(self, ins, outs, aliases, scratch, start, mid, finish):
        self.ins, self.outs, self.aliases, self.scratch = list(ins), list(outs), dict(aliases), list(scratch)
        self.start, self.mid, self.finish = start, mid, finish


def carried_call(body, name, grid, in_specs, out_specs, out_shape, scratch, args, aliases, job, mid_at=0.9):
    sem = ("arbitrary",) * len(grid)
    if job is None:
        res = pl.pallas_call(body, name=name, grid=grid, in_specs=in_specs, out_specs=out_specs, out_shape=out_shape,
                             scratch_shapes=scratch, input_output_aliases=aliases, compiler_params=_params(sem))(*args)
        return list(res), []
    ni, no, ns = len(in_specs), len(out_specs), len(scratch)
    ci, co = len(job.ins), len(job.outs)
    total = 1
    for g in grid:
        total *= g
    mid_step = min(max(int(total * mid_at), 1), total - 1)

    def full(*refs):
        ins, cins = refs[:ni], refs[ni:ni + ci]
        outs, couts = refs[ni + ci:ni + ci + no], refs[ni + ci + no:ni + ci + no + co]
        scr, cscr = refs[ni + ci + no + co:ni + ci + no + co + ns], refs[ni + ci + no + co + ns:]
        step = 0
        for d, g in enumerate(grid):
            step = step * g + pl.program_id(d)

        @pl.when(step == 0)
        def _():
            job.start(cins, couts, cscr)

        body(*ins, *outs, *scr)

        @pl.when(step == mid_step)
        def _():
            job.mid(cins, couts, cscr)

        @pl.when(step == total - 1)
        def _():
            job.finish(cins, couts, cscr)

    al = dict(aliases)
    al.update({ni + k: no + v for k, v in job.aliases.items()})
    res = pl.pallas_call(
        full, name=name, grid=grid, in_specs=in_specs + [ANY] * ci, out_specs=out_specs + [ANY] * co,
        out_shape=out_shape + job.outs, scratch_shapes=scratch + job.scratch, input_output_aliases=al,
        compiler_params=_params(sem))(*args, *job.ins)
    return list(res[:no]), list(res[no:])


def _with_moved(res, job):
    return res if job is not None else (res, [])


def run_job(name, job):
    ci = len(job.ins)

    def body(*refs):
        cins, couts, cscr = refs[:ci], refs[ci:ci + len(job.outs)], refs[ci + len(job.outs):]
        job.start(cins, couts, cscr)
        job.mid(cins, couts, cscr)
        job.finish(cins, couts, cscr)

    return list(pl.pallas_call(
        body, name=name, in_specs=[ANY] * ci, out_specs=[ANY] * len(job.outs), out_shape=job.outs,
        scratch_shapes=job.scratch, input_output_aliases=job.aliases)(*job.ins))


def _place():
    x, y, c = lax.axis_index("x"), lax.axis_index("y"), lax.axis_index("c")
    chips = [((1 - x) if fx else x, (1 - y) if fy else y) for fx, fy in FLIPS]
    return x, y, c, chips


def allgather_small(name, v):
    r = v.shape[0]

    def body(x_ref, out_ref, send_sems, recv_sems, local_sem):
        x, y, c, chips = _place()
        me, sibling = (x, y, c), (x, y, 1 - c)

        def rows(px, py, pc):
            return out_ref.at[4 * px + 2 * py + pc]

        def copy(k, block, to, src=None):
            return pltpu.make_async_remote_copy(
                src_ref=rows(*block) if src is None else src, dst_ref=rows(*block),
                send_sem=send_sems.at[k], recv_sem=recv_sems.at[k], device_id=to, device_id_type=MESH)

        mine = pltpu.make_async_copy(x_ref, rows(*me), local_sem)
        mine.start()
        first = [copy(0, me, sibling, src=x_ref)]
        first += [copy(1 + j, me, (*chip, c), src=x_ref) for j, chip in enumerate(chips)]
        for cp in first:
            cp.start()
        passed = [copy(4 + j, (*chip, c), sibling) for j, chip in enumerate(chips)]
        for j, chip in enumerate(chips):
            copy(1 + j, (*chip, c), me).wait_recv()
            passed[j].start()
        copy(0, sibling, me).wait_recv()
        for j, chip in enumerate(chips):
            copy(4 + j, (*chip, 1 - c), me).wait_recv()
        for cp in first + passed:
            cp.wait_send()
        mine.wait()

    return pl.pallas_call(
        body, name=name, out_shape=sds((8, r, 128), v.dtype),
        in_specs=[pl.BlockSpec(memory_space=pltpu.VMEM)], out_specs=pl.BlockSpec(memory_space=pltpu.VMEM),
        scratch_shapes=[pltpu.SemaphoreType.DMA((7,)), pltpu.SemaphoreType.DMA((7,)), pltpu.SemaphoreType.DMA],
    )(v)


def _half(ref, rows, hf):
    hr = rows // 2
    return ref.at[pl.ds(_aligned(hf * hr, 16), hr)]


def view_lead(ref, p):
    return ref.at[p]


def view_ffn_out(ref, p):
    return ref.at[p // 2, pl.ds(_aligned((p % 2) * FO, 16), FO)]


def _remote(ref, dst, send_sems, recv_sems, idx, to):
    return pltpu.make_async_remote_copy(src_ref=ref, dst_ref=dst, send_sem=send_sems.at[idx], recv_sem=recv_sems.at[idx],
                                        device_id=to, device_id_type=MESH)


def gather_job(items):
    nw = len(items)
    pads = [w for w, it in enumerate(items) if it[1] is view_ffn_out]

    def piece(ref, w, p, hf):
        _, view, rws, part, parts = items[w]
        pr = rws // 2 // parts
        return view(ref, p).at[pl.ds(_aligned(hf * (rws // 2) + part * pr, 16), pr)]

    def pad_copies(outs, scr):
        return [pltpu.make_async_copy(scr[2], outs[w].at[h, pl.ds(2 * FO, FHP - 2 * FO)], scr[3].at[2 * n + h])
                for n, w in enumerate(pads) for h in range(2)]

    def start(_, outs, scr):
        x, y, c, chips = _place()
        if pads:
            scr[2][...] = jnp.zeros_like(scr[2])
            for cp in pad_copies(outs, scr):
                cp.start()
        for w in range(nw):
            mine = piece(outs[w], w, 2 * x + y, c)
            for f, (px, py) in enumerate(chips):
                _remote(mine, mine, scr[0], scr[1], (w, f), (px, py, c)).start()

    def mid(_, outs, scr):
        x, y, c, chips = _place()
        for w in range(nw):
            for f, (px, py) in enumerate(chips):
                land = piece(outs[w], w, 2 * px + py, c)
                _remote(land, land, scr[0], scr[1], (w, f), (px, py, c)).wait_recv()
                _remote(land, land, scr[0], scr[1], (w, 3 + f), (x, y, 1 - c)).start()

    def finish(_, outs, scr):
        x, y, c, chips = _place()
        for w in range(nw):
            for f, (px, py) in enumerate(chips):
                land = piece(outs[w], w, 2 * px + py, 1 - c)
                _remote(land, land, scr[0], scr[1], (w, 3 + f), (x, y, 1 - c)).wait_recv()
        for w in range(nw):
            mine = piece(outs[w], w, 2 * x + y, c)
            for f in range(6):
                _remote(mine, mine, scr[0], scr[1], (w, f), (x, y, 1 - c)).wait_send()
        for cp in pad_copies(outs, scr):
            cp.wait()

    scratch = [pltpu.SemaphoreType.DMA((nw, 6)), pltpu.SemaphoreType.DMA((nw, 6))]
    if pads:
        scratch += [pltpu.VMEM((FHP - 2 * FO, D), BF16), pltpu.SemaphoreType.DMA((2 * len(pads),))]
    bufs = [it[0] for it in items]
    return Job(bufs, [sds(b.shape, BF16) for b in bufs], {w: w for w in range(nw)}, scratch, start, mid, finish)


def reduce_sibling_job(items):
    nw = len(items)

    def copies(ins, got, scr):
        x, y, c, _ = _place()
        return [_remote(_half(view(ins[w], p), rws, 1 - c), got[w].at[p], scr[0], scr[1], (w, p), (x, y, 1 - c))
                for w, (_, view, rws, _) in enumerate(items) for p in range(4)]

    def start(ins, got, scr):
        for cp in copies(ins, got, scr):
            cp.start()

    def finish(ins, got, scr):
        for cp in copies(ins, got, scr):
            cp.wait()

    return Job([it[0] for it in items], [sds((4, it[2] // 2, it[3]), BF16) for it in items], {},
               [pltpu.SemaphoreType.DMA((nw, 4)), pltpu.SemaphoreType.DMA((nw, 4))], start, lambda *_: None, finish)


def reduce_chips_job(qs):
    nw = len(qs)

    def copies(ins, got, scr):
        x, y, c, chips = _place()
        return [_remote(ins[w].at[2 * px + py], got[w].at[f], scr[0], scr[1], (w, f), (px, py, c))
                for w in range(nw) for f, (px, py) in enumerate(chips)]

    def start(ins, got, scr):
        for cp in copies(ins, got, scr):
            cp.start()

    def finish(ins, got, scr):
        for cp in copies(ins, got, scr):
            cp.wait()

    return Job(qs, [sds((3,) + q.shape[1:], BF16) for q in qs], {},
               [pltpu.SemaphoreType.DMA((nw, 3)), pltpu.SemaphoreType.DMA((nw, 3))], start, lambda *_: None, finish)


def share_halves_job(gs):
    nw = len(gs)

    def start(_, outs, scr):
        x, y, c, _ = _place()
        for w in range(nw):
            mine = _half(outs[w], gs[w].shape[0], c)
            _remote(mine, mine, scr[0], scr[1], w, (x, y, 1 - c)).start()

    def finish(_, outs, scr):
        x, y, c, _ = _place()
        for w in range(nw):
            mine = _half(outs[w], gs[w].shape[0], c)
            theirs = _half(outs[w], gs[w].shape[0], 1 - c)
            _remote(mine, mine, scr[0], scr[1], w, (x, y, 1 - c)).wait_send()
            _remote(theirs, theirs, scr[0], scr[1], w, (x, y, 1 - c)).wait_recv()

    return Job(gs, [sds(g.shape, F32) for g in gs], {w: w for w in range(nw)},
               [pltpu.SemaphoreType.DMA((nw,)), pltpu.SemaphoreType.DMA((nw,))], start, lambda *_: None, finish)


def rope_tables(t):
    pos = jnp.arange(t, dtype=F32)
    inv_freq = ROPE_THETA ** (-jnp.arange(0, ROT, 2, dtype=F32) / ROT)
    ang = pos[:, None] * inv_freq[None, :]
    cos, sin = jnp.cos(ang), jnp.sin(ang)
    d = jnp.arange(128) % HD
    half = ROT // 2
    cs = jnp.take(cos, d % half, axis=1)
    sn = jnp.take(sin, d % half, axis=1)
    cc = jnp.where(d[None] < ROT, cs, 1.0)
    sa = jnp.where(d[None] < half, -sn, 0.0)
    sb = jnp.where((d[None] >= half) & (d[None] < ROT), sn, 0.0)
    return cc, sa, sb


def _rope(v, cc, sa, sb):
    w = v.shape[1]
    reps = w // 128
    half = ROT // 2
    return (v * jnp.tile(cc, (1, reps)) + pltpu.roll(v, w - half, 1) * jnp.tile(sa, (1, reps))
            + pltpu.roll(v, half, 1) * jnp.tile(sb, (1, reps)))


def _rope_t(dv, cc, sa, sb):
    w = dv.shape[1]
    reps = w // 128
    half = ROT // 2
    return (dv * jnp.tile(cc, (1, reps)) + pltpu.roll(dv * jnp.tile(sa, (1, reps)), half, 1)
            + pltpu.roll(dv * jnp.tile(sb, (1, reps)), w - half, 1))


def pool_fwd(h, b_in, t, tm):
    tm = min(tm, t)
    per = tm // HALO

    def body(prev_ref, cur_ref, b_ref, o_ref, xx):
        i = pl.program_id(0)
        b = b_ref[...]
        xx[pl.ds(0, HALO), :] = jnp.where(i > 0, prev_ref[...] + b, 0.0)
        xx[pl.ds(HALO, tm), :] = cur_ref[...] + b
        tpos = i * tm + lax.broadcasted_iota(jnp.int32, (tm, PG), 0) + 1
        for gi, w in enumerate(POOL_WINDOWS):
            cols = pl.ds(gi * PG, PG)
            acc = xx[pl.ds(HALO, tm), cols]
            for s in range(1, w):
                acc = acc + xx[pl.ds(HALO - s, tm), cols]
            cnt = jnp.minimum(tpos, w).astype(F32)
            o_ref[:, cols] = (acc / cnt - xx[pl.ds(HALO, tm), cols]).astype(o_ref.dtype)

    return pl.pallas_call(
        body, name="pool_fwd", grid=(t // tm,),
        in_specs=[pl.BlockSpec((HALO, PW), lambda i: (jnp.maximum(i * per - 1, 0), 0)),
                  pl.BlockSpec((tm, PW), lambda i: (i, 0)), pl.BlockSpec((1, PW), lambda i: (0, 0))],
        out_specs=pl.BlockSpec((tm, PW), lambda i: (i, 0)), out_shape=sds((t, PW), BF16),
        scratch_shapes=[pltpu.VMEM((tm + HALO, PW), F32)], compiler_params=_params(("arbitrary",)),
    )(h, h, b_in)


def pool_bwd(dpooled, t, tm):
    tm = min(tm, t)
    per = tm // HALO
    nt = t // tm

    def body(cur_ref, nxt_ref, o_ref, db_ref, ee):
        i = pl.program_id(0)
        tpos = i * tm + lax.broadcasted_iota(jnp.int32, (tm, PG), 0) + 1
        for gi, w in enumerate(POOL_WINDOWS):
            cols = pl.ds(gi * PG, PG)
            ee[pl.ds(0, tm), cols] = cur_ref[:, cols] / jnp.minimum(tpos, w).astype(F32)
            ee[pl.ds(tm, HALO), cols] = jnp.where(i < nt - 1, nxt_ref[:, cols] / float(w), 0.0)
        for gi, w in enumerate(POOL_WINDOWS):
            cols = pl.ds(gi * PG, PG)
            acc = ee[pl.ds(0, tm), cols]
            for s in range(1, w):
                acc = acc + ee[pl.ds(s, tm), cols]
            dxp = acc - cur_ref[:, cols]
            o_ref[:, cols] = dxp.astype(o_ref.dtype)
            part = colsum(dxp)

            @pl.when(i == 0)
            def _(cols=cols, part=part):
                db_ref[:, cols] = part

            @pl.when(i > 0)
            def _(cols=cols, part=part):
                db_ref[:, cols] += part

    return pl.pallas_call(
        body, name="pool_bwd", grid=(nt,),
        in_specs=[pl.BlockSpec((tm, PW), lambda i: (i, 0)),
                  pl.BlockSpec((HALO, PW), lambda i: (jnp.minimum((i + 1) * per, t // HALO - 1), 0))],
        out_specs=[pl.BlockSpec((tm, PW), lambda i: (i, 0)), pl.BlockSpec((1, PW), lambda i: (0, 0))],
        out_shape=[sds((t, PW), BF16), sds((1, PW), F32)],
        scratch_shapes=[pltpu.VMEM((tm + HALO, PW), F32)], compiler_params=_params(("arbitrary",)),
    )(dpooled, dpooled)


def _scores(qh, kp, kc, mask_p, mask_c, sink):
    sp = jnp.where(mask_p, lax.dot_general(qh, kp, NT, preferred_element_type=F32), -1e30)
    sc = jnp.where(mask_c, lax.dot_general(qh, kc, NT, preferred_element_type=F32), -1e30)
    m = jnp.maximum(jnp.maximum(jnp.max(sp, axis=-1, keepdims=True), jnp.max(sc, axis=-1, keepdims=True)), sink)
    pp, pc = jnp.exp(sp - m), jnp.exp(sc - m)
    es = jnp.exp(sink - m)
    inv = 1.0 / (jnp.sum(pp, axis=-1, keepdims=True) + jnp.sum(pc, axis=-1, keepdims=True) + es)
    return pp * inv, pc * inv, es * inv


GRP = N_Q // N_KV


def _masks(n):
    qi = lax.broadcasted_iota(jnp.int32, (GRP * BLK, BLK), 0) % BLK
    kj = lax.broadcasted_iota(jnp.int32, (GRP * BLK, BLK), 1)
    return (kj > qi) & (n > 0), kj <= qi


def _head(hk, g):
    return pl.ds(HD * (GRP * hk + g), HD)


def _stack_heads(ref, hk):
    return jnp.concatenate([ref[:, _head(hk, g)] for g in range(GRP)], axis=0)


def _stack_sinks(s_ref, hk):
    return jnp.concatenate([jnp.full((BLK, 1), s_ref[0, GRP * hk + g], F32) for g in range(GRP)], axis=0)


def attn_fwd(q, k, v, sinks, t, job=None):
    def body(s_ref, q_ref, kp_ref, kc_ref, vp_ref, vc_ref, o_ref):
        n = pl.program_id(0)
        mask_p, mask_c = _masks(n)
        for hk in range(N_KV):
            kv = pl.ds(HD * hk, HD)
            pp, pc, _ = _scores(_stack_heads(q_ref, hk), kp_ref[:, kv], kc_ref[:, kv], mask_p, mask_c,
                                _stack_sinks(s_ref, hk))
            o = (lax.dot_general(pp.astype(BF16), vp_ref[:, kv], NN, preferred_element_type=F32)
                 + lax.dot_general(pc.astype(BF16), vc_ref[:, kv], NN, preferred_element_type=F32))
            for g in range(GRP):
                o_ref[:, _head(hk, g)] = o[g * BLK:(g + 1) * BLK].astype(o_ref.dtype)

    prev = lambda n: (jnp.maximum(n - 1, 0), 0)
    cur = lambda n: (n, 0)
    res, moved = carried_call(
        body, "attn_fwd", (t // BLK,),
        [pl.BlockSpec(memory_space=pltpu.SMEM), pl.BlockSpec((BLK, QW), cur),
         pl.BlockSpec((BLK, KVW), prev), pl.BlockSpec((BLK, KVW), cur),
         pl.BlockSpec((BLK, KVW), prev), pl.BlockSpec((BLK, KVW), cur)],
        [pl.BlockSpec((BLK, QW), cur)], [sds((t, QW), BF16)], [], [sinks, q, k, k, v, v], {}, job)
    return res[0], moved


def attn_bwd(q, k, v, do, sinks, t):
    nb = t // BLK

    def body(s_ref, q_ref, do_ref, kp_ref, kc_ref, vp_ref, vc_ref, dq_ref, dk_ref, dv_ref, ds_ref, dkc, dvc):
        n = pl.program_id(0)

        @pl.when(n == 0)
        def _():
            dkc[...] = jnp.zeros_like(dkc)
            dvc[...] = jnp.zeros_like(dvc)
            ds_ref[...] = jnp.zeros_like(ds_ref)

        @pl.when(n < nb)
        def _():
            mask_p, mask_c = _masks(n)
            lane = lax.broadcasted_iota(jnp.int32, (1, 128), 1)
            dsink = jnp.zeros((1, 128), F32)
            for hk in range(N_KV):
                kv = pl.ds(HD * hk, HD)
                kp, kc, vp, vc = kp_ref[:, kv], kc_ref[:, kv], vp_ref[:, kv], vc_ref[:, kv]
                qs, dos = _stack_heads(q_ref, hk), _stack_heads(do_ref, hk)
                pp, pc, ps = _scores(qs, kp, kc, mask_p, mask_c, _stack_sinks(s_ref, hk))
                dpp = lax.dot_general(dos, vp, NT, preferred_element_type=F32)
                dpc = lax.dot_general(dos, vc, NT, preferred_element_type=F32)
                delta = jnp.sum(pp * dpp, axis=-1, keepdims=True) + jnp.sum(pc * dpc, axis=-1, keepdims=True)
                dsp = (pp * (dpp - delta)).astype(BF16)
                dsc = (pc * (dpc - delta)).astype(BF16)
                sd = ps * delta
                dq = (lax.dot_general(dsp, kp, NN, preferred_element_type=F32)
                      + lax.dot_general(dsc, kc, NN, preferred_element_type=F32))
                for g in range(GRP):
                    rows = slice(g * BLK, (g + 1) * BLK)
                    dsink = dsink + jnp.where(lane == GRP * hk + g, -jnp.sum(sd[rows]), 0.0)
                    dq_ref[:, _head(hk, g)] = dq[rows]
                dk_ref[:, kv] = dkc[:, kv] + lax.dot_general(dsp, qs, TN, preferred_element_type=F32)
                dv_ref[:, kv] = dvc[:, kv] + lax.dot_general(pp.astype(BF16), dos, TN, preferred_element_type=F32)
                dkc[:, kv] = lax.dot_general(dsc, qs, TN, preferred_element_type=F32)
                dvc[:, kv] = lax.dot_general(pc.astype(BF16), dos, TN, preferred_element_type=F32)
            ds_ref[...] += dsink

        @pl.when(n == nb)
        def _():
            dk_ref[...] = dkc[...]
            dv_ref[...] = dvc[...]

    cur = lambda n: (jnp.minimum(n, nb - 1), 0)
    prev = lambda n: (jnp.clip(n - 1, 0, nb - 1), 0)
    return pl.pallas_call(
        body, name="attn_bwd", grid=(nb + 1,),
        in_specs=[pl.BlockSpec(memory_space=pltpu.SMEM), pl.BlockSpec((BLK, QW), cur), pl.BlockSpec((BLK, QW), cur),
                  pl.BlockSpec((BLK, KVW), prev), pl.BlockSpec((BLK, KVW), cur),
                  pl.BlockSpec((BLK, KVW), prev), pl.BlockSpec((BLK, KVW), cur)],
        out_specs=[pl.BlockSpec((BLK, QW), cur), pl.BlockSpec((BLK, KVW), prev), pl.BlockSpec((BLK, KVW), prev),
                   pl.BlockSpec((1, 128), lambda n: (0, 0))],
        out_shape=[sds((t, QW), F32), sds((t, KVW), F32), sds((t, KVW), F32), sds((1, 128), F32)],
        scratch_shapes=[pltpu.VMEM((BLK, KVW), F32), pltpu.VMEM((BLK, KVW), F32)],
        compiler_params=_params(("arbitrary",)),
    )(sinks, q, do, k, k, v, v)


def _adamw(w, g, m, v):
    m2 = B1 * m + (1.0 - B1) * g
    v2 = B2 * v + (1.0 - B2) * jnp.square(g)
    m_hat = m2 / (1.0 - B1 ** STEP)
    v_hat = v2 / (1.0 - B2 ** STEP)
    return -LR * (m_hat / (jnp.sqrt(v_hat) + EPS) + WD * w), m2, v2


def ada_fwd(c16, w_ada, b_sh):
    tn = 512

    def body(c_ref, w_ref, b_ref, o_ref):
        cv = c_ref[...]
        sc = (cv * _sigmoid(cv)).astype(BF16)
        o_ref[...] = lax.dot_general(sc, w_ref[...].astype(BF16), NN, preferred_element_type=F32) + b_ref[...]

    return pl.pallas_call(
        body, name="ada_fwd", grid=(ADA_SH // tn,),
        in_specs=[pl.BlockSpec((16, D), lambda j: (0, 0)), pl.BlockSpec((D, tn), lambda j: (0, j)),
                  pl.BlockSpec((1, tn), lambda j: (0, j))],
        out_specs=pl.BlockSpec((16, tn), lambda j: (0, j)), out_shape=sds((16, ADA_SH), F32),
        compiler_params=_params(("arbitrary",)),
    )(c16, w_ada, b_sh)


def ada_bwd_adam(c16, gm16, w, m, v, job):
    tm, tn = 256, 512

    def body(c_ref, g_ref, w_ref, m_ref, v_ref, go_ref, d_ref, mo_ref, vo_ref):
        cv = c_ref[...]
        sc = (cv * _sigmoid(cv)).astype(BF16)
        g = lax.dot_general(sc, g_ref[...].astype(BF16), TN, preferred_element_type=F32)
        dl, m2, v2 = _adamw(w_ref[...], g, m_ref[...], v_ref[...])
        go_ref[...] = g
        d_ref[...] = dl
        mo_ref[...] = m2
        vo_ref[...] = v2

    blk = pl.BlockSpec((tm, tn), lambda i, j: (i, j))
    return carried_call(
        body, "ada_bwd_adam", (D // tm, ADA_SH // tn),
        [pl.BlockSpec((16, tm), lambda i, j: (0, i)), pl.BlockSpec((16, tn), lambda i, j: (0, j)), blk, blk, blk],
        [blk] * 4, [sds((D, ADA_SH), F32)] * 4, [], [c16, gm16, w, m, v], {}, job)


def adam_rows(name, w, g, m, v, tm):
    rows, cols = w.shape

    def fn(wv, gv, mv, vv):
        gv = gv[:, :cols]
        dl, m2, v2 = _adamw(wv, gv, mv, vv)
        return gv, dl, m2, v2

    return rowmap(name, fn, [T_(w), T_(g), T_(m), T_(v)], [(cols, F32)] * 4, rows=rows, tm=tm)


def adam_small(name, w, g, m, v):
    def body(w_ref, g_ref, m_ref, v_ref, d_ref, mo_ref, vo_ref):
        dl, m2, v2 = _adamw(w_ref[...], g_ref[...], m_ref[...], v_ref[...])
        d_ref[...] = dl
        mo_ref[...] = m2
        vo_ref[...] = v2

    return pl.pallas_call(body, name=name, out_shape=[sds(w.shape, F32)] * 3)(w, g, m, v)


def sum_devices(allv):
    def body(a_ref, o_ref):
        acc = a_ref[0]
        for d in range(1, 8):
            acc = acc + a_ref[d]
        o_ref[...] = acc

    return pl.pallas_call(body, name="sum_devices", out_shape=sds(allv.shape[1:], F32))(allv)


def _ln_fwd(z, g, b):
    mu = jnp.mean(z, axis=-1, keepdims=True)
    zc = z - mu
    var = jnp.mean(jnp.square(zc), axis=-1, keepdims=True)
    return zc * lax.rsqrt(var + LN_EPS) * g + b


def _ln_bwd(z, g, dout):
    mu = jnp.mean(z, axis=-1, keepdims=True)
    zc = z - mu
    var = jnp.mean(jnp.square(zc), axis=-1, keepdims=True)
    rstd = lax.rsqrt(var + LN_EPS)
    xh = zc * rstd
    dxh = dout * g
    dz = rstd * (dxh - jnp.mean(dxh, axis=-1, keepdims=True) - xh * jnp.mean(dxh * xh, axis=-1, keepdims=True))
    return dz, colsum(dout * xh), colsum(dout)


def modulate(name, xin, shift, scale, t):
    return rowmap(name, lambda xv, sh, sc: xv * (1.0 + sc) + sh, [T_(xin), B_(shift), B_(scale)], [(D, BF16)],
                  rows=t, tm=512)


def residual_ln_mod(name, xin, y, gate, lg, lb, wgt, shift_n, scale_n, t):
    def fn(xv, yv, gt, g, b, sh, sc):
        z = ALPHA * xv + (wgt * (1.0 + gt)) * yv
        xo = _ln_fwd(z, g, b)
        return xo, z, xo * (1.0 + sc) + sh

    return rowmap(name, fn, [T_(xin), T_(y), B_(gate), B_(lg), B_(lb), B_(shift_n), B_(scale_n)],
                  [(D, F32), (D, F32), (D, BF16)], rows=t, tm=256)


def residual_ln_bwd(name, z, dnext, y, gate, lg, wgt, t):
    dzn, dun, xn, scn = dnext

    def fn(zv, yv, gt, g, dzv, duv, xv, sc):
        dv = ALPHA * dzv + duv * (1.0 + sc)
        dz, dg, db = _ln_bwd(zv, g, dv)
        return dz, (wgt * (1.0 + gt)) * dz, dg, db, colsum(wgt * dz * yv), colsum(duv), colsum(duv * xv)

    return rowmap(name, fn, [T_(z), T_(y), B_(gate), B_(lg), T_(dzn), T_(dun), T_(xn), B_(scn)],
                  [(D, F32), (D, BF16)], [(1, D)] * 5, rows=t, tm=256)


def residual_ln_loss_bwd(name, xin, y, tgt, gate, lg, lb, wgt, t):
    def fn(xv, yv, tv, gt, g, b):
        z = ALPHA * xv + (wgt * (1.0 + gt)) * yv
        d = _ln_fwd(z, g, b) - tv
        dz, dg, db = _ln_bwd(z, g, d * (1.0 / D))
        return dz, (wgt * (1.0 + gt)) * dz, dg, db, colsum(wgt * dz * yv), jnp.sum(d * d).reshape(1, 1)

    dz, dy, dlg, dlb, dgate, sq = rowmap(
        name, fn, [T_(xin), T_(y), T_(tgt), B_(gate), B_(lg), B_(lb)], [(D, F32), (D, BF16)],
        [(1, D), (1, D), (1, D), (1, 1)], rows=t, tm=256)
    return dz, dy, dlg, dlb, dgate, sq


def modulate_bwd(name, dz, du, xin, scale, t):
    def fn(dzv, duv, xv, sc):
        return ALPHA * dzv + duv * (1.0 + sc), colsum(duv), colsum(duv * xv)

    return rowmap(name, fn, [T_(dz), T_(du), T_(xin), B_(scale)], [(D, F32)], [(1, D), (1, D)], rows=t, tm=256)


def ffn_fwd(tag, u, wi, t, up_job, down_job=None):
    tm = min(1024, t)
    tn = 256
    per = FHP // tn

    def act(accs, _):
        a, b = accs
        s = _sigmoid(a)
        sl = a * s
        return b * (s * (1.0 + a * (1.0 - s))), sl, sl * b

    hblk = pl.BlockSpec((tm, tn), lambda i, j, k: (i, j))
    (ha, hb, g), up_moved = mm(
        tag + "_up", [u], [wi, wi], [(0, 0, 0), (0, 1, 1)], dims=NT, grid=(t // tm, 2 * per, 1),
        a_specs=[pl.BlockSpec((tm, D), lambda i, j, k: (i, 0))],
        b_specs=[pl.BlockSpec((None, tn, D), lambda i, j, k: (j // per, j % per, 0)),
                 pl.BlockSpec((None, tn, D), lambda i, j, k: (2 + j // per, j % per, 0))],
        outs=[sds((t, 2 * FHP), BF16)] * 3, out_specs=[hblk] * 3, acc_shapes=[(tm, tn)] * 2, epilogue=act, job=up_job,
        sub_rows=tm // 2)
    wo = up_moved[0].reshape(2 * FHP, D)
    tk = FHP
    y, down_moved = _with_moved(mm(
        tag + "_down", [g], [wo], [(0, 0, 0)], dims=NN, grid=(t // tm, 2, 2),
        a_specs=[pl.BlockSpec((tm, tk), lambda i, j, k: (i, k))],
        b_specs=[pl.BlockSpec((tk, D // 2), lambda i, j, k: (k, j))],
        outs=[sds((t, D), F32)], out_specs=[pl.BlockSpec((tm, D // 2), lambda i, j, k: (i, j))],
        acc_shapes=[(tm, D // 2)], job=down_job), down_job)
    return ha, hb, g, y, wo, up_moved, down_moved


def ffn_bwd(tag, u, ha, hb, g, dy, wi, wo, t, sp, dact_job=None, dwo_job=None):
    tm = min(1024, t)

    def dact(accs, ex):
        dg = accs[0]
        return dg * ex[0].astype(F32), dg * ex[1].astype(F32)

    tn = 256
    hblk = pl.BlockSpec((tm, tn), lambda i, j, k: (i, j))
    (dha, dhb), dact_moved = _with_moved(mm(
        tag + "_dact", [dy], [wo], [(0, 0, 0)], dims=NT, grid=(t // tm, 2 * FHP // tn, 1),
        a_specs=[pl.BlockSpec((tm, D), lambda i, j, k: (i, 0))],
        b_specs=[pl.BlockSpec((tn, D), lambda i, j, k: (j, 0))],
        outs=[sds((t, 2 * FHP), BF16)] * 2, out_specs=[hblk] * 2, acc_shapes=[(tm, tn)],
        epilogue=dact, extras=[ha, hb], extra_specs=[hblk] * 2, job=dact_job, sub_rows=tm // 2), dact_job)
    tk = min(2048, t)
    th = FHP // 2
    dwo, dwo_moved = _with_moved(mm(
        tag + "_dwo", [g], [dy], [(0, 0, 0)], dims=TN, grid=(4, 2, t // tk),
        a_specs=[pl.BlockSpec((tk, th), lambda i, j, k: (k, i))],
        b_specs=[pl.BlockSpec((tk, D // 2), lambda i, j, k: (k, j))],
        outs=[sds((2 * FHP, D), BF16)], out_specs=[pl.BlockSpec((th, D // 2), lambda i, j, k: (i, j))],
        acc_shapes=[(th, D // 2)], job=dwo_job), dwo_job)
    dwo = dwo.reshape(2, FHP, D)

    def dwi_part(part, dh, carry, job):
        return mm(
            f"{tag}_dwi{part}", [dh], [u], [(0, 0, 0)], dims=TN, grid=(4, 2, t // tk),
            a_specs=[pl.BlockSpec((tk, th), lambda i, j, k: (k, i))],
            b_specs=[pl.BlockSpec((tk, D // 2), lambda i, j, k: (k, j))],
            outs=[sds((4, FHP, D), BF16)],
            out_specs=[pl.BlockSpec((None, th, D // 2), lambda i, j, k: (2 * part + i // 2, i % 2, j))],
            acc_shapes=[(th, D // 2)], carry=carry, job=job)

    dwi, (sib_fo,) = dwi_part(0, dha, None, reduce_sibling_job([(dwo, view_ffn_out, FO, D)]))
    q_fo = chip_sum(tag + "_chipsum_fo", dwo, sib_fo, sp, FO, FO // 2, ffn_out=True)
    dwi, (far_fo,) = dwi_part(1, dhb, dwi, reduce_chips_job([q_fo]))
    (sib_fi,) = run_job(tag + "_sibling_fi", reduce_sibling_job([(dwi, view_lead, FHP, D)]))
    q_fi = chip_sum(tag + "_chipsum_fi", dwi, sib_fi, sp, FHP, FHP // 8)
    tmd = min(512, t)
    du, (far_fi,) = mm(
        tag + "_du", [dha, dhb], [wi, wi], [(0, 0, 0), (1, 1, 0)], dims=NN, grid=(t // tmd, 2, 2),
        a_specs=[pl.BlockSpec((tmd, FHP), lambda i, j, k: (i, k))] * 2,
        b_specs=[pl.BlockSpec((None, FHP, D // 2), lambda i, j, k: (k, 0, j)),
                 pl.BlockSpec((None, FHP, D // 2), lambda i, j, k: (2 + k, 0, j))],
        outs=[sds((t, D), F32)], out_specs=[pl.BlockSpec((tmd, D // 2), lambda i, j, k: (i, j))],
        acc_shapes=[(tmd, D // 2)], job=reduce_chips_job([q_fi]))
    return du, (q_fi, far_fi), (q_fo, far_fo), dact_moved, dwo_moved


def mix_fwd(u, wts, b_in, pool_scale, sinks, tabs, t, in_job, attn_job):
    w_in, wp, wba, wbb, wo = wts
    tm = min(1024, t)
    tmh = min(512, t)
    h, in_moved = mm("mix_in", [u], [w_in], [(0, 0, 0)], dims=NN, grid=(t // tmh, 4, 1),
                     a_specs=[pl.BlockSpec((tmh, D), lambda i, j, k: (i, 0))],
                     b_specs=[pl.BlockSpec((None, D, IN_SH), lambda i, j, k: (j, 0, 0))],
                     outs=[sds((t, IN_W), F32)], out_specs=[pl.BlockSpec((tmh, IN_SH), lambda i, j, k: (i, j))],
                     acc_shapes=[(tmh, IN_SH)], job=in_job)
    attn_job = attn_job(in_moved)
    pooled = pool_fwd(h, b_in, t, 512)
    gblk = pl.BlockSpec((tm, PG), lambda i, j, k: (i, j))
    mixed = mm("mix_pool", [pooled], [wp], [(0, 0, 0)], dims=NN, grid=(t // tm, 4, 1), a_specs=[gblk],
               b_specs=[pl.BlockSpec((None, PG, PG), lambda i, j, k: (j, 0, 0))],
               outs=[sds((t, PW), F32)], out_specs=[gblk], acc_shapes=[(tm, PG)])
    pm = rowmap("mix_pscale", lambda mv, ps: mv * ps, [T_(mixed), B_(pool_scale)], [(PW, BF16)], rows=t, tm=512)

    def branch(name, a, w):
        return mm(name, [a], [w], [(0, 0, 0)], dims=NN, grid=(t // tm, 4, 1),
                  a_specs=[pl.BlockSpec((tm, PW), lambda i, j, k: (i, 0))],
                  b_specs=[pl.BlockSpec((None, PW, D // 4), lambda i, j, k: (j, 0, 0))],
                  outs=[sds((t, D), F32)], out_specs=[pl.BlockSpec((tm, D // 4), lambda i, j, k: (i, j))],
                  acc_shapes=[(tm, D // 4)])

    ya = branch("mix_branch_a", pm, wba)

    def qkv(hq, hk, hv, bq, bk, bv, cc, sa, sb):
        return (_rope(hq + bq, cc, sa, sb) * (HD ** -0.5), _rope(hk + bk, cc, sa, sb), hv + bv)

    qr, kr, vv = rowmap(
        "mix_rope", qkv,
        [T_(h, QW, 1), T_(h, KVW, 8), T_(h, KVW, 9), B_(b_in, QW, 1), B_(b_in, KVW, 8), B_(b_in, KVW, 9),
         T_(tabs[0]), T_(tabs[1]), T_(tabs[2])],
        [(QW, BF16), (KVW, BF16), (KVW, BF16)], rows=t, tm=512)
    attn, attn_moved = attn_fwd(qr, kr, vv, sinks, t, attn_job)
    yb = branch("mix_branch_b", attn, wbb)
    cw = 512

    def merge(ga, gb, ba, bb, yav, ybv):
        return _sigmoid(ga + ba) * yav + _sigmoid(gb + bb) * ybv

    merged = rowmap(
        "mix_merge", merge,
        [T_(h, cw, 5), T_(h, cw, 9), B_(b_in, cw, 5), B_(b_in, cw, 9), T_(ya, cw), T_(yb, cw)],
        [(D, BF16)], rows=t, tm=512, ncol=D // cw)
    y = mm("mix_out", [merged], [wo], [(0, 0, 0)], dims=NN, grid=(t // tm, 2, 1),
           a_specs=[pl.BlockSpec((tm, D), lambda i, j, k: (i, 0))],
           b_specs=[pl.BlockSpec((D, D // 2), lambda i, j, k: (0, j))],
           outs=[sds((t, D), F32)], out_specs=[pl.BlockSpec((tm, D // 2), lambda i, j, k: (i, j))],
           acc_shapes=[(tm, D // 2)])
    return y, (h, pooled, mixed, pm, ya, qr, kr, vv, attn, yb, merged), attn_moved


def mix_bwd(u, saved, dy, wts, b_in, pool_scale, sinks, tabs, t):
    h, pooled, mixed, pm, ya, qr, kr, vv, attn, yb, merged = saved
    w_in, wp, wba, wbb, wo = wts
    tm = min(1024, t)
    tk = min(2048, t)
    dmerged = mm("mix_dmerged", [dy], [wo], [(0, 0, 0)], dims=NT, grid=(t // tm, 2, 1),
                 a_specs=[pl.BlockSpec((tm, D), lambda i, j, k: (i, 0))],
                 b_specs=[pl.BlockSpec((D // 2, D), lambda i, j, k: (j, 0))],
                 outs=[sds((t, D), F32)], out_specs=[pl.BlockSpec((tm, D // 2), lambda i, j, k: (i, j))],
                 acc_shapes=[(tm, D // 2)])
    half = pl.BlockSpec((tk, D // 2), lambda i, j, k: (k, i))
    dwo = mm("mix_dwo", [merged], [dy], [(0, 0, 0)], dims=TN, grid=(2, 2, t // tk), a_specs=[half],
             b_specs=[pl.BlockSpec((tk, D // 2), lambda i, j, k: (k, j))],
             outs=[sds((D, D), BF16)], out_specs=[pl.BlockSpec((D // 2, D // 2), lambda i, j, k: (i, j))],
             acc_shapes=[(D // 2, D // 2)])
    cw = 512

    def dmerge(dm, ga, gb, ba, bb, yav, ybv):
        sa_, sb_ = _sigmoid(ga + ba), _sigmoid(gb + bb)
        dga = dm * yav * sa_ * (1.0 - sa_)
        dgb = dm * ybv * sb_ * (1.0 - sb_)
        return dm * sa_, dm * sb_, dga, dgb, colsum(dga), colsum(dgb)

    dya, dyb, dgla, dglb, dbga, dbgb = rowmap(
        "mix_dmerge", dmerge,
        [T_(dmerged, cw), T_(h, cw, 5), T_(h, cw, 9), B_(b_in, cw, 5), B_(b_in, cw, 9), T_(ya, cw), T_(yb, cw)],
        [(D, BF16)] * 4, [(1, D), (1, D)], rows=t, tm=512, ncol=D // cw)

    def dbranch(name, dyv, act, w):
        dwb = mm(name + "_dw", [act], [dyv], [(0, 0, 0)], dims=TN, grid=(1, 4, t // tk),
                 a_specs=[pl.BlockSpec((tk, PW), lambda i, j, k: (k, 0))],
                 b_specs=[pl.BlockSpec((tk, D // 4), lambda i, j, k: (k, j))],
                 outs=[sds((4, PW, D // 4), BF16)], out_specs=[pl.BlockSpec((None, PW, D // 4), lambda i, j, k: (j, 0, 0))],
                 acc_shapes=[(PW, D // 4)])
        return dwb, lambda dt: mm(
            name + "_dx", [dyv], [w], [(0, 0, 0)], dims=NT, grid=(t // tm, 1, 4),
            a_specs=[pl.BlockSpec((tm, D // 4), lambda i, j, k: (i, k))],
            b_specs=[pl.BlockSpec((None, PW, D // 4), lambda i, j, k: (k, 0, 0))],
            outs=[sds((t, PW), dt)], out_specs=[pl.BlockSpec((tm, PW), lambda i, j, k: (i, 0))], acc_shapes=[(tm, PW)])

    dwba, dpm_fn = dbranch("mix_dbranch_a", dya, pm, wba)
    dwbb, dattn_fn = dbranch("mix_dbranch_b", dyb, attn, wbb)
    dpm, dattn = dpm_fn(F32), dattn_fn(BF16)
    dmixed, dps = rowmap("mix_dpscale", lambda dp, mv, ps: (dp * ps, colsum(dp * mv)),
                         [T_(dpm), T_(mixed), B_(pool_scale)], [(PW, BF16)], [(1, PW)], rows=t, tm=512)
    gblk = pl.BlockSpec((tm, PG), lambda i, j, k: (i, j))
    dpooled = mm("mix_dpool", [dmixed], [wp], [(0, 0, 0)], dims=NT, grid=(t // tm, 4, 1), a_specs=[gblk],
                 b_specs=[pl.BlockSpec((None, PG, PG), lambda i, j, k: (j, 0, 0))],
                 outs=[sds((t, PW), F32)], out_specs=[gblk], acc_shapes=[(tm, PG)])
    kblk = pl.BlockSpec((tk, PG), lambda i, j, k: (k, i))
    dwp = mm("mix_dwpool", [pooled], [dmixed], [(0, 0, 0)], dims=TN, grid=(4, 1, t // tk), a_specs=[kblk], b_specs=[kblk],
             outs=[sds((4, PG, PG), BF16)], out_specs=[pl.BlockSpec((None, PG, PG), lambda i, j, k: (i, 0, 0))],
             acc_shapes=[(PG, PG)])
    dxp, dbxp = pool_bwd(dpooled, t, 512)
    dqr, dkr, dvv, dsinks = attn_bwd(qr, kr, vv, dattn, sinks, t)

    def dqkv(dq, dk, dv, cc, sa, sb):
        dq = _rope_t(dq, cc, sa, sb) * (HD ** -0.5)
        dk = _rope_t(dk, cc, sa, sb)
        return dq, dk, dv, colsum(dq), colsum(dk), colsum(dv)

    dq, dk, dvb, dbq, dbk, dbv = rowmap(
        "mix_rope_bwd", dqkv, [T_(dqr), T_(dkr), T_(dvv), T_(tabs[0]), T_(tabs[1]), T_(tabs[2])],
        [(QW, BF16), (KVW, BF16), (KVW, BF16)], [(1, QW), (1, KVW), (1, KVW)], rows=t, tm=512)
    dh = jnp.concatenate([dxp, dq, dk, dvb, dgla, dglb], axis=1)
    db_in = jnp.concatenate([dbxp, dbq, dbk, dbv, dbga, dbgb], axis=1)
    dwin = mm("mix_dwin", [u], [dh], [(0, 0, 0)], dims=TN, grid=(2, 4, t // tk), a_specs=[half],
              b_specs=[pl.BlockSpec((tk, IN_SH), lambda i, j, k: (k, j))],
              outs=[sds((4, D, IN_SH), BF16)], out_specs=[pl.BlockSpec((None, D // 2, IN_SH), lambda i, j, k: (j, i, 0))],
              acc_shapes=[(D // 2, IN_SH)])
    dwp_sh = jnp.transpose(dwp.reshape(4, 4, 64, PG), (1, 0, 2, 3)).reshape(4, 4 * 64, PG)
    parts = {"win": dwin, "wp": dwp_sh, "wba": dwba, "wbb": dwbb, "wo": dwo.reshape(4, D // 4, D)}
    du, sib = mm("mix_du", [dh], [w_in], [(0, 0, 0)], dims=NT, grid=(t // tm, 2, 4),
                 a_specs=[pl.BlockSpec((tm, IN_SH), lambda i, j, k: (i, k))],
                 b_specs=[pl.BlockSpec((None, D // 2, IN_SH), lambda i, j, k: (k, j, 0))],
                 outs=[sds((t, D), F32)], out_specs=[pl.BlockSpec((tm, D // 2), lambda i, j, k: (i, j))],
                 acc_shapes=[(tm, D // 2)],
                 job=reduce_sibling_job([(p, view_lead, p.shape[1], p.shape[2]) for p in parts.values()]))
    return du, parts, dict(zip(parts, sib)), db_in, dps, dsinks


def cast_shard(name, w, sp, ffn_out=False):
    rows, cols = w.shape
    if ffn_out:
        tm = rows // 2
        shape = (2, FHP, D)
        spec = pl.BlockSpec((None, tm, cols), lambda j, i, s: (s[0] // 2, (s[0] % 2) * 2 + i, 0))
    else:
        tm = rows // 4
        shape = (4, rows, cols)
        spec = pl.BlockSpec((None, tm, cols), lambda j, i, s: (s[0], i, 0))
    return rowmap(name, lambda wv: wv, [T_(w)], [(shape, BF16, spec)], rows=rows, tm=tm, sp=sp)


def cast_ffn_in(name, wt, sp):
    tm = 64
    full = FH // tm

    def fn(_, i, wv):
        return jnp.where(i < full, wv, 0.0)

    return rowmap(name, fn, [X_(wt, pl.BlockSpec((tm, D), lambda j, i, s: (jnp.minimum(i, full - 1), 0)))],
                  [((4, FHP, D), BF16, pl.BlockSpec((None, tm, D), lambda j, i, s: (s[0], i, 0)))],
                  rows=FHP, tm=tm, sp=sp, with_ids=True)


def chip_sum(name, dw, got, sp, rows, tm, ffn_out=False):
    hr, cols = rows // 2, got.shape[2]
    per = hr // tm
    pos = pl.BlockSpec((None, tm, cols), lambda j, i, s: (i // per, i % per, 0))
    if ffn_out:
        mine = pl.BlockSpec((None, tm, cols), lambda j, i, s: (i // 2, (i % 2) * 2 + s[1], 0))
    else:
        mine = pl.BlockSpec((None, tm, cols), lambda j, i, s: (i // per, s[1] * per + i % per, 0))
    return rowmap(name, lambda av, bv: av.astype(F32) + bv.astype(F32), [X_(dw, mine), X_(got, pos)],
                  [(got.shape, BF16, pos)], rows=4 * hr, tm=tm, sp=sp)


def chip_total(name, q, got, sp, rows, tm):
    hr, cols = rows // 2, q.shape[2]
    per = hr // tm

    def part(f):
        return X_(got, pl.BlockSpec((None, tm, cols), lambda j, i, s, f=f: (f, i, 0)))

    return rowmap(
        name, lambda av, b0, b1, b2: ((av.astype(F32) + b0.astype(F32)) + b1.astype(F32)) + b2.astype(F32),
        [X_(q, pl.BlockSpec((None, tm, cols), lambda j, i, s: (s[0], i, 0))), part(0), part(1), part(2)],
        [((rows, cols), F32, pl.BlockSpec((tm, cols), lambda j, i, s: (s[1] * per + i, 0)))], rows=hr, tm=tm, sp=sp)


def kernel(x, c, w_ada, b_ada, ln_g, ln_b, w_ffn1_in, w_ffn1_out, w_in, b_in, w_pool, pool_scale, sinks, w_branch_a, w_branch_b, w_out, w_ffn2_in, w_ffn2_out, loss_target, m_w_ada, m_b_ada, m_ln_g, m_ln_b, m_w_ffn1_in, m_w_ffn1_out, m_w_in, m_b_in, m_w_pool, m_pool_scale, m_sinks, m_w_branch_a, m_w_branch_b, m_w_out, m_w_ffn2_in, m_w_ffn2_out, v_w_ada, v_b_ada, v_ln_g, v_ln_b, v_w_ffn1_in, v_w_ffn1_out, v_w_in, v_b_in, v_w_pool, v_pool_scale, v_sinks, v_w_branch_a, v_w_branch_b, v_w_out, v_w_ffn2_in, v_w_ffn2_out):
    t = x.shape[1]
    xs, tgt = x[0], loss_target[0]
    xi, yi, ci = lax.axis_index("x"), lax.axis_index("y"), lax.axis_index("c")
    chip = 2 * xi + yi
    dev = 2 * chip + ci
    b_in2, ps2, sinks2 = b_in, pool_scale, sinks

    first = jnp.concatenate([c.reshape(-1), ln_g.reshape(-1), ln_b.reshape(-1)]).reshape(-1, 128)
    first_all = allgather_small("gather_cond", first).reshape(8, -1)
    c_all = first_all[:, :D]
    ln_parts = first_all[0::2, D:].reshape(4, 2, 3, D // 4)
    ln_full = jnp.transpose(ln_parts, (1, 2, 0, 3)).reshape(2, 3, D)
    lgs = [ln_full[0, s:s + 1] for s in range(3)]
    lbs = [ln_full[1, s:s + 1] for s in range(3)]
    c16 = jnp.pad(c_all, ((0, 8), (0, 0)))
    b_ada_sh = lax.dynamic_slice(b_ada, (0, chip * ADA_SH), (1, ADA_SH))
    mod_part = ada_fwd(c16, w_ada[0], b_ada_sh)[:8]
    mod_all = allgather_small("gather_mod", mod_part.reshape(-1, 128)).reshape(8, 8, ADA_SH)
    mod_mine = lax.dynamic_index_in_dim(mod_all[0::2], dev, axis=1, keepdims=False).reshape(9, D)
    mods = [[mod_mine[3 * s + k:3 * s + k + 1] for k in range(3)] for s in range(3)]

    plain = [("f1o", w_ffn1_out[0]), ("win", w_in[0]), ("wp", w_pool[0].reshape(4 * 64, PG)), ("wba", w_branch_a[0]),
             ("wbb", w_branch_b[0]), ("wo", w_out[0]), ("f2o", w_ffn2_out[0])]
    sp = jnp.stack([chip, ci]).astype(jnp.int32)
    sh = {n: cast_shard("cast_" + n, w, sp, ffn_out=n in ("f1o", "f2o")) for n, w in plain}
    tr = lambda a: jnp.swapaxes(a[0], 0, 1)
    sh["f1i"] = cast_ffn_in("cast_f1i", tr(w_ffn1_in), sp)
    sh["f2i"] = cast_ffn_in("cast_f2i", tr(w_ffn2_in), sp)
    order = ["f1i", "f1o", "win", "wp", "wba", "wbb", "wo", "f2i", "f2o"]
    views = {n: (view_ffn_out if n in ("f1o", "f2o") else view_lead) for n in order}
    shard_rows = {n: (FO if n in ("f1o", "f2o") else sh[n].shape[1]) for n in order}
    shard_cols = {n: sh[n].shape[2] for n in order}
    tiles = {"f1i": FHP // 8, "f1o": FO // 2, "win": 512, "wp": 128, "wba": 512, "wbb": 512, "wo": 256,
             "f2i": FHP // 8, "f2o": FO // 2}

    def item(n, part=0, parts=1):
        return (sh[n], views[n], shard_rows[n], part, parts)

    tabs = rope_tables(t)
    (sh0, sc0, gt0), (sh1, sc1, gt1), (sh2, sc2, gt2) = mods

    (g_f1i,) = run_job("gather_f1i", gather_job([item("f1i")]))
    u0 = modulate("ffn1_mod", xs, sh0, sc0, t)
    ha1, hb1, g1, y1, f1o, up1, (g_win,) = ffn_fwd(
        "ffn1", u0, g_f1i, t, gather_job([item(n) for n in ("f1o", "wp", "wba", "wbb", "wo")]),
        gather_job([item("win")]))
    x1, z1, u1 = residual_ln_mod("ffn1_ln", xs, y1, gt0, lgs[0], lbs[0], 0.5, sh1, sc1, t)
    _, g_wp, g_wba, g_wbb, g_wo = up1
    wp_full = jnp.transpose(g_wp.reshape(4, 4, 64, PG), (1, 0, 2, 3)).reshape(4, PG, PG)
    wts = (g_win, wp_full, g_wba, g_wbb, g_wo.reshape(D, D))
    y2, sv2, (g_f2i,) = mix_fwd(
        u1, wts, b_in2, ps2, sinks2, tabs, t, gather_job([item("f2i", 0, 2)]),
        lambda moved: gather_job([(moved[0], view_lead, FHP, 1, 2)]))
    x2, z2, u2 = residual_ln_mod("mix_ln", x1, y2, gt1, lgs[1], lbs[1], 1.0, sh2, sc2, t)
    ha3, hb3, g3, y3, f2o, _, _ = ffn_fwd("ffn2", u2, g_f2i, t, gather_job([item("f2o")]))

    dz3, dy3, dlg2, dlb2, dgt2, sq = residual_ln_loss_bwd("ffn2_ln_loss", x2, y3, tgt, gt2, lgs[2], lbs[2], 0.5, t)
    loss = lax.psum(0.5 * sq[0, 0] / D, ("x", "y", "c"))
    du3, red_f2i, red_f2o, _, _ = ffn_bwd("ffn2", u2, ha3, hb3, g3, dy3, g_f2i, f2o, t, sp)
    dz2, dy2, dlg1, dlb1, dgt1, dsh2, dsc2 = residual_ln_bwd("mix_ln_bwd", z2, (dz3, du3, x2, sc2), y2, gt1, lgs[1], 1.0, t)
    du2, mix_parts, sib, db_in, dps, dsinks = mix_bwd(u1, sv2, dy2, wts, b_in2, ps2, sinks2, tabs, t)
    q = {n: chip_sum("chipsum_" + n, mix_parts[n], sib[n], sp, shard_rows[n], tiles[n]) for n in mix_parts}
    dz1, dy1, dlg0, dlb0, dgt0, dsh1, dsc1 = residual_ln_bwd("ffn1_ln_bwd", z1, (dz2, du2, x1, sc1), y1, gt0, lgs[0], 0.5, t)
    du1, red_f1i, red_f1o, far_a, far_b = ffn_bwd(
        "ffn1", u0, ha1, hb1, g1, dy1, g_f1i, f1o, t, sp,
        reduce_chips_job([q["win"], q["wp"]]), reduce_chips_job([q["wo"], q["wba"], q["wbb"]]))
    dx0, dsh0, dsc0 = modulate_bwd("ffn1_mod_bwd", dz1, du1, xs, sc0, t)
    gm0, gm1, gm2 = (dsh0, dsc0, dgt0), (dsh1, dsc1, dgt1), (dsh2, dsc2, dgt2)
    reduced = {"f1i": red_f1i, "f1o": red_f1o, "f2i": red_f2i, "f2o": red_f2o, "win": (q["win"], far_a[0]),
               "wp": (q["wp"], far_a[1]), "wo": (q["wo"], far_b[0]), "wba": (q["wba"], far_b[1]), "wbb": (q["wbb"], far_b[2])}
    halves = [chip_total("total_" + n, *reduced[n], sp, shard_rows[n], tiles[n]) for n in order]

    small = jnp.concatenate([*gm0, *gm1, *gm2, dlg0, dlg1, dlg2, dlb0, dlb1, dlb2, db_in, dps, dsinks], axis=1)
    n_small = small.shape[1]
    rows_small = -(-n_small // 1024) * 8
    small = jnp.pad(small, ((0, 0), (0, rows_small * 128 - n_small))).reshape(rows_small, 128)
    small_all = allgather_small("gather_small", small)
    tot = sum_devices(small_all).reshape(1, -1)
    gmod_all = small_all.reshape(8, -1)[:, :9 * D]
    o = 9 * D
    g_b_ada = tot[:, :o]
    g_ln_g = lax.dynamic_slice(tot[:, o:o + 3 * D].reshape(3, D), (0, chip * (D // 4)), (3, D // 4))
    g_ln_b = lax.dynamic_slice(tot[:, o + 3 * D:o + 6 * D].reshape(3, D), (0, chip * (D // 4)), (3, D // 4))
    o += 6 * D
    g_b_in, g_ps, g_sinks = tot[:, o:o + IN_W], tot[:, o + IN_W:o + IN_W + PW], tot[:, o + IN_W + PW:o + IN_W + PW + N_Q]

    gm16 = jnp.pad(lax.dynamic_slice(gmod_all, (0, chip * ADA_SH), (8, ADA_SH)), ((0, 8), (0, 0)))
    (g_w_ada, d_w_ada, nm_w_ada, nv_w_ada), shared = ada_bwd_adam(
        c16, gm16, w_ada[0], m_w_ada[0], v_w_ada[0], share_halves_job(halves))
    gw = dict(zip(order, shared))

    def big(n, w, m, v, tm):
        shape = w.shape
        w2, m2, v2 = (a.reshape(shape[-2] if a.ndim == 3 else -1, shape[-1]) for a in (w, m, v))
        return [r.reshape(shape) for r in adam_rows("adam_" + n, w2, gw[n], m2, v2, tm)]

    def big_t(n, w, m, v):
        return [jnp.swapaxes(r, 0, 1)[None] for r in adam_rows("adam_" + n, tr(w), gw[n], tr(m), tr(v), 64)]

    def tiny(n, w, g, m, v):
        return [g.reshape(w.shape)] + list(adam_small("adam_" + n, w, g.reshape(w.shape), m, v))

    res = {
        "w_ada": [a[None] for a in (g_w_ada, d_w_ada, nm_w_ada, nv_w_ada)],
        "b_ada": tiny("b_ada", b_ada, g_b_ada, m_b_ada, v_b_ada),
        "ln_g": tiny("ln_g", ln_g, g_ln_g, m_ln_g, v_ln_g),
        "ln_b": tiny("ln_b", ln_b, g_ln_b, m_ln_b, v_ln_b),
        "w_ffn1_in": big_t("f1i", w_ffn1_in, m_w_ffn1_in, v_w_ffn1_in),
        "w_ffn1_out": big("f1o", w_ffn1_out, m_w_ffn1_out, v_w_ffn1_out, 32),
        "w_in": big("win", w_in, m_w_in, v_w_in, 256),
        "b_in": tiny("b_in", b_in, g_b_in, m_b_in, v_b_in),
        "w_pool": big("wp", w_pool, m_w_pool, v_w_pool, 256),
        "pool_scale": tiny("pool_scale", pool_scale, g_ps, m_pool_scale, v_pool_scale),
        "sinks": tiny("sinks", sinks, g_sinks, m_sinks, v_sinks),
        "w_branch_a": big("wba", w_branch_a, m_w_branch_a, v_w_branch_a, 512),
        "w_branch_b": big("wbb", w_branch_b, m_w_branch_b, v_w_branch_b, 512),
        "w_out": big("wo", w_out, m_w_out, v_w_out, 128),
        "w_ffn2_in": big_t("f2i", w_ffn2_in, m_w_ffn2_in, v_w_ffn2_in),
        "w_ffn2_out": big("f2o", w_ffn2_out, m_w_ffn2_out, v_w_ffn2_out, 32),
    }
    names = ["w_ada", "b_ada", "ln_g", "ln_b", "w_ffn1_in", "w_ffn1_out", "w_in", "b_in", "w_pool", "pool_scale", "sinks",
             "w_branch_a", "w_branch_b", "w_out", "w_ffn2_in", "w_ffn2_out"]
    return (loss, dx0[None], *[res[n][0] for n in names], *[res[n][1] for n in names],
            *[res[n][2] for n in names], *[res[n][3] for n in names])
```

```python
import jax
import jax.numpy as jnp
from jax import lax
from jax.experimental import pallas as pl
from jax.experimental.pallas import tpu as pltpu

F32 = jnp.float32
BF16 = jnp.bfloat16
MESH = pl.DeviceIdType.MESH
ANY = pl.BlockSpec(memory_space=pl.ANY)

D = 2048
N_Q, N_KV, HD = 16, 4, 64
QW, KVW = N_Q * HD, N_KV * HD
BLK = 128
POOL_WINDOWS = (2, 4, 8, 16)
PW, PG = 1024, 256
HALO = 16
ROPE_THETA = 500000.0
ROT = HD // 4
LN_EPS = 1e-5
ALPHA = 2.0 ** 0.25
FH = 2752
FHP = 2816
FO = 1376
IN_W = 6656
IN_SH = IN_W // 4
ADA_SH = 18432 // 4
B1, B2, LR, EPS, WD, STEP = 0.9, 0.999, 0.001, 1e-08, 0.01, 10
VMEM_LIMIT = 56 * 1024 * 1024
FLIPS = ((1, 0), (0, 1), (1, 1))
NN = (((1,), (0,)), ((), ()))
NT = (((1,), (1,)), ((), ()))
TN = (((0,), (0,)), ((), ()))


def _params(sem):
    return pltpu.CompilerParams(dimension_semantics=sem, vmem_limit_bytes=VMEM_LIMIT)


def _aligned(v, m):
    return v if isinstance(v, int) else pl.multiple_of(v, m)


def _sigmoid(v):
    return 1.0 / (1.0 + jnp.exp(-v))


def T_(arr, width=None, off=0):
    return ("t", arr, width, off)


def B_(arr, width=None, off=0):
    return ("b", arr, width, off)


def X_(arr, spec):
    return ("x", arr, spec, 0)


def rowmap(name, fn, ins, outs, accs=(), *, rows, tm, ncol=1, with_ids=False, sp=None, alias=None):
    tm = min(tm, rows)
    nrow = rows // tm
    in_specs, arrs = [], []
    for kind, arr, width, off in ins:
        if kind == "x":
            in_specs.append(width)
        elif kind == "t":
            w = arr.shape[1] if width is None else width
            in_specs.append(pl.BlockSpec((tm, w), lambda j, i, *_, off=off: (i, off + j)))
        else:
            w = arr.shape[1] if width is None else width
            in_specs.append(pl.BlockSpec((arr.shape[0], w), lambda j, i, *_, off=off: (0, off + j)))
        arrs.append(arr)
    out_shape, out_specs = [], []
    for o in outs:
        if len(o) == 3:
            out_shape.append(jax.ShapeDtypeStruct(o[0], o[1]))
            out_specs.append(o[2])
        else:
            out_shape.append(jax.ShapeDtypeStruct((rows, o[0]), o[1]))
            out_specs.append(pl.BlockSpec((tm, o[0] // ncol), lambda j, i, *_: (i, j)))
    for r, width in accs:
        out_shape.append(jax.ShapeDtypeStruct((r, width), F32))
        out_specs.append(pl.BlockSpec((r, width // ncol), lambda j, i, *_: (0, j)))
    ni, no = len(ins), len(outs)
    nsp = 0 if sp is None else 1

    def body(*refs):
        refs = refs[nsp:]
        i = pl.program_id(1)
        vals = [r[...] for r in refs[:ni]]
        res = fn(pl.program_id(0), i, *vals) if with_ids else fn(*vals)
        if not isinstance(res, (tuple, list)):
            res = (res,)
        for r, v in zip(refs[ni:ni + no], res[:no]):
            r[...] = v.astype(r.dtype)
        for r, v in zip(refs[ni + no:], res[no:]):
            @pl.when(i == 0)
            def _(r=r, v=v):
                r[...] = v

            @pl.when(i > 0)
            def _(r=r, v=v):
                r[...] += v

    grid_spec = pltpu.PrefetchScalarGridSpec(num_scalar_prefetch=nsp, grid=(ncol, nrow), in_specs=in_specs,
                                             out_specs=out_specs)
    res = pl.pallas_call(
        body, name=name, grid_spec=grid_spec, out_shape=out_shape,
        input_output_aliases={nsp + k: v for k, v in (alias or {}).items()},
        compiler_params=_params(("arbitrary", "arbitrary")),
    )(*([sp] if nsp else []), *arrs)
    return res[0] if len(res) == 1 else res


def colsum(v):
    return jnp.sum(v, axis=0, keepdims=True)


def mm(name, a_ops, b_ops, ops, *, dims, grid, a_specs, b_specs, outs, out_specs, acc_shapes,
       epilogue=None, extras=(), extra_specs=(), carry=None, job=None, sub_rows=None):
    gk = grid[2]
    na, nb, ne, nacc = len(a_ops), len(b_ops), len(extras), len(acc_shapes)
    nc = 0 if carry is None else 1
    no = len(outs)

    def body(*refs):
        a_refs = refs[:na]
        b_refs = refs[na:na + nb]
        e_refs = refs[na + nb:na + nb + ne]
        o_refs = refs[na + nb + ne + nc:na + nb + ne + nc + no]
        acc_refs = refs[na + nb + ne + nc + no:]
        k = pl.program_id(2)

        def partials(rows=slice(None)):
            res = [None] * nacc
            for ai, bi, ci in ops:
                p = lax.dot_general(a_refs[ai][rows], b_refs[bi][...], dims, preferred_element_type=F32)
                res[ci] = p if res[ci] is None else res[ci] + p
            return res

        def finish(accs, rows=slice(None)):
            outv = epilogue(accs, [e[rows] for e in e_refs]) if epilogue else (accs[0],)
            for o, v in zip(o_refs, outv):
                o[rows] = v.astype(o.dtype)

        if gk == 1 and sub_rows:
            for s in range(out_specs[0].block_shape[-2] // sub_rows):
                rows = pl.ds(s * sub_rows, sub_rows)
                finish(partials(rows), rows)
        elif gk == 1:
            finish(partials())
        else:
            ps = partials()

            @pl.when(k == 0)
            def _():
                for acc, p in zip(acc_refs, ps):
                    acc[...] = p

            @pl.when((k > 0) & (k < gk - 1))
            def _():
                for acc, p in zip(acc_refs, ps):
                    acc[...] += p

            @pl.when(k == gk - 1)
            def _():
                finish([acc[...] + p for acc, p in zip(acc_refs, ps)])

    res, moved = carried_call(
        body, name, grid,
        list(a_specs) + list(b_specs) + list(extra_specs) + ([ANY] if nc else []), list(out_specs), list(outs),
        [pltpu.VMEM(s, F32) for s in acc_shapes] if gk > 1 else [],
        [*a_ops, *b_ops, *extras, *([carry] if nc else [])], {na + nb + ne: 0} if nc else {}, job)
    res = res[0] if len(res) == 1 else res
    return res if job is None else (res, moved)


def sds(shape, dt):
    return jax.ShapeDtypeStruct(shape, dt)


class Job:
    def __init__(self, ins, outs, aliases, scratch, start, mid, finish):
        self.ins, self.outs, self.aliases, self.scratch = list(ins), list(outs), dict(aliases), list(scratch)
        self.start, self.mid, self.finish = start, mid, finish


def carried_call(body, name, grid, in_specs, out_specs, out_shape, scratch, args, aliases, job, mid_at=0.9):
    sem = ("arbitrary",) * len(grid)
    if job is None:
        res = pl.pallas_call(body, name=name, grid=grid, in_specs=in_specs, out_specs=out_specs, out_shape=out_shape,
                             scratch_shapes=scratch, input_output_aliases=aliases, compiler_params=_params(sem))(*args)
        return list(res), []
    ni, no, ns = len(in_specs), len(out_specs), len(scratch)
    ci, co = len(job.ins), len(job.outs)
    total = 1
    for g in grid:
        total *= g
    mid_step = min(max(int(total * mid_at), 1), total - 1)

    def full(*refs):
        ins, cins = refs[:ni], refs[ni:ni + ci]
        outs, couts = refs[ni + ci:ni + ci + no], refs[ni + ci + no:ni + ci + no + co]
        scr, cscr = refs[ni + ci + no + co:ni + ci + no + co + ns], refs[ni + ci + no + co + ns:]
        step = 0
        for d, g in enumerate(grid):
            step = step * g + pl.program_id(d)

        @pl.when(step == 0)
        def _():
            job.start(cins, couts, cscr)

        body(*ins, *outs, *scr)

        @pl.when(step == mid_step)
        def _():
            job.mid(cins, couts, cscr)

        @pl.when(step == total - 1)
        def _():
            job.finish(cins, couts, cscr)

    al = dict(aliases)
    al.update({ni + k: no + v for k, v in job.aliases.items()})
    res = pl.pallas_call(
        full, name=name, grid=grid, in_specs=in_specs + [ANY] * ci, out_specs=out_specs + [ANY] * co,
        out_shape=out_shape + job.outs, scratch_shapes=scratch + job.scratch, input_output_aliases=al,
        compiler_params=_params(sem))(*args, *job.ins)
    return list(res[:no]), list(res[no:])


def _with_moved(res, job):
    return res if job is not None else (res, [])


def run_job(name, job):
    ci = len(job.ins)

    def body(*refs):
        cins, couts, cscr = refs[:ci], refs[ci:ci + len(job.outs)], refs[ci + len(job.outs):]
        job.start(cins, couts, cscr)
        job.mid(cins, couts, cscr)
        job.finish(cins, couts, cscr)

    return list(pl.pallas_call(
        body, name=name, in_specs=[ANY] * ci, out_specs=[ANY] * len(job.outs), out_shape=job.outs,
        scratch_shapes=job.scratch, input_output_aliases=job.aliases)(*job.ins))


def _place():
    x, y, c = lax.axis_index("x"), lax.axis_index("y"), lax.axis_index("c")
    chips = [((1 - x) if fx else x, (1 - y) if fy else y) for fx, fy in FLIPS]
    return x, y, c, chips


def allgather_small(name, v):
    r = v.shape[0]

    def body(x_ref, out_ref, send_sems, recv_sems, local_sem):
        x, y, c, chips = _place()
        me, sibling = (x, y, c), (x, y, 1 - c)

        def rows(px, py, pc):
            return out_ref.at[4 * px + 2 * py + pc]

        def copy(k, block, to, src=None):
            return pltpu.make_async_remote_copy(
                src_ref=rows(*block) if src is None else src, dst_ref=rows(*block),
                send_sem=send_sems.at[k], recv_sem=recv_sems.at[k], device_id=to, device_id_type=MESH)

        mine = pltpu.make_async_copy(x_ref, rows(*me), local_sem)
        mine.start()
        first = [copy(0, me, sibling, src=x_ref)]
        first += [copy(1 + j, me, (*chip, c), src=x_ref) for j, chip in enumerate(chips)]
        for cp in first:
            cp.start()
        passed = [copy(4 + j, (*chip, c), sibling) for j, chip in enumerate(chips)]
        for j, chip in enumerate(chips):
            copy(1 + j, (*chip, c), me).wait_recv()
            passed[j].start()
        copy(0, sibling, me).wait_recv()
        for j, chip in enumerate(chips):
            copy(4 + j, (*chip, 1 - c), me).wait_recv()
        for cp in first + passed:
            cp.wait_send()
        mine.wait()

    return pl.pallas_call(
        body, name=name, out_shape=sds((8, r, 128), v.dtype),
        in_specs=[pl.BlockSpec(memory_space=pltpu.VMEM)], out_specs=pl.BlockSpec(memory_space=pltpu.VMEM),
        scratch_shapes=[pltpu.SemaphoreType.DMA((7,)), pltpu.SemaphoreType.DMA((7,)), pltpu.SemaphoreType.DMA],
    )(v)


def _half(ref, rows, hf):
    hr = rows // 2
    return ref.at[pl.ds(_aligned(hf * hr, 16), hr)]


def view_lead(ref, p):
    return ref.at[p]


def view_ffn_out(ref, p):
    return ref.at[p // 2, pl.ds(_aligned((p % 2) * FO, 16), FO)]


def _remote(ref, dst, send_sems, recv_sems, idx, to):
    return pltpu.make_async_remote_copy(src_ref=ref, dst_ref=dst, send_sem=send_sems.at[idx], recv_sem=recv_sems.at[idx],
                                        device_id=to, device_id_type=MESH)


def gather_job(items):
    nw = len(items)
    pads = [w for w, it in enumerate(items) if it[1] is view_ffn_out]

    def piece(ref, w, p, hf):
        _, view, rws, part, parts = items[w]
        pr = rws // 2 // parts
        return view(ref, p).at[pl.ds(_aligned(hf * (rws // 2) + part * pr, 16), pr)]

    def pad_copies(outs, scr):
        return [pltpu.make_async_copy(scr[2], outs[w].at[h, pl.ds(2 * FO, FHP - 2 * FO)], scr[3].at[2 * n + h])
                for n, w in enumerate(pads) for h in range(2)]

    def start(_, outs, scr):
        x, y, c, chips = _place()
        if pads:
            scr[2][...] = jnp.zeros_like(scr[2])
            for cp in pad_copies(outs, scr):
                cp.start()
        for w in range(nw):
            mine = piece(outs[w], w, 2 * x + y, c)
            for f, (px, py) in enumerate(chips):
                _remote(mine, mine, scr[0], scr[1], (w, f), (px, py, c)).start()

    def mid(_, outs, scr):
        x, y, c, chips = _place()
        for w in range(nw):
            for f, (px, py) in enumerate(chips):
                land = piece(outs[w], w, 2 * px + py, c)
                _remote(land, land, scr[0], scr[1], (w, f), (px, py, c)).wait_recv()
                _remote(land, land, scr[0], scr[1], (w, 3 + f), (x, y, 1 - c)).start()

    def finish(_, outs, scr):
        x, y, c, chips = _place()
        for w in range(nw):
            for f, (px, py) in enumerate(chips):
                land = piece(outs[w], w, 2 * px + py, 1 - c)
                _remote(land, land, scr[0], scr[1], (w, 3 + f), (x, y, 1 - c)).wait_recv()
        for w in range(nw):
            mine = piece(outs[w], w, 2 * x + y, c)
            for f in range(6):
                _remote(mine, mine, scr[0], scr[1], (w, f), (x, y, 1 - c)).wait_send()
        for cp in pad_copies(outs, scr):
            cp.wait()

    scratch = [pltpu.SemaphoreType.DMA((nw, 6)), pltpu.SemaphoreType.DMA((nw, 6))]
    if pads:
        scratch += [pltpu.VMEM((FHP - 2 * FO, D), BF16), pltpu.SemaphoreType.DMA((2 * len(pads),))]
    bufs = [it[0] for it in items]
    return Job(bufs, [sds(b.shape, BF16) for b in bufs], {w: w for w in range(nw)}, scratch, start, mid, finish)


def reduce_sibling_job(items):
    nw = len(items)

    def copies(ins, got, scr):
        x, y, c, _ = _place()
        return [_remote(_half(view(ins[w], p), rws, 1 - c), got[w].at[p], scr[0], scr[1], (w, p), (x, y, 1 - c))
                for w, (_, view, rws, _) in enumerate(items) for p in range(4)]

    def start(ins, got, scr):
        for cp in copies(ins, got, scr):
            cp.start()

    def finish(ins, got, scr):
        for cp in copies(ins, got, scr):
            cp.wait()

    return Job([it[0] for it in items], [sds((4, it[2] // 2, it[3]), BF16) for it in items], {},
               [pltpu.SemaphoreType.DMA((nw, 4)), pltpu.SemaphoreType.DMA((nw, 4))], start, lambda *_: None, finish)


def reduce_chips_job(qs):
    nw = len(qs)

    def copies(ins, got, scr):
        x, y, c, chips = _place()
        return [_remote(ins[w].at[2 * px + py], got[w].at[f], scr[0], scr[1], (w, f), (px, py, c))
                for w in range(nw) for f, (px, py) in enumerate(chips)]

    def start(ins, got, scr):
        for cp in copies(ins, got, scr):
            cp.start()

    def finish(ins, got, scr):
        for cp in copies(ins, got, scr):
            cp.wait()

    return Job(qs, [sds((3,) + q.shape[1:], BF16) for q in qs], {},
               [pltpu.SemaphoreType.DMA((nw, 3)), pltpu.SemaphoreType.DMA((nw, 3))], start, lambda *_: None, finish)


def share_halves_job(gs):
    nw = len(gs)

    def start(_, outs, scr):
        x, y, c, _ = _place()
        for w in range(nw):
            mine = _half(outs[w], gs[w].shape[0], c)
            _remote(mine, mine, scr[0], scr[1], w, (x, y, 1 - c)).start()

    def finish(_, outs, scr):
        x, y, c, _ = _place()
        for w in range(nw):
            mine = _half(outs[w], gs[w].shape[0], c)
            theirs = _half(outs[w], gs[w].shape[0], 1 - c)
            _remote(mine, mine, scr[0], scr[1], w, (x, y, 1 - c)).wait_send()
            _remote(theirs, theirs, scr[0], scr[1], w, (x, y, 1 - c)).wait_recv()

    return Job(gs, [sds(g.shape, F32) for g in gs], {w: w for w in range(nw)},
               [pltpu.SemaphoreType.DMA((nw,)), pltpu.SemaphoreType.DMA((nw,))], start, lambda *_: None, finish)


def rope_tables(t):
    pos = jnp.arange(t, dtype=F32)
    inv_freq = ROPE_THETA ** (-jnp.arange(0, ROT, 2, dtype=F32) / ROT)
    ang = pos[:, None] * inv_freq[None, :]
    cos, sin = jnp.cos(ang), jnp.sin(ang)
    d = jnp.arange(128) % HD
    half = ROT // 2
    cs = jnp.take(cos, d % half, axis=1)
    sn = jnp.take(sin, d % half, axis=1)
    cc = jnp.where(d[None] < ROT, cs, 1.0)
    sa = jnp.where(d[None] < half, -sn, 0.0)
    sb = jnp.where((d[None] >= half) & (d[None] < ROT), sn, 0.0)
    return cc, sa, sb


def _rope(v, cc, sa, sb):
    w = v.shape[1]
    reps = w // 128
    half = ROT // 2
    return (v * jnp.tile(cc, (1, reps)) + pltpu.roll(v, w - half, 1) * jnp.tile(sa, (1, reps))
            + pltpu.roll(v, half, 1) * jnp.tile(sb, (1, reps)))


def _rope_t(dv, cc, sa, sb):
    w = dv.shape[1]
    reps = w // 128
    half = ROT // 2
    return (dv * jnp.tile(cc, (1, reps)) + pltpu.roll(dv * jnp.tile(sa, (1, reps)), half, 1)
            + pltpu.roll(dv * jnp.tile(sb, (1, reps)), w - half, 1))


def pool_fwd(h, b_in, t, tm):
    tm = min(tm, t)
    per = tm // HALO

    def body(prev_ref, cur_ref, b_ref, o_ref, xx):
        i = pl.program_id(0)
        b = b_ref[...]
        xx[pl.ds(0, HALO), :] = jnp.where(i > 0, prev_ref[...] + b, 0.0)
        xx[pl.ds(HALO, tm), :] = cur_ref[...] + b
        tpos = i * tm + lax.broadcasted_iota(jnp.int32, (tm, PG), 0) + 1
        for gi, w in enumerate(POOL_WINDOWS):
            cols = pl.ds(gi * PG, PG)
            acc = xx[pl.ds(HALO, tm), cols]
            for s in range(1, w):
                acc = acc + xx[pl.ds(HALO - s, tm), cols]
            cnt = jnp.minimum(tpos, w).astype(F32)
            o_ref[:, cols] = (acc / cnt - xx[pl.ds(HALO, tm), cols]).astype(o_ref.dtype)

    return pl.pallas_call(
        body, name="pool_fwd", grid=(t // tm,),
        in_specs=[pl.BlockSpec((HALO, PW), lambda i: (jnp.maximum(i * per - 1, 0), 0)),
                  pl.BlockSpec((tm, PW), lambda i: (i, 0)), pl.BlockSpec((1, PW), lambda i: (0, 0))],
        out_specs=pl.BlockSpec((tm, PW), lambda i: (i, 0)), out_shape=sds((t, PW), BF16),
        scratch_shapes=[pltpu.VMEM((tm + HALO, PW), F32)], compiler_params=_params(("arbitrary",)),
    )(h, h, b_in)


def pool_bwd(dpooled, t, tm):
    tm = min(tm, t)
    per = tm // HALO
    nt = t // tm

    def body(cur_ref, nxt_ref, o_ref, db_ref, ee):
        i = pl.program_id(0)
        tpos = i * tm + lax.broadcasted_iota(jnp.int32, (tm, PG), 0) + 1
        for gi, w in enumerate(POOL_WINDOWS):
            cols = pl.ds(gi * PG, PG)
            ee[pl.ds(0, tm), cols] = cur_ref[:, cols] / jnp.minimum(tpos, w).astype(F32)
            ee[pl.ds(tm, HALO), cols] = jnp.where(i < nt - 1, nxt_ref[:, cols] / float(w), 0.0)
        for gi, w in enumerate(POOL_WINDOWS):
            cols = pl.ds(gi * PG, PG)
            acc = ee[pl.ds(0, tm), cols]
            for s in range(1, w):
                acc = acc + ee[pl.ds(s, tm), cols]
            dxp = acc - cur_ref[:, cols]
            o_ref[:, cols] = dxp.astype(o_ref.dtype)
            part = colsum(dxp)

            @pl.when(i == 0)
            def _(cols=cols, part=part):
                db_ref[:, cols] = part

            @pl.when(i > 0)
            def _(cols=cols, part=part):
                db_ref[:, cols] += part

    return pl.pallas_call(
        body, name="pool_bwd", grid=(nt,),
        in_specs=[pl.BlockSpec((tm, PW), lambda i: (i, 0)),
                  pl.BlockSpec((HALO, PW), lambda i: (jnp.minimum((i + 1) * per, t // HALO - 1), 0))],
        out_specs=[pl.BlockSpec((tm, PW), lambda i: (i, 0)), pl.BlockSpec((1, PW), lambda i: (0, 0))],
        out_shape=[sds((t, PW), BF16), sds((1, PW), F32)],
        scratch_shapes=[pltpu.VMEM((tm + HALO, PW), F32)], compiler_params=_params(("arbitrary",)),
    )(dpooled, dpooled)


def _scores(qh, kp, kc, mask_p, mask_c, sink):
    sp = jnp.where(mask_p, lax.dot_general(qh, kp, NT, preferred_element_type=F32), -1e30)
    sc = jnp.where(mask_c, lax.dot_general(qh, kc, NT, preferred_element_type=F32), -1e30)
    m = jnp.maximum(jnp.maximum(jnp.max(sp, axis=-1, keepdims=True), jnp.max(sc, axis=-1, keepdims=True)), sink)
    pp, pc = jnp.exp(sp - m), jnp.exp(sc - m)
    es = jnp.exp(sink - m)
    inv = 1.0 / (jnp.sum(pp, axis=-1, keepdims=True) + jnp.sum(pc, axis=-1, keepdims=True) + es)
    return pp * inv, pc * inv, es * inv


GRP = N_Q // N_KV


def _masks(n):
    qi = lax.broadcasted_iota(jnp.int32, (GRP * BLK, BLK), 0) % BLK
    kj = lax.broadcasted_iota(jnp.int32, (GRP * BLK, BLK), 1)
    return (kj > qi) & (n > 0), kj <= qi


def _head(hk, g):
    return pl.ds(HD * (GRP * hk + g), HD)


def _stack_heads(ref, hk):
    return jnp.concatenate([ref[:, _head(hk, g)] for g in range(GRP)], axis=0)


def _stack_sinks(s_ref, hk):
    return jnp.concatenate([jnp.full((BLK, 1), s_ref[0, GRP * hk + g], F32) for g in range(GRP)], axis=0)


def attn_fwd(q, k, v, sinks, t, job=None):
    def body(s_ref, q_ref, kp_ref, kc_ref, vp_ref, vc_ref, o_ref):
        n = pl.program_id(0)
        mask_p, mask_c = _masks(n)
        for hk in range(N_KV):
            kv = pl.ds(HD * hk, HD)
            pp, pc, _ = _scores(_stack_heads(q_ref, hk), kp_ref[:, kv], kc_ref[:, kv], mask_p, mask_c,
                                _stack_sinks(s_ref, hk))
            o = (lax.dot_general(pp.astype(BF16), vp_ref[:, kv], NN, preferred_element_type=F32)
                 + lax.dot_general(pc.astype(BF16), vc_ref[:, kv], NN, preferred_element_type=F32))
            for g in range(GRP):
                o_ref[:, _head(hk, g)] = o[g * BLK:(g + 1) * BLK].astype(o_ref.dtype)

    prev = lambda n: (jnp.maximum(n - 1, 0), 0)
    cur = lambda n: (n, 0)
    res, moved = carried_call(
        body, "attn_fwd", (t // BLK,),
        [pl.BlockSpec(memory_space=pltpu.SMEM), pl.BlockSpec((BLK, QW), cur),
         pl.BlockSpec((BLK, KVW), prev), pl.BlockSpec((BLK, KVW), cur),
         pl.BlockSpec((BLK, KVW), prev), pl.BlockSpec((BLK, KVW), cur)],
        [pl.BlockSpec((BLK, QW), cur)], [sds((t, QW), BF16)], [], [sinks, q, k, k, v, v], {}, job)
    return res[0], moved


def attn_bwd(q, k, v, do, sinks, t):
    nb = t // BLK

    def body(s_ref, q_ref, do_ref, kp_ref, kc_ref, vp_ref, vc_ref, dq_ref, dk_ref, dv_ref, ds_ref, dkc, dvc):
        n = pl.program_id(0)

        @pl.when(n == 0)
        def _():
            dkc[...] = jnp.zeros_like(dkc)
            dvc[...] = jnp.zeros_like(dvc)
            ds_ref[...] = jnp.zeros_like(ds_ref)

        @pl.when(n < nb)
        def _():
            mask_p, mask_c = _masks(n)
            lane = lax.broadcasted_iota(jnp.int32, (1, 128), 1)
            dsink = jnp.zeros((1, 128), F32)
            for hk in range(N_KV):
                kv = pl.ds(HD * hk, HD)
                kp, kc, vp, vc = kp_ref[:, kv], kc_ref[:, kv], vp_ref[:, kv], vc_ref[:, kv]
                qs, dos = _stack_heads(q_ref, hk), _stack_heads(do_ref, hk)
                pp, pc, ps = _scores(qs, kp, kc, mask_p, mask_c, _stack_sinks(s_ref, hk))
                dpp = lax.dot_general(dos, vp, NT, preferred_element_type=F32)
                dpc = lax.dot_general(dos, vc, NT, preferred_element_type=F32)
                delta = jnp.sum(pp * dpp, axis=-1, keepdims=True) + jnp.sum(pc * dpc, axis=-1, keepdims=True)
                dsp = (pp * (dpp - delta)).astype(BF16)
                dsc = (pc * (dpc - delta)).astype(BF16)
                sd = ps * delta
                dq = (lax.dot_general(dsp, kp, NN, preferred_element_type=F32)
                      + lax.dot_general(dsc, kc, NN, preferred_element_type=F32))
                for g in range(GRP):
                    rows = slice(g * BLK, (g + 1) * BLK)
                    dsink = dsink + jnp.where(lane == GRP * hk + g, -jnp.sum(sd[rows]), 0.0)
                    dq_ref[:, _head(hk, g)] = dq[rows]
                dk_ref[:, kv] = dkc[:, kv] + lax.dot_general(dsp, qs, TN, preferred_element_type=F32)
                dv_ref[:, kv] = dvc[:, kv] + lax.dot_general(pp.astype(BF16), dos, TN, preferred_element_type=F32)
                dkc[:, kv] = lax.dot_general(dsc, qs, TN, preferred_element_type=F32)
                dvc[:, kv] = lax.dot_general(pc.astype(BF16), dos, TN, preferred_element_type=F32)
            ds_ref[...] += dsink

        @pl.when(n == nb)
        def _():
            dk_ref[...] = dkc[...]
            dv_ref[...] = dvc[...]

    cur = lambda n: (jnp.minimum(n, nb - 1), 0)
    prev = lambda n: (jnp.clip(n - 1, 0, nb - 1), 0)
    return pl.pallas_call(
        body, name="attn_bwd", grid=(nb + 1,),
        in_specs=[pl.BlockSpec(memory_space=pltpu.SMEM), pl.BlockSpec((BLK, QW), cur), pl.BlockSpec((BLK, QW), cur),
                  pl.BlockSpec((BLK, KVW), prev), pl.BlockSpec((BLK, KVW), cur),
                  pl.BlockSpec((BLK, KVW), prev), pl.BlockSpec((BLK, KVW), cur)],
        out_specs=[pl.BlockSpec((BLK, QW), cur), pl.BlockSpec((BLK, KVW), prev), pl.BlockSpec((BLK, KVW), prev),
                   pl.BlockSpec((1, 128), lambda n: (0, 0))],
        out_shape=[sds((t, QW), F32), sds((t, KVW), F32), sds((t, KVW), F32), sds((1, 128), F32)],
        scratch_shapes=[pltpu.VMEM((BLK, KVW), F32), pltpu.VMEM((BLK, KVW), F32)],
        compiler_params=_params(("arbitrary",)),
    )(sinks, q, do, k, k, v, v)


def _adamw(w, g, m, v):
    m2 = B1 * m + (1.0 - B1) * g
    v2 = B2 * v + (1.0 - B2) * jnp.square(g)
    m_hat = m2 / (1.0 - B1 ** STEP)
    v_hat = v2 / (1.0 - B2 ** STEP)
    return -LR * (m_hat / (jnp.sqrt(v_hat) + EPS) + WD * w), m2, v2


def ada_fwd(c16, w_ada, b_sh):
    tn = 512

    def body(c_ref, w_ref, b_ref, o_ref):
        cv = c_ref[...]
        sc = (cv * _sigmoid(cv)).astype(BF16)
        o_ref[...] = lax.dot_general(sc, w_ref[...].astype(BF16), NN, preferred_element_type=F32) + b_ref[...]

    return pl.pallas_call(
        body, name="ada_fwd", grid=(ADA_SH // tn,),
        in_specs=[pl.BlockSpec((16, D), lambda j: (0, 0)), pl.BlockSpec((D, tn), lambda j: (0, j)),
                  pl.BlockSpec((1, tn), lambda j: (0, j))],
        out_specs=pl.BlockSpec((16, tn), lambda j: (0, j)), out_shape=sds((16, ADA_SH), F32),
        compiler_params=_params(("arbitrary",)),
    )(c16, w_ada, b_sh)


def ada_bwd_adam(c16, gm16, w, m, v, job):
    tm, tn = 256, 512

    def body(c_ref, g_ref, w_ref, m_ref, v_ref, go_ref, d_ref, mo_ref, vo_ref):
        cv = c_ref[...]
        sc = (cv * _sigmoid(cv)).astype(BF16)
        g = lax.dot_general(sc, g_ref[...].astype(BF16), TN, preferred_element_type=F32)
        dl, m2, v2 = _adamw(w_ref[...], g, m_ref[...], v_ref[...])
        go_ref[...] = g
        d_ref[...] = dl
        mo_ref[...] = m2
        vo_ref[...] = v2

    blk = pl.BlockSpec((tm, tn), lambda i, j: (i, j))
    return carried_call(
        body, "ada_bwd_adam", (D // tm, ADA_SH // tn),
        [pl.BlockSpec((16, tm), lambda i, j: (0, i)), pl.BlockSpec((16, tn), lambda i, j: (0, j)), blk, blk, blk],
        [blk] * 4, [sds((D, ADA_SH), F32)] * 4, [], [c16, gm16, w, m, v], {}, job)


def adam_rows(name, w, g, m, v, tm):
    rows, cols = w.shape

    def fn(wv, gv, mv, vv):
        gv = gv[:, :cols]
        dl, m2, v2 = _adamw(wv, gv, mv, vv)
        return gv, dl, m2, v2

    return rowmap(name, fn, [T_(w), T_(g), T_(m), T_(v)], [(cols, F32)] * 4, rows=rows, tm=tm)


def adam_small(name, w, g, m, v):
    def body(w_ref, g_ref, m_ref, v_ref, d_ref, mo_ref, vo_ref):
        dl, m2, v2 = _adamw(w_ref[...], g_ref[...], m_ref[...], v_ref[...])
        d_ref[...] = dl
        mo_ref[...] = m2
        vo_ref[...] = v2

    return pl.pallas_call(body, name=name, out_shape=[sds(w.shape, F32)] * 3)(w, g, m, v)


def sum_devices(allv):
    def body(a_ref, o_ref):
        acc = a_ref[0]
        for d in range(1, 8):
            acc = acc + a_ref[d]
        o_ref[...] = acc

    return pl.pallas_call(body, name="sum_devices", out_shape=sds(allv.shape[1:], F32))(allv)


def _ln_fwd(z, g, b):
    mu = jnp.mean(z, axis=-1, keepdims=True)
    zc = z - mu
    var = jnp.mean(jnp.square(zc), axis=-1, keepdims=True)
    return zc * lax.rsqrt(var + LN_EPS) * g + b


def _ln_bwd(z, g, dout):
    mu = jnp.mean(z, axis=-1, keepdims=True)
    zc = z - mu
    var = jnp.mean(jnp.square(zc), axis=-1, keepdims=True)
    rstd = lax.rsqrt(var + LN_EPS)
    xh = zc * rstd
    dxh = dout * g
    dz = rstd * (dxh - jnp.mean(dxh, axis=-1, keepdims=True) - xh * jnp.mean(dxh * xh, axis=-1, keepdims=True))
    return dz, colsum(dout * xh), colsum(dout)


def modulate(name, xin, shift, scale, t):
    return rowmap(name, lambda xv, sh, sc: xv * (1.0 + sc) + sh, [T_(xin), B_(shift), B_(scale)], [(D, BF16)],
                  rows=t, tm=512)


def residual_ln_mod(name, xin, y, gate, lg, lb, wgt, shift_n, scale_n, t):
    def fn(xv, yv, gt, g, b, sh, sc):
        z = ALPHA * xv + (wgt * (1.0 + gt)) * yv
        xo = _ln_fwd(z, g, b)
        return xo, z, xo * (1.0 + sc) + sh

    return rowmap(name, fn, [T_(xin), T_(y), B_(gate), B_(lg), B_(lb), B_(shift_n), B_(scale_n)],
                  [(D, F32), (D, F32), (D, BF16)], rows=t, tm=256)


def residual_ln_bwd(name, z, dnext, y, gate, lg, wgt, t):
    dzn, dun, xn, scn = dnext

    def fn(zv, yv, gt, g, dzv, duv, xv, sc):
        dv = ALPHA * dzv + duv * (1.0 + sc)
        dz, dg, db = _ln_bwd(zv, g, dv)
        return dz, (wgt * (1.0 + gt)) * dz, dg, db, colsum(wgt * dz * yv), colsum(duv), colsum(duv * xv)

    return rowmap(name, fn, [T_(z), T_(y), B_(gate), B_(lg), T_(dzn), T_(dun), T_(xn), B_(scn)],
                  [(D, F32), (D, BF16)], [(1, D)] * 5, rows=t, tm=256)


def residual_ln_loss_bwd(name, xin, y, tgt, gate, lg, lb, wgt, t):
    def fn(xv, yv, tv, gt, g, b):
        z = ALPHA * xv + (wgt * (1.0 + gt)) * yv
        d = _ln_fwd(z, g, b) - tv
        dz, dg, db = _ln_bwd(z, g, d * (1.0 / D))
        return dz, (wgt * (1.0 + gt)) * dz, dg, db, colsum(wgt * dz * yv), jnp.sum(d * d).reshape(1, 1)

    dz, dy, dlg, dlb, dgate, sq = rowmap(
        name, fn, [T_(xin), T_(y), T_(tgt), B_(gate), B_(lg), B_(lb)], [(D, F32), (D, BF16)],
        [(1, D), (1, D), (1, D), (1, 1)], rows=t, tm=256)
    return dz, dy, dlg, dlb, dgate, sq


def modulate_bwd(name, dz, du, xin, scale, t):
    def fn(dzv, duv, xv, sc):
        return ALPHA * dzv + duv * (1.0 + sc), colsum(duv), colsum(duv * xv)

    return rowmap(name, fn, [T_(dz), T_(du), T_(xin), B_(scale)], [(D, F32)], [(1, D), (1, D)], rows=t, tm=256)


def ffn_fwd(tag, u, wi, t, up_job, down_job=None):
    tm = min(1024, t)
    tn = 256
    per = FHP // tn

    def act(accs, _):
        a, b = accs
        s = _sigmoid(a)
        sl = a * s
        return b * (s * (1.0 + a * (1.0 - s))), sl, sl * b

    tmu = min(2048, t)
    hblk = pl.BlockSpec((tmu, tn), lambda i, j, k: (i, j))
    (ha, hb, g), up_moved = mm(
        tag + "_up", [u], [wi, wi], [(0, 0, 0), (0, 1, 1)], dims=NT, grid=(t // tmu, 2 * per, 1),
        a_specs=[pl.BlockSpec((tmu, D), lambda i, j, k: (i, 0))],
        b_specs=[pl.BlockSpec((None, tn, D), lambda i, j, k: (j // per, j % per, 0)),
                 pl.BlockSpec((None, tn, D), lambda i, j, k: (2 + j // per, j % per, 0))],
        outs=[sds((t, 2 * FHP), BF16)] * 3, out_specs=[hblk] * 3, acc_shapes=[(tmu, tn)] * 2, epilogue=act, job=up_job,
        sub_rows=tmu // 2)
    wo = up_moved[0].reshape(2 * FHP, D)
    tk = FHP
    y, down_moved = _with_moved(mm(
        tag + "_down", [g], [wo], [(0, 0, 0)], dims=NN, grid=(t // tm, 2, 2),
        a_specs=[pl.BlockSpec((tm, tk), lambda i, j, k: (i, k))],
        b_specs=[pl.BlockSpec((tk, D // 2), lambda i, j, k: (k, j))],
        outs=[sds((t, D), F32)], out_specs=[pl.BlockSpec((tm, D // 2), lambda i, j, k: (i, j))],
        acc_shapes=[(tm, D // 2)], job=down_job), down_job)
    return ha, hb, g, y, wo, up_moved, down_moved


def ffn_bwd(tag, u, ha, hb, g, dy, wi, wo, t, sp, dact_job=None, dwo_job=None):
    tm = min(1024, t)

    def dact(accs, ex):
        dg = accs[0]
        return dg * ex[0].astype(F32), dg * ex[1].astype(F32)

    tn = 256
    tmu = min(2048, t)
    hblk = pl.BlockSpec((tmu, tn), lambda i, j, k: (i, j))
    (dha, dhb), dact_moved = _with_moved(mm(
        tag + "_dact", [dy], [wo], [(0, 0, 0)], dims=NT, grid=(t // tmu, 2 * FHP // tn, 1),
        a_specs=[pl.BlockSpec((tmu, D), lambda i, j, k: (i, 0))],
        b_specs=[pl.BlockSpec((tn, D), lambda i, j, k: (j, 0))],
        outs=[sds((t, 2 * FHP), BF16)] * 2, out_specs=[hblk] * 2, acc_shapes=[(tmu, tn)],
        epilogue=dact, extras=[ha, hb], extra_specs=[hblk] * 2, job=dact_job, sub_rows=tmu // 2), dact_job)
    tk = min(2048, t)
    th = FHP // 2
    dwo, dwo_moved = _with_moved(mm(
        tag + "_dwo", [g], [dy], [(0, 0, 0)], dims=TN, grid=(4, 2, t // tk),
        a_specs=[pl.BlockSpec((tk, th), lambda i, j, k: (k, i))],
        b_specs=[pl.BlockSpec((tk, D // 2), lambda i, j, k: (k, j))],
        outs=[sds((2 * FHP, D), BF16)], out_specs=[pl.BlockSpec((th, D // 2), lambda i, j, k: (i, j))],
        acc_shapes=[(th, D // 2)], job=dwo_job), dwo_job)
    dwo = dwo.reshape(2, FHP, D)

    def dwi_part(part, dh, carry, job):
        return mm(
            f"{tag}_dwi{part}", [dh], [u], [(0, 0, 0)], dims=TN, grid=(4, 2, t // tk),
            a_specs=[pl.BlockSpec((tk, th), lambda i, j, k: (k, i))],
            b_specs=[pl.BlockSpec((tk, D // 2), lambda i, j, k: (k, j))],
            outs=[sds((4, FHP, D), BF16)],
            out_specs=[pl.BlockSpec((None, th, D // 2), lambda i, j, k: (2 * part + i // 2, i % 2, j))],
            acc_shapes=[(th, D // 2)], carry=carry, job=job)

    dwi, (sib_fo,) = dwi_part(0, dha, None, reduce_sibling_job([(dwo, view_ffn_out, FO, D)]))
    q_fo = chip_sum(tag + "_chipsum_fo", dwo, sib_fo, sp, FO, FO // 2, ffn_out=True)
    dwi, (far_fo,) = dwi_part(1, dhb, dwi, reduce_chips_job([q_fo]))
    (sib_fi,) = run_job(tag + "_sibling_fi", reduce_sibling_job([(dwi, view_lead, FHP, D)]))
    q_fi = chip_sum(tag + "_chipsum_fi", dwi, sib_fi, sp, FHP, FHP // 8)
    tmd = min(512, t)
    du, (far_fi,) = mm(
        tag + "_du", [dha, dhb], [wi, wi], [(0, 0, 0), (1, 1, 0)], dims=NN, grid=(t // tmd, 2, 2),
        a_specs=[pl.BlockSpec((tmd, FHP), lambda i, j, k: (i, k))] * 2,
        b_specs=[pl.BlockSpec((None, FHP, D // 2), lambda i, j, k: (k, 0, j)),
                 pl.BlockSpec((None, FHP, D // 2), lambda i, j, k: (2 + k, 0, j))],
        outs=[sds((t, D), F32)], out_specs=[pl.BlockSpec((tmd, D // 2), lambda i, j, k: (i, j))],
        acc_shapes=[(tmd, D // 2)], job=reduce_chips_job([q_fi]))
    return du, (q_fi, far_fi), (q_fo, far_fo), dact_moved, dwo_moved


def mix_fwd(u, wts, b_in, pool_scale, sinks, tabs, t, in_job, attn_job):
    w_in, wp, wba, wbb, wo = wts
    tm = min(1024, t)
    tmh = min(512, t)
    h, in_moved = mm("mix_in", [u], [w_in], [(0, 0, 0)], dims=NN, grid=(t // tmh, 4, 1),
                     a_specs=[pl.BlockSpec((tmh, D), lambda i, j, k: (i, 0))],
                     b_specs=[pl.BlockSpec((None, D, IN_SH), lambda i, j, k: (j, 0, 0))],
                     outs=[sds((t, IN_W), F32)], out_specs=[pl.BlockSpec((tmh, IN_SH), lambda i, j, k: (i, j))],
                     acc_shapes=[(tmh, IN_SH)], job=in_job)
    attn_job = attn_job(in_moved)
    pooled = pool_fwd(h, b_in, t, 512)
    gblk = pl.BlockSpec((tm, PG), lambda i, j, k: (i, j))
    mixed = mm("mix_pool", [pooled], [wp], [(0, 0, 0)], dims=NN, grid=(t // tm, 4, 1), a_specs=[gblk],
               b_specs=[pl.BlockSpec((None, PG, PG), lambda i, j, k: (j, 0, 0))],
               outs=[sds((t, PW), F32)], out_specs=[gblk], acc_shapes=[(tm, PG)])
    pm = rowmap("mix_pscale", lambda mv, ps: mv * ps, [T_(mixed), B_(pool_scale)], [(PW, BF16)], rows=t, tm=512)

    def branch(name, a, w):
        return mm(name, [a], [w], [(0, 0, 0)], dims=NN, grid=(t // tm, 4, 1),
                  a_specs=[pl.BlockSpec((tm, PW), lambda i, j, k: (i, 0))],
                  b_specs=[pl.BlockSpec((None, PW, D // 4), lambda i, j, k: (j, 0, 0))],
                  outs=[sds((t, D), F32)], out_specs=[pl.BlockSpec((tm, D // 4), lambda i, j, k: (i, j))],
                  acc_shapes=[(tm, D // 4)])

    ya = branch("mix_branch_a", pm, wba)

    def qkv(hq, hk, hv, bq, bk, bv, cc, sa, sb):
        return (_rope(hq + bq, cc, sa, sb) * (HD ** -0.5), _rope(hk + bk, cc, sa, sb), hv + bv)

    qr, kr, vv = rowmap(
        "mix_rope", qkv,
        [T_(h, QW, 1), T_(h, KVW, 8), T_(h, KVW, 9), B_(b_in, QW, 1), B_(b_in, KVW, 8), B_(b_in, KVW, 9),
         T_(tabs[0]), T_(tabs[1]), T_(tabs[2])],
        [(QW, BF16), (KVW, BF16), (KVW, BF16)], rows=t, tm=512)
    attn, attn_moved = attn_fwd(qr, kr, vv, sinks, t, attn_job)
    yb = branch("mix_branch_b", attn, wbb)
    cw = 512

    def merge(ga, gb, ba, bb, yav, ybv):
        return _sigmoid(ga + ba) * yav + _sigmoid(gb + bb) * ybv

    merged = rowmap(
        "mix_merge", merge,
        [T_(h, cw, 5), T_(h, cw, 9), B_(b_in, cw, 5), B_(b_in, cw, 9), T_(ya, cw), T_(yb, cw)],
        [(D, BF16)], rows=t, tm=512, ncol=D // cw)
    y = mm("mix_out", [merged], [wo], [(0, 0, 0)], dims=NN, grid=(t // tm, 2, 1),
           a_specs=[pl.BlockSpec((tm, D), lambda i, j, k: (i, 0))],
           b_specs=[pl.BlockSpec((D, D // 2), lambda i, j, k: (0, j))],
           outs=[sds((t, D), F32)], out_specs=[pl.BlockSpec((tm, D // 2), lambda i, j, k: (i, j))],
           acc_shapes=[(tm, D // 2)])
    return y, (h, pooled, mixed, pm, ya, qr, kr, vv, attn, yb, merged), attn_moved


def mix_bwd(u, saved, dy, wts, b_in, pool_scale, sinks, tabs, t):
    h, pooled, mixed, pm, ya, qr, kr, vv, attn, yb, merged = saved
    w_in, wp, wba, wbb, wo = wts
    tm = min(1024, t)
    tk = min(2048, t)
    dmerged = mm("mix_dmerged", [dy], [wo], [(0, 0, 0)], dims=NT, grid=(t // tm, 2, 1),
                 a_specs=[pl.BlockSpec((tm, D), lambda i, j, k: (i, 0))],
                 b_specs=[pl.BlockSpec((D // 2, D), lambda i, j, k: (j, 0))],
                 outs=[sds((t, D), F32)], out_specs=[pl.BlockSpec((tm, D // 2), lambda i, j, k: (i, j))],
                 acc_shapes=[(tm, D // 2)])
    half = pl.BlockSpec((tk, D // 2), lambda i, j, k: (k, i))
    dwo = mm("mix_dwo", [merged], [dy], [(0, 0, 0)], dims=TN, grid=(2, 2, t // tk), a_specs=[half],
             b_specs=[pl.BlockSpec((tk, D // 2), lambda i, j, k: (k, j))],
             outs=[sds((D, D), BF16)], out_specs=[pl.BlockSpec((D // 2, D // 2), lambda i, j, k: (i, j))],
             acc_shapes=[(D // 2, D // 2)])
    cw = 512

    def dmerge(dm, ga, gb, ba, bb, yav, ybv):
        sa_, sb_ = _sigmoid(ga + ba), _sigmoid(gb + bb)
        dga = dm * yav * sa_ * (1.0 - sa_)
        dgb = dm * ybv * sb_ * (1.0 - sb_)
        return dm * sa_, dm * sb_, dga, dgb, colsum(dga), colsum(dgb)

    dya, dyb, dgla, dglb, dbga, dbgb = rowmap(
        "mix_dmerge", dmerge,
        [T_(dmerged, cw), T_(h, cw, 5), T_(h, cw, 9), B_(b_in, cw, 5), B_(b_in, cw, 9), T_(ya, cw), T_(yb, cw)],
        [(D, BF16)] * 4, [(1, D), (1, D)], rows=t, tm=512, ncol=D // cw)

    def dbranch(name, dyv, act, w):
        dwb = mm(name + "_dw", [act], [dyv], [(0, 0, 0)], dims=TN, grid=(1, 4, t // tk),
                 a_specs=[pl.BlockSpec((tk, PW), lambda i, j, k: (k, 0))],
                 b_specs=[pl.BlockSpec((tk, D // 4), lambda i, j, k: (k, j))],
                 outs=[sds((4, PW, D // 4), BF16)], out_specs=[pl.BlockSpec((None, PW, D // 4), lambda i, j, k: (j, 0, 0))],
                 acc_shapes=[(PW, D // 4)])
        return dwb, lambda dt: mm(
            name + "_dx", [dyv], [w], [(0, 0, 0)], dims=NT, grid=(t // tm, 1, 4),
            a_specs=[pl.BlockSpec((tm, D // 4), lambda i, j, k: (i, k))],
            b_specs=[pl.BlockSpec((None, PW, D // 4), lambda i, j, k: (k, 0, 0))],
            outs=[sds((t, PW), dt)], out_specs=[pl.BlockSpec((tm, PW), lambda i, j, k: (i, 0))], acc_shapes=[(tm, PW)])

    dwba, dpm_fn = dbranch("mix_dbranch_a", dya, pm, wba)
    dwbb, dattn_fn = dbranch("mix_dbranch_b", dyb, attn, wbb)
    dpm, dattn = dpm_fn(F32), dattn_fn(BF16)
    dmixed, dps = rowmap("mix_dpscale", lambda dp, mv, ps: (dp * ps, colsum(dp * mv)),
                         [T_(dpm), T_(mixed), B_(pool_scale)], [(PW, BF16)], [(1, PW)], rows=t, tm=512)
    gblk = pl.BlockSpec((tm, PG), lambda i, j, k: (i, j))
    dpooled = mm("mix_dpool", [dmixed], [wp], [(0, 0, 0)], dims=NT, grid=(t // tm, 4, 1), a_specs=[gblk],
                 b_specs=[pl.BlockSpec((None, PG, PG), lambda i, j, k: (j, 0, 0))],
                 outs=[sds((t, PW), F32)], out_specs=[gblk], acc_shapes=[(tm, PG)])
    kblk = pl.BlockSpec((tk, PG), lambda i, j, k: (k, i))
    dwp = mm("mix_dwpool", [pooled], [dmixed], [(0, 0, 0)], dims=TN, grid=(4, 1, t // tk), a_specs=[kblk], b_specs=[kblk],
             outs=[sds((4, PG, PG), BF16)], out_specs=[pl.BlockSpec((None, PG, PG), lambda i, j, k: (i, 0, 0))],
             acc_shapes=[(PG, PG)])
    dxp, dbxp = pool_bwd(dpooled, t, 512)
    dqr, dkr, dvv, dsinks = attn_bwd(qr, kr, vv, dattn, sinks, t)

    def dqkv(dq, dk, dv, cc, sa, sb):
        dq = _rope_t(dq, cc, sa, sb) * (HD ** -0.5)
        dk = _rope_t(dk, cc, sa, sb)
        return dq, dk, dv, colsum(dq), colsum(dk), colsum(dv)

    dq, dk, dvb, dbq, dbk, dbv = rowmap(
        "mix_rope_bwd", dqkv, [T_(dqr), T_(dkr), T_(dvv), T_(tabs[0]), T_(tabs[1]), T_(tabs[2])],
        [(QW, BF16), (KVW, BF16), (KVW, BF16)], [(1, QW), (1, KVW), (1, KVW)], rows=t, tm=512)
    dh = jnp.concatenate([dxp, dq, dk, dvb, dgla, dglb], axis=1)
    db_in = jnp.concatenate([dbxp, dbq, dbk, dbv, dbga, dbgb], axis=1)
    dwin = mm("mix_dwin", [u], [dh], [(0, 0, 0)], dims=TN, grid=(2, 4, t // tk), a_specs=[half],
              b_specs=[pl.BlockSpec((tk, IN_SH), lambda i, j, k: (k, j))],
              outs=[sds((4, D, IN_SH), BF16)], out_specs=[pl.BlockSpec((None, D // 2, IN_SH), lambda i, j, k: (j, i, 0))],
              acc_shapes=[(D // 2, IN_SH)])
    dwp_sh = jnp.transpose(dwp.reshape(4, 4, 64, PG), (1, 0, 2, 3)).reshape(4, 4 * 64, PG)
    parts = {"win": dwin, "wp": dwp_sh, "wba": dwba, "wbb": dwbb, "wo": dwo.reshape(4, D // 4, D)}
    du, sib = mm("mix_du", [dh], [w_in], [(0, 0, 0)], dims=NT, grid=(t // tm, 2, 4),
                 a_specs=[pl.BlockSpec((tm, IN_SH), lambda i, j, k: (i, k))],
                 b_specs=[pl.BlockSpec((None, D // 2, IN_SH), lambda i, j, k: (k, j, 0))],
                 outs=[sds((t, D), F32)], out_specs=[pl.BlockSpec((tm, D // 2), lambda i, j, k: (i, j))],
                 acc_shapes=[(tm, D // 2)],
                 job=reduce_sibling_job([(p, view_lead, p.shape[1], p.shape[2]) for p in parts.values()]))
    return du, parts, dict(zip(parts, sib)), db_in, dps, dsinks


def cast_shard(name, w, sp, ffn_out=False):
    rows, cols = w.shape
    if ffn_out:
        tm = rows // 2
        shape = (2, FHP, D)
        spec = pl.BlockSpec((None, tm, cols), lambda j, i, s: (s[0] // 2, (s[0] % 2) * 2 + i, 0))
    else:
        tm = rows // 4
        shape = (4, rows, cols)
        spec = pl.BlockSpec((None, tm, cols), lambda j, i, s: (s[0], i, 0))
    return rowmap(name, lambda wv: wv, [T_(w)], [(shape, BF16, spec)], rows=rows, tm=tm, sp=sp)


def cast_ffn_in(name, wt, sp):
    tm = 64
    full = FH // tm

    def fn(_, i, wv):
        return jnp.where(i < full, wv, 0.0)

    return rowmap(name, fn, [X_(wt, pl.BlockSpec((tm, D), lambda j, i, s: (jnp.minimum(i, full - 1), 0)))],
                  [((4, FHP, D), BF16, pl.BlockSpec((None, tm, D), lambda j, i, s: (s[0], i, 0)))],
                  rows=FHP, tm=tm, sp=sp, with_ids=True)


def chip_sum(name, dw, got, sp, rows, tm, ffn_out=False):
    hr, cols = rows // 2, got.shape[2]
    per = hr // tm
    pos = pl.BlockSpec((None, tm, cols), lambda j, i, s: (i // per, i % per, 0))
    if ffn_out:
        mine = pl.BlockSpec((None, tm, cols), lambda j, i, s: (i // 2, (i % 2) * 2 + s[1], 0))
    else:
        mine = pl.BlockSpec((None, tm, cols), lambda j, i, s: (i // per, s[1] * per + i % per, 0))
    return rowmap(name, lambda av, bv: av.astype(F32) + bv.astype(F32), [X_(dw, mine), X_(got, pos)],
                  [(got.shape, BF16, pos)], rows=4 * hr, tm=tm, sp=sp)


def chip_total(name, q, got, sp, rows, tm):
    hr, cols = rows // 2, q.shape[2]
    per = hr // tm

    def part(f):
        return X_(got, pl.BlockSpec((None, tm, cols), lambda j, i, s, f=f: (f, i, 0)))

    return rowmap(
        name, lambda av, b0, b1, b2: ((av.astype(F32) + b0.astype(F32)) + b1.astype(F32)) + b2.astype(F32),
        [X_(q, pl.BlockSpec((None, tm, cols), lambda j, i, s: (s[0], i, 0))), part(0), part(1), part(2)],
        [((rows, cols), F32, pl.BlockSpec((tm, cols), lambda j, i, s: (s[1] * per + i, 0)))], rows=hr, tm=tm, sp=sp)


def kernel(x, c, w_ada, b_ada, ln_g, ln_b, w_ffn1_in, w_ffn1_out, w_in, b_in, w_pool, pool_scale, sinks, w_branch_a, w_branch_b, w_out, w_ffn2_in, w_ffn2_out, loss_target, m_w_ada, m_b_ada, m_ln_g, m_ln_b, m_w_ffn1_in, m_w_ffn1_out, m_w_in, m_b_in, m_w_pool, m_pool_scale, m_sinks, m_w_branch_a, m_w_branch_b, m_w_out, m_w_ffn2_in, m_w_ffn2_out, v_w_ada, v_b_ada, v_ln_g, v_ln_b, v_w_ffn1_in, v_w_ffn1_out, v_w_in, v_b_in, v_w_pool, v_pool_scale, v_sinks, v_w_branch_a, v_w_branch_b, v_w_out, v_w_ffn2_in, v_w_ffn2_out):
    t = x.shape[1]
    xs, tgt = x[0], loss_target[0]
    xi, yi, ci = lax.axis_index("x"), lax.axis_index("y"), lax.axis_index("c")
    chip = 2 * xi + yi
    dev = 2 * chip + ci
    b_in2, ps2, sinks2 = b_in, pool_scale, sinks

    first = jnp.concatenate([c.reshape(-1), ln_g.reshape(-1), ln_b.reshape(-1)]).reshape(-1, 128)
    first_all = allgather_small("gather_cond", first).reshape(8, -1)
    c_all = first_all[:, :D]
    ln_parts = first_all[0::2, D:].reshape(4, 2, 3, D // 4)
    ln_full = jnp.transpose(ln_parts, (1, 2, 0, 3)).reshape(2, 3, D)
    lgs = [ln_full[0, s:s + 1] for s in range(3)]
    lbs = [ln_full[1, s:s + 1] for s in range(3)]
    c16 = jnp.pad(c_all, ((0, 8), (0, 0)))
    b_ada_sh = lax.dynamic_slice(b_ada, (0, chip * ADA_SH), (1, ADA_SH))
    mod_part = ada_fwd(c16, w_ada[0], b_ada_sh)[:8]
    mod_all = allgather_small("gather_mod", mod_part.reshape(-1, 128)).reshape(8, 8, ADA_SH)
    mod_mine = lax.dynamic_index_in_dim(mod_all[0::2], dev, axis=1, keepdims=False).reshape(9, D)
    mods = [[mod_mine[3 * s + k:3 * s + k + 1] for k in range(3)] for s in range(3)]

    plain = [("f1o", w_ffn1_out[0]), ("win", w_in[0]), ("wp", w_pool[0].reshape(4 * 64, PG)), ("wba", w_branch_a[0]),
             ("wbb", w_branch_b[0]), ("wo", w_out[0]), ("f2o", w_ffn2_out[0])]
    sp = jnp.stack([chip, ci]).astype(jnp.int32)
    sh = {n: cast_shard("cast_" + n, w, sp, ffn_out=n in ("f1o", "f2o")) for n, w in plain}
    tr = lambda a: jnp.swapaxes(a[0], 0, 1)
    sh["f1i"] = cast_ffn_in("cast_f1i", tr(w_ffn1_in), sp)
    sh["f2i"] = cast_ffn_in("cast_f2i", tr(w_ffn2_in), sp)
    order = ["f1i", "f1o", "win", "wp", "wba", "wbb", "wo", "f2i", "f2o"]
    views = {n: (view_ffn_out if n in ("f1o", "f2o") else view_lead) for n in order}
    shard_rows = {n: (FO if n in ("f1o", "f2o") else sh[n].shape[1]) for n in order}
    shard_cols = {n: sh[n].shape[2] for n in order}
    tiles = {"f1i": FHP // 8, "f1o": FO // 2, "win": 512, "wp": 128, "wba": 512, "wbb": 512, "wo": 256,
             "f2i": FHP // 8, "f2o": FO // 2}

    def item(n, part=0, parts=1):
        return (sh[n], views[n], shard_rows[n], part, parts)

    tabs = rope_tables(t)
    (sh0, sc0, gt0), (sh1, sc1, gt1), (sh2, sc2, gt2) = mods

    (g_f1i,) = run_job("gather_f1i", gather_job([item("f1i")]))
    u0 = modulate("ffn1_mod", xs, sh0, sc0, t)
    ha1, hb1, g1, y1, f1o, up1, (g_win,) = ffn_fwd(
        "ffn1", u0, g_f1i, t, gather_job([item(n) for n in ("f1o", "wp", "wba", "wbb", "wo")]),
        gather_job([item("win")]))
    x1, z1, u1 = residual_ln_mod("ffn1_ln", xs, y1, gt0, lgs[0], lbs[0], 0.5, sh1, sc1, t)
    _, g_wp, g_wba, g_wbb, g_wo = up1
    wp_full = jnp.transpose(g_wp.reshape(4, 4, 64, PG), (1, 0, 2, 3)).reshape(4, PG, PG)
    wts = (g_win, wp_full, g_wba, g_wbb, g_wo.reshape(D, D))
    y2, sv2, (g_f2i,) = mix_fwd(
        u1, wts, b_in2, ps2, sinks2, tabs, t, gather_job([item("f2i", 0, 2)]),
        lambda moved: gather_job([(moved[0], view_lead, FHP, 1, 2)]))
    x2, z2, u2 = residual_ln_mod("mix_ln", x1, y2, gt1, lgs[1], lbs[1], 1.0, sh2, sc2, t)
    ha3, hb3, g3, y3, f2o, _, _ = ffn_fwd("ffn2", u2, g_f2i, t, gather_job([item("f2o")]))

    dz3, dy3, dlg2, dlb2, dgt2, sq = residual_ln_loss_bwd("ffn2_ln_loss", x2, y3, tgt, gt2, lgs[2], lbs[2], 0.5, t)
    loss = lax.psum(0.5 * sq[0, 0] / D, ("x", "y", "c"))
    du3, red_f2i, red_f2o, _, _ = ffn_bwd("ffn2", u2, ha3, hb3, g3, dy3, g_f2i, f2o, t, sp)
    dz2, dy2, dlg1, dlb1, dgt1, dsh2, dsc2 = residual_ln_bwd("mix_ln_bwd", z2, (dz3, du3, x2, sc2), y2, gt1, lgs[1], 1.0, t)
    du2, mix_parts, sib, db_in, dps, dsinks = mix_bwd(u1, sv2, dy2, wts, b_in2, ps2, sinks2, tabs, t)
    q = {n: chip_sum("chipsum_" + n, mix_parts[n], sib[n], sp, shard_rows[n], tiles[n]) for n in mix_parts}
    dz1, dy1, dlg0, dlb0, dgt0, dsh1, dsc1 = residual_ln_bwd("ffn1_ln_bwd", z1, (dz2, du2, x1, sc1), y1, gt0, lgs[0], 0.5, t)
    du1, red_f1i, red_f1o, far_a, far_b = ffn_bwd(
        "ffn1", u0, ha1, hb1, g1, dy1, g_f1i, f1o, t, sp,
        reduce_chips_job([q["win"], q["wp"]]), reduce_chips_job([q["wo"], q["wba"], q["wbb"]]))
    dx0, dsh0, dsc0 = modulate_bwd("ffn1_mod_bwd", dz1, du1, xs, sc0, t)
    gm0, gm1, gm2 = (dsh0, dsc0, dgt0), (dsh1, dsc1, dgt1), (dsh2, dsc2, dgt2)
    reduced = {"f1i": red_f1i, "f1o": red_f1o, "f2i": red_f2i, "f2o": red_f2o, "win": (q["win"], far_a[0]),
               "wp": (q["wp"], far_a[1]), "wo": (q["wo"], far_b[0]), "wba": (q["wba"], far_b[1]), "wbb": (q["wbb"], far_b[2])}
    halves = [chip_total("total_" + n, *reduced[n], sp, shard_rows[n], tiles[n]) for n in order]

    small = jnp.concatenate([*gm0, *gm1, *gm2, dlg0, dlg1, dlg2, dlb0, dlb1, dlb2, db_in, dps, dsinks], axis=1)
    n_small = small.shape[1]
    rows_small = -(-n_small // 1024) * 8
    small = jnp.pad(small, ((0, 0), (0, rows_small * 128 - n_small))).reshape(rows_small, 128)
    small_all = allgather_small("gather_small", small)
    tot = sum_devices(small_all).reshape(1, -1)
    gmod_all = small_all.reshape(8, -1)[:, :9 * D]
    o = 9 * D
    g_b_ada = tot[:, :o]
    g_ln_g = lax.dynamic_slice(tot[:, o:o + 3 * D].reshape(3, D), (0, chip * (D // 4)), (3, D // 4))
    g_ln_b = lax.dynamic_slice(tot[:, o + 3 * D:o + 6 * D].reshape(3, D), (0, chip * (D // 4)), (3, D // 4))
    o += 6 * D
    g_b_in, g_ps, g_sinks = tot[:, o:o + IN_W], tot[:, o + IN_W:o + IN_W + PW], tot[:, o + IN_W + PW:o + IN_W + PW + N_Q]

    gm16 = jnp.pad(lax.dynamic_slice(gmod_all, (0, chip * ADA_SH), (8, ADA_SH)), ((0, 8), (0, 0)))
    (g_w_ada, d_w_ada, nm_w_ada, nv_w_ada), _ = ada_bwd_adam(c16, gm16, w_ada[0], m_w_ada[0], v_w_ada[0], None)
    gw = dict(zip(order, run_job("share_halves", share_halves_job(halves))))

    def big(n, w, m, v, tm):
        shape = w.shape
        w2, m2, v2 = (a.reshape(shape[-2] if a.ndim == 3 else -1, shape[-1]) for a in (w, m, v))
        return [r.reshape(shape) for r in adam_rows("adam_" + n, w2, gw[n], m2, v2, tm)]

    def big_t(n, w, m, v):
        return [jnp.swapaxes(r, 0, 1)[None] for r in adam_rows("adam_" + n, tr(w), gw[n], tr(m), tr(v), 64)]

    def tiny(n, w, g, m, v):
        return [g.reshape(w.shape)] + list(adam_small("adam_" + n, w, g.reshape(w.shape), m, v))

    res = {
        "w_ada": [a[None] for a in (g_w_ada, d_w_ada, nm_w_ada, nv_w_ada)],
        "b_ada": tiny("b_ada", b_ada, g_b_ada, m_b_ada, v_b_ada),
        "ln_g": tiny("ln_g", ln_g, g_ln_g, m_ln_g, v_ln_g),
        "ln_b": tiny("ln_b", ln_b, g_ln_b, m_ln_b, v_ln_b),
        "w_ffn1_in": big_t("f1i", w_ffn1_in, m_w_ffn1_in, v_w_ffn1_in),
        "w_ffn1_out": big("f1o", w_ffn1_out, m_w_ffn1_out, v_w_ffn1_out, 32),
        "w_in": big("win", w_in, m_w_in, v_w_in, 256),
        "b_in": tiny("b_in", b_in, g_b_in, m_b_in, v_b_in),
        "w_pool": big("wp", w_pool, m_w_pool, v_w_pool, 256),
        "pool_scale": tiny("pool_scale", pool_scale, g_ps, m_pool_scale, v_pool_scale),
        "sinks": tiny("sinks", sinks, g_sinks, m_sinks, v_sinks),
        "w_branch_a": big("wba", w_branch_a, m_w_branch_a, v_w_branch_a, 512),
        "w_branch_b": big("wbb", w_branch_b, m_w_branch_b, v_w_branch_b, 512),
        "w_out": big("wo", w_out, m_w_out, v_w_out, 128),
        "w_ffn2_in": big_t("f2i", w_ffn2_in, m_w_ffn2_in, v_w_ffn2_in),
        "w_ffn2_out": big("f2o", w_ffn2_out, m_w_ffn2_out, v_w_ffn2_out, 32),
    }
    names = ["w_ada", "b_ada", "ln_g", "ln_b", "w_ffn1_in", "w_ffn1_out", "w_in", "b_in", "w_pool", "pool_scale", "sinks",
             "w_branch_a", "w_branch_b", "w_out", "w_ffn2_in", "w_ffn2_out"]
    return (loss, dx0[None], *[res[n][0] for n in names], *[res[n][1] for n in names],
            *[res[n][2] for n in names], *[res[n][3] for n in names])
```

```python
import jax
import jax.numpy as jnp
from jax import lax
from jax.experimental import pallas as pl
from jax.experimental.pallas import tpu as pltpu

F32 = jnp.float32
BF16 = jnp.bfloat16
MESH = pl.DeviceIdType.MESH
ANY = pl.BlockSpec(memory_space=pl.ANY)

D = 2048
N_Q, N_KV, HD = 16, 4, 64
QW, KVW = N_Q * HD, N_KV * HD
BLK = 128
POOL_WINDOWS = (2, 4, 8, 16)
PW, PG = 1024, 256
HALO = 16
ROPE_THETA = 500000.0
ROT = HD // 4
LN_EPS = 1e-5
ALPHA = 2.0 ** 0.25
FH = 2752
FHP = 2816
FO = 1376
IN_W = 6656
IN_SH = IN_W // 4
ADA_SH = 18432 // 4
B1, B2, LR, EPS, WD, STEP = 0.9, 0.999, 0.001, 1e-08, 0.01, 10
VMEM_LIMIT = 56 * 1024 * 1024
FLIPS = ((1, 0), (0, 1), (1, 1))
NN = (((1,), (0,)), ((), ()))
NT = (((1,), (1,)), ((), ()))
TN = (((0,), (0,)), ((), ()))


def _params(sem):
    return pltpu.CompilerParams(dimension_semantics=sem, vmem_limit_bytes=VMEM_LIMIT)


def _aligned(v, m):
    return v if isinstance(v, int) else pl.multiple_of(v, m)


def _sigmoid(v):
    return 1.0 / (1.0 + jnp.exp(-v))


def T_(arr, width=None, off=0):
    return ("t", arr, width, off)


def B_(arr, width=None, off=0):
    return ("b", arr, width, off)


def X_(arr, spec):
    return ("x", arr, spec, 0)


def rowmap(name, fn, ins, outs, accs=(), *, rows, tm, ncol=1, with_ids=False, sp=None, alias=None):
    tm = min(tm, rows)
    nrow = rows // tm
    in_specs, arrs = [], []
    for kind, arr, width, off in ins:
        if kind == "x":
            in_specs.append(width)
        elif kind == "t":
            w = arr.shape[1] if width is None else width
            in_specs.append(pl.BlockSpec((tm, w), lambda j, i, *_, off=off: (i, off + j)))
        else:
            w = arr.shape[1] if width is None else width
            in_specs.append(pl.BlockSpec((arr.shape[0], w), lambda j, i, *_, off=off: (0, off + j)))
        arrs.append(arr)
    out_shape, out_specs = [], []
    for o in outs:
        if len(o) == 3:
            out_shape.append(jax.ShapeDtypeStruct(o[0], o[1]))
            out_specs.append(o[2])
        else:
            out_shape.append(jax.ShapeDtypeStruct((rows, o[0]), o[1]))
            out_specs.append(pl.BlockSpec((tm, o[0] // ncol), lambda j, i, *_: (i, j)))
    for r, width in accs:
        out_shape.append(jax.ShapeDtypeStruct((r, width), F32))
        out_specs.append(pl.BlockSpec((r, width // ncol), lambda j, i, *_: (0, j)))
    ni, no = len(ins), len(outs)
    nsp = 0 if sp is None else 1

    def body(*refs):
        refs = refs[nsp:]
        i = pl.program_id(1)
        vals = [r[...] for r in refs[:ni]]
        res = fn(pl.program_id(0), i, *vals) if with_ids else fn(*vals)
        if not isinstance(res, (tuple, list)):
            res = (res,)
        for r, v in zip(refs[ni:ni + no], res[:no]):
            r[...] = v.astype(r.dtype)
        for r, v in zip(refs[ni + no:], res[no:]):
            @pl.when(i == 0)
            def _(r=r, v=v):
                r[...] = v

            @pl.when(i > 0)
            def _(r=r, v=v):
                r[...] += v

    grid_spec = pltpu.PrefetchScalarGridSpec(num_scalar_prefetch=nsp, grid=(ncol, nrow), in_specs=in_specs,
                                             out_specs=out_specs)
    res = pl.pallas_call(
        body, name=name, grid_spec=grid_spec, out_shape=out_shape,
        input_output_aliases={nsp + k: v for k, v in (alias or {}).items()},
        compiler_params=_params(("arbitrary", "arbitrary")),
    )(*([sp] if nsp else []), *arrs)
    return res[0] if len(res) == 1 else res


def colsum(v):
    return jnp.sum(v, axis=0, keepdims=True)


def mm(name, a_ops, b_ops, ops, *, dims, grid, a_specs, b_specs, outs, out_specs, acc_shapes,
       epilogue=None, extras=(), extra_specs=(), carry=None, job=None, sub_rows=None):
    gk = grid[2]
    na, nb, ne, nacc = len(a_ops), len(b_ops), len(extras), len(acc_shapes)
    nc = 0 if carry is None else 1
    no = len(outs)

    def body(*refs):
        a_refs = refs[:na]
        b_refs = refs[na:na + nb]
        e_refs = refs[na + nb:na + nb + ne]
        o_refs = refs[na + nb + ne + nc:na + nb + ne + nc + no]
        acc_refs = refs[na + nb + ne + nc + no:]
        k = pl.program_id(2)

        def partials(rows=slice(None)):
            res = [None] * nacc
            for ai, bi, ci in ops:
                p = lax.dot_general(a_refs[ai][rows], b_refs[bi][...], dims, preferred_element_type=F32)
                res[ci] = p if res[ci] is None else res[ci] + p
            return res

        def finish(accs, rows=slice(None)):
            outv = epilogue(accs, [e[rows] for e in e_refs]) if epilogue else (accs[0],)
            for o, v in zip(o_refs, outv):
                o[rows] = v.astype(o.dtype)

        if gk == 1 and sub_rows:
            for s in range(out_specs[0].block_shape[-2] // sub_rows):
                rows = pl.ds(s * sub_rows, sub_rows)
                finish(partials(rows), rows)
        elif gk == 1:
            finish(partials())
        else:
            ps = partials()

            @pl.when(k == 0)
            def _():
                for acc, p in zip(acc_refs, ps):
                    acc[...] = p

            @pl.when((k > 0) & (k < gk - 1))
            def _():
                for acc, p in zip(acc_refs, ps):
                    acc[...] += p

            @pl.when(k == gk - 1)
            def _():
                finish([acc[...] + p for acc, p in zip(acc_refs, ps)])

    res, moved = carried_call(
        body, name, grid,
        list(a_specs) + list(b_specs) + list(extra_specs) + ([ANY] if nc else []), list(out_specs), list(outs),
        [pltpu.VMEM(s, F32) for s in acc_shapes] if gk > 1 else [],
        [*a_ops, *b_ops, *extras, *([carry] if nc else [])], {na + nb + ne: 0} if nc else {}, job)
    res = res[0] if len(res) == 1 else res
    return res if job is None else (res, moved)


def sds(shape, dt):
    return jax.ShapeDtypeStruct(shape, dt)


class Job:
    def __init__(self, ins, outs, aliases, scratch, start, mid, finish):
        self.ins, self.outs, self.aliases, self.scratch = list(ins), list(outs), dict(aliases), list(scratch)
        self.start, self.mid, self.finish = start, mid, finish


def carried_call(body, name, grid, in_specs, out_specs, out_shape, scratch, args, aliases, job, mid_at=0.9):
    sem = ("arbitrary",) * len(grid)
    if job is None:
        res = pl.pallas_call(body, name=name, grid=grid, in_specs=in_specs, out_specs=out_specs, out_shape=out_shape,
                             scratch_shapes=scratch, input_output_aliases=aliases, compiler_params=_params(sem))(*args)
        return list(res), []
    ni, no, ns = len(in_specs), len(out_specs), len(scratch)
    ci, co = len(job.ins), len(job.outs)
    total = 1
    for g in grid:
        total *= g
    mid_step = min(max(int(total * mid_at), 1), total - 1)

    def full(*refs):
        ins, cins = refs[:ni], refs[ni:ni + ci]
        outs, couts = refs[ni + ci:ni + ci + no], refs[ni + ci + no:ni + ci + no + co]
        scr, cscr = refs[ni + ci + no + co:ni + ci + no + co + ns], refs[ni + ci + no + co + ns:]
        step = 0
        for d, g in enumerate(grid):
            step = step * g + pl.program_id(d)

        @pl.when(step == 0)
        def _():
            job.start(cins, couts, cscr)

        body(*ins, *outs, *scr)

        @pl.when(step == mid_step)
        def _():
            job.mid(cins, couts, cscr)

        @pl.when(step == total - 1)
        def _():
            job.finish(cins, couts, cscr)

    al = dict(aliases)
    al.update({ni + k: no + v for k, v in job.aliases.items()})
    res = pl.pallas_call(
        full, name=name, grid=grid, in_specs=in_specs + [ANY] * ci, out_specs=out_specs + [ANY] * co,
        out_shape=out_shape + job.outs, scratch_shapes=scratch + job.scratch, input_output_aliases=al,
        compiler_params=_params(sem))(*args, *job.ins)
    return list(res[:no]), list(res[no:])


def _with_moved(res, job):
    return res if job is not None else (res, [])


def run_job(name, job):
    ci = len(job.ins)

    def body(*refs):
        cins, couts, cscr = refs[:ci], refs[ci:ci + len(job.outs)], refs[ci + len(job.outs):]
        job.start(cins, couts, cscr)
        job.mid(cins, couts, cscr)
        job.finish(cins, couts, cscr)

    return list(pl.pallas_call(
        body, name=name, in_specs=[ANY] * ci, out_specs=[ANY] * len(job.outs), out_shape=job.outs,
        scratch_shapes=job.scratch, input_output_aliases=job.aliases)(*job.ins))


def _place():
    x, y, c = lax.axis_index("x"), lax.axis_index("y"), lax.axis_index("c")
    chips = [((1 - x) if fx else x, (1 - y) if fy else y) for fx, fy in FLIPS]
    return x, y, c, chips


def allgather_small(name, v):
    r = v.shape[0]

    def body(x_ref, out_ref, send_sems, recv_sems, local_sem):
        x, y, c, chips = _place()
        me, sibling = (x, y, c), (x, y, 1 - c)

        def rows(px, py, pc):
            return out_ref.at[4 * px + 2 * py + pc]

        def copy(k, block, to, src=None):
            return pltpu.make_async_remote_copy(
                src_ref=rows(*block) if src is None else src, dst_ref=rows(*block),
                send_sem=send_sems.at[k], recv_sem=recv_sems.at[k], device_id=to, device_id_type=MESH)

        mine = pltpu.make_async_copy(x_ref, rows(*me), local_sem)
        mine.start()
        first = [copy(0, me, sibling, src=x_ref)]
        first += [copy(1 + j, me, (*chip, c), src=x_ref) for j, chip in enumerate(chips)]
        for cp in first:
            cp.start()
        passed = [copy(4 + j, (*chip, c), sibling) for j, chip in enumerate(chips)]
        for j, chip in enumerate(chips):
            copy(1 + j, (*chip, c), me).wait_recv()
            passed[j].start()
        copy(0, sibling, me).wait_recv()
        for j, chip in enumerate(chips):
            copy(4 + j, (*chip, 1 - c), me).wait_recv()
        for cp in first + passed:
            cp.wait_send()
        mine.wait()

    return pl.pallas_call(
        body, name=name, out_shape=sds((8, r, 128), v.dtype),
        in_specs=[pl.BlockSpec(memory_space=pltpu.VMEM)], out_specs=pl.BlockSpec(memory_space=pltpu.VMEM),
        scratch_shapes=[pltpu.SemaphoreType.DMA((7,)), pltpu.SemaphoreType.DMA((7,)), pltpu.SemaphoreType.DMA],
    )(v)


def _half(ref, rows, hf):
    hr = rows // 2
    return ref.at[pl.ds(_aligned(hf * hr, 16), hr)]


def view_lead(ref, p):
    return ref.at[p]


def view_ffn_out(ref, p):
    return ref.at[p // 2, pl.ds(_aligned((p % 2) * FO, 16), FO)]


def _remote(ref, dst, send_sems, recv_sems, idx, to):
    return pltpu.make_async_remote_copy(src_ref=ref, dst_ref=dst, send_sem=send_sems.at[idx], recv_sem=recv_sems.at[idx],
                                        device_id=to, device_id_type=MESH)


def gather_job(items):
    nw = len(items)
    pads = [w for w, it in enumerate(items) if it[1] is view_ffn_out]

    def piece(ref, w, p, hf):
        _, view, rws, part, parts = items[w]
        pr = rws // 2 // parts
        return view(ref, p).at[pl.ds(_aligned(hf * (rws // 2) + part * pr, 16), pr)]

    def pad_copies(outs, scr):
        return [pltpu.make_async_copy(scr[2], outs[w].at[h, pl.ds(2 * FO, FHP - 2 * FO)], scr[3].at[2 * n + h])
                for n, w in enumerate(pads) for h in range(2)]

    def start(_, outs, scr):
        x, y, c, chips = _place()
        if pads:
            scr[2][...] = jnp.zeros_like(scr[2])
            for cp in pad_copies(outs, scr):
                cp.start()
        for w in range(nw):
            mine = piece(outs[w], w, 2 * x + y, c)
            for f, (px, py) in enumerate(chips):
                _remote(mine, mine, scr[0], scr[1], (w, f), (px, py, c)).start()

    def mid(_, outs, scr):
        x, y, c, chips = _place()
        for w in range(nw):
            for f, (px, py) in enumerate(chips):
                land = piece(outs[w], w, 2 * px + py, c)
                _remote(land, land, scr[0], scr[1], (w, f), (px, py, c)).wait_recv()
                _remote(land, land, scr[0], scr[1], (w, 3 + f), (x, y, 1 - c)).start()

    def finish(_, outs, scr):
        x, y, c, chips = _place()
        for w in range(nw):
            for f, (px, py) in enumerate(chips):
                land = piece(outs[w], w, 2 * px + py, 1 - c)
                _remote(land, land, scr[0], scr[1], (w, 3 + f), (x, y, 1 - c)).wait_recv()
        for w in range(nw):
            mine = piece(outs[w], w, 2 * x + y, c)
            for f in range(6):
                _remote(mine, mine, scr[0], scr[1], (w, f), (x, y, 1 - c)).wait_send()
        for cp in pad_copies(outs, scr):
            cp.wait()

    scratch = [pltpu.SemaphoreType.DMA((nw, 6)), pltpu.SemaphoreType.DMA((nw, 6))]
    if pads:
        scratch += [pltpu.VMEM((FHP - 2 * FO, D), BF16), pltpu.SemaphoreType.DMA((2 * len(pads),))]
    bufs = [it[0] for it in items]
    return Job(bufs, [sds(b.shape, BF16) for b in bufs], {w: w for w in range(nw)}, scratch, start, mid, finish)


HBM = pl.BlockSpec(memory_space=pltpu.HBM)
SEM = pl.BlockSpec(memory_space=pltpu.SEMAPHORE)
SPLIT = pltpu.CompilerParams(has_side_effects=pltpu.SideEffectType.DATAFLOW_SIDE_EFFECTING)


def gather_start(name, buf, rows):
    def body(_, out, send_sems, recv_sems, token):
        x, y, c, chips = _place()
        mine = _half(out.at[2 * x + y], rows, c)
        for f, (px, py) in enumerate(chips):
            _remote(mine, mine, send_sems, recv_sems, f, (px, py, c)).start()
        token[...] = jnp.zeros_like(token)

    return pl.pallas_call(
        body, name=name,
        out_shape=(pltpu.HBM(buf.shape, buf.dtype), pltpu.SemaphoreType.DMA((3,)), pltpu.SemaphoreType.DMA((3,)),
                   sds((8, 128), F32)),
        in_specs=(HBM,), out_specs=(HBM, SEM, SEM, pl.BlockSpec(memory_space=pltpu.VMEM)),
        input_output_aliases={0: 0}, compiler_params=SPLIT)(pltpu.with_memory_space_constraint(buf, pltpu.HBM))


def gather_wait(name, buf, send_sems, recv_sems, rows, after):
    def body(_, send_sems, recv_sems, *rest):
        out = rest[-1]
        x, y, c, chips = _place()
        mine = _half(out.at[2 * x + y], rows, c)
        for f, (px, py) in enumerate(chips):
            cp = _remote(mine, _half(out.at[2 * px + py], rows, c), send_sems, recv_sems, f, (px, py, c))
            cp.wait_send()
            cp.wait_recv()

    return pl.pallas_call(
        body, name=name, out_shape=pltpu.HBM(buf.shape, buf.dtype),
        in_specs=(HBM, SEM, SEM) + (ANY,) * len(after), out_specs=HBM, input_output_aliases={0: 0},
        compiler_params=SPLIT)(buf, send_sems, recv_sems, *after)


def forward_job(buf, rows):
    def copies(outs, scr, hf):
        x, y, c, chips = _place()
        half = c if hf == 0 else 1 - c
        return [_remote(_half(outs[0].at[2 * px + py], rows, half), _half(outs[0].at[2 * px + py], rows, half),
                        scr[0], scr[1], f, (x, y, 1 - c)) for f, (px, py) in enumerate(chips)]

    def start(_, outs, scr):
        for cp in copies(outs, scr, 0):
            cp.start()

    def finish(_, outs, scr):
        for cp in copies(outs, scr, 1):
            cp.wait_recv()
        for cp in copies(outs, scr, 0):
            cp.wait_send()

    return Job([buf], [sds(buf.shape, buf.dtype)], {0: 0},
               [pltpu.SemaphoreType.DMA((3,)), pltpu.SemaphoreType.DMA((3,))], start, lambda *_: None, finish)


def reduce_sibling_job(items):
    nw = len(items)

    def copies(ins, got, scr):
        x, y, c, _ = _place()
        return [_remote(_half(view(ins[w], p), rws, 1 - c), got[w].at[p], scr[0], scr[1], (w, p), (x, y, 1 - c))
                for w, (_, view, rws, _) in enumerate(items) for p in range(4)]

    def start(ins, got, scr):
        for cp in copies(ins, got, scr):
            cp.start()

    def finish(ins, got, scr):
        for cp in copies(ins, got, scr):
            cp.wait()

    return Job([it[0] for it in items], [sds((4, it[2] // 2, it[3]), BF16) for it in items], {},
               [pltpu.SemaphoreType.DMA((nw, 4)), pltpu.SemaphoreType.DMA((nw, 4))], start, lambda *_: None, finish)


def reduce_chips_job(qs):
    nw = len(qs)

    def copies(ins, got, scr):
        x, y, c, chips = _place()
        return [_remote(ins[w].at[2 * px + py], got[w].at[f], scr[0], scr[1], (w, f), (px, py, c))
                for w in range(nw) for f, (px, py) in enumerate(chips)]

    def start(ins, got, scr):
        for cp in copies(ins, got, scr):
            cp.start()

    def finish(ins, got, scr):
        for cp in copies(ins, got, scr):
            cp.wait()

    return Job(qs, [sds((3,) + q.shape[1:], BF16) for q in qs], {},
               [pltpu.SemaphoreType.DMA((nw, 3)), pltpu.SemaphoreType.DMA((nw, 3))], start, lambda *_: None, finish)


def share_halves_job(gs):
    nw = len(gs)

    def start(_, outs, scr):
        x, y, c, _ = _place()
        for w in range(nw):
            mine = _half(outs[w], gs[w].shape[0], c)
            _remote(mine, mine, scr[0], scr[1], w, (x, y, 1 - c)).start()

    def finish(_, outs, scr):
        x, y, c, _ = _place()
        for w in range(nw):
            mine = _half(outs[w], gs[w].shape[0], c)
            theirs = _half(outs[w], gs[w].shape[0], 1 - c)
            _remote(mine, mine, scr[0], scr[1], w, (x, y, 1 - c)).wait_send()
            _remote(theirs, theirs, scr[0], scr[1], w, (x, y, 1 - c)).wait_recv()

    return Job(gs, [sds(g.shape, F32) for g in gs], {w: w for w in range(nw)},
               [pltpu.SemaphoreType.DMA((nw,)), pltpu.SemaphoreType.DMA((nw,))], start, lambda *_: None, finish)


def rope_tables(t):
    pos = jnp.arange(t, dtype=F32)
    inv_freq = ROPE_THETA ** (-jnp.arange(0, ROT, 2, dtype=F32) / ROT)
    ang = pos[:, None] * inv_freq[None, :]
    cos, sin = jnp.cos(ang), jnp.sin(ang)
    d = jnp.arange(128) % HD
    half = ROT // 2
    cs = jnp.take(cos, d % half, axis=1)
    sn = jnp.take(sin, d % half, axis=1)
    cc = jnp.where(d[None] < ROT, cs, 1.0)
    sa = jnp.where(d[None] < half, -sn, 0.0)
    sb = jnp.where((d[None] >= half) & (d[None] < ROT), sn, 0.0)
    return cc, sa, sb


def _rope(v, cc, sa, sb):
    w = v.shape[1]
    reps = w // 128
    half = ROT // 2
    return (v * jnp.tile(cc, (1, reps)) + pltpu.roll(v, w - half, 1) * jnp.tile(sa, (1, reps))
            + pltpu.roll(v, half, 1) * jnp.tile(sb, (1, reps)))


def _rope_t(dv, cc, sa, sb):
    w = dv.shape[1]
    reps = w // 128
    half = ROT // 2
    return (dv * jnp.tile(cc, (1, reps)) + pltpu.roll(dv * jnp.tile(sa, (1, reps)), half, 1)
            + pltpu.roll(dv * jnp.tile(sb, (1, reps)), w - half, 1))


def pool_fwd(h, b_in, t, tm):
    tm = min(tm, t)
    per = tm // HALO

    def body(prev_ref, cur_ref, b_ref, o_ref, xx):
        i = pl.program_id(0)
        b = b_ref[...]
        xx[pl.ds(0, HALO), :] = jnp.where(i > 0, prev_ref[...] + b, 0.0)
        xx[pl.ds(HALO, tm), :] = cur_ref[...] + b
        tpos = i * tm + lax.broadcasted_iota(jnp.int32, (tm, PG), 0) + 1
        for gi, w in enumerate(POOL_WINDOWS):
            cols = pl.ds(gi * PG, PG)
            acc = xx[pl.ds(HALO, tm), cols]
            for s in range(1, w):
                acc = acc + xx[pl.ds(HALO - s, tm), cols]
            cnt = jnp.minimum(tpos, w).astype(F32)
            o_ref[:, cols] = (acc / cnt - xx[pl.ds(HALO, tm), cols]).astype(o_ref.dtype)

    return pl.pallas_call(
        body, name="pool_fwd", grid=(t // tm,),
        in_specs=[pl.BlockSpec((HALO, PW), lambda i: (jnp.maximum(i * per - 1, 0), 0)),
                  pl.BlockSpec((tm, PW), lambda i: (i, 0)), pl.BlockSpec((1, PW), lambda i: (0, 0))],
        out_specs=pl.BlockSpec((tm, PW), lambda i: (i, 0)), out_shape=sds((t, PW), BF16),
        scratch_shapes=[pltpu.VMEM((tm + HALO, PW), F32)], compiler_params=_params(("arbitrary",)),
    )(h, h, b_in)


def pool_bwd(dpooled, t, tm):
    tm = min(tm, t)
    per = tm // HALO
    nt = t // tm

    def body(cur_ref, nxt_ref, o_ref, db_ref, ee):
        i = pl.program_id(0)
        tpos = i * tm + lax.broadcasted_iota(jnp.int32, (tm, PG), 0) + 1
        for gi, w in enumerate(POOL_WINDOWS):
            cols = pl.ds(gi * PG, PG)
            ee[pl.ds(0, tm), cols] = cur_ref[:, cols] / jnp.minimum(tpos, w).astype(F32)
            ee[pl.ds(tm, HALO), cols] = jnp.where(i < nt - 1, nxt_ref[:, cols] / float(w), 0.0)
        for gi, w in enumerate(POOL_WINDOWS):
            cols = pl.ds(gi * PG, PG)
            acc = ee[pl.ds(0, tm), cols]
            for s in range(1, w):
                acc = acc + ee[pl.ds(s, tm), cols]
            dxp = acc - cur_ref[:, cols]
            o_ref[:, cols] = dxp.astype(o_ref.dtype)
            part = colsum(dxp)

            @pl.when(i == 0)
            def _(cols=cols, part=part):
                db_ref[:, cols] = part

            @pl.when(i > 0)
            def _(cols=cols, part=part):
                db_ref[:, cols] += part

    return pl.pallas_call(
        body, name="pool_bwd", grid=(nt,),
        in_specs=[pl.BlockSpec((tm, PW), lambda i: (i, 0)),
                  pl.BlockSpec((HALO, PW), lambda i: (jnp.minimum((i + 1) * per, t // HALO - 1), 0))],
        out_specs=[pl.BlockSpec((tm, PW), lambda i: (i, 0)), pl.BlockSpec((1, PW), lambda i: (0, 0))],
        out_shape=[sds((t, PW), BF16), sds((1, PW), F32)],
        scratch_shapes=[pltpu.VMEM((tm + HALO, PW), F32)], compiler_params=_params(("arbitrary",)),
    )(dpooled, dpooled)


def _scores(qh, kp, kc, mask_p, mask_c, sink):
    sp = jnp.where(mask_p, lax.dot_general(qh, kp, NT, preferred_element_type=F32), -1e30)
    sc = jnp.where(mask_c, lax.dot_general(qh, kc, NT, preferred_element_type=F32), -1e30)
    m = jnp.maximum(jnp.maximum(jnp.max(sp, axis=-1, keepdims=True), jnp.max(sc, axis=-1, keepdims=True)), sink)
    pp, pc = jnp.exp(sp - m), jnp.exp(sc - m)
    es = jnp.exp(sink - m)
    inv = 1.0 / (jnp.sum(pp, axis=-1, keepdims=True) + jnp.sum(pc, axis=-1, keepdims=True) + es)
    return pp * inv, pc * inv, es * inv


GRP = N_Q // N_KV


def _masks(n):
    qi = lax.broadcasted_iota(jnp.int32, (GRP * BLK, BLK), 0) % BLK
    kj = lax.broadcasted_iota(jnp.int32, (GRP * BLK, BLK), 1)
    return (kj > qi) & (n > 0), kj <= qi


def _head(hk, g):
    return pl.ds(HD * (GRP * hk + g), HD)


def _stack_heads(ref, hk):
    return jnp.concatenate([ref[:, _head(hk, g)] for g in range(GRP)], axis=0)


def _stack_sinks(s_ref, hk):
    return jnp.concatenate([jnp.full((BLK, 1), s_ref[0, GRP * hk + g], F32) for g in range(GRP)], axis=0)


def attn_fwd(q, k, v, sinks, t, job=None):
    def body(s_ref, q_ref, kp_ref, kc_ref, vp_ref, vc_ref, o_ref):
        n = pl.program_id(0)
        mask_p, mask_c = _masks(n)
        for hk in range(N_KV):
            kv = pl.ds(HD * hk, HD)
            pp, pc, _ = _scores(_stack_heads(q_ref, hk), kp_ref[:, kv], kc_ref[:, kv], mask_p, mask_c,
                                _stack_sinks(s_ref, hk))
            o = (lax.dot_general(pp.astype(BF16), vp_ref[:, kv], NN, preferred_element_type=F32)
                 + lax.dot_general(pc.astype(BF16), vc_ref[:, kv], NN, preferred_element_type=F32))
            for g in range(GRP):
                o_ref[:, _head(hk, g)] = o[g * BLK:(g + 1) * BLK].astype(o_ref.dtype)

    prev = lambda n: (jnp.maximum(n - 1, 0), 0)
    cur = lambda n: (n, 0)
    res, moved = carried_call(
        body, "attn_fwd", (t // BLK,),
        [pl.BlockSpec(memory_space=pltpu.SMEM), pl.BlockSpec((BLK, QW), cur),
         pl.BlockSpec((BLK, KVW), prev), pl.BlockSpec((BLK, KVW), cur),
         pl.BlockSpec((BLK, KVW), prev), pl.BlockSpec((BLK, KVW), cur)],
        [pl.BlockSpec((BLK, QW), cur)], [sds((t, QW), BF16)], [], [sinks, q, k, k, v, v], {}, job)
    return res[0], moved


def attn_bwd(q, k, v, do, sinks, t):
    nb = t // BLK

    def body(s_ref, q_ref, do_ref, kp_ref, kc_ref, vp_ref, vc_ref, dq_ref, dk_ref, dv_ref, ds_ref, dkc, dvc):
        n = pl.program_id(0)

        @pl.when(n == 0)
        def _():
            dkc[...] = jnp.zeros_like(dkc)
            dvc[...] = jnp.zeros_like(dvc)
            ds_ref[...] = jnp.zeros_like(ds_ref)

        @pl.when(n < nb)
        def _():
            mask_p, mask_c = _masks(n)
            lane = lax.broadcasted_iota(jnp.int32, (1, 128), 1)
            dsink = jnp.zeros((1, 128), F32)
            for hk in range(N_KV):
                kv = pl.ds(HD * hk, HD)
                kp, kc, vp, vc = kp_ref[:, kv], kc_ref[:, kv], vp_ref[:, kv], vc_ref[:, kv]
                qs, dos = _stack_heads(q_ref, hk), _stack_heads(do_ref, hk)
                pp, pc, ps = _scores(qs, kp, kc, mask_p, mask_c, _stack_sinks(s_ref, hk))
                dpp = lax.dot_general(dos, vp, NT, preferred_element_type=F32)
                dpc = lax.dot_general(dos, vc, NT, preferred_element_type=F32)
                delta = jnp.sum(pp * dpp, axis=-1, keepdims=True) + jnp.sum(pc * dpc, axis=-1, keepdims=True)
                dsp = (pp * (dpp - delta)).astype(BF16)
                dsc = (pc * (dpc - delta)).astype(BF16)
                sd = ps * delta
                dq = (lax.dot_general(dsp, kp, NN, preferred_element_type=F32)
                      + lax.dot_general(dsc, kc, NN, preferred_element_type=F32))
                for g in range(GRP):
                    rows = slice(g * BLK, (g + 1) * BLK)
                    dsink = dsink + jnp.where(lane == GRP * hk + g, -jnp.sum(sd[rows]), 0.0)
                    dq_ref[:, _head(hk, g)] = dq[rows]
                dk_ref[:, kv] = dkc[:, kv] + lax.dot_general(dsp, qs, TN, preferred_element_type=F32)
                dv_ref[:, kv] = dvc[:, kv] + lax.dot_general(pp.astype(BF16), dos, TN, preferred_element_type=F32)
                dkc[:, kv] = lax.dot_general(dsc, qs, TN, preferred_element_type=F32)
                dvc[:, kv] = lax.dot_general(pc.astype(BF16), dos, TN, preferred_element_type=F32)
            ds_ref[...] += dsink

        @pl.when(n == nb)
        def _():
            dk_ref[...] = dkc[...]
            dv_ref[...] = dvc[...]

    cur = lambda n: (jnp.minimum(n, nb - 1), 0)
    prev = lambda n: (jnp.clip(n - 1, 0, nb - 1), 0)
    return pl.pallas_call(
        body, name="attn_bwd", grid=(nb + 1,),
        in_specs=[pl.BlockSpec(memory_space=pltpu.SMEM), pl.BlockSpec((BLK, QW), cur), pl.BlockSpec((BLK, QW), cur),
                  pl.BlockSpec((BLK, KVW), prev), pl.BlockSpec((BLK, KVW), cur),
                  pl.BlockSpec((BLK, KVW), prev), pl.BlockSpec((BLK, KVW), cur)],
        out_specs=[pl.BlockSpec((BLK, QW), cur), pl.BlockSpec((BLK, KVW), prev), pl.BlockSpec((BLK, KVW), prev),
                   pl.BlockSpec((1, 128), lambda n: (0, 0))],
        out_shape=[sds((t, QW), F32), sds((t, KVW), F32), sds((t, KVW), F32), sds((1, 128), F32)],
        scratch_shapes=[pltpu.VMEM((BLK, KVW), F32), pltpu.VMEM((BLK, KVW), F32)],
        compiler_params=_params(("arbitrary",)),
    )(sinks, q, do, k, k, v, v)


def _adamw(w, g, m, v):
    m2 = B1 * m + (1.0 - B1) * g
    v2 = B2 * v + (1.0 - B2) * jnp.square(g)
    m_hat = m2 / (1.0 - B1 ** STEP)
    v_hat = v2 / (1.0 - B2 ** STEP)
    return -LR * (m_hat / (jnp.sqrt(v_hat) + EPS) + WD * w), m2, v2


def ada_fwd(c16, w_ada, b_sh):
    tn = 512

    def body(c_ref, w_ref, b_ref, o_ref):
        cv = c_ref[...]
        sc = (cv * _sigmoid(cv)).astype(BF16)
        o_ref[...] = lax.dot_general(sc, w_ref[...].astype(BF16), NN, preferred_element_type=F32) + b_ref[...]

    return pl.pallas_call(
        body, name="ada_fwd", grid=(ADA_SH // tn,),
        in_specs=[pl.BlockSpec((16, D), lambda j: (0, 0)), pl.BlockSpec((D, tn), lambda j: (0, j)),
                  pl.BlockSpec((1, tn), lambda j: (0, j))],
        out_specs=pl.BlockSpec((16, tn), lambda j: (0, j)), out_shape=sds((16, ADA_SH), F32),
        compiler_params=_params(("arbitrary",)),
    )(c16, w_ada, b_sh)


def ada_bwd_adam(c16, gm16, w, m, v, job):
    tm, tn = 256, 512

    def body(c_ref, g_ref, w_ref, m_ref, v_ref, go_ref, d_ref, mo_ref, vo_ref):
        cv = c_ref[...]
        sc = (cv * _sigmoid(cv)).astype(BF16)
        g = lax.dot_general(sc, g_ref[...].astype(BF16), TN, preferred_element_type=F32)
        dl, m2, v2 = _adamw(w_ref[...], g, m_ref[...], v_ref[...])
        go_ref[...] = g
        d_ref[...] = dl
        mo_ref[...] = m2
        vo_ref[...] = v2

    blk = pl.BlockSpec((tm, tn), lambda i, j: (i, j))
    return carried_call(
        body, "ada_bwd_adam", (D // tm, ADA_SH // tn),
        [pl.BlockSpec((16, tm), lambda i, j: (0, i)), pl.BlockSpec((16, tn), lambda i, j: (0, j)), blk, blk, blk],
        [blk] * 4, [sds((D, ADA_SH), F32)] * 4, [], [c16, gm16, w, m, v], {}, job)


def adam_rows(name, w, g, m, v, tm):
    rows, cols = w.shape

    def fn(wv, gv, mv, vv):
        gv = gv[:, :cols]
        dl, m2, v2 = _adamw(wv, gv, mv, vv)
        return gv, dl, m2, v2

    return rowmap(name, fn, [T_(w), T_(g), T_(m), T_(v)], [(cols, F32)] * 4, rows=rows, tm=tm)


def adam_small(name, w, g, m, v):
    def body(w_ref, g_ref, m_ref, v_ref, d_ref, mo_ref, vo_ref):
        dl, m2, v2 = _adamw(w_ref[...], g_ref[...], m_ref[...], v_ref[...])
        d_ref[...] = dl
        mo_ref[...] = m2
        vo_ref[...] = v2

    return pl.pallas_call(body, name=name, out_shape=[sds(w.shape, F32)] * 3)(w, g, m, v)


def sum_devices(allv):
    def body(a_ref, o_ref):
        acc = a_ref[0]
        for d in range(1, 8):
            acc = acc + a_ref[d]
        o_ref[...] = acc

    return pl.pallas_call(body, name="sum_devices", out_shape=sds(allv.shape[1:], F32))(allv)


def _ln_fwd(z, g, b):
    mu = jnp.mean(z, axis=-1, keepdims=True)
    zc = z - mu
    var = jnp.mean(jnp.square(zc), axis=-1, keepdims=True)
    return zc * lax.rsqrt(var + LN_EPS) * g + b


def _ln_bwd(z, g, dout):
    mu = jnp.mean(z, axis=-1, keepdims=True)
    zc = z - mu
    var = jnp.mean(jnp.square(zc), axis=-1, keepdims=True)
    rstd = lax.rsqrt(var + LN_EPS)
    xh = zc * rstd
    dxh = dout * g
    dz = rstd * (dxh - jnp.mean(dxh, axis=-1, keepdims=True) - xh * jnp.mean(dxh * xh, axis=-1, keepdims=True))
    return dz, colsum(dout * xh), colsum(dout)


def modulate(name, xin, shift, scale, t):
    return rowmap(name, lambda xv, sh, sc: xv * (1.0 + sc) + sh, [T_(xin), B_(shift), B_(scale)], [(D, BF16)],
                  rows=t, tm=512)


def residual_ln_mod(name, xin, y, gate, lg, lb, wgt, shift_n, scale_n, t):
    def fn(xv, yv, gt, g, b, sh, sc):
        z = ALPHA * xv + (wgt * (1.0 + gt)) * yv
        xo = _ln_fwd(z, g, b)
        return xo, z, xo * (1.0 + sc) + sh

    return rowmap(name, fn, [T_(xin), T_(y), B_(gate), B_(lg), B_(lb), B_(shift_n), B_(scale_n)],
                  [(D, F32), (D, F32), (D, BF16)], rows=t, tm=256)


def residual_ln_bwd(name, z, dnext, y, gate, lg, wgt, t):
    dzn, dun, xn, scn = dnext

    def fn(zv, yv, gt, g, dzv, duv, xv, sc):
        dv = ALPHA * dzv + duv * (1.0 + sc)
        dz, dg, db = _ln_bwd(zv, g, dv)
        return dz, (wgt * (1.0 + gt)) * dz, dg, db, colsum(wgt * dz * yv), colsum(duv), colsum(duv * xv)

    return rowmap(name, fn, [T_(z), T_(y), B_(gate), B_(lg), T_(dzn), T_(dun), T_(xn), B_(scn)],
                  [(D, F32), (D, BF16)], [(1, D)] * 5, rows=t, tm=256)


def residual_ln_loss_bwd(name, xin, y, tgt, gate, lg, lb, wgt, t):
    def fn(xv, yv, tv, gt, g, b):
        z = ALPHA * xv + (wgt * (1.0 + gt)) * yv
        d = _ln_fwd(z, g, b) - tv
        dz, dg, db = _ln_bwd(z, g, d * (1.0 / D))
        return dz, (wgt * (1.0 + gt)) * dz, dg, db, colsum(wgt * dz * yv), jnp.sum(d * d).reshape(1, 1)

    dz, dy, dlg, dlb, dgate, sq = rowmap(
        name, fn, [T_(xin), T_(y), T_(tgt), B_(gate), B_(lg), B_(lb)], [(D, F32), (D, BF16)],
        [(1, D), (1, D), (1, D), (1, 1)], rows=t, tm=256)
    return dz, dy, dlg, dlb, dgate, sq


def modulate_bwd(name, dz, du, xin, scale, t):
    def fn(dzv, duv, xv, sc):
        return ALPHA * dzv + duv * (1.0 + sc), colsum(duv), colsum(duv * xv)

    return rowmap(name, fn, [T_(dz), T_(du), T_(xin), B_(scale)], [(D, F32)], [(1, D), (1, D)], rows=t, tm=256)


def ffn_fwd(tag, u, wi, t, up_job, down_job=None):
    tm = min(1024, t)
    tn = 256
    per = FHP // tn

    def act(accs, _):
        a, b = accs
        s = _sigmoid(a)
        sl = a * s
        return b * (s * (1.0 + a * (1.0 - s))), sl, sl * b

    tmu = min(2048, t)
    hblk = pl.BlockSpec((tmu, tn), lambda i, j, k: (i, j))
    (ha, hb, g), up_moved = mm(
        tag + "_up", [u], [wi, wi], [(0, 0, 0), (0, 1, 1)], dims=NT, grid=(t // tmu, 2 * per, 1),
        a_specs=[pl.BlockSpec((tmu, D), lambda i, j, k: (i, 0))],
        b_specs=[pl.BlockSpec((None, tn, D), lambda i, j, k: (j // per, j % per, 0)),
                 pl.BlockSpec((None, tn, D), lambda i, j, k: (2 + j // per, j % per, 0))],
        outs=[sds((t, 2 * FHP), BF16)] * 3, out_specs=[hblk] * 3, acc_shapes=[(tmu, tn)] * 2, epilogue=act, job=up_job,
        sub_rows=tmu // 2)
    wo = up_moved[0].reshape(2 * FHP, D)
    tk = FHP
    y, down_moved = _with_moved(mm(
        tag + "_down", [g], [wo], [(0, 0, 0)], dims=NN, grid=(t // tm, 2, 2),
        a_specs=[pl.BlockSpec((tm, tk), lambda i, j, k: (i, k))],
        b_specs=[pl.BlockSpec((tk, D // 2), lambda i, j, k: (k, j))],
        outs=[sds((t, D), F32)], out_specs=[pl.BlockSpec((tm, D // 2), lambda i, j, k: (i, j))],
        acc_shapes=[(tm, D // 2)], job=down_job), down_job)
    return ha, hb, g, y, wo, up_moved, down_moved


def ffn_bwd(tag, u, ha, hb, g, dy, wi, wo, t, sp, dact_job=None, dwo_job=None):
    tm = min(1024, t)

    def dact(accs, ex):
        dg = accs[0]
        return dg * ex[0].astype(F32), dg * ex[1].astype(F32)

    tn = 256
    tmu = min(2048, t)
    hblk = pl.BlockSpec((tmu, tn), lambda i, j, k: (i, j))
    (dha, dhb), dact_moved = _with_moved(mm(
        tag + "_dact", [dy], [wo], [(0, 0, 0)], dims=NT, grid=(t // tmu, 2 * FHP // tn, 1),
        a_specs=[pl.BlockSpec((tmu, D), lambda i, j, k: (i, 0))],
        b_specs=[pl.BlockSpec((tn, D), lambda i, j, k: (j, 0))],
        outs=[sds((t, 2 * FHP), BF16)] * 2, out_specs=[hblk] * 2, acc_shapes=[(tmu, tn)],
        epilogue=dact, extras=[ha, hb], extra_specs=[hblk] * 2, job=dact_job, sub_rows=tmu // 2), dact_job)
    tk = min(2048, t)
    th = FHP // 2
    dwo, dwo_moved = _with_moved(mm(
        tag + "_dwo", [g], [dy], [(0, 0, 0)], dims=TN, grid=(4, 2, t // tk),
        a_specs=[pl.BlockSpec((tk, th), lambda i, j, k: (k, i))],
        b_specs=[pl.BlockSpec((tk, D // 2), lambda i, j, k: (k, j))],
        outs=[sds((2 * FHP, D), BF16)], out_specs=[pl.BlockSpec((th, D // 2), lambda i, j, k: (i, j))],
        acc_shapes=[(th, D // 2)], job=dwo_job), dwo_job)
    dwo = dwo.reshape(2, FHP, D)

    def dwi_part(part, dh, carry, job):
        return mm(
            f"{tag}_dwi{part}", [dh], [u], [(0, 0, 0)], dims=TN, grid=(4, 2, t // tk),
            a_specs=[pl.BlockSpec((tk, th), lambda i, j, k: (k, i))],
            b_specs=[pl.BlockSpec((tk, D // 2), lambda i, j, k: (k, j))],
            outs=[sds((4, FHP, D), BF16)],
            out_specs=[pl.BlockSpec((None, th, D // 2), lambda i, j, k: (2 * part + i // 2, i % 2, j))],
            acc_shapes=[(th, D // 2)], carry=carry, job=job)

    dwi, (sib_fo,) = dwi_part(0, dha, None, reduce_sibling_job([(dwo, view_ffn_out, FO, D)]))
    q_fo = chip_sum(tag + "_chipsum_fo", dwo, sib_fo, sp, FO, FO // 2, ffn_out=True)
    dwi, (far_fo,) = dwi_part(1, dhb, dwi, reduce_chips_job([q_fo]))
    (sib_fi,) = run_job(tag + "_sibling_fi", reduce_sibling_job([(dwi, view_lead, FHP, D)]))
    q_fi = chip_sum(tag + "_chipsum_fi", dwi, sib_fi, sp, FHP, FHP // 8)
    tmd = min(512, t)
    du, (far_fi,) = mm(
        tag + "_du", [dha, dhb], [wi, wi], [(0, 0, 0), (1, 1, 0)], dims=NN, grid=(t // tmd, 2, 2),
        a_specs=[pl.BlockSpec((tmd, FHP), lambda i, j, k: (i, k))] * 2,
        b_specs=[pl.BlockSpec((None, FHP, D // 2), lambda i, j, k: (k, 0, j)),
                 pl.BlockSpec((None, FHP, D // 2), lambda i, j, k: (2 + k, 0, j))],
        outs=[sds((t, D), F32)], out_specs=[pl.BlockSpec((tmd, D // 2), lambda i, j, k: (i, j))],
        acc_shapes=[(tmd, D // 2)], job=reduce_chips_job([q_fi]))
    return du, (q_fi, far_fi), (q_fo, far_fo), dact_moved, dwo_moved


def mix_fwd(u, wts, b_in, pool_scale, sinks, tabs, t, in_job, attn_job):
    w_in, wp, wba, wbb, wo = wts
    tm = min(1024, t)
    tmh = min(512, t)
    h, in_moved = mm("mix_in", [u], [w_in], [(0, 0, 0)], dims=NN, grid=(t // tmh, 4, 1),
                     a_specs=[pl.BlockSpec((tmh, D), lambda i, j, k: (i, 0))],
                     b_specs=[pl.BlockSpec((None, D, IN_SH), lambda i, j, k: (j, 0, 0))],
                     outs=[sds((t, IN_W), F32)], out_specs=[pl.BlockSpec((tmh, IN_SH), lambda i, j, k: (i, j))],
                     acc_shapes=[(tmh, IN_SH)], job=in_job)
    attn_job = attn_job(in_moved)
    pooled = pool_fwd(h, b_in, t, 512)
    gblk = pl.BlockSpec((tm, PG), lambda i, j, k: (i, j))
    mixed = mm("mix_pool", [pooled], [wp], [(0, 0, 0)], dims=NN, grid=(t // tm, 4, 1), a_specs=[gblk],
               b_specs=[pl.BlockSpec((None, PG, PG), lambda i, j, k: (j, 0, 0))],
               outs=[sds((t, PW), F32)], out_specs=[gblk], acc_shapes=[(tm, PG)])
    pm = rowmap("mix_pscale", lambda mv, ps: mv * ps, [T_(mixed), B_(pool_scale)], [(PW, BF16)], rows=t, tm=512)

    def branch(name, a, w):
        return mm(name, [a], [w], [(0, 0, 0)], dims=NN, grid=(t // tm, 4, 1),
                  a_specs=[pl.BlockSpec((tm, PW), lambda i, j, k: (i, 0))],
                  b_specs=[pl.BlockSpec((None, PW, D // 4), lambda i, j, k: (j, 0, 0))],
                  outs=[sds((t, D), F32)], out_specs=[pl.BlockSpec((tm, D // 4), lambda i, j, k: (i, j))],
                  acc_shapes=[(tm, D // 4)])

    ya = branch("mix_branch_a", pm, wba)

    def qkv(hq, hk, hv, bq, bk, bv, cc, sa, sb):
        return (_rope(hq + bq, cc, sa, sb) * (HD ** -0.5), _rope(hk + bk, cc, sa, sb), hv + bv)

    qr, kr, vv = rowmap(
        "mix_rope", qkv,
        [T_(h, QW, 1), T_(h, KVW, 8), T_(h, KVW, 9), B_(b_in, QW, 1), B_(b_in, KVW, 8), B_(b_in, KVW, 9),
         T_(tabs[0]), T_(tabs[1]), T_(tabs[2])],
        [(QW, BF16), (KVW, BF16), (KVW, BF16)], rows=t, tm=512)
    attn, attn_moved = attn_fwd(qr, kr, vv, sinks, t, attn_job)
    yb = branch("mix_branch_b", attn, wbb)
    cw = 512

    def merge(ga, gb, ba, bb, yav, ybv):
        return _sigmoid(ga + ba) * yav + _sigmoid(gb + bb) * ybv

    merged = rowmap(
        "mix_merge", merge,
        [T_(h, cw, 5), T_(h, cw, 9), B_(b_in, cw, 5), B_(b_in, cw, 9), T_(ya, cw), T_(yb, cw)],
        [(D, BF16)], rows=t, tm=512, ncol=D // cw)
    y = mm("mix_out", [merged], [wo], [(0, 0, 0)], dims=NN, grid=(t // tm, 2, 1),
           a_specs=[pl.BlockSpec((tm, D), lambda i, j, k: (i, 0))],
           b_specs=[pl.BlockSpec((D, D // 2), lambda i, j, k: (0, j))],
           outs=[sds((t, D), F32)], out_specs=[pl.BlockSpec((tm, D // 2), lambda i, j, k: (i, j))],
           acc_shapes=[(tm, D // 2)])
    return y, (h, pooled, mixed, pm, ya, qr, kr, vv, attn, yb, merged), attn_moved


def mix_bwd(u, saved, dy, wts, b_in, pool_scale, sinks, tabs, t):
    h, pooled, mixed, pm, ya, qr, kr, vv, attn, yb, merged = saved
    w_in, wp, wba, wbb, wo = wts
    tm = min(1024, t)
    tk = min(2048, t)
    dmerged = mm("mix_dmerged", [dy], [wo], [(0, 0, 0)], dims=NT, grid=(t // tm, 2, 1),
                 a_specs=[pl.BlockSpec((tm, D), lambda i, j, k: (i, 0))],
                 b_specs=[pl.BlockSpec((D // 2, D), lambda i, j, k: (j, 0))],
                 outs=[sds((t, D), F32)], out_specs=[pl.BlockSpec((tm, D // 2), lambda i, j, k: (i, j))],
                 acc_shapes=[(tm, D // 2)])
    half = pl.BlockSpec((tk, D // 2), lambda i, j, k: (k, i))
    dwo = mm("mix_dwo", [merged], [dy], [(0, 0, 0)], dims=TN, grid=(2, 2, t // tk), a_specs=[half],
             b_specs=[pl.BlockSpec((tk, D // 2), lambda i, j, k: (k, j))],
             outs=[sds((D, D), BF16)], out_specs=[pl.BlockSpec((D // 2, D // 2), lambda i, j, k: (i, j))],
             acc_shapes=[(D // 2, D // 2)])
    cw = 512

    def dmerge(dm, ga, gb, ba, bb, yav, ybv):
        sa_, sb_ = _sigmoid(ga + ba), _sigmoid(gb + bb)
        dga = dm * yav * sa_ * (1.0 - sa_)
        dgb = dm * ybv * sb_ * (1.0 - sb_)
        return dm * sa_, dm * sb_, dga, dgb, colsum(dga), colsum(dgb)

    dya, dyb, dgla, dglb, dbga, dbgb = rowmap(
        "mix_dmerge", dmerge,
        [T_(dmerged, cw), T_(h, cw, 5), T_(h, cw, 9), B_(b_in, cw, 5), B_(b_in, cw, 9), T_(ya, cw), T_(yb, cw)],
        [(D, BF16)] * 4, [(1, D), (1, D)], rows=t, tm=512, ncol=D // cw)

    def dbranch(name, dyv, act, w):
        dwb = mm(name + "_dw", [act], [dyv], [(0, 0, 0)], dims=TN, grid=(1, 4, t // tk),
                 a_specs=[pl.BlockSpec((tk, PW), lambda i, j, k: (k, 0))],
                 b_specs=[pl.BlockSpec((tk, D // 4), lambda i, j, k: (k, j))],
                 outs=[sds((4, PW, D // 4), BF16)], out_specs=[pl.BlockSpec((None, PW, D // 4), lambda i, j, k: (j, 0, 0))],
                 acc_shapes=[(PW, D // 4)])
        return dwb, lambda dt: mm(
            name + "_dx", [dyv], [w], [(0, 0, 0)], dims=NT, grid=(t // tm, 1, 4),
            a_specs=[pl.BlockSpec((tm, D // 4), lambda i, j, k: (i, k))],
            b_specs=[pl.BlockSpec((None, PW, D // 4), lambda i, j, k: (k, 0, 0))],
            outs=[sds((t, PW), dt)], out_specs=[pl.BlockSpec((tm, PW), lambda i, j, k: (i, 0))], acc_shapes=[(tm, PW)])

    dwba, dpm_fn = dbranch("mix_dbranch_a", dya, pm, wba)
    dwbb, dattn_fn = dbranch("mix_dbranch_b", dyb, attn, wbb)
    dpm, dattn = dpm_fn(F32), dattn_fn(BF16)
    dmixed, dps = rowmap("mix_dpscale", lambda dp, mv, ps: (dp * ps, colsum(dp * mv)),
                         [T_(dpm), T_(mixed), B_(pool_scale)], [(PW, BF16)], [(1, PW)], rows=t, tm=512)
    gblk = pl.BlockSpec((tm, PG), lambda i, j, k: (i, j))
    dpooled = mm("mix_dpool", [dmixed], [wp], [(0, 0, 0)], dims=NT, grid=(t // tm, 4, 1), a_specs=[gblk],
                 b_specs=[pl.BlockSpec((None, PG, PG), lambda i, j, k: (j, 0, 0))],
                 outs=[sds((t, PW), F32)], out_specs=[gblk], acc_shapes=[(tm, PG)])
    kblk = pl.BlockSpec((tk, PG), lambda i, j, k: (k, i))
    dwp = mm("mix_dwpool", [pooled], [dmixed], [(0, 0, 0)], dims=TN, grid=(4, 1, t // tk), a_specs=[kblk], b_specs=[kblk],
             outs=[sds((4, PG, PG), BF16)], out_specs=[pl.BlockSpec((None, PG, PG), lambda i, j, k: (i, 0, 0))],
             acc_shapes=[(PG, PG)])
    dxp, dbxp = pool_bwd(dpooled, t, 512)
    dqr, dkr, dvv, dsinks = attn_bwd(qr, kr, vv, dattn, sinks, t)

    def dqkv(dq, dk, dv, cc, sa, sb):
        dq = _rope_t(dq, cc, sa, sb) * (HD ** -0.5)
        dk = _rope_t(dk, cc, sa, sb)
        return dq, dk, dv, colsum(dq), colsum(dk), colsum(dv)

    dq, dk, dvb, dbq, dbk, dbv = rowmap(
        "mix_rope_bwd", dqkv, [T_(dqr), T_(dkr), T_(dvv), T_(tabs[0]), T_(tabs[1]), T_(tabs[2])],
        [(QW, BF16), (KVW, BF16), (KVW, BF16)], [(1, QW), (1, KVW), (1, KVW)], rows=t, tm=512)
    dh = jnp.concatenate([dxp, dq, dk, dvb, dgla, dglb], axis=1)
    db_in = jnp.concatenate([dbxp, dbq, dbk, dbv, dbga, dbgb], axis=1)
    dwin = mm("mix_dwin", [u], [dh], [(0, 0, 0)], dims=TN, grid=(2, 4, t // tk), a_specs=[half],
              b_specs=[pl.BlockSpec((tk, IN_SH), lambda i, j, k: (k, j))],
              outs=[sds((4, D, IN_SH), BF16)], out_specs=[pl.BlockSpec((None, D // 2, IN_SH), lambda i, j, k: (j, i, 0))],
              acc_shapes=[(D // 2, IN_SH)])
    dwp_sh = jnp.transpose(dwp.reshape(4, 4, 64, PG), (1, 0, 2, 3)).reshape(4, 4 * 64, PG)
    parts = {"win": dwin, "wp": dwp_sh, "wba": dwba, "wbb": dwbb, "wo": dwo.reshape(4, D // 4, D)}
    du, sib = mm("mix_du", [dh], [w_in], [(0, 0, 0)], dims=NT, grid=(t // tm, 2, 4),
                 a_specs=[pl.BlockSpec((tm, IN_SH), lambda i, j, k: (i, k))],
                 b_specs=[pl.BlockSpec((None, D // 2, IN_SH), lambda i, j, k: (k, j, 0))],
                 outs=[sds((t, D), F32)], out_specs=[pl.BlockSpec((tm, D // 2), lambda i, j, k: (i, j))],
                 acc_shapes=[(tm, D // 2)],
                 job=reduce_sibling_job([(p, view_lead, p.shape[1], p.shape[2]) for p in parts.values()]))
    return du, parts, dict(zip(parts, sib)), db_in, dps, dsinks


def cast_shard(name, w, sp, ffn_out=False):
    rows, cols = w.shape
    if ffn_out:
        tm = rows // 2
        shape = (2, FHP, D)
        spec = pl.BlockSpec((None, tm, cols), lambda j, i, s: (s[0] // 2, (s[0] % 2) * 2 + i, 0))
    else:
        tm = rows // 4
        shape = (4, rows, cols)
        spec = pl.BlockSpec((None, tm, cols), lambda j, i, s: (s[0], i, 0))
    return rowmap(name, lambda wv: wv, [T_(w)], [(shape, BF16, spec)], rows=rows, tm=tm, sp=sp)


def cast_ffn_in(name, wt, sp):
    tm = 64
    full = FH // tm

    def fn(_, i, wv):
        return jnp.where(i < full, wv, 0.0)

    return rowmap(name, fn, [X_(wt, pl.BlockSpec((tm, D), lambda j, i, s: (jnp.minimum(i, full - 1), 0)))],
                  [((4, FHP, D), BF16, pl.BlockSpec((None, tm, D), lambda j, i, s: (s[0], i, 0)))],
                  rows=FHP, tm=tm, sp=sp, with_ids=True)


def chip_sum(name, dw, got, sp, rows, tm, ffn_out=False):
    hr, cols = rows // 2, got.shape[2]
    per = hr // tm
    pos = pl.BlockSpec((None, tm, cols), lambda j, i, s: (i // per, i % per, 0))
    if ffn_out:
        mine = pl.BlockSpec((None, tm, cols), lambda j, i, s: (i // 2, (i % 2) * 2 + s[1], 0))
    else:
        mine = pl.BlockSpec((None, tm, cols), lambda j, i, s: (i // per, s[1] * per + i % per, 0))
    return rowmap(name, lambda av, bv: av.astype(F32) + bv.astype(F32), [X_(dw, mine), X_(got, pos)],
                  [(got.shape, BF16, pos)], rows=4 * hr, tm=tm, sp=sp)


def chip_total(name, q, got, sp, rows, tm):
    hr, cols = rows // 2, q.shape[2]
    per = hr // tm

    def part(f):
        return X_(got, pl.BlockSpec((None, tm, cols), lambda j, i, s, f=f: (f, i, 0)))

    return rowmap(
        name, lambda av, b0, b1, b2: ((av.astype(F32) + b0.astype(F32)) + b1.astype(F32)) + b2.astype(F32),
        [X_(q, pl.BlockSpec((None, tm, cols), lambda j, i, s: (s[0], i, 0))), part(0), part(1), part(2)],
        [((rows, cols), F32, pl.BlockSpec((tm, cols), lambda j, i, s: (s[1] * per + i, 0)))], rows=hr, tm=tm, sp=sp)


def kernel(x, c, w_ada, b_ada, ln_g, ln_b, w_ffn1_in, w_ffn1_out, w_in, b_in, w_pool, pool_scale, sinks, w_branch_a, w_branch_b, w_out, w_ffn2_in, w_ffn2_out, loss_target, m_w_ada, m_b_ada, m_ln_g, m_ln_b, m_w_ffn1_in, m_w_ffn1_out, m_w_in, m_b_in, m_w_pool, m_pool_scale, m_sinks, m_w_branch_a, m_w_branch_b, m_w_out, m_w_ffn2_in, m_w_ffn2_out, v_w_ada, v_b_ada, v_ln_g, v_ln_b, v_w_ffn1_in, v_w_ffn1_out, v_w_in, v_b_in, v_w_pool, v_pool_scale, v_sinks, v_w_branch_a, v_w_branch_b, v_w_out, v_w_ffn2_in, v_w_ffn2_out):
    t = x.shape[1]
    xs, tgt = x[0], loss_target[0]
    xi, yi, ci = lax.axis_index("x"), lax.axis_index("y"), lax.axis_index("c")
    chip = 2 * xi + yi
    dev = 2 * chip + ci
    b_in2, ps2, sinks2 = b_in, pool_scale, sinks

    sp = jnp.stack([chip, ci]).astype(jnp.int32)
    tr = lambda a: jnp.swapaxes(a[0], 0, 1)
    f1i_buf, f1i_send, f1i_recv, token = gather_start("gather_f1i_start", cast_ffn_in("cast_f1i", tr(w_ffn1_in), sp), FHP)

    first = (jnp.concatenate([c.reshape(-1), ln_g.reshape(-1), ln_b.reshape(-1)]) + token[0, 0]).reshape(-1, 128)
    first_all = allgather_small("gather_cond", first).reshape(8, -1)
    c_all = first_all[:, :D]
    ln_parts = first_all[0::2, D:].reshape(4, 2, 3, D // 4)
    ln_full = jnp.transpose(ln_parts, (1, 2, 0, 3)).reshape(2, 3, D)
    lgs = [ln_full[0, s:s + 1] for s in range(3)]
    lbs = [ln_full[1, s:s + 1] for s in range(3)]
    c16 = jnp.pad(c_all, ((0, 8), (0, 0)))
    b_ada_sh = lax.dynamic_slice(b_ada, (0, chip * ADA_SH), (1, ADA_SH))
    mod_part = ada_fwd(c16, w_ada[0], b_ada_sh)[:8]
    mod_all = allgather_small("gather_mod", mod_part.reshape(-1, 128)).reshape(8, 8, ADA_SH)
    mod_mine = lax.dynamic_index_in_dim(mod_all[0::2], dev, axis=1, keepdims=False).reshape(9, D)
    mods = [[mod_mine[3 * s + k:3 * s + k + 1] for k in range(3)] for s in range(3)]

    plain = [("f1o", w_ffn1_out[0]), ("win", w_in[0]), ("wp", w_pool[0].reshape(4 * 64, PG)), ("wba", w_branch_a[0]),
             ("wbb", w_branch_b[0]), ("wo", w_out[0]), ("f2o", w_ffn2_out[0])]
    sh = {n: cast_shard("cast_" + n, w, sp, ffn_out=n in ("f1o", "f2o")) for n, w in plain}
    sh["f1i"] = f1i_buf
    sh["f2i"] = cast_ffn_in("cast_f2i", tr(w_ffn2_in), sp)
    order = ["f1i", "f1o", "win", "wp", "wba", "wbb", "wo", "f2i", "f2o"]
    views = {n: (view_ffn_out if n in ("f1o", "f2o") else view_lead) for n in order}
    shard_rows = {n: (FO if n in ("f1o", "f2o") else sh[n].shape[1]) for n in order}
    shard_cols = {n: sh[n].shape[2] for n in order}
    tiles = {"f1i": FHP // 8, "f1o": FO // 2, "win": 512, "wp": 128, "wba": 512, "wbb": 512, "wo": 256,
             "f2i": FHP // 8, "f2o": FO // 2}

    def item(n, part=0, parts=1):
        return (sh[n], views[n], shard_rows[n], part, parts)

    tabs = rope_tables(t)
    (sh0, sc0, gt0), (sh1, sc1, gt1), (sh2, sc2, gt2) = mods

    u0 = modulate("ffn1_mod", xs, sh0, sc0, t)
    landed = gather_wait("gather_f1i_wait", f1i_buf, f1i_send, f1i_recv, FHP,
                         [u0] + [sh[n] for n in order if n != "f1i"])
    (g_f1i,) = run_job("gather_f1i_forward", forward_job(landed, FHP))
    ha1, hb1, g1, y1, f1o, up1, (g_win,) = ffn_fwd(
        "ffn1", u0, g_f1i, t, gather_job([item(n) for n in ("f1o", "wp", "wba", "wbb", "wo")]),
        gather_job([item("win")]))
    x1, z1, u1 = residual_ln_mod("ffn1_ln", xs, y1, gt0, lgs[0], lbs[0], 0.5, sh1, sc1, t)
    _, g_wp, g_wba, g_wbb, g_wo = up1
    wp_full = jnp.transpose(g_wp.reshape(4, 4, 64, PG), (1, 0, 2, 3)).reshape(4, PG, PG)
    wts = (g_win, wp_full, g_wba, g_wbb, g_wo.reshape(D, D))
    y2, sv2, (g_f2i,) = mix_fwd(
        u1, wts, b_in2, ps2, sinks2, tabs, t, gather_job([item("f2i", 0, 2)]),
        lambda moved: gather_job([(moved[0], view_lead, FHP, 1, 2)]))
    x2, z2, u2 = residual_ln_mod("mix_ln", x1, y2, gt1, lgs[1], lbs[1], 1.0, sh2, sc2, t)
    ha3, hb3, g3, y3, f2o, _, _ = ffn_fwd("ffn2", u2, g_f2i, t, gather_job([item("f2o")]))

    dz3, dy3, dlg2, dlb2, dgt2, sq = residual_ln_loss_bwd("ffn2_ln_loss", x2, y3, tgt, gt2, lgs[2], lbs[2], 0.5, t)
    loss = lax.psum(0.5 * sq[0, 0] / D, ("x", "y", "c"))
    du3, red_f2i, red_f2o, _, _ = ffn_bwd("ffn2", u2, ha3, hb3, g3, dy3, g_f2i, f2o, t, sp)
    dz2, dy2, dlg1, dlb1, dgt1, dsh2, dsc2 = residual_ln_bwd("mix_ln_bwd", z2, (dz3, du3, x2, sc2), y2, gt1, lgs[1], 1.0, t)
    du2, mix_parts, sib, db_in, dps, dsinks = mix_bwd(u1, sv2, dy2, wts, b_in2, ps2, sinks2, tabs, t)
    q = {n: chip_sum("chipsum_" + n, mix_parts[n], sib[n], sp, shard_rows[n], tiles[n]) for n in mix_parts}
    dz1, dy1, dlg0, dlb0, dgt0, dsh1, dsc1 = residual_ln_bwd("ffn1_ln_bwd", z1, (dz2, du2, x1, sc1), y1, gt0, lgs[0], 0.5, t)
    du1, red_f1i, red_f1o, far_a, far_b = ffn_bwd(
        "ffn1", u0, ha1, hb1, g1, dy1, g_f1i, f1o, t, sp,
        reduce_chips_job([q["win"], q["wp"]]), reduce_chips_job([q["wo"], q["wba"], q["wbb"]]))
    dx0, dsh0, dsc0 = modulate_bwd("ffn1_mod_bwd", dz1, du1, xs, sc0, t)
    gm0, gm1, gm2 = (dsh0, dsc0, dgt0), (dsh1, dsc1, dgt1), (dsh2, dsc2, dgt2)
    reduced = {"f1i": red_f1i, "f1o": red_f1o, "f2i": red_f2i, "f2o": red_f2o, "win": (q["win"], far_a[0]),
               "wp": (q["wp"], far_a[1]), "wo": (q["wo"], far_b[0]), "wba": (q["wba"], far_b[1]), "wbb": (q["wbb"], far_b[2])}
    halves = [chip_total("total_" + n, *reduced[n], sp, shard_rows[n], tiles[n]) for n in order]

    small = jnp.concatenate([*gm0, *gm1, *gm2, dlg0, dlg1, dlg2, dlb0, dlb1, dlb2, db_in, dps, dsinks], axis=1)
    n_small = small.shape[1]
    rows_small = -(-n_small // 1024) * 8
    small = jnp.pad(small, ((0, 0), (0, rows_small * 128 - n_small))).reshape(rows_small, 128)
    small_all = allgather_small("gather_small", small)
    tot = sum_devices(small_all).reshape(1, -1)
    gmod_all = small_all.reshape(8, -1)[:, :9 * D]
    o = 9 * D
    g_b_ada = tot[:, :o]
    g_ln_g = lax.dynamic_slice(tot[:, o:o + 3 * D].reshape(3, D), (0, chip * (D // 4)), (3, D // 4))
    g_ln_b = lax.dynamic_slice(tot[:, o + 3 * D:o + 6 * D].reshape(3, D), (0, chip * (D // 4)), (3, D // 4))
    o += 6 * D
    g_b_in, g_ps, g_sinks = tot[:, o:o + IN_W], tot[:, o + IN_W:o + IN_W + PW], tot[:, o + IN_W + PW:o + IN_W + PW + N_Q]

    gm16 = jnp.pad(lax.dynamic_slice(gmod_all, (0, chip * ADA_SH), (8, ADA_SH)), ((0, 8), (0, 0)))
    (g_w_ada, d_w_ada, nm_w_ada, nv_w_ada), _ = ada_bwd_adam(c16, gm16, w_ada[0], m_w_ada[0], v_w_ada[0], None)
    gw = dict(zip(order, run_job("share_halves", share_halves_job(halves))))

    def big(n, w, m, v, tm):
        shape = w.shape
        w2, m2, v2 = (a.reshape(shape[-2] if a.ndim == 3 else -1, shape[-1]) for a in (w, m, v))
        return [r.reshape(shape) for r in adam_rows("adam_" + n, w2, gw[n], m2, v2, tm)]

    def big_t(n, w, m, v):
        return [jnp.swapaxes(r, 0, 1)[None] for r in adam_rows("adam_" + n, tr(w), gw[n], tr(m), tr(v), 64)]

    def tiny(n, w, g, m, v):
        return [g.reshape(w.shape)] + list(adam_small("adam_" + n, w, g.reshape(w.shape), m, v))

    res = {
        "w_ada": [a[None] for a in (g_w_ada, d_w_ada, nm_w_ada, nv_w_ada)],
        "b_ada": tiny("b_ada", b_ada, g_b_ada, m_b_ada, v_b_ada),
        "ln_g": tiny("ln_g", ln_g, g_ln_g, m_ln_g, v_ln_g),
        "ln_b": tiny("ln_b", ln_b, g_ln_b, m_ln_b, v_ln_b),
        "w_ffn1_in": big_t("f1i", w_ffn1_in, m_w_ffn1_in, v_w_ffn1_in),
        "w_ffn1_out": big("f1o", w_ffn1_out, m_w_ffn1_out, v_w_ffn1_out, 32),
        "w_in": big("win", w_in, m_w_in, v_w_in, 256),
        "b_in": tiny("b_in", b_in, g_b_in, m_b_in, v_b_in),
        "w_pool": big("wp", w_pool, m_w_pool, v_w_pool, 256),
        "pool_scale": tiny("pool_scale", pool_scale, g_ps, m_pool_scale, v_pool_scale),
        "sinks": tiny("sinks", sinks, g_sinks, m_sinks, v_sinks),
        "w_branch_a": big("wba", w_branch_a, m_w_branch_a, v_w_branch_a, 512),
        "w_branch_b": big("wbb", w_branch_b, m_w_branch_b, v_w_branch_b, 512),
        "w_out": big("wo", w_out, m_w_out, v_w_out, 128),
        "w_ffn2_in": big_t("f2i", w_ffn2_in, m_w_ffn2_in, v_w_ffn2_in),
        "w_ffn2_out": big("f2o", w_ffn2_out, m_w_ffn2_out, v_w_ffn2_out, 32),
    }
    names = ["w_ada", "b_ada", "ln_g", "ln_b", "w_ffn1_in", "w_ffn1_out", "w_in", "b_in", "w_pool", "pool_scale", "sinks",
             "w_branch_a", "w_branch_b", "w_out", "w_ffn2_in", "w_ffn2_out"]
    return (loss, dx0[None], *[res[n][0] for n in names], *[res[n][1] for n in names],
            *[res[n][2] for n in names], *[res[n][3] for n in names])
```

```python
import jax
import jax.numpy as jnp
from jax import lax
from jax.experimental import pallas as pl
from jax.experimental.pallas import tpu as pltpu

F32 = jnp.float32
BF16 = jnp.bfloat16
MESH = pl.DeviceIdType.MESH
ANY = pl.BlockSpec(memory_space=pl.ANY)

D = 2048
N_Q, N_KV, HD = 16, 4, 64
QW, KVW = N_Q * HD, N_KV * HD
BLK = 128
POOL_WINDOWS = (2, 4, 8, 16)
PW, PG = 1024, 256
HALO = 16
ROPE_THETA = 500000.0
ROT = HD // 4
LN_EPS = 1e-5
ALPHA = 2.0 ** 0.25
FH = 2752
FHP = 2816
FO = 1376
IN_W = 6656
IN_SH = IN_W // 4
ADA_SH = 18432 // 4
B1, B2, LR, EPS, WD, STEP = 0.9, 0.999, 0.001, 1e-08, 0.01, 10
VMEM_LIMIT = 56 * 1024 * 1024
FLIPS = ((1, 0), (0, 1), (1, 1))
NN = (((1,), (0,)), ((), ()))
NT = (((1,), (1,)), ((), ()))
TN = (((0,), (0,)), ((), ()))


def _params(sem):
    return pltpu.CompilerParams(dimension_semantics=sem, vmem_limit_bytes=VMEM_LIMIT)


def _aligned(v, m):
    return v if isinstance(v, int) else pl.multiple_of(v, m)


def _sigmoid(v):
    return 1.0 / (1.0 + jnp.exp(-v))


def T_(arr, width=None, off=0):
    return ("t", arr, width, off)


def B_(arr, width=None, off=0):
    return ("b", arr, width, off)


def X_(arr, spec):
    return ("x", arr, spec, 0)


def rowmap(name, fn, ins, outs, accs=(), *, rows, tm, ncol=1, with_ids=False, sp=None, alias=None):
    tm = min(tm, rows)
    nrow = rows // tm
    in_specs, arrs = [], []
    for kind, arr, width, off in ins:
        if kind == "x":
            in_specs.append(width)
        elif kind == "t":
            w = arr.shape[1] if width is None else width
            in_specs.append(pl.BlockSpec((tm, w), lambda j, i, *_, off=off: (i, off + j)))
        else:
            w = arr.shape[1] if width is None else width
            in_specs.append(pl.BlockSpec((arr.shape[0], w), lambda j, i, *_, off=off: (0, off + j)))
        arrs.append(arr)
    out_shape, out_specs = [], []
    for o in outs:
        if len(o) == 3:
            out_shape.append(jax.ShapeDtypeStruct(o[0], o[1]))
            out_specs.append(o[2])
        else:
            out_shape.append(jax.ShapeDtypeStruct((rows, o[0]), o[1]))
            out_specs.append(pl.BlockSpec((tm, o[0] // ncol), lambda j, i, *_: (i, j)))
    for r, width in accs:
        out_shape.append(jax.ShapeDtypeStruct((r, width), F32))
        out_specs.append(pl.BlockSpec((r, width // ncol), lambda j, i, *_: (0, j)))
    ni, no = len(ins), len(outs)
    nsp = 0 if sp is None else 1

    def body(*refs):
        refs = refs[nsp:]
        i = pl.program_id(1)
        vals = [r[...] for r in refs[:ni]]
        res = fn(pl.program_id(0), i, *vals) if with_ids else fn(*vals)
        if not isinstance(res, (tuple, list)):
            res = (res,)
        for r, v in zip(refs[ni:ni + no], res[:no]):
            r[...] = v.astype(r.dtype)
        for r, v in zip(refs[ni + no:], res[no:]):
            @pl.when(i == 0)
            def _(r=r, v=v):
                r[...] = v

            @pl.when(i > 0)
            def _(r=r, v=v):
                r[...] += v

    grid_spec = pltpu.PrefetchScalarGridSpec(num_scalar_prefetch=nsp, grid=(ncol, nrow), in_specs=in_specs,
                                             out_specs=out_specs)
    res = pl.pallas_call(
        body, name=name, grid_spec=grid_spec, out_shape=out_shape,
        input_output_aliases={nsp + k: v for k, v in (alias or {}).items()},
        compiler_params=_params(("arbitrary", "arbitrary")),
    )(*([sp] if nsp else []), *arrs)
    return res[0] if len(res) == 1 else res


def colsum(v):
    return jnp.sum(v, axis=0, keepdims=True)


def mm(name, a_ops, b_ops, ops, *, dims, grid, a_specs, b_specs, outs, out_specs, acc_shapes,
       epilogue=None, extras=(), extra_specs=(), carry=None, job=None, sub_rows=None):
    gk = grid[2]
    na, nb, ne, nacc = len(a_ops), len(b_ops), len(extras), len(acc_shapes)
    nc = 0 if carry is None else 1
    no = len(outs)

    def body(*refs):
        a_refs = refs[:na]
        b_refs = refs[na:na + nb]
        e_refs = refs[na + nb:na + nb + ne]
        o_refs = refs[na + nb + ne + nc:na + nb + ne + nc + no]
        acc_refs = refs[na + nb + ne + nc + no:]
        k = pl.program_id(2)

        def partials(rows=slice(None)):
            res = [None] * nacc
            for ai, bi, ci in ops:
                p = lax.dot_general(a_refs[ai][rows], b_refs[bi][...], dims, preferred_element_type=F32)
                res[ci] = p if res[ci] is None else res[ci] + p
            return res

        def finish(accs, rows=slice(None)):
            outv = epilogue(accs, [e[rows] for e in e_refs]) if epilogue else (accs[0],)
            for o, v in zip(o_refs, outv):
                o[rows] = v.astype(o.dtype)

        if gk == 1 and sub_rows:
            for s in range(out_specs[0].block_shape[-2] // sub_rows):
                rows = pl.ds(s * sub_rows, sub_rows)
                finish(partials(rows), rows)
        elif gk == 1:
            finish(partials())
        else:
            ps = partials()

            @pl.when(k == 0)
            def _():
                for acc, p in zip(acc_refs, ps):
                    acc[...] = p

            @pl.when((k > 0) & (k < gk - 1))
            def _():
                for acc, p in zip(acc_refs, ps):
                    acc[...] += p

            @pl.when(k == gk - 1)
            def _():
                finish([acc[...] + p for acc, p in zip(acc_refs, ps)])

    res, moved = carried_call(
        body, name, grid,
        list(a_specs) + list(b_specs) + list(extra_specs) + ([ANY] if nc else []), list(out_specs), list(outs),
        [pltpu.VMEM(s, F32) for s in acc_shapes] if gk > 1 else [],
        [*a_ops, *b_ops, *extras, *([carry] if nc else [])], {na + nb + ne: 0} if nc else {}, job)
    res = res[0] if len(res) == 1 else res
    return res if job is None else (res, moved)


def sds(shape, dt):
    return jax.ShapeDtypeStruct(shape, dt)


class Job:
    def __init__(self, ins, outs, aliases, scratch, start, mid, finish):
        self.ins, self.outs, self.aliases, self.scratch = list(ins), list(outs), dict(aliases), list(scratch)
        self.start, self.mid, self.finish = start, mid, finish


def carried_call(body, name, grid, in_specs, out_specs, out_shape, scratch, args, aliases, job, mid_at=0.9):
    sem = ("arbitrary",) * len(grid)
    if job is None:
        res = pl.pallas_call(body, name=name, grid=grid, in_specs=in_specs, out_specs=out_specs, out_shape=out_shape,
                             scratch_shapes=scratch, input_output_aliases=aliases, compiler_params=_params(sem))(*args)
        return list(res), []
    ni, no, ns = len(in_specs), len(out_specs), len(scratch)
    ci, co = len(job.ins), len(job.outs)
    total = 1
    for g in grid:
        total *= g
    mid_step = min(max(int(total * mid_at), 1), total - 1)

    def full(*refs):
        ins, cins = refs[:ni], refs[ni:ni + ci]
        outs, couts = refs[ni + ci:ni + ci + no], refs[ni + ci + no:ni + ci + no + co]
        scr, cscr = refs[ni + ci + no + co:ni + ci + no + co + ns], refs[ni + ci + no + co + ns:]
        step = 0
        for d, g in enumerate(grid):
            step = step * g + pl.program_id(d)

        @pl.when(step == 0)
        def _():
            job.start(cins, couts, cscr)

        body(*ins, *outs, *scr)

        @pl.when(step == mid_step)
        def _():
            job.mid(cins, couts, cscr)

        @pl.when(step == total - 1)
        def _():
            job.finish(cins, couts, cscr)

    al = dict(aliases)
    al.update({ni + k: no + v for k, v in job.aliases.items()})
    res = pl.pallas_call(
        full, name=name, grid=grid, in_specs=in_specs + [ANY] * ci, out_specs=out_specs + [ANY] * co,
        out_shape=out_shape + job.outs, scratch_shapes=scratch + job.scratch, input_output_aliases=al,
        compiler_params=_params(sem))(*args, *job.ins)
    return list(res[:no]), list(res[no:])


def _with_moved(res, job):
    return res if job is not None else (res, [])


def run_job(name, job):
    ci = len(job.ins)

    def body(*refs):
        cins, couts, cscr = refs[:ci], refs[ci:ci + len(job.outs)], refs[ci + len(job.outs):]
        job.start(cins, couts, cscr)
        job.mid(cins, couts, cscr)
        job.finish(cins, couts, cscr)

    return list(pl.pallas_call(
        body, name=name, in_specs=[ANY] * ci, out_specs=[ANY] * len(job.outs), out_shape=job.outs,
        scratch_shapes=job.scratch, input_output_aliases=job.aliases)(*job.ins))


def _place():
    x, y, c = lax.axis_index("x"), lax.axis_index("y"), lax.axis_index("c")
    chips = [((1 - x) if fx else x, (1 - y) if fy else y) for fx, fy in FLIPS]
    return x, y, c, chips


def allgather_small(name, v):
    r = v.shape[0]

    def body(x_ref, out_ref, send_sems, recv_sems, local_sem):
        x, y, c, chips = _place()
        me, sibling = (x, y, c), (x, y, 1 - c)

        def rows(px, py, pc):
            return out_ref.at[4 * px + 2 * py + pc]

        def copy(k, block, to, src=None):
            return pltpu.make_async_remote_copy(
                src_ref=rows(*block) if src is None else src, dst_ref=rows(*block),
                send_sem=send_sems.at[k], recv_sem=recv_sems.at[k], device_id=to, device_id_type=MESH)

        mine = pltpu.make_async_copy(x_ref, rows(*me), local_sem)
        mine.start()
        first = [copy(0, me, sibling, src=x_ref)]
        first += [copy(1 + j, me, (*chip, c), src=x_ref) for j, chip in enumerate(chips)]
        for cp in first:
            cp.start()
        passed = [copy(4 + j, (*chip, c), sibling) for j, chip in enumerate(chips)]
        for j, chip in enumerate(chips):
            copy(1 + j, (*chip, c), me).wait_recv()
            passed[j].start()
        copy(0, sibling, me).wait_recv()
        for j, chip in enumerate(chips):
            copy(4 + j, (*chip, 1 - c), me).wait_recv()
        for cp in first + passed:
            cp.wait_send()
        mine.wait()

    return pl.pallas_call(
        body, name=name, out_shape=sds((8, r, 128), v.dtype),
        in_specs=[pl.BlockSpec(memory_space=pltpu.VMEM)], out_specs=pl.BlockSpec(memory_space=pltpu.VMEM),
        scratch_shapes=[pltpu.SemaphoreType.DMA((7,)), pltpu.SemaphoreType.DMA((7,)), pltpu.SemaphoreType.DMA],
    )(v)


def _half(ref, rows, hf):
    hr = rows // 2
    return ref.at[pl.ds(_aligned(hf * hr, 16), hr)]


def view_lead(ref, p):
    return ref.at[p]


def view_ffn_out(ref, p):
    return ref.at[p // 2, pl.ds(_aligned((p % 2) * FO, 16), FO)]


def _remote(ref, dst, send_sems, recv_sems, idx, to):
    return pltpu.make_async_remote_copy(src_ref=ref, dst_ref=dst, send_sem=send_sems.at[idx], recv_sem=recv_sems.at[idx],
                                        device_id=to, device_id_type=MESH)


def gather_job(items):
    nw = len(items)
    pads = [w for w, it in enumerate(items) if it[1] is view_ffn_out]

    def piece(ref, w, p, hf):
        _, view, rws, part, parts = items[w]
        pr = rws // 2 // parts
        return view(ref, p).at[pl.ds(_aligned(hf * (rws // 2) + part * pr, 16), pr)]

    def pad_copies(outs, scr):
        return [pltpu.make_async_copy(scr[2], outs[w].at[h, pl.ds(2 * FO, FHP - 2 * FO)], scr[3].at[2 * n + h])
                for n, w in enumerate(pads) for h in range(2)]

    def start(_, outs, scr):
        x, y, c, chips = _place()
        if pads:
            scr[2][...] = jnp.zeros_like(scr[2])
            for cp in pad_copies(outs, scr):
                cp.start()
        for w in range(nw):
            mine = piece(outs[w], w, 2 * x + y, c)
            for f, (px, py) in enumerate(chips):
                _remote(mine, mine, scr[0], scr[1], (w, f), (px, py, c)).start()

    def mid(_, outs, scr):
        x, y, c, chips = _place()
        for w in range(nw):
            for f, (px, py) in enumerate(chips):
                land = piece(outs[w], w, 2 * px + py, c)
                _remote(land, land, scr[0], scr[1], (w, f), (px, py, c)).wait_recv()
                _remote(land, land, scr[0], scr[1], (w, 3 + f), (x, y, 1 - c)).start()

    def finish(_, outs, scr):
        x, y, c, chips = _place()
        for w in range(nw):
            for f, (px, py) in enumerate(chips):
                land = piece(outs[w], w, 2 * px + py, 1 - c)
                _remote(land, land, scr[0], scr[1], (w, 3 + f), (x, y, 1 - c)).wait_recv()
        for w in range(nw):
            mine = piece(outs[w], w, 2 * x + y, c)
            for f in range(6):
                _remote(mine, mine, scr[0], scr[1], (w, f), (x, y, 1 - c)).wait_send()
        for cp in pad_copies(outs, scr):
            cp.wait()

    scratch = [pltpu.SemaphoreType.DMA((nw, 6)), pltpu.SemaphoreType.DMA((nw, 6))]
    if pads:
        scratch += [pltpu.VMEM((FHP - 2 * FO, D), BF16), pltpu.SemaphoreType.DMA((2 * len(pads),))]
    bufs = [it[0] for it in items]
    return Job(bufs, [sds(b.shape, BF16) for b in bufs], {w: w for w in range(nw)}, scratch, start, mid, finish)


HBM = pl.BlockSpec(memory_space=pltpu.HBM)
SEM = pl.BlockSpec(memory_space=pltpu.SEMAPHORE)
SPLIT = pltpu.CompilerParams(has_side_effects=pltpu.SideEffectType.DATAFLOW_SIDE_EFFECTING)


def gather_start(name, buf, rows, after):
    def body(*refs):
        out, send_sems, recv_sems, token = refs[1 + len(after):]
        x, y, c, chips = _place()
        mine = _half(out.at[2 * x + y], rows, c)
        for f, (px, py) in enumerate(chips):
            _remote(mine, mine, send_sems, recv_sems, f, (px, py, c)).start()
        token[...] = jnp.zeros_like(token)

    return pl.pallas_call(
        body, name=name,
        out_shape=(pltpu.HBM(buf.shape, buf.dtype), pltpu.SemaphoreType.DMA((3,)), pltpu.SemaphoreType.DMA((3,)),
                   sds((8, 128), F32)),
        in_specs=(HBM,) + (ANY,) * len(after), out_specs=(HBM, SEM, SEM, pl.BlockSpec(memory_space=pltpu.VMEM)),
        input_output_aliases={0: 0}, compiler_params=SPLIT)(pltpu.with_memory_space_constraint(buf, pltpu.HBM), *after)


def gather_wait(name, buf, send_sems, recv_sems, rows, after):
    def body(_, send_sems, recv_sems, *rest):
        out = rest[-1]
        x, y, c, chips = _place()
        mine = _half(out.at[2 * x + y], rows, c)
        for f, (px, py) in enumerate(chips):
            cp = _remote(mine, _half(out.at[2 * px + py], rows, c), send_sems, recv_sems, f, (px, py, c))
            cp.wait_send()
            cp.wait_recv()

    return pl.pallas_call(
        body, name=name, out_shape=pltpu.HBM(buf.shape, buf.dtype),
        in_specs=(HBM, SEM, SEM) + (ANY,) * len(after), out_specs=HBM, input_output_aliases={0: 0},
        compiler_params=SPLIT)(buf, send_sems, recv_sems, *after)


def forward_job(buf, rows):
    def copies(outs, scr, hf):
        x, y, c, chips = _place()
        half = c if hf == 0 else 1 - c
        return [_remote(_half(outs[0].at[2 * px + py], rows, half), _half(outs[0].at[2 * px + py], rows, half),
                        scr[0], scr[1], f, (x, y, 1 - c)) for f, (px, py) in enumerate(chips)]

    def start(_, outs, scr):
        for cp in copies(outs, scr, 0):
            cp.start()

    def finish(_, outs, scr):
        for cp in copies(outs, scr, 1):
            cp.wait_recv()
        for cp in copies(outs, scr, 0):
            cp.wait_send()

    return Job([buf], [sds(buf.shape, buf.dtype)], {0: 0},
               [pltpu.SemaphoreType.DMA((3,)), pltpu.SemaphoreType.DMA((3,))], start, lambda *_: None, finish)


def reduce_sibling_job(items):
    nw = len(items)

    def copies(ins, got, scr):
        x, y, c, _ = _place()
        return [_remote(_half(view(ins[w], p), rws, 1 - c), got[w].at[p], scr[0], scr[1], (w, p), (x, y, 1 - c))
                for w, (_, view, rws, _) in enumerate(items) for p in range(4)]

    def start(ins, got, scr):
        for cp in copies(ins, got, scr):
            cp.start()

    def finish(ins, got, scr):
        for cp in copies(ins, got, scr):
            cp.wait()

    return Job([it[0] for it in items], [sds((4, it[2] // 2, it[3]), BF16) for it in items], {},
               [pltpu.SemaphoreType.DMA((nw, 4)), pltpu.SemaphoreType.DMA((nw, 4))], start, lambda *_: None, finish)


def reduce_chips_job(qs):
    nw = len(qs)

    def copies(ins, got, scr):
        x, y, c, chips = _place()
        return [_remote(ins[w].at[2 * px + py], got[w].at[f], scr[0], scr[1], (w, f), (px, py, c))
                for w in range(nw) for f, (px, py) in enumerate(chips)]

    def start(ins, got, scr):
        for cp in copies(ins, got, scr):
            cp.start()

    def finish(ins, got, scr):
        for cp in copies(ins, got, scr):
            cp.wait()

    return Job(qs, [sds((3,) + q.shape[1:], BF16) for q in qs], {},
               [pltpu.SemaphoreType.DMA((nw, 3)), pltpu.SemaphoreType.DMA((nw, 3))], start, lambda *_: None, finish)


def share_halves_job(gs):
    nw = len(gs)

    def start(_, outs, scr):
        x, y, c, _ = _place()
        for w in range(nw):
            mine = _half(outs[w], gs[w].shape[0], c)
            _remote(mine, mine, scr[0], scr[1], w, (x, y, 1 - c)).start()

    def finish(_, outs, scr):
        x, y, c, _ = _place()
        for w in range(nw):
            mine = _half(outs[w], gs[w].shape[0], c)
            theirs = _half(outs[w], gs[w].shape[0], 1 - c)
            _remote(mine, mine, scr[0], scr[1], w, (x, y, 1 - c)).wait_send()
            _remote(theirs, theirs, scr[0], scr[1], w, (x, y, 1 - c)).wait_recv()

    return Job(gs, [sds(g.shape, F32) for g in gs], {w: w for w in range(nw)},
               [pltpu.SemaphoreType.DMA((nw,)), pltpu.SemaphoreType.DMA((nw,))], start, lambda *_: None, finish)


def rope_tables(t):
    pos = jnp.arange(t, dtype=F32)
    inv_freq = ROPE_THETA ** (-jnp.arange(0, ROT, 2, dtype=F32) / ROT)
    ang = pos[:, None] * inv_freq[None, :]
    cos, sin = jnp.cos(ang), jnp.sin(ang)
    d = jnp.arange(128) % HD
    half = ROT // 2
    cs = jnp.take(cos, d % half, axis=1)
    sn = jnp.take(sin, d % half, axis=1)
    cc = jnp.where(d[None] < ROT, cs, 1.0)
    sa = jnp.where(d[None] < half, -sn, 0.0)
    sb = jnp.where((d[None] >= half) & (d[None] < ROT), sn, 0.0)
    return cc, sa, sb


def _rope(v, cc, sa, sb):
    w = v.shape[1]
    reps = w // 128
    half = ROT // 2
    return (v * jnp.tile(cc, (1, reps)) + pltpu.roll(v, w - half, 1) * jnp.tile(sa, (1, reps))
            + pltpu.roll(v, half, 1) * jnp.tile(sb, (1, reps)))


def _rope_t(dv, cc, sa, sb):
    w = dv.shape[1]
    reps = w // 128
    half = ROT // 2
    return (dv * jnp.tile(cc, (1, reps)) + pltpu.roll(dv * jnp.tile(sa, (1, reps)), half, 1)
            + pltpu.roll(dv * jnp.tile(sb, (1, reps)), w - half, 1))


def pool_fwd(h, b_in, t, tm):
    tm = min(tm, t)
    per = tm // HALO

    def body(prev_ref, cur_ref, b_ref, o_ref, xx):
        i = pl.program_id(0)
        b = b_ref[...]
        xx[pl.ds(0, HALO), :] = jnp.where(i > 0, prev_ref[...] + b, 0.0)
        xx[pl.ds(HALO, tm), :] = cur_ref[...] + b
        tpos = i * tm + lax.broadcasted_iota(jnp.int32, (tm, PG), 0) + 1
        for gi, w in enumerate(POOL_WINDOWS):
            cols = pl.ds(gi * PG, PG)
            acc = xx[pl.ds(HALO, tm), cols]
            for s in range(1, w):
                acc = acc + xx[pl.ds(HALO - s, tm), cols]
            cnt = jnp.minimum(tpos, w).astype(F32)
            o_ref[:, cols] = (acc / cnt - xx[pl.ds(HALO, tm), cols]).astype(o_ref.dtype)

    return pl.pallas_call(
        body, name="pool_fwd", grid=(t // tm,),
        in_specs=[pl.BlockSpec((HALO, PW), lambda i: (jnp.maximum(i * per - 1, 0), 0)),
                  pl.BlockSpec((tm, PW), lambda i: (i, 0)), pl.BlockSpec((1, PW), lambda i: (0, 0))],
        out_specs=pl.BlockSpec((tm, PW), lambda i: (i, 0)), out_shape=sds((t, PW), BF16),
        scratch_shapes=[pltpu.VMEM((tm + HALO, PW), F32)], compiler_params=_params(("arbitrary",)),
    )(h, h, b_in)


def pool_bwd(dpooled, t, tm):
    tm = min(tm, t)
    per = tm // HALO
    nt = t // tm

    def body(cur_ref, nxt_ref, o_ref, db_ref, ee):
        i = pl.program_id(0)
        tpos = i * tm + lax.broadcasted_iota(jnp.int32, (tm, PG), 0) + 1
        for gi, w in enumerate(POOL_WINDOWS):
            cols = pl.ds(gi * PG, PG)
            ee[pl.ds(0, tm), cols] = cur_ref[:, cols] / jnp.minimum(tpos, w).astype(F32)
            ee[pl.ds(tm, HALO), cols] = jnp.where(i < nt - 1, nxt_ref[:, cols] / float(w), 0.0)
        for gi, w in enumerate(POOL_WINDOWS):
            cols = pl.ds(gi * PG, PG)
            acc = ee[pl.ds(0, tm), cols]
            for s in range(1, w):
                acc = acc + ee[pl.ds(s, tm), cols]
            dxp = acc - cur_ref[:, cols]
            o_ref[:, cols] = dxp.astype(o_ref.dtype)
            part = colsum(dxp)

            @pl.when(i == 0)
            def _(cols=cols, part=part):
                db_ref[:, cols] = part

            @pl.when(i > 0)
            def _(cols=cols, part=part):
                db_ref[:, cols] += part

    return pl.pallas_call(
        body, name="pool_bwd", grid=(nt,),
        in_specs=[pl.BlockSpec((tm, PW), lambda i: (i, 0)),
                  pl.BlockSpec((HALO, PW), lambda i: (jnp.minimum((i + 1) * per, t // HALO - 1), 0))],
        out_specs=[pl.BlockSpec((tm, PW), lambda i: (i, 0)), pl.BlockSpec((1, PW), lambda i: (0, 0))],
        out_shape=[sds((t, PW), BF16), sds((1, PW), F32)],
        scratch_shapes=[pltpu.VMEM((tm + HALO, PW), F32)], compiler_params=_params(("arbitrary",)),
    )(dpooled, dpooled)


def _scores(qh, kp, kc, mask_p, mask_c, sink):
    sp = jnp.where(mask_p, lax.dot_general(qh, kp, NT, preferred_element_type=F32), -1e30)
    sc = jnp.where(mask_c, lax.dot_general(qh, kc, NT, preferred_element_type=F32), -1e30)
    m = jnp.maximum(jnp.maximum(jnp.max(sp, axis=-1, keepdims=True), jnp.max(sc, axis=-1, keepdims=True)), sink)
    pp, pc = jnp.exp(sp - m), jnp.exp(sc - m)
    es = jnp.exp(sink - m)
    inv = 1.0 / (jnp.sum(pp, axis=-1, keepdims=True) + jnp.sum(pc, axis=-1, keepdims=True) + es)
    return pp * inv, pc * inv, es * inv


GRP = N_Q // N_KV


def _masks(n):
    qi = lax.broadcasted_iota(jnp.int32, (GRP * BLK, BLK), 0) % BLK
    kj = lax.broadcasted_iota(jnp.int32, (GRP * BLK, BLK), 1)
    return (kj > qi) & (n > 0), kj <= qi


def _head(hk, g):
    return pl.ds(HD * (GRP * hk + g), HD)


def _stack_heads(ref, hk):
    return jnp.concatenate([ref[:, _head(hk, g)] for g in range(GRP)], axis=0)


def _stack_sinks(s_ref, hk):
    return jnp.concatenate([jnp.full((BLK, 1), s_ref[0, GRP * hk + g], F32) for g in range(GRP)], axis=0)


def attn_fwd(q, k, v, sinks, t, job=None):
    def body(s_ref, q_ref, kp_ref, kc_ref, vp_ref, vc_ref, o_ref):
        n = pl.program_id(0)
        mask_p, mask_c = _masks(n)
        for hk in range(N_KV):
            kv = pl.ds(HD * hk, HD)
            pp, pc, _ = _scores(_stack_heads(q_ref, hk), kp_ref[:, kv], kc_ref[:, kv], mask_p, mask_c,
                                _stack_sinks(s_ref, hk))
            o = (lax.dot_general(pp.astype(BF16), vp_ref[:, kv], NN, preferred_element_type=F32)
                 + lax.dot_general(pc.astype(BF16), vc_ref[:, kv], NN, preferred_element_type=F32))
            for g in range(GRP):
                o_ref[:, _head(hk, g)] = o[g * BLK:(g + 1) * BLK].astype(o_ref.dtype)

    prev = lambda n: (jnp.maximum(n - 1, 0), 0)
    cur = lambda n: (n, 0)
    res, moved = carried_call(
        body, "attn_fwd", (t // BLK,),
        [pl.BlockSpec(memory_space=pltpu.SMEM), pl.BlockSpec((BLK, QW), cur),
         pl.BlockSpec((BLK, KVW), prev), pl.BlockSpec((BLK, KVW), cur),
         pl.BlockSpec((BLK, KVW), prev), pl.BlockSpec((BLK, KVW), cur)],
        [pl.BlockSpec((BLK, QW), cur)], [sds((t, QW), BF16)], [], [sinks, q, k, k, v, v], {}, job)
    return res[0], moved


def attn_bwd(q, k, v, do, sinks, t):
    nb = t // BLK

    def body(s_ref, q_ref, do_ref, kp_ref, kc_ref, vp_ref, vc_ref, dq_ref, dk_ref, dv_ref, ds_ref, dkc, dvc):
        n = pl.program_id(0)

        @pl.when(n == 0)
        def _():
            dkc[...] = jnp.zeros_like(dkc)
            dvc[...] = jnp.zeros_like(dvc)
            ds_ref[...] = jnp.zeros_like(ds_ref)

        @pl.when(n < nb)
        def _():
            mask_p, mask_c = _masks(n)
            lane = lax.broadcasted_iota(jnp.int32, (1, 128), 1)
            dsink = jnp.zeros((1, 128), F32)
            for hk in range(N_KV):
                kv = pl.ds(HD * hk, HD)
                kp, kc, vp, vc = kp_ref[:, kv], kc_ref[:, kv], vp_ref[:, kv], vc_ref[:, kv]
                qs, dos = _stack_heads(q_ref, hk), _stack_heads(do_ref, hk)
                pp, pc, ps = _scores(qs, kp, kc, mask_p, mask_c, _stack_sinks(s_ref, hk))
                dpp = lax.dot_general(dos, vp, NT, preferred_element_type=F32)
                dpc = lax.dot_general(dos, vc, NT, preferred_element_type=F32)
                delta = jnp.sum(pp * dpp, axis=-1, keepdims=True) + jnp.sum(pc * dpc, axis=-1, keepdims=True)
                dsp = (pp * (dpp - delta)).astype(BF16)
                dsc = (pc * (dpc - delta)).astype(BF16)
                sd = ps * delta
                dq = (lax.dot_general(dsp, kp, NN, preferred_element_type=F32)
                      + lax.dot_general(dsc, kc, NN, preferred_element_type=F32))
                for g in range(GRP):
                    rows = slice(g * BLK, (g + 1) * BLK)
                    dsink = dsink + jnp.where(lane == GRP * hk + g, -jnp.sum(sd[rows]), 0.0)
                    dq_ref[:, _head(hk, g)] = dq[rows]
                dk_ref[:, kv] = dkc[:, kv] + lax.dot_general(dsp, qs, TN, preferred_element_type=F32)
                dv_ref[:, kv] = dvc[:, kv] + lax.dot_general(pp.astype(BF16), dos, TN, preferred_element_type=F32)
                dkc[:, kv] = lax.dot_general(dsc, qs, TN, preferred_element_type=F32)
                dvc[:, kv] = lax.dot_general(pc.astype(BF16), dos, TN, preferred_element_type=F32)
            ds_ref[...] += dsink

        @pl.when(n == nb)
        def _():
            dk_ref[...] = dkc[...]
            dv_ref[...] = dvc[...]

    cur = lambda n: (jnp.minimum(n, nb - 1), 0)
    prev = lambda n: (jnp.clip(n - 1, 0, nb - 1), 0)
    return pl.pallas_call(
        body, name="attn_bwd", grid=(nb + 1,),
        in_specs=[pl.BlockSpec(memory_space=pltpu.SMEM), pl.BlockSpec((BLK, QW), cur), pl.BlockSpec((BLK, QW), cur),
                  pl.BlockSpec((BLK, KVW), prev), pl.BlockSpec((BLK, KVW), cur),
                  pl.BlockSpec((BLK, KVW), prev), pl.BlockSpec((BLK, KVW), cur)],
        out_specs=[pl.BlockSpec((BLK, QW), cur), pl.BlockSpec((BLK, KVW), prev), pl.BlockSpec((BLK, KVW), prev),
                   pl.BlockSpec((1, 128), lambda n: (0, 0))],
        out_shape=[sds((t, QW), F32), sds((t, KVW), F32), sds((t, KVW), F32), sds((1, 128), F32)],
        scratch_shapes=[pltpu.VMEM((BLK, KVW), F32), pltpu.VMEM((BLK, KVW), F32)],
        compiler_params=_params(("arbitrary",)),
    )(sinks, q, do, k, k, v, v)


def _adamw(w, g, m, v):
    m2 = B1 * m + (1.0 - B1) * g
    v2 = B2 * v + (1.0 - B2) * jnp.square(g)
    m_hat = m2 / (1.0 - B1 ** STEP)
    v_hat = v2 / (1.0 - B2 ** STEP)
    return -LR * (m_hat / (jnp.sqrt(v_hat) + EPS) + WD * w), m2, v2


def ada_fwd(c16, w_ada, b_sh):
    tn = 512

    def body(c_ref, w_ref, b_ref, o_ref):
        cv = c_ref[...]
        sc = (cv * _sigmoid(cv)).astype(BF16)
        o_ref[...] = lax.dot_general(sc, w_ref[...].astype(BF16), NN, preferred_element_type=F32) + b_ref[...]

    return pl.pallas_call(
        body, name="ada_fwd", grid=(ADA_SH // tn,),
        in_specs=[pl.BlockSpec((16, D), lambda j: (0, 0)), pl.BlockSpec((D, tn), lambda j: (0, j)),
                  pl.BlockSpec((1, tn), lambda j: (0, j))],
        out_specs=pl.BlockSpec((16, tn), lambda j: (0, j)), out_shape=sds((16, ADA_SH), F32),
        compiler_params=_params(("arbitrary",)),
    )(c16, w_ada, b_sh)


def ada_bwd_adam(c16, gm16, w, m, v, job):
    tm, tn = 256, 512

    def body(c_ref, g_ref, w_ref, m_ref, v_ref, go_ref, d_ref, mo_ref, vo_ref):
        cv = c_ref[...]
        sc = (cv * _sigmoid(cv)).astype(BF16)
        g = lax.dot_general(sc, g_ref[...].astype(BF16), TN, preferred_element_type=F32)
        dl, m2, v2 = _adamw(w_ref[...], g, m_ref[...], v_ref[...])
        go_ref[...] = g
        d_ref[...] = dl
        mo_ref[...] = m2
        vo_ref[...] = v2

    blk = pl.BlockSpec((tm, tn), lambda i, j: (i, j))
    return carried_call(
        body, "ada_bwd_adam", (D // tm, ADA_SH // tn),
        [pl.BlockSpec((16, tm), lambda i, j: (0, i)), pl.BlockSpec((16, tn), lambda i, j: (0, j)), blk, blk, blk],
        [blk] * 4, [sds((D, ADA_SH), F32)] * 4, [], [c16, gm16, w, m, v], {}, job)


def adam_rows(name, w, g, m, v, tm):
    rows, cols = w.shape

    def fn(wv, gv, mv, vv):
        gv = gv[:, :cols]
        dl, m2, v2 = _adamw(wv, gv, mv, vv)
        return gv, dl, m2, v2

    return rowmap(name, fn, [T_(w), T_(g), T_(m), T_(v)], [(cols, F32)] * 4, rows=rows, tm=tm)


def adam_small(name, w, g, m, v):
    def body(w_ref, g_ref, m_ref, v_ref, d_ref, mo_ref, vo_ref):
        dl, m2, v2 = _adamw(w_ref[...], g_ref[...], m_ref[...], v_ref[...])
        d_ref[...] = dl
        mo_ref[...] = m2
        vo_ref[...] = v2

    return pl.pallas_call(body, name=name, out_shape=[sds(w.shape, F32)] * 3)(w, g, m, v)


def sum_devices(allv):
    def body(a_ref, o_ref):
        acc = a_ref[0]
        for d in range(1, 8):
            acc = acc + a_ref[d]
        o_ref[...] = acc

    return pl.pallas_call(body, name="sum_devices", out_shape=sds(allv.shape[1:], F32))(allv)


def _ln_fwd(z, g, b):
    mu = jnp.mean(z, axis=-1, keepdims=True)
    zc = z - mu
    var = jnp.mean(jnp.square(zc), axis=-1, keepdims=True)
    return zc * lax.rsqrt(var + LN_EPS) * g + b


def _ln_bwd(z, g, dout):
    mu = jnp.mean(z, axis=-1, keepdims=True)
    zc = z - mu
    var = jnp.mean(jnp.square(zc), axis=-1, keepdims=True)
    rstd = lax.rsqrt(var + LN_EPS)
    xh = zc * rstd
    dxh = dout * g
    dz = rstd * (dxh - jnp.mean(dxh, axis=-1, keepdims=True) - xh * jnp.mean(dxh * xh, axis=-1, keepdims=True))
    return dz, colsum(dout * xh), colsum(dout)


def modulate(name, xin, shift, scale, t):
    return rowmap(name, lambda xv, sh, sc: xv * (1.0 + sc) + sh, [T_(xin), B_(shift), B_(scale)], [(D, BF16)],
                  rows=t, tm=512)


def residual_ln_mod(name, xin, y, gate, lg, lb, wgt, shift_n, scale_n, t):
    def fn(xv, yv, gt, g, b, sh, sc):
        z = ALPHA * xv + (wgt * (1.0 + gt)) * yv
        xo = _ln_fwd(z, g, b)
        return xo, z, xo * (1.0 + sc) + sh

    return rowmap(name, fn, [T_(xin), T_(y), B_(gate), B_(lg), B_(lb), B_(shift_n), B_(scale_n)],
                  [(D, F32), (D, F32), (D, BF16)], rows=t, tm=256)


def residual_ln_bwd(name, z, dnext, y, gate, lg, wgt, t):
    dzn, dun, xn, scn = dnext

    def fn(zv, yv, gt, g, dzv, duv, xv, sc):
        dv = ALPHA * dzv + duv * (1.0 + sc)
        dz, dg, db = _ln_bwd(zv, g, dv)
        return dz, (wgt * (1.0 + gt)) * dz, dg, db, colsum(wgt * dz * yv), colsum(duv), colsum(duv * xv)

    return rowmap(name, fn, [T_(z), T_(y), B_(gate), B_(lg), T_(dzn), T_(dun), T_(xn), B_(scn)],
                  [(D, F32), (D, BF16)], [(1, D)] * 5, rows=t, tm=256)


def residual_ln_loss_bwd(name, xin, y, tgt, gate, lg, lb, wgt, t):
    def fn(xv, yv, tv, gt, g, b):
        z = ALPHA * xv + (wgt * (1.0 + gt)) * yv
        d = _ln_fwd(z, g, b) - tv
        dz, dg, db = _ln_bwd(z, g, d * (1.0 / D))
        return dz, (wgt * (1.0 + gt)) * dz, dg, db, colsum(wgt * dz * yv), jnp.sum(d * d).reshape(1, 1)

    dz, dy, dlg, dlb, dgate, sq = rowmap(
        name, fn, [T_(xin), T_(y), T_(tgt), B_(gate), B_(lg), B_(lb)], [(D, F32), (D, BF16)],
        [(1, D), (1, D), (1, D), (1, 1)], rows=t, tm=256)
    return dz, dy, dlg, dlb, dgate, sq


def modulate_bwd(name, dz, du, xin, scale, t):
    def fn(dzv, duv, xv, sc):
        return ALPHA * dzv + duv * (1.0 + sc), colsum(duv), colsum(duv * xv)

    return rowmap(name, fn, [T_(dz), T_(du), T_(xin), B_(scale)], [(D, F32)], [(1, D), (1, D)], rows=t, tm=256)


def ffn_fwd(tag, u, wi, t, up_job, down_job=None):
    tm = min(1024, t)
    tn = 256
    per = FHP // tn

    def act(accs, _):
        a, b = accs
        s = _sigmoid(a)
        sl = a * s
        return b * (s * (1.0 + a * (1.0 - s))), sl, sl * b

    tmu = min(2048, t)
    hblk = pl.BlockSpec((tmu, tn), lambda i, j, k: (i, j))
    (ha, hb, g), up_moved = mm(
        tag + "_up", [u], [wi, wi], [(0, 0, 0), (0, 1, 1)], dims=NT, grid=(t // tmu, 2 * per, 1),
        a_specs=[pl.BlockSpec((tmu, D), lambda i, j, k: (i, 0))],
        b_specs=[pl.BlockSpec((None, tn, D), lambda i, j, k: (j // per, j % per, 0)),
                 pl.BlockSpec((None, tn, D), lambda i, j, k: (2 + j // per, j % per, 0))],
        outs=[sds((t, 2 * FHP), BF16)] * 3, out_specs=[hblk] * 3, acc_shapes=[(tmu, tn)] * 2, epilogue=act, job=up_job,
        sub_rows=tmu // 2)
    wo = up_moved[0].reshape(2 * FHP, D)
    tk = FHP
    y, down_moved = _with_moved(mm(
        tag + "_down", [g], [wo], [(0, 0, 0)], dims=NN, grid=(t // tm, 2, 2),
        a_specs=[pl.BlockSpec((tm, tk), lambda i, j, k: (i, k))],
        b_specs=[pl.BlockSpec((tk, D // 2), lambda i, j, k: (k, j))],
        outs=[sds((t, D), F32)], out_specs=[pl.BlockSpec((tm, D // 2), lambda i, j, k: (i, j))],
        acc_shapes=[(tm, D // 2)], job=down_job), down_job)
    return ha, hb, g, y, wo, up_moved, down_moved


def ffn_bwd(tag, u, ha, hb, g, dy, wi, wo, t, sp, dact_job=None, dwo_job=None):
    tm = min(1024, t)

    def dact(accs, ex):
        dg = accs[0]
        return dg * ex[0].astype(F32), dg * ex[1].astype(F32)

    tn = 256
    tmu = min(2048, t)
    hblk = pl.BlockSpec((tmu, tn), lambda i, j, k: (i, j))
    (dha, dhb), dact_moved = _with_moved(mm(
        tag + "_dact", [dy], [wo], [(0, 0, 0)], dims=NT, grid=(t // tmu, 2 * FHP // tn, 1),
        a_specs=[pl.BlockSpec((tmu, D), lambda i, j, k: (i, 0))],
        b_specs=[pl.BlockSpec((tn, D), lambda i, j, k: (j, 0))],
        outs=[sds((t, 2 * FHP), BF16)] * 2, out_specs=[hblk] * 2, acc_shapes=[(tmu, tn)],
        epilogue=dact, extras=[ha, hb], extra_specs=[hblk] * 2, job=dact_job, sub_rows=tmu // 2), dact_job)
    tk = min(2048, t)
    th = FHP // 2
    dwo, dwo_moved = _with_moved(mm(
        tag + "_dwo", [g], [dy], [(0, 0, 0)], dims=TN, grid=(4, 2, t // tk),
        a_specs=[pl.BlockSpec((tk, th), lambda i, j, k: (k, i))],
        b_specs=[pl.BlockSpec((tk, D // 2), lambda i, j, k: (k, j))],
        outs=[sds((2 * FHP, D), BF16)], out_specs=[pl.BlockSpec((th, D // 2), lambda i, j, k: (i, j))],
        acc_shapes=[(th, D // 2)], job=dwo_job), dwo_job)
    dwo = dwo.reshape(2, FHP, D)

    def dwi_part(part, dh, carry, job):
        return mm(
            f"{tag}_dwi{part}", [dh], [u], [(0, 0, 0)], dims=TN, grid=(4, 2, t // tk),
            a_specs=[pl.BlockSpec((tk, th), lambda i, j, k: (k, i))],
            b_specs=[pl.BlockSpec((tk, D // 2), lambda i, j, k: (k, j))],
            outs=[sds((4, FHP, D), BF16)],
            out_specs=[pl.BlockSpec((None, th, D // 2), lambda i, j, k: (2 * part + i // 2, i % 2, j))],
            acc_shapes=[(th, D // 2)], carry=carry, job=job)

    dwi, (sib_fo,) = dwi_part(0, dha, None, reduce_sibling_job([(dwo, view_ffn_out, FO, D)]))
    q_fo = chip_sum(tag + "_chipsum_fo", dwo, sib_fo, sp, FO, FO // 2, ffn_out=True)
    dwi, (far_fo,) = dwi_part(1, dhb, dwi, reduce_chips_job([q_fo]))
    (sib_fi,) = run_job(tag + "_sibling_fi", reduce_sibling_job([(dwi, view_lead, FHP, D)]))
    q_fi = chip_sum(tag + "_chipsum_fi", dwi, sib_fi, sp, FHP, FHP // 8)
    tmd = min(512, t)
    du, (far_fi,) = mm(
        tag + "_du", [dha, dhb], [wi, wi], [(0, 0, 0), (1, 1, 0)], dims=NN, grid=(t // tmd, 2, 2),
        a_specs=[pl.BlockSpec((tmd, FHP), lambda i, j, k: (i, k))] * 2,
        b_specs=[pl.BlockSpec((None, FHP, D // 2), lambda i, j, k: (k, 0, j)),
                 pl.BlockSpec((None, FHP, D // 2), lambda i, j, k: (2 + k, 0, j))],
        outs=[sds((t, D), F32)], out_specs=[pl.BlockSpec((tmd, D // 2), lambda i, j, k: (i, j))],
        acc_shapes=[(tmd, D // 2)], job=reduce_chips_job([q_fi]))
    return du, (q_fi, far_fi), (q_fo, far_fo), dact_moved, dwo_moved


def mix_fwd(u, wts, b_in, pool_scale, sinks, tabs, t, in_job, attn_job):
    w_in, wp, wba, wbb, wo = wts
    tm = min(1024, t)
    tmh = min(512, t)
    h, in_moved = mm("mix_in", [u], [w_in], [(0, 0, 0)], dims=NN, grid=(t // tmh, 4, 1),
                     a_specs=[pl.BlockSpec((tmh, D), lambda i, j, k: (i, 0))],
                     b_specs=[pl.BlockSpec((None, D, IN_SH), lambda i, j, k: (j, 0, 0))],
                     outs=[sds((t, IN_W), F32)], out_specs=[pl.BlockSpec((tmh, IN_SH), lambda i, j, k: (i, j))],
                     acc_shapes=[(tmh, IN_SH)], job=in_job)
    attn_job = attn_job(in_moved)
    pooled = pool_fwd(h, b_in, t, 512)
    gblk = pl.BlockSpec((tm, PG), lambda i, j, k: (i, j))
    mixed = mm("mix_pool", [pooled], [wp], [(0, 0, 0)], dims=NN, grid=(t // tm, 4, 1), a_specs=[gblk],
               b_specs=[pl.BlockSpec((None, PG, PG), lambda i, j, k: (j, 0, 0))],
               outs=[sds((t, PW), F32)], out_specs=[gblk], acc_shapes=[(tm, PG)])
    pm = rowmap("mix_pscale", lambda mv, ps: mv * ps, [T_(mixed), B_(pool_scale)], [(PW, BF16)], rows=t, tm=512)

    def branch(name, a, w):
        return mm(name, [a], [w], [(0, 0, 0)], dims=NN, grid=(t // tm, 4, 1),
                  a_specs=[pl.BlockSpec((tm, PW), lambda i, j, k: (i, 0))],
                  b_specs=[pl.BlockSpec((None, PW, D // 4), lambda i, j, k: (j, 0, 0))],
                  outs=[sds((t, D), F32)], out_specs=[pl.BlockSpec((tm, D // 4), lambda i, j, k: (i, j))],
                  acc_shapes=[(tm, D // 4)])

    ya = branch("mix_branch_a", pm, wba)

    def qkv(hq, hk, hv, bq, bk, bv, cc, sa, sb):
        return (_rope(hq + bq, cc, sa, sb) * (HD ** -0.5), _rope(hk + bk, cc, sa, sb), hv + bv)

    qr, kr, vv = rowmap(
        "mix_rope", qkv,
        [T_(h, QW, 1), T_(h, KVW, 8), T_(h, KVW, 9), B_(b_in, QW, 1), B_(b_in, KVW, 8), B_(b_in, KVW, 9),
         T_(tabs[0]), T_(tabs[1]), T_(tabs[2])],
        [(QW, BF16), (KVW, BF16), (KVW, BF16)], rows=t, tm=512)
    attn, attn_moved = attn_fwd(qr, kr, vv, sinks, t, attn_job)
    yb = branch("mix_branch_b", attn, wbb)
    cw = 512

    def merge(ga, gb, ba, bb, yav, ybv):
        return _sigmoid(ga + ba) * yav + _sigmoid(gb + bb) * ybv

    merged = rowmap(
        "mix_merge", merge,
        [T_(h, cw, 5), T_(h, cw, 9), B_(b_in, cw, 5), B_(b_in, cw, 9), T_(ya, cw), T_(yb, cw)],
        [(D, BF16)], rows=t, tm=512, ncol=D // cw)
    y = mm("mix_out", [merged], [wo], [(0, 0, 0)], dims=NN, grid=(t // tm, 2, 1),
           a_specs=[pl.BlockSpec((tm, D), lambda i, j, k: (i, 0))],
           b_specs=[pl.BlockSpec((D, D // 2), lambda i, j, k: (0, j))],
           outs=[sds((t, D), F32)], out_specs=[pl.BlockSpec((tm, D // 2), lambda i, j, k: (i, j))],
           acc_shapes=[(tm, D // 2)])
    return y, (h, pooled, mixed, pm, ya, qr, kr, vv, attn, yb, merged), attn_moved


def mix_bwd(u, saved, dy, wts, b_in, pool_scale, sinks, tabs, t):
    h, pooled, mixed, pm, ya, qr, kr, vv, attn, yb, merged = saved
    w_in, wp, wba, wbb, wo = wts
    tm = min(1024, t)
    tk = min(2048, t)
    dmerged = mm("mix_dmerged", [dy], [wo], [(0, 0, 0)], dims=NT, grid=(t // tm, 2, 1),
                 a_specs=[pl.BlockSpec((tm, D), lambda i, j, k: (i, 0))],
                 b_specs=[pl.BlockSpec((D // 2, D), lambda i, j, k: (j, 0))],
                 outs=[sds((t, D), F32)], out_specs=[pl.BlockSpec((tm, D // 2), lambda i, j, k: (i, j))],
                 acc_shapes=[(tm, D // 2)])
    half = pl.BlockSpec((tk, D // 2), lambda i, j, k: (k, i))
    dwo = mm("mix_dwo", [merged], [dy], [(0, 0, 0)], dims=TN, grid=(2, 2, t // tk), a_specs=[half],
             b_specs=[pl.BlockSpec((tk, D // 2), lambda i, j, k: (k, j))],
             outs=[sds((D, D), BF16)], out_specs=[pl.BlockSpec((D // 2, D // 2), lambda i, j, k: (i, j))],
             acc_shapes=[(D // 2, D // 2)])
    cw = 512

    def dmerge(dm, ga, gb, ba, bb, yav, ybv):
        sa_, sb_ = _sigmoid(ga + ba), _sigmoid(gb + bb)
        dga = dm * yav * sa_ * (1.0 - sa_)
        dgb = dm * ybv * sb_ * (1.0 - sb_)
        return dm * sa_, dm * sb_, dga, dgb, colsum(dga), colsum(dgb)

    dya, dyb, dgla, dglb, dbga, dbgb = rowmap(
        "mix_dmerge", dmerge,
        [T_(dmerged, cw), T_(h, cw, 5), T_(h, cw, 9), B_(b_in, cw, 5), B_(b_in, cw, 9), T_(ya, cw), T_(yb, cw)],
        [(D, BF16)] * 4, [(1, D), (1, D)], rows=t, tm=512, ncol=D // cw)

    def dbranch(name, dyv, act, w):
        dwb = mm(name + "_dw", [act], [dyv], [(0, 0, 0)], dims=TN, grid=(1, 4, t // tk),
                 a_specs=[pl.BlockSpec((tk, PW), lambda i, j, k: (k, 0))],
                 b_specs=[pl.BlockSpec((tk, D // 4), lambda i, j, k: (k, j))],
                 outs=[sds((4, PW, D // 4), BF16)], out_specs=[pl.BlockSpec((None, PW, D // 4), lambda i, j, k: (j, 0, 0))],
                 acc_shapes=[(PW, D // 4)])
        return dwb, lambda dt: mm(
            name + "_dx", [dyv], [w], [(0, 0, 0)], dims=NT, grid=(t // tm, 1, 4),
            a_specs=[pl.BlockSpec((tm, D // 4), lambda i, j, k: (i, k))],
            b_specs=[pl.BlockSpec((None, PW, D // 4), lambda i, j, k: (k, 0, 0))],
            outs=[sds((t, PW), dt)], out_specs=[pl.BlockSpec((tm, PW), lambda i, j, k: (i, 0))], acc_shapes=[(tm, PW)])

    dwba, dpm_fn = dbranch("mix_dbranch_a", dya, pm, wba)
    dwbb, dattn_fn = dbranch("mix_dbranch_b", dyb, attn, wbb)
    dpm, dattn = dpm_fn(F32), dattn_fn(BF16)
    dmixed, dps = rowmap("mix_dpscale", lambda dp, mv, ps: (dp * ps, colsum(dp * mv)),
                         [T_(dpm), T_(mixed), B_(pool_scale)], [(PW, BF16)], [(1, PW)], rows=t, tm=512)
    gblk = pl.BlockSpec((tm, PG), lambda i, j, k: (i, j))
    dpooled = mm("mix_dpool", [dmixed], [wp], [(0, 0, 0)], dims=NT, grid=(t // tm, 4, 1), a_specs=[gblk],
                 b_specs=[pl.BlockSpec((None, PG, PG), lambda i, j, k: (j, 0, 0))],
                 outs=[sds((t, PW), F32)], out_specs=[gblk], acc_shapes=[(tm, PG)])
    kblk = pl.BlockSpec((tk, PG), lambda i, j, k: (k, i))
    dwp = mm("mix_dwpool", [pooled], [dmixed], [(0, 0, 0)], dims=TN, grid=(4, 1, t // tk), a_specs=[kblk], b_specs=[kblk],
             outs=[sds((4, PG, PG), BF16)], out_specs=[pl.BlockSpec((None, PG, PG), lambda i, j, k: (i, 0, 0))],
             acc_shapes=[(PG, PG)])
    dxp, dbxp = pool_bwd(dpooled, t, 512)
    dqr, dkr, dvv, dsinks = attn_bwd(qr, kr, vv, dattn, sinks, t)

    def dqkv(dq, dk, dv, cc, sa, sb):
        dq = _rope_t(dq, cc, sa, sb) * (HD ** -0.5)
        dk = _rope_t(dk, cc, sa, sb)
        return dq, dk, dv, colsum(dq), colsum(dk), colsum(dv)

    dq, dk, dvb, dbq, dbk, dbv = rowmap(
        "mix_rope_bwd", dqkv, [T_(dqr), T_(dkr), T_(dvv), T_(tabs[0]), T_(tabs[1]), T_(tabs[2])],
        [(QW, BF16), (KVW, BF16), (KVW, BF16)], [(1, QW), (1, KVW), (1, KVW)], rows=t, tm=512)
    dh = jnp.concatenate([dxp, dq, dk, dvb, dgla, dglb], axis=1)
    db_in = jnp.concatenate([dbxp, dbq, dbk, dbv, dbga, dbgb], axis=1)
    dwin = mm("mix_dwin", [u], [dh], [(0, 0, 0)], dims=TN, grid=(2, 4, t // tk), a_specs=[half],
              b_specs=[pl.BlockSpec((tk, IN_SH), lambda i, j, k: (k, j))],
              outs=[sds((4, D, IN_SH), BF16)], out_specs=[pl.BlockSpec((None, D // 2, IN_SH), lambda i, j, k: (j, i, 0))],
              acc_shapes=[(D // 2, IN_SH)])
    dwp_sh = jnp.transpose(dwp.reshape(4, 4, 64, PG), (1, 0, 2, 3)).reshape(4, 4 * 64, PG)
    parts = {"win": dwin, "wp": dwp_sh, "wba": dwba, "wbb": dwbb, "wo": dwo.reshape(4, D // 4, D)}
    du, sib = mm("mix_du", [dh], [w_in], [(0, 0, 0)], dims=NT, grid=(t // tm, 2, 4),
                 a_specs=[pl.BlockSpec((tm, IN_SH), lambda i, j, k: (i, k))],
                 b_specs=[pl.BlockSpec((None, D // 2, IN_SH), lambda i, j, k: (k, j, 0))],
                 outs=[sds((t, D), F32)], out_specs=[pl.BlockSpec((tm, D // 2), lambda i, j, k: (i, j))],
                 acc_shapes=[(tm, D // 2)],
                 job=reduce_sibling_job([(p, view_lead, p.shape[1], p.shape[2]) for p in parts.values()]))
    return du, parts, dict(zip(parts, sib)), db_in, dps, dsinks


def cast_shard(name, w, sp, ffn_out=False):
    rows, cols = w.shape
    if ffn_out:
        tm = rows // 2
        shape = (2, FHP, D)
        spec = pl.BlockSpec((None, tm, cols), lambda j, i, s: (s[0] // 2, (s[0] % 2) * 2 + i, 0))
    else:
        tm = rows // 4
        shape = (4, rows, cols)
        spec = pl.BlockSpec((None, tm, cols), lambda j, i, s: (s[0], i, 0))
    return rowmap(name, lambda wv: wv, [T_(w)], [(shape, BF16, spec)], rows=rows, tm=tm, sp=sp)


def cast_ffn_in(name, wt, sp):
    tm = 64
    full = FH // tm

    def fn(_, i, wv):
        return jnp.where(i < full, wv, 0.0)

    return rowmap(name, fn, [X_(wt, pl.BlockSpec((tm, D), lambda j, i, s: (jnp.minimum(i, full - 1), 0)))],
                  [((4, FHP, D), BF16, pl.BlockSpec((None, tm, D), lambda j, i, s: (s[0], i, 0)))],
                  rows=FHP, tm=tm, sp=sp, with_ids=True)


def chip_sum(name, dw, got, sp, rows, tm, ffn_out=False):
    hr, cols = rows // 2, got.shape[2]
    per = hr // tm
    pos = pl.BlockSpec((None, tm, cols), lambda j, i, s: (i // per, i % per, 0))
    if ffn_out:
        mine = pl.BlockSpec((None, tm, cols), lambda j, i, s: (i // 2, (i % 2) * 2 + s[1], 0))
    else:
        mine = pl.BlockSpec((None, tm, cols), lambda j, i, s: (i // per, s[1] * per + i % per, 0))
    return rowmap(name, lambda av, bv: av.astype(F32) + bv.astype(F32), [X_(dw, mine), X_(got, pos)],
                  [(got.shape, BF16, pos)], rows=4 * hr, tm=tm, sp=sp)


def chip_total(name, q, got, sp, rows, tm):
    hr, cols = rows // 2, q.shape[2]
    per = hr // tm

    def part(f):
        return X_(got, pl.BlockSpec((None, tm, cols), lambda j, i, s, f=f: (f, i, 0)))

    return rowmap(
        name, lambda av, b0, b1, b2: ((av.astype(F32) + b0.astype(F32)) + b1.astype(F32)) + b2.astype(F32),
        [X_(q, pl.BlockSpec((None, tm, cols), lambda j, i, s: (s[0], i, 0))), part(0), part(1), part(2)],
        [((rows, cols), F32, pl.BlockSpec((tm, cols), lambda j, i, s: (s[1] * per + i, 0)))], rows=hr, tm=tm, sp=sp)


def kernel(x, c, w_ada, b_ada, ln_g, ln_b, w_ffn1_in, w_ffn1_out, w_in, b_in, w_pool, pool_scale, sinks, w_branch_a, w_branch_b, w_out, w_ffn2_in, w_ffn2_out, loss_target, m_w_ada, m_b_ada, m_ln_g, m_ln_b, m_w_ffn1_in, m_w_ffn1_out, m_w_in, m_b_in, m_w_pool, m_pool_scale, m_sinks, m_w_branch_a, m_w_branch_b, m_w_out, m_w_ffn2_in, m_w_ffn2_out, v_w_ada, v_b_ada, v_ln_g, v_ln_b, v_w_ffn1_in, v_w_ffn1_out, v_w_in, v_b_in, v_w_pool, v_pool_scale, v_sinks, v_w_branch_a, v_w_branch_b, v_w_out, v_w_ffn2_in, v_w_ffn2_out):
    t = x.shape[1]
    xs, tgt = x[0], loss_target[0]
    xi, yi, ci = lax.axis_index("x"), lax.axis_index("y"), lax.axis_index("c")
    chip = 2 * xi + yi
    dev = 2 * chip + ci
    b_in2, ps2, sinks2 = b_in, pool_scale, sinks

    sp = jnp.stack([chip, ci]).astype(jnp.int32)
    tr = lambda a: jnp.swapaxes(a[0], 0, 1)

    first = jnp.concatenate([c.reshape(-1), ln_g.reshape(-1), ln_b.reshape(-1)]).reshape(-1, 128)
    first_all = allgather_small("gather_cond", first).reshape(8, -1)
    c_all = first_all[:, :D]
    ln_parts = first_all[0::2, D:].reshape(4, 2, 3, D // 4)
    ln_full = jnp.transpose(ln_parts, (1, 2, 0, 3)).reshape(2, 3, D)
    lgs = [ln_full[0, s:s + 1] for s in range(3)]
    lbs = [ln_full[1, s:s + 1] for s in range(3)]
    c16 = jnp.pad(c_all, ((0, 8), (0, 0)))
    b_ada_sh = lax.dynamic_slice(b_ada, (0, chip * ADA_SH), (1, ADA_SH))
    mod_part = ada_fwd(c16, w_ada[0], b_ada_sh)[:8]
    mod_all = allgather_small("gather_mod", mod_part.reshape(-1, 128)).reshape(8, 8, ADA_SH)
    mod_mine = lax.dynamic_index_in_dim(mod_all[0::2], dev, axis=1, keepdims=False).reshape(9, D)
    mods = [[mod_mine[3 * s + k:3 * s + k + 1] for k in range(3)] for s in range(3)]

    f1i_buf, f1i_send, f1i_recv, _ = gather_start(
        "gather_f1i_start", cast_ffn_in("cast_f1i", tr(w_ffn1_in), sp), FHP, [mod_mine])
    plain = [("f1o", w_ffn1_out[0]), ("win", w_in[0]), ("wp", w_pool[0].reshape(4 * 64, PG)), ("wba", w_branch_a[0]),
             ("wbb", w_branch_b[0]), ("wo", w_out[0]), ("f2o", w_ffn2_out[0])]
    sh = {n: cast_shard("cast_" + n, w, sp, ffn_out=n in ("f1o", "f2o")) for n, w in plain}
    sh["f1i"] = f1i_buf
    sh["f2i"] = cast_ffn_in("cast_f2i", tr(w_ffn2_in), sp)
    order = ["f1i", "f1o", "win", "wp", "wba", "wbb", "wo", "f2i", "f2o"]
    views = {n: (view_ffn_out if n in ("f1o", "f2o") else view_lead) for n in order}
    shard_rows = {n: (FO if n in ("f1o", "f2o") else sh[n].shape[1]) for n in order}
    shard_cols = {n: sh[n].shape[2] for n in order}
    tiles = {"f1i": FHP // 8, "f1o": FO // 2, "win": 512, "wp": 128, "wba": 512, "wbb": 512, "wo": 256,
             "f2i": FHP // 8, "f2o": FO // 2}

    def item(n, part=0, parts=1):
        return (sh[n], views[n], shard_rows[n], part, parts)

    tabs = rope_tables(t)
    (sh0, sc0, gt0), (sh1, sc1, gt1), (sh2, sc2, gt2) = mods

    u0 = modulate("ffn1_mod", xs, sh0, sc0, t)
    landed = gather_wait("gather_f1i_wait", f1i_buf, f1i_send, f1i_recv, FHP,
                         [u0] + [sh[n] for n in order if n != "f1i"])
    (g_f1i,) = run_job("gather_f1i_forward", forward_job(landed, FHP))
    ha1, hb1, g1, y1, f1o, (_, g_win), (g_wp, g_wba, g_wbb, g_wo) = ffn_fwd(
        "ffn1", u0, g_f1i, t, gather_job([item("f1o"), item("win")]),
        gather_job([item(n) for n in ("wp", "wba", "wbb", "wo")]))
    x1, z1, u1 = residual_ln_mod("ffn1_ln", xs, y1, gt0, lgs[0], lbs[0], 0.5, sh1, sc1, t)
    wp_full = jnp.transpose(g_wp.reshape(4, 4, 64, PG), (1, 0, 2, 3)).reshape(4, PG, PG)
    wts = (g_win, wp_full, g_wba, g_wbb, g_wo.reshape(D, D))
    y2, sv2, (g_f2i,) = mix_fwd(
        u1, wts, b_in2, ps2, sinks2, tabs, t, gather_job([item("f2i", 0, 2)]),
        lambda moved: gather_job([(moved[0], view_lead, FHP, 1, 2)]))
    x2, z2, u2 = residual_ln_mod("mix_ln", x1, y2, gt1, lgs[1], lbs[1], 1.0, sh2, sc2, t)
    ha3, hb3, g3, y3, f2o, _, _ = ffn_fwd("ffn2", u2, g_f2i, t, gather_job([item("f2o")]))

    dz3, dy3, dlg2, dlb2, dgt2, sq = residual_ln_loss_bwd("ffn2_ln_loss", x2, y3, tgt, gt2, lgs[2], lbs[2], 0.5, t)
    loss = lax.psum(0.5 * sq[0, 0] / D, ("x", "y", "c"))
    du3, red_f2i, red_f2o, _, _ = ffn_bwd("ffn2", u2, ha3, hb3, g3, dy3, g_f2i, f2o, t, sp)
    dz2, dy2, dlg1, dlb1, dgt1, dsh2, dsc2 = residual_ln_bwd("mix_ln_bwd", z2, (dz3, du3, x2, sc2), y2, gt1, lgs[1], 1.0, t)
    du2, mix_parts, sib, db_in, dps, dsinks = mix_bwd(u1, sv2, dy2, wts, b_in2, ps2, sinks2, tabs, t)
    q = {n: chip_sum("chipsum_" + n, mix_parts[n], sib[n], sp, shard_rows[n], tiles[n]) for n in mix_parts}
    dz1, dy1, dlg0, dlb0, dgt0, dsh1, dsc1 = residual_ln_bwd("ffn1_ln_bwd", z1, (dz2, du2, x1, sc1), y1, gt0, lgs[0], 0.5, t)
    du1, red_f1i, red_f1o, far_a, far_b = ffn_bwd(
        "ffn1", u0, ha1, hb1, g1, dy1, g_f1i, f1o, t, sp,
        reduce_chips_job([q["win"], q["wp"]]), reduce_chips_job([q["wo"], q["wba"], q["wbb"]]))
    dx0, dsh0, dsc0 = modulate_bwd("ffn1_mod_bwd", dz1, du1, xs, sc0, t)
    gm0, gm1, gm2 = (dsh0, dsc0, dgt0), (dsh1, dsc1, dgt1), (dsh2, dsc2, dgt2)
    reduced = {"f1i": red_f1i, "f1o": red_f1o, "f2i": red_f2i, "f2o": red_f2o, "win": (q["win"], far_a[0]),
               "wp": (q["wp"], far_a[1]), "wo": (q["wo"], far_b[0]), "wba": (q["wba"], far_b[1]), "wbb": (q["wbb"], far_b[2])}
    halves = [chip_total("total_" + n, *reduced[n], sp, shard_rows[n], tiles[n]) for n in order]

    small = jnp.concatenate([*gm0, *gm1, *gm2, dlg0, dlg1, dlg2, dlb0, dlb1, dlb2, db_in, dps, dsinks], axis=1)
    n_small = small.shape[1]
    rows_small = -(-n_small // 1024) * 8
    small = jnp.pad(small, ((0, 0), (0, rows_small * 128 - n_small))).reshape(rows_small, 128)
    small_all = allgather_small("gather_small", small)
    tot = sum_devices(small_all).reshape(1, -1)
    gmod_all = small_all.reshape(8, -1)[:, :9 * D]
    o = 9 * D
    g_b_ada = tot[:, :o]
    g_ln_g = lax.dynamic_slice(tot[:, o:o + 3 * D].reshape(3, D), (0, chip * (D // 4)), (3, D // 4))
    g_ln_b = lax.dynamic_slice(tot[:, o + 3 * D:o + 6 * D].reshape(3, D), (0, chip * (D // 4)), (3, D // 4))
    o += 6 * D
    g_b_in, g_ps, g_sinks = tot[:, o:o + IN_W], tot[:, o + IN_W:o + IN_W + PW], tot[:, o + IN_W + PW:o + IN_W + PW + N_Q]

    gm16 = jnp.pad(lax.dynamic_slice(gmod_all, (0, chip * ADA_SH), (8, ADA_SH)), ((0, 8), (0, 0)))
    (g_w_ada, d_w_ada, nm_w_ada, nv_w_ada), _ = ada_bwd_adam(c16, gm16, w_ada[0], m_w_ada[0], v_w_ada[0], None)
    gw = dict(zip(order, run_job("share_halves", share_halves_job(halves))))

    def big(n, w, m, v, tm):
        shape = w.shape
        w2, m2, v2 = (a.reshape(shape[-2] if a.ndim == 3 else -1, shape[-1]) for a in (w, m, v))
        return [r.reshape(shape) for r in adam_rows("adam_" + n, w2, gw[n], m2, v2, tm)]

    def big_t(n, w, m, v):
        return [jnp.swapaxes(r, 0, 1)[None] for r in adam_rows("adam_" + n, tr(w), gw[n], tr(m), tr(v), 64)]

    def tiny(n, w, g, m, v):
        return [g.reshape(w.shape)] + list(adam_small("adam_" + n, w, g.reshape(w.shape), m, v))

    res = {
        "w_ada": [a[None] for a in (g_w_ada, d_w_ada, nm_w_ada, nv_w_ada)],
        "b_ada": tiny("b_ada", b_ada, g_b_ada, m_b_ada, v_b_ada),
        "ln_g": tiny("ln_g", ln_g, g_ln_g, m_ln_g, v_ln_g),
        "ln_b": tiny("ln_b", ln_b, g_ln_b, m_ln_b, v_ln_b),
        "w_ffn1_in": big_t("f1i", w_ffn1_in, m_w_ffn1_in, v_w_ffn1_in),
        "w_ffn1_out": big("f1o", w_ffn1_out, m_w_ffn1_out, v_w_ffn1_out, 32),
        "w_in": big("win", w_in, m_w_in, v_w_in, 256),
        "b_in": tiny("b_in", b_in, g_b_in, m_b_in, v_b_in),
        "w_pool": big("wp", w_pool, m_w_pool, v_w_pool, 256),
        "pool_scale": tiny("pool_scale", pool_scale, g_ps, m_pool_scale, v_pool_scale),
        "sinks": tiny("sinks", sinks, g_sinks, m_sinks, v_sinks),
        "w_branch_a": big("wba", w_branch_a, m_w_branch_a, v_w_branch_a, 512),
        "w_branch_b": big("wbb", w_branch_b, m_w_branch_b, v_w_branch_b, 512),
        "w_out": big("wo", w_out, m_w_out, v_w_out, 128),
        "w_ffn2_in": big_t("f2i", w_ffn2_in, m_w_ffn2_in, v_w_ffn2_in),
        "w_ffn2_out": big("f2o", w_ffn2_out, m_w_ffn2_out, v_w_ffn2_out, 32),
    }
    names = ["w_ada", "b_ada", "ln_g", "ln_b", "w_ffn1_in", "w_ffn1_out", "w_in", "b_in", "w_pool", "pool_scale", "sinks",
             "w_branch_a", "w_branch_b", "w_out", "w_ffn2_in", "w_ffn2_out"]
    return (loss, dx0[None], *[res[n][0] for n in names], *[res[n][1] for n in names],
            *[res[n][2] for n in names], *[res[n][3] for n in names])
```

```python
import jax
import jax.numpy as jnp
from jax import lax
from jax.experimental import pallas as pl
from jax.experimental.pallas import tpu as pltpu

F32 = jnp.float32
BF16 = jnp.bfloat16
MESH = pl.DeviceIdType.MESH
ANY = pl.BlockSpec(memory_space=pl.ANY)

D = 2048
N_Q, N_KV, HD = 16, 4, 64
QW, KVW = N_Q * HD, N_KV * HD
BLK = 128
POOL_WINDOWS = (2, 4, 8, 16)
PW, PG = 1024, 256
HALO = 16
ROPE_THETA = 500000.0
ROT = HD // 4
LN_EPS = 1e-5
ALPHA = 2.0 ** 0.25
FH = 2752
FHP = 2816
FO = 1376
IN_W = 6656
IN_SH = IN_W // 4
ADA_SH = 18432 // 4
B1, B2, LR, EPS, WD, STEP = 0.9, 0.999, 0.001, 1e-08, 0.01, 10
VMEM_LIMIT = 56 * 1024 * 1024
FLIPS = ((1, 0), (0, 1), (1, 1))
NN = (((1,), (0,)), ((), ()))
NT = (((1,), (1,)), ((), ()))
TN = (((0,), (0,)), ((), ()))


def _params(sem):
    return pltpu.CompilerParams(dimension_semantics=sem, vmem_limit_bytes=VMEM_LIMIT)


def _aligned(v, m):
    return v if isinstance(v, int) else pl.multiple_of(v, m)


def _sigmoid(v):
    return 1.0 / (1.0 + jnp.exp(-v))


def T_(arr, width=None, off=0):
    return ("t", arr, width, off)


def B_(arr, width=None, off=0):
    return ("b", arr, width, off)


def X_(arr, spec):
    return ("x", arr, spec, 0)


def rowmap(name, fn, ins, outs, accs=(), *, rows, tm, ncol=1, with_ids=False, sp=None, alias=None):
    tm = min(tm, rows)
    nrow = rows // tm
    in_specs, arrs = [], []
    for kind, arr, width, off in ins:
        if kind == "x":
            in_specs.append(width)
        elif kind == "t":
            w = arr.shape[1] if width is None else width
            in_specs.append(pl.BlockSpec((tm, w), lambda j, i, *_, off=off: (i, off + j)))
        else:
            w = arr.shape[1] if width is None else width
            in_specs.append(pl.BlockSpec((arr.shape[0], w), lambda j, i, *_, off=off: (0, off + j)))
        arrs.append(arr)
    out_shape, out_specs = [], []
    for o in outs:
        if len(o) == 3:
            out_shape.append(jax.ShapeDtypeStruct(o[0], o[1]))
            out_specs.append(o[2])
        else:
            out_shape.append(jax.ShapeDtypeStruct((rows, o[0]), o[1]))
            out_specs.append(pl.BlockSpec((tm, o[0] // ncol), lambda j, i, *_: (i, j)))
    for r, width in accs:
        out_shape.append(jax.ShapeDtypeStruct((r, width), F32))
        out_specs.append(pl.BlockSpec((r, width // ncol), lambda j, i, *_: (0, j)))
    ni, no = len(ins), len(outs)
    nsp = 0 if sp is None else 1

    def body(*refs):
        refs = refs[nsp:]
        i = pl.program_id(1)
        vals = [r[...] for r in refs[:ni]]
        res = fn(pl.program_id(0), i, *vals) if with_ids else fn(*vals)
        if not isinstance(res, (tuple, list)):
            res = (res,)
        for r, v in zip(refs[ni:ni + no], res[:no]):
            r[...] = v.astype(r.dtype)
        for r, v in zip(refs[ni + no:], res[no:]):
            @pl.when(i == 0)
            def _(r=r, v=v):
                r[...] = v

            @pl.when(i > 0)
            def _(r=r, v=v):
                r[...] += v

    grid_spec = pltpu.PrefetchScalarGridSpec(num_scalar_prefetch=nsp, grid=(ncol, nrow), in_specs=in_specs,
                                             out_specs=out_specs)
    res = pl.pallas_call(
        body, name=name, grid_spec=grid_spec, out_shape=out_shape,
        input_output_aliases={nsp + k: v for k, v in (alias or {}).items()},
        compiler_params=_params(("arbitrary", "arbitrary")),
    )(*([sp] if nsp else []), *arrs)
    return res[0] if len(res) == 1 else res


def colsum(v):
    return jnp.sum(v, axis=0, keepdims=True)


def mm(name, a_ops, b_ops, ops, *, dims, grid, a_specs, b_specs, outs, out_specs, acc_shapes,
       epilogue=None, extras=(), extra_specs=(), carry=None, job=None, sub_rows=None, stacked=False):
    gk = grid[2]
    na, nb, ne, nacc = len(a_ops), len(b_ops), len(extras), len(acc_shapes)
    nc = 0 if carry is None else 1
    no = len(outs)

    def body(*refs):
        a_refs = refs[:na]
        b_refs = refs[na:na + nb]
        e_refs = refs[na + nb:na + nb + ne]
        o_refs = refs[na + nb + ne + nc:na + nb + ne + nc + no]
        acc_refs = refs[na + nb + ne + nc + no:]
        k = pl.program_id(2)

        def partials(rows=slice(None)):
            res = [None] * nacc
            for ai, bi, ci in ops:
                p = lax.dot_general(a_refs[ai][rows], b_refs[bi][...], dims, preferred_element_type=F32)
                res[ci] = p if res[ci] is None else res[ci] + p
            return res

        def finish(accs, rows=slice(None)):
            outv = epilogue(accs, [e[rows] for e in e_refs]) if epilogue else (accs[0],)
            if stacked:
                for n, v in enumerate(outv):
                    o_refs[0][n, rows] = v.astype(o_refs[0].dtype)
                return
            for o, v in zip(o_refs, outv):
                o[rows] = v.astype(o.dtype)

        if gk == 1 and sub_rows:
            for s in range(out_specs[0].block_shape[-2] // sub_rows):
                rows = pl.ds(s * sub_rows, sub_rows)
                finish(partials(rows), rows)
        elif gk == 1:
            finish(partials())
        else:
            ps = partials()

            @pl.when(k == 0)
            def _():
                for acc, p in zip(acc_refs, ps):
                    acc[...] = p

            @pl.when((k > 0) & (k < gk - 1))
            def _():
                for acc, p in zip(acc_refs, ps):
                    acc[...] += p

            @pl.when(k == gk - 1)
            def _():
                finish([acc[...] + p for acc, p in zip(acc_refs, ps)])

    res, moved = carried_call(
        body, name, grid,
        list(a_specs) + list(b_specs) + list(extra_specs) + ([ANY] if nc else []), list(out_specs), list(outs),
        [pltpu.VMEM(s, F32) for s in acc_shapes] if gk > 1 else [],
        [*a_ops, *b_ops, *extras, *([carry] if nc else [])], {na + nb + ne: 0} if nc else {}, job)
    res = res[0] if len(res) == 1 else res
    return res if job is None else (res, moved)


def sds(shape, dt):
    return jax.ShapeDtypeStruct(shape, dt)


class Job:
    def __init__(self, ins, outs, aliases, scratch, start, mid, finish):
        self.ins, self.outs, self.aliases, self.scratch = list(ins), list(outs), dict(aliases), list(scratch)
        self.start, self.mid, self.finish = start, mid, finish


def carried_call(body, name, grid, in_specs, out_specs, out_shape, scratch, args, aliases, job, mid_at=0.9):
    sem = ("arbitrary",) * len(grid)
    if job is None:
        res = pl.pallas_call(body, name=name, grid=grid, in_specs=in_specs, out_specs=out_specs, out_shape=out_shape,
                             scratch_shapes=scratch, input_output_aliases=aliases, compiler_params=_params(sem))(*args)
        return list(res), []
    ni, no, ns = len(in_specs), len(out_specs), len(scratch)
    ci, co = len(job.ins), len(job.outs)
    total = 1
    for g in grid:
        total *= g
    mid_step = min(max(int(total * mid_at), 1), total - 1)

    def full(*refs):
        ins, cins = refs[:ni], refs[ni:ni + ci]
        outs, couts = refs[ni + ci:ni + ci + no], refs[ni + ci + no:ni + ci + no + co]
        scr, cscr = refs[ni + ci + no + co:ni + ci + no + co + ns], refs[ni + ci + no + co + ns:]
        step = 0
        for d, g in enumerate(grid):
            step = step * g + pl.program_id(d)

        @pl.when(step == 0)
        def _():
            job.start(cins, couts, cscr)

        body(*ins, *outs, *scr)

        @pl.when(step == mid_step)
        def _():
            job.mid(cins, couts, cscr)

        @pl.when(step == total - 1)
        def _():
            job.finish(cins, couts, cscr)

    al = dict(aliases)
    al.update({ni + k: no + v for k, v in job.aliases.items()})
    res = pl.pallas_call(
        full, name=name, grid=grid, in_specs=in_specs + [ANY] * ci, out_specs=out_specs + [ANY] * co,
        out_shape=out_shape + job.outs, scratch_shapes=scratch + job.scratch, input_output_aliases=al,
        compiler_params=_params(sem))(*args, *job.ins)
    return list(res[:no]), list(res[no:])


def _with_moved(res, job):
    return res if job is not None else (res, [])


def run_job(name, job):
    ci = len(job.ins)

    def body(*refs):
        cins, couts, cscr = refs[:ci], refs[ci:ci + len(job.outs)], refs[ci + len(job.outs):]
        job.start(cins, couts, cscr)
        job.mid(cins, couts, cscr)
        job.finish(cins, couts, cscr)

    return list(pl.pallas_call(
        body, name=name, in_specs=[ANY] * ci, out_specs=[ANY] * len(job.outs), out_shape=job.outs,
        scratch_shapes=job.scratch, input_output_aliases=job.aliases)(*job.ins))


def _place():
    x, y, c = lax.axis_index("x"), lax.axis_index("y"), lax.axis_index("c")
    chips = [((1 - x) if fx else x, (1 - y) if fy else y) for fx, fy in FLIPS]
    return x, y, c, chips


def allgather_small(name, v):
    r = v.shape[0]

    def body(x_ref, out_ref, send_sems, recv_sems, local_sem):
        x, y, c, chips = _place()
        me, sibling = (x, y, c), (x, y, 1 - c)

        def rows(px, py, pc):
            return out_ref.at[4 * px + 2 * py + pc]

        def copy(k, block, to, src=None):
            return pltpu.make_async_remote_copy(
                src_ref=rows(*block) if src is None else src, dst_ref=rows(*block),
                send_sem=send_sems.at[k], recv_sem=recv_sems.at[k], device_id=to, device_id_type=MESH)

        mine = pltpu.make_async_copy(x_ref, rows(*me), local_sem)
        mine.start()
        first = [copy(0, me, sibling, src=x_ref)]
        first += [copy(1 + j, me, (*chip, c), src=x_ref) for j, chip in enumerate(chips)]
        for cp in first:
            cp.start()
        passed = [copy(4 + j, (*chip, c), sibling) for j, chip in enumerate(chips)]
        for j, chip in enumerate(chips):
            copy(1 + j, (*chip, c), me).wait_recv()
            passed[j].start()
        copy(0, sibling, me).wait_recv()
        for j, chip in enumerate(chips):
            copy(4 + j, (*chip, 1 - c), me).wait_recv()
        for cp in first + passed:
            cp.wait_send()
        mine.wait()

    return pl.pallas_call(
        body, name=name, out_shape=sds((8, r, 128), v.dtype),
        in_specs=[pl.BlockSpec(memory_space=pltpu.VMEM)], out_specs=pl.BlockSpec(memory_space=pltpu.VMEM),
        scratch_shapes=[pltpu.SemaphoreType.DMA((7,)), pltpu.SemaphoreType.DMA((7,)), pltpu.SemaphoreType.DMA],
    )(v)


def _half(ref, rows, hf):
    hr = rows // 2
    return ref.at[pl.ds(_aligned(hf * hr, 16), hr)]


def view_lead(ref, p):
    return ref.at[p]


def view_ffn_out(ref, p):
    return ref.at[p // 2, pl.ds(_aligned((p % 2) * FO, 16), FO)]


def _remote(ref, dst, send_sems, recv_sems, idx, to):
    return pltpu.make_async_remote_copy(src_ref=ref, dst_ref=dst, send_sem=send_sems.at[idx], recv_sem=recv_sems.at[idx],
                                        device_id=to, device_id_type=MESH)


def gather_job(items):
    nw = len(items)
    pads = [w for w, it in enumerate(items) if it[1] is view_ffn_out]

    def piece(ref, w, p, hf):
        _, view, rws, part, parts = items[w]
        pr = rws // 2 // parts
        return view(ref, p).at[pl.ds(_aligned(hf * (rws // 2) + part * pr, 16), pr)]

    def pad_copies(outs, scr):
        return [pltpu.make_async_copy(scr[2], outs[w].at[h, pl.ds(2 * FO, FHP - 2 * FO)], scr[3].at[2 * n + h])
                for n, w in enumerate(pads) for h in range(2)]

    def start(_, outs, scr):
        x, y, c, chips = _place()
        if pads:
            scr[2][...] = jnp.zeros_like(scr[2])
            for cp in pad_copies(outs, scr):
                cp.start()
        for w in range(nw):
            mine = piece(outs[w], w, 2 * x + y, c)
            for f, (px, py) in enumerate(chips):
                _remote(mine, mine, scr[0], scr[1], (w, f), (px, py, c)).start()

    def mid(_, outs, scr):
        x, y, c, chips = _place()
        for w in range(nw):
            for f, (px, py) in enumerate(chips):
                land = piece(outs[w], w, 2 * px + py, c)
                _remote(land, land, scr[0], scr[1], (w, f), (px, py, c)).wait_recv()
                _remote(land, land, scr[0], scr[1], (w, 3 + f), (x, y, 1 - c)).start()

    def finish(_, outs, scr):
        x, y, c, chips = _place()
        for w in range(nw):
            for f, (px, py) in enumerate(chips):
                land = piece(outs[w], w, 2 * px + py, 1 - c)
                _remote(land, land, scr[0], scr[1], (w, 3 + f), (x, y, 1 - c)).wait_recv()
        for w in range(nw):
            mine = piece(outs[w], w, 2 * x + y, c)
            for f in range(6):
                _remote(mine, mine, scr[0], scr[1], (w, f), (x, y, 1 - c)).wait_send()
        for cp in pad_copies(outs, scr):
            cp.wait()

    scratch = [pltpu.SemaphoreType.DMA((nw, 6)), pltpu.SemaphoreType.DMA((nw, 6))]
    if pads:
        scratch += [pltpu.VMEM((FHP - 2 * FO, D), BF16), pltpu.SemaphoreType.DMA((2 * len(pads),))]
    bufs = [it[0] for it in items]
    return Job(bufs, [sds(b.shape, BF16) for b in bufs], {w: w for w in range(nw)}, scratch, start, mid, finish)


HBM = pl.BlockSpec(memory_space=pltpu.HBM)
SEM = pl.BlockSpec(memory_space=pltpu.SEMAPHORE)
SPLIT = pltpu.CompilerParams(has_side_effects=pltpu.SideEffectType.DATAFLOW_SIDE_EFFECTING)


def gather_start(name, buf, rows, after):
    def body(*refs):
        out, send_sems, recv_sems, token = refs[1 + len(after):]
        x, y, c, chips = _place()
        mine = _half(out.at[2 * x + y], rows, c)
        for f, (px, py) in enumerate(chips):
            _remote(mine, mine, send_sems, recv_sems, f, (px, py, c)).start()
        token[...] = jnp.zeros_like(token)

    return pl.pallas_call(
        body, name=name,
        out_shape=(pltpu.HBM(buf.shape, buf.dtype), pltpu.SemaphoreType.DMA((3,)), pltpu.SemaphoreType.DMA((3,)),
                   sds((8, 128), F32)),
        in_specs=(HBM,) + (ANY,) * len(after), out_specs=(HBM, SEM, SEM, pl.BlockSpec(memory_space=pltpu.VMEM)),
        input_output_aliases={0: 0}, compiler_params=SPLIT)(pltpu.with_memory_space_constraint(buf, pltpu.HBM), *after)


def gather_wait(name, buf, send_sems, recv_sems, rows, after):
    def body(_, send_sems, recv_sems, *rest):
        out = rest[-1]
        x, y, c, chips = _place()
        mine = _half(out.at[2 * x + y], rows, c)
        for f, (px, py) in enumerate(chips):
            cp = _remote(mine, _half(out.at[2 * px + py], rows, c), send_sems, recv_sems, f, (px, py, c))
            cp.wait_send()
            cp.wait_recv()

    return pl.pallas_call(
        body, name=name, out_shape=pltpu.HBM(buf.shape, buf.dtype),
        in_specs=(HBM, SEM, SEM) + (ANY,) * len(after), out_specs=HBM, input_output_aliases={0: 0},
        compiler_params=SPLIT)(buf, send_sems, recv_sems, *after)


def forward_job(buf, rows):
    def copies(outs, scr, hf):
        x, y, c, chips = _place()
        half = c if hf == 0 else 1 - c
        return [_remote(_half(outs[0].at[2 * px + py], rows, half), _half(outs[0].at[2 * px + py], rows, half),
                        scr[0], scr[1], f, (x, y, 1 - c)) for f, (px, py) in enumerate(chips)]

    def start(_, outs, scr):
        for cp in copies(outs, scr, 0):
            cp.start()

    def finish(_, outs, scr):
        for cp in copies(outs, scr, 1):
            cp.wait_recv()
        for cp in copies(outs, scr, 0):
            cp.wait_send()

    return Job([buf], [sds(buf.shape, buf.dtype)], {0: 0},
               [pltpu.SemaphoreType.DMA((3,)), pltpu.SemaphoreType.DMA((3,))], start, lambda *_: None, finish)


def reduce_sibling_job(items):
    nw = len(items)

    def copies(ins, got, scr):
        x, y, c, _ = _place()
        return [_remote(_half(view(ins[w], p), rws, 1 - c), got[w].at[p], scr[0], scr[1], (w, p), (x, y, 1 - c))
                for w, (_, view, rws, _) in enumerate(items) for p in range(4)]

    def start(ins, got, scr):
        for cp in copies(ins, got, scr):
            cp.start()

    def finish(ins, got, scr):
        for cp in copies(ins, got, scr):
            cp.wait()

    return Job([it[0] for it in items], [sds((4, it[2] // 2, it[3]), BF16) for it in items], {},
               [pltpu.SemaphoreType.DMA((nw, 4)), pltpu.SemaphoreType.DMA((nw, 4))], start, lambda *_: None, finish)


def reduce_chips_job(qs):
    nw = len(qs)

    def copies(ins, got, scr):
        x, y, c, chips = _place()
        return [_remote(ins[w].at[2 * px + py], got[w].at[f], scr[0], scr[1], (w, f), (px, py, c))
                for w in range(nw) for f, (px, py) in enumerate(chips)]

    def start(ins, got, scr):
        for cp in copies(ins, got, scr):
            cp.start()

    def finish(ins, got, scr):
        for cp in copies(ins, got, scr):
            cp.wait()

    return Job(qs, [sds((3,) + q.shape[1:], BF16) for q in qs], {},
               [pltpu.SemaphoreType.DMA((nw, 3)), pltpu.SemaphoreType.DMA((nw, 3))], start, lambda *_: None, finish)


def share_halves_job(gs):
    nw = len(gs)

    def start(_, outs, scr):
        x, y, c, _ = _place()
        for w in range(nw):
            mine = _half(outs[w], gs[w].shape[0], c)
            _remote(mine, mine, scr[0], scr[1], w, (x, y, 1 - c)).start()

    def finish(_, outs, scr):
        x, y, c, _ = _place()
        for w in range(nw):
            mine = _half(outs[w], gs[w].shape[0], c)
            theirs = _half(outs[w], gs[w].shape[0], 1 - c)
            _remote(mine, mine, scr[0], scr[1], w, (x, y, 1 - c)).wait_send()
            _remote(theirs, theirs, scr[0], scr[1], w, (x, y, 1 - c)).wait_recv()

    return Job(gs, [sds(g.shape, F32) for g in gs], {w: w for w in range(nw)},
               [pltpu.SemaphoreType.DMA((nw,)), pltpu.SemaphoreType.DMA((nw,))], start, lambda *_: None, finish)


def rope_tables(t):
    pos = jnp.arange(t, dtype=F32)
    inv_freq = ROPE_THETA ** (-jnp.arange(0, ROT, 2, dtype=F32) / ROT)
    ang = pos[:, None] * inv_freq[None, :]
    cos, sin = jnp.cos(ang), jnp.sin(ang)
    d = jnp.arange(128) % HD
    half = ROT // 2
    cs = jnp.take(cos, d % half, axis=1)
    sn = jnp.take(sin, d % half, axis=1)
    cc = jnp.where(d[None] < ROT, cs, 1.0)
    sa = jnp.where(d[None] < half, -sn, 0.0)
    sb = jnp.where((d[None] >= half) & (d[None] < ROT), sn, 0.0)
    return cc, sa, sb


def _rope(v, cc, sa, sb):
    w = v.shape[1]
    reps = w // 128
    half = ROT // 2
    return (v * jnp.tile(cc, (1, reps)) + pltpu.roll(v, w - half, 1) * jnp.tile(sa, (1, reps))
            + pltpu.roll(v, half, 1) * jnp.tile(sb, (1, reps)))


def _rope_t(dv, cc, sa, sb):
    w = dv.shape[1]
    reps = w // 128
    half = ROT // 2
    return (dv * jnp.tile(cc, (1, reps)) + pltpu.roll(dv * jnp.tile(sa, (1, reps)), half, 1)
            + pltpu.roll(dv * jnp.tile(sb, (1, reps)), w - half, 1))


def pool_fwd(h, b_in, t, tm):
    tm = min(tm, t)
    per = tm // HALO

    def body(prev_ref, cur_ref, b_ref, o_ref, xx):
        i = pl.program_id(0)
        b = b_ref[...]
        xx[pl.ds(0, HALO), :] = jnp.where(i > 0, prev_ref[...] + b, 0.0)
        xx[pl.ds(HALO, tm), :] = cur_ref[...] + b
        tpos = i * tm + lax.broadcasted_iota(jnp.int32, (tm, PG), 0) + 1
        for gi, w in enumerate(POOL_WINDOWS):
            cols = pl.ds(gi * PG, PG)
            acc = xx[pl.ds(HALO, tm), cols]
            for s in range(1, w):
                acc = acc + xx[pl.ds(HALO - s, tm), cols]
            cnt = jnp.minimum(tpos, w).astype(F32)
            o_ref[:, cols] = (acc / cnt - xx[pl.ds(HALO, tm), cols]).astype(o_ref.dtype)

    return pl.pallas_call(
        body, name="pool_fwd", grid=(t // tm,),
        in_specs=[pl.BlockSpec((HALO, PW), lambda i: (jnp.maximum(i * per - 1, 0), 0)),
                  pl.BlockSpec((tm, PW), lambda i: (i, 0)), pl.BlockSpec((1, PW), lambda i: (0, 0))],
        out_specs=pl.BlockSpec((tm, PW), lambda i: (i, 0)), out_shape=sds((t, PW), BF16),
        scratch_shapes=[pltpu.VMEM((tm + HALO, PW), F32)], compiler_params=_params(("arbitrary",)),
    )(h, h, b_in)


def pool_bwd(dpooled, t, tm):
    tm = min(tm, t)
    per = tm // HALO
    nt = t // tm

    def body(cur_ref, nxt_ref, o_ref, db_ref, ee):
        i = pl.program_id(0)
        tpos = i * tm + lax.broadcasted_iota(jnp.int32, (tm, PG), 0) + 1
        for gi, w in enumerate(POOL_WINDOWS):
            cols = pl.ds(gi * PG, PG)
            ee[pl.ds(0, tm), cols] = cur_ref[:, cols] / jnp.minimum(tpos, w).astype(F32)
            ee[pl.ds(tm, HALO), cols] = jnp.where(i < nt - 1, nxt_ref[:, cols] / float(w), 0.0)
        for gi, w in enumerate(POOL_WINDOWS):
            cols = pl.ds(gi * PG, PG)
            acc = ee[pl.ds(0, tm), cols]
            for s in range(1, w):
                acc = acc + ee[pl.ds(s, tm), cols]
            dxp = acc - cur_ref[:, cols]
            o_ref[:, cols] = dxp.astype(o_ref.dtype)
            part = colsum(dxp)

            @pl.when(i == 0)
            def _(cols=cols, part=part):
                db_ref[:, cols] = part

            @pl.when(i > 0)
            def _(cols=cols, part=part):
                db_ref[:, cols] += part

    return pl.pallas_call(
        body, name="pool_bwd", grid=(nt,),
        in_specs=[pl.BlockSpec((tm, PW), lambda i: (i, 0)),
                  pl.BlockSpec((HALO, PW), lambda i: (jnp.minimum((i + 1) * per, t // HALO - 1), 0))],
        out_specs=[pl.BlockSpec((tm, PW), lambda i: (i, 0)), pl.BlockSpec((1, PW), lambda i: (0, 0))],
        out_shape=[sds((t, PW), BF16), sds((1, PW), F32)],
        scratch_shapes=[pltpu.VMEM((tm + HALO, PW), F32)], compiler_params=_params(("arbitrary",)),
    )(dpooled, dpooled)


def _scores(qh, kp, kc, mask_p, mask_c, sink):
    sp = jnp.where(mask_p, lax.dot_general(qh, kp, NT, preferred_element_type=F32), -1e30)
    sc = jnp.where(mask_c, lax.dot_general(qh, kc, NT, preferred_element_type=F32), -1e30)
    m = jnp.maximum(jnp.maximum(jnp.max(sp, axis=-1, keepdims=True), jnp.max(sc, axis=-1, keepdims=True)), sink)
    pp, pc = jnp.exp(sp - m), jnp.exp(sc - m)
    es = jnp.exp(sink - m)
    inv = 1.0 / (jnp.sum(pp, axis=-1, keepdims=True) + jnp.sum(pc, axis=-1, keepdims=True) + es)
    return pp * inv, pc * inv, es * inv


GRP = N_Q // N_KV


def _masks(n):
    qi = lax.broadcasted_iota(jnp.int32, (GRP * BLK, BLK), 0) % BLK
    kj = lax.broadcasted_iota(jnp.int32, (GRP * BLK, BLK), 1)
    return (kj > qi) & (n > 0), kj <= qi


def _head(hk, g):
    return pl.ds(HD * (GRP * hk + g), HD)


def _stack_heads(ref, hk):
    return jnp.concatenate([ref[:, _head(hk, g)] for g in range(GRP)], axis=0)


def _stack_sinks(s_ref, hk):
    return jnp.concatenate([jnp.full((BLK, 1), s_ref[0, GRP * hk + g], F32) for g in range(GRP)], axis=0)


def attn_fwd(q, k, v, sinks, t, job=None):
    def body(s_ref, q_ref, kp_ref, kc_ref, vp_ref, vc_ref, o_ref):
        n = pl.program_id(0)
        mask_p, mask_c = _masks(n)
        for hk in range(N_KV):
            kv = pl.ds(HD * hk, HD)
            pp, pc, _ = _scores(_stack_heads(q_ref, hk), kp_ref[:, kv], kc_ref[:, kv], mask_p, mask_c,
                                _stack_sinks(s_ref, hk))
            o = (lax.dot_general(pp.astype(BF16), vp_ref[:, kv], NN, preferred_element_type=F32)
                 + lax.dot_general(pc.astype(BF16), vc_ref[:, kv], NN, preferred_element_type=F32))
            for g in range(GRP):
                o_ref[:, _head(hk, g)] = o[g * BLK:(g + 1) * BLK].astype(o_ref.dtype)

    prev = lambda n: (jnp.maximum(n - 1, 0), 0)
    cur = lambda n: (n, 0)
    res, moved = carried_call(
        body, "attn_fwd", (t // BLK,),
        [pl.BlockSpec(memory_space=pltpu.SMEM), pl.BlockSpec((BLK, QW), cur),
         pl.BlockSpec((BLK, KVW), prev), pl.BlockSpec((BLK, KVW), cur),
         pl.BlockSpec((BLK, KVW), prev), pl.BlockSpec((BLK, KVW), cur)],
        [pl.BlockSpec((BLK, QW), cur)], [sds((t, QW), BF16)], [], [sinks, q, k, k, v, v], {}, job)
    return res[0], moved


def attn_bwd(q, k, v, do, sinks, t):
    nb = t // BLK

    def body(s_ref, q_ref, do_ref, kp_ref, kc_ref, vp_ref, vc_ref, dq_ref, dk_ref, dv_ref, ds_ref, dkc, dvc):
        n = pl.program_id(0)

        @pl.when(n == 0)
        def _():
            dkc[...] = jnp.zeros_like(dkc)
            dvc[...] = jnp.zeros_like(dvc)
            ds_ref[...] = jnp.zeros_like(ds_ref)

        @pl.when(n < nb)
        def _():
            mask_p, mask_c = _masks(n)
            lane = lax.broadcasted_iota(jnp.int32, (1, 128), 1)
            dsink = jnp.zeros((1, 128), F32)
            for hk in range(N_KV):
                kv = pl.ds(HD * hk, HD)
                kp, kc, vp, vc = kp_ref[:, kv], kc_ref[:, kv], vp_ref[:, kv], vc_ref[:, kv]
                qs, dos = _stack_heads(q_ref, hk), _stack_heads(do_ref, hk)
                pp, pc, ps = _scores(qs, kp, kc, mask_p, mask_c, _stack_sinks(s_ref, hk))
                dpp = lax.dot_general(dos, vp, NT, preferred_element_type=F32)
                dpc = lax.dot_general(dos, vc, NT, preferred_element_type=F32)
                delta = jnp.sum(pp * dpp, axis=-1, keepdims=True) + jnp.sum(pc * dpc, axis=-1, keepdims=True)
                dsp = (pp * (dpp - delta)).astype(BF16)
                dsc = (pc * (dpc - delta)).astype(BF16)
                sd = ps * delta
                dq = (lax.dot_general(dsp, kp, NN, preferred_element_type=F32)
                      + lax.dot_general(dsc, kc, NN, preferred_element_type=F32))
                for g in range(GRP):
                    rows = slice(g * BLK, (g + 1) * BLK)
                    dsink = dsink + jnp.where(lane == GRP * hk + g, -jnp.sum(sd[rows]), 0.0)
                    dq_ref[:, _head(hk, g)] = dq[rows]
                dk_ref[:, kv] = dkc[:, kv] + lax.dot_general(dsp, qs, TN, preferred_element_type=F32)
                dv_ref[:, kv] = dvc[:, kv] + lax.dot_general(pp.astype(BF16), dos, TN, preferred_element_type=F32)
                dkc[:, kv] = lax.dot_general(dsc, qs, TN, preferred_element_type=F32)
                dvc[:, kv] = lax.dot_general(pc.astype(BF16), dos, TN, preferred_element_type=F32)
            ds_ref[...] += dsink

        @pl.when(n == nb)
        def _():
            dk_ref[...] = dkc[...]
            dv_ref[...] = dvc[...]

    cur = lambda n: (jnp.minimum(n, nb - 1), 0)
    prev = lambda n: (jnp.clip(n - 1, 0, nb - 1), 0)
    return pl.pallas_call(
        body, name="attn_bwd", grid=(nb + 1,),
        in_specs=[pl.BlockSpec(memory_space=pltpu.SMEM), pl.BlockSpec((BLK, QW), cur), pl.BlockSpec((BLK, QW), cur),
                  pl.BlockSpec((BLK, KVW), prev), pl.BlockSpec((BLK, KVW), cur),
                  pl.BlockSpec((BLK, KVW), prev), pl.BlockSpec((BLK, KVW), cur)],
        out_specs=[pl.BlockSpec((BLK, QW), cur), pl.BlockSpec((BLK, KVW), prev), pl.BlockSpec((BLK, KVW), prev),
                   pl.BlockSpec((1, 128), lambda n: (0, 0))],
        out_shape=[sds((t, QW), F32), sds((t, KVW), F32), sds((t, KVW), F32), sds((1, 128), F32)],
        scratch_shapes=[pltpu.VMEM((BLK, KVW), F32), pltpu.VMEM((BLK, KVW), F32)],
        compiler_params=_params(("arbitrary",)),
    )(sinks, q, do, k, k, v, v)


def _adamw(w, g, m, v):
    m2 = B1 * m + (1.0 - B1) * g
    v2 = B2 * v + (1.0 - B2) * jnp.square(g)
    m_hat = m2 / (1.0 - B1 ** STEP)
    v_hat = v2 / (1.0 - B2 ** STEP)
    return -LR * (m_hat / (jnp.sqrt(v_hat) + EPS) + WD * w), m2, v2


def ada_fwd(c16, w_ada, b_sh):
    tn = 512

    def body(c_ref, w_ref, b_ref, o_ref):
        cv = c_ref[...]
        sc = (cv * _sigmoid(cv)).astype(BF16)
        o_ref[...] = lax.dot_general(sc, w_ref[...].astype(BF16), NN, preferred_element_type=F32) + b_ref[...]

    return pl.pallas_call(
        body, name="ada_fwd", grid=(ADA_SH // tn,),
        in_specs=[pl.BlockSpec((16, D), lambda j: (0, 0)), pl.BlockSpec((D, tn), lambda j: (0, j)),
                  pl.BlockSpec((1, tn), lambda j: (0, j))],
        out_specs=pl.BlockSpec((16, tn), lambda j: (0, j)), out_shape=sds((16, ADA_SH), F32),
        compiler_params=_params(("arbitrary",)),
    )(c16, w_ada, b_sh)


def ada_bwd_adam(c16, gm16, w, m, v, job):
    tm, tn = 256, 512

    def body(c_ref, g_ref, w_ref, m_ref, v_ref, go_ref, d_ref, mo_ref, vo_ref):
        cv = c_ref[...]
        sc = (cv * _sigmoid(cv)).astype(BF16)
        g = lax.dot_general(sc, g_ref[...].astype(BF16), TN, preferred_element_type=F32)
        dl, m2, v2 = _adamw(w_ref[...], g, m_ref[...], v_ref[...])
        go_ref[...] = g
        d_ref[...] = dl
        mo_ref[...] = m2
        vo_ref[...] = v2

    blk = pl.BlockSpec((tm, tn), lambda i, j: (i, j))
    return carried_call(
        body, "ada_bwd_adam", (D // tm, ADA_SH // tn),
        [pl.BlockSpec((16, tm), lambda i, j: (0, i)), pl.BlockSpec((16, tn), lambda i, j: (0, j)), blk, blk, blk],
        [blk] * 4, [sds((D, ADA_SH), F32)] * 4, [], [c16, gm16, w, m, v], {}, job)


def adam_rows(name, w, g, m, v, tm):
    rows, cols = w.shape

    def fn(wv, gv, mv, vv):
        gv = gv[:, :cols]
        dl, m2, v2 = _adamw(wv, gv, mv, vv)
        return gv, dl, m2, v2

    return rowmap(name, fn, [T_(w), T_(g), T_(m), T_(v)], [(cols, F32)] * 4, rows=rows, tm=tm)


def adam_small(name, w, g, m, v):
    def body(w_ref, g_ref, m_ref, v_ref, d_ref, mo_ref, vo_ref):
        dl, m2, v2 = _adamw(w_ref[...], g_ref[...], m_ref[...], v_ref[...])
        d_ref[...] = dl
        mo_ref[...] = m2
        vo_ref[...] = v2

    return pl.pallas_call(body, name=name, out_shape=[sds(w.shape, F32)] * 3)(w, g, m, v)


def sum_devices(allv):
    def body(a_ref, o_ref):
        acc = a_ref[0]
        for d in range(1, 8):
            acc = acc + a_ref[d]
        o_ref[...] = acc

    return pl.pallas_call(body, name="sum_devices", out_shape=sds(allv.shape[1:], F32))(allv)


def _ln_fwd(z, g, b):
    mu = jnp.mean(z, axis=-1, keepdims=True)
    zc = z - mu
    var = jnp.mean(jnp.square(zc), axis=-1, keepdims=True)
    return zc * lax.rsqrt(var + LN_EPS) * g + b


def _ln_bwd(z, g, dout):
    mu = jnp.mean(z, axis=-1, keepdims=True)
    zc = z - mu
    var = jnp.mean(jnp.square(zc), axis=-1, keepdims=True)
    rstd = lax.rsqrt(var + LN_EPS)
    xh = zc * rstd
    dxh = dout * g
    dz = rstd * (dxh - jnp.mean(dxh, axis=-1, keepdims=True) - xh * jnp.mean(dxh * xh, axis=-1, keepdims=True))
    return dz, colsum(dout * xh), colsum(dout)


def modulate(name, xin, shift, scale, t):
    return rowmap(name, lambda xv, sh, sc: xv * (1.0 + sc) + sh, [T_(xin), B_(shift), B_(scale)], [(D, BF16)],
                  rows=t, tm=512)


def residual_ln_mod(name, xin, y, gate, lg, lb, wgt, shift_n, scale_n, t):
    def fn(xv, yv, gt, g, b, sh, sc):
        z = ALPHA * xv + (wgt * (1.0 + gt)) * yv
        xo = _ln_fwd(z, g, b)
        return xo, z, xo * (1.0 + sc) + sh

    return rowmap(name, fn, [T_(xin), T_(y), B_(gate), B_(lg), B_(lb), B_(shift_n), B_(scale_n)],
                  [(D, F32), (D, F32), (D, BF16)], rows=t, tm=256)


def residual_ln_bwd(name, z, dnext, y, gate, lg, wgt, t):
    dzn, dun, xn, scn = dnext

    def fn(zv, yv, gt, g, dzv, duv, xv, sc):
        dv = ALPHA * dzv + duv * (1.0 + sc)
        dz, dg, db = _ln_bwd(zv, g, dv)
        return dz, (wgt * (1.0 + gt)) * dz, dg, db, colsum(wgt * dz * yv), colsum(duv), colsum(duv * xv)

    return rowmap(name, fn, [T_(z), T_(y), B_(gate), B_(lg), T_(dzn), T_(dun), T_(xn), B_(scn)],
                  [(D, F32), (D, BF16)], [(1, D)] * 5, rows=t, tm=256)


def residual_ln_loss_bwd(name, xin, y, tgt, gate, lg, lb, wgt, t):
    def fn(xv, yv, tv, gt, g, b):
        z = ALPHA * xv + (wgt * (1.0 + gt)) * yv
        d = _ln_fwd(z, g, b) - tv
        dz, dg, db = _ln_bwd(z, g, d * (1.0 / D))
        return dz, (wgt * (1.0 + gt)) * dz, dg, db, colsum(wgt * dz * yv), jnp.sum(d * d).reshape(1, 1)

    dz, dy, dlg, dlb, dgate, sq = rowmap(
        name, fn, [T_(xin), T_(y), T_(tgt), B_(gate), B_(lg), B_(lb)], [(D, F32), (D, BF16)],
        [(1, D), (1, D), (1, D), (1, 1)], rows=t, tm=256)
    return dz, dy, dlg, dlb, dgate, sq


def modulate_bwd(name, dz, du, xin, scale, t):
    def fn(dzv, duv, xv, sc):
        return ALPHA * dzv + duv * (1.0 + sc), colsum(duv), colsum(duv * xv)

    return rowmap(name, fn, [T_(dz), T_(du), T_(xin), B_(scale)], [(D, F32)], [(1, D), (1, D)], rows=t, tm=256)


def ffn_fwd(tag, u, wi, t, up_job, down_job=None):
    tm = min(1024, t)
    tn = 256
    per = FHP // tn

    def act(accs, _):
        a, b = accs
        s = _sigmoid(a)
        sl = a * s
        return b * (s * (1.0 + a * (1.0 - s))), sl, sl * b

    tmu = min(2048, t)
    hblk = pl.BlockSpec((tmu, tn), lambda i, j, k: (i, j))
    (ha, hb, g), up_moved = mm(
        tag + "_up", [u], [wi, wi], [(0, 0, 0), (0, 1, 1)], dims=NT, grid=(t // tmu, 2 * per, 1),
        a_specs=[pl.BlockSpec((tmu, D), lambda i, j, k: (i, 0))],
        b_specs=[pl.BlockSpec((None, tn, D), lambda i, j, k: (j // per, j % per, 0)),
                 pl.BlockSpec((None, tn, D), lambda i, j, k: (2 + j // per, j % per, 0))],
        outs=[sds((t, 2 * FHP), BF16)] * 3, out_specs=[hblk] * 3, acc_shapes=[(tmu, tn)] * 2, epilogue=act, job=up_job,
        sub_rows=tmu // 2)
    wo = up_moved[0].reshape(2 * FHP, D)
    tk = FHP
    y, down_moved = _with_moved(mm(
        tag + "_down", [g], [wo], [(0, 0, 0)], dims=NN, grid=(t // tm, 2, 2),
        a_specs=[pl.BlockSpec((tm, tk), lambda i, j, k: (i, k))],
        b_specs=[pl.BlockSpec((tk, D // 2), lambda i, j, k: (k, j))],
        outs=[sds((t, D), F32)], out_specs=[pl.BlockSpec((tm, D // 2), lambda i, j, k: (i, j))],
        acc_shapes=[(tm, D // 2)], job=down_job), down_job)
    return ha, hb, g, y, wo, up_moved, down_moved


def ffn_bwd(tag, u, ha, hb, g, dy, wi, wo, t, sp, dact_job=None, dwo_job=None):
    tm = min(1024, t)

    def dact(accs, ex):
        dg = accs[0]
        return dg * ex[0].astype(F32), dg * ex[1].astype(F32)

    tn = 256
    tmu = min(2048, t)
    hblk = pl.BlockSpec((tmu, tn), lambda i, j, k: (i, j))
    dh, dact_moved = _with_moved(mm(
        tag + "_dact", [dy], [wo], [(0, 0, 0)], dims=NT, grid=(t // tmu, 2 * FHP // tn, 1),
        a_specs=[pl.BlockSpec((tmu, D), lambda i, j, k: (i, 0))],
        b_specs=[pl.BlockSpec((tn, D), lambda i, j, k: (j, 0))],
        outs=[sds((2, t, 2 * FHP), BF16)], out_specs=[pl.BlockSpec((2, tmu, tn), lambda i, j, k: (0, i, j))],
        acc_shapes=[(tmu, tn)], epilogue=dact, extras=[ha, hb], extra_specs=[hblk] * 2, job=dact_job,
        sub_rows=tmu // 2, stacked=True), dact_job)
    tk = min(2048, t)
    th = FHP // 2
    dwo, dwo_moved = _with_moved(mm(
        tag + "_dwo", [g], [dy], [(0, 0, 0)], dims=TN, grid=(4, 2, t // tk),
        a_specs=[pl.BlockSpec((tk, th), lambda i, j, k: (k, i))],
        b_specs=[pl.BlockSpec((tk, D // 2), lambda i, j, k: (k, j))],
        outs=[sds((2 * FHP, D), BF16)], out_specs=[pl.BlockSpec((th, D // 2), lambda i, j, k: (i, j))],
        acc_shapes=[(th, D // 2)], job=dwo_job), dwo_job)
    dwo = dwo.reshape(2, FHP, D)

    def dwi_part(part, carry, job):
        return mm(
            f"{tag}_dwi{part}", [dh], [u], [(0, 0, 0)], dims=TN, grid=(4, 2, t // tk),
            a_specs=[pl.BlockSpec((None, tk, th), lambda i, j, k: (part, k, i))],
            b_specs=[pl.BlockSpec((tk, D // 2), lambda i, j, k: (k, j))],
            outs=[sds((4, FHP, D), BF16)],
            out_specs=[pl.BlockSpec((None, th, D // 2), lambda i, j, k: (2 * part + i // 2, i % 2, j))],
            acc_shapes=[(th, D // 2)], carry=carry, job=job)

    dwi, (sib_fo,) = dwi_part(0, None, reduce_sibling_job([(dwo, view_ffn_out, FO, D)]))
    q_fo = chip_sum(tag + "_chipsum_fo", dwo, sib_fo, sp, FO, FO // 2, ffn_out=True)
    dwi, (far_fo,) = dwi_part(1, dwi, reduce_chips_job([q_fo]))
    (sib_fi,) = run_job(tag + "_sibling_fi", reduce_sibling_job([(dwi, view_lead, FHP, D)]))
    q_fi = chip_sum(tag + "_chipsum_fi", dwi, sib_fi, sp, FHP, FHP // 8)
    du, (far_fi,) = mm(
        tag + "_du", [dh], [wi], [(0, 0, 0)], dims=NN, grid=(t // tm, 2, 4),
        a_specs=[pl.BlockSpec((None, tm, FHP), lambda i, j, k: (k // 2, i, k % 2))],
        b_specs=[pl.BlockSpec((None, FHP, D // 2), lambda i, j, k: (k, 0, j))],
        outs=[sds((t, D), F32)], out_specs=[pl.BlockSpec((tm, D // 2), lambda i, j, k: (i, j))],
        acc_shapes=[(tm, D // 2)], job=reduce_chips_job([q_fi]))
    return du, (q_fi, far_fi), (q_fo, far_fo), dact_moved, dwo_moved


def mix_fwd(u, wts, b_in, pool_scale, sinks, tabs, t, in_job, attn_job):
    w_in, wp, wba, wbb, wo = wts
    tm = min(1024, t)
    tmh = min(512, t)
    h, in_moved = mm("mix_in", [u], [w_in], [(0, 0, 0)], dims=NN, grid=(t // tmh, 4, 1),
                     a_specs=[pl.BlockSpec((tmh, D), lambda i, j, k: (i, 0))],
                     b_specs=[pl.BlockSpec((None, D, IN_SH), lambda i, j, k: (j, 0, 0))],
                     outs=[sds((t, IN_W), F32)], out_specs=[pl.BlockSpec((tmh, IN_SH), lambda i, j, k: (i, j))],
                     acc_shapes=[(tmh, IN_SH)], job=in_job)
    attn_job = attn_job(in_moved)
    pooled = pool_fwd(h, b_in, t, 512)
    gblk = pl.BlockSpec((tm, PG), lambda i, j, k: (i, j))
    mixed = mm("mix_pool", [pooled], [wp], [(0, 0, 0)], dims=NN, grid=(t // tm, 4, 1), a_specs=[gblk],
               b_specs=[pl.BlockSpec((None, PG, PG), lambda i, j, k: (j, 0, 0))],
               outs=[sds((t, PW), F32)], out_specs=[gblk], acc_shapes=[(tm, PG)])
    pm = rowmap("mix_pscale", lambda mv, ps: mv * ps, [T_(mixed), B_(pool_scale)], [(PW, BF16)], rows=t, tm=512)

    def branch(name, a, w):
        return mm(name, [a], [w], [(0, 0, 0)], dims=NN, grid=(t // tm, 4, 1),
                  a_specs=[pl.BlockSpec((tm, PW), lambda i, j, k: (i, 0))],
                  b_specs=[pl.BlockSpec((None, PW, D // 4), lambda i, j, k: (j, 0, 0))],
                  outs=[sds((t, D), F32)], out_specs=[pl.BlockSpec((tm, D // 4), lambda i, j, k: (i, j))],
                  acc_shapes=[(tm, D // 4)])

    ya = branch("mix_branch_a", pm, wba)

    def qkv(hq, hk, hv, bq, bk, bv, cc, sa, sb):
        return (_rope(hq + bq, cc, sa, sb) * (HD ** -0.5), _rope(hk + bk, cc, sa, sb), hv + bv)

    qr, kr, vv = rowmap(
        "mix_rope", qkv,
        [T_(h, QW, 1), T_(h, KVW, 8), T_(h, KVW, 9), B_(b_in, QW, 1), B_(b_in, KVW, 8), B_(b_in, KVW, 9),
         T_(tabs[0]), T_(tabs[1]), T_(tabs[2])],
        [(QW, BF16), (KVW, BF16), (KVW, BF16)], rows=t, tm=512)
    attn, attn_moved = attn_fwd(qr, kr, vv, sinks, t, attn_job)
    yb = branch("mix_branch_b", attn, wbb)
    cw = 512

    def merge(ga, gb, ba, bb, yav, ybv):
        return _sigmoid(ga + ba) * yav + _sigmoid(gb + bb) * ybv

    merged = rowmap(
        "mix_merge", merge,
        [T_(h, cw, 5), T_(h, cw, 9), B_(b_in, cw, 5), B_(b_in, cw, 9), T_(ya, cw), T_(yb, cw)],
        [(D, BF16)], rows=t, tm=512, ncol=D // cw)
    y = mm("mix_out", [merged], [wo], [(0, 0, 0)], dims=NN, grid=(t // tm, 2, 1),
           a_specs=[pl.BlockSpec((tm, D), lambda i, j, k: (i, 0))],
           b_specs=[pl.BlockSpec((D, D // 2), lambda i, j, k: (0, j))],
           outs=[sds((t, D), F32)], out_specs=[pl.BlockSpec((tm, D // 2), lambda i, j, k: (i, j))],
           acc_shapes=[(tm, D // 2)])
    return y, (h, pooled, mixed, pm, ya, qr, kr, vv, attn, yb, merged), attn_moved


def mix_bwd(u, saved, dy, wts, b_in, pool_scale, sinks, tabs, t):
    h, pooled, mixed, pm, ya, qr, kr, vv, attn, yb, merged = saved
    w_in, wp, wba, wbb, wo = wts
    tm = min(1024, t)
    tk = min(2048, t)
    dmerged = mm("mix_dmerged", [dy], [wo], [(0, 0, 0)], dims=NT, grid=(t // tm, 2, 1),
                 a_specs=[pl.BlockSpec((tm, D), lambda i, j, k: (i, 0))],
                 b_specs=[pl.BlockSpec((D // 2, D), lambda i, j, k: (j, 0))],
                 outs=[sds((t, D), F32)], out_specs=[pl.BlockSpec((tm, D // 2), lambda i, j, k: (i, j))],
                 acc_shapes=[(tm, D // 2)])
    half = pl.BlockSpec((tk, D // 2), lambda i, j, k: (k, i))
    dwo = mm("mix_dwo", [merged], [dy], [(0, 0, 0)], dims=TN, grid=(2, 2, t // tk), a_specs=[half],
             b_specs=[pl.BlockSpec((tk, D // 2), lambda i, j, k: (k, j))],
             outs=[sds((D, D), BF16)], out_specs=[pl.BlockSpec((D // 2, D // 2), lambda i, j, k: (i, j))],
             acc_shapes=[(D // 2, D // 2)])
    cw = 512

    def dmerge(dm, ga, gb, ba, bb, yav, ybv):
        sa_, sb_ = _sigmoid(ga + ba), _sigmoid(gb + bb)
        dga = dm * yav * sa_ * (1.0 - sa_)
        dgb = dm * ybv * sb_ * (1.0 - sb_)
        return dm * sa_, dm * sb_, dga, dgb, colsum(dga), colsum(dgb)

    dya, dyb, dgla, dglb, dbga, dbgb = rowmap(
        "mix_dmerge", dmerge,
        [T_(dmerged, cw), T_(h, cw, 5), T_(h, cw, 9), B_(b_in, cw, 5), B_(b_in, cw, 9), T_(ya, cw), T_(yb, cw)],
        [(D, BF16)] * 4, [(1, D), (1, D)], rows=t, tm=512, ncol=D // cw)

    def dbranch(name, dyv, act, w):
        dwb = mm(name + "_dw", [act], [dyv], [(0, 0, 0)], dims=TN, grid=(1, 4, t // tk),
                 a_specs=[pl.BlockSpec((tk, PW), lambda i, j, k: (k, 0))],
                 b_specs=[pl.BlockSpec((tk, D // 4), lambda i, j, k: (k, j))],
                 outs=[sds((4, PW, D // 4), BF16)], out_specs=[pl.BlockSpec((None, PW, D // 4), lambda i, j, k: (j, 0, 0))],
                 acc_shapes=[(PW, D // 4)])
        return dwb, lambda dt: mm(
            name + "_dx", [dyv], [w], [(0, 0, 0)], dims=NT, grid=(t // tm, 1, 4),
            a_specs=[pl.BlockSpec((tm, D // 4), lambda i, j, k: (i, k))],
            b_specs=[pl.BlockSpec((None, PW, D // 4), lambda i, j, k: (k, 0, 0))],
            outs=[sds((t, PW), dt)], out_specs=[pl.BlockSpec((tm, PW), lambda i, j, k: (i, 0))], acc_shapes=[(tm, PW)])

    dwba, dpm_fn = dbranch("mix_dbranch_a", dya, pm, wba)
    dwbb, dattn_fn = dbranch("mix_dbranch_b", dyb, attn, wbb)
    dpm, dattn = dpm_fn(F32), dattn_fn(BF16)
    dmixed, dps = rowmap("mix_dpscale", lambda dp, mv, ps: (dp * ps, colsum(dp * mv)),
                         [T_(dpm), T_(mixed), B_(pool_scale)], [(PW, BF16)], [(1, PW)], rows=t, tm=512)
    gblk = pl.BlockSpec((tm, PG), lambda i, j, k: (i, j))
    dpooled = mm("mix_dpool", [dmixed], [wp], [(0, 0, 0)], dims=NT, grid=(t // tm, 4, 1), a_specs=[gblk],
                 b_specs=[pl.BlockSpec((None, PG, PG), lambda i, j, k: (j, 0, 0))],
                 outs=[sds((t, PW), F32)], out_specs=[gblk], acc_shapes=[(tm, PG)])
    kblk = pl.BlockSpec((tk, PG), lambda i, j, k: (k, i))
    dwp = mm("mix_dwpool", [pooled], [dmixed], [(0, 0, 0)], dims=TN, grid=(4, 1, t // tk), a_specs=[kblk], b_specs=[kblk],
             outs=[sds((4, PG, PG), BF16)], out_specs=[pl.BlockSpec((None, PG, PG), lambda i, j, k: (i, 0, 0))],
             acc_shapes=[(PG, PG)])
    dxp, dbxp = pool_bwd(dpooled, t, 512)
    dqr, dkr, dvv, dsinks = attn_bwd(qr, kr, vv, dattn, sinks, t)

    def dqkv(dq, dk, dv, cc, sa, sb):
        dq = _rope_t(dq, cc, sa, sb) * (HD ** -0.5)
        dk = _rope_t(dk, cc, sa, sb)
        return dq, dk, dv, colsum(dq), colsum(dk), colsum(dv)

    dq, dk, dvb, dbq, dbk, dbv = rowmap(
        "mix_rope_bwd", dqkv, [T_(dqr), T_(dkr), T_(dvv), T_(tabs[0]), T_(tabs[1]), T_(tabs[2])],
        [(QW, BF16), (KVW, BF16), (KVW, BF16)], [(1, QW), (1, KVW), (1, KVW)], rows=t, tm=512)
    dh = jnp.concatenate([dxp, dq, dk, dvb, dgla, dglb], axis=1)
    db_in = jnp.concatenate([dbxp, dbq, dbk, dbv, dbga, dbgb], axis=1)
    dwin = mm("mix_dwin", [u], [dh], [(0, 0, 0)], dims=TN, grid=(2, 4, t // tk), a_specs=[half],
              b_specs=[pl.BlockSpec((tk, IN_SH), lambda i, j, k: (k, j))],
              outs=[sds((4, D, IN_SH), BF16)], out_specs=[pl.BlockSpec((None, D // 2, IN_SH), lambda i, j, k: (j, i, 0))],
              acc_shapes=[(D // 2, IN_SH)])
    dwp_sh = jnp.transpose(dwp.reshape(4, 4, 64, PG), (1, 0, 2, 3)).reshape(4, 4 * 64, PG)
    parts = {"win": dwin, "wp": dwp_sh, "wba": dwba, "wbb": dwbb, "wo": dwo.reshape(4, D // 4, D)}
    du, sib = mm("mix_du", [dh], [w_in], [(0, 0, 0)], dims=NT, grid=(t // tm, 2, 4),
                 a_specs=[pl.BlockSpec((tm, IN_SH), lambda i, j, k: (i, k))],
                 b_specs=[pl.BlockSpec((None, D // 2, IN_SH), lambda i, j, k: (k, j, 0))],
                 outs=[sds((t, D), F32)], out_specs=[pl.BlockSpec((tm, D // 2), lambda i, j, k: (i, j))],
                 acc_shapes=[(tm, D // 2)],
                 job=reduce_sibling_job([(p, view_lead, p.shape[1], p.shape[2]) for p in parts.values()]))
    return du, parts, dict(zip(parts, sib)), db_in, dps, dsinks


def cast_shard(name, w, sp, ffn_out=False):
    rows, cols = w.shape
    if ffn_out:
        tm = rows // 2
        shape = (2, FHP, D)
        spec = pl.BlockSpec((None, tm, cols), lambda j, i, s: (s[0] // 2, (s[0] % 2) * 2 + i, 0))
    else:
        tm = rows // 4
        shape = (4, rows, cols)
        spec = pl.BlockSpec((None, tm, cols), lambda j, i, s: (s[0], i, 0))
    return rowmap(name, lambda wv: wv, [T_(w)], [(shape, BF16, spec)], rows=rows, tm=tm, sp=sp)


def cast_ffn_in(name, wt, sp):
    tm = 64
    full = FH // tm

    def fn(_, i, wv):
        return jnp.where(i < full, wv, 0.0)

    return rowmap(name, fn, [X_(wt, pl.BlockSpec((tm, D), lambda j, i, s: (jnp.minimum(i, full - 1), 0)))],
                  [((4, FHP, D), BF16, pl.BlockSpec((None, tm, D), lambda j, i, s: (s[0], i, 0)))],
                  rows=FHP, tm=tm, sp=sp, with_ids=True)


def chip_sum(name, dw, got, sp, rows, tm, ffn_out=False):
    hr, cols = rows // 2, got.shape[2]
    per = hr // tm
    pos = pl.BlockSpec((None, tm, cols), lambda j, i, s: (i // per, i % per, 0))
    if ffn_out:
        mine = pl.BlockSpec((None, tm, cols), lambda j, i, s: (i // 2, (i % 2) * 2 + s[1], 0))
    else:
        mine = pl.BlockSpec((None, tm, cols), lambda j, i, s: (i // per, s[1] * per + i % per, 0))
    return rowmap(name, lambda av, bv: av.astype(F32) + bv.astype(F32), [X_(dw, mine), X_(got, pos)],
                  [(got.shape, BF16, pos)], rows=4 * hr, tm=tm, sp=sp)


def chip_total(name, q, got, sp, rows, tm):
    hr, cols = rows // 2, q.shape[2]
    per = hr // tm

    def part(f):
        return X_(got, pl.BlockSpec((None, tm, cols), lambda j, i, s, f=f: (f, i, 0)))

    return rowmap(
        name, lambda av, b0, b1, b2: ((av.astype(F32) + b0.astype(F32)) + b1.astype(F32)) + b2.astype(F32),
        [X_(q, pl.BlockSpec((None, tm, cols), lambda j, i, s: (s[0], i, 0))), part(0), part(1), part(2)],
        [((rows, cols), F32, pl.BlockSpec((tm, cols), lambda j, i, s: (s[1] * per + i, 0)))], rows=hr, tm=tm, sp=sp)


def kernel(x, c, w_ada, b_ada, ln_g, ln_b, w_ffn1_in, w_ffn1_out, w_in, b_in, w_pool, pool_scale, sinks, w_branch_a, w_branch_b, w_out, w_ffn2_in, w_ffn2_out, loss_target, m_w_ada, m_b_ada, m_ln_g, m_ln_b, m_w_ffn1_in, m_w_ffn1_out, m_w_in, m_b_in, m_w_pool, m_pool_scale, m_sinks, m_w_branch_a, m_w_branch_b, m_w_out, m_w_ffn2_in, m_w_ffn2_out, v_w_ada, v_b_ada, v_ln_g, v_ln_b, v_w_ffn1_in, v_w_ffn1_out, v_w_in, v_b_in, v_w_pool, v_pool_scale, v_sinks, v_w_branch_a, v_w_branch_b, v_w_out, v_w_ffn2_in, v_w_ffn2_out):
    t = x.shape[1]
    xs, tgt = x[0], loss_target[0]
    xi, yi, ci = lax.axis_index("x"), lax.axis_index("y"), lax.axis_index("c")
    chip = 2 * xi + yi
    dev = 2 * chip + ci
    b_in2, ps2, sinks2 = b_in, pool_scale, sinks

    sp = jnp.stack([chip, ci]).astype(jnp.int32)
    tr = lambda a: jnp.swapaxes(a[0], 0, 1)

    first = jnp.concatenate([c.reshape(-1), ln_g.reshape(-1), ln_b.reshape(-1)]).reshape(-1, 128)
    first_all = allgather_small("gather_cond", first).reshape(8, -1)
    c_all = first_all[:, :D]
    ln_parts = first_all[0::2, D:].reshape(4, 2, 3, D // 4)
    ln_full = jnp.transpose(ln_parts, (1, 2, 0, 3)).reshape(2, 3, D)
    lgs = [ln_full[0, s:s + 1] for s in range(3)]
    lbs = [ln_full[1, s:s + 1] for s in range(3)]
    c16 = jnp.pad(c_all, ((0, 8), (0, 0)))
    b_ada_sh = lax.dynamic_slice(b_ada, (0, chip * ADA_SH), (1, ADA_SH))
    mod_part = ada_fwd(c16, w_ada[0], b_ada_sh)[:8]
    mod_all = allgather_small("gather_mod", mod_part.reshape(-1, 128)).reshape(8, 8, ADA_SH)
    mod_mine = lax.dynamic_index_in_dim(mod_all[0::2], dev, axis=1, keepdims=False).reshape(9, D)
    mods = [[mod_mine[3 * s + k:3 * s + k + 1] for k in range(3)] for s in range(3)]

    f1i_buf, f1i_send, f1i_recv, _ = gather_start(
        "gather_f1i_start", cast_ffn_in("cast_f1i", tr(w_ffn1_in), sp), FHP, [mod_mine])
    plain = [("f1o", w_ffn1_out[0]), ("win", w_in[0]), ("wp", w_pool[0].reshape(4 * 64, PG)), ("wba", w_branch_a[0]),
             ("wbb", w_branch_b[0]), ("wo", w_out[0]), ("f2o", w_ffn2_out[0])]
    sh = {n: cast_shard("cast_" + n, w, sp, ffn_out=n in ("f1o", "f2o")) for n, w in plain}
    sh["f1i"] = f1i_buf
    sh["f2i"] = cast_ffn_in("cast_f2i", tr(w_ffn2_in), sp)
    order = ["f1i", "f1o", "win", "wp", "wba", "wbb", "wo", "f2i", "f2o"]
    views = {n: (view_ffn_out if n in ("f1o", "f2o") else view_lead) for n in order}
    shard_rows = {n: (FO if n in ("f1o", "f2o") else sh[n].shape[1]) for n in order}
    shard_cols = {n: sh[n].shape[2] for n in order}
    tiles = {"f1i": FHP // 8, "f1o": FO // 2, "win": 512, "wp": 128, "wba": 512, "wbb": 512, "wo": 256,
             "f2i": FHP // 8, "f2o": FO // 2}

    def item(n, part=0, parts=1):
        return (sh[n], views[n], shard_rows[n], part, parts)

    tabs = rope_tables(t)
    (sh0, sc0, gt0), (sh1, sc1, gt1), (sh2, sc2, gt2) = mods

    u0 = modulate("ffn1_mod", xs, sh0, sc0, t)
    landed = gather_wait("gather_f1i_wait", f1i_buf, f1i_send, f1i_recv, FHP,
                         [u0] + [sh[n] for n in order if n != "f1i"])
    (g_f1i,) = run_job("gather_f1i_forward", forward_job(landed, FHP))
    ha1, hb1, g1, y1, f1o, (_, g_win), (g_wp, g_wba, g_wbb, g_wo) = ffn_fwd(
        "ffn1", u0, g_f1i, t, gather_job([item("f1o"), item("win")]),
        gather_job([item(n) for n in ("wp", "wba", "wbb", "wo")]))
    x1, z1, u1 = residual_ln_mod("ffn1_ln", xs, y1, gt0, lgs[0], lbs[0], 0.5, sh1, sc1, t)
    wp_full = jnp.transpose(g_wp.reshape(4, 4, 64, PG), (1, 0, 2, 3)).reshape(4, PG, PG)
    wts = (g_win, wp_full, g_wba, g_wbb, g_wo.reshape(D, D))
    y2, sv2, (g_f2i,) = mix_fwd(
        u1, wts, b_in2, ps2, sinks2, tabs, t, gather_job([item("f2i", 0, 2)]),
        lambda moved: gather_job([(moved[0], view_lead, FHP, 1, 2)]))
    x2, z2, u2 = residual_ln_mod("mix_ln", x1, y2, gt1, lgs[1], lbs[1], 1.0, sh2, sc2, t)
    ha3, hb3, g3, y3, f2o, _, _ = ffn_fwd("ffn2", u2, g_f2i, t, gather_job([item("f2o")]))

    dz3, dy3, dlg2, dlb2, dgt2, sq = residual_ln_loss_bwd("ffn2_ln_loss", x2, y3, tgt, gt2, lgs[2], lbs[2], 0.5, t)
    loss = lax.psum(0.5 * sq[0, 0] / D, ("x", "y", "c"))
    du3, red_f2i, red_f2o, _, _ = ffn_bwd("ffn2", u2, ha3, hb3, g3, dy3, g_f2i, f2o, t, sp)
    dz2, dy2, dlg1, dlb1, dgt1, dsh2, dsc2 = residual_ln_bwd("mix_ln_bwd", z2, (dz3, du3, x2, sc2), y2, gt1, lgs[1], 1.0, t)
    du2, mix_parts, sib, db_in, dps, dsinks = mix_bwd(u1, sv2, dy2, wts, b_in2, ps2, sinks2, tabs, t)
    q = {n: chip_sum("chipsum_" + n, mix_parts[n], sib[n], sp, shard_rows[n], tiles[n]) for n in mix_parts}
    dz1, dy1, dlg0, dlb0, dgt0, dsh1, dsc1 = residual_ln_bwd("ffn1_ln_bwd", z1, (dz2, du2, x1, sc1), y1, gt0, lgs[0], 0.5, t)
    du1, red_f1i, red_f1o, far_a, far_b = ffn_bwd(
        "ffn1", u0, ha1, hb1, g1, dy1, g_f1i, f1o, t, sp,
        reduce_chips_job([q["win"], q["wp"]]), reduce_chips_job([q["wo"], q["wba"], q["wbb"]]))
    dx0, dsh0, dsc0 = modulate_bwd("ffn1_mod_bwd", dz1, du1, xs, sc0, t)
    gm0, gm1, gm2 = (dsh0, dsc0, dgt0), (dsh1, dsc1, dgt1), (dsh2, dsc2, dgt2)
    reduced = {"f1i": red_f1i, "f1o": red_f1o, "f2i": red_f2i, "f2o": red_f2o, "win": (q["win"], far_a[0]),
               "wp": (q["wp"], far_a[1]), "wo": (q["wo"], far_b[0]), "wba": (q["wba"], far_b[1]), "wbb": (q["wbb"], far_b[2])}
    halves = [chip_total("total_" + n, *reduced[n], sp, shard_rows[n], tiles[n]) for n in order]

    small = jnp.concatenate([*gm0, *gm1, *gm2, dlg0, dlg1, dlg2, dlb0, dlb1, dlb2, db_in, dps, dsinks], axis=1)
    n_small = small.shape[1]
    rows_small = -(-n_small // 1024) * 8
    small = jnp.pad(small, ((0, 0), (0, rows_small * 128 - n_small))).reshape(rows_small, 128)
    small_all = allgather_small("gather_small", small)
    tot = sum_devices(small_all).reshape(1, -1)
    gmod_all = small_all.reshape(8, -1)[:, :9 * D]
    o = 9 * D
    g_b_ada = tot[:, :o]
    g_ln_g = lax.dynamic_slice(tot[:, o:o + 3 * D].reshape(3, D), (0, chip * (D // 4)), (3, D // 4))
    g_ln_b = lax.dynamic_slice(tot[:, o + 3 * D:o + 6 * D].reshape(3, D), (0, chip * (D // 4)), (3, D // 4))
    o += 6 * D
    g_b_in, g_ps, g_sinks = tot[:, o:o + IN_W], tot[:, o + IN_W:o + IN_W + PW], tot[:, o + IN_W + PW:o + IN_W + PW + N_Q]

    gm16 = jnp.pad(lax.dynamic_slice(gmod_all, (0, chip * ADA_SH), (8, ADA_SH)), ((0, 8), (0, 0)))
    (g_w_ada, d_w_ada, nm_w_ada, nv_w_ada), _ = ada_bwd_adam(c16, gm16, w_ada[0], m_w_ada[0], v_w_ada[0], None)
    gw = dict(zip(order, run_job("share_halves", share_halves_job(halves))))

    def big(n, w, m, v, tm):
        shape = w.shape
        w2, m2, v2 = (a.reshape(shape[-2] if a.ndim == 3 else -1, shape[-1]) for a in (w, m, v))
        return [r.reshape(shape) for r in adam_rows("adam_" + n, w2, gw[n], m2, v2, tm)]

    def big_t(n, w, m, v):
        return [jnp.swapaxes(r, 0, 1)[None] for r in adam_rows("adam_" + n, tr(w), gw[n], tr(m), tr(v), 64)]

    def tiny(n, w, g, m, v):
        return [g.reshape(w.shape)] + list(adam_small("adam_" + n, w, g.reshape(w.shape), m, v))

    res = {
        "w_ada": [a[None] for a in (g_w_ada, d_w_ada, nm_w_ada, nv_w_ada)],
        "b_ada": tiny("b_ada", b_ada, g_b_ada, m_b_ada, v_b_ada),
        "ln_g": tiny("ln_g", ln_g, g_ln_g, m_ln_g, v_ln_g),
        "ln_b": tiny("ln_b", ln_b, g_ln_b, m_ln_b, v_ln_b),
        "w_ffn1_in": big_t("f1i", w_ffn1_in, m_w_ffn1_in, v_w_ffn1_in),
        "w_ffn1_out": big("f1o", w_ffn1_out, m_w_ffn1_out, v_w_ffn1_out, 32),
        "w_in": big("win", w_in, m_w_in, v_w_in, 256),
        "b_in": tiny("b_in", b_in, g_b_in, m_b_in, v_b_in),
        "w_pool": big("wp", w_pool, m_w_pool, v_w_pool, 256),
        "pool_scale": tiny("pool_scale", pool_scale, g_ps, m_pool_scale, v_pool_scale),
        "sinks": tiny("sinks", sinks, g_sinks, m_sinks, v_sinks),
        "w_branch_a": big("wba", w_branch_a, m_w_branch_a, v_w_branch_a, 512),
        "w_branch_b": big("wbb", w_branch_b, m_w_branch_b, v_w_branch_b, 512),
        "w_out": big("wo", w_out, m_w_out, v_w_out, 128),
        "w_ffn2_in": big_t("f2i", w_ffn2_in, m_w_ffn2_in, v_w_ffn2_in),
        "w_ffn2_out": big("f2o", w_ffn2_out, m_w_ffn2_out, v_w_ffn2_out, 32),
    }
    names = ["w_ada", "b_ada", "ln_g", "ln_b", "w_ffn1_in", "w_ffn1_out", "w_in", "b_in", "w_pool", "pool_scale", "sinks",
             "w_branch_a", "w_branch_b", "w_out", "w_ffn2_in", "w_ffn2_out"]
    return (loss, dx0[None], *[res[n][0] for n in names], *[res[n][1] for n in names],
            *[res[n][2] for n in names], *[res[n][3] for n in names])
```

```python
import jax
import jax.numpy as jnp
from jax import lax
from jax.experimental import pallas as pl
from jax.experimental.pallas import tpu as pltpu

F32 = jnp.float32
BF16 = jnp.bfloat16
MESH = pl.DeviceIdType.MESH
ANY = pl.BlockSpec(memory_space=pl.ANY)

D = 2048
N_Q, N_KV, HD = 16, 4, 64
QW, KVW = N_Q * HD, N_KV * HD
BLK = 128
POOL_WINDOWS = (2, 4, 8, 16)
PW, PG = 1024, 256
HALO = 16
ROPE_THETA = 500000.0
ROT = HD // 4
LN_EPS = 1e-5
ALPHA = 2.0 ** 0.25
FH = 2752
FHP = 2816
FO = 1376
IN_W = 6656
IN_SH = IN_W // 4
ADA_SH = 18432 // 4
B1, B2, LR, EPS, WD, STEP = 0.9, 0.999, 0.001, 1e-08, 0.01, 10
VMEM_LIMIT = 56 * 1024 * 1024
FLIPS = ((1, 0), (0, 1), (1, 1))
NN = (((1,), (0,)), ((), ()))
NT = (((1,), (1,)), ((), ()))
TN = (((0,), (0,)), ((), ()))


def _params(sem):
    return pltpu.CompilerParams(dimension_semantics=sem, vmem_limit_bytes=VMEM_LIMIT)


def _aligned(v, m):
    return v if isinstance(v, int) else pl.multiple_of(v, m)


def _sigmoid(v):
    return 1.0 / (1.0 + jnp.exp(-v))


def T_(arr, width=None, off=0):
    return ("t", arr, width, off)


def B_(arr, width=None, off=0):
    return ("b", arr, width, off)


def X_(arr, spec):
    return ("x", arr, spec, 0)


def rowmap(name, fn, ins, outs, accs=(), *, rows, tm, ncol=1, with_ids=False, sp=None, alias=None):
    tm = min(tm, rows)
    nrow = rows // tm
    in_specs, arrs = [], []
    for kind, arr, width, off in ins:
        if kind == "x":
            in_specs.append(width)
        elif kind == "t":
            w = arr.shape[1] if width is None else width
            in_specs.append(pl.BlockSpec((tm, w), lambda j, i, *_, off=off: (i, off + j)))
        else:
            w = arr.shape[1] if width is None else width
            in_specs.append(pl.BlockSpec((arr.shape[0], w), lambda j, i, *_, off=off: (0, off + j)))
        arrs.append(arr)
    out_shape, out_specs = [], []
    for o in outs:
        if len(o) == 3:
            out_shape.append(jax.ShapeDtypeStruct(o[0], o[1]))
            out_specs.append(o[2])
        else:
            out_shape.append(jax.ShapeDtypeStruct((rows, o[0]), o[1]))
            out_specs.append(pl.BlockSpec((tm, o[0] // ncol), lambda j, i, *_: (i, j)))
    for r, width in accs:
        out_shape.append(jax.ShapeDtypeStruct((r, width), F32))
        out_specs.append(pl.BlockSpec((r, width // ncol), lambda j, i, *_: (0, j)))
    ni, no = len(ins), len(outs)
    nsp = 0 if sp is None else 1

    def body(*refs):
        refs = refs[nsp:]
        i = pl.program_id(1)
        vals = [r[...] for r in refs[:ni]]
        res = fn(pl.program_id(0), i, *vals) if with_ids else fn(*vals)
        if not isinstance(res, (tuple, list)):
            res = (res,)
        for r, v in zip(refs[ni:ni + no], res[:no]):
            r[...] = v.astype(r.dtype)
        for r, v in zip(refs[ni + no:], res[no:]):
            @pl.when(i == 0)
            def _(r=r, v=v):
                r[...] = v

            @pl.when(i > 0)
            def _(r=r, v=v):
                r[...] += v

    grid_spec = pltpu.PrefetchScalarGridSpec(num_scalar_prefetch=nsp, grid=(ncol, nrow), in_specs=in_specs,
                                             out_specs=out_specs)
    res = pl.pallas_call(
        body, name=name, grid_spec=grid_spec, out_shape=out_shape,
        input_output_aliases={nsp + k: v for k, v in (alias or {}).items()},
        compiler_params=_params(("arbitrary", "arbitrary")),
    )(*([sp] if nsp else []), *arrs)
    return res[0] if len(res) == 1 else res


def colsum(v):
    return jnp.sum(v, axis=0, keepdims=True)


def mm(name, a_ops, b_ops, ops, *, dims, grid, a_specs, b_specs, outs, out_specs, acc_shapes,
       epilogue=None, extras=(), extra_specs=(), carry=None, job=None, sub_rows=None):
    gk = grid[2]
    na, nb, ne, nacc = len(a_ops), len(b_ops), len(extras), len(acc_shapes)
    nc = 0 if carry is None else 1
    no = len(outs)

    def body(*refs):
        a_refs = refs[:na]
        b_refs = refs[na:na + nb]
        e_refs = refs[na + nb:na + nb + ne]
        o_refs = refs[na + nb + ne + nc:na + nb + ne + nc + no]
        acc_refs = refs[na + nb + ne + nc + no:]
        k = pl.program_id(2)

        def partials(rows=slice(None)):
            res = [None] * nacc
            for ai, bi, ci in ops:
                p = lax.dot_general(a_refs[ai][rows], b_refs[bi][...], dims, preferred_element_type=F32)
                res[ci] = p if res[ci] is None else res[ci] + p
            return res

        def finish(accs, rows=slice(None)):
            outv = epilogue(accs, [e[rows] for e in e_refs]) if epilogue else (accs[0],)
            for o, v in zip(o_refs, outv):
                o[rows] = v.astype(o.dtype)

        if gk == 1 and sub_rows:
            for s in range(out_specs[0].block_shape[-2] // sub_rows):
                rows = pl.ds(s * sub_rows, sub_rows)
                finish(partials(rows), rows)
        elif gk == 1:
            finish(partials())
        else:
            ps = partials()

            @pl.when(k == 0)
            def _():
                for acc, p in zip(acc_refs, ps):
                    acc[...] = p

            @pl.when((k > 0) & (k < gk - 1))
            def _():
                for acc, p in zip(acc_refs, ps):
                    acc[...] += p

            @pl.when(k == gk - 1)
            def _():
                finish([acc[...] + p for acc, p in zip(acc_refs, ps)])

    res, moved = carried_call(
        body, name, grid,
        list(a_specs) + list(b_specs) + list(extra_specs) + ([ANY] if nc else []), list(out_specs), list(outs),
        [pltpu.VMEM(s, F32) for s in acc_shapes] if gk > 1 else [],
        [*a_ops, *b_ops, *extras, *([carry] if nc else [])], {na + nb + ne: 0} if nc else {}, job)
    res = res[0] if len(res) == 1 else res
    return res if job is None else (res, moved)


def sds(shape, dt):
    return jax.ShapeDtypeStruct(shape, dt)


class Job:
    def __init__(self, ins, outs, aliases, scratch, start, mid, finish):
        self.ins, self.outs, self.aliases, self.scratch = list(ins), list(outs), dict(aliases), list(scratch)
        self.start, self.mid, self.finish = start, mid, finish


def carried_call(body, name, grid, in_specs, out_specs, out_shape, scratch, args, aliases, job, mid_at=0.9):
    sem = ("arbitrary",) * len(grid)
    if job is None:
        res = pl.pallas_call(body, name=name, grid=grid, in_specs=in_specs, out_specs=out_specs, out_shape=out_shape,
                             scratch_shapes=scratch, input_output_aliases=aliases, compiler_params=_params(sem))(*args)
        return list(res), []
    ni, no, ns = len(in_specs), len(out_specs), len(scratch)
    ci, co = len(job.ins), len(job.outs)
    total = 1
    for g in grid:
        total *= g
    mid_step = min(max(int(total * mid_at), 1), total - 1)

    def full(*refs):
        ins, cins = refs[:ni], refs[ni:ni + ci]
        outs, couts = refs[ni + ci:ni + ci + no], refs[ni + ci + no:ni + ci + no + co]
        scr, cscr = refs[ni + ci + no + co:ni + ci + no + co + ns], refs[ni + ci + no + co + ns:]
        step = 0
        for d, g in enumerate(grid):
            step = step * g + pl.program_id(d)

        @pl.when(step == 0)
        def _():
            job.start(cins, couts, cscr)

        body(*ins, *outs, *scr)

        @pl.when(step == mid_step)
        def _():
            job.mid(cins, couts, cscr)

        @pl.when(step == total - 1)
        def _():
            job.finish(cins, couts, cscr)

    al = dict(aliases)
    al.update({ni + k: no + v for k, v in job.aliases.items()})
    res = pl.pallas_call(
        full, name=name, grid=grid, in_specs=in_specs + [ANY] * ci, out_specs=out_specs + [ANY] * co,
        out_shape=out_shape + job.outs, scratch_shapes=scratch + job.scratch, input_output_aliases=al,
        compiler_params=_params(sem))(*args, *job.ins)
    return list(res[:no]), list(res[no:])


def _with_moved(res, job):
    return res if job is not None else (res, [])


def run_job(name, job):
    ci = len(job.ins)

    def body(*refs):
        cins, couts, cscr = refs[:ci], refs[ci:ci + len(job.outs)], refs[ci + len(job.outs):]
        job.start(cins, couts, cscr)
        job.mid(cins, couts, cscr)
        job.finish(cins, couts, cscr)

    return list(pl.pallas_call(
        body, name=name, in_specs=[ANY] * ci, out_specs=[ANY] * len(job.outs), out_shape=job.outs,
        scratch_shapes=job.scratch, input_output_aliases=job.aliases)(*job.ins))


def _place():
    x, y, c = lax.axis_index("x"), lax.axis_index("y"), lax.axis_index("c")
    chips = [((1 - x) if fx else x, (1 - y) if fy else y) for fx, fy in FLIPS]
    return x, y, c, chips


def allgather_small(name, v):
    r = v.shape[0]

    def body(x_ref, out_ref, send_sems, recv_sems, local_sem):
        x, y, c, chips = _place()
        me, sibling = (x, y, c), (x, y, 1 - c)

        def rows(px, py, pc):
            return out_ref.at[4 * px + 2 * py + pc]

        def copy(k, block, to, src=None):
            return pltpu.make_async_remote_copy(
                src_ref=rows(*block) if src is None else src, dst_ref=rows(*block),
                send_sem=send_sems.at[k], recv_sem=recv_sems.at[k], device_id=to, device_id_type=MESH)

        mine = pltpu.make_async_copy(x_ref, rows(*me), local_sem)
        mine.start()
        first = [copy(0, me, sibling, src=x_ref)]
        first += [copy(1 + j, me, (*chip, c), src=x_ref) for j, chip in enumerate(chips)]
        for cp in first:
            cp.start()
        passed = [copy(4 + j, (*chip, c), sibling) for j, chip in enumerate(chips)]
        for j, chip in enumerate(chips):
            copy(1 + j, (*chip, c), me).wait_recv()
            passed[j].start()
        copy(0, sibling, me).wait_recv()
        for j, chip in enumerate(chips):
            copy(4 + j, (*chip, 1 - c), me).wait_recv()
        for cp in first + passed:
            cp.wait_send()
        mine.wait()

    return pl.pallas_call(
        body, name=name, out_shape=sds((8, r, 128), v.dtype),
        in_specs=[pl.BlockSpec(memory_space=pltpu.VMEM)], out_specs=pl.BlockSpec(memory_space=pltpu.VMEM),
        scratch_shapes=[pltpu.SemaphoreType.DMA((7,)), pltpu.SemaphoreType.DMA((7,)), pltpu.SemaphoreType.DMA],
    )(v)


def _half(ref, rows, hf):
    hr = rows // 2
    return ref.at[pl.ds(_aligned(hf * hr, 16), hr)]


def view_lead(ref, p):
    return ref.at[p]


def view_ffn_out(ref, p):
    return ref.at[p // 2, pl.ds(_aligned((p % 2) * FO, 16), FO)]


def _remote(ref, dst, send_sems, recv_sems, idx, to):
    return pltpu.make_async_remote_copy(src_ref=ref, dst_ref=dst, send_sem=send_sems.at[idx], recv_sem=recv_sems.at[idx],
                                        device_id=to, device_id_type=MESH)


def gather_job(items):
    nw = len(items)
    pads = [w for w, it in enumerate(items) if it[1] is view_ffn_out]

    def piece(ref, w, p, hf):
        _, view, rws, part, parts = items[w]
        pr = rws // 2 // parts
        return view(ref, p).at[pl.ds(_aligned(hf * (rws // 2) + part * pr, 16), pr)]

    def pad_copies(outs, scr):
        return [pltpu.make_async_copy(scr[2], outs[w].at[h, pl.ds(2 * FO, FHP - 2 * FO)], scr[3].at[2 * n + h])
                for n, w in enumerate(pads) for h in range(2)]

    def start(_, outs, scr):
        x, y, c, chips = _place()
        if pads:
            scr[2][...] = jnp.zeros_like(scr[2])
            for cp in pad_copies(outs, scr):
                cp.start()
        for w in range(nw):
            mine = piece(outs[w], w, 2 * x + y, c)
            for f, (px, py) in enumerate(chips):
                _remote(mine, mine, scr[0], scr[1], (w, f), (px, py, c)).start()

    def mid(_, outs, scr):
        x, y, c, chips = _place()
        for w in range(nw):
            for f, (px, py) in enumerate(chips):
                land = piece(outs[w], w, 2 * px + py, c)
                _remote(land, land, scr[0], scr[1], (w, f), (px, py, c)).wait_recv()
                _remote(land, land, scr[0], scr[1], (w, 3 + f), (x, y, 1 - c)).start()

    def finish(_, outs, scr):
        x, y, c, chips = _place()
        for w in range(nw):
            for f, (px, py) in enumerate(chips):
                land = piece(outs[w], w, 2 * px + py, 1 - c)
                _remote(land, land, scr[0], scr[1], (w, 3 + f), (x, y, 1 - c)).wait_recv()
        for w in range(nw):
            mine = piece(outs[w], w, 2 * x + y, c)
            for f in range(6):
                _remote(mine, mine, scr[0], scr[1], (w, f), (x, y, 1 - c)).wait_send()
        for cp in pad_copies(outs, scr):
            cp.wait()

    scratch = [pltpu.SemaphoreType.DMA((nw, 6)), pltpu.SemaphoreType.DMA((nw, 6))]
    if pads:
        scratch += [pltpu.VMEM((FHP - 2 * FO, D), BF16), pltpu.SemaphoreType.DMA((2 * len(pads),))]
    bufs = [it[0] for it in items]
    return Job(bufs, [sds(b.shape, BF16) for b in bufs], {w: w for w in range(nw)}, scratch, start, mid, finish)


HBM = pl.BlockSpec(memory_space=pltpu.HBM)
SEM = pl.BlockSpec(memory_space=pltpu.SEMAPHORE)
SPLIT = pltpu.CompilerParams(has_side_effects=pltpu.SideEffectType.DATAFLOW_SIDE_EFFECTING)


def gather_start(name, buf, rows, after):
    def body(*refs):
        out, send_sems, recv_sems, token = refs[1 + len(after):]
        x, y, c, chips = _place()
        mine = _half(out.at[2 * x + y], rows, c)
        for f, (px, py) in enumerate(chips):
            _remote(mine, mine, send_sems, recv_sems, f, (px, py, c)).start()
        token[...] = jnp.zeros_like(token)

    return pl.pallas_call(
        body, name=name,
        out_shape=(pltpu.HBM(buf.shape, buf.dtype), pltpu.SemaphoreType.DMA((3,)), pltpu.SemaphoreType.DMA((3,)),
                   sds((8, 128), F32)),
        in_specs=(HBM,) + (ANY,) * len(after), out_specs=(HBM, SEM, SEM, pl.BlockSpec(memory_space=pltpu.VMEM)),
        input_output_aliases={0: 0}, compiler_params=SPLIT)(pltpu.with_memory_space_constraint(buf, pltpu.HBM), *after)


def gather_wait(name, buf, send_sems, recv_sems, rows, after):
    def body(_, send_sems, recv_sems, *rest):
        out = rest[-1]
        x, y, c, chips = _place()
        mine = _half(out.at[2 * x + y], rows, c)
        for f, (px, py) in enumerate(chips):
            cp = _remote(mine, _half(out.at[2 * px + py], rows, c), send_sems, recv_sems, f, (px, py, c))
            cp.wait_send()
            cp.wait_recv()

    return pl.pallas_call(
        body, name=name, out_shape=pltpu.HBM(buf.shape, buf.dtype),
        in_specs=(HBM, SEM, SEM) + (ANY,) * len(after), out_specs=HBM, input_output_aliases={0: 0},
        compiler_params=SPLIT)(buf, send_sems, recv_sems, *after)


def forward_job(buf, rows):
    def copies(outs, scr, hf):
        x, y, c, chips = _place()
        half = c if hf == 0 else 1 - c
        return [_remote(_half(outs[0].at[2 * px + py], rows, half), _half(outs[0].at[2 * px + py], rows, half),
                        scr[0], scr[1], f, (x, y, 1 - c)) for f, (px, py) in enumerate(chips)]

    def start(_, outs, scr):
        for cp in copies(outs, scr, 0):
            cp.start()

    def finish(_, outs, scr):
        for cp in copies(outs, scr, 1):
            cp.wait_recv()
        for cp in copies(outs, scr, 0):
            cp.wait_send()

    return Job([buf], [sds(buf.shape, buf.dtype)], {0: 0},
               [pltpu.SemaphoreType.DMA((3,)), pltpu.SemaphoreType.DMA((3,))], start, lambda *_: None, finish)


def reduce_sibling_job(items):
    nw = len(items)

    def copies(ins, got, scr):
        x, y, c, _ = _place()
        return [_remote(_half(view(ins[w], p), rws, 1 - c), got[w].at[p], scr[0], scr[1], (w, p), (x, y, 1 - c))
                for w, (_, view, rws, _) in enumerate(items) for p in range(4)]

    def start(ins, got, scr):
        for cp in copies(ins, got, scr):
            cp.start()

    def finish(ins, got, scr):
        for cp in copies(ins, got, scr):
            cp.wait()

    return Job([it[0] for it in items], [sds((4, it[2] // 2, it[3]), BF16) for it in items], {},
               [pltpu.SemaphoreType.DMA((nw, 4)), pltpu.SemaphoreType.DMA((nw, 4))], start, lambda *_: None, finish)


def reduce_chips_job(qs):
    nw = len(qs)

    def copies(ins, got, scr):
        x, y, c, chips = _place()
        return [_remote(ins[w].at[2 * px + py], got[w].at[f], scr[0], scr[1], (w, f), (px, py, c))
                for w in range(nw) for f, (px, py) in enumerate(chips)]

    def start(ins, got, scr):
        for cp in copies(ins, got, scr):
            cp.start()

    def finish(ins, got, scr):
        for cp in copies(ins, got, scr):
            cp.wait()

    return Job(qs, [sds((3,) + q.shape[1:], BF16) for q in qs], {},
               [pltpu.SemaphoreType.DMA((nw, 3)), pltpu.SemaphoreType.DMA((nw, 3))], start, lambda *_: None, finish)


def share_halves_job(gs):
    nw = len(gs)

    def start(_, outs, scr):
        x, y, c, _ = _place()
        for w in range(nw):
            mine = _half(outs[w], gs[w].shape[0], c)
            _remote(mine, mine, scr[0], scr[1], w, (x, y, 1 - c)).start()

    def finish(_, outs, scr):
        x, y, c, _ = _place()
        for w in range(nw):
            mine = _half(outs[w], gs[w].shape[0], c)
            theirs = _half(outs[w], gs[w].shape[0], 1 - c)
            _remote(mine, mine, scr[0], scr[1], w, (x, y, 1 - c)).wait_send()
            _remote(theirs, theirs, scr[0], scr[1], w, (x, y, 1 - c)).wait_recv()

    return Job(gs, [sds(g.shape, F32) for g in gs], {w: w for w in range(nw)},
               [pltpu.SemaphoreType.DMA((nw,)), pltpu.SemaphoreType.DMA((nw,))], start, lambda *_: None, finish)


def rope_tables(t):
    pos = jnp.arange(t, dtype=F32)
    inv_freq = ROPE_THETA ** (-jnp.arange(0, ROT, 2, dtype=F32) / ROT)
    ang = pos[:, None] * inv_freq[None, :]
    cos, sin = jnp.cos(ang), jnp.sin(ang)
    d = jnp.arange(128) % HD
    half = ROT // 2
    cs = jnp.take(cos, d % half, axis=1)
    sn = jnp.take(sin, d % half, axis=1)
    cc = jnp.where(d[None] < ROT, cs, 1.0)
    sa = jnp.where(d[None] < half, -sn, 0.0)
    sb = jnp.where((d[None] >= half) & (d[None] < ROT), sn, 0.0)
    return cc, sa, sb


def _rope(v, cc, sa, sb):
    w = v.shape[1]
    reps = w // 128
    half = ROT // 2
    return (v * jnp.tile(cc, (1, reps)) + pltpu.roll(v, w - half, 1) * jnp.tile(sa, (1, reps))
            + pltpu.roll(v, half, 1) * jnp.tile(sb, (1, reps)))


def _rope_t(dv, cc, sa, sb):
    w = dv.shape[1]
    reps = w // 128
    half = ROT // 2
    return (dv * jnp.tile(cc, (1, reps)) + pltpu.roll(dv * jnp.tile(sa, (1, reps)), half, 1)
            + pltpu.roll(dv * jnp.tile(sb, (1, reps)), w - half, 1))


def pool_fwd(h, b_in, t, tm):
    tm = min(tm, t)
    per = tm // HALO

    def body(prev_ref, cur_ref, b_ref, o_ref, xx):
        i = pl.program_id(0)
        b = b_ref[...]
        xx[pl.ds(0, HALO), :] = jnp.where(i > 0, prev_ref[...] + b, 0.0)
        xx[pl.ds(HALO, tm), :] = cur_ref[...] + b
        tpos = i * tm + lax.broadcasted_iota(jnp.int32, (tm, PG), 0) + 1
        for gi, w in enumerate(POOL_WINDOWS):
            cols = pl.ds(gi * PG, PG)
            acc = xx[pl.ds(HALO, tm), cols]
            for s in range(1, w):
                acc = acc + xx[pl.ds(HALO - s, tm), cols]
            cnt = jnp.minimum(tpos, w).astype(F32)
            o_ref[:, cols] = (acc / cnt - xx[pl.ds(HALO, tm), cols]).astype(o_ref.dtype)

    return pl.pallas_call(
        body, name="pool_fwd", grid=(t // tm,),
        in_specs=[pl.BlockSpec((HALO, PW), lambda i: (jnp.maximum(i * per - 1, 0), 0)),
                  pl.BlockSpec((tm, PW), lambda i: (i, 0)), pl.BlockSpec((1, PW), lambda i: (0, 0))],
        out_specs=pl.BlockSpec((tm, PW), lambda i: (i, 0)), out_shape=sds((t, PW), BF16),
        scratch_shapes=[pltpu.VMEM((tm + HALO, PW), F32)], compiler_params=_params(("arbitrary",)),
    )(h, h, b_in)


def pool_bwd(dpooled, t, tm):
    tm = min(tm, t)
    per = tm // HALO
    nt = t // tm

    def body(cur_ref, nxt_ref, o_ref, db_ref, ee):
        i = pl.program_id(0)
        tpos = i * tm + lax.broadcasted_iota(jnp.int32, (tm, PG), 0) + 1
        for gi, w in enumerate(POOL_WINDOWS):
            cols = pl.ds(gi * PG, PG)
            ee[pl.ds(0, tm), cols] = cur_ref[:, cols] / jnp.minimum(tpos, w).astype(F32)
            ee[pl.ds(tm, HALO), cols] = jnp.where(i < nt - 1, nxt_ref[:, cols] / float(w), 0.0)
        for gi, w in enumerate(POOL_WINDOWS):
            cols = pl.ds(gi * PG, PG)
            acc = ee[pl.ds(0, tm), cols]
            for s in range(1, w):
                acc = acc + ee[pl.ds(s, tm), cols]
            dxp = acc - cur_ref[:, cols]
            o_ref[:, cols] = dxp.astype(o_ref.dtype)
            part = colsum(dxp)

            @pl.when(i == 0)
            def _(cols=cols, part=part):
                db_ref[:, cols] = part

            @pl.when(i > 0)
            def _(cols=cols, part=part):
                db_ref[:, cols] += part

    return pl.pallas_call(
        body, name="pool_bwd", grid=(nt,),
        in_specs=[pl.BlockSpec((tm, PW), lambda i: (i, 0)),
                  pl.BlockSpec((HALO, PW), lambda i: (jnp.minimum((i + 1) * per, t // HALO - 1), 0))],
        out_specs=[pl.BlockSpec((tm, PW), lambda i: (i, 0)), pl.BlockSpec((1, PW), lambda i: (0, 0))],
        out_shape=[sds((t, PW), BF16), sds((1, PW), F32)],
        scratch_shapes=[pltpu.VMEM((tm + HALO, PW), F32)], compiler_params=_params(("arbitrary",)),
    )(dpooled, dpooled)


def _scores(qh, kp, kc, mask_p, mask_c, sink):
    sp = jnp.where(mask_p, lax.dot_general(qh, kp, NT, preferred_element_type=F32), -1e30)
    sc = jnp.where(mask_c, lax.dot_general(qh, kc, NT, preferred_element_type=F32), -1e30)
    m = jnp.maximum(jnp.maximum(jnp.max(sp, axis=-1, keepdims=True), jnp.max(sc, axis=-1, keepdims=True)), sink)
    pp, pc = jnp.exp(sp - m), jnp.exp(sc - m)
    es = jnp.exp(sink - m)
    inv = 1.0 / (jnp.sum(pp, axis=-1, keepdims=True) + jnp.sum(pc, axis=-1, keepdims=True) + es)
    return pp * inv, pc * inv, es * inv


GRP = N_Q // N_KV


def _masks(n):
    qi = lax.broadcasted_iota(jnp.int32, (GRP * BLK, BLK), 0) % BLK
    kj = lax.broadcasted_iota(jnp.int32, (GRP * BLK, BLK), 1)
    return (kj > qi) & (n > 0), kj <= qi


def _head(hk, g):
    return pl.ds(HD * (GRP * hk + g), HD)


def _stack_heads(ref, hk):
    return jnp.concatenate([ref[:, _head(hk, g)] for g in range(GRP)], axis=0)


def _stack_sinks(s_ref, hk):
    return jnp.concatenate([jnp.full((BLK, 1), s_ref[0, GRP * hk + g], F32) for g in range(GRP)], axis=0)


def attn_fwd(q, k, v, sinks, t, job=None):
    def body(s_ref, q_ref, kp_ref, kc_ref, vp_ref, vc_ref, o_ref):
        n = pl.program_id(0)
        mask_p, mask_c = _masks(n)
        for hk in range(N_KV):
            kv = pl.ds(HD * hk, HD)
            pp, pc, _ = _scores(_stack_heads(q_ref, hk), kp_ref[:, kv], kc_ref[:, kv], mask_p, mask_c,
                                _stack_sinks(s_ref, hk))
            o = (lax.dot_general(pp.astype(BF16), vp_ref[:, kv], NN, preferred_element_type=F32)
                 + lax.dot_general(pc.astype(BF16), vc_ref[:, kv], NN, preferred_element_type=F32))
            for g in range(GRP):
                o_ref[:, _head(hk, g)] = o[g * BLK:(g + 1) * BLK].astype(o_ref.dtype)

    prev = lambda n: (jnp.maximum(n - 1, 0), 0)
    cur = lambda n: (n, 0)
    res, moved = carried_call(
        body, "attn_fwd", (t // BLK,),
        [pl.BlockSpec(memory_space=pltpu.SMEM), pl.BlockSpec((BLK, QW), cur),
         pl.BlockSpec((BLK, KVW), prev), pl.BlockSpec((BLK, KVW), cur),
         pl.BlockSpec((BLK, KVW), prev), pl.BlockSpec((BLK, KVW), cur)],
        [pl.BlockSpec((BLK, QW), cur)], [sds((t, QW), BF16)], [], [sinks, q, k, k, v, v], {}, job)
    return res[0], moved


def attn_bwd(q, k, v, do, sinks, t):
    nb = t // BLK

    def body(s_ref, q_ref, do_ref, kp_ref, kc_ref, vp_ref, vc_ref, dq_ref, dk_ref, dv_ref, ds_ref, dkc, dvc):
        n = pl.program_id(0)

        @pl.when(n == 0)
        def _():
            dkc[...] = jnp.zeros_like(dkc)
            dvc[...] = jnp.zeros_like(dvc)
            ds_ref[...] = jnp.zeros_like(ds_ref)

        @pl.when(n < nb)
        def _():
            mask_p, mask_c = _masks(n)
            lane = lax.broadcasted_iota(jnp.int32, (1, 128), 1)
            dsink = jnp.zeros((1, 128), F32)
            for hk in range(N_KV):
                kv = pl.ds(HD * hk, HD)
                kp, kc, vp, vc = kp_ref[:, kv], kc_ref[:, kv], vp_ref[:, kv], vc_ref[:, kv]
                qs, dos = _stack_heads(q_ref, hk), _stack_heads(do_ref, hk)
                pp, pc, ps = _scores(qs, kp, kc, mask_p, mask_c, _stack_sinks(s_ref, hk))
                dpp = lax.dot_general(dos, vp, NT, preferred_element_type=F32)
                dpc = lax.dot_general(dos, vc, NT, preferred_element_type=F32)
                delta = jnp.sum(pp * dpp, axis=-1, keepdims=True) + jnp.sum(pc * dpc, axis=-1, keepdims=True)
                dsp = (pp * (dpp - delta)).astype(BF16)
                dsc = (pc * (dpc - delta)).astype(BF16)
                sd = ps * delta
                dq = (lax.dot_general(dsp, kp, NN, preferred_element_type=F32)
                      + lax.dot_general(dsc, kc, NN, preferred_element_type=F32))
                for g in range(GRP):
                    rows = slice(g * BLK, (g + 1) * BLK)
                    dsink = dsink + jnp.where(lane == GRP * hk + g, -jnp.sum(sd[rows]), 0.0)
                    dq_ref[:, _head(hk, g)] = dq[rows]
                dk_ref[:, kv] = dkc[:, kv] + lax.dot_general(dsp, qs, TN, preferred_element_type=F32)
                dv_ref[:, kv] = dvc[:, kv] + lax.dot_general(pp.astype(BF16), dos, TN, preferred_element_type=F32)
                dkc[:, kv] = lax.dot_general(dsc, qs, TN, preferred_element_type=F32)
                dvc[:, kv] = lax.dot_general(pc.astype(BF16), dos, TN, preferred_element_type=F32)
            ds_ref[...] += dsink

        @pl.when(n == nb)
        def _():
            dk_ref[...] = dkc[...]
            dv_ref[...] = dvc[...]

    cur = lambda n: (jnp.minimum(n, nb - 1), 0)
    prev = lambda n: (jnp.clip(n - 1, 0, nb - 1), 0)
    return pl.pallas_call(
        body, name="attn_bwd", grid=(nb + 1,),
        in_specs=[pl.BlockSpec(memory_space=pltpu.SMEM), pl.BlockSpec((BLK, QW), cur), pl.BlockSpec((BLK, QW), cur),
                  pl.BlockSpec((BLK, KVW), prev), pl.BlockSpec((BLK, KVW), cur),
                  pl.BlockSpec((BLK, KVW), prev), pl.BlockSpec((BLK, KVW), cur)],
        out_specs=[pl.BlockSpec((BLK, QW), cur), pl.BlockSpec((BLK, KVW), prev), pl.BlockSpec((BLK, KVW), prev),
                   pl.BlockSpec((1, 128), lambda n: (0, 0))],
        out_shape=[sds((t, QW), F32), sds((t, KVW), F32), sds((t, KVW), F32), sds((1, 128), F32)],
        scratch_shapes=[pltpu.VMEM((BLK, KVW), F32), pltpu.VMEM((BLK, KVW), F32)],
        compiler_params=_params(("arbitrary",)),
    )(sinks, q, do, k, k, v, v)


def _adamw(w, g, m, v):
    m2 = B1 * m + (1.0 - B1) * g
    v2 = B2 * v + (1.0 - B2) * jnp.square(g)
    m_hat = m2 / (1.0 - B1 ** STEP)
    v_hat = v2 / (1.0 - B2 ** STEP)
    return -LR * (m_hat / (jnp.sqrt(v_hat) + EPS) + WD * w), m2, v2


def ada_fwd(c16, w_ada, b_sh):
    tn = 512

    def body(c_ref, w_ref, b_ref, o_ref):
        cv = c_ref[...]
        sc = (cv * _sigmoid(cv)).astype(BF16)
        o_ref[...] = lax.dot_general(sc, w_ref[...].astype(BF16), NN, preferred_element_type=F32) + b_ref[...]

    return pl.pallas_call(
        body, name="ada_fwd", grid=(ADA_SH // tn,),
        in_specs=[pl.BlockSpec((16, D), lambda j: (0, 0)), pl.BlockSpec((D, tn), lambda j: (0, j)),
                  pl.BlockSpec((1, tn), lambda j: (0, j))],
        out_specs=pl.BlockSpec((16, tn), lambda j: (0, j)), out_shape=sds((16, ADA_SH), F32),
        compiler_params=_params(("arbitrary",)),
    )(c16, w_ada, b_sh)


def ada_bwd_adam(c16, gm16, w, m, v, job):
    tm, tn = 256, 512

    def body(c_ref, g_ref, w_ref, m_ref, v_ref, go_ref, d_ref, mo_ref, vo_ref):
        cv = c_ref[...]
        sc = (cv * _sigmoid(cv)).astype(BF16)
        g = lax.dot_general(sc, g_ref[...].astype(BF16), TN, preferred_element_type=F32)
        dl, m2, v2 = _adamw(w_ref[...], g, m_ref[...], v_ref[...])
        go_ref[...] = g
        d_ref[...] = dl
        mo_ref[...] = m2
        vo_ref[...] = v2

    blk = pl.BlockSpec((tm, tn), lambda i, j: (i, j))
    return carried_call(
        body, "ada_bwd_adam", (D // tm, ADA_SH // tn),
        [pl.BlockSpec((16, tm), lambda i, j: (0, i)), pl.BlockSpec((16, tn), lambda i, j: (0, j)), blk, blk, blk],
        [blk] * 4, [sds((D, ADA_SH), F32)] * 4, [], [c16, gm16, w, m, v], {}, job)


def adam_rows(name, w, g, m, v, tm):
    rows, cols = w.shape

    def fn(wv, gv, mv, vv):
        gv = gv[:, :cols]
        dl, m2, v2 = _adamw(wv, gv, mv, vv)
        return gv, dl, m2, v2

    return rowmap(name, fn, [T_(w), T_(g), T_(m), T_(v)], [(cols, F32)] * 4, rows=rows, tm=tm)


def adam_small(name, w, g, m, v):
    def body(w_ref, g_ref, m_ref, v_ref, d_ref, mo_ref, vo_ref):
        dl, m2, v2 = _adamw(w_ref[...], g_ref[...], m_ref[...], v_ref[...])
        d_ref[...] = dl
        mo_ref[...] = m2
        vo_ref[...] = v2

    return pl.pallas_call(body, name=name, out_shape=[sds(w.shape, F32)] * 3)(w, g, m, v)


def sum_devices(allv):
    def body(a_ref, o_ref):
        acc = a_ref[0]
        for d in range(1, 8):
            acc = acc + a_ref[d]
        o_ref[...] = acc

    return pl.pallas_call(body, name="sum_devices", out_shape=sds(allv.shape[1:], F32))(allv)


def _ln_fwd(z, g, b):
    mu = jnp.mean(z, axis=-1, keepdims=True)
    zc = z - mu
    var = jnp.mean(jnp.square(zc), axis=-1, keepdims=True)
    return zc * lax.rsqrt(var + LN_EPS) * g + b


def _ln_bwd(z, g, dout):
    mu = jnp.mean(z, axis=-1, keepdims=True)
    zc = z - mu
    var = jnp.mean(jnp.square(zc), axis=-1, keepdims=True)
    rstd = lax.rsqrt(var + LN_EPS)
    xh = zc * rstd
    dxh = dout * g
    dz = rstd * (dxh - jnp.mean(dxh, axis=-1, keepdims=True) - xh * jnp.mean(dxh * xh, axis=-1, keepdims=True))
    return dz, colsum(dout * xh), colsum(dout)


def modulate(name, xin, shift, scale, t):
    return rowmap(name, lambda xv, sh, sc: xv * (1.0 + sc) + sh, [T_(xin), B_(shift), B_(scale)], [(D, BF16)],
                  rows=t, tm=512)


def residual_ln_mod(name, xin, y, gate, lg, lb, wgt, shift_n, scale_n, t):
    def fn(xv, yv, gt, g, b, sh, sc):
        z = ALPHA * xv + (wgt * (1.0 + gt)) * yv
        xo = _ln_fwd(z, g, b)
        return xo, z, xo * (1.0 + sc) + sh

    return rowmap(name, fn, [T_(xin), T_(y), B_(gate), B_(lg), B_(lb), B_(shift_n), B_(scale_n)],
                  [(D, F32), (D, F32), (D, BF16)], rows=t, tm=256)


def residual_ln_bwd(name, z, dnext, y, gate, lg, wgt, t):
    dzn, dun, xn, scn = dnext

    def fn(zv, yv, gt, g, dzv, duv, xv, sc):
        dv = ALPHA * dzv + duv * (1.0 + sc)
        dz, dg, db = _ln_bwd(zv, g, dv)
        return dz, (wgt * (1.0 + gt)) * dz, dg, db, colsum(wgt * dz * yv), colsum(duv), colsum(duv * xv)

    return rowmap(name, fn, [T_(z), T_(y), B_(gate), B_(lg), T_(dzn), T_(dun), T_(xn), B_(scn)],
                  [(D, F32), (D, BF16)], [(1, D)] * 5, rows=t, tm=256)


def residual_ln_loss_bwd(name, xin, y, tgt, gate, lg, lb, wgt, t):
    def fn(xv, yv, tv, gt, g, b):
        z = ALPHA * xv + (wgt * (1.0 + gt)) * yv
        d = _ln_fwd(z, g, b) - tv
        dz, dg, db = _ln_bwd(z, g, d * (1.0 / D))
        return dz, (wgt * (1.0 + gt)) * dz, dg, db, colsum(wgt * dz * yv), jnp.sum(d * d).reshape(1, 1)

    dz, dy, dlg, dlb, dgate, sq = rowmap(
        name, fn, [T_(xin), T_(y), T_(tgt), B_(gate), B_(lg), B_(lb)], [(D, F32), (D, BF16)],
        [(1, D), (1, D), (1, D), (1, 1)], rows=t, tm=256)
    return dz, dy, dlg, dlb, dgate, sq


def modulate_bwd(name, dz, du, xin, scale, t):
    def fn(dzv, duv, xv, sc):
        return ALPHA * dzv + duv * (1.0 + sc), colsum(duv), colsum(duv * xv)

    return rowmap(name, fn, [T_(dz), T_(du), T_(xin), B_(scale)], [(D, F32)], [(1, D), (1, D)], rows=t, tm=256)


def ffn_fwd(tag, u, wi, t, up_job, down_job=None):
    tm = min(1024, t)
    tn = 256
    per = FHP // tn

    def act(accs, _):
        a, b = accs
        s = _sigmoid(a)
        sl = a * s
        return b * (s * (1.0 + a * (1.0 - s))), sl, sl * b

    tmu = min(2048, t)
    hblk = pl.BlockSpec((tmu, tn), lambda i, j, k: (i, j))
    (ha, hb, g), up_moved = mm(
        tag + "_up", [u], [wi, wi], [(0, 0, 0), (0, 1, 1)], dims=NT, grid=(t // tmu, 2 * per, 1),
        a_specs=[pl.BlockSpec((tmu, D), lambda i, j, k: (i, 0))],
        b_specs=[pl.BlockSpec((None, tn, D), lambda i, j, k: (j // per, j % per, 0)),
                 pl.BlockSpec((None, tn, D), lambda i, j, k: (2 + j // per, j % per, 0))],
        outs=[sds((t, 2 * FHP), BF16)] * 3, out_specs=[hblk] * 3, acc_shapes=[(tmu, tn)] * 2, epilogue=act, job=up_job,
        sub_rows=tmu // 2)
    wo = up_moved[0].reshape(2 * FHP, D)
    tk = FHP
    y, down_moved = _with_moved(mm(
        tag + "_down", [g], [wo], [(0, 0, 0)], dims=NN, grid=(t // tm, 2, 2),
        a_specs=[pl.BlockSpec((tm, tk), lambda i, j, k: (i, k))],
        b_specs=[pl.BlockSpec((tk, D // 2), lambda i, j, k: (k, j))],
        outs=[sds((t, D), F32)], out_specs=[pl.BlockSpec((tm, D // 2), lambda i, j, k: (i, j))],
        acc_shapes=[(tm, D // 2)], job=down_job), down_job)
    return ha, hb, g, y, wo, up_moved, down_moved


def ffn_bwd(tag, u, ha, hb, g, dy, wi, wo, t, sp, dact_job=None, dwo_job=None):
    tm = min(1024, t)

    def dact(accs, ex):
        dg = accs[0]
        return dg * ex[0].astype(F32), dg * ex[1].astype(F32)

    tn = 256
    tmu = min(2048, t)
    hblk = pl.BlockSpec((tmu, tn), lambda i, j, k: (i, j))
    (dha, dhb), dact_moved = _with_moved(mm(
        tag + "_dact", [dy], [wo], [(0, 0, 0)], dims=NT, grid=(t // tmu, 2 * FHP // tn, 1),
        a_specs=[pl.BlockSpec((tmu, D), lambda i, j, k: (i, 0))],
        b_specs=[pl.BlockSpec((tn, D), lambda i, j, k: (j, 0))],
        outs=[sds((t, 2 * FHP), BF16)] * 2, out_specs=[hblk] * 2, acc_shapes=[(tmu, tn)],
        epilogue=dact, extras=[ha, hb], extra_specs=[hblk] * 2, job=dact_job, sub_rows=tmu // 2), dact_job)
    tk = min(2048, t)
    th = FHP // 2
    dwo, dwo_moved = _with_moved(mm(
        tag + "_dwo", [g], [dy], [(0, 0, 0)], dims=TN, grid=(4, 2, t // tk),
        a_specs=[pl.BlockSpec((tk, th), lambda i, j, k: (k, i))],
        b_specs=[pl.BlockSpec((tk, D // 2), lambda i, j, k: (k, j))],
        outs=[sds((2 * FHP, D), BF16)], out_specs=[pl.BlockSpec((th, D // 2), lambda i, j, k: (i, j))],
        acc_shapes=[(th, D // 2)], job=dwo_job), dwo_job)
    dwo = dwo.reshape(2, FHP, D)

    def dwi_part(part, dh, carry, job):
        return mm(
            f"{tag}_dwi{part}", [dh], [u], [(0, 0, 0)], dims=TN, grid=(4, 2, t // tk),
            a_specs=[pl.BlockSpec((tk, th), lambda i, j, k: (k, i))],
            b_specs=[pl.BlockSpec((tk, D // 2), lambda i, j, k: (k, j))],
            outs=[sds((4, FHP, D), BF16)],
            out_specs=[pl.BlockSpec((None, th, D // 2), lambda i, j, k: (2 * part + i // 2, i % 2, j))],
            acc_shapes=[(th, D // 2)], carry=carry, job=job)

    dwi, (sib_fo,) = dwi_part(0, dha, None, reduce_sibling_job([(dwo, view_ffn_out, FO, D)]))
    q_fo = chip_sum(tag + "_chipsum_fo", dwo, sib_fo, sp, FO, FO // 2, ffn_out=True)
    dwi, (far_fo,) = dwi_part(1, dhb, dwi, reduce_chips_job([q_fo]))
    (sib_fi,) = run_job(tag + "_sibling_fi", reduce_sibling_job([(dwi, view_lead, FHP, D)]))
    q_fi = chip_sum(tag + "_chipsum_fi", dwi, sib_fi, sp, FHP, FHP // 8)
    tmd = min(512, t)
    du, (far_fi,) = mm(
        tag + "_du", [dha, dhb], [wi, wi], [(0, 0, 0), (1, 1, 0)], dims=NN, grid=(t // tmd, 2, 2),
        a_specs=[pl.BlockSpec((tmd, FHP), lambda i, j, k: (i, k))] * 2,
        b_specs=[pl.BlockSpec((None, FHP, D // 2), lambda i, j, k: (k, 0, j)),
                 pl.BlockSpec((None, FHP, D // 2), lambda i, j, k: (2 + k, 0, j))],
        outs=[sds((t, D), F32)], out_specs=[pl.BlockSpec((tmd, D // 2), lambda i, j, k: (i, j))],
        acc_shapes=[(tmd, D // 2)], job=reduce_chips_job([q_fi]))
    return du, (q_fi, far_fi), (q_fo, far_fo), dact_moved, dwo_moved


def mix_fwd(u, wts, b_in, pool_scale, sinks, tabs, t, in_job, attn_job):
    w_in, wp, wba, wbb, wo = wts
    tm = min(1024, t)
    tmh = min(512, t)
    h, in_moved = mm("mix_in", [u], [w_in], [(0, 0, 0)], dims=NN, grid=(t // tmh, 4, 1),
                     a_specs=[pl.BlockSpec((tmh, D), lambda i, j, k: (i, 0))],
                     b_specs=[pl.BlockSpec((None, D, IN_SH), lambda i, j, k: (j, 0, 0))],
                     outs=[sds((t, IN_W), F32)], out_specs=[pl.BlockSpec((tmh, IN_SH), lambda i, j, k: (i, j))],
                     acc_shapes=[(tmh, IN_SH)], job=in_job)
    attn_job = attn_job(in_moved)
    pooled = pool_fwd(h, b_in, t, 512)
    gblk = pl.BlockSpec((tm, PG), lambda i, j, k: (i, j))
    mixed = mm("mix_pool", [pooled], [wp], [(0, 0, 0)], dims=NN, grid=(t // tm, 4, 1), a_specs=[gblk],
               b_specs=[pl.BlockSpec((None, PG, PG), lambda i, j, k: (j, 0, 0))],
               outs=[sds((t, PW), F32)], out_specs=[gblk], acc_shapes=[(tm, PG)])
    pm = rowmap("mix_pscale", lambda mv, ps: mv * ps, [T_(mixed), B_(pool_scale)], [(PW, BF16)], rows=t, tm=512)

    def branch(name, a, w):
        return mm(name, [a], [w], [(0, 0, 0)], dims=NN, grid=(t // tm, 4, 1),
                  a_specs=[pl.BlockSpec((tm, PW), lambda i, j, k: (i, 0))],
                  b_specs=[pl.BlockSpec((None, PW, D // 4), lambda i, j, k: (j, 0, 0))],
                  outs=[sds((t, D), F32)], out_specs=[pl.BlockSpec((tm, D // 4), lambda i, j, k: (i, j))],
                  acc_shapes=[(tm, D // 4)])

    ya = branch("mix_branch_a", pm, wba)

    def qkv(hq, hk, hv, bq, bk, bv, cc, sa, sb):
        return (_rope(hq + bq, cc, sa, sb) * (HD ** -0.5), _rope(hk + bk, cc, sa, sb), hv + bv)

    qr, kr, vv = rowmap(
        "mix_rope", qkv,
        [T_(h, QW, 1), T_(h, KVW, 8), T_(h, KVW, 9), B_(b_in, QW, 1), B_(b_in, KVW, 8), B_(b_in, KVW, 9),
         T_(tabs[0]), T_(tabs[1]), T_(tabs[2])],
        [(QW, BF16), (KVW, BF16), (KVW, BF16)], rows=t, tm=512)
    attn, attn_moved = attn_fwd(qr, kr, vv, sinks, t, attn_job)
    yb = branch("mix_branch_b", attn, wbb)
    cw = 512

    def merge(ga, gb, ba, bb, yav, ybv):
        return _sigmoid(ga + ba) * yav + _sigmoid(gb + bb) * ybv

    merged = rowmap(
        "mix_merge", merge,
        [T_(h, cw, 5), T_(h, cw, 9), B_(b_in, cw, 5), B_(b_in, cw, 9), T_(ya, cw), T_(yb, cw)],
        [(D, BF16)], rows=t, tm=512, ncol=D // cw)
    y = mm("mix_out", [merged], [wo], [(0, 0, 0)], dims=NN, grid=(t // tm, 2, 1),
           a_specs=[pl.BlockSpec((tm, D), lambda i, j, k: (i, 0))],
           b_specs=[pl.BlockSpec((D, D // 2), lambda i, j, k: (0, j))],
           outs=[sds((t, D), F32)], out_specs=[pl.BlockSpec((tm, D // 2), lambda i, j, k: (i, j))],
           acc_shapes=[(tm, D // 2)])
    return y, (h, pooled, mixed, pm, ya, qr, kr, vv, attn, yb, merged), attn_moved


def mix_bwd(u, saved, dy, wts, b_in, pool_scale, sinks, tabs, t):
    h, pooled, mixed, pm, ya, qr, kr, vv, attn, yb, merged = saved
    w_in, wp, wba, wbb, wo = wts
    tm = min(1024, t)
    tk = min(2048, t)
    dmerged = mm("mix_dmerged", [dy], [wo], [(0, 0, 0)], dims=NT, grid=(t // tm, 2, 1),
                 a_specs=[pl.BlockSpec((tm, D), lambda i, j, k: (i, 0))],
                 b_specs=[pl.BlockSpec((D // 2, D), lambda i, j, k: (j, 0))],
                 outs=[sds((t, D), F32)], out_specs=[pl.BlockSpec((tm, D // 2), lambda i, j, k: (i, j))],
                 acc_shapes=[(tm, D // 2)])
    half = pl.BlockSpec((tk, D // 2), lambda i, j, k: (k, i))
    dwo = mm("mix_dwo", [merged], [dy], [(0, 0, 0)], dims=TN, grid=(2, 2, t // tk), a_specs=[half],
             b_specs=[pl.BlockSpec((tk, D // 2), lambda i, j, k: (k, j))],
             outs=[sds((D, D), BF16)], out_specs=[pl.BlockSpec((D // 2, D // 2), lambda i, j, k: (i, j))],
             acc_shapes=[(D // 2, D // 2)])
    cw = 512

    def dmerge(dm, ga, gb, ba, bb, yav, ybv):
        sa_, sb_ = _sigmoid(ga + ba), _sigmoid(gb + bb)
        dga = dm * yav * sa_ * (1.0 - sa_)
        dgb = dm * ybv * sb_ * (1.0 - sb_)
        return dm * sa_, dm * sb_, dga, dgb, colsum(dga), colsum(dgb)

    dya, dyb, dgla, dglb, dbga, dbgb = rowmap(
        "mix_dmerge", dmerge,
        [T_(dmerged, cw), T_(h, cw, 5), T_(h, cw, 9), B_(b_in, cw, 5), B_(b_in, cw, 9), T_(ya, cw), T_(yb, cw)],
        [(D, BF16)] * 4, [(1, D), (1, D)], rows=t, tm=512, ncol=D // cw)

    def dbranch(name, dyv, act, w):
        dwb = mm(name + "_dw", [act], [dyv], [(0, 0, 0)], dims=TN, grid=(1, 4, t // tk),
                 a_specs=[pl.BlockSpec((tk, PW), lambda i, j, k: (k, 0))],
                 b_specs=[pl.BlockSpec((tk, D // 4), lambda i, j, k: (k, j))],
                 outs=[sds((4, PW, D // 4), BF16)], out_specs=[pl.BlockSpec((None, PW, D // 4), lambda i, j, k: (j, 0, 0))],
                 acc_shapes=[(PW, D // 4)])
        return dwb, lambda dt: mm(
            name + "_dx", [dyv], [w], [(0, 0, 0)], dims=NT, grid=(t // tm, 1, 4),
            a_specs=[pl.BlockSpec((tm, D // 4), lambda i, j, k: (i, k))],
            b_specs=[pl.BlockSpec((None, PW, D // 4), lambda i, j, k: (k, 0, 0))],
            outs=[sds((t, PW), dt)], out_specs=[pl.BlockSpec((tm, PW), lambda i, j, k: (i, 0))], acc_shapes=[(tm, PW)])

    dwba, dpm_fn = dbranch("mix_dbranch_a", dya, pm, wba)
    dwbb, dattn_fn = dbranch("mix_dbranch_b", dyb, attn, wbb)
    dpm, dattn = dpm_fn(F32), dattn_fn(BF16)
    dmixed, dps = rowmap("mix_dpscale", lambda dp, mv, ps: (dp * ps, colsum(dp * mv)),
                         [T_(dpm), T_(mixed), B_(pool_scale)], [(PW, BF16)], [(1, PW)], rows=t, tm=512)
    gblk = pl.BlockSpec((tm, PG), lambda i, j, k: (i, j))
    dpooled = mm("mix_dpool", [dmixed], [wp], [(0, 0, 0)], dims=NT, grid=(t // tm, 4, 1), a_specs=[gblk],
                 b_specs=[pl.BlockSpec((None, PG, PG), lambda i, j, k: (j, 0, 0))],
                 outs=[sds((t, PW), F32)], out_specs=[gblk], acc_shapes=[(tm, PG)])
    kblk = pl.BlockSpec((tk, PG), lambda i, j, k: (k, i))
    dwp = mm("mix_dwpool", [pooled], [dmixed], [(0, 0, 0)], dims=TN, grid=(4, 1, t // tk), a_specs=[kblk], b_specs=[kblk],
             outs=[sds((4, PG, PG), BF16)], out_specs=[pl.BlockSpec((None, PG, PG), lambda i, j, k: (i, 0, 0))],
             acc_shapes=[(PG, PG)])
    dxp, dbxp = pool_bwd(dpooled, t, 512)
    dqr, dkr, dvv, dsinks = attn_bwd(qr, kr, vv, dattn, sinks, t)

    def dqkv(dq, dk, dv, cc, sa, sb):
        dq = _rope_t(dq, cc, sa, sb) * (HD ** -0.5)
        dk = _rope_t(dk, cc, sa, sb)
        return dq, dk, dv, colsum(dq), colsum(dk), colsum(dv)

    dq, dk, dvb, dbq, dbk, dbv = rowmap(
        "mix_rope_bwd", dqkv, [T_(dqr), T_(dkr), T_(dvv), T_(tabs[0]), T_(tabs[1]), T_(tabs[2])],
        [(QW, BF16), (KVW, BF16), (KVW, BF16)], [(1, QW), (1, KVW), (1, KVW)], rows=t, tm=512)
    dh = jnp.concatenate([dxp, dq, dk, dvb, dgla, dglb], axis=1)
    db_in = jnp.concatenate([dbxp, dbq, dbk, dbv, dbga, dbgb], axis=1)
    dwin = mm("mix_dwin", [u], [dh], [(0, 0, 0)], dims=TN, grid=(2, 4, t // tk), a_specs=[half],
              b_specs=[pl.BlockSpec((tk, IN_SH), lambda i, j, k: (k, j))],
              outs=[sds((4, D, IN_SH), BF16)], out_specs=[pl.BlockSpec((None, D // 2, IN_SH), lambda i, j, k: (j, i, 0))],
              acc_shapes=[(D // 2, IN_SH)])
    dwp_sh = jnp.transpose(dwp.reshape(4, 4, 64, PG), (1, 0, 2, 3)).reshape(4, 4 * 64, PG)
    parts = {"win": dwin, "wp": dwp_sh, "wba": dwba, "wbb": dwbb, "wo": dwo.reshape(4, D // 4, D)}
    du, sib = mm("mix_du", [dh], [w_in], [(0, 0, 0)], dims=NT, grid=(t // tm, 2, 4),
                 a_specs=[pl.BlockSpec((tm, IN_SH), lambda i, j, k: (i, k))],
                 b_specs=[pl.BlockSpec((None, D // 2, IN_SH), lambda i, j, k: (k, j, 0))],
                 outs=[sds((t, D), F32)], out_specs=[pl.BlockSpec((tm, D // 2), lambda i, j, k: (i, j))],
                 acc_shapes=[(tm, D // 2)],
                 job=reduce_sibling_job([(p, view_lead, p.shape[1], p.shape[2]) for p in parts.values()]))
    return du, parts, dict(zip(parts, sib)), db_in, dps, dsinks


def cast_shard(name, w, sp, ffn_out=False):
    rows, cols = w.shape
    if ffn_out:
        tm = rows // 2
        shape = (2, FHP, D)
        spec = pl.BlockSpec((None, tm, cols), lambda j, i, s: (s[0] // 2, (s[0] % 2) * 2 + i, 0))
    else:
        tm = rows // 4
        shape = (4, rows, cols)
        spec = pl.BlockSpec((None, tm, cols), lambda j, i, s: (s[0], i, 0))
    return rowmap(name, lambda wv: wv, [T_(w)], [(shape, BF16, spec)], rows=rows, tm=tm, sp=sp)


def cast_ffn_in(name, wt, sp):
    tm = FH // 4
    buf = rowmap(name, lambda wv: wv, [T_(wt)],
                 [((4, FHP, D), BF16, pl.BlockSpec((None, tm, D), lambda j, i, s: (s[0], i, 0)))], rows=FH, tm=tm, sp=sp)
    pad = FHP - FH

    def zero_pad(_, __, out):
        out[...] = jnp.zeros_like(out)

    return pl.pallas_call(
        zero_pad, name=name + "_pad", out_shape=sds(buf.shape, BF16), input_output_aliases={1: 0},
        grid_spec=pltpu.PrefetchScalarGridSpec(
            num_scalar_prefetch=1, grid=(1,), in_specs=[ANY],
            out_specs=pl.BlockSpec((None, pad, D), lambda i, s: (s[0], FH // pad, 0))))(sp, buf)


def chip_sum(name, dw, got, sp, rows, tm, ffn_out=False):
    hr, cols = rows // 2, got.shape[2]
    per = hr // tm
    pos = pl.BlockSpec((None, tm, cols), lambda j, i, s: (i // per, i % per, 0))
    if ffn_out:
        mine = pl.BlockSpec((None, tm, cols), lambda j, i, s: (i // 2, (i % 2) * 2 + s[1], 0))
    else:
        mine = pl.BlockSpec((None, tm, cols), lambda j, i, s: (i // per, s[1] * per + i % per, 0))
    return rowmap(name, lambda av, bv: av.astype(F32) + bv.astype(F32), [X_(dw, mine), X_(got, pos)],
                  [(got.shape, BF16, pos)], rows=4 * hr, tm=tm, sp=sp)


def chip_total(name, q, got, sp, rows, tm):
    hr, cols = rows // 2, q.shape[2]
    per = hr // tm

    def part(f):
        return X_(got, pl.BlockSpec((None, tm, cols), lambda j, i, s, f=f: (f, i, 0)))

    return rowmap(
        name, lambda av, b0, b1, b2: ((av.astype(F32) + b0.astype(F32)) + b1.astype(F32)) + b2.astype(F32),
        [X_(q, pl.BlockSpec((None, tm, cols), lambda j, i, s: (s[0], i, 0))), part(0), part(1), part(2)],
        [((rows, cols), F32, pl.BlockSpec((tm, cols), lambda j, i, s: (s[1] * per + i, 0)))], rows=hr, tm=tm, sp=sp)


def kernel(x, c, w_ada, b_ada, ln_g, ln_b, w_ffn1_in, w_ffn1_out, w_in, b_in, w_pool, pool_scale, sinks, w_branch_a, w_branch_b, w_out, w_ffn2_in, w_ffn2_out, loss_target, m_w_ada, m_b_ada, m_ln_g, m_ln_b, m_w_ffn1_in, m_w_ffn1_out, m_w_in, m_b_in, m_w_pool, m_pool_scale, m_sinks, m_w_branch_a, m_w_branch_b, m_w_out, m_w_ffn2_in, m_w_ffn2_out, v_w_ada, v_b_ada, v_ln_g, v_ln_b, v_w_ffn1_in, v_w_ffn1_out, v_w_in, v_b_in, v_w_pool, v_pool_scale, v_sinks, v_w_branch_a, v_w_branch_b, v_w_out, v_w_ffn2_in, v_w_ffn2_out):
    t = x.shape[1]
    xs, tgt = x[0], loss_target[0]
    xi, yi, ci = lax.axis_index("x"), lax.axis_index("y"), lax.axis_index("c")
    chip = 2 * xi + yi
    dev = 2 * chip + ci
    b_in2, ps2, sinks2 = b_in, pool_scale, sinks

    sp = jnp.stack([chip, ci]).astype(jnp.int32)
    tr = lambda a: jnp.swapaxes(a[0], 0, 1)

    first = jnp.concatenate([c.reshape(-1), ln_g.reshape(-1), ln_b.reshape(-1)]).reshape(-1, 128)
    first_all = allgather_small("gather_cond", first).reshape(8, -1)
    c_all = first_all[:, :D]
    ln_parts = first_all[0::2, D:].reshape(4, 2, 3, D // 4)
    ln_full = jnp.transpose(ln_parts, (1, 2, 0, 3)).reshape(2, 3, D)
    lgs = [ln_full[0, s:s + 1] for s in range(3)]
    lbs = [ln_full[1, s:s + 1] for s in range(3)]
    c16 = jnp.pad(c_all, ((0, 8), (0, 0)))
    b_ada_sh = lax.dynamic_slice(b_ada, (0, chip * ADA_SH), (1, ADA_SH))
    mod_part = ada_fwd(c16, w_ada[0], b_ada_sh)[:8]
    mod_all = allgather_small("gather_mod", mod_part.reshape(-1, 128)).reshape(8, 8, ADA_SH)
    mod_mine = lax.dynamic_index_in_dim(mod_all[0::2], dev, axis=1, keepdims=False).reshape(9, D)
    mods = [[mod_mine[3 * s + k:3 * s + k + 1] for k in range(3)] for s in range(3)]

    f1i_buf, f1i_send, f1i_recv, _ = gather_start(
        "gather_f1i_start", cast_ffn_in("cast_f1i", tr(w_ffn1_in), sp), FHP, [mod_mine])
    plain = [("f1o", w_ffn1_out[0]), ("win", w_in[0]), ("wp", w_pool[0].reshape(4 * 64, PG)), ("wba", w_branch_a[0]),
             ("wbb", w_branch_b[0]), ("wo", w_out[0]), ("f2o", w_ffn2_out[0])]
    sh = {n: cast_shard("cast_" + n, w, sp, ffn_out=n in ("f1o", "f2o")) for n, w in plain}
    sh["f1i"] = f1i_buf
    sh["f2i"] = cast_ffn_in("cast_f2i", tr(w_ffn2_in), sp)
    order = ["f1i", "f1o", "win", "wp", "wba", "wbb", "wo", "f2i", "f2o"]
    views = {n: (view_ffn_out if n in ("f1o", "f2o") else view_lead) for n in order}
    shard_rows = {n: (FO if n in ("f1o", "f2o") else sh[n].shape[1]) for n in order}
    shard_cols = {n: sh[n].shape[2] for n in order}
    tiles = {"f1i": FHP // 8, "f1o": FO // 2, "win": 512, "wp": 128, "wba": 512, "wbb": 512, "wo": 256,
             "f2i": FHP // 8, "f2o": FO // 2}

    def item(n, part=0, parts=1):
        return (sh[n], views[n], shard_rows[n], part, parts)

    tabs = rope_tables(t)
    (sh0, sc0, gt0), (sh1, sc1, gt1), (sh2, sc2, gt2) = mods

    u0 = modulate("ffn1_mod", xs, sh0, sc0, t)
    landed = gather_wait("gather_f1i_wait", f1i_buf, f1i_send, f1i_recv, FHP,
                         [u0] + [sh[n] for n in order if n != "f1i"])
    (g_f1i,) = run_job("gather_f1i_forward", forward_job(landed, FHP))
    ha1, hb1, g1, y1, f1o, (_, g_win), (g_wp, g_wba, g_wbb, g_wo) = ffn_fwd(
        "ffn1", u0, g_f1i, t, gather_job([item("f1o"), item("win")]),
        gather_job([item(n) for n in ("wp", "wba", "wbb", "wo")]))
    x1, z1, u1 = residual_ln_mod("ffn1_ln", xs, y1, gt0, lgs[0], lbs[0], 0.5, sh1, sc1, t)
    wp_full = jnp.transpose(g_wp.reshape(4, 4, 64, PG), (1, 0, 2, 3)).reshape(4, PG, PG)
    wts = (g_win, wp_full, g_wba, g_wbb, g_wo.reshape(D, D))
    y2, sv2, (g_f2i,) = mix_fwd(
        u1, wts, b_in2, ps2, sinks2, tabs, t, gather_job([item("f2i", 0, 2)]),
        lambda moved: gather_job([(moved[0], view_lead, FHP, 1, 2)]))
    x2, z2, u2 = residual_ln_mod("mix_ln", x1, y2, gt1, lgs[1], lbs[1], 1.0, sh2, sc2, t)
    ha3, hb3, g3, y3, f2o, _, _ = ffn_fwd("ffn2", u2, g_f2i, t, gather_job([item("f2o")]))

    dz3, dy3, dlg2, dlb2, dgt2, sq = residual_ln_loss_bwd("ffn2_ln_loss", x2, y3, tgt, gt2, lgs[2], lbs[2], 0.5, t)
    loss = lax.psum(0.5 * sq[0, 0] / D, ("x", "y", "c"))
    du3, red_f2i, red_f2o, _, _ = ffn_bwd("ffn2", u2, ha3, hb3, g3, dy3, g_f2i, f2o, t, sp)
    dz2, dy2, dlg1, dlb1, dgt1, dsh2, dsc2 = residual_ln_bwd("mix_ln_bwd", z2, (dz3, du3, x2, sc2), y2, gt1, lgs[1], 1.0, t)
    du2, mix_parts, sib, db_in, dps, dsinks = mix_bwd(u1, sv2, dy2, wts, b_in2, ps2, sinks2, tabs, t)
    q = {n: chip_sum("chipsum_" + n, mix_parts[n], sib[n], sp, shard_rows[n], tiles[n]) for n in mix_parts}
    dz1, dy1, dlg0, dlb0, dgt0, dsh1, dsc1 = residual_ln_bwd("ffn1_ln_bwd", z1, (dz2, du2, x1, sc1), y1, gt0, lgs[0], 0.5, t)
    du1, red_f1i, red_f1o, far_a, far_b = ffn_bwd(
        "ffn1", u0, ha1, hb1, g1, dy1, g_f1i, f1o, t, sp,
        reduce_chips_job([q["win"], q["wp"]]), reduce_chips_job([q["wo"], q["wba"], q["wbb"]]))
    dx0, dsh0, dsc0 = modulate_bwd("ffn1_mod_bwd", dz1, du1, xs, sc0, t)
    gm0, gm1, gm2 = (dsh0, dsc0, dgt0), (dsh1, dsc1, dgt1), (dsh2, dsc2, dgt2)
    reduced = {"f1i": red_f1i, "f1o": red_f1o, "f2i": red_f2i, "f2o": red_f2o, "win": (q["win"], far_a[0]),
               "wp": (q["wp"], far_a[1]), "wo": (q["wo"], far_b[0]), "wba": (q["wba"], far_b[1]), "wbb": (q["wbb"], far_b[2])}
    halves = [chip_total("total_" + n, *reduced[n], sp, shard_rows[n], tiles[n]) for n in order]

    small = jnp.concatenate([*gm0, *gm1, *gm2, dlg0, dlg1, dlg2, dlb0, dlb1, dlb2, db_in, dps, dsinks], axis=1)
    n_small = small.shape[1]
    rows_small = -(-n_small // 1024) * 8
    small = jnp.pad(small, ((0, 0), (0, rows_small * 128 - n_small))).reshape(rows_small, 128)
    small_all = allgather_small("gather_small", small)
    tot = sum_devices(small_all).reshape(1, -1)
    gmod_all = small_all.reshape(8, -1)[:, :9 * D]
    o = 9 * D
    g_b_ada = tot[:, :o]
    g_ln_g = lax.dynamic_slice(tot[:, o:o + 3 * D].reshape(3, D), (0, chip * (D // 4)), (3, D // 4))
    g_ln_b = lax.dynamic_slice(tot[:, o + 3 * D:o + 6 * D].reshape(3, D), (0, chip * (D // 4)), (3, D // 4))
    o += 6 * D
    g_b_in, g_ps, g_sinks = tot[:, o:o + IN_W], tot[:, o + IN_W:o + IN_W + PW], tot[:, o + IN_W + PW:o + IN_W + PW + N_Q]

    gm16 = jnp.pad(lax.dynamic_slice(gmod_all, (0, chip * ADA_SH), (8, ADA_SH)), ((0, 8), (0, 0)))
    (g_w_ada, d_w_ada, nm_w_ada, nv_w_ada), _ = ada_bwd_adam(c16, gm16, w_ada[0], m_w_ada[0], v_w_ada[0], None)
    gw = dict(zip(order, run_job("share_halves", share_halves_job(halves))))

    def big(n, w, m, v, tm):
        shape = w.shape
        w2, m2, v2 = (a.reshape(shape[-2] if a.ndim == 3 else -1, shape[-1]) for a in (w, m, v))
        return [r.reshape(shape) for r in adam_rows("adam_" + n, w2, gw[n], m2, v2, tm)]

    def big_t(n, w, m, v):
        return [jnp.swapaxes(r, 0, 1)[None] for r in adam_rows("adam_" + n, tr(w), gw[n], tr(m), tr(v), FH // 8)]

    def tiny(n, w, g, m, v):
        return [g.reshape(w.shape)] + list(adam_small("adam_" + n, w, g.reshape(w.shape), m, v))

    res = {
        "w_ada": [a[None] for a in (g_w_ada, d_w_ada, nm_w_ada, nv_w_ada)],
        "b_ada": tiny("b_ada", b_ada, g_b_ada, m_b_ada, v_b_ada),
        "ln_g": tiny("ln_g", ln_g, g_ln_g, m_ln_g, v_ln_g),
        "ln_b": tiny("ln_b", ln_b, g_ln_b, m_ln_b, v_ln_b),
        "w_ffn1_in": big_t("f1i", w_ffn1_in, m_w_ffn1_in, v_w_ffn1_in),
        "w_ffn1_out": big("f1o", w_ffn1_out, m_w_ffn1_out, v_w_ffn1_out, FO // 4),
        "w_in": big("win", w_in, m_w_in, v_w_in, 256),
        "b_in": tiny("b_in", b_in, g_b_in, m_b_in, v_b_in),
        "w_pool": big("wp", w_pool, m_w_pool, v_w_pool, 256),
        "pool_scale": tiny("pool_scale", pool_scale, g_ps, m_pool_scale, v_pool_scale),
        "sinks": tiny("sinks", sinks, g_sinks, m_sinks, v_sinks),
        "w_branch_a": big("wba", w_branch_a, m_w_branch_a, v_w_branch_a, 512),
        "w_branch_b": big("wbb", w_branch_b, m_w_branch_b, v_w_branch_b, 512),
        "w_out": big("wo", w_out, m_w_out, v_w_out, 128),
        "w_ffn2_in": big_t("f2i", w_ffn2_in, m_w_ffn2_in, v_w_ffn2_in),
        "w_ffn2_out": big("f2o", w_ffn2_out, m_w_ffn2_out, v_w_ffn2_out, FO // 4),
    }
    names = ["w_ada", "b_ada", "ln_g", "ln_b", "w_ffn1_in", "w_ffn1_out", "w_in", "b_in", "w_pool", "pool_scale", "sinks",
             "w_branch_a", "w_branch_b", "w_out", "w_ffn2_in", "w_ffn2_out"]
    return (loss, dx0[None], *[res[n][0] for n in names], *[res[n][1] for n in names],
            *[res[n][2] for n in names], *[res[n][3] for n in names])
```

```python
import jax
import jax.numpy as jnp
from jax import lax
from jax.experimental import pallas as pl
from jax.experimental.pallas import tpu as pltpu

F32 = jnp.float32
BF16 = jnp.bfloat16
MESH = pl.DeviceIdType.MESH
ANY = pl.BlockSpec(memory_space=pl.ANY)

D = 2048
N_Q, N_KV, HD = 16, 4, 64
QW, KVW = N_Q * HD, N_KV * HD
BLK = 128
POOL_WINDOWS = (2, 4, 8, 16)
PW, PG = 1024, 256
HALO = 16
ROPE_THETA = 500000.0
ROT = HD // 4
LN_EPS = 1e-5
ALPHA = 2.0 ** 0.25
FH = 2752
FHP = 2816
FO = 1376
IN_W = 6656
IN_SH = IN_W // 4
ADA_SH = 18432 // 4
B1, B2, LR, EPS, WD, STEP = 0.9, 0.999, 0.001, 1e-08, 0.01, 10
VMEM_LIMIT = 56 * 1024 * 1024
FLIPS = ((1, 0), (0, 1), (1, 1))
NN = (((1,), (0,)), ((), ()))
NT = (((1,), (1,)), ((), ()))
TN = (((0,), (0,)), ((), ()))


def _params(sem):
    return pltpu.CompilerParams(dimension_semantics=sem, vmem_limit_bytes=VMEM_LIMIT)


def _aligned(v, m):
    return v if isinstance(v, int) else pl.multiple_of(v, m)


def _sigmoid(v):
    return 1.0 / (1.0 + jnp.exp(-v))


def T_(arr, width=None, off=0):
    return ("t", arr, width, off)


def B_(arr, width=None, off=0):
    return ("b", arr, width, off)


def X_(arr, spec):
    return ("x", arr, spec, 0)


def rowmap(name, fn, ins, outs, accs=(), *, rows, tm, ncol=1, with_ids=False, sp=None, alias=None):
    tm = min(tm, rows)
    nrow = rows // tm
    in_specs, arrs = [], []
    for kind, arr, width, off in ins:
        if kind == "x":
            in_specs.append(width)
        elif kind == "t":
            w = arr.shape[1] if width is None else width
            in_specs.append(pl.BlockSpec((tm, w), lambda j, i, *_, off=off: (i, off + j)))
        else:
            w = arr.shape[1] if width is None else width
            in_specs.append(pl.BlockSpec((arr.shape[0], w), lambda j, i, *_, off=off: (0, off + j)))
        arrs.append(arr)
    out_shape, out_specs = [], []
    for o in outs:
        if len(o) == 3:
            out_shape.append(jax.ShapeDtypeStruct(o[0], o[1]))
            out_specs.append(o[2])
        else:
            out_shape.append(jax.ShapeDtypeStruct((rows, o[0]), o[1]))
            out_specs.append(pl.BlockSpec((tm, o[0] // ncol), lambda j, i, *_: (i, j)))
    for r, width in accs:
        out_shape.append(jax.ShapeDtypeStruct((r, width), F32))
        out_specs.append(pl.BlockSpec((r, width // ncol), lambda j, i, *_: (0, j)))
    ni, no = len(ins), len(outs)
    nsp = 0 if sp is None else 1

    def body(*refs):
        refs = refs[nsp:]
        i = pl.program_id(1)
        vals = [r[...] for r in refs[:ni]]
        res = fn(pl.program_id(0), i, *vals) if with_ids else fn(*vals)
        if not isinstance(res, (tuple, list)):
            res = (res,)
        for r, v in zip(refs[ni:ni + no], res[:no]):
            r[...] = v.astype(r.dtype)
        for r, v in zip(refs[ni + no:], res[no:]):
            @pl.when(i == 0)
            def _(r=r, v=v):
                r[...] = v

            @pl.when(i > 0)
            def _(r=r, v=v):
                r[...] += v

    grid_spec = pltpu.PrefetchScalarGridSpec(num_scalar_prefetch=nsp, grid=(ncol, nrow), in_specs=in_specs,
                                             out_specs=out_specs)
    res = pl.pallas_call(
        body, name=name, grid_spec=grid_spec, out_shape=out_shape,
        input_output_aliases={nsp + k: v for k, v in (alias or {}).items()},
        compiler_params=_params(("arbitrary", "arbitrary")),
    )(*([sp] if nsp else []), *arrs)
    return res[0] if len(res) == 1 else res


def colsum(v):
    return jnp.sum(v, axis=0, keepdims=True)


def mm(name, a_ops, b_ops, ops, *, dims, grid, a_specs, b_specs, outs, out_specs, acc_shapes,
       epilogue=None, extras=(), extra_specs=(), carry=None, job=None, sub_rows=None):
    gk = grid[2]
    na, nb, ne, nacc = len(a_ops), len(b_ops), len(extras), len(acc_shapes)
    nc = 0 if carry is None else 1
    no = len(outs)

    def body(*refs):
        a_refs = refs[:na]
        b_refs = refs[na:na + nb]
        e_refs = refs[na + nb:na + nb + ne]
        o_refs = refs[na + nb + ne + nc:na + nb + ne + nc + no]
        acc_refs = refs[na + nb + ne + nc + no:]
        k = pl.program_id(2)

        def partials(rows=slice(None)):
            res = [None] * nacc
            for ai, bi, ci in ops:
                p = lax.dot_general(a_refs[ai][rows], b_refs[bi][...], dims, preferred_element_type=F32)
                res[ci] = p if res[ci] is None else res[ci] + p
            return res

        def finish(accs, rows=slice(None)):
            outv = epilogue(accs, [e[rows] for e in e_refs]) if epilogue else (accs[0],)
            for o, v in zip(o_refs, outv):
                o[rows] = v.astype(o.dtype)

        if gk == 1 and sub_rows:
            for s in range(out_specs[0].block_shape[-2] // sub_rows):
                rows = pl.ds(s * sub_rows, sub_rows)
                finish(partials(rows), rows)
        elif gk == 1:
            finish(partials())
        else:
            ps = partials()

            @pl.when(k == 0)
            def _():
                for acc, p in zip(acc_refs, ps):
                    acc[...] = p

            @pl.when((k > 0) & (k < gk - 1))
            def _():
                for acc, p in zip(acc_refs, ps):
                    acc[...] += p

            @pl.when(k == gk - 1)
            def _():
                finish([acc[...] + p for acc, p in zip(acc_refs, ps)])

    res, moved = carried_call(
        body, name, grid,
        list(a_specs) + list(b_specs) + list(extra_specs) + ([ANY] if nc else []), list(out_specs), list(outs),
        [pltpu.VMEM(s, F32) for s in acc_shapes] if gk > 1 else [],
        [*a_ops, *b_ops, *extras, *([carry] if nc else [])], {na + nb + ne: 0} if nc else {}, job)
    res = res[0] if len(res) == 1 else res
    return res if job is None else (res, moved)


def sds(shape, dt):
    return jax.ShapeDtypeStruct(shape, dt)


class Job:
    def __init__(self, ins, outs, aliases, scratch, start, mid, finish):
        self.ins, self.outs, self.aliases, self.scratch = list(ins), list(outs), dict(aliases), list(scratch)
        self.start, self.mid, self.finish = start, mid, finish


def carried_call(body, name, grid, in_specs, out_specs, out_shape, scratch, args, aliases, job, mid_at=0.9):
    sem = ("arbitrary",) * len(grid)
    if job is None:
        res = pl.pallas_call(body, name=name, grid=grid, in_specs=in_specs, out_specs=out_specs, out_shape=out_shape,
                             scratch_shapes=scratch, input_output_aliases=aliases, compiler_params=_params(sem))(*args)
        return list(res), []
    ni, no, ns = len(in_specs), len(out_specs), len(scratch)
    ci, co = len(job.ins), len(job.outs)
    total = 1
    for g in grid:
        total *= g
    mid_step = min(max(int(total * mid_at), 1), total - 1)

    def full(*refs):
        ins, cins = refs[:ni], refs[ni:ni + ci]
        outs, couts = refs[ni + ci:ni + ci + no], refs[ni + ci + no:ni + ci + no + co]
        scr, cscr = refs[ni + ci + no + co:ni + ci + no + co + ns], refs[ni + ci + no + co + ns:]
        step = 0
        for d, g in enumerate(grid):
            step = step * g + pl.program_id(d)

        @pl.when(step == 0)
        def _():
            job.start(cins, couts, cscr)

        body(*ins, *outs, *scr)

        @pl.when(step == mid_step)
        def _():
            job.mid(cins, couts, cscr)

        @pl.when(step == total - 1)
        def _():
            job.finish(cins, couts, cscr)

    al = dict(aliases)
    al.update({ni + k: no + v for k, v in job.aliases.items()})
    res = pl.pallas_call(
        full, name=name, grid=grid, in_specs=in_specs + [ANY] * ci, out_specs=out_specs + [ANY] * co,
        out_shape=out_shape + job.outs, scratch_shapes=scratch + job.scratch, input_output_aliases=al,
        compiler_params=_params(sem))(*args, *job.ins)
    return list(res[:no]), list(res[no:])


def _with_moved(res, job):
    return res if job is not None else (res, [])


def run_job(name, job):
    ci = len(job.ins)

    def body(*refs):
        cins, couts, cscr = refs[:ci], refs[ci:ci + len(job.outs)], refs[ci + len(job.outs):]
        job.start(cins, couts, cscr)
        job.mid(cins, couts, cscr)
        job.finish(cins, couts, cscr)

    return list(pl.pallas_call(
        body, name=name, in_specs=[ANY] * ci, out_specs=[ANY] * len(job.outs), out_shape=job.outs,
        scratch_shapes=job.scratch, input_output_aliases=job.aliases)(*job.ins))


def _place():
    x, y, c = lax.axis_index("x"), lax.axis_index("y"), lax.axis_index("c")
    chips = [((1 - x) if fx else x, (1 - y) if fy else y) for fx, fy in FLIPS]
    return x, y, c, chips


def allgather_small(name, v):
    r = v.shape[0]

    def body(x_ref, out_ref, send_sems, recv_sems, local_sem):
        x, y, c, chips = _place()
        me, sibling = (x, y, c), (x, y, 1 - c)

        def rows(px, py, pc):
            return out_ref.at[4 * px + 2 * py + pc]

        def copy(k, block, to, src=None):
            return pltpu.make_async_remote_copy(
                src_ref=rows(*block) if src is None else src, dst_ref=rows(*block),
                send_sem=send_sems.at[k], recv_sem=recv_sems.at[k], device_id=to, device_id_type=MESH)

        mine = pltpu.make_async_copy(x_ref, rows(*me), local_sem)
        mine.start()
        first = [copy(0, me, sibling, src=x_ref)]
        first += [copy(1 + j, me, (*chip, c), src=x_ref) for j, chip in enumerate(chips)]
        for cp in first:
            cp.start()
        passed = [copy(4 + j, (*chip, c), sibling) for j, chip in enumerate(chips)]
        for j, chip in enumerate(chips):
            copy(1 + j, (*chip, c), me).wait_recv()
            passed[j].start()
        copy(0, sibling, me).wait_recv()
        for j, chip in enumerate(chips):
            copy(4 + j, (*chip, 1 - c), me).wait_recv()
        for cp in first + passed:
            cp.wait_send()
        mine.wait()

    return pl.pallas_call(
        body, name=name, out_shape=sds((8, r, 128), v.dtype),
        in_specs=[pl.BlockSpec(memory_space=pltpu.VMEM)], out_specs=pl.BlockSpec(memory_space=pltpu.VMEM),
        scratch_shapes=[pltpu.SemaphoreType.DMA((7,)), pltpu.SemaphoreType.DMA((7,)), pltpu.SemaphoreType.DMA],
    )(v)


def _half(ref, rows, hf):
    hr = rows // 2
    return ref.at[pl.ds(_aligned(hf * hr, 16), hr)]


def view_lead(ref, p):
    return ref.at[p]


def view_ffn_out(ref, p):
    return ref.at[p // 2, pl.ds(_aligned((p % 2) * FO, 16), FO)]


def _remote(ref, dst, send_sems, recv_sems, idx, to):
    return pltpu.make_async_remote_copy(src_ref=ref, dst_ref=dst, send_sem=send_sems.at[idx], recv_sem=recv_sems.at[idx],
                                        device_id=to, device_id_type=MESH)


def gather_job(items):
    nw = len(items)
    pads = [w for w, it in enumerate(items) if it[1] is view_ffn_out]

    def piece(ref, w, p, hf):
        _, view, rws, part, parts = items[w]
        pr = rws // 2 // parts
        return view(ref, p).at[pl.ds(_aligned(hf * (rws // 2) + part * pr, 16), pr)]

    def pad_copies(outs, scr):
        return [pltpu.make_async_copy(scr[2], outs[w].at[h, pl.ds(2 * FO, FHP - 2 * FO)], scr[3].at[2 * n + h])
                for n, w in enumerate(pads) for h in range(2)]

    def start(_, outs, scr):
        x, y, c, chips = _place()
        if pads:
            scr[2][...] = jnp.zeros_like(scr[2])
            for cp in pad_copies(outs, scr):
                cp.start()
        for w in range(nw):
            mine = piece(outs[w], w, 2 * x + y, c)
            for f, (px, py) in enumerate(chips):
                _remote(mine, mine, scr[0], scr[1], (w, f), (px, py, c)).start()

    def mid(_, outs, scr):
        x, y, c, chips = _place()
        for w in range(nw):
            for f, (px, py) in enumerate(chips):
                land = piece(outs[w], w, 2 * px + py, c)
                _remote(land, land, scr[0], scr[1], (w, f), (px, py, c)).wait_recv()
                _remote(land, land, scr[0], scr[1], (w, 3 + f), (x, y, 1 - c)).start()

    def finish(_, outs, scr):
        x, y, c, chips = _place()
        for w in range(nw):
            for f, (px, py) in enumerate(chips):
                land = piece(outs[w], w, 2 * px + py, 1 - c)
                _remote(land, land, scr[0], scr[1], (w, 3 + f), (x, y, 1 - c)).wait_recv()
        for w in range(nw):
            mine = piece(outs[w], w, 2 * x + y, c)
            for f in range(6):
                _remote(mine, mine, scr[0], scr[1], (w, f), (x, y, 1 - c)).wait_send()
        for cp in pad_copies(outs, scr):
            cp.wait()

    scratch = [pltpu.SemaphoreType.DMA((nw, 6)), pltpu.SemaphoreType.DMA((nw, 6))]
    if pads:
        scratch += [pltpu.VMEM((FHP - 2 * FO, D), BF16), pltpu.SemaphoreType.DMA((2 * len(pads),))]
    bufs = [it[0] for it in items]
    return Job(bufs, [sds(b.shape, BF16) for b in bufs], {w: w for w in range(nw)}, scratch, start, mid, finish)


HBM = pl.BlockSpec(memory_space=pltpu.HBM)
SEM = pl.BlockSpec(memory_space=pltpu.SEMAPHORE)
SPLIT = pltpu.CompilerParams(has_side_effects=pltpu.SideEffectType.DATAFLOW_SIDE_EFFECTING)


def gather_start(name, buf, rows, after):
    def body(*refs):
        out, send_sems, recv_sems, token = refs[1 + len(after):]
        x, y, c, chips = _place()
        mine = _half(out.at[2 * x + y], rows, c)
        for f, (px, py) in enumerate(chips):
            _remote(mine, mine, send_sems, recv_sems, f, (px, py, c)).start()
        token[...] = jnp.zeros_like(token)

    return pl.pallas_call(
        body, name=name,
        out_shape=(pltpu.HBM(buf.shape, buf.dtype), pltpu.SemaphoreType.DMA((3,)), pltpu.SemaphoreType.DMA((3,)),
                   sds((8, 128), F32)),
        in_specs=(HBM,) + (ANY,) * len(after), out_specs=(HBM, SEM, SEM, pl.BlockSpec(memory_space=pltpu.VMEM)),
        input_output_aliases={0: 0}, compiler_params=SPLIT)(pltpu.with_memory_space_constraint(buf, pltpu.HBM), *after)


def gather_wait(name, buf, send_sems, recv_sems, rows, after):
    def body(_, send_sems, recv_sems, *rest):
        out = rest[-1]
        x, y, c, chips = _place()
        mine = _half(out.at[2 * x + y], rows, c)
        for f, (px, py) in enumerate(chips):
            cp = _remote(mine, _half(out.at[2 * px + py], rows, c), send_sems, recv_sems, f, (px, py, c))
            cp.wait_send()
            cp.wait_recv()

    return pl.pallas_call(
        body, name=name, out_shape=pltpu.HBM(buf.shape, buf.dtype),
        in_specs=(HBM, SEM, SEM) + (ANY,) * len(after), out_specs=HBM, input_output_aliases={0: 0},
        compiler_params=SPLIT)(buf, send_sems, recv_sems, *after)


def forward_job(buf, rows):
    def copies(outs, scr, hf):
        x, y, c, chips = _place()
        half = c if hf == 0 else 1 - c
        return [_remote(_half(outs[0].at[2 * px + py], rows, half), _half(outs[0].at[2 * px + py], rows, half),
                        scr[0], scr[1], f, (x, y, 1 - c)) for f, (px, py) in enumerate(chips)]

    def start(_, outs, scr):
        for cp in copies(outs, scr, 0):
            cp.start()

    def finish(_, outs, scr):
        for cp in copies(outs, scr, 1):
            cp.wait_recv()
        for cp in copies(outs, scr, 0):
            cp.wait_send()

    return Job([buf], [sds(buf.shape, buf.dtype)], {0: 0},
               [pltpu.SemaphoreType.DMA((3,)), pltpu.SemaphoreType.DMA((3,))], start, lambda *_: None, finish)


def reduce_sibling_job(items):
    nw = len(items)

    def copies(ins, got, scr):
        x, y, c, _ = _place()
        return [_remote(_half(view(ins[w], p), rws, 1 - c), got[w].at[p], scr[0], scr[1], (w, p), (x, y, 1 - c))
                for w, (_, view, rws, _) in enumerate(items) for p in range(4)]

    def start(ins, got, scr):
        for cp in copies(ins, got, scr):
            cp.start()

    def finish(ins, got, scr):
        for cp in copies(ins, got, scr):
            cp.wait()

    return Job([it[0] for it in items], [sds((4, it[2] // 2, it[3]), BF16) for it in items], {},
               [pltpu.SemaphoreType.DMA((nw, 4)), pltpu.SemaphoreType.DMA((nw, 4))], start, lambda *_: None, finish)


def reduce_chips_job(qs):
    nw = len(qs)

    def copies(ins, got, scr):
        x, y, c, chips = _place()
        return [_remote(ins[w].at[2 * px + py], got[w].at[f], scr[0], scr[1], (w, f), (px, py, c))
                for w in range(nw) for f, (px, py) in enumerate(chips)]

    def start(ins, got, scr):
        for cp in copies(ins, got, scr):
            cp.start()

    def finish(ins, got, scr):
        for cp in copies(ins, got, scr):
            cp.wait()

    return Job(qs, [sds((3,) + q.shape[1:], BF16) for q in qs], {},
               [pltpu.SemaphoreType.DMA((nw, 3)), pltpu.SemaphoreType.DMA((nw, 3))], start, lambda *_: None, finish)


def share_halves_job(gs):
    nw = len(gs)

    def start(_, outs, scr):
        x, y, c, _ = _place()
        for w in range(nw):
            mine = _half(outs[w], gs[w].shape[0], c)
            _remote(mine, mine, scr[0], scr[1], w, (x, y, 1 - c)).start()

    def finish(_, outs, scr):
        x, y, c, _ = _place()
        for w in range(nw):
            mine = _half(outs[w], gs[w].shape[0], c)
            theirs = _half(outs[w], gs[w].shape[0], 1 - c)
            _remote(mine, mine, scr[0], scr[1], w, (x, y, 1 - c)).wait_send()
            _remote(theirs, theirs, scr[0], scr[1], w, (x, y, 1 - c)).wait_recv()

    return Job(gs, [sds(g.shape, F32) for g in gs], {w: w for w in range(nw)},
               [pltpu.SemaphoreType.DMA((nw,)), pltpu.SemaphoreType.DMA((nw,))], start, lambda *_: None, finish)


def rope_tables(t):
    pos = jnp.arange(t, dtype=F32)
    inv_freq = ROPE_THETA ** (-jnp.arange(0, ROT, 2, dtype=F32) / ROT)
    ang = pos[:, None] * inv_freq[None, :]
    cos, sin = jnp.cos(ang), jnp.sin(ang)
    d = jnp.arange(128) % HD
    half = ROT // 2
    cs = jnp.take(cos, d % half, axis=1)
    sn = jnp.take(sin, d % half, axis=1)
    cc = jnp.where(d[None] < ROT, cs, 1.0)
    sa = jnp.where(d[None] < half, -sn, 0.0)
    sb = jnp.where((d[None] >= half) & (d[None] < ROT), sn, 0.0)
    return cc, sa, sb


def _rope(v, cc, sa, sb):
    w = v.shape[1]
    reps = w // 128
    half = ROT // 2
    return (v * jnp.tile(cc, (1, reps)) + pltpu.roll(v, w - half, 1) * jnp.tile(sa, (1, reps))
            + pltpu.roll(v, half, 1) * jnp.tile(sb, (1, reps)))


def _rope_t(dv, cc, sa, sb):
    w = dv.shape[1]
    reps = w // 128
    half = ROT // 2
    return (dv * jnp.tile(cc, (1, reps)) + pltpu.roll(dv * jnp.tile(sa, (1, reps)), half, 1)
            + pltpu.roll(dv * jnp.tile(sb, (1, reps)), w - half, 1))


def pool_fwd(h, b_in, t, tm):
    tm = min(tm, t)
    per = tm // HALO

    def body(prev_ref, cur_ref, b_ref, o_ref, xx):
        i = pl.program_id(0)
        b = b_ref[...]
        xx[pl.ds(0, HALO), :] = jnp.where(i > 0, prev_ref[...] + b, 0.0)
        xx[pl.ds(HALO, tm), :] = cur_ref[...] + b
        tpos = i * tm + lax.broadcasted_iota(jnp.int32, (tm, PG), 0) + 1
        for gi, w in enumerate(POOL_WINDOWS):
            cols = pl.ds(gi * PG, PG)
            acc = xx[pl.ds(HALO, tm), cols]
            for s in range(1, w):
                acc = acc + xx[pl.ds(HALO - s, tm), cols]
            cnt = jnp.minimum(tpos, w).astype(F32)
            o_ref[:, cols] = (acc / cnt - xx[pl.ds(HALO, tm), cols]).astype(o_ref.dtype)

    return pl.pallas_call(
        body, name="pool_fwd", grid=(t // tm,),
        in_specs=[pl.BlockSpec((HALO, PW), lambda i: (jnp.maximum(i * per - 1, 0), 0)),
                  pl.BlockSpec((tm, PW), lambda i: (i, 0)), pl.BlockSpec((1, PW), lambda i: (0, 0))],
        out_specs=pl.BlockSpec((tm, PW), lambda i: (i, 0)), out_shape=sds((t, PW), BF16),
        scratch_shapes=[pltpu.VMEM((tm + HALO, PW), F32)], compiler_params=_params(("arbitrary",)),
    )(h, h, b_in)


def pool_bwd(dpooled, t, tm):
    tm = min(tm, t)
    per = tm // HALO
    nt = t // tm

    def body(cur_ref, nxt_ref, o_ref, db_ref, ee):
        i = pl.program_id(0)
        tpos = i * tm + lax.broadcasted_iota(jnp.int32, (tm, PG), 0) + 1
        for gi, w in enumerate(POOL_WINDOWS):
            cols = pl.ds(gi * PG, PG)
            ee[pl.ds(0, tm), cols] = cur_ref[:, cols] / jnp.minimum(tpos, w).astype(F32)
            ee[pl.ds(tm, HALO), cols] = jnp.where(i < nt - 1, nxt_ref[:, cols] / float(w), 0.0)
        for gi, w in enumerate(POOL_WINDOWS):
            cols = pl.ds(gi * PG, PG)
            acc = ee[pl.ds(0, tm), cols]
            for s in range(1, w):
                acc = acc + ee[pl.ds(s, tm), cols]
            dxp = acc - cur_ref[:, cols]
            o_ref[:, cols] = dxp.astype(o_ref.dtype)
            part = colsum(dxp)

            @pl.when(i == 0)
            def _(cols=cols, part=part):
                db_ref[:, cols] = part

            @pl.when(i > 0)
            def _(cols=cols, part=part):
                db_ref[:, cols] += part

    return pl.pallas_call(
        body, name="pool_bwd", grid=(nt,),
        in_specs=[pl.BlockSpec((tm, PW), lambda i: (i, 0)),
                  pl.BlockSpec((HALO, PW), lambda i: (jnp.minimum((i + 1) * per, t // HALO - 1), 0))],
        out_specs=[pl.BlockSpec((tm, PW), lambda i: (i, 0)), pl.BlockSpec((1, PW), lambda i: (0, 0))],
        out_shape=[sds((t, PW), BF16), sds((1, PW), F32)],
        scratch_shapes=[pltpu.VMEM((tm + HALO, PW), F32)], compiler_params=_params(("arbitrary",)),
    )(dpooled, dpooled)


def _scores(qh, kp, kc, mask_p, mask_c, sink):
    sp = jnp.where(mask_p, lax.dot_general(qh, kp, NT, preferred_element_type=F32), -1e30)
    sc = jnp.where(mask_c, lax.dot_general(qh, kc, NT, preferred_element_type=F32), -1e30)
    m = jnp.maximum(jnp.maximum(jnp.max(sp, axis=-1, keepdims=True), jnp.max(sc, axis=-1, keepdims=True)), sink)
    pp, pc = jnp.exp(sp - m), jnp.exp(sc - m)
    es = jnp.exp(sink - m)
    inv = 1.0 / (jnp.sum(pp, axis=-1, keepdims=True) + jnp.sum(pc, axis=-1, keepdims=True) + es)
    return pp * inv, pc * inv, es * inv


GRP = N_Q // N_KV


def _masks(n):
    qi = lax.broadcasted_iota(jnp.int32, (GRP * BLK, BLK), 0) % BLK
    kj = lax.broadcasted_iota(jnp.int32, (GRP * BLK, BLK), 1)
    return (kj > qi) & (n > 0), kj <= qi


def _head(hk, g):
    return pl.ds(HD * (GRP * hk + g), HD)


def _stack_heads(ref, hk):
    return jnp.concatenate([ref[:, _head(hk, g)] for g in range(GRP)], axis=0)


def _stack_sinks(s_ref, hk):
    return jnp.concatenate([jnp.full((BLK, 1), s_ref[0, GRP * hk + g], F32) for g in range(GRP)], axis=0)


def attn_fwd(q, k, v, sinks, t, job=None):
    def body(s_ref, q_ref, kp_ref, kc_ref, vp_ref, vc_ref, o_ref):
        n = pl.program_id(0)
        mask_p, mask_c = _masks(n)
        for hk in range(N_KV):
            kv = pl.ds(HD * hk, HD)
            pp, pc, _ = _scores(_stack_heads(q_ref, hk), kp_ref[:, kv], kc_ref[:, kv], mask_p, mask_c,
                                _stack_sinks(s_ref, hk))
            o = (lax.dot_general(pp.astype(BF16), vp_ref[:, kv], NN, preferred_element_type=F32)
                 + lax.dot_general(pc.astype(BF16), vc_ref[:, kv], NN, preferred_element_type=F32))
            for g in range(GRP):
                o_ref[:, _head(hk, g)] = o[g * BLK:(g + 1) * BLK].astype(o_ref.dtype)

    prev = lambda n: (jnp.maximum(n - 1, 0), 0)
    cur = lambda n: (n, 0)
    res, moved = carried_call(
        body, "attn_fwd", (t // BLK,),
        [pl.BlockSpec(memory_space=pltpu.SMEM), pl.BlockSpec((BLK, QW), cur),
         pl.BlockSpec((BLK, KVW), prev), pl.BlockSpec((BLK, KVW), cur),
         pl.BlockSpec((BLK, KVW), prev), pl.BlockSpec((BLK, KVW), cur)],
        [pl.BlockSpec((BLK, QW), cur)], [sds((t, QW), BF16)], [], [sinks, q, k, k, v, v], {}, job)
    return res[0], moved


def attn_bwd(q, k, v, do, sinks, t):
    nb = t // BLK

    def body(s_ref, q_ref, do_ref, kp_ref, kc_ref, vp_ref, vc_ref, dq_ref, dk_ref, dv_ref, ds_ref, dkc, dvc):
        n = pl.program_id(0)

        @pl.when(n == 0)
        def _():
            dkc[...] = jnp.zeros_like(dkc)
            dvc[...] = jnp.zeros_like(dvc)
            ds_ref[...] = jnp.zeros_like(ds_ref)

        @pl.when(n < nb)
        def _():
            mask_p, mask_c = _masks(n)
            lane = lax.broadcasted_iota(jnp.int32, (1, 128), 1)
            dsink = jnp.zeros((1, 128), F32)
            for hk in range(N_KV):
                kv = pl.ds(HD * hk, HD)
                kp, kc, vp, vc = kp_ref[:, kv], kc_ref[:, kv], vp_ref[:, kv], vc_ref[:, kv]
                qs, dos = _stack_heads(q_ref, hk), _stack_heads(do_ref, hk)
                pp, pc, ps = _scores(qs, kp, kc, mask_p, mask_c, _stack_sinks(s_ref, hk))
                dpp = lax.dot_general(dos, vp, NT, preferred_element_type=F32)
                dpc = lax.dot_general(dos, vc, NT, preferred_element_type=F32)
                delta = jnp.sum(pp * dpp, axis=-1, keepdims=True) + jnp.sum(pc * dpc, axis=-1, keepdims=True)
                dsp = (pp * (dpp - delta)).astype(BF16)
                dsc = (pc * (dpc - delta)).astype(BF16)
                sd = ps * delta
                dq = (lax.dot_general(dsp, kp, NN, preferred_element_type=F32)
                      + lax.dot_general(dsc, kc, NN, preferred_element_type=F32))
                for g in range(GRP):
                    rows = slice(g * BLK, (g + 1) * BLK)
                    dsink = dsink + jnp.where(lane == GRP * hk + g, -jnp.sum(sd[rows]), 0.0)
                    dq_ref[:, _head(hk, g)] = dq[rows]
                dk_ref[:, kv] = dkc[:, kv] + lax.dot_general(dsp, qs, TN, preferred_element_type=F32)
                dv_ref[:, kv] = dvc[:, kv] + lax.dot_general(pp.astype(BF16), dos, TN, preferred_element_type=F32)
                dkc[:, kv] = lax.dot_general(dsc, qs, TN, preferred_element_type=F32)
                dvc[:, kv] = lax.dot_general(pc.astype(BF16), dos, TN, preferred_element_type=F32)
            ds_ref[...] += dsink

        @pl.when(n == nb)
        def _():
            dk_ref[...] = dkc[...]
            dv_ref[...] = dvc[...]

    cur = lambda n: (jnp.minimum(n, nb - 1), 0)
    prev = lambda n: (jnp.clip(n - 1, 0, nb - 1), 0)
    return pl.pallas_call(
        body, name="attn_bwd", grid=(nb + 1,),
        in_specs=[pl.BlockSpec(memory_space=pltpu.SMEM), pl.BlockSpec((BLK, QW), cur), pl.BlockSpec((BLK, QW), cur),
                  pl.BlockSpec((BLK, KVW), prev), pl.BlockSpec((BLK, KVW), cur),
                  pl.BlockSpec((BLK, KVW), prev), pl.BlockSpec((BLK, KVW), cur)],
        out_specs=[pl.BlockSpec((BLK, QW), cur), pl.BlockSpec((BLK, KVW), prev), pl.BlockSpec((BLK, KVW), prev),
                   pl.BlockSpec((1, 128), lambda n: (0, 0))],
        out_shape=[sds((t, QW), F32), sds((t, KVW), F32), sds((t, KVW), F32), sds((1, 128), F32)],
        scratch_shapes=[pltpu.VMEM((BLK, KVW), F32), pltpu.VMEM((BLK, KVW), F32)],
        compiler_params=_params(("arbitrary",)),
    )(sinks, q, do, k, k, v, v)


def _adamw(w, g, m, v):
    m2 = B1 * m + (1.0 - B1) * g
    v2 = B2 * v + (1.0 - B2) * jnp.square(g)
    m_hat = m2 / (1.0 - B1 ** STEP)
    v_hat = v2 / (1.0 - B2 ** STEP)
    return -LR * (m_hat / (jnp.sqrt(v_hat) + EPS) + WD * w), m2, v2


def ada_fwd(c16, w_ada, b_sh):
    tn = 512

    def body(c_ref, w_ref, b_ref, o_ref):
        cv = c_ref[...]
        sc = (cv * _sigmoid(cv)).astype(BF16)
        o_ref[...] = lax.dot_general(sc, w_ref[...].astype(BF16), NN, preferred_element_type=F32) + b_ref[...]

    return pl.pallas_call(
        body, name="ada_fwd", grid=(ADA_SH // tn,),
        in_specs=[pl.BlockSpec((16, D), lambda j: (0, 0)), pl.BlockSpec((D, tn), lambda j: (0, j)),
                  pl.BlockSpec((1, tn), lambda j: (0, j))],
        out_specs=pl.BlockSpec((16, tn), lambda j: (0, j)), out_shape=sds((16, ADA_SH), F32),
        compiler_params=_params(("arbitrary",)),
    )(c16, w_ada, b_sh)


def ada_bwd_adam(c16, gm16, w, m, v, job):
    tm, tn = 512, ADA_SH // 4

    def body(c_ref, g_ref, w_ref, m_ref, v_ref, go_ref, d_ref, mo_ref, vo_ref):
        cv = c_ref[...]
        sc = (cv * _sigmoid(cv)).astype(BF16)
        g = lax.dot_general(sc, g_ref[...].astype(BF16), TN, preferred_element_type=F32)
        dl, m2, v2 = _adamw(w_ref[...], g, m_ref[...], v_ref[...])
        go_ref[...] = g
        d_ref[...] = dl
        mo_ref[...] = m2
        vo_ref[...] = v2

    blk = pl.BlockSpec((tm, tn), lambda i, j: (i, j))
    return carried_call(
        body, "ada_bwd_adam", (D // tm, ADA_SH // tn),
        [pl.BlockSpec((16, tm), lambda i, j: (0, i)), pl.BlockSpec((16, tn), lambda i, j: (0, j)), blk, blk, blk],
        [blk] * 4, [sds((D, ADA_SH), F32)] * 4, [], [c16, gm16, w, m, v], {}, job)


def adam_rows(name, w, g, m, v, tm):
    rows, cols = w.shape

    def fn(wv, gv, mv, vv):
        gv = gv[:, :cols]
        dl, m2, v2 = _adamw(wv, gv, mv, vv)
        return gv, dl, m2, v2

    return rowmap(name, fn, [T_(w), T_(g), T_(m), T_(v)], [(cols, F32)] * 4, rows=rows, tm=tm)


def adam_small(name, w, g, m, v):
    def body(w_ref, g_ref, m_ref, v_ref, d_ref, mo_ref, vo_ref):
        dl, m2, v2 = _adamw(w_ref[...], g_ref[...], m_ref[...], v_ref[...])
        d_ref[...] = dl
        mo_ref[...] = m2
        vo_ref[...] = v2

    return pl.pallas_call(body, name=name, out_shape=[sds(w.shape, F32)] * 3)(w, g, m, v)


def sum_devices(allv):
    def body(a_ref, o_ref):
        acc = a_ref[0]
        for d in range(1, 8):
            acc = acc + a_ref[d]
        o_ref[...] = acc

    return pl.pallas_call(body, name="sum_devices", out_shape=sds(allv.shape[1:], F32))(allv)


def _ln_fwd(z, g, b):
    mu = jnp.mean(z, axis=-1, keepdims=True)
    zc = z - mu
    var = jnp.mean(jnp.square(zc), axis=-1, keepdims=True)
    return zc * lax.rsqrt(var + LN_EPS) * g + b


def _ln_bwd(z, g, dout):
    mu = jnp.mean(z, axis=-1, keepdims=True)
    zc = z - mu
    var = jnp.mean(jnp.square(zc), axis=-1, keepdims=True)
    rstd = lax.rsqrt(var + LN_EPS)
    xh = zc * rstd
    dxh = dout * g
    dz = rstd * (dxh - jnp.mean(dxh, axis=-1, keepdims=True) - xh * jnp.mean(dxh * xh, axis=-1, keepdims=True))
    return dz, colsum(dout * xh), colsum(dout)


def modulate(name, xin, shift, scale, t):
    return rowmap(name, lambda xv, sh, sc: xv * (1.0 + sc) + sh, [T_(xin), B_(shift), B_(scale)], [(D, BF16)],
                  rows=t, tm=512)


def residual_ln_mod(name, xin, y, gate, lg, lb, wgt, shift_n, scale_n, t):
    def fn(xv, yv, gt, g, b, sh, sc):
        z = ALPHA * xv + (wgt * (1.0 + gt)) * yv
        xo = _ln_fwd(z, g, b)
        return xo, z, xo * (1.0 + sc) + sh

    return rowmap(name, fn, [T_(xin), T_(y), B_(gate), B_(lg), B_(lb), B_(shift_n), B_(scale_n)],
                  [(D, F32), (D, F32), (D, BF16)], rows=t, tm=256)


def residual_ln_bwd(name, z, dnext, y, gate, lg, wgt, t):
    dzn, dun, xn, scn = dnext

    def fn(zv, yv, gt, g, dzv, duv, xv, sc):
        dv = ALPHA * dzv + duv * (1.0 + sc)
        dz, dg, db = _ln_bwd(zv, g, dv)
        return dz, (wgt * (1.0 + gt)) * dz, dg, db, colsum(wgt * dz * yv), colsum(duv), colsum(duv * xv)

    return rowmap(name, fn, [T_(z), T_(y), B_(gate), B_(lg), T_(dzn), T_(dun), T_(xn), B_(scn)],
                  [(D, F32), (D, BF16)], [(1, D)] * 5, rows=t, tm=256)


def residual_ln_loss_bwd(name, xin, y, tgt, gate, lg, lb, wgt, t):
    def fn(xv, yv, tv, gt, g, b):
        z = ALPHA * xv + (wgt * (1.0 + gt)) * yv
        d = _ln_fwd(z, g, b) - tv
        dz, dg, db = _ln_bwd(z, g, d * (1.0 / D))
        return dz, (wgt * (1.0 + gt)) * dz, dg, db, colsum(wgt * dz * yv), jnp.sum(d * d).reshape(1, 1)

    dz, dy, dlg, dlb, dgate, sq = rowmap(
        name, fn, [T_(xin), T_(y), T_(tgt), B_(gate), B_(lg), B_(lb)], [(D, F32), (D, BF16)],
        [(1, D), (1, D), (1, D), (1, 1)], rows=t, tm=256)
    return dz, dy, dlg, dlb, dgate, sq


def modulate_bwd(name, dz, du, xin, scale, t):
    def fn(dzv, duv, xv, sc):
        return ALPHA * dzv + duv * (1.0 + sc), colsum(duv), colsum(duv * xv)

    return rowmap(name, fn, [T_(dz), T_(du), T_(xin), B_(scale)], [(D, F32)], [(1, D), (1, D)], rows=t, tm=256)


def ffn_fwd(tag, u, wi, t, up_job, down_job=None):
    tm = min(1024, t)
    tn = 256
    per = FHP // tn

    def act(accs, _):
        a, b = accs
        s = _sigmoid(a)
        sl = a * s
        return b * (s * (1.0 + a * (1.0 - s))), sl, sl * b

    tmu = min(2048, t)
    hblk = pl.BlockSpec((tmu, tn), lambda i, j, k: (i, j))
    (ha, hb, g), up_moved = mm(
        tag + "_up", [u], [wi, wi], [(0, 0, 0), (0, 1, 1)], dims=NT, grid=(t // tmu, 2 * per, 1),
        a_specs=[pl.BlockSpec((tmu, D), lambda i, j, k: (i, 0))],
        b_specs=[pl.BlockSpec((None, tn, D), lambda i, j, k: (j // per, j % per, 0)),
                 pl.BlockSpec((None, tn, D), lambda i, j, k: (2 + j // per, j % per, 0))],
        outs=[sds((t, 2 * FHP), BF16)] * 3, out_specs=[hblk] * 3, acc_shapes=[(tmu, tn)] * 2, epilogue=act, job=up_job,
        sub_rows=tmu // 2)
    wo = up_moved[0].reshape(2 * FHP, D)
    tk = FHP
    y, down_moved = _with_moved(mm(
        tag + "_down", [g], [wo], [(0, 0, 0)], dims=NN, grid=(t // tm, 2, 2),
        a_specs=[pl.BlockSpec((tm, tk), lambda i, j, k: (i, k))],
        b_specs=[pl.BlockSpec((tk, D // 2), lambda i, j, k: (k, j))],
        outs=[sds((t, D), F32)], out_specs=[pl.BlockSpec((tm, D // 2), lambda i, j, k: (i, j))],
        acc_shapes=[(tm, D // 2)], job=down_job), down_job)
    return ha, hb, g, y, wo, up_moved, down_moved


def ffn_bwd(tag, u, ha, hb, g, dy, wi, wo, t, sp, dact_job=None, dwo_job=None):
    tm = min(1024, t)

    def dact(accs, ex):
        dg = accs[0]
        return dg * ex[0].astype(F32), dg * ex[1].astype(F32)

    tn = 256
    tmu = min(2048, t)
    hblk = pl.BlockSpec((tmu, tn), lambda i, j, k: (i, j))
    (dha, dhb), dact_moved = _with_moved(mm(
        tag + "_dact", [dy], [wo], [(0, 0, 0)], dims=NT, grid=(t // tmu, 2 * FHP // tn, 1),
        a_specs=[pl.BlockSpec((tmu, D), lambda i, j, k: (i, 0))],
        b_specs=[pl.BlockSpec((tn, D), lambda i, j, k: (j, 0))],
        outs=[sds((t, 2 * FHP), BF16)] * 2, out_specs=[hblk] * 2, acc_shapes=[(tmu, tn)],
        epilogue=dact, extras=[ha, hb], extra_specs=[hblk] * 2, job=dact_job, sub_rows=tmu // 2), dact_job)
    tk = min(2048, t)
    th = FHP // 2
    dwo, dwo_moved = _with_moved(mm(
        tag + "_dwo", [g], [dy], [(0, 0, 0)], dims=TN, grid=(4, 2, t // tk),
        a_specs=[pl.BlockSpec((tk, th), lambda i, j, k: (k, i))],
        b_specs=[pl.BlockSpec((tk, D // 2), lambda i, j, k: (k, j))],
        outs=[sds((2 * FHP, D), BF16)], out_specs=[pl.BlockSpec((th, D // 2), lambda i, j, k: (i, j))],
        acc_shapes=[(th, D // 2)], job=dwo_job), dwo_job)
    dwo = dwo.reshape(2, FHP, D)

    def dwi_part(part, dh, carry, job):
        return mm(
            f"{tag}_dwi{part}", [dh], [u], [(0, 0, 0)], dims=TN, grid=(4, 2, t // tk),
            a_specs=[pl.BlockSpec((tk, th), lambda i, j, k: (k, i))],
            b_specs=[pl.BlockSpec((tk, D // 2), lambda i, j, k: (k, j))],
            outs=[sds((4, FHP, D), BF16)],
            out_specs=[pl.BlockSpec((None, th, D // 2), lambda i, j, k: (2 * part + i // 2, i % 2, j))],
            acc_shapes=[(th, D // 2)], carry=carry, job=job)

    dwi, (sib_fo,) = dwi_part(0, dha, None, reduce_sibling_job([(dwo, view_ffn_out, FO, D)]))
    q_fo = chip_sum(tag + "_chipsum_fo", dwo, sib_fo, sp, FO, FO // 2, ffn_out=True)
    dwi, (far_fo,) = dwi_part(1, dhb, dwi, reduce_chips_job([q_fo]))
    (sib_fi,) = run_job(tag + "_sibling_fi", reduce_sibling_job([(dwi, view_lead, FHP, D)]))
    q_fi = chip_sum(tag + "_chipsum_fi", dwi, sib_fi, sp, FHP, FHP // 8)
    tmd = min(512, t)
    du, (far_fi,) = mm(
        tag + "_du", [dha, dhb], [wi, wi], [(0, 0, 0), (1, 1, 0)], dims=NN, grid=(t // tmd, 2, 2),
        a_specs=[pl.BlockSpec((tmd, FHP), lambda i, j, k: (i, k))] * 2,
        b_specs=[pl.BlockSpec((None, FHP, D // 2), lambda i, j, k: (k, 0, j)),
                 pl.BlockSpec((None, FHP, D // 2), lambda i, j, k: (2 + k, 0, j))],
        outs=[sds((t, D), F32)], out_specs=[pl.BlockSpec((tmd, D // 2), lambda i, j, k: (i, j))],
        acc_shapes=[(tmd, D // 2)], job=reduce_chips_job([q_fi]))
    return du, (q_fi, far_fi), (q_fo, far_fo), dact_moved, dwo_moved


def mix_fwd(u, wts, b_in, pool_scale, sinks, tabs, t, in_job, attn_job):
    w_in, wp, wba, wbb, wo = wts
    tm = min(1024, t)
    tmh = min(512, t)
    h, in_moved = mm("mix_in", [u], [w_in], [(0, 0, 0)], dims=NN, grid=(t // tmh, 4, 1),
                     a_specs=[pl.BlockSpec((tmh, D), lambda i, j, k: (i, 0))],
                     b_specs=[pl.BlockSpec((None, D, IN_SH), lambda i, j, k: (j, 0, 0))],
                     outs=[sds((t, IN_W), F32)], out_specs=[pl.BlockSpec((tmh, IN_SH), lambda i, j, k: (i, j))],
                     acc_shapes=[(tmh, IN_SH)], job=in_job)
    attn_job = attn_job(in_moved)
    pooled = pool_fwd(h, b_in, t, 512)
    gblk = pl.BlockSpec((tm, PG), lambda i, j, k: (i, j))
    mixed = mm("mix_pool", [pooled], [wp], [(0, 0, 0)], dims=NN, grid=(t // tm, 4, 1), a_specs=[gblk],
               b_specs=[pl.BlockSpec((None, PG, PG), lambda i, j, k: (j, 0, 0))],
               outs=[sds((t, PW), F32)], out_specs=[gblk], acc_shapes=[(tm, PG)])
    pm = rowmap("mix_pscale", lambda mv, ps: mv * ps, [T_(mixed), B_(pool_scale)], [(PW, BF16)], rows=t, tm=512)

    def branch(name, a, w):
        return mm(name, [a], [w], [(0, 0, 0)], dims=NN, grid=(t // tm, 4, 1),
                  a_specs=[pl.BlockSpec((tm, PW), lambda i, j, k: (i, 0))],
                  b_specs=[pl.BlockSpec((None, PW, D // 4), lambda i, j, k: (j, 0, 0))],
                  outs=[sds((t, D), F32)], out_specs=[pl.BlockSpec((tm, D // 4), lambda i, j, k: (i, j))],
                  acc_shapes=[(tm, D // 4)])

    ya = branch("mix_branch_a", pm, wba)

    def qkv(hq, hk, hv, bq, bk, bv, cc, sa, sb):
        return (_rope(hq + bq, cc, sa, sb) * (HD ** -0.5), _rope(hk + bk, cc, sa, sb), hv + bv)

    qr, kr, vv = rowmap(
        "mix_rope", qkv,
        [T_(h, QW, 1), T_(h, KVW, 8), T_(h, KVW, 9), B_(b_in, QW, 1), B_(b_in, KVW, 8), B_(b_in, KVW, 9),
         T_(tabs[0]), T_(tabs[1]), T_(tabs[2])],
        [(QW, BF16), (KVW, BF16), (KVW, BF16)], rows=t, tm=512)
    attn, attn_moved = attn_fwd(qr, kr, vv, sinks, t, attn_job)
    yb = branch("mix_branch_b", attn, wbb)
    cw = 512

    def merge(ga, gb, ba, bb, yav, ybv):
        return _sigmoid(ga + ba) * yav + _sigmoid(gb + bb) * ybv

    merged = rowmap(
        "mix_merge", merge,
        [T_(h, cw, 5), T_(h, cw, 9), B_(b_in, cw, 5), B_(b_in, cw, 9), T_(ya, cw), T_(yb, cw)],
        [(D, BF16)], rows=t, tm=512, ncol=D // cw)
    y = mm("mix_out", [merged], [wo], [(0, 0, 0)], dims=NN, grid=(t // tm, 2, 1),
           a_specs=[pl.BlockSpec((tm, D), lambda i, j, k: (i, 0))],
           b_specs=[pl.BlockSpec((D, D // 2), lambda i, j, k: (0, j))],
           outs=[sds((t, D), F32)], out_specs=[pl.BlockSpec((tm, D // 2), lambda i, j, k: (i, j))],
           acc_shapes=[(tm, D // 2)])
    return y, (h, pooled, mixed, pm, ya, qr, kr, vv, attn, yb, merged), attn_moved


def mix_bwd(u, saved, dy, wts, b_in, pool_scale, sinks, tabs, t):
    h, pooled, mixed, pm, ya, qr, kr, vv, attn, yb, merged = saved
    w_in, wp, wba, wbb, wo = wts
    tm = min(1024, t)
    tk = min(2048, t)
    dmerged = mm("mix_dmerged", [dy], [wo], [(0, 0, 0)], dims=NT, grid=(t // tm, 2, 1),
                 a_specs=[pl.BlockSpec((tm, D), lambda i, j, k: (i, 0))],
                 b_specs=[pl.BlockSpec((D // 2, D), lambda i, j, k: (j, 0))],
                 outs=[sds((t, D), F32)], out_specs=[pl.BlockSpec((tm, D // 2), lambda i, j, k: (i, j))],
                 acc_shapes=[(tm, D // 2)])
    half = pl.BlockSpec((tk, D // 2), lambda i, j, k: (k, i))
    dwo = mm("mix_dwo", [merged], [dy], [(0, 0, 0)], dims=TN, grid=(2, 2, t // tk), a_specs=[half],
             b_specs=[pl.BlockSpec((tk, D // 2), lambda i, j, k: (k, j))],
             outs=[sds((D, D), BF16)], out_specs=[pl.BlockSpec((D // 2, D // 2), lambda i, j, k: (i, j))],
             acc_shapes=[(D // 2, D // 2)])
    cw = 512

    def dmerge(dm, ga, gb, ba, bb, yav, ybv):
        sa_, sb_ = _sigmoid(ga + ba), _sigmoid(gb + bb)
        dga = dm * yav * sa_ * (1.0 - sa_)
        dgb = dm * ybv * sb_ * (1.0 - sb_)
        return dm * sa_, dm * sb_, dga, dgb, colsum(dga), colsum(dgb)

    dya, dyb, dgla, dglb, dbga, dbgb = rowmap(
        "mix_dmerge", dmerge,
        [T_(dmerged, cw), T_(h, cw, 5), T_(h, cw, 9), B_(b_in, cw, 5), B_(b_in, cw, 9), T_(ya, cw), T_(yb, cw)],
        [(D, BF16)] * 4, [(1, D), (1, D)], rows=t, tm=512, ncol=D // cw)

    def dbranch(name, dyv, act, w):
        dwb = mm(name + "_dw", [act], [dyv], [(0, 0, 0)], dims=TN, grid=(1, 4, t // tk),
                 a_specs=[pl.BlockSpec((tk, PW), lambda i, j, k: (k, 0))],
                 b_specs=[pl.BlockSpec((tk, D // 4), lambda i, j, k: (k, j))],
                 outs=[sds((4, PW, D // 4), BF16)], out_specs=[pl.BlockSpec((None, PW, D // 4), lambda i, j, k: (j, 0, 0))],
                 acc_shapes=[(PW, D // 4)])
        return dwb, lambda dt: mm(
            name + "_dx", [dyv], [w], [(0, 0, 0)], dims=NT, grid=(t // tm, 1, 4),
            a_specs=[pl.BlockSpec((tm, D // 4), lambda i, j, k: (i, k))],
            b_specs=[pl.BlockSpec((None, PW, D // 4), lambda i, j, k: (k, 0, 0))],
            outs=[sds((t, PW), dt)], out_specs=[pl.BlockSpec((tm, PW), lambda i, j, k: (i, 0))], acc_shapes=[(tm, PW)])

    dwba, dpm_fn = dbranch("mix_dbranch_a", dya, pm, wba)
    dwbb, dattn_fn = dbranch("mix_dbranch_b", dyb, attn, wbb)
    dpm, dattn = dpm_fn(F32), dattn_fn(BF16)
    dmixed, dps = rowmap("mix_dpscale", lambda dp, mv, ps: (dp * ps, colsum(dp * mv)),
                         [T_(dpm), T_(mixed), B_(pool_scale)], [(PW, BF16)], [(1, PW)], rows=t, tm=512)
    gblk = pl.BlockSpec((tm, PG), lambda i, j, k: (i, j))
    dpooled = mm("mix_dpool", [dmixed], [wp], [(0, 0, 0)], dims=NT, grid=(t // tm, 4, 1), a_specs=[gblk],
                 b_specs=[pl.BlockSpec((None, PG, PG), lambda i, j, k: (j, 0, 0))],
                 outs=[sds((t, PW), F32)], out_specs=[gblk], acc_shapes=[(tm, PG)])
    kblk = pl.BlockSpec((tk, PG), lambda i, j, k: (k, i))
    dwp = mm("mix_dwpool", [pooled], [dmixed], [(0, 0, 0)], dims=TN, grid=(4, 1, t // tk), a_specs=[kblk], b_specs=[kblk],
             outs=[sds((4, PG, PG), BF16)], out_specs=[pl.BlockSpec((None, PG, PG), lambda i, j, k: (i, 0, 0))],
             acc_shapes=[(PG, PG)])
    dxp, dbxp = pool_bwd(dpooled, t, 512)
    dqr, dkr, dvv, dsinks = attn_bwd(qr, kr, vv, dattn, sinks, t)

    def dqkv(dq, dk, dv, cc, sa, sb):
        dq = _rope_t(dq, cc, sa, sb) * (HD ** -0.5)
        dk = _rope_t(dk, cc, sa, sb)
        return dq, dk, dv, colsum(dq), colsum(dk), colsum(dv)

    dq, dk, dvb, dbq, dbk, dbv = rowmap(
        "mix_rope_bwd", dqkv, [T_(dqr), T_(dkr), T_(dvv), T_(tabs[0]), T_(tabs[1]), T_(tabs[2])],
        [(QW, BF16), (KVW, BF16), (KVW, BF16)], [(1, QW), (1, KVW), (1, KVW)], rows=t, tm=512)
    dh = jnp.concatenate([dxp, dq, dk, dvb, dgla, dglb], axis=1)
    db_in = jnp.concatenate([dbxp, dbq, dbk, dbv, dbga, dbgb], axis=1)
    dwin = mm("mix_dwin", [u], [dh], [(0, 0, 0)], dims=TN, grid=(2, 4, t // tk), a_specs=[half],
              b_specs=[pl.BlockSpec((tk, IN_SH), lambda i, j, k: (k, j))],
              outs=[sds((4, D, IN_SH), BF16)], out_specs=[pl.BlockSpec((None, D // 2, IN_SH), lambda i, j, k: (j, i, 0))],
              acc_shapes=[(D // 2, IN_SH)])
    dwp_sh = jnp.transpose(dwp.reshape(4, 4, 64, PG), (1, 0, 2, 3)).reshape(4, 4 * 64, PG)
    parts = {"win": dwin, "wp": dwp_sh, "wba": dwba, "wbb": dwbb, "wo": dwo.reshape(4, D // 4, D)}
    du, sib = mm("mix_du", [dh], [w_in], [(0, 0, 0)], dims=NT, grid=(t // tm, 2, 4),
                 a_specs=[pl.BlockSpec((tm, IN_SH), lambda i, j, k: (i, k))],
                 b_specs=[pl.BlockSpec((None, D // 2, IN_SH), lambda i, j, k: (k, j, 0))],
                 outs=[sds((t, D), F32)], out_specs=[pl.BlockSpec((tm, D // 2), lambda i, j, k: (i, j))],
                 acc_shapes=[(tm, D // 2)],
                 job=reduce_sibling_job([(p, view_lead, p.shape[1], p.shape[2]) for p in parts.values()]))
    return du, parts, dict(zip(parts, sib)), db_in, dps, dsinks


def cast_shard(name, w, sp, ffn_out=False):
    rows, cols = w.shape
    if ffn_out:
        tm = rows // 2
        shape = (2, FHP, D)
        spec = pl.BlockSpec((None, tm, cols), lambda j, i, s: (s[0] // 2, (s[0] % 2) * 2 + i, 0))
    else:
        tm = rows // 4
        shape = (4, rows, cols)
        spec = pl.BlockSpec((None, tm, cols), lambda j, i, s: (s[0], i, 0))
    return rowmap(name, lambda wv: wv, [T_(w)], [(shape, BF16, spec)], rows=rows, tm=tm, sp=sp)


def cast_ffn_in(name, wt, sp):
    tm = FH // 4
    buf = rowmap(name, lambda wv: wv, [T_(wt)],
                 [((4, FHP, D), BF16, pl.BlockSpec((None, tm, D), lambda j, i, s: (s[0], i, 0)))], rows=FH, tm=tm, sp=sp)
    pad = FHP - FH

    def zero_pad(_, __, out):
        out[...] = jnp.zeros_like(out)

    return pl.pallas_call(
        zero_pad, name=name + "_pad", out_shape=sds(buf.shape, BF16), input_output_aliases={1: 0},
        grid_spec=pltpu.PrefetchScalarGridSpec(
            num_scalar_prefetch=1, grid=(1,), in_specs=[ANY],
            out_specs=pl.BlockSpec((None, pad, D), lambda i, s: (s[0], FH // pad, 0))))(sp, buf)


def chip_sum(name, dw, got, sp, rows, tm, ffn_out=False):
    hr, cols = rows // 2, got.shape[2]
    per = hr // tm
    pos = pl.BlockSpec((None, tm, cols), lambda j, i, s: (i // per, i % per, 0))
    if ffn_out:
        mine = pl.BlockSpec((None, tm, cols), lambda j, i, s: (i // 2, (i % 2) * 2 + s[1], 0))
    else:
        mine = pl.BlockSpec((None, tm, cols), lambda j, i, s: (i // per, s[1] * per + i % per, 0))
    return rowmap(name, lambda av, bv: av.astype(F32) + bv.astype(F32), [X_(dw, mine), X_(got, pos)],
                  [(got.shape, BF16, pos)], rows=4 * hr, tm=tm, sp=sp)


def chip_total(name, q, got, sp, rows, tm):
    hr, cols = rows // 2, q.shape[2]
    per = hr // tm

    def part(f):
        return X_(got, pl.BlockSpec((None, tm, cols), lambda j, i, s, f=f: (f, i, 0)))

    return rowmap(
        name, lambda av, b0, b1, b2: ((av.astype(F32) + b0.astype(F32)) + b1.astype(F32)) + b2.astype(F32),
        [X_(q, pl.BlockSpec((None, tm, cols), lambda j, i, s: (s[0], i, 0))), part(0), part(1), part(2)],
        [((rows, cols), F32, pl.BlockSpec((tm, cols), lambda j, i, s: (s[1] * per + i, 0)))], rows=hr, tm=tm, sp=sp)


def kernel(x, c, w_ada, b_ada, ln_g, ln_b, w_ffn1_in, w_ffn1_out, w_in, b_in, w_pool, pool_scale, sinks, w_branch_a, w_branch_b, w_out, w_ffn2_in, w_ffn2_out, loss_target, m_w_ada, m_b_ada, m_ln_g, m_ln_b, m_w_ffn1_in, m_w_ffn1_out, m_w_in, m_b_in, m_w_pool, m_pool_scale, m_sinks, m_w_branch_a, m_w_branch_b, m_w_out, m_w_ffn2_in, m_w_ffn2_out, v_w_ada, v_b_ada, v_ln_g, v_ln_b, v_w_ffn1_in, v_w_ffn1_out, v_w_in, v_b_in, v_w_pool, v_pool_scale, v_sinks, v_w_branch_a, v_w_branch_b, v_w_out, v_w_ffn2_in, v_w_ffn2_out):
    t = x.shape[1]
    xs, tgt = x[0], loss_target[0]
    xi, yi, ci = lax.axis_index("x"), lax.axis_index("y"), lax.axis_index("c")
    chip = 2 * xi + yi
    dev = 2 * chip + ci
    b_in2, ps2, sinks2 = b_in, pool_scale, sinks

    sp = jnp.stack([chip, ci]).astype(jnp.int32)
    tr = lambda a: jnp.swapaxes(a[0], 0, 1)

    first = jnp.concatenate([c.reshape(-1), ln_g.reshape(-1), ln_b.reshape(-1)]).reshape(-1, 128)
    first_all = allgather_small("gather_cond", first).reshape(8, -1)
    c_all = first_all[:, :D]
    ln_parts = first_all[0::2, D:].reshape(4, 2, 3, D // 4)
    ln_full = jnp.transpose(ln_parts, (1, 2, 0, 3)).reshape(2, 3, D)
    lgs = [ln_full[0, s:s + 1] for s in range(3)]
    lbs = [ln_full[1, s:s + 1] for s in range(3)]
    c16 = jnp.pad(c_all, ((0, 8), (0, 0)))
    b_ada_sh = lax.dynamic_slice(b_ada, (0, chip * ADA_SH), (1, ADA_SH))
    mod_part = ada_fwd(c16, w_ada[0], b_ada_sh)[:8]
    mod_all = allgather_small("gather_mod", mod_part.reshape(-1, 128)).reshape(8, 8, ADA_SH)
    mod_mine = lax.dynamic_index_in_dim(mod_all[0::2], dev, axis=1, keepdims=False).reshape(9, D)
    mods = [[mod_mine[3 * s + k:3 * s + k + 1] for k in range(3)] for s in range(3)]

    f1i_buf, f1i_send, f1i_recv, _ = gather_start(
        "gather_f1i_start", cast_ffn_in("cast_f1i", tr(w_ffn1_in), sp), FHP, [mod_mine])
    plain = [("f1o", w_ffn1_out[0]), ("win", w_in[0]), ("wp", w_pool[0].reshape(4 * 64, PG)), ("wba", w_branch_a[0]),
             ("wbb", w_branch_b[0]), ("wo", w_out[0]), ("f2o", w_ffn2_out[0])]
    sh = {n: cast_shard("cast_" + n, w, sp, ffn_out=n in ("f1o", "f2o")) for n, w in plain}
    sh["f1i"] = f1i_buf
    sh["f2i"] = cast_ffn_in("cast_f2i", tr(w_ffn2_in), sp)
    order = ["f1i", "f1o", "win", "wp", "wba", "wbb", "wo", "f2i", "f2o"]
    views = {n: (view_ffn_out if n in ("f1o", "f2o") else view_lead) for n in order}
    shard_rows = {n: (FO if n in ("f1o", "f2o") else sh[n].shape[1]) for n in order}
    shard_cols = {n: sh[n].shape[2] for n in order}
    tiles = {"f1i": FHP // 8, "f1o": FO // 2, "win": 512, "wp": 128, "wba": 512, "wbb": 512, "wo": 256,
             "f2i": FHP // 8, "f2o": FO // 2}

    def item(n, part=0, parts=1):
        return (sh[n], views[n], shard_rows[n], part, parts)

    tabs = rope_tables(t)
    (sh0, sc0, gt0), (sh1, sc1, gt1), (sh2, sc2, gt2) = mods

    u0 = modulate("ffn1_mod", xs, sh0, sc0, t)
    landed = gather_wait("gather_f1i_wait", f1i_buf, f1i_send, f1i_recv, FHP,
                         [u0] + [sh[n] for n in order if n != "f1i"])
    (g_f1i,) = run_job("gather_f1i_forward", forward_job(landed, FHP))
    ha1, hb1, g1, y1, f1o, (_, g_win), (g_wp, g_wba, g_wbb, g_wo) = ffn_fwd(
        "ffn1", u0, g_f1i, t, gather_job([item("f1o"), item("win")]),
        gather_job([item(n) for n in ("wp", "wba", "wbb", "wo")]))
    x1, z1, u1 = residual_ln_mod("ffn1_ln", xs, y1, gt0, lgs[0], lbs[0], 0.5, sh1, sc1, t)
    wp_full = jnp.transpose(g_wp.reshape(4, 4, 64, PG), (1, 0, 2, 3)).reshape(4, PG, PG)
    wts = (g_win, wp_full, g_wba, g_wbb, g_wo.reshape(D, D))
    y2, sv2, (g_f2i,) = mix_fwd(
        u1, wts, b_in2, ps2, sinks2, tabs, t, gather_job([item("f2i", 0, 2)]),
        lambda moved: gather_job([(moved[0], view_lead, FHP, 1, 2)]))
    x2, z2, u2 = residual_ln_mod("mix_ln", x1, y2, gt1, lgs[1], lbs[1], 1.0, sh2, sc2, t)
    ha3, hb3, g3, y3, f2o, _, _ = ffn_fwd("ffn2", u2, g_f2i, t, gather_job([item("f2o")]))

    dz3, dy3, dlg2, dlb2, dgt2, sq = residual_ln_loss_bwd("ffn2_ln_loss", x2, y3, tgt, gt2, lgs[2], lbs[2], 0.5, t)
    loss = lax.psum(0.5 * sq[0, 0] / D, ("x", "y", "c"))
    du3, red_f2i, red_f2o, _, _ = ffn_bwd("ffn2", u2, ha3, hb3, g3, dy3, g_f2i, f2o, t, sp)
    dz2, dy2, dlg1, dlb1, dgt1, dsh2, dsc2 = residual_ln_bwd("mix_ln_bwd", z2, (dz3, du3, x2, sc2), y2, gt1, lgs[1], 1.0, t)
    du2, mix_parts, sib, db_in, dps, dsinks = mix_bwd(u1, sv2, dy2, wts, b_in2, ps2, sinks2, tabs, t)
    q = {n: chip_sum("chipsum_" + n, mix_parts[n], sib[n], sp, shard_rows[n], tiles[n]) for n in mix_parts}
    dz1, dy1, dlg0, dlb0, dgt0, dsh1, dsc1 = residual_ln_bwd("ffn1_ln_bwd", z1, (dz2, du2, x1, sc1), y1, gt0, lgs[0], 0.5, t)
    du1, red_f1i, red_f1o, far_a, far_b = ffn_bwd(
        "ffn1", u0, ha1, hb1, g1, dy1, g_f1i, f1o, t, sp,
        reduce_chips_job([q["win"], q["wp"]]), reduce_chips_job([q["wo"], q["wba"], q["wbb"]]))
    dx0, dsh0, dsc0 = modulate_bwd("ffn1_mod_bwd", dz1, du1, xs, sc0, t)
    gm0, gm1, gm2 = (dsh0, dsc0, dgt0), (dsh1, dsc1, dgt1), (dsh2, dsc2, dgt2)
    reduced = {"f1i": red_f1i, "f1o": red_f1o, "f2i": red_f2i, "f2o": red_f2o, "win": (q["win"], far_a[0]),
               "wp": (q["wp"], far_a[1]), "wo": (q["wo"], far_b[0]), "wba": (q["wba"], far_b[1]), "wbb": (q["wbb"], far_b[2])}
    halves = [chip_total("total_" + n, *reduced[n], sp, shard_rows[n], tiles[n]) for n in order]

    small = jnp.concatenate([*gm0, *gm1, *gm2, dlg0, dlg1, dlg2, dlb0, dlb1, dlb2, db_in, dps, dsinks], axis=1)
    n_small = small.shape[1]
    rows_small = -(-n_small // 1024) * 8
    small = jnp.pad(small, ((0, 0), (0, rows_small * 128 - n_small))).reshape(rows_small, 128)
    small_all = allgather_small("gather_small", small)
    tot = sum_devices(small_all).reshape(1, -1)
    gmod_all = small_all.reshape(8, -1)[:, :9 * D]
    o = 9 * D
    g_b_ada = tot[:, :o]
    g_ln_g = lax.dynamic_slice(tot[:, o:o + 3 * D].reshape(3, D), (0, chip * (D // 4)), (3, D // 4))
    g_ln_b = lax.dynamic_slice(tot[:, o + 3 * D:o + 6 * D].reshape(3, D), (0, chip * (D // 4)), (3, D // 4))
    o += 6 * D
    g_b_in, g_ps, g_sinks = tot[:, o:o + IN_W], tot[:, o + IN_W:o + IN_W + PW], tot[:, o + IN_W + PW:o + IN_W + PW + N_Q]

    gm16 = jnp.pad(lax.dynamic_slice(gmod_all, (0, chip * ADA_SH), (8, ADA_SH)), ((0, 8), (0, 0)))
    (g_w_ada, d_w_ada, nm_w_ada, nv_w_ada), _ = ada_bwd_adam(c16, gm16, w_ada[0], m_w_ada[0], v_w_ada[0], None)
    gw = dict(zip(order, run_job("share_halves", share_halves_job(halves))))

    def big(n, w, m, v, tm):
        shape = w.shape
        w2, m2, v2 = (a.reshape(shape[-2] if a.ndim == 3 else -1, shape[-1]) for a in (w, m, v))
        return [r.reshape(shape) for r in adam_rows("adam_" + n, w2, gw[n], m2, v2, tm)]

    def big_t(n, w, m, v):
        return [jnp.swapaxes(r, 0, 1)[None] for r in adam_rows("adam_" + n, tr(w), gw[n], tr(m), tr(v), FH // 8)]

    def tiny(n, w, g, m, v):
        return [g.reshape(w.shape)] + list(adam_small("adam_" + n, w, g.reshape(w.shape), m, v))

    res = {
        "w_ada": [a[None] for a in (g_w_ada, d_w_ada, nm_w_ada, nv_w_ada)],
        "b_ada": tiny("b_ada", b_ada, g_b_ada, m_b_ada, v_b_ada),
        "ln_g": tiny("ln_g", ln_g, g_ln_g, m_ln_g, v_ln_g),
        "ln_b": tiny("ln_b", ln_b, g_ln_b, m_ln_b, v_ln_b),
        "w_ffn1_in": big_t("f1i", w_ffn1_in, m_w_ffn1_in, v_w_ffn1_in),
        "w_ffn1_out": big("f1o", w_ffn1_out, m_w_ffn1_out, v_w_ffn1_out, FO // 4),
        "w_in": big("win", w_in, m_w_in, v_w_in, 256),
        "b_in": tiny("b_in", b_in, g_b_in, m_b_in, v_b_in),
        "w_pool": big("wp", w_pool, m_w_pool, v_w_pool, 256),
        "pool_scale": tiny("pool_scale", pool_scale, g_ps, m_pool_scale, v_pool_scale),
        "sinks": tiny("sinks", sinks, g_sinks, m_sinks, v_sinks),
        "w_branch_a": big("wba", w_branch_a, m_w_branch_a, v_w_branch_a, 512),
        "w_branch_b": big("wbb", w_branch_b, m_w_branch_b, v_w_branch_b, 512),
        "w_out": big("wo", w_out, m_w_out, v_w_out, 128),
        "w_ffn2_in": big_t("f2i", w_ffn2_in, m_w_ffn2_in, v_w_ffn2_in),
        "w_ffn2_out": big("f2o", w_ffn2_out, m_w_ffn2_out, v_w_ffn2_out, FO // 4),
    }
    names = ["w_ada", "b_ada", "ln_g", "ln_b", "w_ffn1_in", "w_ffn1_out", "w_in", "b_in", "w_pool", "pool_scale", "sinks",
             "w_branch_a", "w_branch_b", "w_out", "w_ffn2_in", "w_ffn2_out"]
    return (loss, dx0[None], *[res[n][0] for n in names], *[res[n][1] for n in names],
            *[res[n][2] for n in names], *[res[n][3] for n in names])
```

```python
import jax
import jax.numpy as jnp
from jax import lax
from jax.experimental import pallas as pl
from jax.experimental.pallas import tpu as pltpu

F32 = jnp.float32
BF16 = jnp.bfloat16
MESH = pl.DeviceIdType.MESH
ANY = pl.BlockSpec(memory_space=pl.ANY)

D = 2048
N_Q, N_KV, HD = 16, 4, 64
QW, KVW = N_Q * HD, N_KV * HD
BLK = 128
POOL_WINDOWS = (2, 4, 8, 16)
PW, PG = 1024, 256
HALO = 16
ROPE_THETA = 500000.0
ROT = HD // 4
LN_EPS = 1e-5
ALPHA = 2.0 ** 0.25
FH = 2752
FHP = 2816
FO = 1376
IN_W = 6656
IN_SH = IN_W // 4
ADA_SH = 18432 // 4
B1, B2, LR, EPS, WD, STEP = 0.9, 0.999, 0.001, 1e-08, 0.01, 10
VMEM_LIMIT = 56 * 1024 * 1024
FLIPS = ((1, 0), (0, 1), (1, 1))
NN = (((1,), (0,)), ((), ()))
NT = (((1,), (1,)), ((), ()))
TN = (((0,), (0,)), ((), ()))


def _params(sem):
    return pltpu.CompilerParams(dimension_semantics=sem, vmem_limit_bytes=VMEM_LIMIT)


def _aligned(v, m):
    return v if isinstance(v, int) else pl.multiple_of(v, m)


def _sigmoid(v):
    return 1.0 / (1.0 + jnp.exp(-v))


def T_(arr, width=None, off=0):
    return ("t", arr, width, off)


def B_(arr, width=None, off=0):
    return ("b", arr, width, off)


def X_(arr, spec):
    return ("x", arr, spec, 0)


def rowmap(name, fn, ins, outs, accs=(), *, rows, tm, ncol=1, with_ids=False, sp=None, alias=None):
    tm = min(tm, rows)
    nrow = rows // tm
    in_specs, arrs = [], []
    for kind, arr, width, off in ins:
        if kind == "x":
            in_specs.append(width)
        elif kind == "t":
            w = arr.shape[1] if width is None else width
            in_specs.append(pl.BlockSpec((tm, w), lambda j, i, *_, off=off: (i, off + j)))
        else:
            w = arr.shape[1] if width is None else width
            in_specs.append(pl.BlockSpec((arr.shape[0], w), lambda j, i, *_, off=off: (0, off + j)))
        arrs.append(arr)
    out_shape, out_specs = [], []
    for o in outs:
        if len(o) == 3:
            out_shape.append(jax.ShapeDtypeStruct(o[0], o[1]))
            out_specs.append(o[2])
        else:
            out_shape.append(jax.ShapeDtypeStruct((rows, o[0]), o[1]))
            out_specs.append(pl.BlockSpec((tm, o[0] // ncol), lambda j, i, *_: (i, j)))
    for r, width in accs:
        out_shape.append(jax.ShapeDtypeStruct((r, width), F32))
        out_specs.append(pl.BlockSpec((r, width // ncol), lambda j, i, *_: (0, j)))
    ni, no = len(ins), len(outs)
    nsp = 0 if sp is None else 1

    def body(*refs):
        refs = refs[nsp:]
        i = pl.program_id(1)
        vals = [r[...] for r in refs[:ni]]
        res = fn(pl.program_id(0), i, *vals) if with_ids else fn(*vals)
        if not isinstance(res, (tuple, list)):
            res = (res,)
        for r, v in zip(refs[ni:ni + no], res[:no]):
            r[...] = v.astype(r.dtype)
        for r, v in zip(refs[ni + no:], res[no:]):
            @pl.when(i == 0)
            def _(r=r, v=v):
                r[...] = v

            @pl.when(i > 0)
            def _(r=r, v=v):
                r[...] += v

    grid_spec = pltpu.PrefetchScalarGridSpec(num_scalar_prefetch=nsp, grid=(ncol, nrow), in_specs=in_specs,
                                             out_specs=out_specs)
    res = pl.pallas_call(
        body, name=name, grid_spec=grid_spec, out_shape=out_shape,
        input_output_aliases={nsp + k: v for k, v in (alias or {}).items()},
        compiler_params=_params(("arbitrary", "arbitrary")),
    )(*([sp] if nsp else []), *arrs)
    return res[0] if len(res) == 1 else res


def colsum(v):
    return jnp.sum(v, axis=0, keepdims=True)


def mm(name, a_ops, b_ops, ops, *, dims, grid, a_specs, b_specs, outs, out_specs, acc_shapes,
       epilogue=None, extras=(), extra_specs=(), carry=None, job=None, sub_rows=None):
    gk = grid[2]
    na, nb, ne, nacc = len(a_ops), len(b_ops), len(extras), len(acc_shapes)
    nc = 0 if carry is None else 1
    no = len(outs)

    def body(*refs):
        a_refs = refs[:na]
        b_refs = refs[na:na + nb]
        e_refs = refs[na + nb:na + nb + ne]
        o_refs = refs[na + nb + ne + nc:na + nb + ne + nc + no]
        acc_refs = refs[na + nb + ne + nc + no:]
        k = pl.program_id(2)

        def partials(rows=slice(None)):
            res = [None] * nacc
            for ai, bi, ci in ops:
                p = lax.dot_general(a_refs[ai][rows], b_refs[bi][...], dims, preferred_element_type=F32)
                res[ci] = p if res[ci] is None else res[ci] + p
            return res

        def finish(accs, rows=slice(None)):
            outv = epilogue(accs, [e[rows] for e in e_refs]) if epilogue else (accs[0],)
            for o, v in zip(o_refs, outv):
                o[rows] = v.astype(o.dtype)

        if gk == 1 and sub_rows:
            for s in range(out_specs[0].block_shape[-2] // sub_rows):
                rows = pl.ds(s * sub_rows, sub_rows)
                finish(partials(rows), rows)
        elif gk == 1:
            finish(partials())
        else:
            ps = partials()

            @pl.when(k == 0)
            def _():
                for acc, p in zip(acc_refs, ps):
                    acc[...] = p

            @pl.when((k > 0) & (k < gk - 1))
            def _():
                for acc, p in zip(acc_refs, ps):
                    acc[...] += p

            @pl.when(k == gk - 1)
            def _():
                finish([acc[...] + p for acc, p in zip(acc_refs, ps)])

    res, moved = carried_call(
        body, name, grid,
        list(a_specs) + list(b_specs) + list(extra_specs) + ([ANY] if nc else []), list(out_specs), list(outs),
        [pltpu.VMEM(s, F32) for s in acc_shapes] if gk > 1 else [],
        [*a_ops, *b_ops, *extras, *([carry] if nc else [])], {na + nb + ne: 0} if nc else {}, job)
    res = res[0] if len(res) == 1 else res
    return res if job is None else (res, moved)


def sds(shape, dt):
    return jax.ShapeDtypeStruct(shape, dt)


class Job:
    def __init__(self, ins, outs, aliases, scratch, start, mid, finish):
        self.ins, self.outs, self.aliases, self.scratch = list(ins), list(outs), dict(aliases), list(scratch)
        self.start, self.mid, self.finish = start, mid, finish


def carried_call(body, name, grid, in_specs, out_specs, out_shape, scratch, args, aliases, job, mid_at=0.9):
    sem = ("arbitrary",) * len(grid)
    if job is None:
        res = pl.pallas_call(body, name=name, grid=grid, in_specs=in_specs, out_specs=out_specs, out_shape=out_shape,
                             scratch_shapes=scratch, input_output_aliases=aliases, compiler_params=_params(sem))(*args)
        return list(res), []
    ni, no, ns = len(in_specs), len(out_specs), len(scratch)
    ci, co = len(job.ins), len(job.outs)
    total = 1
    for g in grid:
        total *= g
    mid_step = min(max(int(total * mid_at), 1), total - 1)

    def full(*refs):
        ins, cins = refs[:ni], refs[ni:ni + ci]
        outs, couts = refs[ni + ci:ni + ci + no], refs[ni + ci + no:ni + ci + no + co]
        scr, cscr = refs[ni + ci + no + co:ni + ci + no + co + ns], refs[ni + ci + no + co + ns:]
        step = 0
        for d, g in enumerate(grid):
            step = step * g + pl.program_id(d)

        @pl.when(step == 0)
        def _():
            job.start(cins, couts, cscr)

        body(*ins, *outs, *scr)

        @pl.when(step == mid_step)
        def _():
            job.mid(cins, couts, cscr)

        @pl.when(step == total - 1)
        def _():
            job.finish(cins, couts, cscr)

    al = dict(aliases)
    al.update({ni + k: no + v for k, v in job.aliases.items()})
    res = pl.pallas_call(
        full, name=name, grid=grid, in_specs=in_specs + [ANY] * ci, out_specs=out_specs + [ANY] * co,
        out_shape=out_shape + job.outs, scratch_shapes=scratch + job.scratch, input_output_aliases=al,
        compiler_params=_params(sem))(*args, *job.ins)
    return list(res[:no]), list(res[no:])


def merge_jobs(a, b):
    ni, no, ns = len(a.ins), len(a.outs), len(a.scratch)

    def both(fa, fb):
        def run(ins, outs, scr):
            fa(ins[:ni], outs[:no], scr[:ns])
            fb(ins[ni:], outs[no:], scr[ns:])
        return run

    aliases = dict(a.aliases)
    aliases.update({ni + k: no + v for k, v in b.aliases.items()})
    return Job(a.ins + b.ins, a.outs + b.outs, aliases, a.scratch + b.scratch,
               both(a.start, b.start), both(a.mid, b.mid), both(a.finish, b.finish))


def _with_moved(res, job):
    return res if job is not None else (res, [])


def run_job(name, job):
    ci = len(job.ins)

    def body(*refs):
        cins, couts, cscr = refs[:ci], refs[ci:ci + len(job.outs)], refs[ci + len(job.outs):]
        job.start(cins, couts, cscr)
        job.mid(cins, couts, cscr)
        job.finish(cins, couts, cscr)

    return list(pl.pallas_call(
        body, name=name, in_specs=[ANY] * ci, out_specs=[ANY] * len(job.outs), out_shape=job.outs,
        scratch_shapes=job.scratch, input_output_aliases=job.aliases)(*job.ins))


def _place():
    x, y, c = lax.axis_index("x"), lax.axis_index("y"), lax.axis_index("c")
    chips = [((1 - x) if fx else x, (1 - y) if fy else y) for fx, fy in FLIPS]
    return x, y, c, chips


def allgather_small(name, v):
    r = v.shape[0]

    def body(x_ref, out_ref, send_sems, recv_sems, local_sem):
        x, y, c, chips = _place()
        me, sibling = (x, y, c), (x, y, 1 - c)

        def rows(px, py, pc):
            return out_ref.at[4 * px + 2 * py + pc]

        def copy(k, block, to, src=None):
            return pltpu.make_async_remote_copy(
                src_ref=rows(*block) if src is None else src, dst_ref=rows(*block),
                send_sem=send_sems.at[k], recv_sem=recv_sems.at[k], device_id=to, device_id_type=MESH)

        mine = pltpu.make_async_copy(x_ref, rows(*me), local_sem)
        mine.start()
        first = [copy(0, me, sibling, src=x_ref)]
        first += [copy(1 + j, me, (*chip, c), src=x_ref) for j, chip in enumerate(chips)]
        for cp in first:
            cp.start()
        passed = [copy(4 + j, (*chip, c), sibling) for j, chip in enumerate(chips)]
        for j, chip in enumerate(chips):
            copy(1 + j, (*chip, c), me).wait_recv()
            passed[j].start()
        copy(0, sibling, me).wait_recv()
        for j, chip in enumerate(chips):
            copy(4 + j, (*chip, 1 - c), me).wait_recv()
        for cp in first + passed:
            cp.wait_send()
        mine.wait()

    return pl.pallas_call(
        body, name=name, out_shape=sds((8, r, 128), v.dtype),
        in_specs=[pl.BlockSpec(memory_space=pltpu.VMEM)], out_specs=pl.BlockSpec(memory_space=pltpu.VMEM),
        scratch_shapes=[pltpu.SemaphoreType.DMA((7,)), pltpu.SemaphoreType.DMA((7,)), pltpu.SemaphoreType.DMA],
    )(v)


def _half(ref, rows, hf):
    hr = rows // 2
    return ref.at[pl.ds(_aligned(hf * hr, 16), hr)]


def view_lead(ref, p):
    return ref.at[p]


def view_ffn_out(ref, p):
    return ref.at[p // 2, pl.ds(_aligned((p % 2) * FO, 16), FO)]


def _remote(ref, dst, send_sems, recv_sems, idx, to):
    return pltpu.make_async_remote_copy(src_ref=ref, dst_ref=dst, send_sem=send_sems.at[idx], recv_sem=recv_sems.at[idx],
                                        device_id=to, device_id_type=MESH)


def gather_job(items):
    nw = len(items)
    pads = [w for w, it in enumerate(items) if it[1] is view_ffn_out]

    def piece(ref, w, p, hf):
        _, view, rws, part, parts = items[w]
        pr = rws // 2 // parts
        return view(ref, p).at[pl.ds(_aligned(hf * (rws // 2) + part * pr, 16), pr)]

    def pad_copies(outs, scr):
        return [pltpu.make_async_copy(scr[2], outs[w].at[h, pl.ds(2 * FO, FHP - 2 * FO)], scr[3].at[2 * n + h])
                for n, w in enumerate(pads) for h in range(2)]

    def start(_, outs, scr):
        x, y, c, chips = _place()
        if pads:
            scr[2][...] = jnp.zeros_like(scr[2])
            for cp in pad_copies(outs, scr):
                cp.start()
        for w in range(nw):
            mine = piece(outs[w], w, 2 * x + y, c)
            for f, (px, py) in enumerate(chips):
                _remote(mine, mine, scr[0], scr[1], (w, f), (px, py, c)).start()

    def mid(_, outs, scr):
        x, y, c, chips = _place()
        for w in range(nw):
            for f, (px, py) in enumerate(chips):
                land = piece(outs[w], w, 2 * px + py, c)
                _remote(land, land, scr[0], scr[1], (w, f), (px, py, c)).wait_recv()
                _remote(land, land, scr[0], scr[1], (w, 3 + f), (x, y, 1 - c)).start()

    def finish(_, outs, scr):
        x, y, c, chips = _place()
        for w in range(nw):
            for f, (px, py) in enumerate(chips):
                land = piece(outs[w], w, 2 * px + py, 1 - c)
                _remote(land, land, scr[0], scr[1], (w, 3 + f), (x, y, 1 - c)).wait_recv()
        for w in range(nw):
            mine = piece(outs[w], w, 2 * x + y, c)
            for f in range(6):
                _remote(mine, mine, scr[0], scr[1], (w, f), (x, y, 1 - c)).wait_send()
        for cp in pad_copies(outs, scr):
            cp.wait()

    scratch = [pltpu.SemaphoreType.DMA((nw, 6)), pltpu.SemaphoreType.DMA((nw, 6))]
    if pads:
        scratch += [pltpu.VMEM((FHP - 2 * FO, D), BF16), pltpu.SemaphoreType.DMA((2 * len(pads),))]
    bufs = [it[0] for it in items]
    return Job(bufs, [sds(b.shape, BF16) for b in bufs], {w: w for w in range(nw)}, scratch, start, mid, finish)


HBM = pl.BlockSpec(memory_space=pltpu.HBM)
SEM = pl.BlockSpec(memory_space=pltpu.SEMAPHORE)
SPLIT = pltpu.CompilerParams(has_side_effects=pltpu.SideEffectType.DATAFLOW_SIDE_EFFECTING)


def gather_start(name, buf, rows, after):
    def body(*refs):
        out, send_sems, recv_sems, token = refs[1 + len(after):]
        x, y, c, chips = _place()
        mine = _half(out.at[2 * x + y], rows, c)
        for f, (px, py) in enumerate(chips):
            _remote(mine, mine, send_sems, recv_sems, f, (px, py, c)).start()
        token[...] = jnp.zeros_like(token)

    return pl.pallas_call(
        body, name=name,
        out_shape=(pltpu.HBM(buf.shape, buf.dtype), pltpu.SemaphoreType.DMA((3,)), pltpu.SemaphoreType.DMA((3,)),
                   sds((8, 128), F32)),
        in_specs=(HBM,) + (ANY,) * len(after), out_specs=(HBM, SEM, SEM, pl.BlockSpec(memory_space=pltpu.VMEM)),
        input_output_aliases={0: 0}, compiler_params=SPLIT)(pltpu.with_memory_space_constraint(buf, pltpu.HBM), *after)


def gather_wait(name, buf, send_sems, recv_sems, rows, after):
    def body(_, send_sems, recv_sems, *rest):
        out = rest[-1]
        x, y, c, chips = _place()
        mine = _half(out.at[2 * x + y], rows, c)
        for f, (px, py) in enumerate(chips):
            cp = _remote(mine, _half(out.at[2 * px + py], rows, c), send_sems, recv_sems, f, (px, py, c))
            cp.wait_send()
            cp.wait_recv()

    return pl.pallas_call(
        body, name=name, out_shape=pltpu.HBM(buf.shape, buf.dtype),
        in_specs=(HBM, SEM, SEM) + (ANY,) * len(after), out_specs=HBM, input_output_aliases={0: 0},
        compiler_params=SPLIT)(buf, send_sems, recv_sems, *after)


def forward_job(buf, rows):
    def copies(outs, scr, hf):
        x, y, c, chips = _place()
        half = c if hf == 0 else 1 - c
        return [_remote(_half(outs[0].at[2 * px + py], rows, half), _half(outs[0].at[2 * px + py], rows, half),
                        scr[0], scr[1], f, (x, y, 1 - c)) for f, (px, py) in enumerate(chips)]

    def start(_, outs, scr):
        for cp in copies(outs, scr, 0):
            cp.start()

    def finish(_, outs, scr):
        for cp in copies(outs, scr, 1):
            cp.wait_recv()
        for cp in copies(outs, scr, 0):
            cp.wait_send()

    return Job([buf], [sds(buf.shape, buf.dtype)], {0: 0},
               [pltpu.SemaphoreType.DMA((3,)), pltpu.SemaphoreType.DMA((3,))], start, lambda *_: None, finish)


def reduce_sibling_job(items):
    nw = len(items)

    def copies(ins, got, scr):
        x, y, c, _ = _place()
        return [_remote(_half(view(ins[w], p), rws, 1 - c), got[w].at[p], scr[0], scr[1], (w, p), (x, y, 1 - c))
                for w, (_, view, rws, _) in enumerate(items) for p in range(4)]

    def start(ins, got, scr):
        for cp in copies(ins, got, scr):
            cp.start()

    def finish(ins, got, scr):
        for cp in copies(ins, got, scr):
            cp.wait()

    return Job([it[0] for it in items], [sds((4, it[2] // 2, it[3]), BF16) for it in items], {},
               [pltpu.SemaphoreType.DMA((nw, 4)), pltpu.SemaphoreType.DMA((nw, 4))], start, lambda *_: None, finish)


def reduce_chips_job(qs):
    nw = len(qs)

    def copies(ins, got, scr):
        x, y, c, chips = _place()
        return [_remote(ins[w].at[2 * px + py], got[w].at[f], scr[0], scr[1], (w, f), (px, py, c))
                for w in range(nw) for f, (px, py) in enumerate(chips)]

    def start(ins, got, scr):
        for cp in copies(ins, got, scr):
            cp.start()

    def finish(ins, got, scr):
        for cp in copies(ins, got, scr):
            cp.wait()

    return Job(qs, [sds((3,) + q.shape[1:], BF16) for q in qs], {},
               [pltpu.SemaphoreType.DMA((nw, 3)), pltpu.SemaphoreType.DMA((nw, 3))], start, lambda *_: None, finish)


def share_halves_job(gs):
    nw = len(gs)

    def start(_, outs, scr):
        x, y, c, _ = _place()
        for w in range(nw):
            mine = _half(outs[w], gs[w].shape[0], c)
            _remote(mine, mine, scr[0], scr[1], w, (x, y, 1 - c)).start()

    def finish(_, outs, scr):
        x, y, c, _ = _place()
        for w in range(nw):
            mine = _half(outs[w], gs[w].shape[0], c)
            theirs = _half(outs[w], gs[w].shape[0], 1 - c)
            _remote(mine, mine, scr[0], scr[1], w, (x, y, 1 - c)).wait_send()
            _remote(theirs, theirs, scr[0], scr[1], w, (x, y, 1 - c)).wait_recv()

    return Job(gs, [sds(g.shape, F32) for g in gs], {w: w for w in range(nw)},
               [pltpu.SemaphoreType.DMA((nw,)), pltpu.SemaphoreType.DMA((nw,))], start, lambda *_: None, finish)


def rope_tables(t):
    pos = jnp.arange(t, dtype=F32)
    inv_freq = ROPE_THETA ** (-jnp.arange(0, ROT, 2, dtype=F32) / ROT)
    ang = pos[:, None] * inv_freq[None, :]
    cos, sin = jnp.cos(ang), jnp.sin(ang)
    d = jnp.arange(128) % HD
    half = ROT // 2
    cs = jnp.take(cos, d % half, axis=1)
    sn = jnp.take(sin, d % half, axis=1)
    cc = jnp.where(d[None] < ROT, cs, 1.0)
    sa = jnp.where(d[None] < half, -sn, 0.0)
    sb = jnp.where((d[None] >= half) & (d[None] < ROT), sn, 0.0)
    return cc, sa, sb


def _rope(v, cc, sa, sb):
    w = v.shape[1]
    reps = w // 128
    half = ROT // 2
    return (v * jnp.tile(cc, (1, reps)) + pltpu.roll(v, w - half, 1) * jnp.tile(sa, (1, reps))
            + pltpu.roll(v, half, 1) * jnp.tile(sb, (1, reps)))


def _rope_t(dv, cc, sa, sb):
    w = dv.shape[1]
    reps = w // 128
    half = ROT // 2
    return (dv * jnp.tile(cc, (1, reps)) + pltpu.roll(dv * jnp.tile(sa, (1, reps)), half, 1)
            + pltpu.roll(dv * jnp.tile(sb, (1, reps)), w - half, 1))


def pool_fwd(h, b_in, t, tm):
    tm = min(tm, t)
    per = tm // HALO

    def body(prev_ref, cur_ref, b_ref, o_ref, xx):
        i = pl.program_id(0)
        b = b_ref[...]
        xx[pl.ds(0, HALO), :] = jnp.where(i > 0, prev_ref[...] + b, 0.0)
        xx[pl.ds(HALO, tm), :] = cur_ref[...] + b
        tpos = i * tm + lax.broadcasted_iota(jnp.int32, (tm, PG), 0) + 1
        for gi, w in enumerate(POOL_WINDOWS):
            cols = pl.ds(gi * PG, PG)
            acc = xx[pl.ds(HALO, tm), cols]
            for s in range(1, w):
                acc = acc + xx[pl.ds(HALO - s, tm), cols]
            cnt = jnp.minimum(tpos, w).astype(F32)
            o_ref[:, cols] = (acc / cnt - xx[pl.ds(HALO, tm), cols]).astype(o_ref.dtype)

    return pl.pallas_call(
        body, name="pool_fwd", grid=(t // tm,),
        in_specs=[pl.BlockSpec((HALO, PW), lambda i: (jnp.maximum(i * per - 1, 0), 0)),
                  pl.BlockSpec((tm, PW), lambda i: (i, 0)), pl.BlockSpec((1, PW), lambda i: (0, 0))],
        out_specs=pl.BlockSpec((tm, PW), lambda i: (i, 0)), out_shape=sds((t, PW), BF16),
        scratch_shapes=[pltpu.VMEM((tm + HALO, PW), F32)], compiler_params=_params(("arbitrary",)),
    )(h, h, b_in)


def pool_bwd(dpooled, t, tm):
    tm = min(tm, t)
    per = tm // HALO
    nt = t // tm

    def body(cur_ref, nxt_ref, o_ref, db_ref, ee):
        i = pl.program_id(0)
        tpos = i * tm + lax.broadcasted_iota(jnp.int32, (tm, PG), 0) + 1
        for gi, w in enumerate(POOL_WINDOWS):
            cols = pl.ds(gi * PG, PG)
            ee[pl.ds(0, tm), cols] = cur_ref[:, cols] / jnp.minimum(tpos, w).astype(F32)
            ee[pl.ds(tm, HALO), cols] = jnp.where(i < nt - 1, nxt_ref[:, cols] / float(w), 0.0)
        for gi, w in enumerate(POOL_WINDOWS):
            cols = pl.ds(gi * PG, PG)
            acc = ee[pl.ds(0, tm), cols]
            for s in range(1, w):
                acc = acc + ee[pl.ds(s, tm), cols]
            dxp = acc - cur_ref[:, cols]
            o_ref[:, cols] = dxp.astype(o_ref.dtype)
            part = colsum(dxp)

            @pl.when(i == 0)
            def _(cols=cols, part=part):
                db_ref[:, cols] = part

            @pl.when(i > 0)
            def _(cols=cols, part=part):
                db_ref[:, cols] += part

    return pl.pallas_call(
        body, name="pool_bwd", grid=(nt,),
        in_specs=[pl.BlockSpec((tm, PW), lambda i: (i, 0)),
                  pl.BlockSpec((HALO, PW), lambda i: (jnp.minimum((i + 1) * per, t // HALO - 1), 0))],
        out_specs=[pl.BlockSpec((tm, PW), lambda i: (i, 0)), pl.BlockSpec((1, PW), lambda i: (0, 0))],
        out_shape=[sds((t, PW), BF16), sds((1, PW), F32)],
        scratch_shapes=[pltpu.VMEM((tm + HALO, PW), F32)], compiler_params=_params(("arbitrary",)),
    )(dpooled, dpooled)


def _scores(qh, kp, kc, mask_p, mask_c, sink):
    sp = jnp.where(mask_p, lax.dot_general(qh, kp, NT, preferred_element_type=F32), -1e30)
    sc = jnp.where(mask_c, lax.dot_general(qh, kc, NT, preferred_element_type=F32), -1e30)
    m = jnp.maximum(jnp.maximum(jnp.max(sp, axis=-1, keepdims=True), jnp.max(sc, axis=-1, keepdims=True)), sink)
    pp, pc = jnp.exp(sp - m), jnp.exp(sc - m)
    es = jnp.exp(sink - m)
    inv = 1.0 / (jnp.sum(pp, axis=-1, keepdims=True) + jnp.sum(pc, axis=-1, keepdims=True) + es)
    return pp * inv, pc * inv, es * inv


GRP = N_Q // N_KV


def _masks(n):
    qi = lax.broadcasted_iota(jnp.int32, (GRP * BLK, BLK), 0) % BLK
    kj = lax.broadcasted_iota(jnp.int32, (GRP * BLK, BLK), 1)
    return (kj > qi) & (n > 0), kj <= qi


def _head(hk, g):
    return pl.ds(HD * (GRP * hk + g), HD)


def _stack_heads(ref, hk):
    return jnp.concatenate([ref[:, _head(hk, g)] for g in range(GRP)], axis=0)


def _stack_sinks(s_ref, hk):
    return jnp.concatenate([jnp.full((BLK, 1), s_ref[0, GRP * hk + g], F32) for g in range(GRP)], axis=0)


def attn_fwd(q, k, v, sinks, t, job=None):
    def body(s_ref, q_ref, kp_ref, kc_ref, vp_ref, vc_ref, o_ref):
        n = pl.program_id(0)
        mask_p, mask_c = _masks(n)
        for hk in range(N_KV):
            kv = pl.ds(HD * hk, HD)
            pp, pc, _ = _scores(_stack_heads(q_ref, hk), kp_ref[:, kv], kc_ref[:, kv], mask_p, mask_c,
                                _stack_sinks(s_ref, hk))
            o = (lax.dot_general(pp.astype(BF16), vp_ref[:, kv], NN, preferred_element_type=F32)
                 + lax.dot_general(pc.astype(BF16), vc_ref[:, kv], NN, preferred_element_type=F32))
            for g in range(GRP):
                o_ref[:, _head(hk, g)] = o[g * BLK:(g + 1) * BLK].astype(o_ref.dtype)

    prev = lambda n: (jnp.maximum(n - 1, 0), 0)
    cur = lambda n: (n, 0)
    res, moved = carried_call(
        body, "attn_fwd", (t // BLK,),
        [pl.BlockSpec(memory_space=pltpu.SMEM), pl.BlockSpec((BLK, QW), cur),
         pl.BlockSpec((BLK, KVW), prev), pl.BlockSpec((BLK, KVW), cur),
         pl.BlockSpec((BLK, KVW), prev), pl.BlockSpec((BLK, KVW), cur)],
        [pl.BlockSpec((BLK, QW), cur)], [sds((t, QW), BF16)], [], [sinks, q, k, k, v, v], {}, job)
    return res[0], moved


def attn_bwd(q, k, v, do, sinks, t):
    nb = t // BLK

    def body(s_ref, q_ref, do_ref, kp_ref, kc_ref, vp_ref, vc_ref, dq_ref, dk_ref, dv_ref, ds_ref, dkc, dvc):
        n = pl.program_id(0)

        @pl.when(n == 0)
        def _():
            dkc[...] = jnp.zeros_like(dkc)
            dvc[...] = jnp.zeros_like(dvc)
            ds_ref[...] = jnp.zeros_like(ds_ref)

        @pl.when(n < nb)
        def _():
            mask_p, mask_c = _masks(n)
            lane = lax.broadcasted_iota(jnp.int32, (1, 128), 1)
            dsink = jnp.zeros((1, 128), F32)
            for hk in range(N_KV):
                kv = pl.ds(HD * hk, HD)
                kp, kc, vp, vc = kp_ref[:, kv], kc_ref[:, kv], vp_ref[:, kv], vc_ref[:, kv]
                qs, dos = _stack_heads(q_ref, hk), _stack_heads(do_ref, hk)
                pp, pc, ps = _scores(qs, kp, kc, mask_p, mask_c, _stack_sinks(s_ref, hk))
                dpp = lax.dot_general(dos, vp, NT, preferred_element_type=F32)
                dpc = lax.dot_general(dos, vc, NT, preferred_element_type=F32)
                delta = jnp.sum(pp * dpp, axis=-1, keepdims=True) + jnp.sum(pc * dpc, axis=-1, keepdims=True)
                dsp = (pp * (dpp - delta)).astype(BF16)
                dsc = (pc * (dpc - delta)).astype(BF16)
                sd = ps * delta
                dq = (lax.dot_general(dsp, kp, NN, preferred_element_type=F32)
                      + lax.dot_general(dsc, kc, NN, preferred_element_type=F32))
                for g in range(GRP):
                    rows = slice(g * BLK, (g + 1) * BLK)
                    dsink = dsink + jnp.where(lane == GRP * hk + g, -jnp.sum(sd[rows]), 0.0)
                    dq_ref[:, _head(hk, g)] = dq[rows]
                dk_ref[:, kv] = dkc[:, kv] + lax.dot_general(dsp, qs, TN, preferred_element_type=F32)
                dv_ref[:, kv] = dvc[:, kv] + lax.dot_general(pp.astype(BF16), dos, TN, preferred_element_type=F32)
                dkc[:, kv] = lax.dot_general(dsc, qs, TN, preferred_element_type=F32)
                dvc[:, kv] = lax.dot_general(pc.astype(BF16), dos, TN, preferred_element_type=F32)
            ds_ref[...] += dsink

        @pl.when(n == nb)
        def _():
            dk_ref[...] = dkc[...]
            dv_ref[...] = dvc[...]

    cur = lambda n: (jnp.minimum(n, nb - 1), 0)
    prev = lambda n: (jnp.clip(n - 1, 0, nb - 1), 0)
    return pl.pallas_call(
        body, name="attn_bwd", grid=(nb + 1,),
        in_specs=[pl.BlockSpec(memory_space=pltpu.SMEM), pl.BlockSpec((BLK, QW), cur), pl.BlockSpec((BLK, QW), cur),
                  pl.BlockSpec((BLK, KVW), prev), pl.BlockSpec((BLK, KVW), cur),
                  pl.BlockSpec((BLK, KVW), prev), pl.BlockSpec((BLK, KVW), cur)],
        out_specs=[pl.BlockSpec((BLK, QW), cur), pl.BlockSpec((BLK, KVW), prev), pl.BlockSpec((BLK, KVW), prev),
                   pl.BlockSpec((1, 128), lambda n: (0, 0))],
        out_shape=[sds((t, QW), F32), sds((t, KVW), F32), sds((t, KVW), F32), sds((1, 128), F32)],
        scratch_shapes=[pltpu.VMEM((BLK, KVW), F32), pltpu.VMEM((BLK, KVW), F32)],
        compiler_params=_params(("arbitrary",)),
    )(sinks, q, do, k, k, v, v)


def _adamw(w, g, m, v):
    m2 = B1 * m + (1.0 - B1) * g
    v2 = B2 * v + (1.0 - B2) * jnp.square(g)
    m_hat = m2 / (1.0 - B1 ** STEP)
    v_hat = v2 / (1.0 - B2 ** STEP)
    return -LR * (m_hat / (jnp.sqrt(v_hat) + EPS) + WD * w), m2, v2


def ada_fwd(c16, w_ada, b_sh):
    tn = 512

    def body(c_ref, w_ref, b_ref, o_ref):
        cv = c_ref[...]
        sc = (cv * _sigmoid(cv)).astype(BF16)
        o_ref[...] = lax.dot_general(sc, w_ref[...].astype(BF16), NN, preferred_element_type=F32) + b_ref[...]

    return pl.pallas_call(
        body, name="ada_fwd", grid=(ADA_SH // tn,),
        in_specs=[pl.BlockSpec((16, D), lambda j: (0, 0)), pl.BlockSpec((D, tn), lambda j: (0, j)),
                  pl.BlockSpec((1, tn), lambda j: (0, j))],
        out_specs=pl.BlockSpec((16, tn), lambda j: (0, j)), out_shape=sds((16, ADA_SH), F32),
        compiler_params=_params(("arbitrary",)),
    )(c16, w_ada, b_sh)


def ada_bwd_adam(c16, gm16, w, m, v, job):
    tm, tn = 512, ADA_SH // 4

    def body(c_ref, g_ref, w_ref, m_ref, v_ref, go_ref, d_ref, mo_ref, vo_ref):
        cv = c_ref[...]
        sc = (cv * _sigmoid(cv)).astype(BF16)
        g = lax.dot_general(sc, g_ref[...].astype(BF16), TN, preferred_element_type=F32)
        dl, m2, v2 = _adamw(w_ref[...], g, m_ref[...], v_ref[...])
        go_ref[...] = g
        d_ref[...] = dl
        mo_ref[...] = m2
        vo_ref[...] = v2

    blk = pl.BlockSpec((tm, tn), lambda i, j: (i, j))
    return carried_call(
        body, "ada_bwd_adam", (D // tm, ADA_SH // tn),
        [pl.BlockSpec((16, tm), lambda i, j: (0, i)), pl.BlockSpec((16, tn), lambda i, j: (0, j)), blk, blk, blk],
        [blk] * 4, [sds((D, ADA_SH), F32)] * 4, [], [c16, gm16, w, m, v], {}, job)


def adam_rows(name, w, g, m, v, tm):
    rows, cols = w.shape

    def fn(wv, gv, mv, vv):
        gv = gv[:, :cols]
        dl, m2, v2 = _adamw(wv, gv, mv, vv)
        return gv, dl, m2, v2

    return rowmap(name, fn, [T_(w), T_(g), T_(m), T_(v)], [(cols, F32)] * 4, rows=rows, tm=tm)


def adam_small(name, w, g, m, v):
    def body(w_ref, g_ref, m_ref, v_ref, d_ref, mo_ref, vo_ref):
        dl, m2, v2 = _adamw(w_ref[...], g_ref[...], m_ref[...], v_ref[...])
        d_ref[...] = dl
        mo_ref[...] = m2
        vo_ref[...] = v2

    return pl.pallas_call(body, name=name, out_shape=[sds(w.shape, F32)] * 3)(w, g, m, v)


def sum_devices(allv):
    def body(a_ref, o_ref):
        acc = a_ref[0]
        for d in range(1, 8):
            acc = acc + a_ref[d]
        o_ref[...] = acc

    return pl.pallas_call(body, name="sum_devices", out_shape=sds(allv.shape[1:], F32))(allv)


def _ln_fwd(z, g, b):
    mu = jnp.mean(z, axis=-1, keepdims=True)
    zc = z - mu
    var = jnp.mean(jnp.square(zc), axis=-1, keepdims=True)
    return zc * lax.rsqrt(var + LN_EPS) * g + b


def _ln_bwd(z, g, dout):
    mu = jnp.mean(z, axis=-1, keepdims=True)
    zc = z - mu
    var = jnp.mean(jnp.square(zc), axis=-1, keepdims=True)
    rstd = lax.rsqrt(var + LN_EPS)
    xh = zc * rstd
    dxh = dout * g
    dz = rstd * (dxh - jnp.mean(dxh, axis=-1, keepdims=True) - xh * jnp.mean(dxh * xh, axis=-1, keepdims=True))
    return dz, colsum(dout * xh), colsum(dout)


def modulate(name, xin, shift, scale, t):
    return rowmap(name, lambda xv, sh, sc: xv * (1.0 + sc) + sh, [T_(xin), B_(shift), B_(scale)], [(D, BF16)],
                  rows=t, tm=512)


def residual_ln_mod(name, xin, y, gate, lg, lb, wgt, shift_n, scale_n, t):
    def fn(xv, yv, gt, g, b, sh, sc):
        z = ALPHA * xv + (wgt * (1.0 + gt)) * yv
        xo = _ln_fwd(z, g, b)
        return xo, z, xo * (1.0 + sc) + sh

    return rowmap(name, fn, [T_(xin), T_(y), B_(gate), B_(lg), B_(lb), B_(shift_n), B_(scale_n)],
                  [(D, F32), (D, F32), (D, BF16)], rows=t, tm=256)


def residual_ln_bwd(name, z, dnext, y, gate, lg, wgt, t):
    dzn, dun, xn, scn = dnext

    def fn(zv, yv, gt, g, dzv, duv, xv, sc):
        dv = ALPHA * dzv + duv * (1.0 + sc)
        dz, dg, db = _ln_bwd(zv, g, dv)
        return dz, (wgt * (1.0 + gt)) * dz, dg, db, colsum(wgt * dz * yv), colsum(duv), colsum(duv * xv)

    return rowmap(name, fn, [T_(z), T_(y), B_(gate), B_(lg), T_(dzn), T_(dun), T_(xn), B_(scn)],
                  [(D, F32), (D, BF16)], [(1, D)] * 5, rows=t, tm=256)


def residual_ln_loss_bwd(name, xin, y, tgt, gate, lg, lb, wgt, t):
    def fn(xv, yv, tv, gt, g, b):
        z = ALPHA * xv + (wgt * (1.0 + gt)) * yv
        d = _ln_fwd(z, g, b) - tv
        dz, dg, db = _ln_bwd(z, g, d * (1.0 / D))
        return dz, (wgt * (1.0 + gt)) * dz, dg, db, colsum(wgt * dz * yv), jnp.sum(d * d).reshape(1, 1)

    dz, dy, dlg, dlb, dgate, sq = rowmap(
        name, fn, [T_(xin), T_(y), T_(tgt), B_(gate), B_(lg), B_(lb)], [(D, F32), (D, BF16)],
        [(1, D), (1, D), (1, D), (1, 1)], rows=t, tm=256)
    return dz, dy, dlg, dlb, dgate, sq


def modulate_bwd(name, dz, du, xin, scale, t):
    def fn(dzv, duv, xv, sc):
        return ALPHA * dzv + duv * (1.0 + sc), colsum(duv), colsum(duv * xv)

    return rowmap(name, fn, [T_(dz), T_(du), T_(xin), B_(scale)], [(D, F32)], [(1, D), (1, D)], rows=t, tm=256)


def ffn_fwd(tag, u, wi, t, up_job, down_job=None):
    tm = min(1024, t)
    tn = 256
    per = FHP // tn

    def act(accs, _):
        a, b = accs
        s = _sigmoid(a)
        sl = a * s
        return b * (s * (1.0 + a * (1.0 - s))), sl, sl * b

    tmu = min(2048, t)
    hblk = pl.BlockSpec((tmu, tn), lambda i, j, k: (i, j))
    (ha, hb, g), up_moved = mm(
        tag + "_up", [u], [wi, wi], [(0, 0, 0), (0, 1, 1)], dims=NT, grid=(t // tmu, 2 * per, 1),
        a_specs=[pl.BlockSpec((tmu, D), lambda i, j, k: (i, 0))],
        b_specs=[pl.BlockSpec((None, tn, D), lambda i, j, k: (j // per, j % per, 0)),
                 pl.BlockSpec((None, tn, D), lambda i, j, k: (2 + j // per, j % per, 0))],
        outs=[sds((t, 2 * FHP), BF16)] * 3, out_specs=[hblk] * 3, acc_shapes=[(tmu, tn)] * 2, epilogue=act, job=up_job,
        sub_rows=tmu // 2)
    wo = up_moved[0].reshape(2 * FHP, D)
    tk = FHP
    y, down_moved = _with_moved(mm(
        tag + "_down", [g], [wo], [(0, 0, 0)], dims=NN, grid=(t // tm, 2, 2),
        a_specs=[pl.BlockSpec((tm, tk), lambda i, j, k: (i, k))],
        b_specs=[pl.BlockSpec((tk, D // 2), lambda i, j, k: (k, j))],
        outs=[sds((t, D), F32)], out_specs=[pl.BlockSpec((tm, D // 2), lambda i, j, k: (i, j))],
        acc_shapes=[(tm, D // 2)], job=down_job), down_job)
    return ha, hb, g, y, wo, up_moved, down_moved


def ffn_bwd(tag, u, ha, hb, g, dy, wi, wo, t, sp, dact_job=None, dwo_job=None, du_extra=None):
    tm = min(1024, t)

    def dact(accs, ex):
        dg = accs[0]
        return dg * ex[0].astype(F32), dg * ex[1].astype(F32)

    tn = 256
    tmu = min(2048, t)
    hblk = pl.BlockSpec((tmu, tn), lambda i, j, k: (i, j))
    (dha, dhb), dact_moved = _with_moved(mm(
        tag + "_dact", [dy], [wo], [(0, 0, 0)], dims=NT, grid=(t // tmu, 2 * FHP // tn, 1),
        a_specs=[pl.BlockSpec((tmu, D), lambda i, j, k: (i, 0))],
        b_specs=[pl.BlockSpec((tn, D), lambda i, j, k: (j, 0))],
        outs=[sds((t, 2 * FHP), BF16)] * 2, out_specs=[hblk] * 2, acc_shapes=[(tmu, tn)],
        epilogue=dact, extras=[ha, hb], extra_specs=[hblk] * 2, job=dact_job, sub_rows=tmu // 2), dact_job)
    tk = min(2048, t)
    th = FHP // 2
    dwo, dwo_moved = _with_moved(mm(
        tag + "_dwo", [g], [dy], [(0, 0, 0)], dims=TN, grid=(4, 2, t // tk),
        a_specs=[pl.BlockSpec((tk, th), lambda i, j, k: (k, i))],
        b_specs=[pl.BlockSpec((tk, D // 2), lambda i, j, k: (k, j))],
        outs=[sds((2 * FHP, D), BF16)], out_specs=[pl.BlockSpec((th, D // 2), lambda i, j, k: (i, j))],
        acc_shapes=[(th, D // 2)], job=dwo_job), dwo_job)
    dwo = dwo.reshape(2, FHP, D)

    def dwi_part(part, dh, carry, job):
        return mm(
            f"{tag}_dwi{part}", [dh], [u], [(0, 0, 0)], dims=TN, grid=(4, 2, t // tk),
            a_specs=[pl.BlockSpec((tk, th), lambda i, j, k: (k, i))],
            b_specs=[pl.BlockSpec((tk, D // 2), lambda i, j, k: (k, j))],
            outs=[sds((4, FHP, D), BF16)],
            out_specs=[pl.BlockSpec((None, th, D // 2), lambda i, j, k: (2 * part + i // 2, i % 2, j))],
            acc_shapes=[(th, D // 2)], carry=carry, job=job)

    dwi, (sib_fo,) = dwi_part(0, dha, None, reduce_sibling_job([(dwo, view_ffn_out, FO, D)]))
    q_fo = chip_sum(tag + "_chipsum_fo", dwo, sib_fo, sp, FO, FO // 2, ffn_out=True)
    dwi, (far_fo,) = dwi_part(1, dhb, dwi, reduce_chips_job([q_fo]))
    (sib_fi,) = run_job(tag + "_sibling_fi", reduce_sibling_job([(dwi, view_lead, FHP, D)]))
    q_fi = chip_sum(tag + "_chipsum_fi", dwi, sib_fi, sp, FHP, FHP // 8)
    tmd = min(512, t)
    job = reduce_chips_job([q_fi])
    if du_extra is not None:
        job = merge_jobs(job, du_extra(dact_moved, dwo_moved, q_fo, far_fo))
    du, (far_fi, *extra_moved) = mm(
        tag + "_du", [dha, dhb], [wi, wi], [(0, 0, 0), (1, 1, 0)], dims=NN, grid=(t // tmd, 2, 2),
        a_specs=[pl.BlockSpec((tmd, FHP), lambda i, j, k: (i, k))] * 2,
        b_specs=[pl.BlockSpec((None, FHP, D // 2), lambda i, j, k: (k, 0, j)),
                 pl.BlockSpec((None, FHP, D // 2), lambda i, j, k: (2 + k, 0, j))],
        outs=[sds((t, D), F32)], out_specs=[pl.BlockSpec((tmd, D // 2), lambda i, j, k: (i, j))],
        acc_shapes=[(tmd, D // 2)], job=job)
    return du, (q_fi, far_fi), (q_fo, far_fo), extra_moved


def mix_fwd(u, wts, b_in, pool_scale, sinks, tabs, t, in_job, attn_job):
    w_in, wp, wba, wbb, wo = wts
    tm = min(1024, t)
    tmh = min(512, t)
    h, in_moved = mm("mix_in", [u], [w_in], [(0, 0, 0)], dims=NN, grid=(t // tmh, 4, 1),
                     a_specs=[pl.BlockSpec((tmh, D), lambda i, j, k: (i, 0))],
                     b_specs=[pl.BlockSpec((None, D, IN_SH), lambda i, j, k: (j, 0, 0))],
                     outs=[sds((t, IN_W), F32)], out_specs=[pl.BlockSpec((tmh, IN_SH), lambda i, j, k: (i, j))],
                     acc_shapes=[(tmh, IN_SH)], job=in_job)
    attn_job = attn_job(in_moved)
    pooled = pool_fwd(h, b_in, t, 512)
    gblk = pl.BlockSpec((tm, PG), lambda i, j, k: (i, j))
    mixed = mm("mix_pool", [pooled], [wp], [(0, 0, 0)], dims=NN, grid=(t // tm, 4, 1), a_specs=[gblk],
               b_specs=[pl.BlockSpec((None, PG, PG), lambda i, j, k: (j, 0, 0))],
               outs=[sds((t, PW), F32)], out_specs=[gblk], acc_shapes=[(tm, PG)])
    pm = rowmap("mix_pscale", lambda mv, ps: mv * ps, [T_(mixed), B_(pool_scale)], [(PW, BF16)], rows=t, tm=512)

    def branch(name, a, w):
        return mm(name, [a], [w], [(0, 0, 0)], dims=NN, grid=(t // tm, 4, 1),
                  a_specs=[pl.BlockSpec((tm, PW), lambda i, j, k: (i, 0))],
                  b_specs=[pl.BlockSpec((None, PW, D // 4), lambda i, j, k: (j, 0, 0))],
                  outs=[sds((t, D), F32)], out_specs=[pl.BlockSpec((tm, D // 4), lambda i, j, k: (i, j))],
                  acc_shapes=[(tm, D // 4)])

    ya = branch("mix_branch_a", pm, wba)

    def qkv(hq, hk, hv, bq, bk, bv, cc, sa, sb):
        return (_rope(hq + bq, cc, sa, sb) * (HD ** -0.5), _rope(hk + bk, cc, sa, sb), hv + bv)

    qr, kr, vv = rowmap(
        "mix_rope", qkv,
        [T_(h, QW, 1), T_(h, KVW, 8), T_(h, KVW, 9), B_(b_in, QW, 1), B_(b_in, KVW, 8), B_(b_in, KVW, 9),
         T_(tabs[0]), T_(tabs[1]), T_(tabs[2])],
        [(QW, BF16), (KVW, BF16), (KVW, BF16)], rows=t, tm=512)
    attn, attn_moved = attn_fwd(qr, kr, vv, sinks, t, attn_job)
    yb = branch("mix_branch_b", attn, wbb)
    cw = 512

    def merge(ga, gb, ba, bb, yav, ybv):
        return _sigmoid(ga + ba) * yav + _sigmoid(gb + bb) * ybv

    merged = rowmap(
        "mix_merge", merge,
        [T_(h, cw, 5), T_(h, cw, 9), B_(b_in, cw, 5), B_(b_in, cw, 9), T_(ya, cw), T_(yb, cw)],
        [(D, BF16)], rows=t, tm=512, ncol=D // cw)
    y = mm("mix_out", [merged], [wo], [(0, 0, 0)], dims=NN, grid=(t // tm, 2, 1),
           a_specs=[pl.BlockSpec((tm, D), lambda i, j, k: (i, 0))],
           b_specs=[pl.BlockSpec((D, D // 2), lambda i, j, k: (0, j))],
           outs=[sds((t, D), F32)], out_specs=[pl.BlockSpec((tm, D // 2), lambda i, j, k: (i, j))],
           acc_shapes=[(tm, D // 2)])
    return y, (h, pooled, mixed, pm, ya, qr, kr, vv, attn, yb, merged), attn_moved


def mix_bwd(u, saved, dy, wts, b_in, pool_scale, sinks, tabs, t):
    h, pooled, mixed, pm, ya, qr, kr, vv, attn, yb, merged = saved
    w_in, wp, wba, wbb, wo = wts
    tm = min(1024, t)
    tk = min(2048, t)
    dmerged = mm("mix_dmerged", [dy], [wo], [(0, 0, 0)], dims=NT, grid=(t // tm, 2, 1),
                 a_specs=[pl.BlockSpec((tm, D), lambda i, j, k: (i, 0))],
                 b_specs=[pl.BlockSpec((D // 2, D), lambda i, j, k: (j, 0))],
                 outs=[sds((t, D), F32)], out_specs=[pl.BlockSpec((tm, D // 2), lambda i, j, k: (i, j))],
                 acc_shapes=[(tm, D // 2)])
    half = pl.BlockSpec((tk, D // 2), lambda i, j, k: (k, i))
    dwo = mm("mix_dwo", [merged], [dy], [(0, 0, 0)], dims=TN, grid=(2, 2, t // tk), a_specs=[half],
             b_specs=[pl.BlockSpec((tk, D // 2), lambda i, j, k: (k, j))],
             outs=[sds((D, D), BF16)], out_specs=[pl.BlockSpec((D // 2, D // 2), lambda i, j, k: (i, j))],
             acc_shapes=[(D // 2, D // 2)])
    cw = 512

    def dmerge(dm, ga, gb, ba, bb, yav, ybv):
        sa_, sb_ = _sigmoid(ga + ba), _sigmoid(gb + bb)
        dga = dm * yav * sa_ * (1.0 - sa_)
        dgb = dm * ybv * sb_ * (1.0 - sb_)
        return dm * sa_, dm * sb_, dga, dgb, colsum(dga), colsum(dgb)

    dya, dyb, dgla, dglb, dbga, dbgb = rowmap(
        "mix_dmerge", dmerge,
        [T_(dmerged, cw), T_(h, cw, 5), T_(h, cw, 9), B_(b_in, cw, 5), B_(b_in, cw, 9), T_(ya, cw), T_(yb, cw)],
        [(D, BF16)] * 4, [(1, D), (1, D)], rows=t, tm=512, ncol=D // cw)

    def dbranch(name, dyv, act, w):
        dwb = mm(name + "_dw", [act], [dyv], [(0, 0, 0)], dims=TN, grid=(1, 4, t // tk),
                 a_specs=[pl.BlockSpec((tk, PW), lambda i, j, k: (k, 0))],
                 b_specs=[pl.BlockSpec((tk, D // 4), lambda i, j, k: (k, j))],
                 outs=[sds((4, PW, D // 4), BF16)], out_specs=[pl.BlockSpec((None, PW, D // 4), lambda i, j, k: (j, 0, 0))],
                 acc_shapes=[(PW, D // 4)])
        return dwb, lambda dt: mm(
            name + "_dx", [dyv], [w], [(0, 0, 0)], dims=NT, grid=(t // tm, 1, 4),
            a_specs=[pl.BlockSpec((tm, D // 4), lambda i, j, k: (i, k))],
            b_specs=[pl.BlockSpec((None, PW, D // 4), lambda i, j, k: (k, 0, 0))],
            outs=[sds((t, PW), dt)], out_specs=[pl.BlockSpec((tm, PW), lambda i, j, k: (i, 0))], acc_shapes=[(tm, PW)])

    dwba, dpm_fn = dbranch("mix_dbranch_a", dya, pm, wba)
    dwbb, dattn_fn = dbranch("mix_dbranch_b", dyb, attn, wbb)
    dpm, dattn = dpm_fn(F32), dattn_fn(BF16)
    dmixed, dps = rowmap("mix_dpscale", lambda dp, mv, ps: (dp * ps, colsum(dp * mv)),
                         [T_(dpm), T_(mixed), B_(pool_scale)], [(PW, BF16)], [(1, PW)], rows=t, tm=512)
    gblk = pl.BlockSpec((tm, PG), lambda i, j, k: (i, j))
    dpooled = mm("mix_dpool", [dmixed], [wp], [(0, 0, 0)], dims=NT, grid=(t // tm, 4, 1), a_specs=[gblk],
                 b_specs=[pl.BlockSpec((None, PG, PG), lambda i, j, k: (j, 0, 0))],
                 outs=[sds((t, PW), F32)], out_specs=[gblk], acc_shapes=[(tm, PG)])
    kblk = pl.BlockSpec((tk, PG), lambda i, j, k: (k, i))
    dwp = mm("mix_dwpool", [pooled], [dmixed], [(0, 0, 0)], dims=TN, grid=(4, 1, t // tk), a_specs=[kblk], b_specs=[kblk],
             outs=[sds((4, PG, PG), BF16)], out_specs=[pl.BlockSpec((None, PG, PG), lambda i, j, k: (i, 0, 0))],
             acc_shapes=[(PG, PG)])
    dxp, dbxp = pool_bwd(dpooled, t, 512)
    dqr, dkr, dvv, dsinks = attn_bwd(qr, kr, vv, dattn, sinks, t)

    def dqkv(dq, dk, dv, cc, sa, sb):
        dq = _rope_t(dq, cc, sa, sb) * (HD ** -0.5)
        dk = _rope_t(dk, cc, sa, sb)
        return dq, dk, dv, colsum(dq), colsum(dk), colsum(dv)

    dq, dk, dvb, dbq, dbk, dbv = rowmap(
        "mix_rope_bwd", dqkv, [T_(dqr), T_(dkr), T_(dvv), T_(tabs[0]), T_(tabs[1]), T_(tabs[2])],
        [(QW, BF16), (KVW, BF16), (KVW, BF16)], [(1, QW), (1, KVW), (1, KVW)], rows=t, tm=512)
    dh = jnp.concatenate([dxp, dq, dk, dvb, dgla, dglb], axis=1)
    db_in = jnp.concatenate([dbxp, dbq, dbk, dbv, dbga, dbgb], axis=1)
    dwin = mm("mix_dwin", [u], [dh], [(0, 0, 0)], dims=TN, grid=(2, 4, t // tk), a_specs=[half],
              b_specs=[pl.BlockSpec((tk, IN_SH), lambda i, j, k: (k, j))],
              outs=[sds((4, D, IN_SH), BF16)], out_specs=[pl.BlockSpec((None, D // 2, IN_SH), lambda i, j, k: (j, i, 0))],
              acc_shapes=[(D // 2, IN_SH)])
    dwp_sh = jnp.transpose(dwp.reshape(4, 4, 64, PG), (1, 0, 2, 3)).reshape(4, 4 * 64, PG)
    parts = {"win": dwin, "wp": dwp_sh, "wba": dwba, "wbb": dwbb, "wo": dwo.reshape(4, D // 4, D)}
    du, sib = mm("mix_du", [dh], [w_in], [(0, 0, 0)], dims=NT, grid=(t // tm, 2, 4),
                 a_specs=[pl.BlockSpec((tm, IN_SH), lambda i, j, k: (i, k))],
                 b_specs=[pl.BlockSpec((None, D // 2, IN_SH), lambda i, j, k: (k, j, 0))],
                 outs=[sds((t, D), F32)], out_specs=[pl.BlockSpec((tm, D // 2), lambda i, j, k: (i, j))],
                 acc_shapes=[(tm, D // 2)],
                 job=reduce_sibling_job([(p, view_lead, p.shape[1], p.shape[2]) for p in parts.values()]))
    return du, parts, dict(zip(parts, sib)), db_in, dps, dsinks


def cast_shard(name, w, sp, ffn_out=False):
    rows, cols = w.shape
    if ffn_out:
        tm = rows // 2
        shape = (2, FHP, D)
        spec = pl.BlockSpec((None, tm, cols), lambda j, i, s: (s[0] // 2, (s[0] % 2) * 2 + i, 0))
    else:
        tm = rows // 4
        shape = (4, rows, cols)
        spec = pl.BlockSpec((None, tm, cols), lambda j, i, s: (s[0], i, 0))
    return rowmap(name, lambda wv: wv, [T_(w)], [(shape, BF16, spec)], rows=rows, tm=tm, sp=sp)


def cast_ffn_in(name, wt, sp):
    tm = FH // 4
    buf = rowmap(name, lambda wv: wv, [T_(wt)],
                 [((4, FHP, D), BF16, pl.BlockSpec((None, tm, D), lambda j, i, s: (s[0], i, 0)))], rows=FH, tm=tm, sp=sp)
    pad = FHP - FH

    def zero_pad(_, __, out):
        out[...] = jnp.zeros_like(out)

    return pl.pallas_call(
        zero_pad, name=name + "_pad", out_shape=sds(buf.shape, BF16), input_output_aliases={1: 0},
        grid_spec=pltpu.PrefetchScalarGridSpec(
            num_scalar_prefetch=1, grid=(1,), in_specs=[ANY],
            out_specs=pl.BlockSpec((None, pad, D), lambda i, s: (s[0], FH // pad, 0))))(sp, buf)


def chip_sum(name, dw, got, sp, rows, tm, ffn_out=False):
    hr, cols = rows // 2, got.shape[2]
    per = hr // tm
    pos = pl.BlockSpec((None, tm, cols), lambda j, i, s: (i // per, i % per, 0))
    if ffn_out:
        mine = pl.BlockSpec((None, tm, cols), lambda j, i, s: (i // 2, (i % 2) * 2 + s[1], 0))
    else:
        mine = pl.BlockSpec((None, tm, cols), lambda j, i, s: (i // per, s[1] * per + i % per, 0))
    return rowmap(name, lambda av, bv: av.astype(F32) + bv.astype(F32), [X_(dw, mine), X_(got, pos)],
                  [(got.shape, BF16, pos)], rows=4 * hr, tm=tm, sp=sp)


def chip_total(name, q, got, sp, rows, tm):
    hr, cols = rows // 2, q.shape[2]
    per = hr // tm

    def part(f):
        return X_(got, pl.BlockSpec((None, tm, cols), lambda j, i, s, f=f: (f, i, 0)))

    return rowmap(
        name, lambda av, b0, b1, b2: ((av.astype(F32) + b0.astype(F32)) + b1.astype(F32)) + b2.astype(F32),
        [X_(q, pl.BlockSpec((None, tm, cols), lambda j, i, s: (s[0], i, 0))), part(0), part(1), part(2)],
        [((rows, cols), F32, pl.BlockSpec((tm, cols), lambda j, i, s: (s[1] * per + i, 0)))], rows=hr, tm=tm, sp=sp)


def kernel(x, c, w_ada, b_ada, ln_g, ln_b, w_ffn1_in, w_ffn1_out, w_in, b_in, w_pool, pool_scale, sinks, w_branch_a, w_branch_b, w_out, w_ffn2_in, w_ffn2_out, loss_target, m_w_ada, m_b_ada, m_ln_g, m_ln_b, m_w_ffn1_in, m_w_ffn1_out, m_w_in, m_b_in, m_w_pool, m_pool_scale, m_sinks, m_w_branch_a, m_w_branch_b, m_w_out, m_w_ffn2_in, m_w_ffn2_out, v_w_ada, v_b_ada, v_ln_g, v_ln_b, v_w_ffn1_in, v_w_ffn1_out, v_w_in, v_b_in, v_w_pool, v_pool_scale, v_sinks, v_w_branch_a, v_w_branch_b, v_w_out, v_w_ffn2_in, v_w_ffn2_out):
    t = x.shape[1]
    xs, tgt = x[0], loss_target[0]
    xi, yi, ci = lax.axis_index("x"), lax.axis_index("y"), lax.axis_index("c")
    chip = 2 * xi + yi
    dev = 2 * chip + ci
    b_in2, ps2, sinks2 = b_in, pool_scale, sinks

    sp = jnp.stack([chip, ci]).astype(jnp.int32)
    tr = lambda a: jnp.swapaxes(a[0], 0, 1)

    first = jnp.concatenate([c.reshape(-1), ln_g.reshape(-1), ln_b.reshape(-1)]).reshape(-1, 128)
    first_all = allgather_small("gather_cond", first).reshape(8, -1)
    c_all = first_all[:, :D]
    ln_parts = first_all[0::2, D:].reshape(4, 2, 3, D // 4)
    ln_full = jnp.transpose(ln_parts, (1, 2, 0, 3)).reshape(2, 3, D)
    lgs = [ln_full[0, s:s + 1] for s in range(3)]
    lbs = [ln_full[1, s:s + 1] for s in range(3)]
    c16 = jnp.pad(c_all, ((0, 8), (0, 0)))
    b_ada_sh = lax.dynamic_slice(b_ada, (0, chip * ADA_SH), (1, ADA_SH))
    mod_part = ada_fwd(c16, w_ada[0], b_ada_sh)[:8]
    mod_all = allgather_small("gather_mod", mod_part.reshape(-1, 128)).reshape(8, 8, ADA_SH)
    mod_mine = lax.dynamic_index_in_dim(mod_all[0::2], dev, axis=1, keepdims=False).reshape(9, D)
    mods = [[mod_mine[3 * s + k:3 * s + k + 1] for k in range(3)] for s in range(3)]

    f1i_buf, f1i_send, f1i_recv, _ = gather_start(
        "gather_f1i_start", cast_ffn_in("cast_f1i", tr(w_ffn1_in), sp), FHP, [mod_mine])
    plain = [("f1o", w_ffn1_out[0]), ("win", w_in[0]), ("wp", w_pool[0].reshape(4 * 64, PG)), ("wba", w_branch_a[0]),
             ("wbb", w_branch_b[0]), ("wo", w_out[0]), ("f2o", w_ffn2_out[0])]
    sh = {n: cast_shard("cast_" + n, w, sp, ffn_out=n in ("f1o", "f2o")) for n, w in plain}
    sh["f1i"] = f1i_buf
    sh["f2i"] = cast_ffn_in("cast_f2i", tr(w_ffn2_in), sp)
    order = ["f1i", "f1o", "win", "wp", "wba", "wbb", "wo", "f2i", "f2o"]
    views = {n: (view_ffn_out if n in ("f1o", "f2o") else view_lead) for n in order}
    shard_rows = {n: (FO if n in ("f1o", "f2o") else sh[n].shape[1]) for n in order}
    shard_cols = {n: sh[n].shape[2] for n in order}
    tiles = {"f1i": FHP // 8, "f1o": FO // 2, "win": 512, "wp": 128, "wba": 512, "wbb": 512, "wo": 256,
             "f2i": FHP // 8, "f2o": FO // 2}

    def item(n, part=0, parts=1):
        return (sh[n], views[n], shard_rows[n], part, parts)

    tabs = rope_tables(t)
    (sh0, sc0, gt0), (sh1, sc1, gt1), (sh2, sc2, gt2) = mods

    u0 = modulate("ffn1_mod", xs, sh0, sc0, t)
    landed = gather_wait("gather_f1i_wait", f1i_buf, f1i_send, f1i_recv, FHP,
                         [u0] + [sh[n] for n in order if n != "f1i"])
    (g_f1i,) = run_job("gather_f1i_forward", forward_job(landed, FHP))
    ha1, hb1, g1, y1, f1o, (_, g_win), (g_wp, g_wba, g_wbb, g_wo) = ffn_fwd(
        "ffn1", u0, g_f1i, t, gather_job([item("f1o"), item("win")]),
        gather_job([item(n) for n in ("wp", "wba", "wbb", "wo")]))
    x1, z1, u1 = residual_ln_mod("ffn1_ln", xs, y1, gt0, lgs[0], lbs[0], 0.5, sh1, sc1, t)
    wp_full = jnp.transpose(g_wp.reshape(4, 4, 64, PG), (1, 0, 2, 3)).reshape(4, PG, PG)
    wts = (g_win, wp_full, g_wba, g_wbb, g_wo.reshape(D, D))
    y2, sv2, (g_f2i,) = mix_fwd(
        u1, wts, b_in2, ps2, sinks2, tabs, t, gather_job([item("f2i", 0, 2)]),
        lambda moved: gather_job([(moved[0], view_lead, FHP, 1, 2)]))
    x2, z2, u2 = residual_ln_mod("mix_ln", x1, y2, gt1, lgs[1], lbs[1], 1.0, sh2, sc2, t)
    ha3, hb3, g3, y3, f2o, _, _ = ffn_fwd("ffn2", u2, g_f2i, t, gather_job([item("f2o")]))

    dz3, dy3, dlg2, dlb2, dgt2, sq = residual_ln_loss_bwd("ffn2_ln_loss", x2, y3, tgt, gt2, lgs[2], lbs[2], 0.5, t)
    loss = lax.psum(0.5 * sq[0, 0] / D, ("x", "y", "c"))
    du3, red_f2i, red_f2o, _ = ffn_bwd("ffn2", u2, ha3, hb3, g3, dy3, g_f2i, f2o, t, sp)
    dz2, dy2, dlg1, dlb1, dgt1, dsh2, dsc2 = residual_ln_bwd("mix_ln_bwd", z2, (dz3, du3, x2, sc2), y2, gt1, lgs[1], 1.0, t)
    du2, mix_parts, sib, db_in, dps, dsinks = mix_bwd(u1, sv2, dy2, wts, b_in2, ps2, sinks2, tabs, t)
    q = {n: chip_sum("chipsum_" + n, mix_parts[n], sib[n], sp, shard_rows[n], tiles[n]) for n in mix_parts}
    dz1, dy1, dlg0, dlb0, dgt0, dsh1, dsc1 = residual_ln_bwd("ffn1_ln_bwd", z1, (dz2, du2, x1, sc1), y1, gt0, lgs[0], 0.5, t)
    early = [n for n in order if n != "f1i"]

    def total(n, q_n, far_n):
        return chip_total("total_" + n, q_n, far_n, sp, shard_rows[n], tiles[n])

    def swap_early(far_a, far_b, q_f1o, far_f1o):
        reduced = {"f1o": (q_f1o, far_f1o), "f2i": red_f2i, "f2o": red_f2o, "win": (q["win"], far_a[0]),
                   "wp": (q["wp"], far_a[1]), "wo": (q["wo"], far_b[0]), "wba": (q["wba"], far_b[1]),
                   "wbb": (q["wbb"], far_b[2])}
        return share_halves_job([total(n, *reduced[n]) for n in early])

    du1, red_f1i, _, shared_early = ffn_bwd(
        "ffn1", u0, ha1, hb1, g1, dy1, g_f1i, f1o, t, sp,
        reduce_chips_job([q["win"], q["wp"]]), reduce_chips_job([q["wo"], q["wba"], q["wbb"]]), swap_early)
    dx0, dsh0, dsc0 = modulate_bwd("ffn1_mod_bwd", dz1, du1, xs, sc0, t)
    gm0, gm1, gm2 = (dsh0, dsc0, dgt0), (dsh1, dsc1, dgt1), (dsh2, dsc2, dgt2)
    gw = dict(zip(early, shared_early))
    (gw["f1i"],) = run_job("share_f1i", share_halves_job([total("f1i", *red_f1i)]))

    small = jnp.concatenate([*gm0, *gm1, *gm2, dlg0, dlg1, dlg2, dlb0, dlb1, dlb2, db_in, dps, dsinks], axis=1)
    n_small = small.shape[1]
    rows_small = -(-n_small // 1024) * 8
    small = jnp.pad(small, ((0, 0), (0, rows_small * 128 - n_small))).reshape(rows_small, 128)
    small_all = allgather_small("gather_small", small)
    tot = sum_devices(small_all).reshape(1, -1)
    gmod_all = small_all.reshape(8, -1)[:, :9 * D]
    o = 9 * D
    g_b_ada = tot[:, :o]
    g_ln_g = lax.dynamic_slice(tot[:, o:o + 3 * D].reshape(3, D), (0, chip * (D // 4)), (3, D // 4))
    g_ln_b = lax.dynamic_slice(tot[:, o + 3 * D:o + 6 * D].reshape(3, D), (0, chip * (D // 4)), (3, D // 4))
    o += 6 * D
    g_b_in, g_ps, g_sinks = tot[:, o:o + IN_W], tot[:, o + IN_W:o + IN_W + PW], tot[:, o + IN_W + PW:o + IN_W + PW + N_Q]

    gm16 = jnp.pad(lax.dynamic_slice(gmod_all, (0, chip * ADA_SH), (8, ADA_SH)), ((0, 8), (0, 0)))
    (g_w_ada, d_w_ada, nm_w_ada, nv_w_ada), _ = ada_bwd_adam(c16, gm16, w_ada[0], m_w_ada[0], v_w_ada[0], None)

    def big(n, w, m, v, tm):
        shape = w.shape
        w2, m2, v2 = (a.reshape(shape[-2] if a.ndim == 3 else -1, shape[-1]) for a in (w, m, v))
        return [r.reshape(shape) for r in adam_rows("adam_" + n, w2, gw[n], m2, v2, tm)]

    def big_t(n, w, m, v):
        return [jnp.swapaxes(r, 0, 1)[None] for r in adam_rows("adam_" + n, tr(w), gw[n], tr(m), tr(v), FH // 8)]

    def tiny(n, w, g, m, v):
        return [g.reshape(w.shape)] + list(adam_small("adam_" + n, w, g.reshape(w.shape), m, v))

    res = {
        "w_ada": [a[None] for a in (g_w_ada, d_w_ada, nm_w_ada, nv_w_ada)],
        "b_ada": tiny("b_ada", b_ada, g_b_ada, m_b_ada, v_b_ada),
        "ln_g": tiny("ln_g", ln_g, g_ln_g, m_ln_g, v_ln_g),
        "ln_b": tiny("ln_b", ln_b, g_ln_b, m_ln_b, v_ln_b),
        "w_ffn1_in": big_t("f1i", w_ffn1_in, m_w_ffn1_in, v_w_ffn1_in),
        "w_ffn1_out": big("f1o", w_ffn1_out, m_w_ffn1_out, v_w_ffn1_out, FO // 4),
        "w_in": big("win", w_in, m_w_in, v_w_in, 256),
        "b_in": tiny("b_in", b_in, g_b_in, m_b_in, v_b_in),
        "w_pool": big("wp", w_pool, m_w_pool, v_w_pool, 256),
        "pool_scale": tiny("pool_scale", pool_scale, g_ps, m_pool_scale, v_pool_scale),
        "sinks": tiny("sinks", sinks, g_sinks, m_sinks, v_sinks),
        "w_branch_a": big("wba", w_branch_a, m_w_branch_a, v_w_branch_a, 512),
        "w_branch_b": big("wbb", w_branch_b, m_w_branch_b, v_w_branch_b, 512),
        "w_out": big("wo", w_out, m_w_out, v_w_out, 128),
        "w_ffn2_in": big_t("f2i", w_ffn2_in, m_w_ffn2_in, v_w_ffn2_in),
        "w_ffn2_out": big("f2o", w_ffn2_out, m_w_ffn2_out, v_w_ffn2_out, FO // 4),
    }
    names = ["w_ada", "b_ada", "ln_g", "ln_b", "w_ffn1_in", "w_ffn1_out", "w_in", "b_in", "w_pool", "pool_scale", "sinks",
             "w_branch_a", "w_branch_b", "w_out", "w_ffn2_in", "w_ffn2_out"]
    return (loss, dx0[None], *[res[n][0] for n in names], *[res[n][1] for n in names],
            *[res[n][2] for n in names], *[res[n][3] for n in names])
```

```python
import jax
import jax.numpy as jnp
from jax import lax
from jax.experimental import pallas as pl
from jax.experimental.pallas import tpu as pltpu

F32 = jnp.float32
BF16 = jnp.bfloat16
MESH = pl.DeviceIdType.MESH
ANY = pl.BlockSpec(memory_space=pl.ANY)

D = 2048
N_Q, N_KV, HD = 16, 4, 64
QW, KVW = N_Q * HD, N_KV * HD
BLK = 128
POOL_WINDOWS = (2, 4, 8, 16)
PW, PG = 1024, 256
HALO = 16
ROPE_THETA = 500000.0
ROT = HD // 4
LN_EPS = 1e-5
ALPHA = 2.0 ** 0.25
FH = 2752
FHP = 2816
FO = 1376
IN_W = 6656
IN_SH = IN_W // 4
ADA_SH = 18432 // 4
B1, B2, LR, EPS, WD, STEP = 0.9, 0.999, 0.001, 1e-08, 0.01, 10
VMEM_LIMIT = 56 * 1024 * 1024
FLIPS = ((1, 0), (0, 1), (1, 1))
NN = (((1,), (0,)), ((), ()))
NT = (((1,), (1,)), ((), ()))
TN = (((0,), (0,)), ((), ()))


def _params(sem):
    return pltpu.CompilerParams(dimension_semantics=sem, vmem_limit_bytes=VMEM_LIMIT)


def _aligned(v, m):
    return v if isinstance(v, int) else pl.multiple_of(v, m)


def _sigmoid(v):
    return 1.0 / (1.0 + jnp.exp(-v))


def T_(arr, width=None, off=0):
    return ("t", arr, width, off)


def B_(arr, width=None, off=0):
    return ("b", arr, width, off)


def X_(arr, spec):
    return ("x", arr, spec, 0)


def rowmap(name, fn, ins, outs, accs=(), *, rows, tm, ncol=1, with_ids=False, sp=None, alias=None):
    tm = min(tm, rows)
    nrow = rows // tm
    in_specs, arrs = [], []
    for kind, arr, width, off in ins:
        if kind == "x":
            in_specs.append(width)
        elif kind == "t":
            w = arr.shape[1] if width is None else width
            in_specs.append(pl.BlockSpec((tm, w), lambda j, i, *_, off=off: (i, off + j)))
        else:
            w = arr.shape[1] if width is None else width
            in_specs.append(pl.BlockSpec((arr.shape[0], w), lambda j, i, *_, off=off: (0, off + j)))
        arrs.append(arr)
    out_shape, out_specs = [], []
    for o in outs:
        if len(o) == 3:
            out_shape.append(jax.ShapeDtypeStruct(o[0], o[1]))
            out_specs.append(o[2])
        else:
            out_shape.append(jax.ShapeDtypeStruct((rows, o[0]), o[1]))
            out_specs.append(pl.BlockSpec((tm, o[0] // ncol), lambda j, i, *_: (i, j)))
    for r, width in accs:
        out_shape.append(jax.ShapeDtypeStruct((r, width), F32))
        out_specs.append(pl.BlockSpec((r, width // ncol), lambda j, i, *_: (0, j)))
    ni, no = len(ins), len(outs)
    nsp = 0 if sp is None else 1

    def body(*refs):
        refs = refs[nsp:]
        i = pl.program_id(1)
        vals = [r[...] for r in refs[:ni]]
        res = fn(pl.program_id(0), i, *vals) if with_ids else fn(*vals)
        if not isinstance(res, (tuple, list)):
            res = (res,)
        for r, v in zip(refs[ni:ni + no], res[:no]):
            r[...] = v.astype(r.dtype)
        for r, v in zip(refs[ni + no:], res[no:]):
            @pl.when(i == 0)
            def _(r=r, v=v):
                r[...] = v

            @pl.when(i > 0)
            def _(r=r, v=v):
                r[...] += v

    grid_spec = pltpu.PrefetchScalarGridSpec(num_scalar_prefetch=nsp, grid=(ncol, nrow), in_specs=in_specs,
                                             out_specs=out_specs)
    res = pl.pallas_call(
        body, name=name, grid_spec=grid_spec, out_shape=out_shape,
        input_output_aliases={nsp + k: v for k, v in (alias or {}).items()},
        compiler_params=_params(("arbitrary", "arbitrary")),
    )(*([sp] if nsp else []), *arrs)
    return res[0] if len(res) == 1 else res


def colsum(v):
    return jnp.sum(v, axis=0, keepdims=True)


def mm(name, a_ops, b_ops, ops, *, dims, grid, a_specs, b_specs, outs, out_specs, acc_shapes,
       epilogue=None, extras=(), extra_specs=(), carry=None, job=None, sub_rows=None):
    gk = grid[2]
    na, nb, ne, nacc = len(a_ops), len(b_ops), len(extras), len(acc_shapes)
    nc = 0 if carry is None else 1
    no = len(outs)

    def body(*refs):
        a_refs = refs[:na]
        b_refs = refs[na:na + nb]
        e_refs = refs[na + nb:na + nb + ne]
        o_refs = refs[na + nb + ne + nc:na + nb + ne + nc + no]
        acc_refs = refs[na + nb + ne + nc + no:]
        k = pl.program_id(2)

        def partials(rows=slice(None)):
            res = [None] * nacc
            for ai, bi, ci in ops:
                p = lax.dot_general(a_refs[ai][rows], b_refs[bi][...], dims, preferred_element_type=F32)
                res[ci] = p if res[ci] is None else res[ci] + p
            return res

        def finish(accs, rows=slice(None)):
            outv = epilogue(accs, [e[rows] for e in e_refs]) if epilogue else (accs[0],)
            for o, v in zip(o_refs, outv):
                o[rows] = v.astype(o.dtype)

        if gk == 1 and sub_rows:
            for s in range(out_specs[0].block_shape[-2] // sub_rows):
                rows = pl.ds(s * sub_rows, sub_rows)
                finish(partials(rows), rows)
        elif gk == 1:
            finish(partials())
        else:
            ps = partials()

            @pl.when(k == 0)
            def _():
                for acc, p in zip(acc_refs, ps):
                    acc[...] = p

            @pl.when((k > 0) & (k < gk - 1))
            def _():
                for acc, p in zip(acc_refs, ps):
                    acc[...] += p

            @pl.when(k == gk - 1)
            def _():
                finish([acc[...] + p for acc, p in zip(acc_refs, ps)])

    res, moved = carried_call(
        body, name, grid,
        list(a_specs) + list(b_specs) + list(extra_specs) + ([ANY] if nc else []), list(out_specs), list(outs),
        [pltpu.VMEM(s, F32) for s in acc_shapes] if gk > 1 else [],
        [*a_ops, *b_ops, *extras, *([carry] if nc else [])], {na + nb + ne: 0} if nc else {}, job)
    res = res[0] if len(res) == 1 else res
    return res if job is None else (res, moved)


def sds(shape, dt):
    return jax.ShapeDtypeStruct(shape, dt)


class Job:
    def __init__(self, ins, outs, aliases, scratch, start, mid, finish):
        self.ins, self.outs, self.aliases, self.scratch = list(ins), list(outs), dict(aliases), list(scratch)
        self.start, self.mid, self.finish = start, mid, finish


def carried_call(body, name, grid, in_specs, out_specs, out_shape, scratch, args, aliases, job, mid_at=0.9):
    sem = ("arbitrary",) * len(grid)
    if job is None:
        res = pl.pallas_call(body, name=name, grid=grid, in_specs=in_specs, out_specs=out_specs, out_shape=out_shape,
                             scratch_shapes=scratch, input_output_aliases=aliases, compiler_params=_params(sem))(*args)
        return list(res), []
    ni, no, ns = len(in_specs), len(out_specs), len(scratch)
    ci, co = len(job.ins), len(job.outs)
    total = 1
    for g in grid:
        total *= g
    mid_step = min(max(int(total * mid_at), 1), total - 1)

    def full(*refs):
        ins, cins = refs[:ni], refs[ni:ni + ci]
        outs, couts = refs[ni + ci:ni + ci + no], refs[ni + ci + no:ni + ci + no + co]
        scr, cscr = refs[ni + ci + no + co:ni + ci + no + co + ns], refs[ni + ci + no + co + ns:]
        step = 0
        for d, g in enumerate(grid):
            step = step * g + pl.program_id(d)

        @pl.when(step == 0)
        def _():
            job.start(cins, couts, cscr)

        body(*ins, *outs, *scr)

        @pl.when(step == mid_step)
        def _():
            job.mid(cins, couts, cscr)

        @pl.when(step == total - 1)
        def _():
            job.finish(cins, couts, cscr)

    al = dict(aliases)
    al.update({ni + k: no + v for k, v in job.aliases.items()})
    res = pl.pallas_call(
        full, name=name, grid=grid, in_specs=in_specs + [ANY] * ci, out_specs=out_specs + [ANY] * co,
        out_shape=out_shape + job.outs, scratch_shapes=scratch + job.scratch, input_output_aliases=al,
        compiler_params=_params(sem))(*args, *job.ins)
    return list(res[:no]), list(res[no:])


def merge_jobs(a, b):
    ni, no, ns = len(a.ins), len(a.outs), len(a.scratch)

    def both(fa, fb):
        def run(ins, outs, scr):
            fa(ins[:ni], outs[:no], scr[:ns])
            fb(ins[ni:], outs[no:], scr[ns:])
        return run

    aliases = dict(a.aliases)
    aliases.update({ni + k: no + v for k, v in b.aliases.items()})
    return Job(a.ins + b.ins, a.outs + b.outs, aliases, a.scratch + b.scratch,
               both(a.start, b.start), both(a.mid, b.mid), both(a.finish, b.finish))


def _with_moved(res, job):
    return res if job is not None else (res, [])


def run_job(name, job):
    ci = len(job.ins)

    def body(*refs):
        cins, couts, cscr = refs[:ci], refs[ci:ci + len(job.outs)], refs[ci + len(job.outs):]
        job.start(cins, couts, cscr)
        job.mid(cins, couts, cscr)
        job.finish(cins, couts, cscr)

    return list(pl.pallas_call(
        body, name=name, in_specs=[ANY] * ci, out_specs=[ANY] * len(job.outs), out_shape=job.outs,
        scratch_shapes=job.scratch, input_output_aliases=job.aliases)(*job.ins))


def _place():
    x, y, c = lax.axis_index("x"), lax.axis_index("y"), lax.axis_index("c")
    chips = [((1 - x) if fx else x, (1 - y) if fy else y) for fx, fy in FLIPS]
    return x, y, c, chips


def allgather_small(name, v):
    r = v.shape[0]

    def body(x_ref, out_ref, send_sems, recv_sems, local_sem):
        x, y, c, chips = _place()
        me, sibling = (x, y, c), (x, y, 1 - c)

        def rows(px, py, pc):
            return out_ref.at[4 * px + 2 * py + pc]

        def copy(k, block, to, src=None):
            return pltpu.make_async_remote_copy(
                src_ref=rows(*block) if src is None else src, dst_ref=rows(*block),
                send_sem=send_sems.at[k], recv_sem=recv_sems.at[k], device_id=to, device_id_type=MESH)

        mine = pltpu.make_async_copy(x_ref, rows(*me), local_sem)
        mine.start()
        first = [copy(0, me, sibling, src=x_ref)]
        first += [copy(1 + j, me, (*chip, c), src=x_ref) for j, chip in enumerate(chips)]
        for cp in first:
            cp.start()
        passed = [copy(4 + j, (*chip, c), sibling) for j, chip in enumerate(chips)]
        for j, chip in enumerate(chips):
            copy(1 + j, (*chip, c), me).wait_recv()
            passed[j].start()
        copy(0, sibling, me).wait_recv()
        for j, chip in enumerate(chips):
            copy(4 + j, (*chip, 1 - c), me).wait_recv()
        for cp in first + passed:
            cp.wait_send()
        mine.wait()

    return pl.pallas_call(
        body, name=name, out_shape=sds((8, r, 128), v.dtype),
        in_specs=[pl.BlockSpec(memory_space=pltpu.VMEM)], out_specs=pl.BlockSpec(memory_space=pltpu.VMEM),
        scratch_shapes=[pltpu.SemaphoreType.DMA((7,)), pltpu.SemaphoreType.DMA((7,)), pltpu.SemaphoreType.DMA],
    )(v)


def _half(ref, rows, hf):
    hr = rows // 2
    return ref.at[pl.ds(_aligned(hf * hr, 16), hr)]


def view_lead(ref, p):
    return ref.at[p]


def view_ffn_out(ref, p):
    return ref.at[p // 2, pl.ds(_aligned((p % 2) * FO, 16), FO)]


def _remote(ref, dst, send_sems, recv_sems, idx, to):
    return pltpu.make_async_remote_copy(src_ref=ref, dst_ref=dst, send_sem=send_sems.at[idx], recv_sem=recv_sems.at[idx],
                                        device_id=to, device_id_type=MESH)


def gather_job(items):
    nw = len(items)
    pads = [w for w, it in enumerate(items) if it[1] is view_ffn_out]

    def piece(ref, w, p, hf):
        _, view, rws, part, parts = items[w]
        pr = rws // 2 // parts
        return view(ref, p).at[pl.ds(_aligned(hf * (rws // 2) + part * pr, 16), pr)]

    def pad_copies(outs, scr):
        return [pltpu.make_async_copy(scr[2], outs[w].at[h, pl.ds(2 * FO, FHP - 2 * FO)], scr[3].at[2 * n + h])
                for n, w in enumerate(pads) for h in range(2)]

    def start(_, outs, scr):
        x, y, c, chips = _place()
        if pads:
            scr[2][...] = jnp.zeros_like(scr[2])
            for cp in pad_copies(outs, scr):
                cp.start()
        for w in range(nw):
            mine = piece(outs[w], w, 2 * x + y, c)
            for f, (px, py) in enumerate(chips):
                _remote(mine, mine, scr[0], scr[1], (w, f), (px, py, c)).start()

    def mid(_, outs, scr):
        x, y, c, chips = _place()
        for w in range(nw):
            for f, (px, py) in enumerate(chips):
                land = piece(outs[w], w, 2 * px + py, c)
                _remote(land, land, scr[0], scr[1], (w, f), (px, py, c)).wait_recv()
                _remote(land, land, scr[0], scr[1], (w, 3 + f), (x, y, 1 - c)).start()

    def finish(_, outs, scr):
        x, y, c, chips = _place()
        for w in range(nw):
            for f, (px, py) in enumerate(chips):
                land = piece(outs[w], w, 2 * px + py, 1 - c)
                _remote(land, land, scr[0], scr[1], (w, 3 + f), (x, y, 1 - c)).wait_recv()
        for w in range(nw):
            mine = piece(outs[w], w, 2 * x + y, c)
            for f in range(6):
                _remote(mine, mine, scr[0], scr[1], (w, f), (x, y, 1 - c)).wait_send()
        for cp in pad_copies(outs, scr):
            cp.wait()

    scratch = [pltpu.SemaphoreType.DMA((nw, 6)), pltpu.SemaphoreType.DMA((nw, 6))]
    if pads:
        scratch += [pltpu.VMEM((FHP - 2 * FO, D), BF16), pltpu.SemaphoreType.DMA((2 * len(pads),))]
    bufs = [it[0] for it in items]
    return Job(bufs, [sds(b.shape, BF16) for b in bufs], {w: w for w in range(nw)}, scratch, start, mid, finish)


HBM = pl.BlockSpec(memory_space=pltpu.HBM)
SEM = pl.BlockSpec(memory_space=pltpu.SEMAPHORE)
SPLIT = pltpu.CompilerParams(has_side_effects=pltpu.SideEffectType.DATAFLOW_SIDE_EFFECTING)


def gather_start(name, buf, rows, after):
    def body(*refs):
        out, send_sems, recv_sems, token = refs[1 + len(after):]
        x, y, c, chips = _place()
        mine = _half(out.at[2 * x + y], rows, c)
        for f, (px, py) in enumerate(chips):
            _remote(mine, mine, send_sems, recv_sems, f, (px, py, c)).start()
        token[...] = jnp.zeros_like(token)

    return pl.pallas_call(
        body, name=name,
        out_shape=(pltpu.HBM(buf.shape, buf.dtype), pltpu.SemaphoreType.DMA((3,)), pltpu.SemaphoreType.DMA((3,)),
                   sds((8, 128), F32)),
        in_specs=(HBM,) + (ANY,) * len(after), out_specs=(HBM, SEM, SEM, pl.BlockSpec(memory_space=pltpu.VMEM)),
        input_output_aliases={0: 0}, compiler_params=SPLIT)(pltpu.with_memory_space_constraint(buf, pltpu.HBM), *after)


def gather_wait(name, buf, send_sems, recv_sems, rows, after):
    def body(_, send_sems, recv_sems, *rest):
        out = rest[-1]
        x, y, c, chips = _place()
        mine = _half(out.at[2 * x + y], rows, c)
        for f, (px, py) in enumerate(chips):
            cp = _remote(mine, _half(out.at[2 * px + py], rows, c), send_sems, recv_sems, f, (px, py, c))
            cp.wait_send()
            cp.wait_recv()

    return pl.pallas_call(
        body, name=name, out_shape=pltpu.HBM(buf.shape, buf.dtype),
        in_specs=(HBM, SEM, SEM) + (ANY,) * len(after), out_specs=HBM, input_output_aliases={0: 0},
        compiler_params=SPLIT)(buf, send_sems, recv_sems, *after)


def forward_job(buf, rows):
    def copies(outs, scr, hf):
        x, y, c, chips = _place()
        half = c if hf == 0 else 1 - c
        return [_remote(_half(outs[0].at[2 * px + py], rows, half), _half(outs[0].at[2 * px + py], rows, half),
                        scr[0], scr[1], f, (x, y, 1 - c)) for f, (px, py) in enumerate(chips)]

    def start(_, outs, scr):
        for cp in copies(outs, scr, 0):
            cp.start()

    def finish(_, outs, scr):
        for cp in copies(outs, scr, 1):
            cp.wait_recv()
        for cp in copies(outs, scr, 0):
            cp.wait_send()

    return Job([buf], [sds(buf.shape, buf.dtype)], {0: 0},
               [pltpu.SemaphoreType.DMA((3,)), pltpu.SemaphoreType.DMA((3,))], start, lambda *_: None, finish)


def reduce_sibling_job(items):
    nw = len(items)

    def copies(ins, got, scr):
        x, y, c, _ = _place()
        return [_remote(_half(view(ins[w], p), rws, 1 - c), got[w].at[p], scr[0], scr[1], (w, p), (x, y, 1 - c))
                for w, (_, view, rws, _) in enumerate(items) for p in range(4)]

    def start(ins, got, scr):
        for cp in copies(ins, got, scr):
            cp.start()

    def finish(ins, got, scr):
        for cp in copies(ins, got, scr):
            cp.wait()

    return Job([it[0] for it in items], [sds((4, it[2] // 2, it[3]), BF16) for it in items], {},
               [pltpu.SemaphoreType.DMA((nw, 4)), pltpu.SemaphoreType.DMA((nw, 4))], start, lambda *_: None, finish)


def reduce_chips_job(qs):
    nw = len(qs)

    def copies(ins, got, scr):
        x, y, c, chips = _place()
        return [_remote(ins[w].at[2 * px + py], got[w].at[f], scr[0], scr[1], (w, f), (px, py, c))
                for w in range(nw) for f, (px, py) in enumerate(chips)]

    def start(ins, got, scr):
        for cp in copies(ins, got, scr):
            cp.start()

    def finish(ins, got, scr):
        for cp in copies(ins, got, scr):
            cp.wait()

    return Job(qs, [sds((3,) + q.shape[1:], BF16) for q in qs], {},
               [pltpu.SemaphoreType.DMA((nw, 3)), pltpu.SemaphoreType.DMA((nw, 3))], start, lambda *_: None, finish)


def share_halves_job(gs):
    nw = len(gs)

    def start(_, outs, scr):
        x, y, c, _ = _place()
        for w in range(nw):
            mine = _half(outs[w], gs[w].shape[0], c)
            _remote(mine, mine, scr[0], scr[1], w, (x, y, 1 - c)).start()

    def finish(_, outs, scr):
        x, y, c, _ = _place()
        for w in range(nw):
            mine = _half(outs[w], gs[w].shape[0], c)
            theirs = _half(outs[w], gs[w].shape[0], 1 - c)
            _remote(mine, mine, scr[0], scr[1], w, (x, y, 1 - c)).wait_send()
            _remote(theirs, theirs, scr[0], scr[1], w, (x, y, 1 - c)).wait_recv()

    return Job(gs, [sds(g.shape, F32) for g in gs], {w: w for w in range(nw)},
               [pltpu.SemaphoreType.DMA((nw,)), pltpu.SemaphoreType.DMA((nw,))], start, lambda *_: None, finish)


def rope_tables(t):
    pos = jnp.arange(t, dtype=F32)
    inv_freq = ROPE_THETA ** (-jnp.arange(0, ROT, 2, dtype=F32) / ROT)
    ang = pos[:, None] * inv_freq[None, :]
    cos, sin = jnp.cos(ang), jnp.sin(ang)
    d = jnp.arange(128) % HD
    half = ROT // 2
    cs = jnp.take(cos, d % half, axis=1)
    sn = jnp.take(sin, d % half, axis=1)
    cc = jnp.where(d[None] < ROT, cs, 1.0)
    sa = jnp.where(d[None] < half, -sn, 0.0)
    sb = jnp.where((d[None] >= half) & (d[None] < ROT), sn, 0.0)
    return cc, sa, sb


def _rope(v, cc, sa, sb):
    w = v.shape[1]
    reps = w // 128
    half = ROT // 2
    return (v * jnp.tile(cc, (1, reps)) + pltpu.roll(v, w - half, 1) * jnp.tile(sa, (1, reps))
            + pltpu.roll(v, half, 1) * jnp.tile(sb, (1, reps)))


def _rope_t(dv, cc, sa, sb):
    w = dv.shape[1]
    reps = w // 128
    half = ROT // 2
    return (dv * jnp.tile(cc, (1, reps)) + pltpu.roll(dv * jnp.tile(sa, (1, reps)), half, 1)
            + pltpu.roll(dv * jnp.tile(sb, (1, reps)), w - half, 1))


def pool_fwd(h, b_in, t, tm):
    tm = min(tm, t)
    per = tm // HALO

    def body(prev_ref, cur_ref, b_ref, o_ref, xx):
        i = pl.program_id(0)
        b = b_ref[...]
        xx[pl.ds(0, HALO), :] = jnp.where(i > 0, prev_ref[...] + b, 0.0)
        xx[pl.ds(HALO, tm), :] = cur_ref[...] + b
        tpos = i * tm + lax.broadcasted_iota(jnp.int32, (tm, PG), 0) + 1
        for gi, w in enumerate(POOL_WINDOWS):
            cols = pl.ds(gi * PG, PG)
            acc = xx[pl.ds(HALO, tm), cols]
            for s in range(1, w):
                acc = acc + xx[pl.ds(HALO - s, tm), cols]
            cnt = jnp.minimum(tpos, w).astype(F32)
            o_ref[:, cols] = (acc / cnt - xx[pl.ds(HALO, tm), cols]).astype(o_ref.dtype)

    return pl.pallas_call(
        body, name="pool_fwd", grid=(t // tm,),
        in_specs=[pl.BlockSpec((HALO, PW), lambda i: (jnp.maximum(i * per - 1, 0), 0)),
                  pl.BlockSpec((tm, PW), lambda i: (i, 0)), pl.BlockSpec((1, PW), lambda i: (0, 0))],
        out_specs=pl.BlockSpec((tm, PW), lambda i: (i, 0)), out_shape=sds((t, PW), BF16),
        scratch_shapes=[pltpu.VMEM((tm + HALO, PW), F32)], compiler_params=_params(("arbitrary",)),
    )(h, h, b_in)


def pool_bwd(dpooled, t, tm):
    tm = min(tm, t)
    per = tm // HALO
    nt = t // tm

    def body(cur_ref, nxt_ref, o_ref, db_ref, ee):
        i = pl.program_id(0)
        tpos = i * tm + lax.broadcasted_iota(jnp.int32, (tm, PG), 0) + 1
        for gi, w in enumerate(POOL_WINDOWS):
            cols = pl.ds(gi * PG, PG)
            ee[pl.ds(0, tm), cols] = cur_ref[:, cols] / jnp.minimum(tpos, w).astype(F32)
            ee[pl.ds(tm, HALO), cols] = jnp.where(i < nt - 1, nxt_ref[:, cols] / float(w), 0.0)
        for gi, w in enumerate(POOL_WINDOWS):
            cols = pl.ds(gi * PG, PG)
            acc = ee[pl.ds(0, tm), cols]
            for s in range(1, w):
                acc = acc + ee[pl.ds(s, tm), cols]
            dxp = acc - cur_ref[:, cols]
            o_ref[:, cols] = dxp.astype(o_ref.dtype)
            part = colsum(dxp)

            @pl.when(i == 0)
            def _(cols=cols, part=part):
                db_ref[:, cols] = part

            @pl.when(i > 0)
            def _(cols=cols, part=part):
                db_ref[:, cols] += part

    return pl.pallas_call(
        body, name="pool_bwd", grid=(nt,),
        in_specs=[pl.BlockSpec((tm, PW), lambda i: (i, 0)),
                  pl.BlockSpec((HALO, PW), lambda i: (jnp.minimum((i + 1) * per, t // HALO - 1), 0))],
        out_specs=[pl.BlockSpec((tm, PW), lambda i: (i, 0)), pl.BlockSpec((1, PW), lambda i: (0, 0))],
        out_shape=[sds((t, PW), BF16), sds((1, PW), F32)],
        scratch_shapes=[pltpu.VMEM((tm + HALO, PW), F32)], compiler_params=_params(("arbitrary",)),
    )(dpooled, dpooled)


def _scores(qh, kp, kc, mask_p, mask_c, sink):
    sp = jnp.where(mask_p, lax.dot_general(qh, kp, NT, preferred_element_type=F32), -1e30)
    sc = jnp.where(mask_c, lax.dot_general(qh, kc, NT, preferred_element_type=F32), -1e30)
    m = jnp.maximum(jnp.maximum(jnp.max(sp, axis=-1, keepdims=True), jnp.max(sc, axis=-1, keepdims=True)), sink)
    pp, pc = jnp.exp(sp - m), jnp.exp(sc - m)
    es = jnp.exp(sink - m)
    inv = 1.0 / (jnp.sum(pp, axis=-1, keepdims=True) + jnp.sum(pc, axis=-1, keepdims=True) + es)
    return pp * inv, pc * inv, es * inv


GRP = N_Q // N_KV


def _masks(n):
    qi = lax.broadcasted_iota(jnp.int32, (GRP * BLK, BLK), 0) % BLK
    kj = lax.broadcasted_iota(jnp.int32, (GRP * BLK, BLK), 1)
    return (kj > qi) & (n > 0), kj <= qi


def _head(hk, g):
    return pl.ds(HD * (GRP * hk + g), HD)


def _stack_heads(ref, hk):
    return jnp.concatenate([ref[:, _head(hk, g)] for g in range(GRP)], axis=0)


def _stack_sinks(s_ref, hk):
    return jnp.concatenate([jnp.full((BLK, 1), s_ref[0, GRP * hk + g], F32) for g in range(GRP)], axis=0)


def attn_fwd(q, k, v, sinks, t, job=None):
    def body(s_ref, q_ref, kp_ref, kc_ref, vp_ref, vc_ref, o_ref):
        n = pl.program_id(0)
        mask_p, mask_c = _masks(n)
        for hk in range(N_KV):
            kv = pl.ds(HD * hk, HD)
            pp, pc, _ = _scores(_stack_heads(q_ref, hk), kp_ref[:, kv], kc_ref[:, kv], mask_p, mask_c,
                                _stack_sinks(s_ref, hk))
            o = (lax.dot_general(pp.astype(BF16), vp_ref[:, kv], NN, preferred_element_type=F32)
                 + lax.dot_general(pc.astype(BF16), vc_ref[:, kv], NN, preferred_element_type=F32))
            for g in range(GRP):
                o_ref[:, _head(hk, g)] = o[g * BLK:(g + 1) * BLK].astype(o_ref.dtype)

    prev = lambda n: (jnp.maximum(n - 1, 0), 0)
    cur = lambda n: (n, 0)
    res, moved = carried_call(
        body, "attn_fwd", (t // BLK,),
        [pl.BlockSpec(memory_space=pltpu.SMEM), pl.BlockSpec((BLK, QW), cur),
         pl.BlockSpec((BLK, KVW), prev), pl.BlockSpec((BLK, KVW), cur),
         pl.BlockSpec((BLK, KVW), prev), pl.BlockSpec((BLK, KVW), cur)],
        [pl.BlockSpec((BLK, QW), cur)], [sds((t, QW), BF16)], [], [sinks, q, k, k, v, v], {}, job)
    return res[0], moved


def attn_bwd(q, k, v, do, sinks, t):
    nb = t // BLK

    def body(s_ref, q_ref, do_ref, kp_ref, kc_ref, vp_ref, vc_ref, dq_ref, dk_ref, dv_ref, ds_ref, dkc, dvc):
        n = pl.program_id(0)

        @pl.when(n == 0)
        def _():
            dkc[...] = jnp.zeros_like(dkc)
            dvc[...] = jnp.zeros_like(dvc)
            ds_ref[...] = jnp.zeros_like(ds_ref)

        @pl.when(n < nb)
        def _():
            mask_p, mask_c = _masks(n)
            lane = lax.broadcasted_iota(jnp.int32, (1, 128), 1)
            dsink = jnp.zeros((1, 128), F32)
            for hk in range(N_KV):
                kv = pl.ds(HD * hk, HD)
                kp, kc, vp, vc = kp_ref[:, kv], kc_ref[:, kv], vp_ref[:, kv], vc_ref[:, kv]
                qs, dos = _stack_heads(q_ref, hk), _stack_heads(do_ref, hk)
                pp, pc, ps = _scores(qs, kp, kc, mask_p, mask_c, _stack_sinks(s_ref, hk))
                dpp = lax.dot_general(dos, vp, NT, preferred_element_type=F32)
                dpc = lax.dot_general(dos, vc, NT, preferred_element_type=F32)
                delta = jnp.sum(pp * dpp, axis=-1, keepdims=True) + jnp.sum(pc * dpc, axis=-1, keepdims=True)
                dsp = (pp * (dpp - delta)).astype(BF16)
                dsc = (pc * (dpc - delta)).astype(BF16)
                sd = ps * delta
                dq = (lax.dot_general(dsp, kp, NN, preferred_element_type=F32)
                      + lax.dot_general(dsc, kc, NN, preferred_element_type=F32))
                for g in range(GRP):
                    rows = slice(g * BLK, (g + 1) * BLK)
                    dsink = dsink + jnp.where(lane == GRP * hk + g, -jnp.sum(sd[rows]), 0.0)
                    dq_ref[:, _head(hk, g)] = dq[rows]
                dk_ref[:, kv] = dkc[:, kv] + lax.dot_general(dsp, qs, TN, preferred_element_type=F32)
                dv_ref[:, kv] = dvc[:, kv] + lax.dot_general(pp.astype(BF16), dos, TN, preferred_element_type=F32)
                dkc[:, kv] = lax.dot_general(dsc, qs, TN, preferred_element_type=F32)
                dvc[:, kv] = lax.dot_general(pc.astype(BF16), dos, TN, preferred_element_type=F32)
            ds_ref[...] += dsink

        @pl.when(n == nb)
        def _():
            dk_ref[...] = dkc[...]
            dv_ref[...] = dvc[...]

    cur = lambda n: (jnp.minimum(n, nb - 1), 0)
    prev = lambda n: (jnp.clip(n - 1, 0, nb - 1), 0)
    return pl.pallas_call(
        body, name="attn_bwd", grid=(nb + 1,),
        in_specs=[pl.BlockSpec(memory_space=pltpu.SMEM), pl.BlockSpec((BLK, QW), cur), pl.BlockSpec((BLK, QW), cur),
                  pl.BlockSpec((BLK, KVW), prev), pl.BlockSpec((BLK, KVW), cur),
                  pl.BlockSpec((BLK, KVW), prev), pl.BlockSpec((BLK, KVW), cur)],
        out_specs=[pl.BlockSpec((BLK, QW), cur), pl.BlockSpec((BLK, KVW), prev), pl.BlockSpec((BLK, KVW), prev),
                   pl.BlockSpec((1, 128), lambda n: (0, 0))],
        out_shape=[sds((t, QW), F32), sds((t, KVW), F32), sds((t, KVW), F32), sds((1, 128), F32)],
        scratch_shapes=[pltpu.VMEM((BLK, KVW), F32), pltpu.VMEM((BLK, KVW), F32)],
        compiler_params=_params(("arbitrary",)),
    )(sinks, q, do, k, k, v, v)


def _adamw(w, g, m, v):
    m2 = B1 * m + (1.0 - B1) * g
    v2 = B2 * v + (1.0 - B2) * jnp.square(g)
    m_hat = m2 / (1.0 - B1 ** STEP)
    v_hat = v2 / (1.0 - B2 ** STEP)
    return -LR * (m_hat / (jnp.sqrt(v_hat) + EPS) + WD * w), m2, v2


def ada_fwd(c16, w_ada, b_sh):
    tn = 512

    def body(c_ref, w_ref, b_ref, o_ref):
        cv = c_ref[...]
        sc = (cv * _sigmoid(cv)).astype(BF16)
        o_ref[...] = lax.dot_general(sc, w_ref[...].astype(BF16), NN, preferred_element_type=F32) + b_ref[...]

    return pl.pallas_call(
        body, name="ada_fwd", grid=(ADA_SH // tn,),
        in_specs=[pl.BlockSpec((16, D), lambda j: (0, 0)), pl.BlockSpec((D, tn), lambda j: (0, j)),
                  pl.BlockSpec((1, tn), lambda j: (0, j))],
        out_specs=pl.BlockSpec((16, tn), lambda j: (0, j)), out_shape=sds((16, ADA_SH), F32),
        compiler_params=_params(("arbitrary",)),
    )(c16, w_ada, b_sh)


def ada_bwd_adam(c16, gm16, w, m, v, job):
    tm, tn = 512, ADA_SH // 4

    def body(c_ref, g_ref, w_ref, m_ref, v_ref, go_ref, d_ref, mo_ref, vo_ref):
        cv = c_ref[...]
        sc = (cv * _sigmoid(cv)).astype(BF16)
        g = lax.dot_general(sc, g_ref[...].astype(BF16), TN, preferred_element_type=F32)
        dl, m2, v2 = _adamw(w_ref[...], g, m_ref[...], v_ref[...])
        go_ref[...] = g
        d_ref[...] = dl
        mo_ref[...] = m2
        vo_ref[...] = v2

    blk = pl.BlockSpec((tm, tn), lambda i, j: (i, j))
    return carried_call(
        body, "ada_bwd_adam", (D // tm, ADA_SH // tn),
        [pl.BlockSpec((16, tm), lambda i, j: (0, i)), pl.BlockSpec((16, tn), lambda i, j: (0, j)), blk, blk, blk],
        [blk] * 4, [sds((D, ADA_SH), F32)] * 4, [], [c16, gm16, w, m, v], {}, job)


def adam_rows(name, w, g, m, v, tm):
    rows, cols = w.shape

    def fn(wv, gv, mv, vv):
        gv = gv[:, :cols]
        dl, m2, v2 = _adamw(wv, gv, mv, vv)
        return gv, dl, m2, v2

    return rowmap(name, fn, [T_(w), T_(g), T_(m), T_(v)], [(cols, F32)] * 4, rows=rows, tm=tm)


def adam_small(name, w, g, m, v):
    def body(w_ref, g_ref, m_ref, v_ref, d_ref, mo_ref, vo_ref):
        dl, m2, v2 = _adamw(w_ref[...], g_ref[...], m_ref[...], v_ref[...])
        d_ref[...] = dl
        mo_ref[...] = m2
        vo_ref[...] = v2

    return pl.pallas_call(body, name=name, out_shape=[sds(w.shape, F32)] * 3)(w, g, m, v)


def sum_devices(allv):
    def body(a_ref, o_ref):
        acc = a_ref[0]
        for d in range(1, 8):
            acc = acc + a_ref[d]
        o_ref[...] = acc

    return pl.pallas_call(body, name="sum_devices", out_shape=sds(allv.shape[1:], F32))(allv)


def _ln_fwd(z, g, b):
    mu = jnp.mean(z, axis=-1, keepdims=True)
    zc = z - mu
    var = jnp.mean(jnp.square(zc), axis=-1, keepdims=True)
    return zc * lax.rsqrt(var + LN_EPS) * g + b


def _ln_bwd(z, g, dout):
    mu = jnp.mean(z, axis=-1, keepdims=True)
    zc = z - mu
    var = jnp.mean(jnp.square(zc), axis=-1, keepdims=True)
    rstd = lax.rsqrt(var + LN_EPS)
    xh = zc * rstd
    dxh = dout * g
    dz = rstd * (dxh - jnp.mean(dxh, axis=-1, keepdims=True) - xh * jnp.mean(dxh * xh, axis=-1, keepdims=True))
    return dz, colsum(dout * xh), colsum(dout)


def modulate(name, xin, shift, scale, t):
    return rowmap(name, lambda xv, sh, sc: xv * (1.0 + sc) + sh, [T_(xin), B_(shift), B_(scale)], [(D, BF16)],
                  rows=t, tm=512)


def residual_ln_mod(name, xin, y, gate, lg, lb, wgt, shift_n, scale_n, t):
    def fn(xv, yv, gt, g, b, sh, sc):
        z = ALPHA * xv + (wgt * (1.0 + gt)) * yv
        xo = _ln_fwd(z, g, b)
        return xo, z, xo * (1.0 + sc) + sh

    return rowmap(name, fn, [T_(xin), T_(y), B_(gate), B_(lg), B_(lb), B_(shift_n), B_(scale_n)],
                  [(D, F32), (D, F32), (D, BF16)], rows=t, tm=512)


def residual_ln_bwd(name, z, dnext, y, gate, lg, wgt, t):
    dzn, dun, xn, scn = dnext

    def fn(zv, yv, gt, g, dzv, duv, xv, sc):
        dv = ALPHA * dzv + duv * (1.0 + sc)
        dz, dg, db = _ln_bwd(zv, g, dv)
        return dz, (wgt * (1.0 + gt)) * dz, dg, db, colsum(wgt * dz * yv), colsum(duv), colsum(duv * xv)

    return rowmap(name, fn, [T_(z), T_(y), B_(gate), B_(lg), T_(dzn), T_(dun), T_(xn), B_(scn)],
                  [(D, F32), (D, BF16)], [(1, D)] * 5, rows=t, tm=256)


def residual_ln_loss_bwd(name, xin, y, tgt, gate, lg, lb, wgt, t):
    def fn(xv, yv, tv, gt, g, b):
        z = ALPHA * xv + (wgt * (1.0 + gt)) * yv
        d = _ln_fwd(z, g, b) - tv
        dz, dg, db = _ln_bwd(z, g, d * (1.0 / D))
        return dz, (wgt * (1.0 + gt)) * dz, dg, db, colsum(wgt * dz * yv), jnp.sum(d * d).reshape(1, 1)

    dz, dy, dlg, dlb, dgate, sq = rowmap(
        name, fn, [T_(xin), T_(y), T_(tgt), B_(gate), B_(lg), B_(lb)], [(D, F32), (D, BF16)],
        [(1, D), (1, D), (1, D), (1, 1)], rows=t, tm=512)
    return dz, dy, dlg, dlb, dgate, sq


def modulate_bwd(name, dz, du, xin, scale, t):
    def fn(dzv, duv, xv, sc):
        return ALPHA * dzv + duv * (1.0 + sc), colsum(duv), colsum(duv * xv)

    return rowmap(name, fn, [T_(dz), T_(du), T_(xin), B_(scale)], [(D, F32)], [(1, D), (1, D)], rows=t, tm=256)


def ffn_fwd(tag, u, wi, t, up_job, down_job=None):
    tm = min(1024, t)
    tn = 256
    per = FHP // tn

    def act(accs, _):
        a, b = accs
        s = _sigmoid(a)
        sl = a * s
        return b * (s * (1.0 + a * (1.0 - s))), sl, sl * b

    tmu = min(2048, t)
    hblk = pl.BlockSpec((tmu, tn), lambda i, j, k: (i, j))
    (ha, hb, g), up_moved = mm(
        tag + "_up", [u], [wi, wi], [(0, 0, 0), (0, 1, 1)], dims=NT, grid=(t // tmu, 2 * per, 1),
        a_specs=[pl.BlockSpec((tmu, D), lambda i, j, k: (i, 0))],
        b_specs=[pl.BlockSpec((None, tn, D), lambda i, j, k: (j // per, j % per, 0)),
                 pl.BlockSpec((None, tn, D), lambda i, j, k: (2 + j // per, j % per, 0))],
        outs=[sds((t, 2 * FHP), BF16)] * 3, out_specs=[hblk] * 3, acc_shapes=[(tmu, tn)] * 2, epilogue=act, job=up_job,
        sub_rows=tmu // 2)
    wo = up_moved[0].reshape(2 * FHP, D)
    tk = FHP
    y, down_moved = _with_moved(mm(
        tag + "_down", [g], [wo], [(0, 0, 0)], dims=NN, grid=(t // tm, 2, 2),
        a_specs=[pl.BlockSpec((tm, tk), lambda i, j, k: (i, k))],
        b_specs=[pl.BlockSpec((tk, D // 2), lambda i, j, k: (k, j))],
        outs=[sds((t, D), F32)], out_specs=[pl.BlockSpec((tm, D // 2), lambda i, j, k: (i, j))],
        acc_shapes=[(tm, D // 2)], job=down_job), down_job)
    return ha, hb, g, y, wo, up_moved, down_moved


def ffn_bwd(tag, u, ha, hb, g, dy, wi, wo, t, sp, dact_job=None, dwo_job=None, du_extra=None):
    tm = min(1024, t)

    def dact(accs, ex):
        dg = accs[0]
        return dg * ex[0].astype(F32), dg * ex[1].astype(F32)

    tn = 256
    tmu = min(2048, t)
    hblk = pl.BlockSpec((tmu, tn), lambda i, j, k: (i, j))
    (dha, dhb), dact_moved = _with_moved(mm(
        tag + "_dact", [dy], [wo], [(0, 0, 0)], dims=NT, grid=(t // tmu, 2 * FHP // tn, 1),
        a_specs=[pl.BlockSpec((tmu, D), lambda i, j, k: (i, 0))],
        b_specs=[pl.BlockSpec((tn, D), lambda i, j, k: (j, 0))],
        outs=[sds((t, 2 * FHP), BF16)] * 2, out_specs=[hblk] * 2, acc_shapes=[(tmu, tn)],
        epilogue=dact, extras=[ha, hb], extra_specs=[hblk] * 2, job=dact_job, sub_rows=tmu // 2), dact_job)
    tk = min(2048, t)
    th = FHP // 2
    dwo, dwo_moved = _with_moved(mm(
        tag + "_dwo", [g], [dy], [(0, 0, 0)], dims=TN, grid=(4, 2, t // tk),
        a_specs=[pl.BlockSpec((tk, th), lambda i, j, k: (k, i))],
        b_specs=[pl.BlockSpec((tk, D // 2), lambda i, j, k: (k, j))],
        outs=[sds((2 * FHP, D), BF16)], out_specs=[pl.BlockSpec((th, D // 2), lambda i, j, k: (i, j))],
        acc_shapes=[(th, D // 2)], job=dwo_job), dwo_job)
    dwo = dwo.reshape(2, FHP, D)

    def dwi_part(part, dh, carry, job):
        return mm(
            f"{tag}_dwi{part}", [dh], [u], [(0, 0, 0)], dims=TN, grid=(4, 2, t // tk),
            a_specs=[pl.BlockSpec((tk, th), lambda i, j, k: (k, i))],
            b_specs=[pl.BlockSpec((tk, D // 2), lambda i, j, k: (k, j))],
            outs=[sds((4, FHP, D), BF16)],
            out_specs=[pl.BlockSpec((None, th, D // 2), lambda i, j, k: (2 * part + i // 2, i % 2, j))],
            acc_shapes=[(th, D // 2)], carry=carry, job=job)

    dwi, (sib_fo,) = dwi_part(0, dha, None, reduce_sibling_job([(dwo, view_ffn_out, FO, D)]))
    q_fo = chip_sum(tag + "_chipsum_fo", dwo, sib_fo, sp, FO, FO // 2, ffn_out=True)
    dwi, (far_fo,) = dwi_part(1, dhb, dwi, reduce_chips_job([q_fo]))
    (sib_fi,) = run_job(tag + "_sibling_fi", reduce_sibling_job([(dwi, view_lead, FHP, D)]))
    q_fi = chip_sum(tag + "_chipsum_fi", dwi, sib_fi, sp, FHP, FHP // 8)
    tmd = min(512, t)
    job = reduce_chips_job([q_fi])
    if du_extra is not None:
        job = merge_jobs(job, du_extra(dact_moved, dwo_moved, q_fo, far_fo))
    du, (far_fi, *extra_moved) = mm(
        tag + "_du", [dha, dhb], [wi, wi], [(0, 0, 0), (1, 1, 0)], dims=NN, grid=(t // tmd, 2, 2),
        a_specs=[pl.BlockSpec((tmd, FHP), lambda i, j, k: (i, k))] * 2,
        b_specs=[pl.BlockSpec((None, FHP, D // 2), lambda i, j, k: (k, 0, j)),
                 pl.BlockSpec((None, FHP, D // 2), lambda i, j, k: (2 + k, 0, j))],
        outs=[sds((t, D), F32)], out_specs=[pl.BlockSpec((tmd, D // 2), lambda i, j, k: (i, j))],
        acc_shapes=[(tmd, D // 2)], job=job)
    return du, (q_fi, far_fi), (q_fo, far_fo), extra_moved


def mix_fwd(u, wts, b_in, pool_scale, sinks, tabs, t, in_job, attn_job):
    w_in, wp, wba, wbb, wo = wts
    tm = min(1024, t)
    tmh = min(512, t)
    h, in_moved = mm("mix_in", [u], [w_in], [(0, 0, 0)], dims=NN, grid=(t // tmh, 4, 1),
                     a_specs=[pl.BlockSpec((tmh, D), lambda i, j, k: (i, 0))],
                     b_specs=[pl.BlockSpec((None, D, IN_SH), lambda i, j, k: (j, 0, 0))],
                     outs=[sds((t, IN_W), F32)], out_specs=[pl.BlockSpec((tmh, IN_SH), lambda i, j, k: (i, j))],
                     acc_shapes=[(tmh, IN_SH)], job=in_job)
    attn_job = attn_job(in_moved)
    pooled = pool_fwd(h, b_in, t, 512)
    gblk = pl.BlockSpec((tm, PG), lambda i, j, k: (i, j))
    mixed = mm("mix_pool", [pooled], [wp], [(0, 0, 0)], dims=NN, grid=(t // tm, 4, 1), a_specs=[gblk],
               b_specs=[pl.BlockSpec((None, PG, PG), lambda i, j, k: (j, 0, 0))],
               outs=[sds((t, PW), F32)], out_specs=[gblk], acc_shapes=[(tm, PG)])
    pm = rowmap("mix_pscale", lambda mv, ps: mv * ps, [T_(mixed), B_(pool_scale)], [(PW, BF16)], rows=t, tm=512)

    def branch(name, a, w):
        return mm(name, [a], [w], [(0, 0, 0)], dims=NN, grid=(t // tm, 4, 1),
                  a_specs=[pl.BlockSpec((tm, PW), lambda i, j, k: (i, 0))],
                  b_specs=[pl.BlockSpec((None, PW, D // 4), lambda i, j, k: (j, 0, 0))],
                  outs=[sds((t, D), F32)], out_specs=[pl.BlockSpec((tm, D // 4), lambda i, j, k: (i, j))],
                  acc_shapes=[(tm, D // 4)])

    ya = branch("mix_branch_a", pm, wba)

    def qkv(hq, hk, hv, bq, bk, bv, cc, sa, sb):
        return (_rope(hq + bq, cc, sa, sb) * (HD ** -0.5), _rope(hk + bk, cc, sa, sb), hv + bv)

    qr, kr, vv = rowmap(
        "mix_rope", qkv,
        [T_(h, QW, 1), T_(h, KVW, 8), T_(h, KVW, 9), B_(b_in, QW, 1), B_(b_in, KVW, 8), B_(b_in, KVW, 9),
         T_(tabs[0]), T_(tabs[1]), T_(tabs[2])],
        [(QW, BF16), (KVW, BF16), (KVW, BF16)], rows=t, tm=512)
    attn, attn_moved = attn_fwd(qr, kr, vv, sinks, t, attn_job)
    yb = branch("mix_branch_b", attn, wbb)
    cw = 512

    def merge(ga, gb, ba, bb, yav, ybv):
        return _sigmoid(ga + ba) * yav + _sigmoid(gb + bb) * ybv

    merged = rowmap(
        "mix_merge", merge,
        [T_(h, cw, 5), T_(h, cw, 9), B_(b_in, cw, 5), B_(b_in, cw, 9), T_(ya, cw), T_(yb, cw)],
        [(D, BF16)], rows=t, tm=1024, ncol=D // cw)
    y = mm("mix_out", [merged], [wo], [(0, 0, 0)], dims=NN, grid=(t // tm, 2, 1),
           a_specs=[pl.BlockSpec((tm, D), lambda i, j, k: (i, 0))],
           b_specs=[pl.BlockSpec((D, D // 2), lambda i, j, k: (0, j))],
           outs=[sds((t, D), F32)], out_specs=[pl.BlockSpec((tm, D // 2), lambda i, j, k: (i, j))],
           acc_shapes=[(tm, D // 2)])
    return y, (h, pooled, mixed, pm, ya, qr, kr, vv, attn, yb, merged), attn_moved


def mix_bwd(u, saved, dy, wts, b_in, pool_scale, sinks, tabs, t):
    h, pooled, mixed, pm, ya, qr, kr, vv, attn, yb, merged = saved
    w_in, wp, wba, wbb, wo = wts
    tm = min(1024, t)
    tk = min(2048, t)
    dmerged = mm("mix_dmerged", [dy], [wo], [(0, 0, 0)], dims=NT, grid=(t // tm, 2, 1),
                 a_specs=[pl.BlockSpec((tm, D), lambda i, j, k: (i, 0))],
                 b_specs=[pl.BlockSpec((D // 2, D), lambda i, j, k: (j, 0))],
                 outs=[sds((t, D), F32)], out_specs=[pl.BlockSpec((tm, D // 2), lambda i, j, k: (i, j))],
                 acc_shapes=[(tm, D // 2)])
    half = pl.BlockSpec((tk, D // 2), lambda i, j, k: (k, i))
    dwo = mm("mix_dwo", [merged], [dy], [(0, 0, 0)], dims=TN, grid=(2, 2, t // tk), a_specs=[half],
             b_specs=[pl.BlockSpec((tk, D // 2), lambda i, j, k: (k, j))],
             outs=[sds((D, D), BF16)], out_specs=[pl.BlockSpec((D // 2, D // 2), lambda i, j, k: (i, j))],
             acc_shapes=[(D // 2, D // 2)])
    cw = 512

    def dmerge(dm, ga, gb, ba, bb, yav, ybv):
        sa_, sb_ = _sigmoid(ga + ba), _sigmoid(gb + bb)
        dga = dm * yav * sa_ * (1.0 - sa_)
        dgb = dm * ybv * sb_ * (1.0 - sb_)
        return dm * sa_, dm * sb_, dga, dgb, colsum(dga), colsum(dgb)

    dya, dyb, dgla, dglb, dbga, dbgb = rowmap(
        "mix_dmerge", dmerge,
        [T_(dmerged, cw), T_(h, cw, 5), T_(h, cw, 9), B_(b_in, cw, 5), B_(b_in, cw, 9), T_(ya, cw), T_(yb, cw)],
        [(D, BF16)] * 4, [(1, D), (1, D)], rows=t, tm=1024, ncol=D // cw)

    def dbranch(name, dyv, act, w):
        dwb = mm(name + "_dw", [act], [dyv], [(0, 0, 0)], dims=TN, grid=(1, 4, t // tk),
                 a_specs=[pl.BlockSpec((tk, PW), lambda i, j, k: (k, 0))],
                 b_specs=[pl.BlockSpec((tk, D // 4), lambda i, j, k: (k, j))],
                 outs=[sds((4, PW, D // 4), BF16)], out_specs=[pl.BlockSpec((None, PW, D // 4), lambda i, j, k: (j, 0, 0))],
                 acc_shapes=[(PW, D // 4)])
        return dwb, lambda dt: mm(
            name + "_dx", [dyv], [w], [(0, 0, 0)], dims=NT, grid=(t // tm, 1, 4),
            a_specs=[pl.BlockSpec((tm, D // 4), lambda i, j, k: (i, k))],
            b_specs=[pl.BlockSpec((None, PW, D // 4), lambda i, j, k: (k, 0, 0))],
            outs=[sds((t, PW), dt)], out_specs=[pl.BlockSpec((tm, PW), lambda i, j, k: (i, 0))], acc_shapes=[(tm, PW)])

    dwba, dpm_fn = dbranch("mix_dbranch_a", dya, pm, wba)
    dwbb, dattn_fn = dbranch("mix_dbranch_b", dyb, attn, wbb)
    dpm, dattn = dpm_fn(F32), dattn_fn(BF16)
    dmixed, dps = rowmap("mix_dpscale", lambda dp, mv, ps: (dp * ps, colsum(dp * mv)),
                         [T_(dpm), T_(mixed), B_(pool_scale)], [(PW, BF16)], [(1, PW)], rows=t, tm=512)
    gblk = pl.BlockSpec((tm, PG), lambda i, j, k: (i, j))
    dpooled = mm("mix_dpool", [dmixed], [wp], [(0, 0, 0)], dims=NT, grid=(t // tm, 4, 1), a_specs=[gblk],
                 b_specs=[pl.BlockSpec((None, PG, PG), lambda i, j, k: (j, 0, 0))],
                 outs=[sds((t, PW), F32)], out_specs=[gblk], acc_shapes=[(tm, PG)])
    kblk = pl.BlockSpec((tk, PG), lambda i, j, k: (k, i))
    dwp = mm("mix_dwpool", [pooled], [dmixed], [(0, 0, 0)], dims=TN, grid=(4, 1, t // tk), a_specs=[kblk], b_specs=[kblk],
             outs=[sds((4, PG, PG), BF16)], out_specs=[pl.BlockSpec((None, PG, PG), lambda i, j, k: (i, 0, 0))],
             acc_shapes=[(PG, PG)])
    dxp, dbxp = pool_bwd(dpooled, t, 512)
    dqr, dkr, dvv, dsinks = attn_bwd(qr, kr, vv, dattn, sinks, t)

    def dqkv(dq, dk, dv, cc, sa, sb):
        dq = _rope_t(dq, cc, sa, sb) * (HD ** -0.5)
        dk = _rope_t(dk, cc, sa, sb)
        return dq, dk, dv, colsum(dq), colsum(dk), colsum(dv)

    dq, dk, dvb, dbq, dbk, dbv = rowmap(
        "mix_rope_bwd", dqkv, [T_(dqr), T_(dkr), T_(dvv), T_(tabs[0]), T_(tabs[1]), T_(tabs[2])],
        [(QW, BF16), (KVW, BF16), (KVW, BF16)], [(1, QW), (1, KVW), (1, KVW)], rows=t, tm=512)
    dh = jnp.concatenate([dxp, dq, dk, dvb, dgla, dglb], axis=1)
    db_in = jnp.concatenate([dbxp, dbq, dbk, dbv, dbga, dbgb], axis=1)
    dwin = mm("mix_dwin", [u], [dh], [(0, 0, 0)], dims=TN, grid=(2, 4, t // tk), a_specs=[half],
              b_specs=[pl.BlockSpec((tk, IN_SH), lambda i, j, k: (k, j))],
              outs=[sds((4, D, IN_SH), BF16)], out_specs=[pl.BlockSpec((None, D // 2, IN_SH), lambda i, j, k: (j, i, 0))],
              acc_shapes=[(D // 2, IN_SH)])
    dwp_sh = jnp.transpose(dwp.reshape(4, 4, 64, PG), (1, 0, 2, 3)).reshape(4, 4 * 64, PG)
    parts = {"win": dwin, "wp": dwp_sh, "wba": dwba, "wbb": dwbb, "wo": dwo.reshape(4, D // 4, D)}
    du, sib = mm("mix_du", [dh], [w_in], [(0, 0, 0)], dims=NT, grid=(t // tm, 2, 4),
                 a_specs=[pl.BlockSpec((tm, IN_SH), lambda i, j, k: (i, k))],
                 b_specs=[pl.BlockSpec((None, D // 2, IN_SH), lambda i, j, k: (k, j, 0))],
                 outs=[sds((t, D), F32)], out_specs=[pl.BlockSpec((tm, D // 2), lambda i, j, k: (i, j))],
                 acc_shapes=[(tm, D // 2)],
                 job=reduce_sibling_job([(p, view_lead, p.shape[1], p.shape[2]) for p in parts.values()]))
    return du, parts, dict(zip(parts, sib)), db_in, dps, dsinks


def cast_shard(name, w, sp, ffn_out=False):
    rows, cols = w.shape
    if ffn_out:
        tm = rows // 2
        shape = (2, FHP, D)
        spec = pl.BlockSpec((None, tm, cols), lambda j, i, s: (s[0] // 2, (s[0] % 2) * 2 + i, 0))
    else:
        tm = rows // 4
        shape = (4, rows, cols)
        spec = pl.BlockSpec((None, tm, cols), lambda j, i, s: (s[0], i, 0))
    return rowmap(name, lambda wv: wv, [T_(w)], [(shape, BF16, spec)], rows=rows, tm=tm, sp=sp)


def cast_ffn_in(name, wt, sp):
    tm = FH // 4
    buf = rowmap(name, lambda wv: wv, [T_(wt)],
                 [((4, FHP, D), BF16, pl.BlockSpec((None, tm, D), lambda j, i, s: (s[0], i, 0)))], rows=FH, tm=tm, sp=sp)
    pad = FHP - FH

    def zero_pad(_, __, out):
        out[...] = jnp.zeros_like(out)

    return pl.pallas_call(
        zero_pad, name=name + "_pad", out_shape=sds(buf.shape, BF16), input_output_aliases={1: 0},
        grid_spec=pltpu.PrefetchScalarGridSpec(
            num_scalar_prefetch=1, grid=(1,), in_specs=[ANY],
            out_specs=pl.BlockSpec((None, pad, D), lambda i, s: (s[0], FH // pad, 0))))(sp, buf)


def chip_sum(name, dw, got, sp, rows, tm, ffn_out=False):
    hr, cols = rows // 2, got.shape[2]
    per = hr // tm
    pos = pl.BlockSpec((None, tm, cols), lambda j, i, s: (i // per, i % per, 0))
    if ffn_out:
        mine = pl.BlockSpec((None, tm, cols), lambda j, i, s: (i // 2, (i % 2) * 2 + s[1], 0))
    else:
        mine = pl.BlockSpec((None, tm, cols), lambda j, i, s: (i // per, s[1] * per + i % per, 0))
    return rowmap(name, lambda av, bv: av.astype(F32) + bv.astype(F32), [X_(dw, mine), X_(got, pos)],
                  [(got.shape, BF16, pos)], rows=4 * hr, tm=tm, sp=sp)


def chip_total(name, q, got, sp, rows, tm):
    hr, cols = rows // 2, q.shape[2]
    per = hr // tm

    def part(f):
        return X_(got, pl.BlockSpec((None, tm, cols), lambda j, i, s, f=f: (f, i, 0)))

    return rowmap(
        name, lambda av, b0, b1, b2: ((av.astype(F32) + b0.astype(F32)) + b1.astype(F32)) + b2.astype(F32),
        [X_(q, pl.BlockSpec((None, tm, cols), lambda j, i, s: (s[0], i, 0))), part(0), part(1), part(2)],
        [((rows, cols), F32, pl.BlockSpec((tm, cols), lambda j, i, s: (s[1] * per + i, 0)))], rows=hr, tm=tm, sp=sp)


def kernel(x, c, w_ada, b_ada, ln_g, ln_b, w_ffn1_in, w_ffn1_out, w_in, b_in, w_pool, pool_scale, sinks, w_branch_a, w_branch_b, w_out, w_ffn2_in, w_ffn2_out, loss_target, m_w_ada, m_b_ada, m_ln_g, m_ln_b, m_w_ffn1_in, m_w_ffn1_out, m_w_in, m_b_in, m_w_pool, m_pool_scale, m_sinks, m_w_branch_a, m_w_branch_b, m_w_out, m_w_ffn2_in, m_w_ffn2_out, v_w_ada, v_b_ada, v_ln_g, v_ln_b, v_w_ffn1_in, v_w_ffn1_out, v_w_in, v_b_in, v_w_pool, v_pool_scale, v_sinks, v_w_branch_a, v_w_branch_b, v_w_out, v_w_ffn2_in, v_w_ffn2_out):
    t = x.shape[1]
    xs, tgt = x[0], loss_target[0]
    xi, yi, ci = lax.axis_index("x"), lax.axis_index("y"), lax.axis_index("c")
    chip = 2 * xi + yi
    dev = 2 * chip + ci
    b_in2, ps2, sinks2 = b_in, pool_scale, sinks

    sp = jnp.stack([chip, ci]).astype(jnp.int32)
    tr = lambda a: jnp.swapaxes(a[0], 0, 1)

    first = jnp.concatenate([c.reshape(-1), ln_g.reshape(-1), ln_b.reshape(-1)]).reshape(-1, 128)
    first_all = allgather_small("gather_cond", first).reshape(8, -1)
    c_all = first_all[:, :D]
    ln_parts = first_all[0::2, D:].reshape(4, 2, 3, D // 4)
    ln_full = jnp.transpose(ln_parts, (1, 2, 0, 3)).reshape(2, 3, D)
    lgs = [ln_full[0, s:s + 1] for s in range(3)]
    lbs = [ln_full[1, s:s + 1] for s in range(3)]
    c16 = jnp.pad(c_all, ((0, 8), (0, 0)))
    b_ada_sh = lax.dynamic_slice(b_ada, (0, chip * ADA_SH), (1, ADA_SH))
    mod_part = ada_fwd(c16, w_ada[0], b_ada_sh)[:8]
    mod_all = allgather_small("gather_mod", mod_part.reshape(-1, 128)).reshape(8, 8, ADA_SH)
    mod_mine = lax.dynamic_index_in_dim(mod_all[0::2], dev, axis=1, keepdims=False).reshape(9, D)
    mods = [[mod_mine[3 * s + k:3 * s + k + 1] for k in range(3)] for s in range(3)]

    f1i_buf, f1i_send, f1i_recv, _ = gather_start(
        "gather_f1i_start", cast_ffn_in("cast_f1i", tr(w_ffn1_in), sp), FHP, [mod_mine])
    plain = [("f1o", w_ffn1_out[0]), ("win", w_in[0]), ("wp", w_pool[0].reshape(4 * 64, PG)), ("wba", w_branch_a[0]),
             ("wbb", w_branch_b[0]), ("wo", w_out[0]), ("f2o", w_ffn2_out[0])]
    sh = {n: cast_shard("cast_" + n, w, sp, ffn_out=n in ("f1o", "f2o")) for n, w in plain}
    sh["f1i"] = f1i_buf
    sh["f2i"] = cast_ffn_in("cast_f2i", tr(w_ffn2_in), sp)
    order = ["f1i", "f1o", "win", "wp", "wba", "wbb", "wo", "f2i", "f2o"]
    views = {n: (view_ffn_out if n in ("f1o", "f2o") else view_lead) for n in order}
    shard_rows = {n: (FO if n in ("f1o", "f2o") else sh[n].shape[1]) for n in order}
    shard_cols = {n: sh[n].shape[2] for n in order}
    tiles = {"f1i": FHP // 8, "f1o": FO // 2, "win": 512, "wp": 128, "wba": 512, "wbb": 512, "wo": 256,
             "f2i": FHP // 8, "f2o": FO // 2}

    def item(n, part=0, parts=1):
        return (sh[n], views[n], shard_rows[n], part, parts)

    tabs = rope_tables(t)
    (sh0, sc0, gt0), (sh1, sc1, gt1), (sh2, sc2, gt2) = mods

    u0 = modulate("ffn1_mod", xs, sh0, sc0, t)
    landed = gather_wait("gather_f1i_wait", f1i_buf, f1i_send, f1i_recv, FHP,
                         [u0] + [sh[n] for n in order if n != "f1i"])
    (g_f1i,) = run_job("gather_f1i_forward", forward_job(landed, FHP))
    ha1, hb1, g1, y1, f1o, (_, g_win), (g_wp, g_wba, g_wbb, g_wo) = ffn_fwd(
        "ffn1", u0, g_f1i, t, gather_job([item("f1o"), item("win")]),
        gather_job([item(n) for n in ("wp", "wba", "wbb", "wo")]))
    x1, z1, u1 = residual_ln_mod("ffn1_ln", xs, y1, gt0, lgs[0], lbs[0], 0.5, sh1, sc1, t)
    wp_full = jnp.transpose(g_wp.reshape(4, 4, 64, PG), (1, 0, 2, 3)).reshape(4, PG, PG)
    wts = (g_win, wp_full, g_wba, g_wbb, g_wo.reshape(D, D))
    y2, sv2, (g_f2i,) = mix_fwd(
        u1, wts, b_in2, ps2, sinks2, tabs, t, gather_job([item("f2i", 0, 2)]),
        lambda moved: gather_job([(moved[0], view_lead, FHP, 1, 2)]))
    x2, z2, u2 = residual_ln_mod("mix_ln", x1, y2, gt1, lgs[1], lbs[1], 1.0, sh2, sc2, t)
    ha3, hb3, g3, y3, f2o, _, _ = ffn_fwd("ffn2", u2, g_f2i, t, gather_job([item("f2o")]))

    dz3, dy3, dlg2, dlb2, dgt2, sq = residual_ln_loss_bwd("ffn2_ln_loss", x2, y3, tgt, gt2, lgs[2], lbs[2], 0.5, t)
    loss = lax.psum(0.5 * sq[0, 0] / D, ("x", "y", "c"))
    du3, red_f2i, red_f2o, _ = ffn_bwd("ffn2", u2, ha3, hb3, g3, dy3, g_f2i, f2o, t, sp)
    dz2, dy2, dlg1, dlb1, dgt1, dsh2, dsc2 = residual_ln_bwd("mix_ln_bwd", z2, (dz3, du3, x2, sc2), y2, gt1, lgs[1], 1.0, t)
    du2, mix_parts, sib, db_in, dps, dsinks = mix_bwd(u1, sv2, dy2, wts, b_in2, ps2, sinks2, tabs, t)
    q = {n: chip_sum("chipsum_" + n, mix_parts[n], sib[n], sp, shard_rows[n], tiles[n]) for n in mix_parts}
    dz1, dy1, dlg0, dlb0, dgt0, dsh1, dsc1 = residual_ln_bwd("ffn1_ln_bwd", z1, (dz2, du2, x1, sc1), y1, gt0, lgs[0], 0.5, t)
    early = [n for n in order if n != "f1i"]

    def total(n, q_n, far_n):
        return chip_total("total_" + n, q_n, far_n, sp, shard_rows[n], tiles[n])

    def swap_early(far_a, far_b, q_f1o, far_f1o):
        reduced = {"f1o": (q_f1o, far_f1o), "f2i": red_f2i, "f2o": red_f2o, "win": (q["win"], far_a[0]),
                   "wp": (q["wp"], far_a[1]), "wo": (q["wo"], far_b[0]), "wba": (q["wba"], far_b[1]),
                   "wbb": (q["wbb"], far_b[2])}
        return share_halves_job([total(n, *reduced[n]) for n in early])

    du1, red_f1i, _, shared_early = ffn_bwd(
        "ffn1", u0, ha1, hb1, g1, dy1, g_f1i, f1o, t, sp,
        reduce_chips_job([q["win"], q["wp"]]), reduce_chips_job([q["wo"], q["wba"], q["wbb"]]), swap_early)
    dx0, dsh0, dsc0 = modulate_bwd("ffn1_mod_bwd", dz1, du1, xs, sc0, t)
    gm0, gm1, gm2 = (dsh0, dsc0, dgt0), (dsh1, dsc1, dgt1), (dsh2, dsc2, dgt2)
    gw = dict(zip(early, shared_early))
    (gw["f1i"],) = run_job("share_f1i", share_halves_job([total("f1i", *red_f1i)]))

    small = jnp.concatenate([*gm0, *gm1, *gm2, dlg0, dlg1, dlg2, dlb0, dlb1, dlb2, db_in, dps, dsinks], axis=1)
    n_small = small.shape[1]
    rows_small = -(-n_small // 1024) * 8
    small = jnp.pad(small, ((0, 0), (0, rows_small * 128 - n_small))).reshape(rows_small, 128)
    small_all = allgather_small("gather_small", small)
    tot = sum_devices(small_all).reshape(1, -1)
    gmod_all = small_all.reshape(8, -1)[:, :9 * D]
    o = 9 * D
    g_b_ada = tot[:, :o]
    g_ln_g = lax.dynamic_slice(tot[:, o:o + 3 * D].reshape(3, D), (0, chip * (D // 4)), (3, D // 4))
    g_ln_b = lax.dynamic_slice(tot[:, o + 3 * D:o + 6 * D].reshape(3, D), (0, chip * (D // 4)), (3, D // 4))
    o += 6 * D
    g_b_in, g_ps, g_sinks = tot[:, o:o + IN_W], tot[:, o + IN_W:o + IN_W + PW], tot[:, o + IN_W + PW:o + IN_W + PW + N_Q]

    gm16 = jnp.pad(lax.dynamic_slice(gmod_all, (0, chip * ADA_SH), (8, ADA_SH)), ((0, 8), (0, 0)))
    (g_w_ada, d_w_ada, nm_w_ada, nv_w_ada), _ = ada_bwd_adam(c16, gm16, w_ada[0], m_w_ada[0], v_w_ada[0], None)

    def big(n, w, m, v, tm):
        shape = w.shape
        w2, m2, v2 = (a.reshape(shape[-2] if a.ndim == 3 else -1, shape[-1]) for a in (w, m, v))
        return [r.reshape(shape) for r in adam_rows("adam_" + n, w2, gw[n], m2, v2, tm)]

    def big_t(n, w, m, v):
        return [jnp.swapaxes(r, 0, 1)[None] for r in adam_rows("adam_" + n, tr(w), gw[n], tr(m), tr(v), FH // 8)]

    def tiny(n, w, g, m, v):
        return [g.reshape(w.shape)] + list(adam_small("adam_" + n, w, g.reshape(w.shape), m, v))

    res = {
        "w_ada": [a[None] for a in (g_w_ada, d_w_ada, nm_w_ada, nv_w_ada)],
        "b_ada": tiny("b_ada", b_ada, g_b_ada, m_b_ada, v_b_ada),
        "ln_g": tiny("ln_g", ln_g, g_ln_g, m_ln_g, v_ln_g),
        "ln_b": tiny("ln_b", ln_b, g_ln_b, m_ln_b, v_ln_b),
        "w_ffn1_in": big_t("f1i", w_ffn1_in, m_w_ffn1_in, v_w_ffn1_in),
        "w_ffn1_out": big("f1o", w_ffn1_out, m_w_ffn1_out, v_w_ffn1_out, FO // 4),
        "w_in": big("win", w_in, m_w_in, v_w_in, 256),
        "b_in": tiny("b_in", b_in, g_b_in, m_b_in, v_b_in),
        "w_pool": big("wp", w_pool, m_w_pool, v_w_pool, 256),
        "pool_scale": tiny("pool_scale", pool_scale, g_ps, m_pool_scale, v_pool_scale),
        "sinks": tiny("sinks", sinks, g_sinks, m_sinks, v_sinks),
        "w_branch_a": big("wba", w_branch_a, m_w_branch_a, v_w_branch_a, 512),
        "w_branch_b": big("wbb", w_branch_b, m_w_branch_b, v_w_branch_b, 512),
        "w_out": big("wo", w_out, m_w_out, v_w_out, 128),
        "w_ffn2_in": big_t("f2i", w_ffn2_in, m_w_ffn2_in, v_w_ffn2_in),
        "w_ffn2_out": big("f2o", w_ffn2_out, m_w_ffn2_out, v_w_ffn2_out, FO // 4),
    }
    names = ["w_ada", "b_ada", "ln_g", "ln_b", "w_ffn1_in", "w_ffn1_out", "w_in", "b_in", "w_pool", "pool_scale", "sinks",
             "w_branch_a", "w_branch_b", "w_out", "w_ffn2_in", "w_ffn2_out"]
    return (loss, dx0[None], *[res[n][0] for n in names], *[res[n][1] for n in names],
            *[res[n][2] for n in names], *[res[n][3] for n in names])
```

```python
import jax
import jax.numpy as jnp
from jax import lax
from jax.experimental import pallas as pl
from jax.experimental.pallas import tpu as pltpu

F32 = jnp.float32
BF16 = jnp.bfloat16
MESH = pl.DeviceIdType.MESH
ANY = pl.BlockSpec(memory_space=pl.ANY)

D = 2048
N_Q, N_KV, HD = 16, 4, 64
QW, KVW = N_Q * HD, N_KV * HD
BLK = 128
POOL_WINDOWS = (2, 4, 8, 16)
PW, PG = 1024, 256
HALO = 16
ROPE_THETA = 500000.0
ROT = HD // 4
LN_EPS = 1e-5
ALPHA = 2.0 ** 0.25
FH = 2752
FHP = 2816
FO = 1376
IN_W = 6656
IN_SH = IN_W // 4
ADA_SH = 18432 // 4
B1, B2, LR, EPS, WD, STEP = 0.9, 0.999, 0.001, 1e-08, 0.01, 10
VMEM_LIMIT = 56 * 1024 * 1024
FLIPS = ((1, 0), (0, 1), (1, 1))
NN = (((1,), (0,)), ((), ()))
NT = (((1,), (1,)), ((), ()))
TN = (((0,), (0,)), ((), ()))


def _params(sem):
    return pltpu.CompilerParams(dimension_semantics=sem, vmem_limit_bytes=VMEM_LIMIT)


def _aligned(v, m):
    return v if isinstance(v, int) else pl.multiple_of(v, m)


def _sigmoid(v):
    return 1.0 / (1.0 + jnp.exp(-v))


def T_(arr, width=None, off=0):
    return ("t", arr, width, off)


def B_(arr, width=None, off=0):
    return ("b", arr, width, off)


def X_(arr, spec):
    return ("x", arr, spec, 0)


def rowmap(name, fn, ins, outs, accs=(), *, rows, tm, ncol=1, with_ids=False, sp=None, alias=None):
    tm = min(tm, rows)
    nrow = rows // tm
    in_specs, arrs = [], []
    for kind, arr, width, off in ins:
        if kind == "x":
            in_specs.append(width)
        elif kind == "t":
            w = arr.shape[1] if width is None else width
            in_specs.append(pl.BlockSpec((tm, w), lambda j, i, *_, off=off: (i, off + j)))
        else:
            w = arr.shape[1] if width is None else width
            in_specs.append(pl.BlockSpec((arr.shape[0], w), lambda j, i, *_, off=off: (0, off + j)))
        arrs.append(arr)
    out_shape, out_specs = [], []
    for o in outs:
        if len(o) == 3:
            out_shape.append(jax.ShapeDtypeStruct(o[0], o[1]))
            out_specs.append(o[2])
        else:
            out_shape.append(jax.ShapeDtypeStruct((rows, o[0]), o[1]))
            out_specs.append(pl.BlockSpec((tm, o[0] // ncol), lambda j, i, *_: (i, j)))
    for r, width in accs:
        out_shape.append(jax.ShapeDtypeStruct((r, width), F32))
        out_specs.append(pl.BlockSpec((r, width // ncol), lambda j, i, *_: (0, j)))
    ni, no = len(ins), len(outs)
    nsp = 0 if sp is None else 1

    def body(*refs):
        refs = refs[nsp:]
        i = pl.program_id(1)
        vals = [r[...] for r in refs[:ni]]
        res = fn(pl.program_id(0), i, *vals) if with_ids else fn(*vals)
        if not isinstance(res, (tuple, list)):
            res = (res,)
        for r, v in zip(refs[ni:ni + no], res[:no]):
            r[...] = v.astype(r.dtype)
        for r, v in zip(refs[ni + no:], res[no:]):
            @pl.when(i == 0)
            def _(r=r, v=v):
                r[...] = v

            @pl.when(i > 0)
            def _(r=r, v=v):
                r[...] += v

    grid_spec = pltpu.PrefetchScalarGridSpec(num_scalar_prefetch=nsp, grid=(ncol, nrow), in_specs=in_specs,
                                             out_specs=out_specs)
    res = pl.pallas_call(
        body, name=name, grid_spec=grid_spec, out_shape=out_shape,
        input_output_aliases={nsp + k: v for k, v in (alias or {}).items()},
        compiler_params=_params(("arbitrary", "arbitrary")),
    )(*([sp] if nsp else []), *arrs)
    return res[0] if len(res) == 1 else res


def colsum(v):
    return jnp.sum(v, axis=0, keepdims=True)


def mm(name, a_ops, b_ops, ops, *, dims, grid, a_specs, b_specs, outs, out_specs, acc_shapes,
       epilogue=None, extras=(), extra_specs=(), carry=None, job=None, sub_rows=None):
    gk = grid[2]
    na, nb, ne, nacc = len(a_ops), len(b_ops), len(extras), len(acc_shapes)
    nc = 0 if carry is None else 1
    no = len(outs)

    def body(*refs):
        a_refs = refs[:na]
        b_refs = refs[na:na + nb]
        e_refs = refs[na + nb:na + nb + ne]
        o_refs = refs[na + nb + ne + nc:na + nb + ne + nc + no]
        acc_refs = refs[na + nb + ne + nc + no:]
        k = pl.program_id(2)

        def partials(rows=slice(None)):
            res = [None] * nacc
            for ai, bi, ci in ops:
                p = lax.dot_general(a_refs[ai][rows], b_refs[bi][...], dims, preferred_element_type=F32)
                res[ci] = p if res[ci] is None else res[ci] + p
            return res

        def finish(accs, rows=slice(None)):
            outv = epilogue(accs, [e[rows] for e in e_refs]) if epilogue else (accs[0],)
            for o, v in zip(o_refs, outv):
                o[rows] = v.astype(o.dtype)

        if gk == 1 and sub_rows:
            for s in range(out_specs[0].block_shape[-2] // sub_rows):
                rows = pl.ds(s * sub_rows, sub_rows)
                finish(partials(rows), rows)
        elif gk == 1:
            finish(partials())
        else:
            ps = partials()

            @pl.when(k == 0)
            def _():
                for acc, p in zip(acc_refs, ps):
                    acc[...] = p

            @pl.when((k > 0) & (k < gk - 1))
            def _():
                for acc, p in zip(acc_refs, ps):
                    acc[...] += p

            @pl.when(k == gk - 1)
            def _():
                finish([acc[...] + p for acc, p in zip(acc_refs, ps)])

    res, moved = carried_call(
        body, name, grid,
        list(a_specs) + list(b_specs) + list(extra_specs) + ([ANY] if nc else []), list(out_specs), list(outs),
        [pltpu.VMEM(s, F32) for s in acc_shapes] if gk > 1 else [],
        [*a_ops, *b_ops, *extras, *([carry] if nc else [])], {na + nb + ne: 0} if nc else {}, job)
    res = res[0] if len(res) == 1 else res
    return res if job is None else (res, moved)


def sds(shape, dt):
    return jax.ShapeDtypeStruct(shape, dt)


class Job:
    def __init__(self, ins, outs, aliases, scratch, start, mid, finish):
        self.ins, self.outs, self.aliases, self.scratch = list(ins), list(outs), dict(aliases), list(scratch)
        self.start, self.mid, self.finish = start, mid, finish


def carried_call(body, name, grid, in_specs, out_specs, out_shape, scratch, args, aliases, job, mid_at=0.9):
    sem = ("arbitrary",) * len(grid)
    if job is None:
        res = pl.pallas_call(body, name=name, grid=grid, in_specs=in_specs, out_specs=out_specs, out_shape=out_shape,
                             scratch_shapes=scratch, input_output_aliases=aliases, compiler_params=_params(sem))(*args)
        return list(res), []
    ni, no, ns = len(in_specs), len(out_specs), len(scratch)
    ci, co = len(job.ins), len(job.outs)
    total = 1
    for g in grid:
        total *= g
    mid_step = min(max(int(total * mid_at), 1), total - 1)

    def full(*refs):
        ins, cins = refs[:ni], refs[ni:ni + ci]
        outs, couts = refs[ni + ci:ni + ci + no], refs[ni + ci + no:ni + ci + no + co]
        scr, cscr = refs[ni + ci + no + co:ni + ci + no + co + ns], refs[ni + ci + no + co + ns:]
        step = 0
        for d, g in enumerate(grid):
            step = step * g + pl.program_id(d)

        @pl.when(step == 0)
        def _():
            job.start(cins, couts, cscr)

        body(*ins, *outs, *scr)

        @pl.when(step == mid_step)
        def _():
            job.mid(cins, couts, cscr)

        @pl.when(step == total - 1)
        def _():
            job.finish(cins, couts, cscr)

    al = dict(aliases)
    al.update({ni + k: no + v for k, v in job.aliases.items()})
    res = pl.pallas_call(
        full, name=name, grid=grid, in_specs=in_specs + [ANY] * ci, out_specs=out_specs + [ANY] * co,
        out_shape=out_shape + job.outs, scratch_shapes=scratch + job.scratch, input_output_aliases=al,
        compiler_params=_params(sem))(*args, *job.ins)
    return list(res[:no]), list(res[no:])


def merge_jobs(a, b):
    ni, no, ns = len(a.ins), len(a.outs), len(a.scratch)

    def both(fa, fb):
        def run(ins, outs, scr):
            fa(ins[:ni], outs[:no], scr[:ns])
            fb(ins[ni:], outs[no:], scr[ns:])
        return run

    aliases = dict(a.aliases)
    aliases.update({ni + k: no + v for k, v in b.aliases.items()})
    return Job(a.ins + b.ins, a.outs + b.outs, aliases, a.scratch + b.scratch,
               both(a.start, b.start), both(a.mid, b.mid), both(a.finish, b.finish))


def _with_moved(res, job):
    return res if job is not None else (res, [])


def run_job(name, job):
    ci = len(job.ins)

    def body(*refs):
        cins, couts, cscr = refs[:ci], refs[ci:ci + len(job.outs)], refs[ci + len(job.outs):]
        job.start(cins, couts, cscr)
        job.mid(cins, couts, cscr)
        job.finish(cins, couts, cscr)

    return list(pl.pallas_call(
        body, name=name, in_specs=[ANY] * ci, out_specs=[ANY] * len(job.outs), out_shape=job.outs,
        scratch_shapes=job.scratch, input_output_aliases=job.aliases)(*job.ins))


def _place():
    x, y, c = lax.axis_index("x"), lax.axis_index("y"), lax.axis_index("c")
    chips = [((1 - x) if fx else x, (1 - y) if fy else y) for fx, fy in FLIPS]
    return x, y, c, chips


def allgather_small(name, v):
    r = v.shape[0]

    def body(x_ref, out_ref, send_sems, recv_sems, local_sem):
        x, y, c, chips = _place()
        me, sibling = (x, y, c), (x, y, 1 - c)

        def rows(px, py, pc):
            return out_ref.at[4 * px + 2 * py + pc]

        def copy(k, block, to, src=None):
            return pltpu.make_async_remote_copy(
                src_ref=rows(*block) if src is None else src, dst_ref=rows(*block),
                send_sem=send_sems.at[k], recv_sem=recv_sems.at[k], device_id=to, device_id_type=MESH)

        mine = pltpu.make_async_copy(x_ref, rows(*me), local_sem)
        mine.start()
        first = [copy(0, me, sibling, src=x_ref)]
        first += [copy(1 + j, me, (*chip, c), src=x_ref) for j, chip in enumerate(chips)]
        for cp in first:
            cp.start()
        passed = [copy(4 + j, (*chip, c), sibling) for j, chip in enumerate(chips)]
        for j, chip in enumerate(chips):
            copy(1 + j, (*chip, c), me).wait_recv()
            passed[j].start()
        copy(0, sibling, me).wait_recv()
        for j, chip in enumerate(chips):
            copy(4 + j, (*chip, 1 - c), me).wait_recv()
        for cp in first + passed:
            cp.wait_send()
        mine.wait()

    return pl.pallas_call(
        body, name=name, out_shape=sds((8, r, 128), v.dtype),
        in_specs=[pl.BlockSpec(memory_space=pltpu.VMEM)], out_specs=pl.BlockSpec(memory_space=pltpu.VMEM),
        scratch_shapes=[pltpu.SemaphoreType.DMA((7,)), pltpu.SemaphoreType.DMA((7,)), pltpu.SemaphoreType.DMA],
    )(v)


def _half(ref, rows, hf):
    hr = rows // 2
    return ref.at[pl.ds(_aligned(hf * hr, 16), hr)]


def view_lead(ref, p):
    return ref.at[p]


def view_ffn_out(ref, p):
    return ref.at[p // 2, pl.ds(_aligned((p % 2) * FO, 16), FO)]


def _remote(ref, dst, send_sems, recv_sems, idx, to):
    return pltpu.make_async_remote_copy(src_ref=ref, dst_ref=dst, send_sem=send_sems.at[idx], recv_sem=recv_sems.at[idx],
                                        device_id=to, device_id_type=MESH)


def gather_job(items):
    nw = len(items)
    pads = [w for w, it in enumerate(items) if it[1] is view_ffn_out]

    def piece(ref, w, p, hf):
        _, view, rws, part, parts = items[w]
        pr = rws // 2 // parts
        return view(ref, p).at[pl.ds(_aligned(hf * (rws // 2) + part * pr, 16), pr)]

    def pad_copies(outs, scr):
        return [pltpu.make_async_copy(scr[2], outs[w].at[h, pl.ds(2 * FO, FHP - 2 * FO)], scr[3].at[2 * n + h])
                for n, w in enumerate(pads) for h in range(2)]

    def start(_, outs, scr):
        x, y, c, chips = _place()
        if pads:
            scr[2][...] = jnp.zeros_like(scr[2])
            for cp in pad_copies(outs, scr):
                cp.start()
        for w in range(nw):
            mine = piece(outs[w], w, 2 * x + y, c)
            for f, (px, py) in enumerate(chips):
                _remote(mine, mine, scr[0], scr[1], (w, f), (px, py, c)).start()

    def mid(_, outs, scr):
        x, y, c, chips = _place()
        for w in range(nw):
            for f, (px, py) in enumerate(chips):
                land = piece(outs[w], w, 2 * px + py, c)
                _remote(land, land, scr[0], scr[1], (w, f), (px, py, c)).wait_recv()
                _remote(land, land, scr[0], scr[1], (w, 3 + f), (x, y, 1 - c)).start()

    def finish(_, outs, scr):
        x, y, c, chips = _place()
        for w in range(nw):
            for f, (px, py) in enumerate(chips):
                land = piece(outs[w], w, 2 * px + py, 1 - c)
                _remote(land, land, scr[0], scr[1], (w, 3 + f), (x, y, 1 - c)).wait_recv()
        for w in range(nw):
            mine = piece(outs[w], w, 2 * x + y, c)
            for f in range(6):
                _remote(mine, mine, scr[0], scr[1], (w, f), (x, y, 1 - c)).wait_send()
        for cp in pad_copies(outs, scr):
            cp.wait()

    scratch = [pltpu.SemaphoreType.DMA((nw, 6)), pltpu.SemaphoreType.DMA((nw, 6))]
    if pads:
        scratch += [pltpu.VMEM((FHP - 2 * FO, D), BF16), pltpu.SemaphoreType.DMA((2 * len(pads),))]
    bufs = [it[0] for it in items]
    return Job(bufs, [sds(b.shape, BF16) for b in bufs], {w: w for w in range(nw)}, scratch, start, mid, finish)


HBM = pl.BlockSpec(memory_space=pltpu.HBM)
SEM = pl.BlockSpec(memory_space=pltpu.SEMAPHORE)
SPLIT = pltpu.CompilerParams(has_side_effects=pltpu.SideEffectType.DATAFLOW_SIDE_EFFECTING)


def gather_start(name, buf, rows, after):
    def body(*refs):
        out, send_sems, recv_sems, token = refs[1 + len(after):]
        x, y, c, chips = _place()
        mine = _half(out.at[2 * x + y], rows, c)
        for f, (px, py) in enumerate(chips):
            _remote(mine, mine, send_sems, recv_sems, f, (px, py, c)).start()
        token[...] = jnp.zeros_like(token)

    return pl.pallas_call(
        body, name=name,
        out_shape=(pltpu.HBM(buf.shape, buf.dtype), pltpu.SemaphoreType.DMA((3,)), pltpu.SemaphoreType.DMA((3,)),
                   sds((8, 128), F32)),
        in_specs=(HBM,) + (ANY,) * len(after), out_specs=(HBM, SEM, SEM, pl.BlockSpec(memory_space=pltpu.VMEM)),
        input_output_aliases={0: 0}, compiler_params=SPLIT)(pltpu.with_memory_space_constraint(buf, pltpu.HBM), *after)


def gather_wait(name, buf, send_sems, recv_sems, rows, after):
    def body(_, send_sems, recv_sems, *rest):
        out = rest[-1]
        x, y, c, chips = _place()
        mine = _half(out.at[2 * x + y], rows, c)
        for f, (px, py) in enumerate(chips):
            cp = _remote(mine, _half(out.at[2 * px + py], rows, c), send_sems, recv_sems, f, (px, py, c))
            cp.wait_send()
            cp.wait_recv()

    return pl.pallas_call(
        body, name=name, out_shape=pltpu.HBM(buf.shape, buf.dtype),
        in_specs=(HBM, SEM, SEM) + (ANY,) * len(after), out_specs=HBM, input_output_aliases={0: 0},
        compiler_params=SPLIT)(buf, send_sems, recv_sems, *after)


def forward_job(buf, rows):
    def copies(outs, scr, hf):
        x, y, c, chips = _place()
        half = c if hf == 0 else 1 - c
        return [_remote(_half(outs[0].at[2 * px + py], rows, half), _half(outs[0].at[2 * px + py], rows, half),
                        scr[0], scr[1], f, (x, y, 1 - c)) for f, (px, py) in enumerate(chips)]

    def start(_, outs, scr):
        for cp in copies(outs, scr, 0):
            cp.start()

    def finish(_, outs, scr):
        for cp in copies(outs, scr, 1):
            cp.wait_recv()
        for cp in copies(outs, scr, 0):
            cp.wait_send()

    return Job([buf], [sds(buf.shape, buf.dtype)], {0: 0},
               [pltpu.SemaphoreType.DMA((3,)), pltpu.SemaphoreType.DMA((3,))], start, lambda *_: None, finish)


def reduce_sibling_job(items):
    nw = len(items)

    def copies(ins, got, scr):
        x, y, c, _ = _place()
        return [_remote(_half(view(ins[w], p), rws, 1 - c), got[w].at[p], scr[0], scr[1], (w, p), (x, y, 1 - c))
                for w, (_, view, rws, _) in enumerate(items) for p in range(4)]

    def start(ins, got, scr):
        for cp in copies(ins, got, scr):
            cp.start()

    def finish(ins, got, scr):
        for cp in copies(ins, got, scr):
            cp.wait()

    return Job([it[0] for it in items], [sds((4, it[2] // 2, it[3]), BF16) for it in items], {},
               [pltpu.SemaphoreType.DMA((nw, 4)), pltpu.SemaphoreType.DMA((nw, 4))], start, lambda *_: None, finish)


def reduce_chips_job(qs):
    nw = len(qs)

    def copies(ins, got, scr):
        x, y, c, chips = _place()
        return [_remote(ins[w].at[2 * px + py], got[w].at[f], scr[0], scr[1], (w, f), (px, py, c))
                for w in range(nw) for f, (px, py) in enumerate(chips)]

    def start(ins, got, scr):
        for cp in copies(ins, got, scr):
            cp.start()

    def finish(ins, got, scr):
        for cp in copies(ins, got, scr):
            cp.wait()

    return Job(qs, [sds((3,) + q.shape[1:], BF16) for q in qs], {},
               [pltpu.SemaphoreType.DMA((nw, 3)), pltpu.SemaphoreType.DMA((nw, 3))], start, lambda *_: None, finish)


def share_halves_job(gs):
    nw = len(gs)

    def start(_, outs, scr):
        x, y, c, _ = _place()
        for w in range(nw):
            mine = _half(outs[w], gs[w].shape[0], c)
            _remote(mine, mine, scr[0], scr[1], w, (x, y, 1 - c)).start()

    def finish(_, outs, scr):
        x, y, c, _ = _place()
        for w in range(nw):
            mine = _half(outs[w], gs[w].shape[0], c)
            theirs = _half(outs[w], gs[w].shape[0], 1 - c)
            _remote(mine, mine, scr[0], scr[1], w, (x, y, 1 - c)).wait_send()
            _remote(theirs, theirs, scr[0], scr[1], w, (x, y, 1 - c)).wait_recv()

    return Job(gs, [sds(g.shape, F32) for g in gs], {w: w for w in range(nw)},
               [pltpu.SemaphoreType.DMA((nw,)), pltpu.SemaphoreType.DMA((nw,))], start, lambda *_: None, finish)


def rope_tables(t):
    pos = jnp.arange(t, dtype=F32)
    inv_freq = ROPE_THETA ** (-jnp.arange(0, ROT, 2, dtype=F32) / ROT)
    ang = pos[:, None] * inv_freq[None, :]
    cos, sin = jnp.cos(ang), jnp.sin(ang)
    d = jnp.arange(128) % HD
    half = ROT // 2
    cs = jnp.take(cos, d % half, axis=1)
    sn = jnp.take(sin, d % half, axis=1)
    cc = jnp.where(d[None] < ROT, cs, 1.0)
    sa = jnp.where(d[None] < half, -sn, 0.0)
    sb = jnp.where((d[None] >= half) & (d[None] < ROT), sn, 0.0)
    return cc, sa, sb


def _rope(v, cc, sa, sb):
    w = v.shape[1]
    reps = w // 128
    half = ROT // 2
    return (v * jnp.tile(cc, (1, reps)) + pltpu.roll(v, w - half, 1) * jnp.tile(sa, (1, reps))
            + pltpu.roll(v, half, 1) * jnp.tile(sb, (1, reps)))


def _rope_t(dv, cc, sa, sb):
    w = dv.shape[1]
    reps = w // 128
    half = ROT // 2
    return (dv * jnp.tile(cc, (1, reps)) + pltpu.roll(dv * jnp.tile(sa, (1, reps)), half, 1)
            + pltpu.roll(dv * jnp.tile(sb, (1, reps)), w - half, 1))


def pool_fwd(h, b_in, t, tm):
    tm = min(tm, t)
    per = tm // HALO

    def body(prev_ref, cur_ref, b_ref, o_ref, xx):
        i = pl.program_id(0)
        b = b_ref[...]
        xx[pl.ds(0, HALO), :] = jnp.where(i > 0, prev_ref[...] + b, 0.0)
        xx[pl.ds(HALO, tm), :] = cur_ref[...] + b
        tpos = i * tm + lax.broadcasted_iota(jnp.int32, (tm, PG), 0) + 1
        for gi, w in enumerate(POOL_WINDOWS):
            cols = pl.ds(gi * PG, PG)
            acc = xx[pl.ds(HALO, tm), cols]
            for s in range(1, w):
                acc = acc + xx[pl.ds(HALO - s, tm), cols]
            cnt = jnp.minimum(tpos, w).astype(F32)
            o_ref[:, cols] = (acc / cnt - xx[pl.ds(HALO, tm), cols]).astype(o_ref.dtype)

    return pl.pallas_call(
        body, name="pool_fwd", grid=(t // tm,),
        in_specs=[pl.BlockSpec((HALO, PW), lambda i: (jnp.maximum(i * per - 1, 0), 0)),
                  pl.BlockSpec((tm, PW), lambda i: (i, 0)), pl.BlockSpec((1, PW), lambda i: (0, 0))],
        out_specs=pl.BlockSpec((tm, PW), lambda i: (i, 0)), out_shape=sds((t, PW), BF16),
        scratch_shapes=[pltpu.VMEM((tm + HALO, PW), F32)], compiler_params=_params(("arbitrary",)),
    )(h, h, b_in)


def pool_bwd(dpooled, t, tm):
    tm = min(tm, t)
    per = tm // HALO
    nt = t // tm

    def body(cur_ref, nxt_ref, o_ref, db_ref, ee):
        i = pl.program_id(0)
        tpos = i * tm + lax.broadcasted_iota(jnp.int32, (tm, PG), 0) + 1
        for gi, w in enumerate(POOL_WINDOWS):
            cols = pl.ds(gi * PG, PG)
            ee[pl.ds(0, tm), cols] = cur_ref[:, cols] / jnp.minimum(tpos, w).astype(F32)
            ee[pl.ds(tm, HALO), cols] = jnp.where(i < nt - 1, nxt_ref[:, cols] / float(w), 0.0)
        for gi, w in enumerate(POOL_WINDOWS):
            cols = pl.ds(gi * PG, PG)
            acc = ee[pl.ds(0, tm), cols]
            for s in range(1, w):
                acc = acc + ee[pl.ds(s, tm), cols]
            dxp = acc - cur_ref[:, cols]
            o_ref[:, cols] = dxp.astype(o_ref.dtype)
            part = colsum(dxp)

            @pl.when(i == 0)
            def _(cols=cols, part=part):
                db_ref[:, cols] = part

            @pl.when(i > 0)
            def _(cols=cols, part=part):
                db_ref[:, cols] += part

    return pl.pallas_call(
        body, name="pool_bwd", grid=(nt,),
        in_specs=[pl.BlockSpec((tm, PW), lambda i: (i, 0)),
                  pl.BlockSpec((HALO, PW), lambda i: (jnp.minimum((i + 1) * per, t // HALO - 1), 0))],
        out_specs=[pl.BlockSpec((tm, PW), lambda i: (i, 0)), pl.BlockSpec((1, PW), lambda i: (0, 0))],
        out_shape=[sds((t, PW), BF16), sds((1, PW), F32)],
        scratch_shapes=[pltpu.VMEM((tm + HALO, PW), F32)], compiler_params=_params(("arbitrary",)),
    )(dpooled, dpooled)


def _scores(qh, kp, kc, mask_p, mask_c, sink):
    sp = jnp.where(mask_p, lax.dot_general(qh, kp, NT, preferred_element_type=F32), -1e30)
    sc = jnp.where(mask_c, lax.dot_general(qh, kc, NT, preferred_element_type=F32), -1e30)
    m = jnp.maximum(jnp.maximum(jnp.max(sp, axis=-1, keepdims=True), jnp.max(sc, axis=-1, keepdims=True)), sink)
    pp, pc = jnp.exp(sp - m), jnp.exp(sc - m)
    es = jnp.exp(sink - m)
    inv = 1.0 / (jnp.sum(pp, axis=-1, keepdims=True) + jnp.sum(pc, axis=-1, keepdims=True) + es)
    return pp * inv, pc * inv, es * inv


GRP = N_Q // N_KV


def _masks(n):
    qi = lax.broadcasted_iota(jnp.int32, (GRP * BLK, BLK), 0) % BLK
    kj = lax.broadcasted_iota(jnp.int32, (GRP * BLK, BLK), 1)
    return (kj > qi) & (n > 0), kj <= qi


def _head(hk, g):
    return pl.ds(HD * (GRP * hk + g), HD)


def _stack_heads(ref, hk):
    return jnp.concatenate([ref[:, _head(hk, g)] for g in range(GRP)], axis=0)


def _stack_sinks(s_ref, hk):
    return jnp.concatenate([jnp.full((BLK, 1), s_ref[0, GRP * hk + g], F32) for g in range(GRP)], axis=0)


def attn_fwd(q, k, v, sinks, t, job=None):
    def body(s_ref, q_ref, kp_ref, kc_ref, vp_ref, vc_ref, o_ref):
        n = pl.program_id(0)
        mask_p, mask_c = _masks(n)
        for hk in range(N_KV):
            kv = pl.ds(HD * hk, HD)
            pp, pc, _ = _scores(_stack_heads(q_ref, hk), kp_ref[:, kv], kc_ref[:, kv], mask_p, mask_c,
                                _stack_sinks(s_ref, hk))
            o = (lax.dot_general(pp.astype(BF16), vp_ref[:, kv], NN, preferred_element_type=F32)
                 + lax.dot_general(pc.astype(BF16), vc_ref[:, kv], NN, preferred_element_type=F32))
            for g in range(GRP):
                o_ref[:, _head(hk, g)] = o[g * BLK:(g + 1) * BLK].astype(o_ref.dtype)

    prev = lambda n: (jnp.maximum(n - 1, 0), 0)
    cur = lambda n: (n, 0)
    res, moved = carried_call(
        body, "attn_fwd", (t // BLK,),
        [pl.BlockSpec(memory_space=pltpu.SMEM), pl.BlockSpec((BLK, QW), cur),
         pl.BlockSpec((BLK, KVW), prev), pl.BlockSpec((BLK, KVW), cur),
         pl.BlockSpec((BLK, KVW), prev), pl.BlockSpec((BLK, KVW), cur)],
        [pl.BlockSpec((BLK, QW), cur)], [sds((t, QW), BF16)], [], [sinks, q, k, k, v, v], {}, job)
    return res[0], moved


def attn_bwd(q, k, v, do, sinks, t, job=None):
    nb = t // BLK

    def body(s_ref, q_ref, do_ref, kp_ref, kc_ref, vp_ref, vc_ref, dq_ref, dk_ref, dv_ref, ds_ref, dkc, dvc):
        n = pl.program_id(0)

        @pl.when(n == 0)
        def _():
            dkc[...] = jnp.zeros_like(dkc)
            dvc[...] = jnp.zeros_like(dvc)
            ds_ref[...] = jnp.zeros_like(ds_ref)

        @pl.when(n < nb)
        def _():
            mask_p, mask_c = _masks(n)
            lane = lax.broadcasted_iota(jnp.int32, (1, 128), 1)
            dsink = jnp.zeros((1, 128), F32)
            for hk in range(N_KV):
                kv = pl.ds(HD * hk, HD)
                kp, kc, vp, vc = kp_ref[:, kv], kc_ref[:, kv], vp_ref[:, kv], vc_ref[:, kv]
                qs, dos = _stack_heads(q_ref, hk), _stack_heads(do_ref, hk)
                pp, pc, ps = _scores(qs, kp, kc, mask_p, mask_c, _stack_sinks(s_ref, hk))
                dpp = lax.dot_general(dos, vp, NT, preferred_element_type=F32)
                dpc = lax.dot_general(dos, vc, NT, preferred_element_type=F32)
                delta = jnp.sum(pp * dpp, axis=-1, keepdims=True) + jnp.sum(pc * dpc, axis=-1, keepdims=True)
                dsp = (pp * (dpp - delta)).astype(BF16)
                dsc = (pc * (dpc - delta)).astype(BF16)
                sd = ps * delta
                dq = (lax.dot_general(dsp, kp, NN, preferred_element_type=F32)
                      + lax.dot_general(dsc, kc, NN, preferred_element_type=F32))
                for g in range(GRP):
                    rows = slice(g * BLK, (g + 1) * BLK)
                    dsink = dsink + jnp.where(lane == GRP * hk + g, -jnp.sum(sd[rows]), 0.0)
                    dq_ref[:, _head(hk, g)] = dq[rows]
                dk_ref[:, kv] = dkc[:, kv] + lax.dot_general(dsp, qs, TN, preferred_element_type=F32)
                dv_ref[:, kv] = dvc[:, kv] + lax.dot_general(pp.astype(BF16), dos, TN, preferred_element_type=F32)
                dkc[:, kv] = lax.dot_general(dsc, qs, TN, preferred_element_type=F32)
                dvc[:, kv] = lax.dot_general(pc.astype(BF16), dos, TN, preferred_element_type=F32)
            ds_ref[...] += dsink

        @pl.when(n == nb)
        def _():
            dk_ref[...] = dkc[...]
            dv_ref[...] = dvc[...]

    cur = lambda n: (jnp.minimum(n, nb - 1), 0)
    prev = lambda n: (jnp.clip(n - 1, 0, nb - 1), 0)
    return carried_call(
        body, "attn_bwd", (nb + 1,),
        [pl.BlockSpec(memory_space=pltpu.SMEM), pl.BlockSpec((BLK, QW), cur), pl.BlockSpec((BLK, QW), cur),
         pl.BlockSpec((BLK, KVW), prev), pl.BlockSpec((BLK, KVW), cur),
         pl.BlockSpec((BLK, KVW), prev), pl.BlockSpec((BLK, KVW), cur)],
        [pl.BlockSpec((BLK, QW), cur), pl.BlockSpec((BLK, KVW), prev), pl.BlockSpec((BLK, KVW), prev),
         pl.BlockSpec((1, 128), lambda n: (0, 0))],
        [sds((t, QW), F32), sds((t, KVW), F32), sds((t, KVW), F32), sds((1, 128), F32)],
        [pltpu.VMEM((BLK, KVW), F32), pltpu.VMEM((BLK, KVW), F32)], [sinks, q, do, k, k, v, v], {}, job)


def _adamw(w, g, m, v):
    m2 = B1 * m + (1.0 - B1) * g
    v2 = B2 * v + (1.0 - B2) * jnp.square(g)
    m_hat = m2 / (1.0 - B1 ** STEP)
    v_hat = v2 / (1.0 - B2 ** STEP)
    return -LR * (m_hat / (jnp.sqrt(v_hat) + EPS) + WD * w), m2, v2


def ada_fwd(c16, w_ada, b_sh):
    tn = 512

    def body(c_ref, w_ref, b_ref, o_ref):
        cv = c_ref[...]
        sc = (cv * _sigmoid(cv)).astype(BF16)
        o_ref[...] = lax.dot_general(sc, w_ref[...].astype(BF16), NN, preferred_element_type=F32) + b_ref[...]

    return pl.pallas_call(
        body, name="ada_fwd", grid=(ADA_SH // tn,),
        in_specs=[pl.BlockSpec((16, D), lambda j: (0, 0)), pl.BlockSpec((D, tn), lambda j: (0, j)),
                  pl.BlockSpec((1, tn), lambda j: (0, j))],
        out_specs=pl.BlockSpec((16, tn), lambda j: (0, j)), out_shape=sds((16, ADA_SH), F32),
        compiler_params=_params(("arbitrary",)),
    )(c16, w_ada, b_sh)


def ada_bwd_adam(c16, gm16, w, m, v, job):
    tm, tn = 512, ADA_SH // 4

    def body(c_ref, g_ref, w_ref, m_ref, v_ref, go_ref, d_ref, mo_ref, vo_ref):
        cv = c_ref[...]
        sc = (cv * _sigmoid(cv)).astype(BF16)
        g = lax.dot_general(sc, g_ref[...].astype(BF16), TN, preferred_element_type=F32)
        dl, m2, v2 = _adamw(w_ref[...], g, m_ref[...], v_ref[...])
        go_ref[...] = g
        d_ref[...] = dl
        mo_ref[...] = m2
        vo_ref[...] = v2

    blk = pl.BlockSpec((tm, tn), lambda i, j: (i, j))
    return carried_call(
        body, "ada_bwd_adam", (D // tm, ADA_SH // tn),
        [pl.BlockSpec((16, tm), lambda i, j: (0, i)), pl.BlockSpec((16, tn), lambda i, j: (0, j)), blk, blk, blk],
        [blk] * 4, [sds((D, ADA_SH), F32)] * 4, [], [c16, gm16, w, m, v], {}, job)


def adam_rows(name, w, g, m, v, tm):
    rows, cols = w.shape

    def fn(wv, gv, mv, vv):
        gv = gv[:, :cols]
        dl, m2, v2 = _adamw(wv, gv, mv, vv)
        return gv, dl, m2, v2

    return rowmap(name, fn, [T_(w), T_(g), T_(m), T_(v)], [(cols, F32)] * 4, rows=rows, tm=tm)


def adam_small(name, w, g, m, v):
    def body(w_ref, g_ref, m_ref, v_ref, d_ref, mo_ref, vo_ref):
        dl, m2, v2 = _adamw(w_ref[...], g_ref[...], m_ref[...], v_ref[...])
        d_ref[...] = dl
        mo_ref[...] = m2
        vo_ref[...] = v2

    return pl.pallas_call(body, name=name, out_shape=[sds(w.shape, F32)] * 3)(w, g, m, v)


def sum_devices(allv):
    def body(a_ref, o_ref):
        acc = a_ref[0]
        for d in range(1, 8):
            acc = acc + a_ref[d]
        o_ref[...] = acc

    return pl.pallas_call(body, name="sum_devices", out_shape=sds(allv.shape[1:], F32))(allv)


def _ln_fwd(z, g, b):
    mu = jnp.mean(z, axis=-1, keepdims=True)
    zc = z - mu
    var = jnp.mean(jnp.square(zc), axis=-1, keepdims=True)
    return zc * lax.rsqrt(var + LN_EPS) * g + b


def _ln_bwd(z, g, dout):
    mu = jnp.mean(z, axis=-1, keepdims=True)
    zc = z - mu
    var = jnp.mean(jnp.square(zc), axis=-1, keepdims=True)
    rstd = lax.rsqrt(var + LN_EPS)
    xh = zc * rstd
    dxh = dout * g
    dz = rstd * (dxh - jnp.mean(dxh, axis=-1, keepdims=True) - xh * jnp.mean(dxh * xh, axis=-1, keepdims=True))
    return dz, colsum(dout * xh), colsum(dout)


def modulate(name, xin, shift, scale, t):
    return rowmap(name, lambda xv, sh, sc: xv * (1.0 + sc) + sh, [T_(xin), B_(shift), B_(scale)], [(D, BF16)],
                  rows=t, tm=512)


def residual_ln_mod(name, xin, y, gate, lg, lb, wgt, shift_n, scale_n, t):
    def fn(xv, yv, gt, g, b, sh, sc):
        z = ALPHA * xv + (wgt * (1.0 + gt)) * yv
        xo = _ln_fwd(z, g, b)
        return xo, z, xo * (1.0 + sc) + sh

    return rowmap(name, fn, [T_(xin), T_(y), B_(gate), B_(lg), B_(lb), B_(shift_n), B_(scale_n)],
                  [(D, F32), (D, F32), (D, BF16)], rows=t, tm=512)


def residual_ln_bwd(name, z, dnext, y, gate, lg, wgt, t):
    dzn, dun, xn, scn = dnext

    def fn(zv, yv, gt, g, dzv, duv, xv, sc):
        dv = ALPHA * dzv + duv * (1.0 + sc)
        dz, dg, db = _ln_bwd(zv, g, dv)
        return dz, (wgt * (1.0 + gt)) * dz, dg, db, colsum(wgt * dz * yv), colsum(duv), colsum(duv * xv)

    return rowmap(name, fn, [T_(z), T_(y), B_(gate), B_(lg), T_(dzn), T_(dun), T_(xn), B_(scn)],
                  [(D, F32), (D, BF16)], [(1, D)] * 5, rows=t, tm=256)


def residual_ln_loss_bwd(name, xin, y, tgt, gate, lg, lb, wgt, t):
    def fn(xv, yv, tv, gt, g, b):
        z = ALPHA * xv + (wgt * (1.0 + gt)) * yv
        d = _ln_fwd(z, g, b) - tv
        dz, dg, db = _ln_bwd(z, g, d * (1.0 / D))
        return dz, (wgt * (1.0 + gt)) * dz, dg, db, colsum(wgt * dz * yv), jnp.sum(d * d).reshape(1, 1)

    dz, dy, dlg, dlb, dgate, sq = rowmap(
        name, fn, [T_(xin), T_(y), T_(tgt), B_(gate), B_(lg), B_(lb)], [(D, F32), (D, BF16)],
        [(1, D), (1, D), (1, D), (1, 1)], rows=t, tm=512)
    return dz, dy, dlg, dlb, dgate, sq


def modulate_bwd(name, dz, du, xin, scale, t):
    def fn(dzv, duv, xv, sc):
        return ALPHA * dzv + duv * (1.0 + sc), colsum(duv), colsum(duv * xv)

    return rowmap(name, fn, [T_(dz), T_(du), T_(xin), B_(scale)], [(D, F32)], [(1, D), (1, D)], rows=t, tm=256)


def ffn_fwd(tag, u, wi, t, up_job, down_job=None):
    tm = min(1024, t)
    tn = 256
    per = FHP // tn

    def act(accs, _):
        a, b = accs
        s = _sigmoid(a)
        sl = a * s
        return b * (s * (1.0 + a * (1.0 - s))), sl, sl * b

    tmu = min(2048, t)
    hblk = pl.BlockSpec((tmu, tn), lambda i, j, k: (i, j))
    (ha, hb, g), up_moved = mm(
        tag + "_up", [u], [wi, wi], [(0, 0, 0), (0, 1, 1)], dims=NT, grid=(t // tmu, 2 * per, 1),
        a_specs=[pl.BlockSpec((tmu, D), lambda i, j, k: (i, 0))],
        b_specs=[pl.BlockSpec((None, tn, D), lambda i, j, k: (j // per, j % per, 0)),
                 pl.BlockSpec((None, tn, D), lambda i, j, k: (2 + j // per, j % per, 0))],
        outs=[sds((t, 2 * FHP), BF16)] * 3, out_specs=[hblk] * 3, acc_shapes=[(tmu, tn)] * 2, epilogue=act, job=up_job,
        sub_rows=tmu // 2)
    wo = up_moved[0].reshape(2 * FHP, D)
    tk = FHP
    y, down_moved = _with_moved(mm(
        tag + "_down", [g], [wo], [(0, 0, 0)], dims=NN, grid=(t // tm, 2, 2),
        a_specs=[pl.BlockSpec((tm, tk), lambda i, j, k: (i, k))],
        b_specs=[pl.BlockSpec((tk, D // 2), lambda i, j, k: (k, j))],
        outs=[sds((t, D), F32)], out_specs=[pl.BlockSpec((tm, D // 2), lambda i, j, k: (i, j))],
        acc_shapes=[(tm, D // 2)], job=down_job), down_job)
    return ha, hb, g, y, wo, up_moved, down_moved


def ffn_bwd(tag, u, ha, hb, g, dy, wi, wo, t, sp, dact_job=None, dwo_job=None, du_extra=None, defer=False):
    tm = min(1024, t)

    def dact(accs, ex):
        dg = accs[0]
        return dg * ex[0].astype(F32), dg * ex[1].astype(F32)

    tn = 256
    tmu = min(2048, t)
    hblk = pl.BlockSpec((tmu, tn), lambda i, j, k: (i, j))
    (dha, dhb), dact_moved = _with_moved(mm(
        tag + "_dact", [dy], [wo], [(0, 0, 0)], dims=NT, grid=(t // tmu, 2 * FHP // tn, 1),
        a_specs=[pl.BlockSpec((tmu, D), lambda i, j, k: (i, 0))],
        b_specs=[pl.BlockSpec((tn, D), lambda i, j, k: (j, 0))],
        outs=[sds((t, 2 * FHP), BF16)] * 2, out_specs=[hblk] * 2, acc_shapes=[(tmu, tn)],
        epilogue=dact, extras=[ha, hb], extra_specs=[hblk] * 2, job=dact_job, sub_rows=tmu // 2), dact_job)
    tk = min(2048, t)
    th = FHP // 2
    dwo, dwo_moved = _with_moved(mm(
        tag + "_dwo", [g], [dy], [(0, 0, 0)], dims=TN, grid=(4, 2, t // tk),
        a_specs=[pl.BlockSpec((tk, th), lambda i, j, k: (k, i))],
        b_specs=[pl.BlockSpec((tk, D // 2), lambda i, j, k: (k, j))],
        outs=[sds((2 * FHP, D), BF16)], out_specs=[pl.BlockSpec((th, D // 2), lambda i, j, k: (i, j))],
        acc_shapes=[(th, D // 2)], job=dwo_job), dwo_job)
    dwo = dwo.reshape(2, FHP, D)

    def dwi_part(part, dh, carry, job):
        return mm(
            f"{tag}_dwi{part}", [dh], [u], [(0, 0, 0)], dims=TN, grid=(4, 2, t // tk),
            a_specs=[pl.BlockSpec((tk, th), lambda i, j, k: (k, i))],
            b_specs=[pl.BlockSpec((tk, D // 2), lambda i, j, k: (k, j))],
            outs=[sds((4, FHP, D), BF16)],
            out_specs=[pl.BlockSpec((None, th, D // 2), lambda i, j, k: (2 * part + i // 2, i % 2, j))],
            acc_shapes=[(th, D // 2)], carry=carry, job=job)

    dwi, (sib_fo,) = dwi_part(0, dha, None, reduce_sibling_job([(dwo, view_ffn_out, FO, D)]))
    q_fo = chip_sum(tag + "_chipsum_fo", dwo, sib_fo, sp, FO, FO // 2, ffn_out=True)
    dwi, (far_fo,) = dwi_part(1, dhb, dwi, reduce_chips_job([q_fo]))
    tmd = min(512, t)
    if defer:
        q_fi = dwi
        job = reduce_sibling_job([(dwi, view_lead, FHP, D)])
    else:
        (sib_fi,) = run_job(tag + "_sibling_fi", reduce_sibling_job([(dwi, view_lead, FHP, D)]))
        q_fi = chip_sum(tag + "_chipsum_fi", dwi, sib_fi, sp, FHP, FHP // 8)
        job = reduce_chips_job([q_fi])
    if du_extra is not None:
        job = merge_jobs(job, du_extra(dact_moved, dwo_moved, q_fo, far_fo))
    du, (far_fi, *extra_moved) = mm(
        tag + "_du", [dha, dhb], [wi, wi], [(0, 0, 0), (1, 1, 0)], dims=NN, grid=(t // tmd, 2, 2),
        a_specs=[pl.BlockSpec((tmd, FHP), lambda i, j, k: (i, k))] * 2,
        b_specs=[pl.BlockSpec((None, FHP, D // 2), lambda i, j, k: (k, 0, j)),
                 pl.BlockSpec((None, FHP, D // 2), lambda i, j, k: (2 + k, 0, j))],
        outs=[sds((t, D), F32)], out_specs=[pl.BlockSpec((tmd, D // 2), lambda i, j, k: (i, j))],
        acc_shapes=[(tmd, D // 2)], job=job)
    return du, (q_fi, far_fi), (q_fo, far_fo), extra_moved


def mix_fwd(u, wts, b_in, pool_scale, sinks, tabs, t, in_job, attn_job):
    w_in, wp, wba, wbb, wo = wts
    tm = min(1024, t)
    tmh = min(512, t)
    h, in_moved = mm("mix_in", [u], [w_in], [(0, 0, 0)], dims=NN, grid=(t // tmh, 4, 1),
                     a_specs=[pl.BlockSpec((tmh, D), lambda i, j, k: (i, 0))],
                     b_specs=[pl.BlockSpec((None, D, IN_SH), lambda i, j, k: (j, 0, 0))],
                     outs=[sds((t, IN_W), F32)], out_specs=[pl.BlockSpec((tmh, IN_SH), lambda i, j, k: (i, j))],
                     acc_shapes=[(tmh, IN_SH)], job=in_job)
    attn_job = attn_job(in_moved)
    pooled = pool_fwd(h, b_in, t, 512)
    gblk = pl.BlockSpec((tm, PG), lambda i, j, k: (i, j))
    mixed = mm("mix_pool", [pooled], [wp], [(0, 0, 0)], dims=NN, grid=(t // tm, 4, 1), a_specs=[gblk],
               b_specs=[pl.BlockSpec((None, PG, PG), lambda i, j, k: (j, 0, 0))],
               outs=[sds((t, PW), F32)], out_specs=[gblk], acc_shapes=[(tm, PG)])
    pm = rowmap("mix_pscale", lambda mv, ps: mv * ps, [T_(mixed), B_(pool_scale)], [(PW, BF16)], rows=t, tm=512)

    def branch(name, a, w):
        return mm(name, [a], [w], [(0, 0, 0)], dims=NN, grid=(t // tm, 4, 1),
                  a_specs=[pl.BlockSpec((tm, PW), lambda i, j, k: (i, 0))],
                  b_specs=[pl.BlockSpec((None, PW, D // 4), lambda i, j, k: (j, 0, 0))],
                  outs=[sds((t, D), F32)], out_specs=[pl.BlockSpec((tm, D // 4), lambda i, j, k: (i, j))],
                  acc_shapes=[(tm, D // 4)])

    ya = branch("mix_branch_a", pm, wba)

    def qkv(hq, hk, hv, bq, bk, bv, cc, sa, sb):
        return (_rope(hq + bq, cc, sa, sb) * (HD ** -0.5), _rope(hk + bk, cc, sa, sb), hv + bv)

    qr, kr, vv = rowmap(
        "mix_rope", qkv,
        [T_(h, QW, 1), T_(h, KVW, 8), T_(h, KVW, 9), B_(b_in, QW, 1), B_(b_in, KVW, 8), B_(b_in, KVW, 9),
         T_(tabs[0]), T_(tabs[1]), T_(tabs[2])],
        [(QW, BF16), (KVW, BF16), (KVW, BF16)], rows=t, tm=512)
    attn, attn_moved = attn_fwd(qr, kr, vv, sinks, t, attn_job)
    yb = branch("mix_branch_b", attn, wbb)
    cw = 512

    def merge(ga, gb, ba, bb, yav, ybv):
        return _sigmoid(ga + ba) * yav + _sigmoid(gb + bb) * ybv

    merged = rowmap(
        "mix_merge", merge,
        [T_(h, cw, 5), T_(h, cw, 9), B_(b_in, cw, 5), B_(b_in, cw, 9), T_(ya, cw), T_(yb, cw)],
        [(D, BF16)], rows=t, tm=1024, ncol=D // cw)
    y = mm("mix_out", [merged], [wo], [(0, 0, 0)], dims=NN, grid=(t // tm, 2, 1),
           a_specs=[pl.BlockSpec((tm, D), lambda i, j, k: (i, 0))],
           b_specs=[pl.BlockSpec((D, D // 2), lambda i, j, k: (0, j))],
           outs=[sds((t, D), F32)], out_specs=[pl.BlockSpec((tm, D // 2), lambda i, j, k: (i, j))],
           acc_shapes=[(tm, D // 2)])
    return y, (h, pooled, mixed, pm, ya, qr, kr, vv, attn, yb, merged), attn_moved


def mix_bwd(u, saved, dy, wts, b_in, pool_scale, sinks, tabs, t, attn_job=None):
    h, pooled, mixed, pm, ya, qr, kr, vv, attn, yb, merged = saved
    w_in, wp, wba, wbb, wo = wts
    tm = min(1024, t)
    tk = min(2048, t)
    dmerged = mm("mix_dmerged", [dy], [wo], [(0, 0, 0)], dims=NT, grid=(t // tm, 2, 1),
                 a_specs=[pl.BlockSpec((tm, D), lambda i, j, k: (i, 0))],
                 b_specs=[pl.BlockSpec((D // 2, D), lambda i, j, k: (j, 0))],
                 outs=[sds((t, D), F32)], out_specs=[pl.BlockSpec((tm, D // 2), lambda i, j, k: (i, j))],
                 acc_shapes=[(tm, D // 2)])
    half = pl.BlockSpec((tk, D // 2), lambda i, j, k: (k, i))
    dwo = mm("mix_dwo", [merged], [dy], [(0, 0, 0)], dims=TN, grid=(2, 2, t // tk), a_specs=[half],
             b_specs=[pl.BlockSpec((tk, D // 2), lambda i, j, k: (k, j))],
             outs=[sds((D, D), BF16)], out_specs=[pl.BlockSpec((D // 2, D // 2), lambda i, j, k: (i, j))],
             acc_shapes=[(D // 2, D // 2)])
    cw = 512

    def dmerge(dm, ga, gb, ba, bb, yav, ybv):
        sa_, sb_ = _sigmoid(ga + ba), _sigmoid(gb + bb)
        dga = dm * yav * sa_ * (1.0 - sa_)
        dgb = dm * ybv * sb_ * (1.0 - sb_)
        return dm * sa_, dm * sb_, dga, dgb, colsum(dga), colsum(dgb)

    dya, dyb, dgla, dglb, dbga, dbgb = rowmap(
        "mix_dmerge", dmerge,
        [T_(dmerged, cw), T_(h, cw, 5), T_(h, cw, 9), B_(b_in, cw, 5), B_(b_in, cw, 9), T_(ya, cw), T_(yb, cw)],
        [(D, BF16)] * 4, [(1, D), (1, D)], rows=t, tm=1024, ncol=D // cw)

    def dbranch(name, dyv, act, w):
        dwb = mm(name + "_dw", [act], [dyv], [(0, 0, 0)], dims=TN, grid=(1, 4, t // tk),
                 a_specs=[pl.BlockSpec((tk, PW), lambda i, j, k: (k, 0))],
                 b_specs=[pl.BlockSpec((tk, D // 4), lambda i, j, k: (k, j))],
                 outs=[sds((4, PW, D // 4), BF16)], out_specs=[pl.BlockSpec((None, PW, D // 4), lambda i, j, k: (j, 0, 0))],
                 acc_shapes=[(PW, D // 4)])
        return dwb, lambda dt: mm(
            name + "_dx", [dyv], [w], [(0, 0, 0)], dims=NT, grid=(t // tm, 1, 4),
            a_specs=[pl.BlockSpec((tm, D // 4), lambda i, j, k: (i, k))],
            b_specs=[pl.BlockSpec((None, PW, D // 4), lambda i, j, k: (k, 0, 0))],
            outs=[sds((t, PW), dt)], out_specs=[pl.BlockSpec((tm, PW), lambda i, j, k: (i, 0))], acc_shapes=[(tm, PW)])

    dwba, dpm_fn = dbranch("mix_dbranch_a", dya, pm, wba)
    dwbb, dattn_fn = dbranch("mix_dbranch_b", dyb, attn, wbb)
    dpm, dattn = dpm_fn(F32), dattn_fn(BF16)
    dmixed, dps = rowmap("mix_dpscale", lambda dp, mv, ps: (dp * ps, colsum(dp * mv)),
                         [T_(dpm), T_(mixed), B_(pool_scale)], [(PW, BF16)], [(1, PW)], rows=t, tm=512)
    gblk = pl.BlockSpec((tm, PG), lambda i, j, k: (i, j))
    dpooled = mm("mix_dpool", [dmixed], [wp], [(0, 0, 0)], dims=NT, grid=(t // tm, 4, 1), a_specs=[gblk],
                 b_specs=[pl.BlockSpec((None, PG, PG), lambda i, j, k: (j, 0, 0))],
                 outs=[sds((t, PW), F32)], out_specs=[gblk], acc_shapes=[(tm, PG)])
    kblk = pl.BlockSpec((tk, PG), lambda i, j, k: (k, i))
    dwp = mm("mix_dwpool", [pooled], [dmixed], [(0, 0, 0)], dims=TN, grid=(4, 1, t // tk), a_specs=[kblk], b_specs=[kblk],
             outs=[sds((4, PG, PG), BF16)], out_specs=[pl.BlockSpec((None, PG, PG), lambda i, j, k: (i, 0, 0))],
             acc_shapes=[(PG, PG)])
    dxp, dbxp = pool_bwd(dpooled, t, 512)
    (dqr, dkr, dvv, dsinks), attn_moved = attn_bwd(qr, kr, vv, dattn, sinks, t, attn_job)

    def dqkv(dq, dk, dv, cc, sa, sb):
        dq = _rope_t(dq, cc, sa, sb) * (HD ** -0.5)
        dk = _rope_t(dk, cc, sa, sb)
        return dq, dk, dv, colsum(dq), colsum(dk), colsum(dv)

    dq, dk, dvb, dbq, dbk, dbv = rowmap(
        "mix_rope_bwd", dqkv, [T_(dqr), T_(dkr), T_(dvv), T_(tabs[0]), T_(tabs[1]), T_(tabs[2])],
        [(QW, BF16), (KVW, BF16), (KVW, BF16)], [(1, QW), (1, KVW), (1, KVW)], rows=t, tm=512)
    dh = jnp.concatenate([dxp, dq, dk, dvb, dgla, dglb], axis=1)
    db_in = jnp.concatenate([dbxp, dbq, dbk, dbv, dbga, dbgb], axis=1)
    dwin = mm("mix_dwin", [u], [dh], [(0, 0, 0)], dims=TN, grid=(2, 4, t // tk), a_specs=[half],
              b_specs=[pl.BlockSpec((tk, IN_SH), lambda i, j, k: (k, j))],
              outs=[sds((4, D, IN_SH), BF16)], out_specs=[pl.BlockSpec((None, D // 2, IN_SH), lambda i, j, k: (j, i, 0))],
              acc_shapes=[(D // 2, IN_SH)])
    dwp_sh = jnp.transpose(dwp.reshape(4, 4, 64, PG), (1, 0, 2, 3)).reshape(4, 4 * 64, PG)
    parts = {"win": dwin, "wp": dwp_sh, "wba": dwba, "wbb": dwbb, "wo": dwo.reshape(4, D // 4, D)}
    du, sib = mm("mix_du", [dh], [w_in], [(0, 0, 0)], dims=NT, grid=(t // tm, 2, 4),
                 a_specs=[pl.BlockSpec((tm, IN_SH), lambda i, j, k: (i, k))],
                 b_specs=[pl.BlockSpec((None, D // 2, IN_SH), lambda i, j, k: (k, j, 0))],
                 outs=[sds((t, D), F32)], out_specs=[pl.BlockSpec((tm, D // 2), lambda i, j, k: (i, j))],
                 acc_shapes=[(tm, D // 2)],
                 job=reduce_sibling_job([(p, view_lead, p.shape[1], p.shape[2]) for p in parts.values()]))
    return du, parts, dict(zip(parts, sib)), db_in, dps, dsinks, attn_moved


def cast_shard(name, w, sp, ffn_out=False):
    rows, cols = w.shape
    if ffn_out:
        tm = rows // 2
        shape = (2, FHP, D)
        spec = pl.BlockSpec((None, tm, cols), lambda j, i, s: (s[0] // 2, (s[0] % 2) * 2 + i, 0))
    else:
        tm = rows // 4
        shape = (4, rows, cols)
        spec = pl.BlockSpec((None, tm, cols), lambda j, i, s: (s[0], i, 0))
    return rowmap(name, lambda wv: wv, [T_(w)], [(shape, BF16, spec)], rows=rows, tm=tm, sp=sp)


def cast_ffn_in(name, wt, sp):
    tm = FH // 4
    buf = rowmap(name, lambda wv: wv, [T_(wt)],
                 [((4, FHP, D), BF16, pl.BlockSpec((None, tm, D), lambda j, i, s: (s[0], i, 0)))], rows=FH, tm=tm, sp=sp)
    pad = FHP - FH

    def zero_pad(_, __, out):
        out[...] = jnp.zeros_like(out)

    return pl.pallas_call(
        zero_pad, name=name + "_pad", out_shape=sds(buf.shape, BF16), input_output_aliases={1: 0},
        grid_spec=pltpu.PrefetchScalarGridSpec(
            num_scalar_prefetch=1, grid=(1,), in_specs=[ANY],
            out_specs=pl.BlockSpec((None, pad, D), lambda i, s: (s[0], FH // pad, 0))))(sp, buf)


def chip_sum(name, dw, got, sp, rows, tm, ffn_out=False):
    hr, cols = rows // 2, got.shape[2]
    per = hr // tm
    pos = pl.BlockSpec((None, tm, cols), lambda j, i, s: (i // per, i % per, 0))
    if ffn_out:
        mine = pl.BlockSpec((None, tm, cols), lambda j, i, s: (i // 2, (i % 2) * 2 + s[1], 0))
    else:
        mine = pl.BlockSpec((None, tm, cols), lambda j, i, s: (i // per, s[1] * per + i % per, 0))
    return rowmap(name, lambda av, bv: av.astype(F32) + bv.astype(F32), [X_(dw, mine), X_(got, pos)],
                  [(got.shape, BF16, pos)], rows=4 * hr, tm=tm, sp=sp)


def chip_total(name, q, got, sp, rows, tm):
    hr, cols = rows // 2, q.shape[2]
    per = hr // tm

    def part(f):
        return X_(got, pl.BlockSpec((None, tm, cols), lambda j, i, s, f=f: (f, i, 0)))

    return rowmap(
        name, lambda av, b0, b1, b2: ((av.astype(F32) + b0.astype(F32)) + b1.astype(F32)) + b2.astype(F32),
        [X_(q, pl.BlockSpec((None, tm, cols), lambda j, i, s: (s[0], i, 0))), part(0), part(1), part(2)],
        [((rows, cols), F32, pl.BlockSpec((tm, cols), lambda j, i, s: (s[1] * per + i, 0)))], rows=hr, tm=tm, sp=sp)


def kernel(x, c, w_ada, b_ada, ln_g, ln_b, w_ffn1_in, w_ffn1_out, w_in, b_in, w_pool, pool_scale, sinks, w_branch_a, w_branch_b, w_out, w_ffn2_in, w_ffn2_out, loss_target, m_w_ada, m_b_ada, m_ln_g, m_ln_b, m_w_ffn1_in, m_w_ffn1_out, m_w_in, m_b_in, m_w_pool, m_pool_scale, m_sinks, m_w_branch_a, m_w_branch_b, m_w_out, m_w_ffn2_in, m_w_ffn2_out, v_w_ada, v_b_ada, v_ln_g, v_ln_b, v_w_ffn1_in, v_w_ffn1_out, v_w_in, v_b_in, v_w_pool, v_pool_scale, v_sinks, v_w_branch_a, v_w_branch_b, v_w_out, v_w_ffn2_in, v_w_ffn2_out):
    t = x.shape[1]
    xs, tgt = x[0], loss_target[0]
    xi, yi, ci = lax.axis_index("x"), lax.axis_index("y"), lax.axis_index("c")
    chip = 2 * xi + yi
    dev = 2 * chip + ci
    b_in2, ps2, sinks2 = b_in, pool_scale, sinks

    sp = jnp.stack([chip, ci]).astype(jnp.int32)
    tr = lambda a: jnp.swapaxes(a[0], 0, 1)

    first = jnp.concatenate([c.reshape(-1), ln_g.reshape(-1), ln_b.reshape(-1)]).reshape(-1, 128)
    first_all = allgather_small("gather_cond", first).reshape(8, -1)
    c_all = first_all[:, :D]
    ln_parts = first_all[0::2, D:].reshape(4, 2, 3, D // 4)
    ln_full = jnp.transpose(ln_parts, (1, 2, 0, 3)).reshape(2, 3, D)
    lgs = [ln_full[0, s:s + 1] for s in range(3)]
    lbs = [ln_full[1, s:s + 1] for s in range(3)]
    c16 = jnp.pad(c_all, ((0, 8), (0, 0)))
    b_ada_sh = lax.dynamic_slice(b_ada, (0, chip * ADA_SH), (1, ADA_SH))
    mod_part = ada_fwd(c16, w_ada[0], b_ada_sh)[:8]
    mod_all = allgather_small("gather_mod", mod_part.reshape(-1, 128)).reshape(8, 8, ADA_SH)
    mod_mine = lax.dynamic_index_in_dim(mod_all[0::2], dev, axis=1, keepdims=False).reshape(9, D)
    mods = [[mod_mine[3 * s + k:3 * s + k + 1] for k in range(3)] for s in range(3)]

    f1i_buf, f1i_send, f1i_recv, _ = gather_start(
        "gather_f1i_start", cast_ffn_in("cast_f1i", tr(w_ffn1_in), sp), FHP, [mod_mine])
    plain = [("f1o", w_ffn1_out[0]), ("win", w_in[0]), ("wp", w_pool[0].reshape(4 * 64, PG)), ("wba", w_branch_a[0]),
             ("wbb", w_branch_b[0]), ("wo", w_out[0]), ("f2o", w_ffn2_out[0])]
    sh = {n: cast_shard("cast_" + n, w, sp, ffn_out=n in ("f1o", "f2o")) for n, w in plain}
    sh["f1i"] = f1i_buf
    sh["f2i"] = cast_ffn_in("cast_f2i", tr(w_ffn2_in), sp)
    order = ["f1i", "f1o", "win", "wp", "wba", "wbb", "wo", "f2i", "f2o"]
    views = {n: (view_ffn_out if n in ("f1o", "f2o") else view_lead) for n in order}
    shard_rows = {n: (FO if n in ("f1o", "f2o") else sh[n].shape[1]) for n in order}
    shard_cols = {n: sh[n].shape[2] for n in order}
    tiles = {"f1i": FHP // 8, "f1o": FO // 2, "win": 512, "wp": 128, "wba": 512, "wbb": 512, "wo": 256,
             "f2i": FHP // 8, "f2o": FO // 2}

    def item(n, part=0, parts=1):
        return (sh[n], views[n], shard_rows[n], part, parts)

    tabs = rope_tables(t)
    (sh0, sc0, gt0), (sh1, sc1, gt1), (sh2, sc2, gt2) = mods

    u0 = modulate("ffn1_mod", xs, sh0, sc0, t)
    landed = gather_wait("gather_f1i_wait", f1i_buf, f1i_send, f1i_recv, FHP,
                         [u0] + [sh[n] for n in order if n != "f1i"])
    (g_f1i,) = run_job("gather_f1i_forward", forward_job(landed, FHP))
    ha1, hb1, g1, y1, f1o, (_, g_win), (g_wp, g_wba, g_wbb, g_wo) = ffn_fwd(
        "ffn1", u0, g_f1i, t, gather_job([item("f1o"), item("win")]),
        gather_job([item(n) for n in ("wp", "wba", "wbb", "wo")]))
    x1, z1, u1 = residual_ln_mod("ffn1_ln", xs, y1, gt0, lgs[0], lbs[0], 0.5, sh1, sc1, t)
    wp_full = jnp.transpose(g_wp.reshape(4, 4, 64, PG), (1, 0, 2, 3)).reshape(4, PG, PG)
    wts = (g_win, wp_full, g_wba, g_wbb, g_wo.reshape(D, D))
    y2, sv2, (g_f2i,) = mix_fwd(
        u1, wts, b_in2, ps2, sinks2, tabs, t, gather_job([item("f2i", 0, 2)]),
        lambda moved: gather_job([(moved[0], view_lead, FHP, 1, 2)]))
    x2, z2, u2 = residual_ln_mod("mix_ln", x1, y2, gt1, lgs[1], lbs[1], 1.0, sh2, sc2, t)
    ha3, hb3, g3, y3, f2o, _, _ = ffn_fwd("ffn2", u2, g_f2i, t, gather_job([item("f2o")]))

    dz3, dy3, dlg2, dlb2, dgt2, sq = residual_ln_loss_bwd("ffn2_ln_loss", x2, y3, tgt, gt2, lgs[2], lbs[2], 0.5, t)
    loss = lax.psum(0.5 * sq[0, 0] / D, ("x", "y", "c"))
    du3, (dw_f2i, sib_f2i), red_f2o, _ = ffn_bwd("ffn2", u2, ha3, hb3, g3, dy3, g_f2i, f2o, t, sp, defer=True)
    q_f2i = chip_sum("ffn2_chipsum_fi", dw_f2i, sib_f2i, sp, FHP, FHP // 8)
    dz2, dy2, dlg1, dlb1, dgt1, dsh2, dsc2 = residual_ln_bwd("mix_ln_bwd", z2, (dz3, du3, x2, sc2), y2, gt1, lgs[1], 1.0, t)
    du2, mix_parts, sib, db_in, dps, dsinks, (far_f2i,) = mix_bwd(
        u1, sv2, dy2, wts, b_in2, ps2, sinks2, tabs, t, reduce_chips_job([q_f2i]))
    red_f2i = (q_f2i, far_f2i)
    q = {n: chip_sum("chipsum_" + n, mix_parts[n], sib[n], sp, shard_rows[n], tiles[n]) for n in mix_parts}
    dz1, dy1, dlg0, dlb0, dgt0, dsh1, dsc1 = residual_ln_bwd("ffn1_ln_bwd", z1, (dz2, du2, x1, sc1), y1, gt0, lgs[0], 0.5, t)
    early = [n for n in order if n != "f1i"]

    def total(n, q_n, far_n):
        return chip_total("total_" + n, q_n, far_n, sp, shard_rows[n], tiles[n])

    def swap_early(far_a, far_b, q_f1o, far_f1o):
        reduced = {"f1o": (q_f1o, far_f1o), "f2i": red_f2i, "f2o": red_f2o, "win": (q["win"], far_a[0]),
                   "wp": (q["wp"], far_a[1]), "wo": (q["wo"], far_b[0]), "wba": (q["wba"], far_b[1]),
                   "wbb": (q["wbb"], far_b[2])}
        return share_halves_job([total(n, *reduced[n]) for n in early])

    du1, red_f1i, _, shared_early = ffn_bwd(
        "ffn1", u0, ha1, hb1, g1, dy1, g_f1i, f1o, t, sp,
        reduce_chips_job([q["win"], q["wp"]]), reduce_chips_job([q["wo"], q["wba"], q["wbb"]]), swap_early)
    dx0, dsh0, dsc0 = modulate_bwd("ffn1_mod_bwd", dz1, du1, xs, sc0, t)
    gm0, gm1, gm2 = (dsh0, dsc0, dgt0), (dsh1, dsc1, dgt1), (dsh2, dsc2, dgt2)
    gw = dict(zip(early, shared_early))
    (gw["f1i"],) = run_job("share_f1i", share_halves_job([total("f1i", *red_f1i)]))

    small = jnp.concatenate([*gm0, *gm1, *gm2, dlg0, dlg1, dlg2, dlb0, dlb1, dlb2, db_in, dps, dsinks], axis=1)
    n_small = small.shape[1]
    rows_small = -(-n_small // 1024) * 8
    small = jnp.pad(small, ((0, 0), (0, rows_small * 128 - n_small))).reshape(rows_small, 128)
    small_all = allgather_small("gather_small", small)
    tot = sum_devices(small_all).reshape(1, -1)
    gmod_all = small_all.reshape(8, -1)[:, :9 * D]
    o = 9 * D
    g_b_ada = tot[:, :o]
    g_ln_g = lax.dynamic_slice(tot[:, o:o + 3 * D].reshape(3, D), (0, chip * (D // 4)), (3, D // 4))
    g_ln_b = lax.dynamic_slice(tot[:, o + 3 * D:o + 6 * D].reshape(3, D), (0, chip * (D // 4)), (3, D // 4))
    o += 6 * D
    g_b_in, g_ps, g_sinks = tot[:, o:o + IN_W], tot[:, o + IN_W:o + IN_W + PW], tot[:, o + IN_W + PW:o + IN_W + PW + N_Q]

    gm16 = jnp.pad(lax.dynamic_slice(gmod_all, (0, chip * ADA_SH), (8, ADA_SH)), ((0, 8), (0, 0)))
    (g_w_ada, d_w_ada, nm_w_ada, nv_w_ada), _ = ada_bwd_adam(c16, gm16, w_ada[0], m_w_ada[0], v_w_ada[0], None)

    def big(n, w, m, v, tm):
        shape = w.shape
        w2, m2, v2 = (a.reshape(shape[-2] if a.ndim == 3 else -1, shape[-1]) for a in (w, m, v))
        return [r.reshape(shape) for r in adam_rows("adam_" + n, w2, gw[n], m2, v2, tm)]

    def big_t(n, w, m, v):
        return [jnp.swapaxes(r, 0, 1)[None] for r in adam_rows("adam_" + n, tr(w), gw[n], tr(m), tr(v), FH // 8)]

    def tiny(n, w, g, m, v):
        return [g.reshape(w.shape)] + list(adam_small("adam_" + n, w, g.reshape(w.shape), m, v))

    res = {
        "w_ada": [a[None] for a in (g_w_ada, d_w_ada, nm_w_ada, nv_w_ada)],
        "b_ada": tiny("b_ada", b_ada, g_b_ada, m_b_ada, v_b_ada),
        "ln_g": tiny("ln_g", ln_g, g_ln_g, m_ln_g, v_ln_g),
        "ln_b": tiny("ln_b", ln_b, g_ln_b, m_ln_b, v_ln_b),
        "w_ffn1_in": big_t("f1i", w_ffn1_in, m_w_ffn1_in, v_w_ffn1_in),
        "w_ffn1_out": big("f1o", w_ffn1_out, m_w_ffn1_out, v_w_ffn1_out, FO // 4),
        "w_in": big("win", w_in, m_w_in, v_w_in, 256),
        "b_in": tiny("b_in", b_in, g_b_in, m_b_in, v_b_in),
        "w_pool": big("wp", w_pool, m_w_pool, v_w_pool, 256),
        "pool_scale": tiny("pool_scale", pool_scale, g_ps, m_pool_scale, v_pool_scale),
        "sinks": tiny("sinks", sinks, g_sinks, m_sinks, v_sinks),
        "w_branch_a": big("wba", w_branch_a, m_w_branch_a, v_w_branch_a, 512),
        "w_branch_b": big("wbb", w_branch_b, m_w_branch_b, v_w_branch_b, 512),
        "w_out": big("wo", w_out, m_w_out, v_w_out, 128),
        "w_ffn2_in": big_t("f2i", w_ffn2_in, m_w_ffn2_in, v_w_ffn2_in),
        "w_ffn2_out": big("f2o", w_ffn2_out, m_w_ffn2_out, v_w_ffn2_out, FO // 4),
    }
    names = ["w_ada", "b_ada", "ln_g", "ln_b", "w_ffn1_in", "w_ffn1_out", "w_in", "b_in", "w_pool", "pool_scale", "sinks",
             "w_branch_a", "w_branch_b", "w_out", "w_ffn2_in", "w_ffn2_out"]
    return (loss, dx0[None], *[res[n][0] for n in names], *[res[n][1] for n in names],
            *[res[n][2] for n in names], *[res[n][3] for n in names])
```
